```python
import jax, jax.numpy as jnp
from jax import lax
import numpy as np

D_MODEL = 2048
BATCH = 16
SEQ = 2048
DEPTH = 1

CHUNK = 64
LEFT_CHUNKS = 8
BAND = (LEFT_CHUNKS + 1) * CHUNK

D_MIX = D_MODEL
D_ATTN = D_MIX // 2
ATTN_HEADS = 16
ATTN_HEAD_DIM = D_ATTN // ATTN_HEADS
REL_CLIP = 128
D_POOL = D_MIX - D_ATTN
POOL_WINDOWS = (2, 4, 8, 16)
N_POOL_GROUPS = len(POOL_WINDOWS)
POOL_GROUP_DIM = D_POOL // N_POOL_GROUPS
D_IN = 3 * D_ATTN + D_POOL

N_MEM = 256
CROSS_HEADS = 4
CROSS_HEAD_DIM = 128
D_CROSS = CROSS_HEADS * CROSS_HEAD_DIM

D_FF = ((8 * D_MODEL // 3) + 255) // 256 * 256
FFN_RES_WEIGHT = 0.5
EPS = 1e-6
NEG_INF = -1e30

kernel_name = "hybrid_chunk_attn_pool_macaron"


def rmsnorm(x, g):
    xf = x.astype(jnp.float32)
    y = xf * lax.rsqrt(jnp.mean(xf * xf, axis=-1, keepdims=True) + EPS)
    return (y * g.astype(jnp.float32)).astype(x.dtype)


def swiglu(x, w_gate, w_up, w_down):
    return (jax.nn.silu(x @ w_gate) * (x @ w_up)) @ w_down


def chunk_rel_attention(q, k, v, rel_table):
    B, S, H, Dh = q.shape
    nc = S // CHUNK
    pad = LEFT_CHUNKS * CHUNK
    kp = jnp.pad(k, ((0, 0), (pad, 0), (0, 0), (0, 0)))
    vp = jnp.pad(v, ((0, 0), (pad, 0), (0, 0), (0, 0)))
    q_band = pad + jnp.arange(CHUNK)
    rel = q_band[:, None] - jnp.arange(BAND)[None, :]
    rel_idx = jnp.clip(rel, -REL_CLIP, REL_CLIP) + REL_CLIP
    bias = rel_table[:, rel_idx].astype(jnp.float32)
    scale = Dh ** -0.5
    qc = q.reshape(B, nc, CHUNK, H, Dh).transpose(1, 0, 2, 3, 4)

    def one_chunk(args):
        c, qb = args
        start = c * CHUNK
        kb = lax.dynamic_slice_in_dim(kp, start, BAND, axis=1)
        vb = lax.dynamic_slice_in_dim(vp, start, BAND, axis=1)
        s = jnp.einsum('bqhd,bkhd->bhqk', qb, kb).astype(jnp.float32) * scale + bias[None]
        valid = (start + jnp.arange(BAND)) >= pad
        s = jnp.where(valid[None, None, None, :], s, NEG_INF)
        p = jax.nn.softmax(s, axis=-1).astype(vb.dtype)
        return jnp.einsum('bhqk,bkhd->bqhd', p, vb)

    out = lax.map(one_chunk, (jnp.arange(nc), qc))
    return out.transpose(1, 0, 2, 3, 4).reshape(B, S, H * Dh)


def multiscale_pool(u, w_pool, pool_scale):
    B, S, _ = u.shape
    uf = u.astype(jnp.float32)
    cs = jnp.concatenate([jnp.zeros((B, 1, D_POOL), jnp.float32), jnp.cumsum(uf, axis=1)], axis=1)
    t = jnp.arange(S)
    diffs = []
    for g, w in enumerate(POOL_WINDOWS):
        lo = jnp.maximum(t + 1 - w, 0)
        count = (t + 1 - lo).astype(jnp.float32)
        csg = cs[..., g * POOL_GROUP_DIM:(g + 1) * POOL_GROUP_DIM]
        mean = (csg[:, 1:] - csg[:, lo]) / count[None, :, None]
        diffs.append(mean - uf[..., g * POOL_GROUP_DIM:(g + 1) * POOL_GROUP_DIM])
    d = jnp.stack(diffs, axis=2).astype(u.dtype)
    y = jnp.einsum('bsgc,gcd->bsgd', d, w_pool).reshape(B, S, D_POOL)
    return y * pool_scale


def memory_cross_attention(h, mem, w_cq, w_ckv, w_co):
    B, S, _ = h.shape
    q = (h @ w_cq).reshape(B, S, CROSS_HEADS, CROSS_HEAD_DIM)
    kv = mem @ w_ckv
    k = kv[..., :D_CROSS].reshape(B, N_MEM, CROSS_HEADS, CROSS_HEAD_DIM)
    v = kv[..., D_CROSS:].reshape(B, N_MEM, CROSS_HEADS, CROSS_HEAD_DIM)
    s = jnp.einsum('bshd,bmhd->bhsm', q, k).astype(jnp.float32) * (CROSS_HEAD_DIM ** -0.5)
    p = jax.nn.softmax(s, axis=-1).astype(v.dtype)
    o = jnp.einsum('bhsm,bmhd->bshd', p, v).reshape(B, S, D_CROSS)
    return o @ w_co


def _fwd_setup_inputs(seed: int = 0) -> dict:
    key = jax.random.key(seed)
    ks = jax.random.split(key, 24)
    f32 = jnp.float32

    def w(k, shape, fan_in):
        return jax.random.normal(k, shape, f32) * (fan_in ** -0.5)

    def gain(k, shape):
        return 1.0 + 0.05 * jax.random.normal(k, shape, f32)

    L = DEPTH
    return {
        "x": jax.random.normal(ks[0], (BATCH, SEQ, D_MODEL), f32),
        "mem": jax.random.normal(ks[1], (BATCH, N_MEM, D_MODEL), f32),
        "ffn1_norm": gain(ks[2], (L, D_MODEL)),
        "ffn1_w_gate": w(ks[3], (L, D_MODEL, D_FF), D_MODEL),
        "ffn1_w_up": w(ks[4], (L, D_MODEL, D_FF), D_MODEL),
        "ffn1_w_down": w(ks[5], (L, D_FF, D_MODEL), D_FF),
        "mix_norm": gain(ks[6], (L, D_MODEL)),
        "w_in": w(ks[7], (L, D_MODEL, D_IN), D_MODEL),
        "rel_bias": 0.5 * jax.random.normal(ks[8], (L, ATTN_HEADS, 2 * REL_CLIP + 1), f32),
        "w_pool": w(ks[9], (L, N_POOL_GROUPS, POOL_GROUP_DIM, POOL_GROUP_DIM), POOL_GROUP_DIM),
        "pool_scale": gain(ks[10], (L, D_POOL)),
        "w_out": w(ks[11], (L, D_MIX, D_MODEL), D_MIX),
        "cross_norm": gain(ks[12], (L, D_MODEL)),
        "mem_norm": gain(ks[13], (L, D_MODEL)),
        "w_cq": w(ks[14], (L, D_MODEL, D_CROSS), D_MODEL),
        "w_ckv": w(ks[15], (L, D_MODEL, 2 * D_CROSS), D_MODEL),
        "w_co": w(ks[16], (L, D_CROSS, D_MODEL), D_CROSS),
        "ffn2_norm": gain(ks[17], (L, D_MODEL)),
        "ffn2_w_gate": w(ks[18], (L, D_MODEL, D_FF), D_MODEL),
        "ffn2_w_up": w(ks[19], (L, D_MODEL, D_FF), D_MODEL),
        "ffn2_w_down": w(ks[20], (L, D_FF, D_MODEL), D_FF),
        "final_norm": gain(ks[21], (D_MODEL,)),
    }


def _fwd_reference(x, mem, ffn1_norm, ffn1_w_gate, ffn1_w_up, ffn1_w_down, mix_norm, w_in,
              rel_bias, w_pool, pool_scale, w_out, cross_norm, mem_norm, w_cq, w_ckv, w_co,
              ffn2_norm, ffn2_w_gate, ffn2_w_up, ffn2_w_down, final_norm):
    B, S, _ = x.shape
    h = x
    for l in range(DEPTH):
        h = h + FFN_RES_WEIGHT * swiglu(rmsnorm(h, ffn1_norm[l]), ffn1_w_gate[l], ffn1_w_up[l], ffn1_w_down[l])
        z = rmsnorm(h, mix_norm[l]) @ w_in[l]
        q = z[..., 0 * D_ATTN:1 * D_ATTN].reshape(B, S, ATTN_HEADS, ATTN_HEAD_DIM)
        k = z[..., 1 * D_ATTN:2 * D_ATTN].reshape(B, S, ATTN_HEADS, ATTN_HEAD_DIM)
        v = z[..., 2 * D_ATTN:3 * D_ATTN].reshape(B, S, ATTN_HEADS, ATTN_HEAD_DIM)
        u = z[..., 3 * D_ATTN:]
        y_attn = chunk_rel_attention(q, k, v, rel_bias[l])
        y_pool = multiscale_pool(u, w_pool[l], pool_scale[l])
        h = h + jnp.concatenate([y_attn, y_pool], axis=-1) @ w_out[l]
        h = h + memory_cross_attention(rmsnorm(h, cross_norm[l]), rmsnorm(mem, mem_norm[l]),
                                       w_cq[l], w_ckv[l], w_co[l])
        h = h + FFN_RES_WEIGHT * swiglu(rmsnorm(h, ffn2_norm[l]), ffn2_w_gate[l], ffn2_w_up[l], ffn2_w_down[l])
    return rmsnorm(h, final_norm)


import jax as _jax
import jax.numpy as _jnp

TWIN_FORMAT = 'train_step'
FWD_PARAMS = ['x', 'mem', 'ffn1_norm', 'ffn1_w_gate', 'ffn1_w_up', 'ffn1_w_down', 'mix_norm', 'w_in', 'rel_bias', 'w_pool', 'pool_scale', 'w_out', 'cross_norm', 'mem_norm', 'w_cq', 'w_ckv', 'w_co', 'ffn2_norm', 'ffn2_w_gate', 'ffn2_w_up', 'ffn2_w_down', 'final_norm']
TWIN_WEIGHTS = ['ffn1_norm', 'ffn1_w_gate', 'ffn1_w_up', 'ffn1_w_down', 'mix_norm', 'w_in', 'rel_bias', 'w_pool', 'pool_scale', 'w_out', 'cross_norm', 'mem_norm', 'w_cq', 'w_ckv', 'w_co', 'ffn2_norm', 'ffn2_w_gate', 'ffn2_w_up', 'ffn2_w_down', 'final_norm']
TWIN_DIFF_INPUT = 'x'
TWIN_INPUTS = ['x', 'mem', 'ffn1_norm', 'ffn1_w_gate', 'ffn1_w_up', 'ffn1_w_down', 'mix_norm', 'w_in', 'rel_bias', 'w_pool', 'pool_scale', 'w_out', 'cross_norm', 'mem_norm', 'w_cq', 'w_ckv', 'w_co', 'ffn2_norm', 'ffn2_w_gate', 'ffn2_w_up', 'ffn2_w_down', 'final_norm', 'loss_target', 'm_ffn1_norm', 'm_ffn1_w_gate', 'm_ffn1_w_up', 'm_ffn1_w_down', 'm_mix_norm', 'm_w_in', 'm_rel_bias', 'm_w_pool', 'm_pool_scale', 'm_w_out', 'm_cross_norm', 'm_mem_norm', 'm_w_cq', 'm_w_ckv', 'm_w_co', 'm_ffn2_norm', 'm_ffn2_w_gate', 'm_ffn2_w_up', 'm_ffn2_w_down', 'm_final_norm', 'v_ffn1_norm', 'v_ffn1_w_gate', 'v_ffn1_w_up', 'v_ffn1_w_down', 'v_mix_norm', 'v_w_in', 'v_rel_bias', 'v_w_pool', 'v_pool_scale', 'v_w_out', 'v_cross_norm', 'v_mem_norm', 'v_w_cq', 'v_w_ckv', 'v_w_co', 'v_ffn2_norm', 'v_ffn2_w_gate', 'v_ffn2_w_up', 'v_ffn2_w_down', 'v_final_norm']
TWIN_OUTPUTS = ['loss', 'grad_x', 'grad_ffn1_norm', 'grad_ffn1_w_gate', 'grad_ffn1_w_up', 'grad_ffn1_w_down', 'grad_mix_norm', 'grad_w_in', 'grad_rel_bias', 'grad_w_pool', 'grad_pool_scale', 'grad_w_out', 'grad_cross_norm', 'grad_mem_norm', 'grad_w_cq', 'grad_w_ckv', 'grad_w_co', 'grad_ffn2_norm', 'grad_ffn2_w_gate', 'grad_ffn2_w_up', 'grad_ffn2_w_down', 'grad_final_norm', 'delta_ffn1_norm', 'delta_ffn1_w_gate', 'delta_ffn1_w_up', 'delta_ffn1_w_down', 'delta_mix_norm', 'delta_w_in', 'delta_rel_bias', 'delta_w_pool', 'delta_pool_scale', 'delta_w_out', 'delta_cross_norm', 'delta_mem_norm', 'delta_w_cq', 'delta_w_ckv', 'delta_w_co', 'delta_ffn2_norm', 'delta_ffn2_w_gate', 'delta_ffn2_w_up', 'delta_ffn2_w_down', 'delta_final_norm', 'new_m_ffn1_norm', 'new_m_ffn1_w_gate', 'new_m_ffn1_w_up', 'new_m_ffn1_w_down', 'new_m_mix_norm', 'new_m_w_in', 'new_m_rel_bias', 'new_m_w_pool', 'new_m_pool_scale', 'new_m_w_out', 'new_m_cross_norm', 'new_m_mem_norm', 'new_m_w_cq', 'new_m_w_ckv', 'new_m_w_co', 'new_m_ffn2_norm', 'new_m_ffn2_w_gate', 'new_m_ffn2_w_up', 'new_m_ffn2_w_down', 'new_m_final_norm', 'new_v_ffn1_norm', 'new_v_ffn1_w_gate', 'new_v_ffn1_w_up', 'new_v_ffn1_w_down', 'new_v_mix_norm', 'new_v_w_in', 'new_v_rel_bias', 'new_v_w_pool', 'new_v_pool_scale', 'new_v_w_out', 'new_v_cross_norm', 'new_v_mem_norm', 'new_v_w_cq', 'new_v_w_ckv', 'new_v_w_co', 'new_v_ffn2_norm', 'new_v_ffn2_w_gate', 'new_v_ffn2_w_up', 'new_v_ffn2_w_down', 'new_v_final_norm']
TWIN_LEAF_KINDS = {'loss': 'loss', 'grad_x': 'grad_x', 'grad_ffn1_norm': 'grad_w', 'grad_ffn1_w_gate': 'grad_w', 'grad_ffn1_w_up': 'grad_w', 'grad_ffn1_w_down': 'grad_w', 'grad_mix_norm': 'grad_w', 'grad_w_in': 'grad_w', 'grad_rel_bias': 'grad_w', 'grad_w_pool': 'grad_w', 'grad_pool_scale': 'grad_w', 'grad_w_out': 'grad_w', 'grad_cross_norm': 'grad_w', 'grad_mem_norm': 'grad_w', 'grad_w_cq': 'grad_w', 'grad_w_ckv': 'grad_w', 'grad_w_co': 'grad_w', 'grad_ffn2_norm': 'grad_w', 'grad_ffn2_w_gate': 'grad_w', 'grad_ffn2_w_up': 'grad_w', 'grad_ffn2_w_down': 'grad_w', 'grad_final_norm': 'grad_w', 'delta_ffn1_norm': 'delta_w', 'delta_ffn1_w_gate': 'delta_w', 'delta_ffn1_w_up': 'delta_w', 'delta_ffn1_w_down': 'delta_w', 'delta_mix_norm': 'delta_w', 'delta_w_in': 'delta_w', 'delta_rel_bias': 'delta_w', 'delta_w_pool': 'delta_w', 'delta_pool_scale': 'delta_w', 'delta_w_out': 'delta_w', 'delta_cross_norm': 'delta_w', 'delta_mem_norm': 'delta_w', 'delta_w_cq': 'delta_w', 'delta_w_ckv': 'delta_w', 'delta_w_co': 'delta_w', 'delta_ffn2_norm': 'delta_w', 'delta_ffn2_w_gate': 'delta_w', 'delta_ffn2_w_up': 'delta_w', 'delta_ffn2_w_down': 'delta_w', 'delta_final_norm': 'delta_w', 'new_m_ffn1_norm': 'new_m', 'new_m_ffn1_w_gate': 'new_m', 'new_m_ffn1_w_up': 'new_m', 'new_m_ffn1_w_down': 'new_m', 'new_m_mix_norm': 'new_m', 'new_m_w_in': 'new_m', 'new_m_rel_bias': 'new_m', 'new_m_w_pool': 'new_m', 'new_m_pool_scale': 'new_m', 'new_m_w_out': 'new_m', 'new_m_cross_norm': 'new_m', 'new_m_mem_norm': 'new_m', 'new_m_w_cq': 'new_m', 'new_m_w_ckv': 'new_m', 'new_m_w_co': 'new_m', 'new_m_ffn2_norm': 'new_m', 'new_m_ffn2_w_gate': 'new_m', 'new_m_ffn2_w_up': 'new_m', 'new_m_ffn2_w_down': 'new_m', 'new_m_final_norm': 'new_m', 'new_v_ffn1_norm': 'new_v', 'new_v_ffn1_w_gate': 'new_v', 'new_v_ffn1_w_up': 'new_v', 'new_v_ffn1_w_down': 'new_v', 'new_v_mix_norm': 'new_v', 'new_v_w_in': 'new_v', 'new_v_rel_bias': 'new_v', 'new_v_w_pool': 'new_v', 'new_v_pool_scale': 'new_v', 'new_v_w_out': 'new_v', 'new_v_cross_norm': 'new_v', 'new_v_mem_norm': 'new_v', 'new_v_w_cq': 'new_v', 'new_v_w_ckv': 'new_v', 'new_v_w_co': 'new_v', 'new_v_ffn2_norm': 'new_v', 'new_v_ffn2_w_gate': 'new_v', 'new_v_ffn2_w_up': 'new_v', 'new_v_ffn2_w_down': 'new_v', 'new_v_final_norm': 'new_v'}


def _forward(args):
    return _fwd_reference(*[args[k] for k in FWD_PARAMS])


def _output_shape():
    out = _jax.eval_shape(lambda: _forward(_fwd_setup_inputs(0)))
    return out.shape, out.dtype

N_MICROBATCH = 1
ADAM_LR = 0.001
ADAM_B1 = 0.9
ADAM_B2 = 0.999
ADAM_EPS = 1e-08
ADAM_WD = 0.01
ADAM_STEP = 10
PER_EXAMPLE_BATCH_AXIS = {'x': 0, 'mem': 0, 'loss_target': 0}
SHARED_INPUTS = []
_WEIGHT_DTYPES = {'ffn1_norm': _jnp.float32, 'ffn1_w_gate': _jnp.float32, 'ffn1_w_up': _jnp.float32, 'ffn1_w_down': _jnp.float32, 'mix_norm': _jnp.float32, 'w_in': _jnp.float32, 'rel_bias': _jnp.float32, 'w_pool': _jnp.float32, 'pool_scale': _jnp.float32, 'w_out': _jnp.float32, 'cross_norm': _jnp.float32, 'mem_norm': _jnp.float32, 'w_cq': _jnp.float32, 'w_ckv': _jnp.float32, 'w_co': _jnp.float32, 'ffn2_norm': _jnp.float32, 'ffn2_w_gate': _jnp.float32, 'ffn2_w_up': _jnp.float32, 'ffn2_w_down': _jnp.float32, 'final_norm': _jnp.float32}
MOMENT_SCALE = {'ffn1_norm': 4.047185e-02, 'ffn1_w_gate': 1.646169e-02, 'ffn1_w_up': 1.596236e-02, 'ffn1_w_down': 2.646862e-02, 'mix_norm': 5.112062e-02, 'w_in': 3.433417e-02, 'rel_bias': 5.617254e-03, 'w_pool': 6.846034e-02, 'pool_scale': 7.749203e-02, 'w_out': 4.844650e-02, 'cross_norm': 7.795787e-03, 'mem_norm': 1.208042e-02, 'w_cq': 1.577002e-02, 'w_ckv': 1.594892e-02, 'w_co': 8.150617e-03, 'ffn2_norm': 3.296426e-02, 'ffn2_w_gate': 1.329723e-02, 'ffn2_w_up': 1.290708e-02, 'ffn2_w_down': 2.143084e-02, 'final_norm': 1.602365e+01}


def _to_microbatches(a, axis):
    t = _jnp.moveaxis(a, axis, 0)
    t = t.reshape((N_MICROBATCH, t.shape[0] // N_MICROBATCH) + t.shape[1:])
    return _jnp.moveaxis(t, 1, axis + 1)


def setup_inputs(seed: int = 0) -> dict:
    inp = _fwd_setup_inputs(seed)
    key = _jax.random.fold_in(_jax.random.key(seed), 7919)
    shape, _ = _output_shape()
    out = dict(inp)
    out["loss_target"] = _jax.random.normal(_jax.random.fold_in(key, 0), shape, _jnp.float32)
    for i, name in enumerate(TWIN_WEIGHTS):
        w = inp[name].astype(_jnp.float32)
        if MOMENT_SCALE is None:
            s = _jnp.sqrt(_jnp.mean(_jnp.square(w)) + 1e-30)
        else:
            s = MOMENT_SCALE[name]
        km, kv = _jax.random.split(_jax.random.fold_in(key, i + 1))
        out[name] = w
        out["m_" + name] = s * _jax.random.normal(km, w.shape, _jnp.float32)
        out["v_" + name] = (s * s) * _jax.random.uniform(kv, w.shape, _jnp.float32, 0.5, 1.5)
    if N_MICROBATCH > 1:
        for name, axis in PER_EXAMPLE_BATCH_AXIS.items():
            out[name] = _to_microbatches(out[name], axis)
    return {'x': out['x'], 'mem': out['mem'], 'ffn1_norm': out['ffn1_norm'], 'ffn1_w_gate': out['ffn1_w_gate'], 'ffn1_w_up': out['ffn1_w_up'], 'ffn1_w_down': out['ffn1_w_down'], 'mix_norm': out['mix_norm'], 'w_in': out['w_in'], 'rel_bias': out['rel_bias'], 'w_pool': out['w_pool'], 'pool_scale': out['pool_scale'], 'w_out': out['w_out'], 'cross_norm': out['cross_norm'], 'mem_norm': out['mem_norm'], 'w_cq': out['w_cq'], 'w_ckv': out['w_ckv'], 'w_co': out['w_co'], 'ffn2_norm': out['ffn2_norm'], 'ffn2_w_gate': out['ffn2_w_gate'], 'ffn2_w_up': out['ffn2_w_up'], 'ffn2_w_down': out['ffn2_w_down'], 'final_norm': out['final_norm'], 'loss_target': out['loss_target'], 'm_ffn1_norm': out['m_ffn1_norm'], 'm_ffn1_w_gate': out['m_ffn1_w_gate'], 'm_ffn1_w_up': out['m_ffn1_w_up'], 'm_ffn1_w_down': out['m_ffn1_w_down'], 'm_mix_norm': out['m_mix_norm'], 'm_w_in': out['m_w_in'], 'm_rel_bias': out['m_rel_bias'], 'm_w_pool': out['m_w_pool'], 'm_pool_scale': out['m_pool_scale'], 'm_w_out': out['m_w_out'], 'm_cross_norm': out['m_cross_norm'], 'm_mem_norm': out['m_mem_norm'], 'm_w_cq': out['m_w_cq'], 'm_w_ckv': out['m_w_ckv'], 'm_w_co': out['m_w_co'], 'm_ffn2_norm': out['m_ffn2_norm'], 'm_ffn2_w_gate': out['m_ffn2_w_gate'], 'm_ffn2_w_up': out['m_ffn2_w_up'], 'm_ffn2_w_down': out['m_ffn2_w_down'], 'm_final_norm': out['m_final_norm'], 'v_ffn1_norm': out['v_ffn1_norm'], 'v_ffn1_w_gate': out['v_ffn1_w_gate'], 'v_ffn1_w_up': out['v_ffn1_w_up'], 'v_ffn1_w_down': out['v_ffn1_w_down'], 'v_mix_norm': out['v_mix_norm'], 'v_w_in': out['v_w_in'], 'v_rel_bias': out['v_rel_bias'], 'v_w_pool': out['v_w_pool'], 'v_pool_scale': out['v_pool_scale'], 'v_w_out': out['v_w_out'], 'v_cross_norm': out['v_cross_norm'], 'v_mem_norm': out['v_mem_norm'], 'v_w_cq': out['v_w_cq'], 'v_w_ckv': out['v_w_ckv'], 'v_w_co': out['v_w_co'], 'v_ffn2_norm': out['v_ffn2_norm'], 'v_ffn2_w_gate': out['v_ffn2_w_gate'], 'v_ffn2_w_up': out['v_ffn2_w_up'], 'v_ffn2_w_down': out['v_ffn2_w_down'], 'v_final_norm': out['v_final_norm']}


def _loss(weights, diff, rest, loss_target):
    with _jax.named_scope("forward"):
        args = {**rest, TWIN_DIFF_INPUT: diff, **{k: w.astype(_WEIGHT_DTYPES[k]) for k, w in weights.items()}}
        y = _forward(args)
    with _jax.named_scope("loss_head"):
        err = _jnp.square(y.astype(_jnp.float32) - loss_target)
        return 0.5 * _jnp.sum(_jnp.mean(err, axis=-1)) if err.ndim else 0.5 * err


def _adamw(w, g, m, v):
    m = ADAM_B1 * m + (1.0 - ADAM_B1) * g
    v = ADAM_B2 * v + (1.0 - ADAM_B2) * _jnp.square(g)
    m_hat = m / (1.0 - ADAM_B1 ** ADAM_STEP)
    v_hat = v / (1.0 - ADAM_B2 ** ADAM_STEP)
    delta = -ADAM_LR * (m_hat / (_jnp.sqrt(v_hat) + ADAM_EPS) + ADAM_WD * w)
    return delta, m, v


def reference(x, mem, ffn1_norm, ffn1_w_gate, ffn1_w_up, ffn1_w_down, mix_norm, w_in, rel_bias, w_pool, pool_scale, w_out, cross_norm, mem_norm, w_cq, w_ckv, w_co, ffn2_norm, ffn2_w_gate, ffn2_w_up, ffn2_w_down, final_norm, loss_target, m_ffn1_norm, m_ffn1_w_gate, m_ffn1_w_up, m_ffn1_w_down, m_mix_norm, m_w_in, m_rel_bias, m_w_pool, m_pool_scale, m_w_out, m_cross_norm, m_mem_norm, m_w_cq, m_w_ckv, m_w_co, m_ffn2_norm, m_ffn2_w_gate, m_ffn2_w_up, m_ffn2_w_down, m_final_norm, v_ffn1_norm, v_ffn1_w_gate, v_ffn1_w_up, v_ffn1_w_down, v_mix_norm, v_w_in, v_rel_bias, v_w_pool, v_pool_scale, v_w_out, v_cross_norm, v_mem_norm, v_w_cq, v_w_ckv, v_w_co, v_ffn2_norm, v_ffn2_w_gate, v_ffn2_w_up, v_ffn2_w_down, v_final_norm):
    given = dict(x=x, mem=mem, ffn1_norm=ffn1_norm, ffn1_w_gate=ffn1_w_gate, ffn1_w_up=ffn1_w_up, ffn1_w_down=ffn1_w_down, mix_norm=mix_norm, w_in=w_in, rel_bias=rel_bias, w_pool=w_pool, pool_scale=pool_scale, w_out=w_out, cross_norm=cross_norm, mem_norm=mem_norm, w_cq=w_cq, w_ckv=w_ckv, w_co=w_co, ffn2_norm=ffn2_norm, ffn2_w_gate=ffn2_w_gate, ffn2_w_up=ffn2_w_up, ffn2_w_down=ffn2_w_down, final_norm=final_norm, loss_target=loss_target, m_ffn1_norm=m_ffn1_norm, m_ffn1_w_gate=m_ffn1_w_gate, m_ffn1_w_up=m_ffn1_w_up, m_ffn1_w_down=m_ffn1_w_down, m_mix_norm=m_mix_norm, m_w_in=m_w_in, m_rel_bias=m_rel_bias, m_w_pool=m_w_pool, m_pool_scale=m_pool_scale, m_w_out=m_w_out, m_cross_norm=m_cross_norm, m_mem_norm=m_mem_norm, m_w_cq=m_w_cq, m_w_ckv=m_w_ckv, m_w_co=m_w_co, m_ffn2_norm=m_ffn2_norm, m_ffn2_w_gate=m_ffn2_w_gate, m_ffn2_w_up=m_ffn2_w_up, m_ffn2_w_down=m_ffn2_w_down, m_final_norm=m_final_norm, v_ffn1_norm=v_ffn1_norm, v_ffn1_w_gate=v_ffn1_w_gate, v_ffn1_w_up=v_ffn1_w_up, v_ffn1_w_down=v_ffn1_w_down, v_mix_norm=v_mix_norm, v_w_in=v_w_in, v_rel_bias=v_rel_bias, v_w_pool=v_w_pool, v_pool_scale=v_pool_scale, v_w_out=v_w_out, v_cross_norm=v_cross_norm, v_mem_norm=v_mem_norm, v_w_cq=v_w_cq, v_w_ckv=v_w_ckv, v_w_co=v_w_co, v_ffn2_norm=v_ffn2_norm, v_ffn2_w_gate=v_ffn2_w_gate, v_ffn2_w_up=v_ffn2_w_up, v_ffn2_w_down=v_ffn2_w_down, v_final_norm=v_final_norm)
    weights = {n: given[n] for n in TWIN_WEIGHTS}
    shared = {n: given[n] for n in SHARED_INPUTS}
    per_example = {n: given[n] for n in ['x', 'mem']}
    grad_fn = _jax.value_and_grad(_loss, argnums=(0, 1))

    def one_microbatch(ex, loss_target):
        ex = dict(ex)
        diff = ex.pop(TWIN_DIFF_INPUT)
        return grad_fn(weights, diff, {**shared, **ex}, loss_target)

    if N_MICROBATCH == 1:
        loss, (grad_w, grad_x) = one_microbatch(per_example, given["loss_target"])
    else:
        def body(carry, xs):
            loss_sum, grad_sum = carry
            l_k, (gw_k, gx_k) = one_microbatch(xs[0], xs[1])
            with _jax.named_scope("update"):
                return (loss_sum + l_k, _jax.tree.map(_jnp.add, grad_sum, gw_k)), gx_k

        init = (_jnp.zeros((), _jnp.float32), _jax.tree.map(_jnp.zeros_like, weights))
        (loss, grad_w), grad_x = _jax.lax.scan(body, init, (per_example, given["loss_target"]))
    with _jax.named_scope("update"):
        delta_w, new_m, new_v = {}, {}, {}
        for n in TWIN_WEIGHTS:
            delta_w[n], new_m[n], new_v[n] = _adamw(weights[n], grad_w[n], given["m_" + n], given["v_" + n])
    return (loss, grad_x, *[grad_w[n] for n in TWIN_WEIGHTS], *[delta_w[n] for n in TWIN_WEIGHTS],
            *[new_m[n] for n in TWIN_WEIGHTS], *[new_v[n] for n in TWIN_WEIGHTS])
```

```python
import functools

import jax
import jax.numpy as jnp
from jax import lax
from jax.experimental import pallas as pl
from jax.experimental.pallas import tpu as pltpu

F32 = jnp.float32
BF16 = jnp.bfloat16

N_DEV = 8
EPS = 1e-6
NEG_INF = -1e30
CHUNK = 64
LEFT_CHUNKS = 8
PAD = LEFT_CHUNKS * CHUNK
QBLK = 2 * CHUNK
KBAND = PAD + QBLK
REL_CLIP = 128
ATTN_HEADS = 16
HEAD_DIM = 64
D_ATTN = ATTN_HEADS * HEAD_DIM
POOL_WINDOWS = (2, 4, 8, 16)
POOL_GROUP = 256
D_POOL = len(POOL_WINDOWS) * POOL_GROUP
CROSS_HEADS = 4
CROSS_DIM = 128
D_CROSS = CROSS_HEADS * CROSS_DIM
FFN_RES = 0.5
ADAM_LR, ADAM_B1, ADAM_B2, ADAM_EPS, ADAM_WD, ADAM_STEP = 0.001, 0.9, 0.999, 1e-08, 0.01, 10

NN = (((1,), (0,)), ((), ()))
NT = (((1,), (1,)), ((), ()))
TN = (((0,), (0,)), ((), ()))
MESH = pl.DeviceIdType.MESH
VMEM_LIMIT = 56 * 1024 * 1024


def _params(**kw):
    return pltpu.CompilerParams(vmem_limit_bytes=VMEM_LIMIT, **kw)


def _bf(v):
    return v if v.dtype == BF16 else v.astype(BF16)


def _gemm(name, pairs, dims, grid, outs, extras=(), epilogue=None, acc_shape=None):
    nk, npairs, nex, nout = grid[2], len(pairs), len(extras), len(outs)

    def body(*refs):
        ab = refs[:2 * npairs]
        ex = refs[2 * npairs:2 * npairs + nex]
        out = refs[2 * npairs + nex:2 * npairs + nex + nout]
        part = None
        for p in range(npairs):
            d = lax.dot_general(_bf(ab[2 * p][...]), _bf(ab[2 * p + 1][...]), dims, preferred_element_type=F32)
            part = d if part is None else part + d

        def finish(total):
            vals = epilogue(total, *[e[...] for e in ex]) if epilogue is not None else (total,)
            for r, v in zip(out, vals):
                r[...] = v.astype(r.dtype)

        if nk == 1:
            finish(part)
        else:
            acc = refs[-1]
            k = pl.program_id(2)

            @pl.when(k == 0)
            def _():
                acc[...] = part

            @pl.when(k > 0)
            def _():
                acc[...] += part

            @pl.when(k == nk - 1)
            def _():
                finish(acc[...])

    operands, in_specs = [], []
    for a, ablk, amap, b, bblk, bmap in pairs:
        operands += [a, b]
        in_specs += [pl.BlockSpec(ablk, amap), pl.BlockSpec(bblk, bmap)]
    for x, xblk, xmap in extras:
        operands.append(x)
        in_specs.append(pl.BlockSpec(xblk, xmap))
    res = pl.pallas_call(
        body, name=name, grid=grid, in_specs=in_specs,
        out_specs=[pl.BlockSpec(blk, m) for _, _, blk, m in outs],
        out_shape=[jax.ShapeDtypeStruct(s, d) for s, d, _, _ in outs],
        scratch_shapes=[pltpu.VMEM(acc_shape, F32)] if nk > 1 else [],
        compiler_params=_params(dimension_semantics=("parallel", "parallel", "arbitrary")),
    )(*operands)
    return res[0] if nout == 1 else res


def _tile(n, want):
    for t in range(min(n, want), 15, -1):
        if n % t == 0 and t % 16 == 0:
            return t
    return n


def _mm_nn(name, a, b, out_dtype, res=None, scale=1.0, tm=1024, tn=1024, tk=None):
    M, K = a.shape
    N = b.shape[1]
    tm, tn = _tile(M, tm), _tile(N, tn)
    tk = K if tk is None else _tile(K, tk)
    extras = [] if res is None else [(res, (tm, tn), lambda i, j, k: (i, j))]
    epi = (lambda t: (t * scale,)) if res is None else (lambda t, r: (r + scale * t,))
    return _gemm(name, [(a, (tm, tk), lambda i, j, k: (i, k), b, (tk, tn), lambda i, j, k: (k, j))], NN,
                 (M // tm, N // tn, K // tk), [((M, N), out_dtype, (tm, tn), lambda i, j, k: (i, j))],
                 extras, epi, (tm, tn))


def _mm_nn_cols(name, a, bs, out_dtype, res=None, tm=1024):
    M, K = a.shape
    nb, _, w = bs.shape
    tm = _tile(M, tm)
    extras = [] if res is None else [(res, (tm, w), lambda i, j, k: (i, j))]
    epi = None if res is None else (lambda t, r: (r + t,))
    return _gemm(name, [(a, (tm, K), lambda i, j, k: (i, 0), bs, (None, K, w), lambda i, j, k: (j, 0, 0))], NN,
                 (M // tm, nb, 1), [((M, nb * w), out_dtype, (tm, w), lambda i, j, k: (i, j))], extras, epi)


def _mm_nt(name, a, b, out_dtype, extras=(), epilogue=None, n_out=1, tm=1024, tn=1024):
    M, K = a.shape
    N = b.shape[0]
    tm, tn = _tile(M, tm), _tile(N, tn)
    ex = [(x, (tm, tn), lambda i, j, k: (i, j)) for x in extras]
    outs = [((M, N), out_dtype, (tm, tn), lambda i, j, k: (i, j))] * n_out
    return _gemm(name, [(a, (tm, K), lambda i, j, k: (i, 0), b, (tn, K), lambda i, j, k: (j, 0))], NT,
                 (M // tm, N // tn, 1), outs, ex, epilogue)


def _mm_nt_cols(name, pairs, out_dtype, tm=1024, tn=1024):
    a0, bs0 = pairs[0]
    M = a0.shape[0]
    nb, N, w = bs0.shape
    tm, tn = _tile(M, tm), _tile(N, tn)
    gp = [(a, (tm, w), lambda i, j, k: (i, k), bs, (None, tn, w), lambda i, j, k: (k, j, 0)) for a, bs in pairs]
    return _gemm(name, gp, NT, (M // tm, N // tn, nb), [((M, N), out_dtype, (tm, tn), lambda i, j, k: (i, j))],
                 acc_shape=(tm, tn))


def _mm_tn(name, a, b, scale=1.0, tm=2048, tn=1024, tk=1024, col_blocks=None):
    T, Ka = a.shape
    Nb = b.shape[1]
    tk = _tile(T, tk)
    tm = _tile(Ka, tm)
    epi = None if scale == 1.0 else (lambda t: (t * scale,))
    if col_blocks is None:
        tn = _tile(Nb, tn)
        out = ((Ka, Nb), BF16, (tm, tn), lambda i, j, k: (i, j))
        nj = Nb // tn
    else:
        tn = Nb // col_blocks
        out = ((col_blocks, Ka, tn), BF16, (None, tm, tn), lambda i, j, k: (j, i, 0))
        nj = col_blocks
    return _gemm(name, [(a, (tk, tm), lambda i, j, k: (k, i), b, (tk, tn), lambda i, j, k: (k, j))], TN,
                 (Ka // tm, nj, T // tk), [out], (), epi, (tm, tn))


def _ffn_down(name, a, wd, res, tm=1024, tn=1024):
    nb, M, w = a.shape
    N = wd.shape[1]
    tm, tn = _tile(M, tm), _tile(N, tn)
    return _gemm(name, [(a, (None, tm, w), lambda i, j, k: (k, i, 0), wd, (w, tn), lambda i, j, k: (k, j))], NN,
                 (M // tm, N // tn, nb), [((M, N), F32, (tm, tn), lambda i, j, k: (i, j))],
                 [(res, (tm, tn), lambda i, j, k: (i, j))], lambda t, r: (r + FFN_RES * t,), (tm, tn))


def _ffn_dact(name, dhb, wd, g, u, tm=1024):
    M, K = dhb.shape
    nb, _, w = g.shape
    tm = _tile(M, tm)
    hid = ((None, tm, w), lambda i, j, k: (j, i, 0))
    return _gemm(name, [(dhb, (tm, K), lambda i, j, k: (i, 0), wd, (w, K), lambda i, j, k: (j, 0))], NT,
                 (M // tm, nb, 1), [((nb, M, w), BF16) + hid] * 2, [(g,) + hid, (u,) + hid], _swiglu_bwd)


def _ffn_dwd(name, a, dhb, tk=1024):
    nb, T, w = a.shape
    N = dhb.shape[1]
    tk = _tile(T, tk)
    return _gemm(name, [(a, (None, tk, w), lambda i, j, k: (i, k, 0), dhb, (tk, N), lambda i, j, k: (k, 0))], TN,
                 (nb, 1, T // tk), [((nb * w, N), BF16, (w, N), lambda i, j, k: (i, 0))], (),
                 lambda t: (t * FFN_RES,), (w, N))


def _ffn_dwcol(name, hn, dg, tk=1024):
    T, K = hn.shape
    nb, _, w = dg.shape
    tk = _tile(T, tk)
    return _gemm(name, [(hn, (tk, K), lambda i, j, k: (k, 0), dg, (None, tk, w), lambda i, j, k: (j, k, 0))], TN,
                 (1, nb, T // tk), [((nb, K, w), BF16, (None, K, w), lambda i, j, k: (j, 0, 0))], (), None, (K, w))


def _ffn_dhn(name, pairs, tm=1024, tn=1024):
    nb, M, w = pairs[0][0].shape
    N = pairs[0][1].shape[1]
    tm, tn = _tile(M, tm), _tile(N, tn)
    gp = [(d, (None, tm, w), lambda i, j, k: (k, i, 0), ws, (None, tn, w), lambda i, j, k: (k, j, 0))
          for d, ws in pairs]
    return _gemm(name, gp, NT, (M // tm, N // tn, nb), [((M, N), F32, (tm, tn), lambda i, j, k: (i, j))],
                 acc_shape=(tm, tn))


def _ffn_up(name, hn, wg, wu, tm=1024):
    M, K = hn.shape
    nb, _, w = wg.shape
    tm = _tile(M, tm)

    def body(a_ref, g_ref, u_ref, og, ou, oa):
        a = a_ref[...]
        g = jnp.dot(a, g_ref[...], preferred_element_type=F32)
        u = jnp.dot(a, u_ref[...], preferred_element_type=F32)
        og[...] = g.astype(BF16)
        ou[...] = u.astype(BF16)
        oa[...] = (g * jax.nn.sigmoid(g) * u).astype(BF16)

    wspec = pl.BlockSpec((None, K, w), lambda i, j: (j, 0, 0))
    ospec = pl.BlockSpec((None, tm, w), lambda i, j: (j, i, 0))
    return pl.pallas_call(
        body, name=name, grid=(M // tm, nb),
        in_specs=[pl.BlockSpec((tm, K), lambda i, j: (i, 0)), wspec, wspec],
        out_specs=[ospec] * 3, out_shape=[jax.ShapeDtypeStruct((nb, M, w), BF16)] * 3,
        compiler_params=_params(dimension_semantics=("parallel", "parallel")),
    )(hn, wg, wu)


def _swiglu_bwd(dact, g, u):
    g = g.astype(F32)
    u = u.astype(F32)
    sig = jax.nn.sigmoid(g)
    silu = g * sig
    d = FFN_RES * dact
    return d * u * (sig * (1.0 + g * (1.0 - sig))), d * silu


def _rms_fwd(name, x, gain, tr=512):
    R, D = x.shape
    tr = _tile(R, tr)

    def body(x_ref, g_ref, o_ref):
        xv = x_ref[...]
        y = xv * lax.rsqrt(jnp.mean(xv * xv, axis=-1, keepdims=True) + EPS)
        o_ref[...] = (y * g_ref[...]).astype(BF16)

    return pl.pallas_call(
        body, name=name, grid=(R // tr,),
        in_specs=[pl.BlockSpec((tr, D), lambda i: (i, 0)), pl.BlockSpec((1, D), lambda i: (0, 0))],
        out_specs=pl.BlockSpec((tr, D), lambda i: (i, 0)), out_shape=jax.ShapeDtypeStruct((R, D), BF16),
        compiler_params=_params(dimension_semantics=("parallel",)),
    )(x, gain)


def _rms_bwd_math(xv, gain, dy):
    rstd = lax.rsqrt(jnp.mean(xv * xv, axis=-1, keepdims=True) + EPS)
    xhat = xv * rstd
    dxh = dy * gain
    dx = rstd * (dxh - xhat * jnp.mean(dxh * xhat, axis=-1, keepdims=True))
    return dx, jnp.sum(dy * xhat, axis=0, keepdims=True)


def _rms_bwd(name, x, gain, dy, skip=None, tr=256):
    R, D = x.shape
    tr = _tile(R, tr)
    has_skip = skip is not None

    def body(*refs):
        x_ref, g_ref, dy_ref = refs[:3]
        dx_ref, dxb_ref, dg_ref = refs[-3:]
        dx, dg = _rms_bwd_math(x_ref[...], g_ref[...], dy_ref[...].astype(F32))
        if has_skip:
            dx = dx + refs[3][...]
        dx_ref[...] = dx
        dxb_ref[...] = dx.astype(BF16)

        @pl.when(pl.program_id(0) == 0)
        def _():
            dg_ref[...] = dg

        @pl.when(pl.program_id(0) > 0)
        def _():
            dg_ref[...] += dg

    row = pl.BlockSpec((tr, D), lambda i: (i, 0))
    vec = pl.BlockSpec((1, D), lambda i: (0, 0))
    return pl.pallas_call(
        body, name=name, grid=(R // tr,),
        in_specs=[row, vec, row] + ([row] if has_skip else []),
        out_specs=[row, row, vec],
        out_shape=[jax.ShapeDtypeStruct((R, D), F32), jax.ShapeDtypeStruct((R, D), BF16),
                   jax.ShapeDtypeStruct((1, D), F32)],
        compiler_params=_params(dimension_semantics=("arbitrary",)),
    )(*([x, gain, dy] + ([skip] if has_skip else [])))


def _loss_and_grad(name, h, gain, target, tr=256):
    R, D = h.shape
    tr = _tile(R, tr)

    def body(h_ref, g_ref, t_ref, loss_ref, dh_ref, dhb_ref, dg_ref):
        hv, gain_v = h_ref[...], g_ref[...]
        y = (hv * lax.rsqrt(jnp.mean(hv * hv, axis=-1, keepdims=True) + EPS)) * gain_v
        err = y - t_ref[...]
        part = jnp.full((8, 128), 0.5 * jnp.sum(jnp.mean(err * err, axis=-1, keepdims=True)), F32)
        dh, dg = _rms_bwd_math(hv, gain_v, err * (1.0 / D))
        dh_ref[...] = dh
        dhb_ref[...] = dh.astype(BF16)

        @pl.when(pl.program_id(0) == 0)
        def _():
            dg_ref[...] = dg
            loss_ref[...] = part

        @pl.when(pl.program_id(0) > 0)
        def _():
            dg_ref[...] += dg
            loss_ref[...] += part

    row = pl.BlockSpec((tr, D), lambda i: (i, 0))
    vec = pl.BlockSpec((1, D), lambda i: (0, 0))
    return pl.pallas_call(
        body, name=name, grid=(R // tr,), in_specs=[row, vec, row],
        out_specs=[pl.BlockSpec((8, 128), lambda i: (0, 0)), row, row, vec],
        out_shape=[jax.ShapeDtypeStruct((8, 128), F32), jax.ShapeDtypeStruct((R, D), F32),
                   jax.ShapeDtypeStruct((R, D), BF16), jax.ShapeDtypeStruct((1, D), F32)],
        compiler_params=_params(dimension_semantics=("arbitrary",)),
    )(h, gain, target)


def _bias_tile(rel):
    i = jnp.arange(QBLK)[:, None]
    j = jnp.arange(KBAND)[None, :]
    idx = jnp.clip(PAD + i - j, -REL_CLIP, REL_CLIP) + REL_CLIP
    valid = (j // CHUNK >= i // CHUNK) & (j // CHUNK <= i // CHUNK + LEFT_CHUNKS)
    return jnp.where(valid[None], rel[:, idx], NEG_INF)


def _band_softmax(q_half, kb, bias, kpos):
    s = lax.dot_general(q_half, kb, NT, preferred_element_type=F32) * (HEAD_DIM ** -0.5) + bias
    s = jnp.where(kpos >= 0, s, NEG_INF)
    e = jnp.exp(s - jnp.max(s, axis=-1, keepdims=True))
    return e / jnp.sum(e, axis=-1, keepdims=True)


def _fill_padded(dst, src, S):
    dst[pl.ds(0, PAD), :] = jnp.zeros((PAD, dst.shape[1]), dst.dtype)
    dst[pl.ds(PAD, S), :] = src[...].astype(dst.dtype)


def _attn_fwd(name, z, bias):
    B, S, _ = z.shape
    nh2 = ATTN_HEADS // 2

    def body(q_ref, k_ref, v_ref, b_ref, o_ref, kp, vp):
        qb = pl.program_id(2)

        @pl.when(qb == 0)
        def _():
            _fill_padded(kp, k_ref, S)
            _fill_padded(vp, v_ref, S)

        start = pl.multiple_of(qb * QBLK, QBLK)
        kb, vb = kp[pl.ds(start, KBAND), :], vp[pl.ds(start, KBAND), :]
        q = q_ref[...].astype(BF16)
        first = lax.broadcasted_iota(jnp.int32, (QBLK, 2 * HEAD_DIM), 1) < HEAD_DIM
        kpos = qb * QBLK - PAD + lax.broadcasted_iota(jnp.int32, (1, KBAND), 1)
        o = []
        for a in range(2):
            p = _band_softmax(jnp.where(first == (a == 0), q, jnp.zeros_like(q)), kb, b_ref[a], kpos)
            o.append(jnp.dot(p.astype(BF16), vb, preferred_element_type=F32))
        o_ref[...] = jnp.where(first, o[0], o[1]).astype(BF16)

    return pl.pallas_call(
        body, name=name, grid=(B, nh2, S // QBLK),
        in_specs=[pl.BlockSpec((None, QBLK, 128), lambda b, h, i: (b, i, h)),
                  pl.BlockSpec((None, S, 128), lambda b, h, i: (b, 0, nh2 + h)),
                  pl.BlockSpec((None, S, 128), lambda b, h, i: (b, 0, 2 * nh2 + h)),
                  pl.BlockSpec((2, QBLK, KBAND), lambda b, h, i: (h, 0, 0))],
        out_specs=pl.BlockSpec((None, QBLK, 128), lambda b, h, i: (b, i, h)),
        out_shape=jax.ShapeDtypeStruct((B, S, D_ATTN), BF16),
        scratch_shapes=[pltpu.VMEM((PAD + S, 128), BF16), pltpu.VMEM((PAD + S, 128), BF16)],
        compiler_params=_params(dimension_semantics=("parallel", "parallel", "arbitrary")),
    )(z, z, z, bias)


def _attn_bwd(name, z, bias, dcat):
    B, S, _ = z.shape
    nh2 = ATTN_HEADS // 2
    nqb = S // QBLK
    scale = HEAD_DIM ** -0.5

    def body(q_ref, k_ref, v_ref, b_ref, do_ref, dq_ref, dk_ref, dv_ref, db_ref, kp, vp, dka, dva):
        b, qb = pl.program_id(1), pl.program_id(2)

        @pl.when(qb == 0)
        def _():
            _fill_padded(kp, k_ref, S)
            _fill_padded(vp, v_ref, S)
            dka[...] = jnp.zeros_like(dka)
            dva[...] = jnp.zeros_like(dva)

        @pl.when((qb == 0) & (b == 0))
        def _():
            db_ref[...] = jnp.zeros_like(db_ref)

        start = pl.multiple_of(qb * QBLK, QBLK)
        band = pl.ds(start, KBAND)
        kb, vb = kp[band, :], vp[band, :]
        q = q_ref[...].astype(BF16)
        do = do_ref[...]
        first = lax.broadcasted_iota(jnp.int32, (QBLK, 2 * HEAD_DIM), 1) < HEAD_DIM
        kpos = qb * QBLK - PAD + lax.broadcasted_iota(jnp.int32, (1, KBAND), 1)
        dq, dk, dv = [], None, None
        for a in range(2):
            mine = first == (a == 0)
            qa = jnp.where(mine, q, jnp.zeros_like(q))
            doa = jnp.where(mine, do, jnp.zeros_like(do))
            p = _band_softmax(qa, kb, b_ref[a], kpos)
            dp = lax.dot_general(doa, vb, NT, preferred_element_type=F32)
            ds = p * (dp - jnp.sum(p * dp, axis=-1, keepdims=True))
            db_ref[a] += ds
            dsb = (ds * scale).astype(BF16)
            dq.append(jnp.dot(dsb, kb, preferred_element_type=F32))
            dka_part = lax.dot_general(dsb, qa, TN, preferred_element_type=F32)
            dva_part = lax.dot_general(p.astype(BF16), doa, TN, preferred_element_type=F32)
            dk = dka_part if dk is None else dk + dka_part
            dv = dva_part if dv is None else dv + dva_part
        dq_ref[...] = jnp.where(first, dq[0], dq[1]).astype(BF16)
        dka[band, :] += dk
        dva[band, :] += dv

        @pl.when(qb == nqb - 1)
        def _():
            dk_ref[...] = dka[pl.ds(PAD, S), :].astype(BF16)
            dv_ref[...] = dva[pl.ds(PAD, S), :].astype(BF16)

    qspec = pl.BlockSpec((None, QBLK, 128), lambda h, b, i: (b, i, h))
    kvout = pl.BlockSpec((None, S, 128), lambda h, b, i: (b, 0, h))
    bspec = pl.BlockSpec((2, QBLK, KBAND), lambda h, b, i: (h, 0, 0))
    act = jax.ShapeDtypeStruct((B, S, D_ATTN), BF16)
    return pl.pallas_call(
        body, name=name, grid=(nh2, B, nqb),
        in_specs=[qspec,
                  pl.BlockSpec((None, S, 128), lambda h, b, i: (b, 0, nh2 + h)),
                  pl.BlockSpec((None, S, 128), lambda h, b, i: (b, 0, 2 * nh2 + h)),
                  bspec, qspec],
        out_specs=[qspec, kvout, kvout, bspec],
        out_shape=[act, act, act, jax.ShapeDtypeStruct((ATTN_HEADS, QBLK, KBAND), F32)],
        scratch_shapes=[pltpu.VMEM((PAD + S, 128), BF16), pltpu.VMEM((PAD + S, 128), BF16),
                        pltpu.VMEM((PAD + S, 128), F32), pltpu.VMEM((PAD + S, 128), F32)],
        compiler_params=_params(dimension_semantics=("arbitrary", "arbitrary", "arbitrary")),
    )(z, z, z, bias, dcat)


def _bias_grad(name, dbias):
    width = KBAND + QBLK

    def body(d_ref, o_ref):
        acc = jnp.zeros((1, width), F32)
        for i in range(QBLK):
            row = jnp.concatenate([d_ref[pl.ds(i, 1), :], jnp.zeros((1, QBLK), F32)], axis=1)
            shift = QBLK - 1 - i
            acc = acc + (pltpu.roll(row, shift, 1) if shift else row)
        o_ref[...] = acc

    return pl.pallas_call(
        body, name=name, grid=(ATTN_HEADS,),
        in_specs=[pl.BlockSpec((None, QBLK, KBAND), lambda h: (h, 0, 0))],
        out_specs=pl.BlockSpec((None, 1, width), lambda h: (h, 0, 0)),
        out_shape=jax.ShapeDtypeStruct((ATTN_HEADS, 1, width), F32),
        compiler_params=_params(dimension_semantics=("parallel",)),
    )(dbias)


def _rel_grad_from_diagonals(diag):
    top = PAD + QBLK - 1 - REL_CLIP
    sat = jnp.sum(diag[:, :top + 1], axis=1, keepdims=True)
    mid = diag[:, top + 1:top + 2 * REL_CLIP][:, ::-1]
    return jnp.concatenate([jnp.zeros_like(sat), mid, sat], axis=1)


def _shift_rows(x, k, forward):
    S = x.shape[0]
    t = lax.broadcasted_iota(jnp.int32, x.shape, 0)
    if forward:
        return jnp.where(t < S - k, pltpu.roll(x, S - k, 0), 0.0)
    return jnp.where(t >= k, pltpu.roll(x, k, 0), 0.0)


def _window_sum(x, g, forward):
    s = x + _shift_rows(x, 1, forward)
    out = s
    for n, k in enumerate((2, 4, 8)):
        s = s + _shift_rows(s, k, forward)
        out = jnp.where(g > n, s, out)
    return out


def _pool_count(S, g):
    t = lax.broadcasted_iota(jnp.int32, (S, 1), 0)
    w = jnp.left_shift(2, g)
    return jnp.minimum(t + 1, w).astype(F32)


def _pool_fwd(name, z, wp, pscale):
    B, S, _ = z.shape
    c0 = 3 * D_ATTN // POOL_GROUP

    def body(u_ref, w_ref, s_ref, d_ref, y_ref):
        g = pl.program_id(1)
        u = u_ref[...]
        d = (_window_sum(u, g, False) / _pool_count(S, g) - u).astype(BF16)
        d_ref[...] = d
        y_ref[...] = (jnp.dot(d, w_ref[...], preferred_element_type=F32) * s_ref[...]).astype(BF16)

    blk = pl.BlockSpec((None, S, POOL_GROUP), lambda b, g: (b, 0, g))
    out = jax.ShapeDtypeStruct((B, S, D_POOL), BF16)
    return pl.pallas_call(
        body, name=name, grid=(B, len(POOL_WINDOWS)),
        in_specs=[pl.BlockSpec((None, S, POOL_GROUP), lambda b, g: (b, 0, c0 + g)),
                  pl.BlockSpec((None, POOL_GROUP, POOL_GROUP), lambda b, g: (g, 0, 0)),
                  pl.BlockSpec((1, POOL_GROUP), lambda b, g: (0, g))],
        out_specs=[blk, blk], out_shape=[out, out],
        compiler_params=_params(dimension_semantics=("parallel", "parallel")),
    )(z, wp, pscale)


def _pool_bwd(name, d, wp, pscale, dcat):
    B, S, _ = d.shape
    c0 = D_ATTN // POOL_GROUP

    def body(d_ref, w_ref, s_ref, dy_ref, du_ref, dw_ref, dsc_ref):
        g, b = pl.program_id(0), pl.program_id(1)
        dv = d_ref[...]
        dy = dy_ref[...].astype(F32)
        w = w_ref[...]
        ypre = jnp.dot(dv, w, preferred_element_type=F32)
        dyp = (dy * s_ref[...]).astype(BF16)
        dd = lax.dot_general(dyp, w, NT, preferred_element_type=F32)
        du_ref[...] = (_window_sum(dd / _pool_count(S, g), g, True) - dd).astype(BF16)
        dw = lax.dot_general(dv, dyp, TN, preferred_element_type=F32)
        dsc = jnp.sum(dy * ypre, axis=0, keepdims=True)

        @pl.when(b == 0)
        def _():
            dw_ref[...] = dw
            dsc_ref[...] = dsc

        @pl.when(b > 0)
        def _():
            dw_ref[...] += dw
            dsc_ref[...] += dsc

    blk = pl.BlockSpec((None, S, POOL_GROUP), lambda g, b: (b, 0, g))
    wspec = pl.BlockSpec((None, POOL_GROUP, POOL_GROUP), lambda g, b: (g, 0, 0))
    sspec = pl.BlockSpec((1, POOL_GROUP), lambda g, b: (0, g))
    return pl.pallas_call(
        body, name=name, grid=(len(POOL_WINDOWS), B),
        in_specs=[blk, wspec, sspec, pl.BlockSpec((None, S, POOL_GROUP), lambda g, b: (b, 0, c0 + g))],
        out_specs=[blk, wspec, sspec],
        out_shape=[jax.ShapeDtypeStruct((B, S, D_POOL), BF16),
                   jax.ShapeDtypeStruct((len(POOL_WINDOWS), POOL_GROUP, POOL_GROUP), F32),
                   jax.ShapeDtypeStruct((1, D_POOL), F32)],
        compiler_params=_params(dimension_semantics=("arbitrary", "arbitrary")),
    )(d, wp, pscale, dcat)


def _cross_softmax(q, k):
    s = lax.dot_general(q, k, NT, preferred_element_type=F32) * (CROSS_DIM ** -0.5)
    e = jnp.exp(s - jnp.max(s, axis=-1, keepdims=True))
    return e / jnp.sum(e, axis=-1, keepdims=True)


def _cross_fwd(name, qc, kv, tq=512):
    B, S, _ = qc.shape
    M = kv.shape[1]
    tq = _tile(S, tq)

    def body(q_ref, k_ref, v_ref, o_ref):
        p = _cross_softmax(q_ref[...], k_ref[...])
        o_ref[...] = jnp.dot(p.astype(BF16), v_ref[...], preferred_element_type=F32).astype(BF16)

    qspec = pl.BlockSpec((None, tq, CROSS_DIM), lambda b, h, i: (b, i, h))
    return pl.pallas_call(
        body, name=name, grid=(B, CROSS_HEADS, S // tq),
        in_specs=[qspec, pl.BlockSpec((None, M, CROSS_DIM), lambda b, h, i: (b, 0, h)),
                  pl.BlockSpec((None, M, CROSS_DIM), lambda b, h, i: (b, 0, CROSS_HEADS + h))],
        out_specs=qspec, out_shape=jax.ShapeDtypeStruct((B, S, D_CROSS), BF16),
        compiler_params=_params(dimension_semantics=("parallel", "parallel", "parallel")),
    )(qc, kv, kv)


def _cross_bwd(name, qc, kv, do, tq=512):
    B, S, _ = qc.shape
    M = kv.shape[1]
    tq = _tile(S, tq)
    nq = S // tq
    scale = CROSS_DIM ** -0.5

    def body(q_ref, k_ref, v_ref, do_ref, dq_ref, dk_ref, dv_ref, dka, dva):
        i = pl.program_id(2)
        q, k, v, dov = q_ref[...], k_ref[...], v_ref[...], do_ref[...]
        p = _cross_softmax(q, k)
        dp = lax.dot_general(dov, v, NT, preferred_element_type=F32)
        ds = ((p * (dp - jnp.sum(p * dp, axis=-1, keepdims=True))) * scale).astype(BF16)
        dq_ref[...] = jnp.dot(ds, k, preferred_element_type=F32).astype(BF16)
        dk = lax.dot_general(ds, q, TN, preferred_element_type=F32)
        dv = lax.dot_general(p.astype(BF16), dov, TN, preferred_element_type=F32)

        @pl.when(i == 0)
        def _():
            dka[...] = dk
            dva[...] = dv

        @pl.when(i > 0)
        def _():
            dka[...] += dk
            dva[...] += dv

        @pl.when(i == nq - 1)
        def _():
            dk_ref[...] = dka[...].astype(BF16)
            dv_ref[...] = dva[...].astype(BF16)

    qspec = pl.BlockSpec((None, tq, CROSS_DIM), lambda b, h, i: (b, i, h))
    kspec = pl.BlockSpec((None, M, CROSS_DIM), lambda b, h, i: (b, 0, h))
    return pl.pallas_call(
        body, name=name, grid=(B, CROSS_HEADS, nq),
        in_specs=[qspec, kspec, pl.BlockSpec((None, M, CROSS_DIM), lambda b, h, i: (b, 0, CROSS_HEADS + h)), qspec],
        out_specs=[qspec, kspec, kspec],
        out_shape=[jax.ShapeDtypeStruct((B, S, D_CROSS), BF16), jax.ShapeDtypeStruct((B, M, D_CROSS), BF16),
                   jax.ShapeDtypeStruct((B, M, D_CROSS), BF16)],
        scratch_shapes=[pltpu.VMEM((M, CROSS_DIM), F32), pltpu.VMEM((M, CROSS_DIM), F32)],
        compiler_params=_params(dimension_semantics=("parallel", "parallel", "arbitrary")),
    )(qc, kv, kv, do)


def _local_step(x, mem, target, wts, small):
    B, S, D = x.shape
    T = B * S
    x2, t2 = x.reshape(T, D), target.reshape(T, D)
    mem2 = mem.reshape(-1, D)
    n_mem = mem.shape[1]

    hn1 = _rms_fwd("norm_ffn1", x2, small["ffn1_norm"])
    g1, u1, a1 = _ffn_up("ffn1_up", hn1, wts["ffn1_w_gate"], wts["ffn1_w_up"])
    h1 = _ffn_down("ffn1_down", a1, wts["ffn1_w_down"], x2)
    hn2 = _rms_fwd("norm_mix", h1, small["mix_norm"])
    z = _mm_nn_cols("mix_in", hn2, wts["w_in"], F32).reshape(B, S, -1)
    bias = _bias_tile(small["rel_bias"])
    y_attn = _attn_fwd("attn_fwd", z, bias)
    d_pool, y_pool = _pool_fwd("pool_fwd", z, wts["w_pool"], small["pool_scale"])
    cat = jnp.concatenate([y_attn, y_pool], axis=-1).reshape(T, -1)
    h2 = _mm_nn("mix_out", cat, wts["w_out"], F32, res=h1)
    hn3 = _rms_fwd("norm_cross", h2, small["cross_norm"])
    memn = _rms_fwd("norm_mem", mem2, small["mem_norm"])
    qc = _mm_nn("cross_q", hn3, wts["w_cq"], BF16)
    kv = _mm_nn("cross_kv", memn, wts["w_ckv"], BF16)
    o = _cross_fwd("cross_fwd", qc.reshape(B, S, -1), kv.reshape(B, n_mem, -1)).reshape(T, -1)
    h3 = _mm_nn_cols("cross_out", o, wts["w_co"], F32, res=h2)
    hn4 = _rms_fwd("norm_ffn2", h3, small["ffn2_norm"])
    g2, u2, a2 = _ffn_up("ffn2_up", hn4, wts["ffn2_w_gate"], wts["ffn2_w_up"])
    h4 = _ffn_down("ffn2_down", a2, wts["ffn2_w_down"], h3)

    gw, gs = {}, {}
    loss_part, dh4, dh4b, gs["final_norm"] = _loss_and_grad("loss", h4, small["final_norm"], t2)

    def ffn_bwd(tag, dh, dhb, h_in, hn, g, u, a, wg, wu, wd, gain):
        dg, du = _ffn_dact(tag + "_dact", dhb, wd, g, u)
        gw[tag + "_w_down"] = _ffn_dwd(tag + "_dwd", a, dhb)
        gw[tag + "_w_gate"] = _ffn_dwcol(tag + "_dwg", hn, dg)
        gw[tag + "_w_up"] = _ffn_dwcol(tag + "_dwu", hn, du)
        dhn = _ffn_dhn(tag + "_dhn", [(dg, wg), (du, wu)])
        return _rms_bwd(tag + "_dnorm", h_in, gain, dhn, skip=dh)

    dh3, dh3b, gs["ffn2_norm"] = ffn_bwd("ffn2", dh4, dh4b, h3, hn4, g2, u2, a2, wts["ffn2_w_gate"],
                                         wts["ffn2_w_up"], wts["ffn2_w_down"], small["ffn2_norm"])
    do = _mm_nt_cols("cross_do", [(dh3b, wts["w_co"])], BF16, tn=D_CROSS)
    gw["w_co"] = _mm_tn("cross_dwo", o, dh3b, tm=D_CROSS, col_blocks=N_DEV)
    dqc, dk, dv = _cross_bwd("cross_bwd", qc.reshape(B, S, -1), kv.reshape(B, n_mem, -1), do.reshape(B, S, -1))
    dqc = dqc.reshape(T, -1)
    dkv = jnp.concatenate([dk, dv], axis=-1).reshape(B * n_mem, -1)
    gw["w_cq"] = _mm_tn("cross_dwq", hn3, dqc, tn=D_CROSS)
    gw["w_ckv"] = _mm_tn("cross_dwkv", memn, dkv, tk=512)
    dhn3 = _mm_nt("cross_dhn", dqc, wts["w_cq"], F32)
    dmemn = _mm_nt("cross_dmem", dkv, wts["w_ckv"], F32, tm=512)
    _, _, gs["mem_norm"] = _rms_bwd("mem_dnorm", mem2, small["mem_norm"], dmemn)
    dh2, dh2b, gs["cross_norm"] = _rms_bwd("cross_dnorm", h2, small["cross_norm"], dhn3, skip=dh3)
    dcat = _mm_nt("mix_dcat", dh2b, wts["w_out"], BF16)
    gw["w_out"] = _mm_tn("mix_dwout", cat, dh2b)
    dcat3 = dcat.reshape(B, S, -1)
    dq, dkk, dvv, dbias = _attn_bwd("attn_bwd", z, bias, dcat3)
    du, dwp, gs["pool_scale"] = _pool_bwd("pool_bwd", d_pool, wts["w_pool"], small["pool_scale"], dcat3)
    gs["rel_bias"] = _rel_grad_from_diagonals(_bias_grad("bias_grad", dbias)[:, 0, :])
    gw["w_pool"] = dwp
    dz = jnp.concatenate([dq, dkk, dvv, du], axis=-1).reshape(T, -1)
    gw["w_in"] = _mm_tn("mix_dwin", hn2, dz, col_blocks=N_DEV)
    dhn2 = _mm_nt_cols("mix_dhn", [(dz, wts["w_in"])], F32)
    dh1, dh1b, gs["mix_norm"] = _rms_bwd("mix_dnorm", h1, small["mix_norm"], dhn2, skip=dh2)
    dx, _, gs["ffn1_norm"] = ffn_bwd("ffn1", dh1, dh1b, x2, hn1, g1, u1, a1, wts["ffn1_w_gate"],
                                     wts["ffn1_w_up"], wts["ffn1_w_down"], small["ffn1_norm"])
    return loss_part, dx.reshape(B, S, D), gw, gs


def _position():
    return lax.axis_index("x"), lax.axis_index("y"), lax.axis_index("c")


def _index(p):
    return 4 * p[0] + 2 * p[1] + p[2]


def _all_gather(name, shards):
    n = len(shards)

    def body(*refs):
        ins, outs = refs[:n], refs[n:2 * n]
        send, recv, local = refs[2 * n:]
        x, y, c = _position()
        me, sibling = (x, y, c), (x, y, 1 - c)
        chips = [(1 - x, y), (x, 1 - y), (1 - x, 1 - y)]

        def copy(a, k, block, to, src=None):
            dst = outs[a].at[_index(block)]
            return pltpu.make_async_remote_copy(src_ref=dst if src is None else src, dst_ref=dst,
                                                send_sem=send.at[a, k], recv_sem=recv.at[a, k],
                                                device_id=to, device_id_type=MESH)

        mine = [pltpu.make_async_copy(ins[a], outs[a].at[_index(me)], local.at[a]) for a in range(n)]
        for cp in mine:
            cp.start()
        first = []
        for a in range(n):
            first.append(copy(a, 0, me, sibling, src=ins[a]))
            first += [copy(a, 1 + j, me, (*chip, c), src=ins[a]) for j, chip in enumerate(chips)]
        for cp in first:
            cp.start()
        passed = []
        for j, chip in enumerate(chips):
            for a in range(n):
                copy(a, 1 + j, (*chip, c), me).wait_recv()
                fwd = copy(a, 4 + j, (*chip, c), sibling)
                fwd.start()
                passed.append(fwd)
        for a in range(n):
            copy(a, 0, sibling, me).wait_recv()
            for j, chip in enumerate(chips):
                copy(a, 4 + j, (*chip, 1 - c), me).wait_recv()
        for cp in first + passed:
            cp.wait_send()
        for cp in mine:
            cp.wait()

    anyspec = pl.BlockSpec(memory_space=pl.ANY)
    return pl.pallas_call(
        body, name=name, in_specs=[anyspec] * n, out_specs=[anyspec] * n,
        out_shape=[jax.ShapeDtypeStruct((N_DEV,) + s.shape, s.dtype) for s in shards],
        scratch_shapes=[pltpu.SemaphoreType.DMA((n, 7)), pltpu.SemaphoreType.DMA((n, 7)),
                        pltpu.SemaphoreType.DMA((n,))],
    )(*shards)


def _exchange_shards(name, stacks):
    n = len(stacks)

    def body(*refs):
        ins, outs = refs[:n], refs[n:2 * n]
        send, recv, local = refs[2 * n:]
        x, y, c = _position()
        me = _index((x, y, c))
        mine = [pltpu.make_async_copy(ins[a].at[me], outs[a].at[me], local.at[a]) for a in range(n)]
        for cp in mine:
            cp.start()
        copies = []
        for k in range(1, N_DEV):
            peer = (x ^ (k >> 2), y ^ ((k >> 1) & 1), c ^ (k & 1))
            for a in range(n):
                copies.append(pltpu.make_async_remote_copy(
                    src_ref=ins[a].at[_index(peer)], dst_ref=outs[a].at[me],
                    send_sem=send.at[a, k - 1], recv_sem=recv.at[a, k - 1], device_id=peer, device_id_type=MESH))
        for cp in copies:
            cp.start()
        for cp in copies:
            cp.wait()
        for cp in mine:
            cp.wait()

    anyspec = pl.BlockSpec(memory_space=pl.ANY)
    return pl.pallas_call(
        body, name=name, in_specs=[anyspec] * n, out_specs=[anyspec] * n,
        out_shape=[jax.ShapeDtypeStruct(s.shape, s.dtype) for s in stacks],
        scratch_shapes=[pltpu.SemaphoreType.DMA((n, 7)), pltpu.SemaphoreType.DMA((n, 7)),
                        pltpu.SemaphoreType.DMA((n,))],
    )(*stacks)


def _adamw_math(w, g, m, v):
    m = ADAM_B1 * m + (1.0 - ADAM_B1) * g
    v = ADAM_B2 * v + (1.0 - ADAM_B2) * (g * g)
    m_hat = m / (1.0 - ADAM_B1 ** ADAM_STEP)
    v_hat = v / (1.0 - ADAM_B2 ** ADAM_STEP)
    delta = -ADAM_LR * (m_hat / (jnp.sqrt(v_hat) + ADAM_EPS) + ADAM_WD * w)
    return delta, m, v


def _adamw(name, parts, w, m, v, tr=128):
    R, C = w.shape
    tr = _tile(R, tr)

    def body(p_ref, w_ref, m_ref, v_ref, g_out, d_out, m_out, v_out):
        g = p_ref[0].astype(F32)
        for d in range(1, N_DEV):
            g = g + p_ref[d].astype(F32)
        g_out[...] = g
        d_out[...], m_out[...], v_out[...] = _adamw_math(w_ref[...], g, m_ref[...], v_ref[...])

    row = pl.BlockSpec((tr, C), lambda i: (i, 0))
    out = jax.ShapeDtypeStruct((R, C), F32)
    return pl.pallas_call(
        body, name=name, grid=(R // tr,),
        in_specs=[pl.BlockSpec((N_DEV, tr, C), lambda i: (0, i, 0)), row, row, row],
        out_specs=[row] * 4, out_shape=[out] * 4,
        compiler_params=_params(dimension_semantics=("parallel",)),
    )(parts, w, m, v)


def _small_allreduce_adamw(name, g, w, m, v):
    R = g.shape[0]

    def body(g_ref, w_ref, m_ref, v_ref, g_out, d_out, m_out, v_out, land, send, recv):
        x, y, c = _position()
        me = _index((x, y, c))
        land[me] = g_ref[...]
        copies = []
        for k in range(1, N_DEV):
            peer = (x ^ (k >> 2), y ^ ((k >> 1) & 1), c ^ (k & 1))
            copies.append(pltpu.make_async_remote_copy(
                src_ref=g_ref, dst_ref=land.at[me], send_sem=send.at[k - 1], recv_sem=recv.at[k - 1],
                device_id=peer, device_id_type=MESH))
        for cp in copies:
            cp.start()
        for cp in copies:
            cp.wait()
        total = land[0]
        for d in range(1, N_DEV):
            total = total + land[d]
        g_out[...] = total
        d_out[...], m_out[...], v_out[...] = _adamw_math(w_ref[...], total, m_ref[...], v_ref[...])

    vm = pl.BlockSpec(memory_space=pltpu.VMEM)
    out = jax.ShapeDtypeStruct((R, 128), F32)
    return pl.pallas_call(
        body, name=name, in_specs=[vm] * 4, out_specs=[vm] * 4, out_shape=[out] * 4,
        scratch_shapes=[pltpu.VMEM((N_DEV, R, 128), F32), pltpu.SemaphoreType.DMA((7,)),
                        pltpu.SemaphoreType.DMA((7,))],
    )(g, w, m, v)


BIG = ("ffn1_w_gate", "ffn1_w_up", "ffn1_w_down", "w_in", "w_pool", "w_out", "w_cq", "w_ckv", "w_co",
       "ffn2_w_gate", "ffn2_w_up", "ffn2_w_down")
SMALL = ("ffn1_norm", "mix_norm", "rel_bias", "pool_scale", "cross_norm", "mem_norm", "ffn2_norm", "final_norm")
ORDER = ("ffn1_norm", "ffn1_w_gate", "ffn1_w_up", "ffn1_w_down", "mix_norm", "w_in", "rel_bias", "w_pool",
         "pool_scale", "w_out", "cross_norm", "mem_norm", "w_cq", "w_ckv", "w_co", "ffn2_norm", "ffn2_w_gate",
         "ffn2_w_up", "ffn2_w_down", "final_norm")
ROW_SHARDED = ("ffn1_w_down", "ffn2_w_down", "w_out", "w_cq", "w_ckv")
GATHER_GROUPS = (("ffn1_w_gate", "ffn1_w_up", "ffn1_w_down"), ("w_in", "w_pool", "w_out"),
                 ("w_cq", "w_ckv", "w_co"), ("ffn2_w_gate", "ffn2_w_up", "ffn2_w_down"))


def _pack(arrays):
    flat = jnp.concatenate([a.reshape(-1) for a in arrays])
    rows = -(-flat.shape[0] // 1024) * 8
    return jnp.pad(flat, (0, rows * 128 - flat.shape[0])).reshape(rows, 128)


def _unpack(packed, like):
    flat, out, at = packed.reshape(-1), [], 0
    for a in like:
        out.append(flat[at:at + a.size].reshape(a.shape))
        at += a.size
    return out


def _shard2d(a):
    a = a[0]
    return a.reshape(-1, a.shape[-1])


def kernel(x, mem, ffn1_norm, ffn1_w_gate, ffn1_w_up, ffn1_w_down, mix_norm, w_in, rel_bias, w_pool, pool_scale, w_out, cross_norm, mem_norm, w_cq, w_ckv, w_co, ffn2_norm, ffn2_w_gate, ffn2_w_up, ffn2_w_down, final_norm, loss_target, m_ffn1_norm, m_ffn1_w_gate, m_ffn1_w_up, m_ffn1_w_down, m_mix_norm, m_w_in, m_rel_bias, m_w_pool, m_pool_scale, m_w_out, m_cross_norm, m_mem_norm, m_w_cq, m_w_ckv, m_w_co, m_ffn2_norm, m_ffn2_w_gate, m_ffn2_w_up, m_ffn2_w_down, m_final_norm, v_ffn1_norm, v_ffn1_w_gate, v_ffn1_w_up, v_ffn1_w_down, v_mix_norm, v_w_in, v_rel_bias, v_w_pool, v_pool_scale, v_w_out, v_cross_norm, v_mem_norm, v_w_cq, v_w_ckv, v_w_co, v_ffn2_norm, v_ffn2_w_gate, v_ffn2_w_up, v_ffn2_w_down, v_final_norm):
    args = dict(locals())
    w_in_ = {n: args[n] for n in ORDER}
    m_in = {n: args["m_" + n] for n in ORDER}
    v_in = {n: args["v_" + n] for n in ORDER}

    wts = {}
    for group in GATHER_GROUPS:
        got = _all_gather("gather_" + group[0], [_shard2d(w_in_[n]).astype(BF16) for n in group])
        for n, full in zip(group, got):
            wts[n] = full.reshape(-1, full.shape[-1]) if n in ROW_SHARDED else full
    n_g, rows = len(POOL_WINDOWS), POOL_GROUP // N_DEV
    wts["w_pool"] = wts["w_pool"].reshape(N_DEV, n_g, rows, POOL_GROUP).transpose(1, 0, 2, 3).reshape(n_g, POOL_GROUP, POOL_GROUP)

    small = {n: w_in_[n].reshape(1, -1) for n in SMALL if n != "rel_bias"}
    small["rel_bias"] = rel_bias[0]
    loss_part, grad_x, gw, gs = _local_step(x, mem, loss_target, wts, small)
    loss = lax.psum(loss_part[0, 0], ("x", "y", "c"))

    gw["w_pool"] = gw["w_pool"].reshape(n_g, N_DEV, rows, POOL_GROUP).transpose(1, 0, 2, 3).astype(BF16)
    stacks = []
    for n in BIG:
        shard = _shard2d(w_in_[n])
        stacks.append(gw[n].reshape((N_DEV,) + shard.shape))
    landed = _exchange_shards("exchange_grads", stacks)
    grad, delta, new_m, new_v = {}, {}, {}, {}
    for n, parts in zip(BIG, landed):
        res = _adamw("adamw_" + n, parts, _shard2d(w_in_[n]), _shard2d(m_in[n]), _shard2d(v_in[n]))
        grad[n], delta[n], new_m[n], new_v[n] = [r.reshape(w_in_[n].shape) for r in res]

    like = [w_in_[n] for n in SMALL]
    gs["rel_bias"] = gs["rel_bias"].reshape(rel_bias.shape)
    res = _small_allreduce_adamw("small_params", _pack([gs[n] for n in SMALL]), _pack(like),
                                 _pack([m_in[n] for n in SMALL]), _pack([v_in[n] for n in SMALL]))
    for d, packed in zip((grad, delta, new_m, new_v), res):
        for n, a in zip(SMALL, _unpack(packed, like)):
            d[n] = a
    return (loss, grad_x, *[grad[n] for n in ORDER], *[delta[n] for n in ORDER],
            *[new_m[n] for n in ORDER], *[new_v[n] for n in ORDER])
```

```python
import functools

import jax
import jax.numpy as jnp
from jax import lax
from jax.experimental import pallas as pl
from jax.experimental.pallas import tpu as pltpu

F32 = jnp.float32
BF16 = jnp.bfloat16

N_DEV = 8
EPS = 1e-6
NEG_INF = -1e30
CHUNK = 64
LEFT_CHUNKS = 8
PAD = LEFT_CHUNKS * CHUNK
QBLK = 2 * CHUNK
KBAND = PAD + QBLK
REL_CLIP = 128
ATTN_HEADS = 16
HEAD_DIM = 64
D_ATTN = ATTN_HEADS * HEAD_DIM
POOL_WINDOWS = (2, 4, 8, 16)
POOL_GROUP = 256
D_POOL = len(POOL_WINDOWS) * POOL_GROUP
CROSS_HEADS = 4
CROSS_DIM = 128
D_CROSS = CROSS_HEADS * CROSS_DIM
FFN_RES = 0.5
ADAM_LR, ADAM_B1, ADAM_B2, ADAM_EPS, ADAM_WD, ADAM_STEP = 0.001, 0.9, 0.999, 1e-08, 0.01, 10

NN = (((1,), (0,)), ((), ()))
NT = (((1,), (1,)), ((), ()))
TN = (((0,), (0,)), ((), ()))
MESH = pl.DeviceIdType.MESH
VMEM_LIMIT = 56 * 1024 * 1024


def _params(**kw):
    return pltpu.CompilerParams(vmem_limit_bytes=VMEM_LIMIT, **kw)


def _bf(v):
    return v if v.dtype == BF16 else v.astype(BF16)


def _gemm(name, pairs, dims, grid, outs, extras=(), epilogue=None, acc_shape=None, after=None):
    nk, npairs, nex, nout = grid[2], len(pairs), len(extras), len(outs)
    first_out = 2 * npairs + nex + (after is not None)

    def body(*refs):
        ab = refs[:2 * npairs]
        ex = refs[2 * npairs:2 * npairs + nex]
        out = refs[first_out:first_out + nout]
        part = None
        for p in range(npairs):
            d = lax.dot_general(_bf(ab[2 * p][...]), _bf(ab[2 * p + 1][...]), dims, preferred_element_type=F32)
            part = d if part is None else part + d

        def finish(total):
            vals = epilogue(total, *[e[...] for e in ex]) if epilogue is not None else (total,)
            for r, v in zip(out, vals):
                r[...] = v.astype(r.dtype)

        if nk == 1:
            finish(part)
        else:
            acc = refs[-1]
            k = pl.program_id(2)

            @pl.when(k == 0)
            def _():
                acc[...] = part

            @pl.when(k > 0)
            def _():
                acc[...] += part

            @pl.when(k == nk - 1)
            def _():
                finish(acc[...])

    operands, in_specs = [], []
    for a, ablk, amap, b, bblk, bmap in pairs:
        operands += [a, b]
        in_specs += [pl.BlockSpec(ablk, amap), pl.BlockSpec(bblk, bmap)]
    for x, xblk, xmap in extras:
        operands.append(x)
        in_specs.append(pl.BlockSpec(xblk, xmap))
    if after is not None:
        operands.append(after)
        in_specs.append(pl.BlockSpec(after.shape, lambda i, j, k: (0, 0)))
    res = pl.pallas_call(
        body, name=name, grid=grid, in_specs=in_specs,
        out_specs=[pl.BlockSpec(blk, m) for _, _, blk, m in outs],
        out_shape=[jax.ShapeDtypeStruct(s, d) for s, d, _, _ in outs],
        scratch_shapes=[pltpu.VMEM(acc_shape, F32)] if nk > 1 else [],
        compiler_params=_params(dimension_semantics=("parallel", "parallel", "arbitrary")),
    )(*operands)
    return res[0] if nout == 1 else res


def _tile(n, want):
    for t in range(min(n, want), 15, -1):
        if n % t == 0 and t % 16 == 0:
            return t
    return n


def _mm_nn(name, a, b, out_dtype, res=None, tm=1024, tn=1024):
    M, K = a.shape
    N = b.shape[1]
    tm, tn = _tile(M, tm), _tile(N, tn)
    extras = [] if res is None else [(res, (tm, tn), lambda i, j, k: (i, j))]
    epi = None if res is None else (lambda t, r: (r + t,))
    return _gemm(name, [(a, (tm, K), lambda i, j, k: (i, 0), b, (K, tn), lambda i, j, k: (0, j))], NN,
                 (M // tm, N // tn, 1), [((M, N), out_dtype, (tm, tn), lambda i, j, k: (i, j))], extras, epi)


def _mm_nn_cols(name, a, bs, out_dtype, res=None, tm=1024):
    M, K = a.shape
    nb, _, w = bs.shape
    tm = _tile(M, tm)
    extras = [] if res is None else [(res, (tm, w), lambda i, j, k: (i, j))]
    epi = None if res is None else (lambda t, r: (r + t,))
    return _gemm(name, [(a, (tm, K), lambda i, j, k: (i, 0), bs, (None, K, w), lambda i, j, k: (j, 0, 0))], NN,
                 (M // tm, nb, 1), [((M, nb * w), out_dtype, (tm, w), lambda i, j, k: (i, j))], extras, epi)


def _mm_nt(name, a, b, out_dtype, tm=1024, tn=1024, after=None):
    M, K = a.shape
    N = b.shape[0]
    tm, tn = _tile(M, tm), _tile(N, tn)
    return _gemm(name, [(a, (tm, K), lambda i, j, k: (i, 0), b, (tn, K), lambda i, j, k: (j, 0))], NT,
                 (M // tm, N // tn, 1), [((M, N), out_dtype, (tm, tn), lambda i, j, k: (i, j))], after=after)


def _mm_nt_cols(name, a, bs, out_dtype, tm=1024, tn=1024, after=None):
    M = a.shape[0]
    nb, N, w = bs.shape
    tm, tn = _tile(M, tm), _tile(N, tn)
    return _gemm(name, [(a, (tm, w), lambda i, j, k: (i, k), bs, (None, tn, w), lambda i, j, k: (k, j, 0))], NT,
                 (M // tm, N // tn, nb), [((M, N), out_dtype, (tm, tn), lambda i, j, k: (i, j))],
                 acc_shape=(tm, tn), after=after)


def _mm_tn(name, a, b, scale=1.0, tm=2048, tn=1024, tk=1024, col_blocks=None):
    T, Ka = a.shape
    Nb = b.shape[1]
    tk = _tile(T, tk)
    tm = _tile(Ka, tm)
    epi = None if scale == 1.0 else (lambda t: (t * scale,))
    if col_blocks is None:
        tn = _tile(Nb, tn)
        out = ((Ka, Nb), BF16, (tm, tn), lambda i, j, k: (i, j))
        nj = Nb // tn
    else:
        tn = Nb // col_blocks
        out = ((col_blocks, Ka, tn), BF16, (None, tm, tn), lambda i, j, k: (j, i, 0))
        nj = col_blocks
    return _gemm(name, [(a, (tk, tm), lambda i, j, k: (k, i), b, (tk, tn), lambda i, j, k: (k, j))], TN,
                 (Ka // tm, nj, T // tk), [out], (), epi, (tm, tn))


def _ffn_down(name, a, wd, res, tm=1024, tn=1024):
    nb, M, w = a.shape
    N = wd.shape[1]
    tm, tn = _tile(M, tm), _tile(N, tn)
    return _gemm(name, [(a, (None, tm, w), lambda i, j, k: (k, i, 0), wd, (w, tn), lambda i, j, k: (k, j))], NN,
                 (M // tm, N // tn, nb), [((M, N), F32, (tm, tn), lambda i, j, k: (i, j))],
                 [(res, (tm, tn), lambda i, j, k: (i, j))], lambda t, r: (r + FFN_RES * t,), (tm, tn))


def _ffn_dact(name, dhb, wd, g, u, tm=1024):
    M, K = dhb.shape
    nb, _, w = g.shape
    tm = _tile(M, tm)
    hid = ((None, tm, w), lambda i, j, k: (j, i, 0))
    return _gemm(name, [(dhb, (tm, K), lambda i, j, k: (i, 0), wd, (w, K), lambda i, j, k: (j, 0))], NT,
                 (M // tm, nb, 1), [((nb, M, w), BF16) + hid] * 2, [(g,) + hid, (u,) + hid], _swiglu_bwd)


def _ffn_dwd(name, a, dhb, tk=1024, after=None):
    nb, T, w = a.shape
    N = dhb.shape[1]
    tk = _tile(T, tk)
    return _gemm(name, [(a, (None, tk, w), lambda i, j, k: (i, k, 0), dhb, (tk, N), lambda i, j, k: (k, 0))], TN,
                 (nb, 1, T // tk), [((nb * w, N), BF16, (w, N), lambda i, j, k: (i, 0))], (),
                 lambda t: (t * FFN_RES,), (w, N), after)


def _ffn_dwcol(name, hn, dg, tk=1024, after=None):
    T, K = hn.shape
    nb, _, w = dg.shape
    tk = _tile(T, tk)
    return _gemm(name, [(hn, (tk, K), lambda i, j, k: (k, 0), dg, (None, tk, w), lambda i, j, k: (j, k, 0))], TN,
                 (1, nb, T // tk), [((nb, K, w), BF16, (None, K, w), lambda i, j, k: (j, 0, 0))], (), None, (K, w), after)


def _ffn_dhn(name, pairs, tm=1024, tn=1024, after=None):
    nb, M, w = pairs[0][0].shape
    N = pairs[0][1].shape[1]
    tm, tn = _tile(M, tm), _tile(N, tn)
    gp = [(d, (None, tm, w), lambda i, j, k: (k, i, 0), ws, (None, tn, w), lambda i, j, k: (k, j, 0))
          for d, ws in pairs]
    return _gemm(name, gp, NT, (M // tm, N // tn, nb), [((M, N), F32, (tm, tn), lambda i, j, k: (i, j))],
                 acc_shape=(tm, tn), after=after)


def _ffn_up(name, hn, wg, wu, tm=1024):
    M, K = hn.shape
    nb, _, w = wg.shape
    tm = _tile(M, tm)

    def body(a_ref, g_ref, u_ref, og, ou, oa):
        a = a_ref[...]
        g = jnp.dot(a, g_ref[...], preferred_element_type=F32)
        u = jnp.dot(a, u_ref[...], preferred_element_type=F32)
        og[...] = g.astype(BF16)
        ou[...] = u.astype(BF16)
        oa[...] = (g * jax.nn.sigmoid(g) * u).astype(BF16)

    wspec = pl.BlockSpec((None, K, w), lambda i, j: (j, 0, 0))
    ospec = pl.BlockSpec((None, tm, w), lambda i, j: (j, i, 0))
    return pl.pallas_call(
        body, name=name, grid=(M // tm, nb),
        in_specs=[pl.BlockSpec((tm, K), lambda i, j: (i, 0)), wspec, wspec],
        out_specs=[ospec] * 3, out_shape=[jax.ShapeDtypeStruct((nb, M, w), BF16)] * 3,
        compiler_params=_params(dimension_semantics=("parallel", "parallel")),
    )(hn, wg, wu)


def _swiglu_bwd(dact, g, u):
    g = g.astype(F32)
    u = u.astype(F32)
    sig = jax.nn.sigmoid(g)
    silu = g * sig
    d = FFN_RES * dact
    return d * u * (sig * (1.0 + g * (1.0 - sig))), d * silu


def _rms_fwd(name, x, gain, tr=512, after=None):
    R, D = x.shape
    tr = _tile(R, tr)

    def body(x_ref, g_ref, *rest):
        xv = x_ref[...]
        y = xv * lax.rsqrt(jnp.mean(xv * xv, axis=-1, keepdims=True) + EPS)
        rest[-1][...] = (y * g_ref[...]).astype(BF16)

    tokens = [] if after is None else [after]
    return pl.pallas_call(
        body, name=name, grid=(R // tr,),
        in_specs=[pl.BlockSpec((tr, D), lambda i: (i, 0)), pl.BlockSpec((1, D), lambda i: (0, 0))]
        + [pl.BlockSpec(t.shape, lambda i: (0, 0)) for t in tokens],
        out_specs=pl.BlockSpec((tr, D), lambda i: (i, 0)), out_shape=jax.ShapeDtypeStruct((R, D), BF16),
        compiler_params=_params(dimension_semantics=("parallel",)),
    )(x, gain, *tokens)


def _rms_bwd_math(xv, gain, dy):
    rstd = lax.rsqrt(jnp.mean(xv * xv, axis=-1, keepdims=True) + EPS)
    xhat = xv * rstd
    dxh = dy * gain
    dx = rstd * (dxh - xhat * jnp.mean(dxh * xhat, axis=-1, keepdims=True))
    return dx, jnp.sum(dy * xhat, axis=0, keepdims=True)


def _rms_bwd(name, x, gain, dy, skip=None, tr=256):
    R, D = x.shape
    tr = _tile(R, tr)
    has_skip = skip is not None

    def body(*refs):
        x_ref, g_ref, dy_ref = refs[:3]
        dx_ref, dxb_ref, dg_ref = refs[-3:]
        dx, dg = _rms_bwd_math(x_ref[...], g_ref[...], dy_ref[...].astype(F32))
        if has_skip:
            dx = dx + refs[3][...]
        dx_ref[...] = dx
        dxb_ref[...] = dx.astype(BF16)

        @pl.when(pl.program_id(0) == 0)
        def _():
            dg_ref[...] = dg

        @pl.when(pl.program_id(0) > 0)
        def _():
            dg_ref[...] += dg

    row = pl.BlockSpec((tr, D), lambda i: (i, 0))
    vec = pl.BlockSpec((1, D), lambda i: (0, 0))
    return pl.pallas_call(
        body, name=name, grid=(R // tr,),
        in_specs=[row, vec, row] + ([row] if has_skip else []),
        out_specs=[row, row, vec],
        out_shape=[jax.ShapeDtypeStruct((R, D), F32), jax.ShapeDtypeStruct((R, D), BF16),
                   jax.ShapeDtypeStruct((1, D), F32)],
        compiler_params=_params(dimension_semantics=("arbitrary",)),
    )(*([x, gain, dy] + ([skip] if has_skip else [])))


def _loss_and_grad(name, h, gain, target, tr=256):
    R, D = h.shape
    tr = _tile(R, tr)

    def body(h_ref, g_ref, t_ref, loss_ref, dh_ref, dhb_ref, dg_ref):
        hv, gain_v = h_ref[...], g_ref[...]
        y = (hv * lax.rsqrt(jnp.mean(hv * hv, axis=-1, keepdims=True) + EPS)) * gain_v
        err = y - t_ref[...]
        part = jnp.full((8, 128), 0.5 * jnp.sum(jnp.mean(err * err, axis=-1, keepdims=True)), F32)
        dh, dg = _rms_bwd_math(hv, gain_v, err * (1.0 / D))
        dh_ref[...] = dh
        dhb_ref[...] = dh.astype(BF16)

        @pl.when(pl.program_id(0) == 0)
        def _():
            dg_ref[...] = dg
            loss_ref[...] = part

        @pl.when(pl.program_id(0) > 0)
        def _():
            dg_ref[...] += dg
            loss_ref[...] += part

    row = pl.BlockSpec((tr, D), lambda i: (i, 0))
    vec = pl.BlockSpec((1, D), lambda i: (0, 0))
    return pl.pallas_call(
        body, name=name, grid=(R // tr,), in_specs=[row, vec, row],
        out_specs=[pl.BlockSpec((8, 128), lambda i: (0, 0)), row, row, vec],
        out_shape=[jax.ShapeDtypeStruct((8, 128), F32), jax.ShapeDtypeStruct((R, D), F32),
                   jax.ShapeDtypeStruct((R, D), BF16), jax.ShapeDtypeStruct((1, D), F32)],
        compiler_params=_params(dimension_semantics=("arbitrary",)),
    )(h, gain, target)


def _bias_tile(name, rel):
    width = KBAND + QBLK
    sat = rel[:, 2 * REL_CLIP:]
    n_left = PAD - REL_CLIP + 1
    row0 = jnp.concatenate([jnp.broadcast_to(sat, (ATTN_HEADS, n_left)), rel[:, :2 * REL_CLIP][:, ::-1],
                            jnp.broadcast_to(sat, (ATTN_HEADS, width - n_left - 2 * REL_CLIP))], axis=1)

    def body(e_ref, o_ref):
        rows = pltpu.roll(jnp.broadcast_to(e_ref[...], (QBLK, width)), 0, 1, stride=1, stride_axis=0)
        i = lax.broadcasted_iota(jnp.int32, (QBLK, KBAND), 0) // CHUNK
        j = lax.broadcasted_iota(jnp.int32, (QBLK, KBAND), 1) // CHUNK
        o_ref[...] = jnp.where((j >= i) & (j <= i + LEFT_CHUNKS), rows[:, :KBAND], NEG_INF)

    return pl.pallas_call(
        body, name=name, grid=(ATTN_HEADS,),
        in_specs=[pl.BlockSpec((None, 1, width), lambda h: (h, 0, 0))],
        out_specs=pl.BlockSpec((None, QBLK, KBAND), lambda h: (h, 0, 0)),
        out_shape=jax.ShapeDtypeStruct((ATTN_HEADS, QBLK, KBAND), F32),
        compiler_params=_params(dimension_semantics=("parallel",)),
    )(row0.reshape(ATTN_HEADS, 1, width))


def _band_softmax(q_half, kb, bias, kpos):
    s = lax.dot_general(q_half, kb, NT, preferred_element_type=F32) * (HEAD_DIM ** -0.5) + bias
    s = jnp.where(kpos >= 0, s, NEG_INF)
    e = jnp.exp(s - jnp.max(s, axis=-1, keepdims=True))
    return e / jnp.sum(e, axis=-1, keepdims=True)


def _fill_padded(dst, src, S):
    dst[pl.ds(0, PAD), :] = jnp.zeros((PAD, dst.shape[1]), dst.dtype)
    dst[pl.ds(PAD, S), :] = src[...].astype(dst.dtype)


def _attn_fwd(name, z, bias):
    B, S, _ = z.shape
    nh2 = ATTN_HEADS // 2

    def body(q_ref, k_ref, v_ref, b_ref, o_ref, kp, vp):
        qb = pl.program_id(2)

        @pl.when(qb == 0)
        def _():
            _fill_padded(kp, k_ref, S)
            _fill_padded(vp, v_ref, S)

        start = pl.multiple_of(qb * QBLK, QBLK)
        kb, vb = kp[pl.ds(start, KBAND), :], vp[pl.ds(start, KBAND), :]
        q = q_ref[...].astype(BF16)
        first = lax.broadcasted_iota(jnp.int32, (QBLK, 2 * HEAD_DIM), 1) < HEAD_DIM
        kpos = qb * QBLK - PAD + lax.broadcasted_iota(jnp.int32, (1, KBAND), 1)
        o = []
        for a in range(2):
            p = _band_softmax(jnp.where(first == (a == 0), q, jnp.zeros_like(q)), kb, b_ref[a], kpos)
            o.append(jnp.dot(p.astype(BF16), vb, preferred_element_type=F32))
        o_ref[...] = jnp.where(first, o[0], o[1]).astype(BF16)

    return pl.pallas_call(
        body, name=name, grid=(B, nh2, S // QBLK),
        in_specs=[pl.BlockSpec((None, QBLK, 128), lambda b, h, i: (b, i, h)),
                  pl.BlockSpec((None, S, 128), lambda b, h, i: (b, 0, nh2 + h)),
                  pl.BlockSpec((None, S, 128), lambda b, h, i: (b, 0, 2 * nh2 + h)),
                  pl.BlockSpec((2, QBLK, KBAND), lambda b, h, i: (h, 0, 0))],
        out_specs=pl.BlockSpec((None, QBLK, 128), lambda b, h, i: (b, i, h)),
        out_shape=jax.ShapeDtypeStruct((B, S, D_ATTN), BF16),
        scratch_shapes=[pltpu.VMEM((PAD + S, 128), BF16), pltpu.VMEM((PAD + S, 128), BF16)],
        compiler_params=_params(dimension_semantics=("parallel", "parallel", "arbitrary")),
    )(z, z, z, bias)


def _attn_bwd(name, z, bias, dcat):
    B, S, _ = z.shape
    nh2 = ATTN_HEADS // 2
    nqb = S // QBLK
    scale = HEAD_DIM ** -0.5

    def body(q_ref, k_ref, v_ref, b_ref, do_ref, dq_ref, dk_ref, dv_ref, db_ref, kp, vp, dka, dva):
        b, qb = pl.program_id(1), pl.program_id(2)

        @pl.when(qb == 0)
        def _():
            _fill_padded(kp, k_ref, S)
            _fill_padded(vp, v_ref, S)
            dka[...] = jnp.zeros_like(dka)
            dva[...] = jnp.zeros_like(dva)

        @pl.when((qb == 0) & (b == 0))
        def _():
            db_ref[...] = jnp.zeros_like(db_ref)

        start = pl.multiple_of(qb * QBLK, QBLK)
        band = pl.ds(start, KBAND)
        kb, vb = kp[band, :], vp[band, :]
        q = q_ref[...].astype(BF16)
        do = do_ref[...]
        first = lax.broadcasted_iota(jnp.int32, (QBLK, 2 * HEAD_DIM), 1) < HEAD_DIM
        kpos = qb * QBLK - PAD + lax.broadcasted_iota(jnp.int32, (1, KBAND), 1)
        dq, dk, dv = [], None, None
        for a in range(2):
            mine = first == (a == 0)
            qa = jnp.where(mine, q, jnp.zeros_like(q))
            doa = jnp.where(mine, do, jnp.zeros_like(do))
            p = _band_softmax(qa, kb, b_ref[a], kpos)
            dp = lax.dot_general(doa, vb, NT, preferred_element_type=F32)
            ds = p * (dp - jnp.sum(p * dp, axis=-1, keepdims=True))
            db_ref[a] += ds
            dsb = (ds * scale).astype(BF16)
            dq.append(jnp.dot(dsb, kb, preferred_element_type=F32))
            dka_part = lax.dot_general(dsb, qa, TN, preferred_element_type=F32)
            dva_part = lax.dot_general(p.astype(BF16), doa, TN, preferred_element_type=F32)
            dk = dka_part if dk is None else dk + dka_part
            dv = dva_part if dv is None else dv + dva_part
        dq_ref[...] = jnp.where(first, dq[0], dq[1]).astype(BF16)
        dka[band, :] += dk
        dva[band, :] += dv

        @pl.when(qb == nqb - 1)
        def _():
            dk_ref[...] = dka[pl.ds(PAD, S), :].astype(BF16)
            dv_ref[...] = dva[pl.ds(PAD, S), :].astype(BF16)

    qspec = pl.BlockSpec((None, QBLK, 128), lambda h, b, i: (b, i, h))
    kvout = pl.BlockSpec((None, S, 128), lambda h, b, i: (b, 0, h))
    bspec = pl.BlockSpec((2, QBLK, KBAND), lambda h, b, i: (h, 0, 0))
    act = jax.ShapeDtypeStruct((B, S, D_ATTN), BF16)
    return pl.pallas_call(
        body, name=name, grid=(nh2, B, nqb),
        in_specs=[qspec,
                  pl.BlockSpec((None, S, 128), lambda h, b, i: (b, 0, nh2 + h)),
                  pl.BlockSpec((None, S, 128), lambda h, b, i: (b, 0, 2 * nh2 + h)),
                  bspec, qspec],
        out_specs=[qspec, kvout, kvout, bspec],
        out_shape=[act, act, act, jax.ShapeDtypeStruct((ATTN_HEADS, QBLK, KBAND), F32)],
        scratch_shapes=[pltpu.VMEM((PAD + S, 128), BF16), pltpu.VMEM((PAD + S, 128), BF16),
                        pltpu.VMEM((PAD + S, 128), F32), pltpu.VMEM((PAD + S, 128), F32)],
        compiler_params=_params(dimension_semantics=("arbitrary", "arbitrary", "arbitrary")),
    )(z, z, z, bias, dcat)


def _bias_grad(name, dbias):
    width = KBAND + QBLK

    def body(d_ref, o_ref):
        acc = jnp.zeros((1, width), F32)
        for i in range(QBLK):
            row = jnp.concatenate([d_ref[pl.ds(i, 1), :], jnp.zeros((1, QBLK), F32)], axis=1)
            shift = QBLK - 1 - i
            acc = acc + (pltpu.roll(row, shift, 1) if shift else row)
        o_ref[...] = acc

    return pl.pallas_call(
        body, name=name, grid=(ATTN_HEADS,),
        in_specs=[pl.BlockSpec((None, QBLK, KBAND), lambda h: (h, 0, 0))],
        out_specs=pl.BlockSpec((None, 1, width), lambda h: (h, 0, 0)),
        out_shape=jax.ShapeDtypeStruct((ATTN_HEADS, 1, width), F32),
        compiler_params=_params(dimension_semantics=("parallel",)),
    )(dbias)


def _rel_grad_from_diagonals(diag):
    top = PAD + QBLK - 1 - REL_CLIP
    sat = jnp.sum(diag[:, :top + 1], axis=1, keepdims=True)
    mid = diag[:, top + 1:top + 2 * REL_CLIP][:, ::-1]
    return jnp.concatenate([jnp.zeros_like(sat), mid, sat], axis=1)


def _shift_rows(x, k, forward):
    S = x.shape[0]
    t = lax.broadcasted_iota(jnp.int32, x.shape, 0)
    if forward:
        return jnp.where(t < S - k, pltpu.roll(x, S - k, 0), 0.0)
    return jnp.where(t >= k, pltpu.roll(x, k, 0), 0.0)


def _window_sum(x, g, forward):
    s = x + _shift_rows(x, 1, forward)
    out = s
    for n, k in enumerate((2, 4, 8)):
        s = s + _shift_rows(s, k, forward)
        out = jnp.where(g > n, s, out)
    return out


def _pool_count(S, g):
    t = lax.broadcasted_iota(jnp.int32, (S, 1), 0)
    w = jnp.left_shift(2, g)
    return jnp.minimum(t + 1, w).astype(F32)


def _pool_fwd(name, z, wp, pscale):
    B, S, _ = z.shape
    c0 = 3 * D_ATTN // POOL_GROUP

    def body(u_ref, w_ref, s_ref, d_ref, y_ref):
        g = pl.program_id(1)
        u = u_ref[...]
        d = (_window_sum(u, g, False) / _pool_count(S, g) - u).astype(BF16)
        d_ref[...] = d
        y_ref[...] = (jnp.dot(d, w_ref[...], preferred_element_type=F32) * s_ref[...]).astype(BF16)

    blk = pl.BlockSpec((None, S, POOL_GROUP), lambda b, g: (b, 0, g))
    out = jax.ShapeDtypeStruct((B, S, D_POOL), BF16)
    return pl.pallas_call(
        body, name=name, grid=(B, len(POOL_WINDOWS)),
        in_specs=[pl.BlockSpec((None, S, POOL_GROUP), lambda b, g: (b, 0, c0 + g)),
                  pl.BlockSpec((None, POOL_GROUP, POOL_GROUP), lambda b, g: (g, 0, 0)),
                  pl.BlockSpec((1, POOL_GROUP), lambda b, g: (0, g))],
        out_specs=[blk, blk], out_shape=[out, out],
        compiler_params=_params(dimension_semantics=("parallel", "parallel")),
    )(z, wp, pscale)


def _pool_bwd(name, d, wp, pscale, dcat):
    B, S, _ = d.shape
    c0 = D_ATTN // POOL_GROUP

    def body(d_ref, w_ref, s_ref, dy_ref, du_ref, dw_ref, dsc_ref):
        g, b = pl.program_id(0), pl.program_id(1)
        dv = d_ref[...]
        dy = dy_ref[...].astype(F32)
        w = w_ref[...]
        ypre = jnp.dot(dv, w, preferred_element_type=F32)
        dyp = (dy * s_ref[...]).astype(BF16)
        dd = lax.dot_general(dyp, w, NT, preferred_element_type=F32)
        du_ref[...] = (_window_sum(dd / _pool_count(S, g), g, True) - dd).astype(BF16)
        dw = lax.dot_general(dv, dyp, TN, preferred_element_type=F32)
        dsc = jnp.sum(dy * ypre, axis=0, keepdims=True)

        @pl.when(b == 0)
        def _():
            dw_ref[...] = dw
            dsc_ref[...] = dsc

        @pl.when(b > 0)
        def _():
            dw_ref[...] += dw
            dsc_ref[...] += dsc

    blk = pl.BlockSpec((None, S, POOL_GROUP), lambda g, b: (b, 0, g))
    wspec = pl.BlockSpec((None, POOL_GROUP, POOL_GROUP), lambda g, b: (g, 0, 0))
    sspec = pl.BlockSpec((1, POOL_GROUP), lambda g, b: (0, g))
    return pl.pallas_call(
        body, name=name, grid=(len(POOL_WINDOWS), B),
        in_specs=[blk, wspec, sspec, pl.BlockSpec((None, S, POOL_GROUP), lambda g, b: (b, 0, c0 + g))],
        out_specs=[blk, wspec, sspec],
        out_shape=[jax.ShapeDtypeStruct((B, S, D_POOL), BF16),
                   jax.ShapeDtypeStruct((len(POOL_WINDOWS), POOL_GROUP, POOL_GROUP), F32),
                   jax.ShapeDtypeStruct((1, D_POOL), F32)],
        compiler_params=_params(dimension_semantics=("arbitrary", "arbitrary")),
    )(d, wp, pscale, dcat)


def _cross_softmax(q, k):
    s = lax.dot_general(q, k, NT, preferred_element_type=F32) * (CROSS_DIM ** -0.5)
    e = jnp.exp(s - jnp.max(s, axis=-1, keepdims=True))
    return e / jnp.sum(e, axis=-1, keepdims=True)


def _cross_fwd(name, qc, kv, tq=512):
    B, S, _ = qc.shape
    M = kv.shape[1]
    tq = _tile(S, tq)

    def body(q_ref, k_ref, v_ref, o_ref):
        p = _cross_softmax(q_ref[...], k_ref[...])
        o_ref[...] = jnp.dot(p.astype(BF16), v_ref[...], preferred_element_type=F32).astype(BF16)

    qspec = pl.BlockSpec((None, tq, CROSS_DIM), lambda b, h, i: (b, i, h))
    return pl.pallas_call(
        body, name=name, grid=(B, CROSS_HEADS, S // tq),
        in_specs=[qspec, pl.BlockSpec((None, M, CROSS_DIM), lambda b, h, i: (b, 0, h)),
                  pl.BlockSpec((None, M, CROSS_DIM), lambda b, h, i: (b, 0, CROSS_HEADS + h))],
        out_specs=qspec, out_shape=jax.ShapeDtypeStruct((B, S, D_CROSS), BF16),
        compiler_params=_params(dimension_semantics=("parallel", "parallel", "parallel")),
    )(qc, kv, kv)


def _cross_bwd(name, qc, kv, do, tq=512):
    B, S, _ = qc.shape
    M = kv.shape[1]
    tq = _tile(S, tq)
    nq = S // tq
    scale = CROSS_DIM ** -0.5

    def body(q_ref, k_ref, v_ref, do_ref, dq_ref, dk_ref, dv_ref, dka, dva):
        i = pl.program_id(2)
        q, k, v, dov = q_ref[...], k_ref[...], v_ref[...], do_ref[...]
        p = _cross_softmax(q, k)
        dp = lax.dot_general(dov, v, NT, preferred_element_type=F32)
        ds = ((p * (dp - jnp.sum(p * dp, axis=-1, keepdims=True))) * scale).astype(BF16)
        dq_ref[...] = jnp.dot(ds, k, preferred_element_type=F32).astype(BF16)
        dk = lax.dot_general(ds, q, TN, preferred_element_type=F32)
        dv = lax.dot_general(p.astype(BF16), dov, TN, preferred_element_type=F32)

        @pl.when(i == 0)
        def _():
            dka[...] = dk
            dva[...] = dv

        @pl.when(i > 0)
        def _():
            dka[...] += dk
            dva[...] += dv

        @pl.when(i == nq - 1)
        def _():
            dk_ref[...] = dka[...].astype(BF16)
            dv_ref[...] = dva[...].astype(BF16)

    qspec = pl.BlockSpec((None, tq, CROSS_DIM), lambda b, h, i: (b, i, h))
    kspec = pl.BlockSpec((None, M, CROSS_DIM), lambda b, h, i: (b, 0, h))
    return pl.pallas_call(
        body, name=name, grid=(B, CROSS_HEADS, nq),
        in_specs=[qspec, kspec, pl.BlockSpec((None, M, CROSS_DIM), lambda b, h, i: (b, 0, CROSS_HEADS + h)), qspec],
        out_specs=[qspec, kspec, kspec],
        out_shape=[jax.ShapeDtypeStruct((B, S, D_CROSS), BF16), jax.ShapeDtypeStruct((B, M, D_CROSS), BF16),
                   jax.ShapeDtypeStruct((B, M, D_CROSS), BF16)],
        scratch_shapes=[pltpu.VMEM((M, CROSS_DIM), F32), pltpu.VMEM((M, CROSS_DIM), F32)],
        compiler_params=_params(dimension_semantics=("parallel", "parallel", "arbitrary")),
    )(qc, kv, kv, do)


def _local_step(x, mem, target, small, weights, emit, start_token=None):
    B, S, D = x.shape
    T = B * S
    x2, t2 = x.reshape(T, D), target.reshape(T, D)
    mem2 = mem.reshape(-1, D)
    n_mem = mem.shape[1]
    wts = {}

    hn1 = _rms_fwd("norm_ffn1", x2, small["ffn1_norm"], after=start_token)
    wts.update(weights(0, hn1))
    g1, u1, a1 = _ffn_up("ffn1_up", hn1, wts["ffn1_w_gate"], wts["ffn1_w_up"])
    wts.update(weights(1, a1))
    h1 = _ffn_down("ffn1_down", a1, wts["ffn1_w_down"], x2)
    hn2 = _rms_fwd("norm_mix", h1, small["mix_norm"])
    wts.update(weights(2, hn2))
    z = _mm_nn_cols("mix_in", hn2, wts["w_in"], F32).reshape(B, S, -1)
    bias = _bias_tile("bias_tile", small["rel_bias"])
    y_attn = _attn_fwd("attn_fwd", z, bias)
    d_pool, y_pool = _pool_fwd("pool_fwd", z, wts["w_pool"], small["pool_scale"])
    cat = jnp.concatenate([y_attn, y_pool], axis=-1).reshape(T, -1)
    h2 = _mm_nn("mix_out", cat, wts["w_out"], F32, res=h1)
    hn3 = _rms_fwd("norm_cross", h2, small["cross_norm"])
    memn = _rms_fwd("norm_mem", mem2, small["mem_norm"])
    wts.update(weights(3, hn3))
    qc = _mm_nn("cross_q", hn3, wts["w_cq"], BF16)
    kv = _mm_nn("cross_kv", memn, wts["w_ckv"], BF16)
    o = _cross_fwd("cross_fwd", qc.reshape(B, S, -1), kv.reshape(B, n_mem, -1)).reshape(T, -1)
    h3 = _mm_nn_cols("cross_out", o, wts["w_co"], F32, res=h2)
    hn4 = _rms_fwd("norm_ffn2", h3, small["ffn2_norm"])
    wts.update(weights(4, hn4))
    g2, u2, a2 = _ffn_up("ffn2_up", hn4, wts["ffn2_w_gate"], wts["ffn2_w_up"])
    h4 = _ffn_down("ffn2_down", a2, wts["ffn2_w_down"], h3)

    gs = {}
    loss_part, dh4, dh4b, gs["final_norm"] = _loss_and_grad("loss", h4, small["final_norm"], t2)

    def ffn_bwd(tag, dh, dhb, h_in, hn, g, u, a, wg, wu, wd, gain):
        dg, du = _ffn_dact(tag + "_dact", dhb, wd, g, u)
        tok = emit({tag + "_w_gate": _ffn_dwcol(tag + "_dwg", hn, dg)})
        tok = emit({tag + "_w_up": _ffn_dwcol(tag + "_dwu", hn, du, after=tok)})
        tok = emit({tag + "_w_down": _ffn_dwd(tag + "_dwd", a, dhb, after=tok)})
        dhn = _ffn_dhn(tag + "_dhn", [(dg, wg), (du, wu)], after=tok)
        return _rms_bwd(tag + "_dnorm", h_in, gain, dhn, skip=dh)

    dh3, dh3b, gs["ffn2_norm"] = ffn_bwd("ffn2", dh4, dh4b, h3, hn4, g2, u2, a2, wts["ffn2_w_gate"],
                                         wts["ffn2_w_up"], wts["ffn2_w_down"], small["ffn2_norm"])
    do = _mm_nt_cols("cross_do", dh3b, wts["w_co"], BF16, tn=D_CROSS)
    gw = {"w_co": _mm_tn("cross_dwo", o, dh3b, tm=D_CROSS, col_blocks=N_DEV)}
    dqc, dk, dv = _cross_bwd("cross_bwd", qc.reshape(B, S, -1), kv.reshape(B, n_mem, -1), do.reshape(B, S, -1))
    dqc = dqc.reshape(T, -1)
    dkv = jnp.concatenate([dk, dv], axis=-1).reshape(B * n_mem, -1)
    gw["w_cq"] = _mm_tn("cross_dwq", hn3, dqc, tn=D_CROSS)
    gw["w_ckv"] = _mm_tn("cross_dwkv", memn, dkv, tk=512)
    tok = emit(gw)
    dhn3 = _mm_nt("cross_dhn", dqc, wts["w_cq"], F32, after=tok)
    dmemn = _mm_nt("cross_dmem", dkv, wts["w_ckv"], F32, tm=512)
    _, _, gs["mem_norm"] = _rms_bwd("mem_dnorm", mem2, small["mem_norm"], dmemn)
    dh2, dh2b, gs["cross_norm"] = _rms_bwd("cross_dnorm", h2, small["cross_norm"], dhn3, skip=dh3)
    dcat = _mm_nt("mix_dcat", dh2b, wts["w_out"], BF16)
    gw = {"w_out": _mm_tn("mix_dwout", cat, dh2b)}
    dcat3 = dcat.reshape(B, S, -1)
    dq, dkk, dvv, dbias = _attn_bwd("attn_bwd", z, bias, dcat3)
    du, gw["w_pool"], gs["pool_scale"] = _pool_bwd("pool_bwd", d_pool, wts["w_pool"], small["pool_scale"], dcat3)
    gs["rel_bias"] = _rel_grad_from_diagonals(_bias_grad("bias_grad", dbias)[:, 0, :])
    dz = jnp.concatenate([dq, dkk, dvv, du], axis=-1).reshape(T, -1)
    gw["w_in"] = _mm_tn("mix_dwin", hn2, dz, col_blocks=N_DEV)
    tok = emit(gw)
    dhn2 = _mm_nt_cols("mix_dhn", dz, wts["w_in"], F32, after=tok)
    dh1, dh1b, gs["mix_norm"] = _rms_bwd("mix_dnorm", h1, small["mix_norm"], dhn2, skip=dh2)
    dx, _, gs["ffn1_norm"] = ffn_bwd("ffn1", dh1, dh1b, x2, hn1, g1, u1, a1, wts["ffn1_w_gate"],
                                     wts["ffn1_w_up"], wts["ffn1_w_down"], small["ffn1_norm"])
    return loss_part, dx.reshape(B, S, D), gs


def _position():
    return lax.axis_index("x"), lax.axis_index("y"), lax.axis_index("c")


def _index(p):
    return 4 * p[0] + 2 * p[1] + p[2]


HBM_SPEC = pl.BlockSpec(memory_space=pltpu.HBM)
SEM_SPEC = pl.BlockSpec(memory_space=pltpu.SEMAPHORE)
ANY_SPEC = pl.BlockSpec(memory_space=pl.ANY)
ORDERED_EFFECT = pltpu.SideEffectType.DATAFLOW_SIDE_EFFECTING


def _peers(x, y, c):
    return [(x ^ (k >> 2), y ^ ((k >> 1) & 1), c ^ (k & 1)) for k in range(1, N_DEV)]


def _exchange_copies(ins, lands, send, recv, whole):
    x, y, c = _position()
    me = _index((x, y, c))
    copies = []
    for k, peer in enumerate(_peers(x, y, c)):
        for a in range(len(ins)):
            copies.append(pltpu.make_async_remote_copy(
                src_ref=ins[a] if whole else ins[a].at[_index(peer)], dst_ref=lands[a].at[me],
                send_sem=send.at[a * (N_DEV - 1) + k], recv_sem=recv.at[a * (N_DEV - 1) + k],
                device_id=peer, device_id_type=MESH))
    return copies


def _exchange_start(name, srcs, lands, whole, after=None):
    n = len(srcs)
    has_after = after is not None

    def body(*refs):
        send, recv = refs[2 * n + has_after], refs[2 * n + has_after + 1]
        for cp in _exchange_copies(refs[:n], refs[n:2 * n], send, recv, whole):
            cp.start()
        refs[-1][...] = jnp.zeros((8, 128), F32)

    sems = pltpu.SemaphoreType.DMA((n * (N_DEV - 1),))
    arrays = list(srcs) + list(lands)
    res = pl.pallas_call(
        body, name=name,
        out_shape=(sems, sems, *[pltpu.HBM(a.shape, a.dtype) for a in arrays], jax.ShapeDtypeStruct((8, 128), F32)),
        in_specs=[HBM_SPEC] * (2 * n) + [ANY_SPEC] * has_after,
        out_specs=(SEM_SPEC, SEM_SPEC, *[HBM_SPEC] * (2 * n), pl.BlockSpec(memory_space=pltpu.VMEM)),
        input_output_aliases={i: 2 + i for i in range(2 * n)},
        compiler_params=pltpu.CompilerParams(has_side_effects=ORDERED_EFFECT),
    )(*[pltpu.with_memory_space_constraint(a, pltpu.HBM) for a in arrays], *([after] if has_after else []))
    return res[0], res[1], list(res[2:2 + n]), list(res[2 + n:2 + 2 * n]), res[-1]


def _exchange_finish(name, send, recv, srcs, lands, whole, after):
    n = len(srcs)

    def body(*refs):
        for cp in _exchange_copies(refs[:n], refs[n:2 * n], refs[2 * n], refs[2 * n + 1], whole):
            cp.wait_send()
            cp.wait_recv()

    arrays = list(srcs) + list(lands)
    res = pl.pallas_call(
        body, name=name, out_shape=[pltpu.HBM(a.shape, a.dtype) for a in arrays],
        in_specs=[HBM_SPEC] * (2 * n) + [SEM_SPEC, SEM_SPEC, ANY_SPEC], out_specs=[HBM_SPEC] * (2 * n),
        input_output_aliases={i: i for i in range(2 * n)},
        compiler_params=pltpu.CompilerParams(has_side_effects=ORDERED_EFFECT),
    )(*arrays, send, recv, after)
    return list(res[n:])


def _adamw_math(w, g, m, v):
    m = ADAM_B1 * m + (1.0 - ADAM_B1) * g
    v = ADAM_B2 * v + (1.0 - ADAM_B2) * (g * g)
    m_hat = m / (1.0 - ADAM_B1 ** ADAM_STEP)
    v_hat = v / (1.0 - ADAM_B2 ** ADAM_STEP)
    delta = -ADAM_LR * (m_hat / (jnp.sqrt(v_hat) + ADAM_EPS) + ADAM_WD * w)
    return delta, m, v


def _adamw(name, parts, w, m, v, tr=128):
    R, C = w.shape
    tr = _tile(R, tr)

    def body(p_ref, w_ref, m_ref, v_ref, g_out, d_out, m_out, v_out):
        g = p_ref[0].astype(F32)
        for d in range(1, N_DEV):
            g = g + p_ref[d].astype(F32)
        g_out[...] = g
        d_out[...], m_out[...], v_out[...] = _adamw_math(w_ref[...], g, m_ref[...], v_ref[...])

    row = pl.BlockSpec((tr, C), lambda i: (i, 0))
    out = jax.ShapeDtypeStruct((R, C), F32)
    return pl.pallas_call(
        body, name=name, grid=(R // tr,),
        in_specs=[pl.BlockSpec((N_DEV, tr, C), lambda i: (0, i, 0)), row, row, row],
        out_specs=[row] * 4, out_shape=[out] * 4,
        compiler_params=_params(dimension_semantics=("parallel",)),
    )(parts, w, m, v)


def _small_allreduce_adamw(name, g, w, m, v):
    R = g.shape[0]

    def body(g_ref, w_ref, m_ref, v_ref, g_out, d_out, m_out, v_out, land, send, recv):
        x, y, c = _position()
        me = _index((x, y, c))
        land[me] = g_ref[...]
        copies = []
        for k in range(1, N_DEV):
            peer = (x ^ (k >> 2), y ^ ((k >> 1) & 1), c ^ (k & 1))
            copies.append(pltpu.make_async_remote_copy(
                src_ref=g_ref, dst_ref=land.at[me], send_sem=send.at[k - 1], recv_sem=recv.at[k - 1],
                device_id=peer, device_id_type=MESH))
        for cp in copies:
            cp.start()
        for cp in copies:
            cp.wait()
        total = land[0]
        for d in range(1, N_DEV):
            total = total + land[d]
        g_out[...] = total
        d_out[...], m_out[...], v_out[...] = _adamw_math(w_ref[...], total, m_ref[...], v_ref[...])

    vm = pl.BlockSpec(memory_space=pltpu.VMEM)
    out = jax.ShapeDtypeStruct((R, 128), F32)
    return pl.pallas_call(
        body, name=name, in_specs=[vm] * 4, out_specs=[vm] * 4, out_shape=[out] * 4,
        scratch_shapes=[pltpu.VMEM((N_DEV, R, 128), F32), pltpu.SemaphoreType.DMA((7,)),
                        pltpu.SemaphoreType.DMA((7,))],
    )(g, w, m, v)


BIG = ("ffn1_w_gate", "ffn1_w_up", "ffn1_w_down", "w_in", "w_pool", "w_out", "w_cq", "w_ckv", "w_co",
       "ffn2_w_gate", "ffn2_w_up", "ffn2_w_down")
SMALL = ("ffn1_norm", "mix_norm", "rel_bias", "pool_scale", "cross_norm", "mem_norm", "ffn2_norm", "final_norm")
ORDER = ("ffn1_norm", "ffn1_w_gate", "ffn1_w_up", "ffn1_w_down", "mix_norm", "w_in", "rel_bias", "w_pool",
         "pool_scale", "w_out", "cross_norm", "mem_norm", "w_cq", "w_ckv", "w_co", "ffn2_norm", "ffn2_w_gate",
         "ffn2_w_up", "ffn2_w_down", "final_norm")
ROW_SHARDED = ("ffn1_w_down", "ffn2_w_down", "w_out", "w_cq", "w_ckv")
GATHER_GROUPS = (("ffn1_w_gate", "ffn1_w_up"), ("ffn1_w_down",), ("w_in", "w_pool", "w_out"),
                 ("w_cq", "w_ckv", "w_co"), ("ffn2_w_gate", "ffn2_w_up", "ffn2_w_down"))


def _pack(arrays):
    flat = jnp.concatenate([a.reshape(-1) for a in arrays])
    rows = -(-flat.shape[0] // 1024) * 8
    return jnp.pad(flat, (0, rows * 128 - flat.shape[0])).reshape(rows, 128)


def _unpack(packed, like):
    flat, out, at = packed.reshape(-1), [], 0
    for a in like:
        out.append(flat[at:at + a.size].reshape(a.shape))
        at += a.size
    return out


def _shard2d(a):
    a = a[0]
    return a.reshape(-1, a.shape[-1])


def kernel(x, mem, ffn1_norm, ffn1_w_gate, ffn1_w_up, ffn1_w_down, mix_norm, w_in, rel_bias, w_pool, pool_scale, w_out, cross_norm, mem_norm, w_cq, w_ckv, w_co, ffn2_norm, ffn2_w_gate, ffn2_w_up, ffn2_w_down, final_norm, loss_target, m_ffn1_norm, m_ffn1_w_gate, m_ffn1_w_up, m_ffn1_w_down, m_mix_norm, m_w_in, m_rel_bias, m_w_pool, m_pool_scale, m_w_out, m_cross_norm, m_mem_norm, m_w_cq, m_w_ckv, m_w_co, m_ffn2_norm, m_ffn2_w_gate, m_ffn2_w_up, m_ffn2_w_down, m_final_norm, v_ffn1_norm, v_ffn1_w_gate, v_ffn1_w_up, v_ffn1_w_down, v_mix_norm, v_w_in, v_rel_bias, v_w_pool, v_pool_scale, v_w_out, v_cross_norm, v_mem_norm, v_w_cq, v_w_ckv, v_w_co, v_ffn2_norm, v_ffn2_w_gate, v_ffn2_w_up, v_ffn2_w_down, v_final_norm):
    args = dict(locals())
    w_in_ = {n: args[n] for n in ORDER}
    m_in = {n: args["m_" + n] for n in ORDER}
    v_in = {n: args["v_" + n] for n in ORDER}

    me = 4 * lax.axis_index("x") + 2 * lax.axis_index("y") + lax.axis_index("c")
    n_g, rows = len(POOL_WINDOWS), POOL_GROUP // N_DEV

    def own_block_placed(block):
        return lax.dynamic_update_slice_in_dim(lax.empty((N_DEV,) + block.shape, block.dtype), block[None], me, 0)

    gathers, tok = [], None
    for gi, group in enumerate(GATHER_GROUPS):
        shards = [_shard2d(w_in_[n]).astype(BF16) for n in group]
        send, recv, srcs, lands, tok = _exchange_start("gather_start_%d" % gi, shards,
                                                       [own_block_placed(s) for s in shards], True, after=tok)
        gathers.append((send, recv, srcs, lands))

    def weights(gi, after):
        out = {}
        for n, full in zip(GATHER_GROUPS[gi], _exchange_finish("gather_finish_%d" % gi, *gathers[gi], True, after)):
            if n == "w_pool":
                full = full.reshape(N_DEV, n_g, rows, POOL_GROUP).transpose(1, 0, 2, 3).reshape(n_g, POOL_GROUP, POOL_GROUP)
            out[n] = full.reshape(-1, full.shape[-1]) if n in ROW_SHARDED else full
        return out

    scatters = []

    def emit(gw):
        names = list(gw)
        stacks = []
        for n in names:
            g = gw[n]
            if n == "w_pool":
                g = g.reshape(n_g, N_DEV, rows, POOL_GROUP).transpose(1, 0, 2, 3).astype(BF16)
            stacks.append(g.reshape((N_DEV,) + _shard2d(w_in_[n]).shape))
        lands = [own_block_placed(lax.dynamic_index_in_dim(s, me, 0, keepdims=False)) for s in stacks]
        send, recv, srcs, lands, token = _exchange_start("grads_start_%d" % len(scatters), stacks, lands, False)
        scatters.append((names, send, recv, srcs, lands))
        return token

    small = {n: w_in_[n].reshape(1, -1) for n in SMALL if n != "rel_bias"}
    small["rel_bias"] = rel_bias[0]
    loss_part, grad_x, gs = _local_step(x, mem, loss_target, small, weights, emit, start_token=tok)
    loss = lax.psum(loss_part[0, 0], ("x", "y", "c"))

    grad, delta, new_m, new_v = {}, {}, {}, {}
    after = grad_x
    for si, (names, send, recv, srcs, lands) in enumerate(scatters):
        landed = _exchange_finish("grads_finish_%d" % si, send, recv, srcs, lands, False, after)
        for n, parts in zip(names, landed):
            res = _adamw("adamw_" + n, parts, _shard2d(w_in_[n]), _shard2d(m_in[n]), _shard2d(v_in[n]))
            grad[n], delta[n], new_m[n], new_v[n] = [r.reshape(w_in_[n].shape) for r in res]
        after = res[0]

    like = [w_in_[n] for n in SMALL]
    gs["rel_bias"] = gs["rel_bias"].reshape(rel_bias.shape)
    res = _small_allreduce_adamw("small_params", _pack([gs[n] for n in SMALL]), _pack(like),
                                 _pack([m_in[n] for n in SMALL]), _pack([v_in[n] for n in SMALL]))
    for d, packed in zip((grad, delta, new_m, new_v), res):
        for n, a in zip(SMALL, _unpack(packed, like)):
            d[n] = a
    return (loss, grad_x, *[grad[n] for n in ORDER], *[delta[n] for n in ORDER],
            *[new_m[n] for n in ORDER], *[new_v[n] for n in ORDER])
```

```python
import functools

import jax
import jax.numpy as jnp
from jax import lax
from jax.experimental import pallas as pl
from jax.experimental.pallas import tpu as pltpu

F32 = jnp.float32
BF16 = jnp.bfloat16

N_DEV = 8
EPS = 1e-6
NEG_INF = -1e30
CHUNK = 64
LEFT_CHUNKS = 8
PAD = LEFT_CHUNKS * CHUNK
QBLK = 2 * CHUNK
KBAND = PAD + QBLK
REL_CLIP = 128
ATTN_HEADS = 16
HEAD_DIM = 64
D_ATTN = ATTN_HEADS * HEAD_DIM
POOL_WINDOWS = (2, 4, 8, 16)
POOL_GROUP = 256
D_POOL = len(POOL_WINDOWS) * POOL_GROUP
CROSS_HEADS = 4
CROSS_DIM = 128
D_CROSS = CROSS_HEADS * CROSS_DIM
FFN_RES = 0.5
ADAM_LR, ADAM_B1, ADAM_B2, ADAM_EPS, ADAM_WD, ADAM_STEP = 0.001, 0.9, 0.999, 1e-08, 0.01, 10

NN = (((1,), (0,)), ((), ()))
NT = (((1,), (1,)), ((), ()))
TN = (((0,), (0,)), ((), ()))
MESH = pl.DeviceIdType.MESH
VMEM_LIMIT = 56 * 1024 * 1024


def _params(**kw):
    return pltpu.CompilerParams(vmem_limit_bytes=VMEM_LIMIT, **kw)


def _bf(v):
    return v if v.dtype == BF16 else v.astype(BF16)


def _gemm(name, pairs, dims, grid, outs, extras=(), epilogue=None, acc_shape=None, after=None):
    nk, npairs, nex, nout = grid[2], len(pairs), len(extras), len(outs)
    first_out = 2 * npairs + nex + (after is not None)

    def body(*refs):
        ab = refs[:2 * npairs]
        ex = refs[2 * npairs:2 * npairs + nex]
        out = refs[first_out:first_out + nout]
        part = None
        for p in range(npairs):
            d = lax.dot_general(_bf(ab[2 * p][...]), _bf(ab[2 * p + 1][...]), dims, preferred_element_type=F32)
            part = d if part is None else part + d

        def finish(total):
            vals = epilogue(total, *[e[...] for e in ex]) if epilogue is not None else (total,)
            for r, v in zip(out, vals):
                r[...] = v.astype(r.dtype)

        if nk == 1:
            finish(part)
        else:
            acc = refs[-1]
            k = pl.program_id(2)

            @pl.when(k == 0)
            def _():
                acc[...] = part

            @pl.when(k > 0)
            def _():
                acc[...] += part

            @pl.when(k == nk - 1)
            def _():
                finish(acc[...])

    operands, in_specs = [], []
    for a, ablk, amap, b, bblk, bmap in pairs:
        operands += [a, b]
        in_specs += [pl.BlockSpec(ablk, amap), pl.BlockSpec(bblk, bmap)]
    for x, xblk, xmap in extras:
        operands.append(x)
        in_specs.append(pl.BlockSpec(xblk, xmap))
    if after is not None:
        operands.append(after)
        in_specs.append(pl.BlockSpec(after.shape, lambda i, j, k: (0, 0)))
    res = pl.pallas_call(
        body, name=name, grid=grid, in_specs=in_specs,
        out_specs=[pl.BlockSpec(blk, m) for _, _, blk, m in outs],
        out_shape=[jax.ShapeDtypeStruct(s, d) for s, d, _, _ in outs],
        scratch_shapes=[pltpu.VMEM(acc_shape, F32)] if nk > 1 else [],
        compiler_params=_params(dimension_semantics=("parallel", "parallel", "arbitrary")),
    )(*operands)
    return res[0] if nout == 1 else res


def _tile(n, want):
    for t in range(min(n, want), 15, -1):
        if n % t == 0 and t % 16 == 0:
            return t
    return n


def _mm_nn(name, a, b, out_dtype, res=None, tm=1024, tn=1024):
    M, K = a.shape
    N = b.shape[1]
    tm, tn = _tile(M, tm), _tile(N, tn)
    extras = [] if res is None else [(res, (tm, tn), lambda i, j, k: (i, j))]
    epi = None if res is None else (lambda t, r: (r + t,))
    return _gemm(name, [(a, (tm, K), lambda i, j, k: (i, 0), b, (K, tn), lambda i, j, k: (0, j))], NN,
                 (M // tm, N // tn, 1), [((M, N), out_dtype, (tm, tn), lambda i, j, k: (i, j))], extras, epi)


def _mm_nn_cols(name, a, bs, out_dtype, res=None, tm=1024):
    M, K = a.shape
    nb, _, w = bs.shape
    tm = _tile(M, tm)
    extras = [] if res is None else [(res, (tm, w), lambda i, j, k: (i, j))]
    epi = None if res is None else (lambda t, r: (r + t,))
    return _gemm(name, [(a, (tm, K), lambda i, j, k: (i, 0), bs, (None, K, w), lambda i, j, k: (j, 0, 0))], NN,
                 (M // tm, nb, 1), [((M, nb * w), out_dtype, (tm, w), lambda i, j, k: (i, j))], extras, epi)


def _mm_nt(name, a, b, out_dtype, tm=1024, tn=1024, after=None):
    M, K = a.shape
    N = b.shape[0]
    tm, tn = _tile(M, tm), _tile(N, tn)
    return _gemm(name, [(a, (tm, K), lambda i, j, k: (i, 0), b, (tn, K), lambda i, j, k: (j, 0))], NT,
                 (M // tm, N // tn, 1), [((M, N), out_dtype, (tm, tn), lambda i, j, k: (i, j))], after=after)


def _mm_nt_cols(name, a, bs, out_dtype, tm=1024, tn=1024, after=None):
    M = a.shape[0]
    nb, N, w = bs.shape
    tm, tn = _tile(M, tm), _tile(N, tn)
    return _gemm(name, [(a, (tm, w), lambda i, j, k: (i, k), bs, (None, tn, w), lambda i, j, k: (k, j, 0))], NT,
                 (M // tm, N // tn, nb), [((M, N), out_dtype, (tm, tn), lambda i, j, k: (i, j))],
                 acc_shape=(tm, tn), after=after)


def _mm_tn(name, a, b, scale=1.0, tm=2048, tn=1024, tk=1024, col_blocks=None):
    T, Ka = a.shape
    Nb = b.shape[1]
    tk = _tile(T, tk)
    tm = _tile(Ka, tm)
    epi = None if scale == 1.0 else (lambda t: (t * scale,))
    if col_blocks is None:
        tn = _tile(Nb, tn)
        out = ((Ka, Nb), BF16, (tm, tn), lambda i, j, k: (i, j))
        nj = Nb // tn
    else:
        tn = Nb // col_blocks
        out = ((col_blocks, Ka, tn), BF16, (None, tm, tn), lambda i, j, k: (j, i, 0))
        nj = col_blocks
    return _gemm(name, [(a, (tk, tm), lambda i, j, k: (k, i), b, (tk, tn), lambda i, j, k: (k, j))], TN,
                 (Ka // tm, nj, T // tk), [out], (), epi, (tm, tn))


def _ffn_down(name, a, wd, res, tm=1024, tn=1024):
    nb, M, w = a.shape
    N = wd.shape[1]
    tm, tn = _tile(M, tm), _tile(N, tn)
    return _gemm(name, [(a, (None, tm, w), lambda i, j, k: (k, i, 0), wd, (w, tn), lambda i, j, k: (k, j))], NN,
                 (M // tm, N // tn, nb), [((M, N), F32, (tm, tn), lambda i, j, k: (i, j))],
                 [(res, (tm, tn), lambda i, j, k: (i, j))], lambda t, r: (r + FFN_RES * t,), (tm, tn))


def _ffn_dact(name, dhb, wd, g, u, tm=1024):
    M, K = dhb.shape
    nb, _, w = g.shape
    tm = _tile(M, tm)
    hid = ((None, tm, w), lambda i, j, k: (j, i, 0))
    return _gemm(name, [(dhb, (tm, K), lambda i, j, k: (i, 0), wd, (w, K), lambda i, j, k: (j, 0))], NT,
                 (M // tm, nb, 1), [((nb, M, w), BF16) + hid] * 2, [(g,) + hid, (u,) + hid], _swiglu_bwd)


def _ffn_dwd(name, a, dhb, tk=1024, after=None):
    nb, T, w = a.shape
    N = dhb.shape[1]
    tk = _tile(T, tk)
    return _gemm(name, [(a, (None, tk, w), lambda i, j, k: (i, k, 0), dhb, (tk, N), lambda i, j, k: (k, 0))], TN,
                 (nb, 1, T // tk), [((nb * w, N), BF16, (w, N), lambda i, j, k: (i, 0))], (),
                 lambda t: (t * FFN_RES,), (w, N), after)


def _ffn_dwcol(name, hn, dg, tk=1024, after=None):
    T, K = hn.shape
    nb, _, w = dg.shape
    tk = _tile(T, tk)
    return _gemm(name, [(hn, (tk, K), lambda i, j, k: (k, 0), dg, (None, tk, w), lambda i, j, k: (j, k, 0))], TN,
                 (1, nb, T // tk), [((nb, K, w), BF16, (None, K, w), lambda i, j, k: (j, 0, 0))], (), None, (K, w), after)


def _ffn_dhn(name, pairs, tm=1024, tn=1024, after=None):
    nb, M, w = pairs[0][0].shape
    N = pairs[0][1].shape[1]
    tm, tn = _tile(M, tm), _tile(N, tn)
    gp = [(d, (None, tm, w), lambda i, j, k: (k, i, 0), ws, (None, tn, w), lambda i, j, k: (k, j, 0))
          for d, ws in pairs]
    return _gemm(name, gp, NT, (M // tm, N // tn, nb), [((M, N), F32, (tm, tn), lambda i, j, k: (i, j))],
                 acc_shape=(tm, tn), after=after)


def _ffn_up(name, hn, wg, wu, tm=1024):
    M, K = hn.shape
    nb, _, w = wg.shape
    tm = _tile(M, tm)

    def body(a_ref, g_ref, u_ref, og, ou, oa):
        a = a_ref[...]
        g = jnp.dot(a, g_ref[...], preferred_element_type=F32)
        u = jnp.dot(a, u_ref[...], preferred_element_type=F32)
        og[...] = g.astype(BF16)
        ou[...] = u.astype(BF16)
        oa[...] = (g * jax.nn.sigmoid(g) * u).astype(BF16)

    wspec = pl.BlockSpec((None, K, w), lambda i, j: (j, 0, 0))
    ospec = pl.BlockSpec((None, tm, w), lambda i, j: (j, i, 0))
    return pl.pallas_call(
        body, name=name, grid=(M // tm, nb),
        in_specs=[pl.BlockSpec((tm, K), lambda i, j: (i, 0)), wspec, wspec],
        out_specs=[ospec] * 3, out_shape=[jax.ShapeDtypeStruct((nb, M, w), BF16)] * 3,
        compiler_params=_params(dimension_semantics=("parallel", "parallel")),
    )(hn, wg, wu)


def _swiglu_bwd(dact, g, u):
    g = g.astype(F32)
    u = u.astype(F32)
    sig = jax.nn.sigmoid(g)
    silu = g * sig
    d = FFN_RES * dact
    return d * u * (sig * (1.0 + g * (1.0 - sig))), d * silu


def _rms_fwd(name, x, gain, tr=512, after=None):
    R, D = x.shape
    tr = _tile(R, tr)

    def body(x_ref, g_ref, *rest):
        xv = x_ref[...]
        y = xv * lax.rsqrt(jnp.mean(xv * xv, axis=-1, keepdims=True) + EPS)
        rest[-1][...] = (y * g_ref[...]).astype(BF16)

    tokens = [] if after is None else [after]
    return pl.pallas_call(
        body, name=name, grid=(R // tr,),
        in_specs=[pl.BlockSpec((tr, D), lambda i: (i, 0)), pl.BlockSpec((1, D), lambda i: (0, 0))]
        + [pl.BlockSpec(t.shape, lambda i: (0, 0)) for t in tokens],
        out_specs=pl.BlockSpec((tr, D), lambda i: (i, 0)), out_shape=jax.ShapeDtypeStruct((R, D), BF16),
        compiler_params=_params(dimension_semantics=("parallel",)),
    )(x, gain, *tokens)


def _rms_bwd_math(xv, gain, dy):
    rstd = lax.rsqrt(jnp.mean(xv * xv, axis=-1, keepdims=True) + EPS)
    xhat = xv * rstd
    dxh = dy * gain
    dx = rstd * (dxh - xhat * jnp.mean(dxh * xhat, axis=-1, keepdims=True))
    return dx, jnp.sum(dy * xhat, axis=0, keepdims=True)


def _rms_bwd(name, x, gain, dy, skip=None, tr=256):
    R, D = x.shape
    tr = _tile(R, tr)
    has_skip = skip is not None

    def body(*refs):
        x_ref, g_ref, dy_ref = refs[:3]
        dx_ref, dxb_ref, dg_ref = refs[-3:]
        dx, dg = _rms_bwd_math(x_ref[...], g_ref[...], dy_ref[...].astype(F32))
        if has_skip:
            dx = dx + refs[3][...]
        dx_ref[...] = dx
        dxb_ref[...] = dx.astype(BF16)

        @pl.when(pl.program_id(0) == 0)
        def _():
            dg_ref[...] = dg

        @pl.when(pl.program_id(0) > 0)
        def _():
            dg_ref[...] += dg

    row = pl.BlockSpec((tr, D), lambda i: (i, 0))
    vec = pl.BlockSpec((1, D), lambda i: (0, 0))
    return pl.pallas_call(
        body, name=name, grid=(R // tr,),
        in_specs=[row, vec, row] + ([row] if has_skip else []),
        out_specs=[row, row, vec],
        out_shape=[jax.ShapeDtypeStruct((R, D), F32), jax.ShapeDtypeStruct((R, D), BF16),
                   jax.ShapeDtypeStruct((1, D), F32)],
        compiler_params=_params(dimension_semantics=("arbitrary",)),
    )(*([x, gain, dy] + ([skip] if has_skip else [])))


def _loss_and_grad(name, h, gain, target, tr=256):
    R, D = h.shape
    tr = _tile(R, tr)

    def body(h_ref, g_ref, t_ref, loss_ref, dh_ref, dhb_ref, dg_ref):
        hv, gain_v = h_ref[...], g_ref[...]
        y = (hv * lax.rsqrt(jnp.mean(hv * hv, axis=-1, keepdims=True) + EPS)) * gain_v
        err = y - t_ref[...]
        part = jnp.full((8, 128), 0.5 * jnp.sum(jnp.mean(err * err, axis=-1, keepdims=True)), F32)
        dh, dg = _rms_bwd_math(hv, gain_v, err * (1.0 / D))
        dh_ref[...] = dh
        dhb_ref[...] = dh.astype(BF16)

        @pl.when(pl.program_id(0) == 0)
        def _():
            dg_ref[...] = dg
            loss_ref[...] = part

        @pl.when(pl.program_id(0) > 0)
        def _():
            dg_ref[...] += dg
            loss_ref[...] += part

    row = pl.BlockSpec((tr, D), lambda i: (i, 0))
    vec = pl.BlockSpec((1, D), lambda i: (0, 0))
    return pl.pallas_call(
        body, name=name, grid=(R // tr,), in_specs=[row, vec, row],
        out_specs=[pl.BlockSpec((8, 128), lambda i: (0, 0)), row, row, vec],
        out_shape=[jax.ShapeDtypeStruct((8, 128), F32), jax.ShapeDtypeStruct((R, D), F32),
                   jax.ShapeDtypeStruct((R, D), BF16), jax.ShapeDtypeStruct((1, D), F32)],
        compiler_params=_params(dimension_semantics=("arbitrary",)),
    )(h, gain, target)


def _bias_tile(name, rel):
    width = KBAND + QBLK
    sat = rel[:, 2 * REL_CLIP:]
    n_left = PAD - REL_CLIP + 1
    row0 = jnp.concatenate([jnp.broadcast_to(sat, (ATTN_HEADS, n_left)), rel[:, :2 * REL_CLIP][:, ::-1],
                            jnp.broadcast_to(sat, (ATTN_HEADS, width - n_left - 2 * REL_CLIP))], axis=1)

    def body(e_ref, o_ref):
        rows = pltpu.roll(jnp.broadcast_to(e_ref[...], (QBLK, width)), 0, 1, stride=1, stride_axis=0)
        i = lax.broadcasted_iota(jnp.int32, (QBLK, KBAND), 0) // CHUNK
        j = lax.broadcasted_iota(jnp.int32, (QBLK, KBAND), 1) // CHUNK
        o_ref[...] = jnp.where((j >= i) & (j <= i + LEFT_CHUNKS), rows[:, :KBAND], NEG_INF)

    return pl.pallas_call(
        body, name=name, grid=(ATTN_HEADS,),
        in_specs=[pl.BlockSpec((None, 1, width), lambda h: (h, 0, 0))],
        out_specs=pl.BlockSpec((None, QBLK, KBAND), lambda h: (h, 0, 0)),
        out_shape=jax.ShapeDtypeStruct((ATTN_HEADS, QBLK, KBAND), F32),
        compiler_params=_params(dimension_semantics=("parallel",)),
    )(row0.reshape(ATTN_HEADS, 1, width))


def _band_softmax(q_half, kb, bias, kpos):
    s = lax.dot_general(q_half, kb, NT, preferred_element_type=F32) * (HEAD_DIM ** -0.5) + bias
    s = jnp.where(kpos >= 0, s, NEG_INF)
    e = jnp.exp(s - jnp.max(s, axis=-1, keepdims=True))
    return e / jnp.sum(e, axis=-1, keepdims=True)


def _fill_padded(dst, src, S):
    dst[pl.ds(0, PAD), :] = jnp.zeros((PAD, dst.shape[1]), dst.dtype)
    dst[pl.ds(PAD, S), :] = src[...].astype(dst.dtype)


def _attn_fwd(name, z, bias):
    B, S, _ = z.shape
    nh2 = ATTN_HEADS // 2

    def body(q_ref, k_ref, v_ref, b_ref, o_ref, kp, vp):
        qb = pl.program_id(2)

        @pl.when(qb == 0)
        def _():
            _fill_padded(kp, k_ref, S)
            _fill_padded(vp, v_ref, S)

        start = pl.multiple_of(qb * QBLK, QBLK)
        kb, vb = kp[pl.ds(start, KBAND), :], vp[pl.ds(start, KBAND), :]
        q = q_ref[...].astype(BF16)
        first = lax.broadcasted_iota(jnp.int32, (QBLK, 2 * HEAD_DIM), 1) < HEAD_DIM
        kpos = qb * QBLK - PAD + lax.broadcasted_iota(jnp.int32, (1, KBAND), 1)
        o = []
        for a in range(2):
            p = _band_softmax(jnp.where(first == (a == 0), q, jnp.zeros_like(q)), kb, b_ref[a], kpos)
            o.append(jnp.dot(p.astype(BF16), vb, preferred_element_type=F32))
        o_ref[...] = jnp.where(first, o[0], o[1]).astype(BF16)

    return pl.pallas_call(
        body, name=name, grid=(B, nh2, S // QBLK),
        in_specs=[pl.BlockSpec((None, QBLK, 128), lambda b, h, i: (b, i, h)),
                  pl.BlockSpec((None, S, 128), lambda b, h, i: (b, 0, nh2 + h)),
                  pl.BlockSpec((None, S, 128), lambda b, h, i: (b, 0, 2 * nh2 + h)),
                  pl.BlockSpec((2, QBLK, KBAND), lambda b, h, i: (h, 0, 0))],
        out_specs=pl.BlockSpec((None, QBLK, 128), lambda b, h, i: (b, i, h)),
        out_shape=jax.ShapeDtypeStruct((B, S, D_ATTN), BF16),
        scratch_shapes=[pltpu.VMEM((PAD + S, 128), BF16), pltpu.VMEM((PAD + S, 128), BF16)],
        compiler_params=_params(dimension_semantics=("parallel", "parallel", "arbitrary")),
    )(z, z, z, bias)


def _attn_bwd(name, z, bias, dcat):
    B, S, _ = z.shape
    nh2 = ATTN_HEADS // 2
    nqb = S // QBLK
    scale = HEAD_DIM ** -0.5

    def body(q_ref, k_ref, v_ref, b_ref, do_ref, dq_ref, dk_ref, dv_ref, db_ref, kp, vp, dka, dva):
        b, qb = pl.program_id(1), pl.program_id(2)

        @pl.when(qb == 0)
        def _():
            _fill_padded(kp, k_ref, S)
            _fill_padded(vp, v_ref, S)
            dka[...] = jnp.zeros_like(dka)
            dva[...] = jnp.zeros_like(dva)

        @pl.when((qb == 0) & (b == 0))
        def _():
            db_ref[...] = jnp.zeros_like(db_ref)

        start = pl.multiple_of(qb * QBLK, QBLK)
        band = pl.ds(start, KBAND)
        kb, vb = kp[band, :], vp[band, :]
        q = q_ref[...].astype(BF16)
        do = do_ref[...]
        first = lax.broadcasted_iota(jnp.int32, (QBLK, 2 * HEAD_DIM), 1) < HEAD_DIM
        kpos = qb * QBLK - PAD + lax.broadcasted_iota(jnp.int32, (1, KBAND), 1)
        dq, dk, dv = [], None, None
        for a in range(2):
            mine = first == (a == 0)
            qa = jnp.where(mine, q, jnp.zeros_like(q))
            doa = jnp.where(mine, do, jnp.zeros_like(do))
            p = _band_softmax(qa, kb, b_ref[a], kpos)
            dp = lax.dot_general(doa, vb, NT, preferred_element_type=F32)
            ds = p * (dp - jnp.sum(p * dp, axis=-1, keepdims=True))
            db_ref[a] += ds
            dsb = (ds * scale).astype(BF16)
            dq.append(jnp.dot(dsb, kb, preferred_element_type=F32))
            dka_part = lax.dot_general(dsb, qa, TN, preferred_element_type=F32)
            dva_part = lax.dot_general(p.astype(BF16), doa, TN, preferred_element_type=F32)
            dk = dka_part if dk is None else dk + dka_part
            dv = dva_part if dv is None else dv + dva_part
        dq_ref[...] = jnp.where(first, dq[0], dq[1]).astype(BF16)
        dka[band, :] += dk
        dva[band, :] += dv

        @pl.when(qb == nqb - 1)
        def _():
            dk_ref[...] = dka[pl.ds(PAD, S), :].astype(BF16)
            dv_ref[...] = dva[pl.ds(PAD, S), :].astype(BF16)

    qspec = pl.BlockSpec((None, QBLK, 128), lambda h, b, i: (b, i, h))
    kvout = pl.BlockSpec((None, S, 128), lambda h, b, i: (b, 0, h))
    bspec = pl.BlockSpec((2, QBLK, KBAND), lambda h, b, i: (h, 0, 0))
    act = jax.ShapeDtypeStruct((B, S, D_ATTN), BF16)
    return pl.pallas_call(
        body, name=name, grid=(nh2, B, nqb),
        in_specs=[qspec,
                  pl.BlockSpec((None, S, 128), lambda h, b, i: (b, 0, nh2 + h)),
                  pl.BlockSpec((None, S, 128), lambda h, b, i: (b, 0, 2 * nh2 + h)),
                  bspec, qspec],
        out_specs=[qspec, kvout, kvout, bspec],
        out_shape=[act, act, act, jax.ShapeDtypeStruct((ATTN_HEADS, QBLK, KBAND), F32)],
        scratch_shapes=[pltpu.VMEM((PAD + S, 128), BF16), pltpu.VMEM((PAD + S, 128), BF16),
                        pltpu.VMEM((PAD + S, 128), F32), pltpu.VMEM((PAD + S, 128), F32)],
        compiler_params=_params(dimension_semantics=("arbitrary", "arbitrary", "arbitrary")),
    )(z, z, z, bias, dcat)


def _bias_grad(name, dbias):
    width = KBAND + QBLK

    def body(d_ref, o_ref):
        acc = jnp.zeros((1, width), F32)
        for i in range(QBLK):
            row = jnp.concatenate([d_ref[pl.ds(i, 1), :], jnp.zeros((1, QBLK), F32)], axis=1)
            shift = QBLK - 1 - i
            acc = acc + (pltpu.roll(row, shift, 1) if shift else row)
        o_ref[...] = acc

    return pl.pallas_call(
        body, name=name, grid=(ATTN_HEADS,),
        in_specs=[pl.BlockSpec((None, QBLK, KBAND), lambda h: (h, 0, 0))],
        out_specs=pl.BlockSpec((None, 1, width), lambda h: (h, 0, 0)),
        out_shape=jax.ShapeDtypeStruct((ATTN_HEADS, 1, width), F32),
        compiler_params=_params(dimension_semantics=("parallel",)),
    )(dbias)


def _rel_grad_from_diagonals(diag):
    top = PAD + QBLK - 1 - REL_CLIP
    sat = jnp.sum(diag[:, :top + 1], axis=1, keepdims=True)
    mid = diag[:, top + 1:top + 2 * REL_CLIP][:, ::-1]
    return jnp.concatenate([jnp.zeros_like(sat), mid, sat], axis=1)


def _shift_rows(x, k, forward):
    S = x.shape[0]
    t = lax.broadcasted_iota(jnp.int32, x.shape, 0)
    if forward:
        return jnp.where(t < S - k, pltpu.roll(x, S - k, 0), 0.0)
    return jnp.where(t >= k, pltpu.roll(x, k, 0), 0.0)


def _window_sum(x, g, forward):
    s = x + _shift_rows(x, 1, forward)
    out = s
    for n, k in enumerate((2, 4, 8)):
        s = s + _shift_rows(s, k, forward)
        out = jnp.where(g > n, s, out)
    return out


def _pool_count(S, g):
    t = lax.broadcasted_iota(jnp.int32, (S, 1), 0)
    w = jnp.left_shift(2, g)
    return jnp.minimum(t + 1, w).astype(F32)


def _pool_fwd(name, z, wp, pscale):
    B, S, _ = z.shape
    c0 = 3 * D_ATTN // POOL_GROUP

    def body(u_ref, w_ref, s_ref, d_ref, y_ref):
        g = pl.program_id(1)
        u = u_ref[...]
        d = (_window_sum(u, g, False) / _pool_count(S, g) - u).astype(BF16)
        d_ref[...] = d
        y_ref[...] = (jnp.dot(d, w_ref[...], preferred_element_type=F32) * s_ref[...]).astype(BF16)

    blk = pl.BlockSpec((None, S, POOL_GROUP), lambda b, g: (b, 0, g))
    out = jax.ShapeDtypeStruct((B, S, D_POOL), BF16)
    return pl.pallas_call(
        body, name=name, grid=(B, len(POOL_WINDOWS)),
        in_specs=[pl.BlockSpec((None, S, POOL_GROUP), lambda b, g: (b, 0, c0 + g)),
                  pl.BlockSpec((None, POOL_GROUP, POOL_GROUP), lambda b, g: (g, 0, 0)),
                  pl.BlockSpec((1, POOL_GROUP), lambda b, g: (0, g))],
        out_specs=[blk, blk], out_shape=[out, out],
        compiler_params=_params(dimension_semantics=("parallel", "parallel")),
    )(z, wp, pscale)


def _pool_bwd(name, d, wp, pscale, dcat):
    B, S, _ = d.shape
    c0 = D_ATTN // POOL_GROUP

    def body(d_ref, w_ref, s_ref, dy_ref, du_ref, dw_ref, dsc_ref):
        g, b = pl.program_id(0), pl.program_id(1)
        dv = d_ref[...]
        dy = dy_ref[...].astype(F32)
        w = w_ref[...]
        ypre = jnp.dot(dv, w, preferred_element_type=F32)
        dyp = (dy * s_ref[...]).astype(BF16)
        dd = lax.dot_general(dyp, w, NT, preferred_element_type=F32)
        du_ref[...] = (_window_sum(dd / _pool_count(S, g), g, True) - dd).astype(BF16)
        dw = lax.dot_general(dv, dyp, TN, preferred_element_type=F32)
        dsc = jnp.sum(dy * ypre, axis=0, keepdims=True)

        @pl.when(b == 0)
        def _():
            dw_ref[...] = dw
            dsc_ref[...] = dsc

        @pl.when(b > 0)
        def _():
            dw_ref[...] += dw
            dsc_ref[...] += dsc

    blk = pl.BlockSpec((None, S, POOL_GROUP), lambda g, b: (b, 0, g))
    wspec = pl.BlockSpec((None, POOL_GROUP, POOL_GROUP), lambda g, b: (g, 0, 0))
    sspec = pl.BlockSpec((1, POOL_GROUP), lambda g, b: (0, g))
    return pl.pallas_call(
        body, name=name, grid=(len(POOL_WINDOWS), B),
        in_specs=[blk, wspec, sspec, pl.BlockSpec((None, S, POOL_GROUP), lambda g, b: (b, 0, c0 + g))],
        out_specs=[blk, wspec, sspec],
        out_shape=[jax.ShapeDtypeStruct((B, S, D_POOL), BF16),
                   jax.ShapeDtypeStruct((len(POOL_WINDOWS), POOL_GROUP, POOL_GROUP), F32),
                   jax.ShapeDtypeStruct((1, D_POOL), F32)],
        compiler_params=_params(dimension_semantics=("arbitrary", "arbitrary")),
    )(d, wp, pscale, dcat)


def _cross_softmax(q, k):
    s = lax.dot_general(q, k, NT, preferred_element_type=F32) * (CROSS_DIM ** -0.5)
    e = jnp.exp(s - jnp.max(s, axis=-1, keepdims=True))
    return e / jnp.sum(e, axis=-1, keepdims=True)


def _cross_fwd(name, qc, kv, tq=512):
    B, S, _ = qc.shape
    M = kv.shape[1]
    tq = _tile(S, tq)

    def body(q_ref, k_ref, v_ref, o_ref):
        p = _cross_softmax(q_ref[...], k_ref[...])
        o_ref[...] = jnp.dot(p.astype(BF16), v_ref[...], preferred_element_type=F32).astype(BF16)

    qspec = pl.BlockSpec((None, tq, CROSS_DIM), lambda b, h, i: (b, i, h))
    return pl.pallas_call(
        body, name=name, grid=(B, CROSS_HEADS, S // tq),
        in_specs=[qspec, pl.BlockSpec((None, M, CROSS_DIM), lambda b, h, i: (b, 0, h)),
                  pl.BlockSpec((None, M, CROSS_DIM), lambda b, h, i: (b, 0, CROSS_HEADS + h))],
        out_specs=qspec, out_shape=jax.ShapeDtypeStruct((B, S, D_CROSS), BF16),
        compiler_params=_params(dimension_semantics=("parallel", "parallel", "parallel")),
    )(qc, kv, kv)


def _cross_bwd(name, qc, kv, do, tq=512):
    B, S, _ = qc.shape
    M = kv.shape[1]
    tq = _tile(S, tq)
    nq = S // tq
    scale = CROSS_DIM ** -0.5

    def body(q_ref, k_ref, v_ref, do_ref, dq_ref, dk_ref, dv_ref, dka, dva):
        i = pl.program_id(2)
        q, k, v, dov = q_ref[...], k_ref[...], v_ref[...], do_ref[...]
        p = _cross_softmax(q, k)
        dp = lax.dot_general(dov, v, NT, preferred_element_type=F32)
        ds = ((p * (dp - jnp.sum(p * dp, axis=-1, keepdims=True))) * scale).astype(BF16)
        dq_ref[...] = jnp.dot(ds, k, preferred_element_type=F32).astype(BF16)
        dk = lax.dot_general(ds, q, TN, preferred_element_type=F32)
        dv = lax.dot_general(p.astype(BF16), dov, TN, preferred_element_type=F32)

        @pl.when(i == 0)
        def _():
            dka[...] = dk
            dva[...] = dv

        @pl.when(i > 0)
        def _():
            dka[...] += dk
            dva[...] += dv

        @pl.when(i == nq - 1)
        def _():
            dk_ref[...] = dka[...].astype(BF16)
            dv_ref[...] = dva[...].astype(BF16)

    qspec = pl.BlockSpec((None, tq, CROSS_DIM), lambda b, h, i: (b, i, h))
    kspec = pl.BlockSpec((None, M, CROSS_DIM), lambda b, h, i: (b, 0, h))
    return pl.pallas_call(
        body, name=name, grid=(B, CROSS_HEADS, nq),
        in_specs=[qspec, kspec, pl.BlockSpec((None, M, CROSS_DIM), lambda b, h, i: (b, 0, CROSS_HEADS + h)), qspec],
        out_specs=[qspec, kspec, kspec],
        out_shape=[jax.ShapeDtypeStruct((B, S, D_CROSS), BF16), jax.ShapeDtypeStruct((B, M, D_CROSS), BF16),
                   jax.ShapeDtypeStruct((B, M, D_CROSS), BF16)],
        scratch_shapes=[pltpu.VMEM((M, CROSS_DIM), F32), pltpu.VMEM((M, CROSS_DIM), F32)],
        compiler_params=_params(dimension_semantics=("parallel", "parallel", "arbitrary")),
    )(qc, kv, kv, do)


def _local_step(x, mem, target, small, weights, emit, start_token=None):
    B, S, D = x.shape
    T = B * S
    x2, t2 = x.reshape(T, D), target.reshape(T, D)
    mem2 = mem.reshape(-1, D)
    n_mem = mem.shape[1]
    wts = {}

    hn1 = _rms_fwd("norm_ffn1", x2, small["ffn1_norm"], after=start_token)
    wts.update(weights(0, hn1))
    g1, u1, a1 = _ffn_up("ffn1_up", hn1, wts["ffn1_w_gate"], wts["ffn1_w_up"])
    wts.update(weights(1, a1))
    h1 = _ffn_down("ffn1_down", a1, wts["ffn1_w_down"], x2)
    hn2 = _rms_fwd("norm_mix", h1, small["mix_norm"])
    wts.update(weights(2, hn2))
    z = _mm_nn_cols("mix_in", hn2, wts["w_in"], F32).reshape(B, S, -1)
    bias = _bias_tile("bias_tile", small["rel_bias"])
    y_attn = _attn_fwd("attn_fwd", z, bias)
    d_pool, y_pool = _pool_fwd("pool_fwd", z, wts["w_pool"], small["pool_scale"])
    cat = jnp.concatenate([y_attn, y_pool], axis=-1).reshape(T, -1)
    h2 = _mm_nn("mix_out", cat, wts["w_out"], F32, res=h1)
    hn3 = _rms_fwd("norm_cross", h2, small["cross_norm"])
    memn = _rms_fwd("norm_mem", mem2, small["mem_norm"])
    wts.update(weights(3, hn3))
    qc = _mm_nn("cross_q", hn3, wts["w_cq"], BF16)
    kv = _mm_nn("cross_kv", memn, wts["w_ckv"], BF16)
    o = _cross_fwd("cross_fwd", qc.reshape(B, S, -1), kv.reshape(B, n_mem, -1)).reshape(T, -1)
    h3 = _mm_nn_cols("cross_out", o, wts["w_co"], F32, res=h2)
    hn4 = _rms_fwd("norm_ffn2", h3, small["ffn2_norm"])
    wts.update(weights(4, hn4))
    g2, u2, a2 = _ffn_up("ffn2_up", hn4, wts["ffn2_w_gate"], wts["ffn2_w_up"])
    h4 = _ffn_down("ffn2_down", a2, wts["ffn2_w_down"], h3)

    gs = {}
    loss_part, dh4, dh4b, gs["final_norm"] = _loss_and_grad("loss", h4, small["final_norm"], t2)

    def ffn_bwd(tag, dh, dhb, h_in, hn, g, u, a, wg, wu, wd, gain):
        dg, du = _ffn_dact(tag + "_dact", dhb, wd, g, u)
        tok = emit({tag + "_w_gate": _ffn_dwcol(tag + "_dwg", hn, dg)})
        tok = emit({tag + "_w_up": _ffn_dwcol(tag + "_dwu", hn, du, after=tok)})
        tok = emit({tag + "_w_down": _ffn_dwd(tag + "_dwd", a, dhb, after=tok)})
        dhn = _ffn_dhn(tag + "_dhn", [(dg, wg), (du, wu)], after=tok)
        return _rms_bwd(tag + "_dnorm", h_in, gain, dhn, skip=dh)

    dh3, dh3b, gs["ffn2_norm"] = ffn_bwd("ffn2", dh4, dh4b, h3, hn4, g2, u2, a2, wts["ffn2_w_gate"],
                                         wts["ffn2_w_up"], wts["ffn2_w_down"], small["ffn2_norm"])
    do = _mm_nt_cols("cross_do", dh3b, wts["w_co"], BF16, tn=D_CROSS)
    gw = {"w_co": _mm_tn("cross_dwo", o, dh3b, tm=D_CROSS, col_blocks=N_DEV)}
    dqc, dk, dv = _cross_bwd("cross_bwd", qc.reshape(B, S, -1), kv.reshape(B, n_mem, -1), do.reshape(B, S, -1))
    dqc = dqc.reshape(T, -1)
    dkv = jnp.concatenate([dk, dv], axis=-1).reshape(B * n_mem, -1)
    gw["w_cq"] = _mm_tn("cross_dwq", hn3, dqc, tn=D_CROSS)
    gw["w_ckv"] = _mm_tn("cross_dwkv", memn, dkv, tk=512)
    tok = emit(gw)
    dhn3 = _mm_nt("cross_dhn", dqc, wts["w_cq"], F32, after=tok)
    dmemn = _mm_nt("cross_dmem", dkv, wts["w_ckv"], F32, tm=512)
    _, _, gs["mem_norm"] = _rms_bwd("mem_dnorm", mem2, small["mem_norm"], dmemn)
    dh2, dh2b, gs["cross_norm"] = _rms_bwd("cross_dnorm", h2, small["cross_norm"], dhn3, skip=dh3)
    dcat = _mm_nt("mix_dcat", dh2b, wts["w_out"], BF16)
    gw = {"w_out": _mm_tn("mix_dwout", cat, dh2b)}
    dcat3 = dcat.reshape(B, S, -1)
    dq, dkk, dvv, dbias = _attn_bwd("attn_bwd", z, bias, dcat3)
    du, gw["w_pool"], gs["pool_scale"] = _pool_bwd("pool_bwd", d_pool, wts["w_pool"], small["pool_scale"], dcat3)
    gs["rel_bias"] = _rel_grad_from_diagonals(_bias_grad("bias_grad", dbias)[:, 0, :])
    dz = jnp.concatenate([dq, dkk, dvv, du], axis=-1).reshape(T, -1)
    gw["w_in"] = _mm_tn("mix_dwin", hn2, dz, col_blocks=N_DEV)
    tok = emit(gw)
    dhn2 = _mm_nt_cols("mix_dhn", dz, wts["w_in"], F32, after=tok)
    dh1, dh1b, gs["mix_norm"] = _rms_bwd("mix_dnorm", h1, small["mix_norm"], dhn2, skip=dh2)
    dx, _, gs["ffn1_norm"] = ffn_bwd("ffn1", dh1, dh1b, x2, hn1, g1, u1, a1, wts["ffn1_w_gate"],
                                     wts["ffn1_w_up"], wts["ffn1_w_down"], small["ffn1_norm"])
    return loss_part, dx.reshape(B, S, D), gs


def _position():
    return lax.axis_index("x"), lax.axis_index("y"), lax.axis_index("c")


def _index(p):
    return 4 * p[0] + 2 * p[1] + p[2]


HBM_SPEC = pl.BlockSpec(memory_space=pltpu.HBM)
SEM_SPEC = pl.BlockSpec(memory_space=pltpu.SEMAPHORE)
ANY_SPEC = pl.BlockSpec(memory_space=pl.ANY)
ORDERED_EFFECT = pltpu.SideEffectType.DATAFLOW_SIDE_EFFECTING


N_COPIES = {"grads": N_DEV - 1, "spread": 4, "relay": 3}


def _copies(pattern, srcs, lands, send, recv):
    x, y, c = _position()
    me, sibling = _index((x, y, c)), (x, y, 1 - c)
    chips = [(1 - x, y), (x, 1 - y), (1 - x, 1 - y)]
    if pattern == "grads":
        targets = [(x ^ (k >> 2), y ^ ((k >> 1) & 1), c ^ (k & 1)) for k in range(1, N_DEV)]
    else:
        targets = [sibling] + [(*chip, c) for chip in chips]
    per, out = N_COPIES[pattern], []
    for k in range(per):
        for a in range(len(lands)):
            if pattern == "relay":
                src = dst = lands[a].at[_index((*chips[k], c))]
                to = sibling
            else:
                to = targets[k]
                src = srcs[a].at[_index(to)] if pattern == "grads" else srcs[a]
                dst = lands[a].at[me]
            out.append(pltpu.make_async_remote_copy(src_ref=src, dst_ref=dst, send_sem=send.at[a * per + k],
                                                    recv_sem=recv.at[a * per + k], device_id=to, device_id_type=MESH))
    return out


def _comm(name, srcs, lands, wait=None, start=None, after=None):
    ns, nl = len(srcs), len(lands)
    na = ns + nl
    arrays = list(srcs) + list(lands)
    n_wait = 2 if wait else 0
    n_start = 2 if start else 0

    def body(*refs):
        ins, lnd = refs[:ns], refs[ns:na]
        if wait:
            for cp in _copies(wait[0], ins, lnd, refs[na], refs[na + 1]):
                cp.wait_send()
                cp.wait_recv()
        if start:
            outs = refs[na + n_wait + (after is not None):]
            for cp in _copies(start, ins, lnd, outs[0], outs[1]):
                cp.start()
            refs[-1][...] = jnp.zeros((8, 128), F32)

    out_shape, out_specs = [], []
    if start:
        sems = pltpu.SemaphoreType.DMA((nl * N_COPIES[start],))
        out_shape += [sems, sems]
        out_specs += [SEM_SPEC, SEM_SPEC]
    out_shape += [pltpu.HBM(a.shape, a.dtype) for a in arrays]
    out_specs += [HBM_SPEC] * na
    if start:
        out_shape.append(jax.ShapeDtypeStruct((8, 128), F32))
        out_specs.append(pl.BlockSpec(memory_space=pltpu.VMEM))
    operands = [pltpu.with_memory_space_constraint(a, pltpu.HBM) for a in arrays]
    operands += list(wait[1:]) if wait else []
    operands += [after] if after is not None else []
    res = pl.pallas_call(
        body, name=name, out_shape=out_shape, out_specs=out_specs,
        in_specs=[HBM_SPEC] * na + [SEM_SPEC] * n_wait + [ANY_SPEC] * (after is not None),
        input_output_aliases={i: n_start + i for i in range(na)},
        compiler_params=pltpu.CompilerParams(has_side_effects=ORDERED_EFFECT),
    )(*operands)
    res = list(res)
    thru = res[n_start:n_start + na]
    return thru[:ns], thru[ns:], (tuple(res[:2]) if start else None), (res[-1] if start else None)


def _adamw_math(w, g, m, v):
    m = ADAM_B1 * m + (1.0 - ADAM_B1) * g
    v = ADAM_B2 * v + (1.0 - ADAM_B2) * (g * g)
    m_hat = m / (1.0 - ADAM_B1 ** ADAM_STEP)
    v_hat = v / (1.0 - ADAM_B2 ** ADAM_STEP)
    delta = -ADAM_LR * (m_hat / (jnp.sqrt(v_hat) + ADAM_EPS) + ADAM_WD * w)
    return delta, m, v


def _adamw(name, parts, w, m, v, tr=128):
    R, C = w.shape
    tr = _tile(R, tr)

    def body(p_ref, w_ref, m_ref, v_ref, g_out, d_out, m_out, v_out):
        g = p_ref[0].astype(F32)
        for d in range(1, N_DEV):
            g = g + p_ref[d].astype(F32)
        g_out[...] = g
        d_out[...], m_out[...], v_out[...] = _adamw_math(w_ref[...], g, m_ref[...], v_ref[...])

    row = pl.BlockSpec((tr, C), lambda i: (i, 0))
    out = jax.ShapeDtypeStruct((R, C), F32)
    return pl.pallas_call(
        body, name=name, grid=(R // tr,),
        in_specs=[pl.BlockSpec((N_DEV, tr, C), lambda i: (0, i, 0)), row, row, row],
        out_specs=[row] * 4, out_shape=[out] * 4,
        compiler_params=_params(dimension_semantics=("parallel",)),
    )(parts, w, m, v)


def _small_allreduce_adamw(name, g, w, m, v):
    R = g.shape[0]

    def body(g_ref, w_ref, m_ref, v_ref, g_out, d_out, m_out, v_out, land, send, recv):
        x, y, c = _position()
        me = _index((x, y, c))
        land[me] = g_ref[...]
        copies = []
        for k in range(1, N_DEV):
            peer = (x ^ (k >> 2), y ^ ((k >> 1) & 1), c ^ (k & 1))
            copies.append(pltpu.make_async_remote_copy(
                src_ref=g_ref, dst_ref=land.at[me], send_sem=send.at[k - 1], recv_sem=recv.at[k - 1],
                device_id=peer, device_id_type=MESH))
        for cp in copies:
            cp.start()
        for cp in copies:
            cp.wait()
        total = land[0]
        for d in range(1, N_DEV):
            total = total + land[d]
        g_out[...] = total
        d_out[...], m_out[...], v_out[...] = _adamw_math(w_ref[...], total, m_ref[...], v_ref[...])

    vm = pl.BlockSpec(memory_space=pltpu.VMEM)
    out = jax.ShapeDtypeStruct((R, 128), F32)
    return pl.pallas_call(
        body, name=name, in_specs=[vm] * 4, out_specs=[vm] * 4, out_shape=[out] * 4,
        scratch_shapes=[pltpu.VMEM((N_DEV, R, 128), F32), pltpu.SemaphoreType.DMA((7,)),
                        pltpu.SemaphoreType.DMA((7,))],
    )(g, w, m, v)


BIG = ("ffn1_w_gate", "ffn1_w_up", "ffn1_w_down", "w_in", "w_pool", "w_out", "w_cq", "w_ckv", "w_co",
       "ffn2_w_gate", "ffn2_w_up", "ffn2_w_down")
SMALL = ("ffn1_norm", "mix_norm", "rel_bias", "pool_scale", "cross_norm", "mem_norm", "ffn2_norm", "final_norm")
ORDER = ("ffn1_norm", "ffn1_w_gate", "ffn1_w_up", "ffn1_w_down", "mix_norm", "w_in", "rel_bias", "w_pool",
         "pool_scale", "w_out", "cross_norm", "mem_norm", "w_cq", "w_ckv", "w_co", "ffn2_norm", "ffn2_w_gate",
         "ffn2_w_up", "ffn2_w_down", "final_norm")
ROW_SHARDED = ("ffn1_w_down", "ffn2_w_down", "w_out", "w_cq", "w_ckv")
GATHER_GROUPS = (("ffn1_w_gate", "ffn1_w_up"), ("ffn1_w_down",), ("w_in", "w_pool", "w_out"),
                 ("w_cq", "w_ckv", "w_co"), ("ffn2_w_gate", "ffn2_w_up", "ffn2_w_down"))
RELAY_BEFORE_USE = ((0,), (1, 2, 3), (), (4,), ())


def _pack(arrays):
    flat = jnp.concatenate([a.reshape(-1) for a in arrays])
    rows = -(-flat.shape[0] // 1024) * 8
    return jnp.pad(flat, (0, rows * 128 - flat.shape[0])).reshape(rows, 128)


def _unpack(packed, like):
    flat, out, at = packed.reshape(-1), [], 0
    for a in like:
        out.append(flat[at:at + a.size].reshape(a.shape))
        at += a.size
    return out


def _shard2d(a):
    a = a[0]
    return a.reshape(-1, a.shape[-1])


def kernel(x, mem, ffn1_norm, ffn1_w_gate, ffn1_w_up, ffn1_w_down, mix_norm, w_in, rel_bias, w_pool, pool_scale, w_out, cross_norm, mem_norm, w_cq, w_ckv, w_co, ffn2_norm, ffn2_w_gate, ffn2_w_up, ffn2_w_down, final_norm, loss_target, m_ffn1_norm, m_ffn1_w_gate, m_ffn1_w_up, m_ffn1_w_down, m_mix_norm, m_w_in, m_rel_bias, m_w_pool, m_pool_scale, m_w_out, m_cross_norm, m_mem_norm, m_w_cq, m_w_ckv, m_w_co, m_ffn2_norm, m_ffn2_w_gate, m_ffn2_w_up, m_ffn2_w_down, m_final_norm, v_ffn1_norm, v_ffn1_w_gate, v_ffn1_w_up, v_ffn1_w_down, v_mix_norm, v_w_in, v_rel_bias, v_w_pool, v_pool_scale, v_w_out, v_cross_norm, v_mem_norm, v_w_cq, v_w_ckv, v_w_co, v_ffn2_norm, v_ffn2_w_gate, v_ffn2_w_up, v_ffn2_w_down, v_final_norm):
    args = dict(locals())
    w_in_ = {n: args[n] for n in ORDER}
    m_in = {n: args["m_" + n] for n in ORDER}
    v_in = {n: args["v_" + n] for n in ORDER}

    me = 4 * lax.axis_index("x") + 2 * lax.axis_index("y") + lax.axis_index("c")
    n_g, rows = len(POOL_WINDOWS), POOL_GROUP // N_DEV

    def own_block_placed(block):
        return lax.dynamic_update_slice_in_dim(lax.empty((N_DEV,) + block.shape, block.dtype), block[None], me, 0)

    gathers, tok = [], None
    for gi, group in enumerate(GATHER_GROUPS):
        shards = [_shard2d(w_in_[n]).astype(BF16) for n in group]
        srcs, lands, sems, tok = _comm("gather_start_%d" % gi, shards, [own_block_placed(s) for s in shards],
                                       start="spread", after=tok)
        gathers.append((srcs, lands, sems))

    def weights(gi, after):
        for ri in RELAY_BEFORE_USE[gi]:
            srcs, lands, sems = gathers[ri]
            _, lands, sems, after = _comm("gather_relay_%d" % ri, srcs, lands, wait=("spread",) + sems,
                                          start="relay", after=after)
            gathers[ri] = (None, lands, sems)
        _, lands, sems = gathers[gi]
        _, lands, _, _ = _comm("gather_finish_%d" % gi, [], lands, wait=("relay",) + sems, after=after)
        out = {}
        for n, full in zip(GATHER_GROUPS[gi], lands):
            if n == "w_pool":
                full = full.reshape(N_DEV, n_g, rows, POOL_GROUP).transpose(1, 0, 2, 3).reshape(n_g, POOL_GROUP, POOL_GROUP)
            out[n] = full.reshape(-1, full.shape[-1]) if n in ROW_SHARDED else full
        return out

    scatters = []

    def emit(gw):
        names = list(gw)
        stacks = []
        for n in names:
            g = gw[n]
            if n == "w_pool":
                g = g.reshape(n_g, N_DEV, rows, POOL_GROUP).transpose(1, 0, 2, 3).astype(BF16)
            stacks.append(g.reshape((N_DEV,) + _shard2d(w_in_[n]).shape))
        lands = [own_block_placed(lax.dynamic_index_in_dim(s, me, 0, keepdims=False)) for s in stacks]
        srcs, lands, sems, token = _comm("grads_start_%d" % len(scatters), stacks, lands, start="grads")
        scatters.append((names, srcs, lands, sems))
        return token

    small = {n: w_in_[n].reshape(1, -1) for n in SMALL if n != "rel_bias"}
    small["rel_bias"] = rel_bias[0]
    loss_part, grad_x, gs = _local_step(x, mem, loss_target, small, weights, emit, start_token=tok)
    loss = lax.psum(loss_part[0, 0], ("x", "y", "c"))

    grad, delta, new_m, new_v = {}, {}, {}, {}
    after = grad_x
    for si, (names, srcs, lands, sems) in enumerate(scatters):
        _, landed, _, _ = _comm("grads_finish_%d" % si, srcs, lands, wait=("grads",) + sems, after=after)
        for n, parts in zip(names, landed):
            res = _adamw("adamw_" + n, parts, _shard2d(w_in_[n]), _shard2d(m_in[n]), _shard2d(v_in[n]))
            grad[n], delta[n], new_m[n], new_v[n] = [r.reshape(w_in_[n].shape) for r in res]
        after = res[0]

    like = [w_in_[n] for n in SMALL]
    gs["rel_bias"] = gs["rel_bias"].reshape(rel_bias.shape)
    res = _small_allreduce_adamw("small_params", _pack([gs[n] for n in SMALL]), _pack(like),
                                 _pack([m_in[n] for n in SMALL]), _pack([v_in[n] for n in SMALL]))
    for d, packed in zip((grad, delta, new_m, new_v), res):
        for n, a in zip(SMALL, _unpack(packed, like)):
            d[n] = a
    return (loss, grad_x, *[grad[n] for n in ORDER], *[delta[n] for n in ORDER],
            *[new_m[n] for n in ORDER], *[new_v[n] for n in ORDER])
```

```python
import functools

import jax
import jax.numpy as jnp
from jax import lax
from jax.experimental import pallas as pl
from jax.experimental.pallas import tpu as pltpu

F32 = jnp.float32
BF16 = jnp.bfloat16

N_DEV = 8
EPS = 1e-6
NEG_INF = -1e30
CHUNK = 64
LEFT_CHUNKS = 8
PAD = LEFT_CHUNKS * CHUNK
QBLK = 4 * CHUNK
KBAND = PAD + QBLK
REL_CLIP = 128
ATTN_HEADS = 16
HEAD_DIM = 64
D_ATTN = ATTN_HEADS * HEAD_DIM
POOL_WINDOWS = (2, 4, 8, 16)
POOL_GROUP = 256
D_POOL = len(POOL_WINDOWS) * POOL_GROUP
CROSS_HEADS = 4
CROSS_DIM = 128
D_CROSS = CROSS_HEADS * CROSS_DIM
FFN_RES = 0.5
ADAM_LR, ADAM_B1, ADAM_B2, ADAM_EPS, ADAM_WD, ADAM_STEP = 0.001, 0.9, 0.999, 1e-08, 0.01, 10

NN = (((1,), (0,)), ((), ()))
NT = (((1,), (1,)), ((), ()))
TN = (((0,), (0,)), ((), ()))
MESH = pl.DeviceIdType.MESH
VMEM_LIMIT = 56 * 1024 * 1024


def _params(**kw):
    return pltpu.CompilerParams(vmem_limit_bytes=VMEM_LIMIT, **kw)


def _bf(v):
    return v if v.dtype == BF16 else v.astype(BF16)


WHOLE = ((Ellipsis,), (Ellipsis,))


def _gemm(name, a, a_spec, b, b_spec, dims, grid, outs, chunks=(WHOLE,), extras=(), epilogue=None, after=None):
    nex, nout = len(extras), len(outs)
    first_out = 2 + nex + (after is not None)

    def body(*refs):
        a_ref, b_ref = refs[:2]
        total = None
        for ia, ib in chunks:
            d = lax.dot_general(_bf(a_ref[ia]), _bf(b_ref[ib]), dims, preferred_element_type=F32)
            total = d if total is None else total + d
        vals = epilogue(total, *[e[...] for e in refs[2:2 + nex]]) if epilogue is not None else (total,)
        for r, v in zip(refs[first_out:first_out + nout], vals):
            r[...] = v.astype(r.dtype)

    operands = [a, b] + [x for x, _, _ in extras]
    in_specs = [pl.BlockSpec(*a_spec), pl.BlockSpec(*b_spec)] + [pl.BlockSpec(blk, m) for _, blk, m in extras]
    if after is not None:
        operands.append(after)
        in_specs.append(pl.BlockSpec(after.shape, lambda i, j: (0, 0)))
    res = pl.pallas_call(
        body, name=name, grid=grid, in_specs=in_specs,
        out_specs=[pl.BlockSpec(blk, m) for _, _, blk, m in outs],
        out_shape=[jax.ShapeDtypeStruct(s, d) for s, d, _, _ in outs],
        compiler_params=_params(dimension_semantics=("parallel", "parallel")),
    )(*operands)
    return res[0] if nout == 1 else res


def _tile(n, want):
    for t in range(min(n, want), 15, -1):
        if n % t == 0 and t % 16 == 0:
            return t
    return n


def _mm_nn(name, a, b, out_dtype, res=None, tm=1024, tn=1024):
    M, K = a.shape
    N = b.shape[1]
    tm, tn = _tile(M, tm), _tile(N, tn)
    extras = [] if res is None else [(res, (tm, tn), lambda i, j: (i, j))]
    epi = None if res is None else (lambda t, r: (r + t,))
    return _gemm(name, a, ((tm, K), lambda i, j: (i, 0)), b, ((K, tn), lambda i, j: (0, j)), NN,
                 (M // tm, N // tn), [((M, N), out_dtype, (tm, tn), lambda i, j: (i, j))], extras=extras, epilogue=epi)


def _mm_nn_cols(name, a, bs, out_dtype, res=None, tm=1024):
    M, K = a.shape
    nb, _, w = bs.shape
    tm = _tile(M, tm)
    extras = [] if res is None else [(res, (tm, w), lambda i, j: (i, j))]
    epi = None if res is None else (lambda t, r: (r + t,))
    return _gemm(name, a, ((tm, K), lambda i, j: (i, 0)), bs, ((None, K, w), lambda i, j: (j, 0, 0)), NN,
                 (M // tm, nb), [((M, nb * w), out_dtype, (tm, w), lambda i, j: (i, j))], extras=extras, epilogue=epi)


def _mm_nt(name, a, b, out_dtype, tm=1024, tn=1024, after=None):
    M, K = a.shape
    N = b.shape[0]
    tm, tn = _tile(M, tm), _tile(N, tn)
    return _gemm(name, a, ((tm, K), lambda i, j: (i, 0)), b, ((tn, K), lambda i, j: (j, 0)), NT,
                 (M // tm, N // tn), [((M, N), out_dtype, (tm, tn), lambda i, j: (i, j))], after=after)


def _mm_nt_cols(name, a, bs, out_dtype, tm=1024, tn=512, after=None):
    M = a.shape[0]
    nb, N, w = bs.shape
    tm, tn = _tile(M, tm), _tile(N, tn)
    chunks = [((slice(None), pl.ds(c * w, w)), (c,)) for c in range(nb)]
    return _gemm(name, a, ((tm, nb * w), lambda i, j: (i, 0)), bs, ((nb, tn, w), lambda i, j: (0, j, 0)), NT,
                 (M // tm, N // tn), [((M, N), out_dtype, (tm, tn), lambda i, j: (i, j))], chunks, after=after)


def _mm_tn(name, a, b, tm=1024, tn=1024, col_blocks=None):
    T, Ka = a.shape
    Nb = b.shape[1]
    tm = _tile(Ka, tm)
    if col_blocks is None:
        tn = _tile(Nb, tn)
        out = ((Ka, Nb), BF16, (tm, tn), lambda i, j: (i, j))
    else:
        tn = Nb // col_blocks
        out = ((col_blocks, Ka, tn), BF16, (None, tm, tn), lambda i, j: (j, i, 0))
    return _gemm(name, a, ((T, tm), lambda i, j: (0, i)), b, ((T, tn), lambda i, j: (0, j)), TN,
                 (Ka // tm, Nb // tn), [out])


def _ffn_down(name, a, wd, res, tm=1024, tn=512):
    nb, M, w = a.shape
    N = wd.shape[1]
    tm, tn = _tile(M, tm), _tile(N, tn)
    chunks = [((c,), (pl.ds(c * w, w),)) for c in range(nb)]
    return _gemm(name, a, ((nb, tm, w), lambda i, j: (0, i, 0)), wd, ((nb * w, tn), lambda i, j: (0, j)), NN,
                 (M // tm, N // tn), [((M, N), F32, (tm, tn), lambda i, j: (i, j))], chunks,
                 [(res, (tm, tn), lambda i, j: (i, j))], lambda t, r: (r + FFN_RES * t,))


def _ffn_dact(name, dhb, wd, g, u, tm=1024):
    M, K = dhb.shape
    nb, _, w = g.shape
    tm = _tile(M, tm)
    tr = _tile(tm, 256)

    def body(dh_ref, wd_ref, g_ref, u_ref, dg_ref, du_ref):
        for r in range(tm // tr):
            rows = pl.ds(r * tr, tr)
            dact = lax.dot_general(dh_ref[rows, :], wd_ref[...], NT, preferred_element_type=F32)
            dg, du = _swiglu_bwd(dact, g_ref[rows, :], u_ref[rows, :])
            dg_ref[rows, :] = dg.astype(BF16)
            du_ref[rows, :] = du.astype(BF16)

    hid = pl.BlockSpec((None, tm, w), lambda i, j: (j, i, 0))
    return pl.pallas_call(
        body, name=name, grid=(M // tm, nb),
        in_specs=[pl.BlockSpec((tm, K), lambda i, j: (i, 0)), pl.BlockSpec((w, K), lambda i, j: (j, 0)), hid, hid],
        out_specs=[hid, hid], out_shape=[jax.ShapeDtypeStruct((nb, M, w), BF16)] * 2,
        compiler_params=_params(dimension_semantics=("parallel", "parallel")),
    )(dhb, wd, g, u)


def _ffn_dwd(name, a, dhb, tn=1024, after=None):
    nb, T, w = a.shape
    N = dhb.shape[1]
    tn = _tile(N, tn)
    return _gemm(name, a, ((None, T, w), lambda i, j: (i, 0, 0)), dhb, ((T, tn), lambda i, j: (0, j)), TN,
                 (nb, N // tn), [((nb * w, N), BF16, (w, tn), lambda i, j: (i, j))],
                 epilogue=lambda t: (t * FFN_RES,), after=after)


def _ffn_dwcol(name, hn, dg, tm=1024, after=None):
    T, K = hn.shape
    nb, _, w = dg.shape
    tm = _tile(K, tm)
    return _gemm(name, hn, ((T, tm), lambda j, i: (0, i)), dg, ((None, T, w), lambda j, i: (j, 0, 0)), TN,
                 (nb, K // tm), [((nb, K, w), BF16, (None, tm, w), lambda j, i: (j, i, 0))], after=after)


def _ffn_dhn(name, d, ws, res=None, tm=1024, tn=512, after=None):
    nb, M, w = d.shape
    N = ws.shape[1]
    tm, tn = _tile(M, tm), _tile(N, tn)
    extras = [] if res is None else [(res, (tm, tn), lambda i, j: (i, j))]
    epi = None if res is None else (lambda t, r: (r + t,))
    return _gemm(name, d, ((nb, tm, w), lambda i, j: (0, i, 0)), ws, ((nb, tn, w), lambda i, j: (0, j, 0)), NT,
                 (M // tm, N // tn), [((M, N), F32, (tm, tn), lambda i, j: (i, j))], [((c,), (c,)) for c in range(nb)],
                 extras, epi, after)


def _ffn_gate(name, hn, wg, tm=1024):
    M, K = hn.shape
    nb, _, w = wg.shape
    tm = _tile(M, tm)
    return _gemm(name, hn, ((tm, K), lambda i, j: (i, 0)), wg, ((None, K, w), lambda i, j: (j, 0, 0)), NN,
                 (M // tm, nb), [((nb, M, w), BF16, (None, tm, w), lambda i, j: (j, i, 0))])


def _ffn_up_act(name, hn, wu, g, tm=1024):
    M, K = hn.shape
    nb, _, w = wu.shape
    tm = _tile(M, tm)
    hid = ((None, tm, w), lambda i, j: (j, i, 0))

    def epilogue(u, gate):
        gate = gate.astype(F32)
        return u, gate * jax.nn.sigmoid(gate) * u

    return _gemm(name, hn, ((tm, K), lambda i, j: (i, 0)), wu, ((None, K, w), lambda i, j: (j, 0, 0)), NN,
                 (M // tm, nb), [((nb, M, w), BF16) + hid] * 2, extras=[(g,) + hid], epilogue=epilogue)


def _ffn_up(name, hn, wg, wu, tm=1024):
    M, K = hn.shape
    nb, _, w = wg.shape
    tm = _tile(M, tm)

    def body(a_ref, g_ref, u_ref, og, ou, oa):
        a = a_ref[...]
        g = jnp.dot(a, g_ref[...], preferred_element_type=F32)
        u = jnp.dot(a, u_ref[...], preferred_element_type=F32)
        og[...] = g.astype(BF16)
        ou[...] = u.astype(BF16)
        oa[...] = (g * jax.nn.sigmoid(g) * u).astype(BF16)

    wspec = pl.BlockSpec((None, K, w), lambda i, j: (j, 0, 0))
    ospec = pl.BlockSpec((None, tm, w), lambda i, j: (j, i, 0))
    return pl.pallas_call(
        body, name=name, grid=(M // tm, nb),
        in_specs=[pl.BlockSpec((tm, K), lambda i, j: (i, 0)), wspec, wspec],
        out_specs=[ospec] * 3, out_shape=[jax.ShapeDtypeStruct((nb, M, w), BF16)] * 3,
        compiler_params=_params(dimension_semantics=("parallel", "parallel")),
    )(hn, wg, wu)


def _swiglu_bwd(dact, g, u):
    g = g.astype(F32)
    u = u.astype(F32)
    sig = jax.nn.sigmoid(g)
    silu = g * sig
    d = FFN_RES * dact
    return d * u * (sig * (1.0 + g * (1.0 - sig))), d * silu


def _rms_fwd(name, x, gain, tr=512, after=None):
    R, D = x.shape
    tr = _tile(R, tr)

    def body(x_ref, g_ref, *rest):
        xv = x_ref[...]
        y = xv * lax.rsqrt(jnp.mean(xv * xv, axis=-1, keepdims=True) + EPS)
        rest[-1][...] = (y * g_ref[...]).astype(BF16)

    tokens = [] if after is None else [after]
    return pl.pallas_call(
        body, name=name, grid=(R // tr,),
        in_specs=[pl.BlockSpec((tr, D), lambda i: (i, 0)), pl.BlockSpec((1, D), lambda i: (0, 0))]
        + [pl.BlockSpec(t.shape, lambda i: (0, 0)) for t in tokens],
        out_specs=pl.BlockSpec((tr, D), lambda i: (i, 0)), out_shape=jax.ShapeDtypeStruct((R, D), BF16),
        compiler_params=_params(dimension_semantics=("parallel",)),
    )(x, gain, *tokens)


def _rms_bwd_math(xv, gain, dy):
    rstd = lax.rsqrt(jnp.mean(xv * xv, axis=-1, keepdims=True) + EPS)
    xhat = xv * rstd
    dxh = dy * gain
    dx = rstd * (dxh - xhat * jnp.mean(dxh * xhat, axis=-1, keepdims=True))
    return dx, jnp.sum(dy * xhat, axis=0, keepdims=True)


def _rms_bwd(name, x, gain, dy, skip=None, tr=256):
    R, D = x.shape
    tr = _tile(R, tr)
    has_skip = skip is not None

    def body(*refs):
        x_ref, g_ref, dy_ref = refs[:3]
        dx_ref, dxb_ref, dg_ref = refs[-3:]
        dx, dg = _rms_bwd_math(x_ref[...], g_ref[...], dy_ref[...].astype(F32))
        if has_skip:
            dx = dx + refs[3][...]
        dx_ref[...] = dx
        dxb_ref[...] = dx.astype(BF16)

        @pl.when(pl.program_id(0) == 0)
        def _():
            dg_ref[...] = dg

        @pl.when(pl.program_id(0) > 0)
        def _():
            dg_ref[...] += dg

    row = pl.BlockSpec((tr, D), lambda i: (i, 0))
    vec = pl.BlockSpec((1, D), lambda i: (0, 0))
    return pl.pallas_call(
        body, name=name, grid=(R // tr,),
        in_specs=[row, vec, row] + ([row] if has_skip else []),
        out_specs=[row, row, vec],
        out_shape=[jax.ShapeDtypeStruct((R, D), F32), jax.ShapeDtypeStruct((R, D), BF16),
                   jax.ShapeDtypeStruct((1, D), F32)],
        compiler_params=_params(dimension_semantics=("arbitrary",)),
    )(*([x, gain, dy] + ([skip] if has_skip else [])))


def _loss_and_grad(name, h, gain, target, tr=256):
    R, D = h.shape
    tr = _tile(R, tr)

    def body(h_ref, g_ref, t_ref, loss_ref, dh_ref, dhb_ref, dg_ref):
        hv, gain_v = h_ref[...], g_ref[...]
        y = (hv * lax.rsqrt(jnp.mean(hv * hv, axis=-1, keepdims=True) + EPS)) * gain_v
        err = y - t_ref[...]
        part = jnp.full((8, 128), 0.5 * jnp.sum(jnp.mean(err * err, axis=-1, keepdims=True)), F32)
        dh, dg = _rms_bwd_math(hv, gain_v, err * (1.0 / D))
        dh_ref[...] = dh
        dhb_ref[...] = dh.astype(BF16)

        @pl.when(pl.program_id(0) == 0)
        def _():
            dg_ref[...] = dg
            loss_ref[...] = part

        @pl.when(pl.program_id(0) > 0)
        def _():
            dg_ref[...] += dg
            loss_ref[...] += part

    row = pl.BlockSpec((tr, D), lambda i: (i, 0))
    vec = pl.BlockSpec((1, D), lambda i: (0, 0))
    return pl.pallas_call(
        body, name=name, grid=(R // tr,), in_specs=[row, vec, row],
        out_specs=[pl.BlockSpec((8, 128), lambda i: (0, 0)), row, row, vec],
        out_shape=[jax.ShapeDtypeStruct((8, 128), F32), jax.ShapeDtypeStruct((R, D), F32),
                   jax.ShapeDtypeStruct((R, D), BF16), jax.ShapeDtypeStruct((1, D), F32)],
        compiler_params=_params(dimension_semantics=("arbitrary",)),
    )(h, gain, target)


def _bias_tile(name, rel):
    width = KBAND + QBLK
    sat = rel[:, 2 * REL_CLIP:]
    n_left = PAD - REL_CLIP + 1
    row0 = jnp.concatenate([jnp.broadcast_to(sat, (ATTN_HEADS, n_left)), rel[:, :2 * REL_CLIP][:, ::-1],
                            jnp.broadcast_to(sat, (ATTN_HEADS, width - n_left - 2 * REL_CLIP))], axis=1)

    def body(e_ref, o_ref):
        rows = pltpu.roll(jnp.broadcast_to(e_ref[...], (QBLK, width)), 0, 1, stride=1, stride_axis=0)
        i = lax.broadcasted_iota(jnp.int32, (QBLK, KBAND), 0) // CHUNK
        j = lax.broadcasted_iota(jnp.int32, (QBLK, KBAND), 1) // CHUNK
        o_ref[...] = jnp.where((j >= i) & (j <= i + LEFT_CHUNKS), rows[:, :KBAND], NEG_INF)

    return pl.pallas_call(
        body, name=name, grid=(ATTN_HEADS,),
        in_specs=[pl.BlockSpec((None, 1, width), lambda h: (h, 0, 0))],
        out_specs=pl.BlockSpec((None, QBLK, KBAND), lambda h: (h, 0, 0)),
        out_shape=jax.ShapeDtypeStruct((ATTN_HEADS, QBLK, KBAND), F32),
        compiler_params=_params(dimension_semantics=("parallel",)),
    )(row0.reshape(ATTN_HEADS, 1, width))


def _band_softmax(q_half, kb, bias, kpos):
    s = lax.dot_general(q_half, kb, NT, preferred_element_type=F32) * (HEAD_DIM ** -0.5) + bias
    s = jnp.where(kpos >= 0, s, NEG_INF)
    e = jnp.exp(s - jnp.max(s, axis=-1, keepdims=True))
    return e / jnp.sum(e, axis=-1, keepdims=True)


def _fill_padded(dst, src, S):
    dst[pl.ds(0, PAD), :] = jnp.zeros((PAD, dst.shape[1]), dst.dtype)
    dst[pl.ds(PAD, S), :] = src[...].astype(dst.dtype)


def _attn_fwd(name, z, bias):
    B, S, _ = z.shape
    nh2 = ATTN_HEADS // 2

    def body(q_ref, k_ref, v_ref, b_ref, o_ref, kp, vp):
        qb = pl.program_id(2)

        @pl.when(qb == 0)
        def _():
            _fill_padded(kp, k_ref, S)
            _fill_padded(vp, v_ref, S)

        start = pl.multiple_of(qb * QBLK, QBLK)
        kb, vb = kp[pl.ds(start, KBAND), :], vp[pl.ds(start, KBAND), :]
        q = q_ref[...].astype(BF16)
        first = lax.broadcasted_iota(jnp.int32, (QBLK, 2 * HEAD_DIM), 1) < HEAD_DIM
        kpos = qb * QBLK - PAD + lax.broadcasted_iota(jnp.int32, (1, KBAND), 1)
        o = []
        for a in range(2):
            p = _band_softmax(jnp.where(first == (a == 0), q, jnp.zeros_like(q)), kb, b_ref[a], kpos)
            o.append(jnp.dot(p.astype(BF16), vb, preferred_element_type=F32))
        o_ref[...] = jnp.where(first, o[0], o[1]).astype(BF16)

    return pl.pallas_call(
        body, name=name, grid=(B, nh2, S // QBLK),
        in_specs=[pl.BlockSpec((None, QBLK, 128), lambda b, h, i: (b, i, h)),
                  pl.BlockSpec((None, S, 128), lambda b, h, i: (b, 0, nh2 + h)),
                  pl.BlockSpec((None, S, 128), lambda b, h, i: (b, 0, 2 * nh2 + h)),
                  pl.BlockSpec((2, QBLK, KBAND), lambda b, h, i: (h, 0, 0))],
        out_specs=pl.BlockSpec((None, QBLK, 128), lambda b, h, i: (b, i, h)),
        out_shape=jax.ShapeDtypeStruct((B, S, D_ATTN), BF16),
        scratch_shapes=[pltpu.VMEM((PAD + S, 128), BF16), pltpu.VMEM((PAD + S, 128), BF16)],
        compiler_params=_params(dimension_semantics=("parallel", "parallel", "arbitrary")),
    )(z, z, z, bias)


def _attn_bwd(name, z, bias, dcat):
    B, S, _ = z.shape
    nh2 = ATTN_HEADS // 2
    nqb = S // QBLK
    scale = HEAD_DIM ** -0.5

    def body(q_ref, k_ref, v_ref, b_ref, do_ref, dq_ref, dk_ref, dv_ref, db_ref, kp, vp, dka, dva):
        b, qb = pl.program_id(1), pl.program_id(2)

        @pl.when(qb == 0)
        def _():
            _fill_padded(kp, k_ref, S)
            _fill_padded(vp, v_ref, S)
            dka[...] = jnp.zeros_like(dka)
            dva[...] = jnp.zeros_like(dva)

        @pl.when((qb == 0) & (b == 0))
        def _():
            db_ref[...] = jnp.zeros_like(db_ref)

        start = pl.multiple_of(qb * QBLK, QBLK)
        band = pl.ds(start, KBAND)
        kb, vb = kp[band, :], vp[band, :]
        q = q_ref[...].astype(BF16)
        do = do_ref[...]
        first = lax.broadcasted_iota(jnp.int32, (QBLK, 2 * HEAD_DIM), 1) < HEAD_DIM
        kpos = qb * QBLK - PAD + lax.broadcasted_iota(jnp.int32, (1, KBAND), 1)
        dq, dk, dv = [], None, None
        for a in range(2):
            mine = first == (a == 0)
            qa = jnp.where(mine, q, jnp.zeros_like(q))
            doa = jnp.where(mine, do, jnp.zeros_like(do))
            p = _band_softmax(qa, kb, b_ref[a], kpos)
            dp = lax.dot_general(doa, vb, NT, preferred_element_type=F32)
            ds = p * (dp - jnp.sum(p * dp, axis=-1, keepdims=True))
            db_ref[a] += ds
            dsb = (ds * scale).astype(BF16)
            dq.append(jnp.dot(dsb, kb, preferred_element_type=F32))
            dka_part = lax.dot_general(dsb, qa, TN, preferred_element_type=F32)
            dva_part = lax.dot_general(p.astype(BF16), doa, TN, preferred_element_type=F32)
            dk = dka_part if dk is None else dk + dka_part
            dv = dva_part if dv is None else dv + dva_part
        dq_ref[...] = jnp.where(first, dq[0], dq[1]).astype(BF16)
        dka[band, :] += dk
        dva[band, :] += dv

        @pl.when(qb == nqb - 1)
        def _():
            dk_ref[...] = dka[pl.ds(PAD, S), :].astype(BF16)
            dv_ref[...] = dva[pl.ds(PAD, S), :].astype(BF16)

    qspec = pl.BlockSpec((None, QBLK, 128), lambda h, b, i: (b, i, h))
    kvout = pl.BlockSpec((None, S, 128), lambda h, b, i: (b, 0, h))
    bspec = pl.BlockSpec((2, QBLK, KBAND), lambda h, b, i: (h, 0, 0))
    act = jax.ShapeDtypeStruct((B, S, D_ATTN), BF16)
    return pl.pallas_call(
        body, name=name, grid=(nh2, B, nqb),
        in_specs=[qspec,
                  pl.BlockSpec((None, S, 128), lambda h, b, i: (b, 0, nh2 + h)),
                  pl.BlockSpec((None, S, 128), lambda h, b, i: (b, 0, 2 * nh2 + h)),
                  bspec, qspec],
        out_specs=[qspec, kvout, kvout, bspec],
        out_shape=[act, act, act, jax.ShapeDtypeStruct((ATTN_HEADS, QBLK, KBAND), F32)],
        scratch_shapes=[pltpu.VMEM((PAD + S, 128), BF16), pltpu.VMEM((PAD + S, 128), BF16),
                        pltpu.VMEM((PAD + S, 128), F32), pltpu.VMEM((PAD + S, 128), F32)],
        compiler_params=_params(dimension_semantics=("arbitrary", "arbitrary", "arbitrary")),
    )(z, z, z, bias, dcat)


def _bias_grad(name, dbias):
    width = KBAND + QBLK

    def body(d_ref, o_ref):
        acc = jnp.zeros((1, width), F32)
        for i in range(QBLK):
            row = jnp.concatenate([d_ref[pl.ds(i, 1), :], jnp.zeros((1, QBLK), F32)], axis=1)
            shift = QBLK - 1 - i
            acc = acc + (pltpu.roll(row, shift, 1) if shift else row)
        o_ref[...] = acc

    return pl.pallas_call(
        body, name=name, grid=(ATTN_HEADS,),
        in_specs=[pl.BlockSpec((None, QBLK, KBAND), lambda h: (h, 0, 0))],
        out_specs=pl.BlockSpec((None, 1, width), lambda h: (h, 0, 0)),
        out_shape=jax.ShapeDtypeStruct((ATTN_HEADS, 1, width), F32),
        compiler_params=_params(dimension_semantics=("parallel",)),
    )(dbias)


def _rel_grad_from_diagonals(diag):
    top = PAD + QBLK - 1 - REL_CLIP
    sat = jnp.sum(diag[:, :top + 1], axis=1, keepdims=True)
    mid = diag[:, top + 1:top + 2 * REL_CLIP][:, ::-1]
    return jnp.concatenate([jnp.zeros_like(sat), mid, sat], axis=1)


def _shift_rows(x, k, forward):
    S = x.shape[0]
    t = lax.broadcasted_iota(jnp.int32, x.shape, 0)
    if forward:
        return jnp.where(t < S - k, pltpu.roll(x, S - k, 0), 0.0)
    return jnp.where(t >= k, pltpu.roll(x, k, 0), 0.0)


def _window_sum(x, g, forward):
    s = x + _shift_rows(x, 1, forward)
    out = s
    for n, k in enumerate((2, 4, 8)):
        s = s + _shift_rows(s, k, forward)
        out = jnp.where(g > n, s, out)
    return out


def _pool_count(S, g):
    t = lax.broadcasted_iota(jnp.int32, (S, 1), 0)
    w = jnp.left_shift(2, g)
    return jnp.minimum(t + 1, w).astype(F32)


def _pool_fwd(name, z, wp, pscale):
    B, S, _ = z.shape
    c0 = 3 * D_ATTN // POOL_GROUP

    def body(u_ref, w_ref, s_ref, d_ref, y_ref):
        g = pl.program_id(1)
        u = u_ref[...]
        d = (_window_sum(u, g, False) / _pool_count(S, g) - u).astype(BF16)
        d_ref[...] = d
        y_ref[...] = (jnp.dot(d, w_ref[...], preferred_element_type=F32) * s_ref[...]).astype(BF16)

    blk = pl.BlockSpec((None, S, POOL_GROUP), lambda b, g: (b, 0, g))
    out = jax.ShapeDtypeStruct((B, S, D_POOL), BF16)
    return pl.pallas_call(
        body, name=name, grid=(B, len(POOL_WINDOWS)),
        in_specs=[pl.BlockSpec((None, S, POOL_GROUP), lambda b, g: (b, 0, c0 + g)),
                  pl.BlockSpec((None, POOL_GROUP, POOL_GROUP), lambda b, g: (g, 0, 0)),
                  pl.BlockSpec((1, POOL_GROUP), lambda b, g: (0, g))],
        out_specs=[blk, blk], out_shape=[out, out],
        compiler_params=_params(dimension_semantics=("parallel", "parallel")),
    )(z, wp, pscale)


def _pool_bwd(name, d, wp, pscale, dcat):
    B, S, _ = d.shape
    c0 = D_ATTN // POOL_GROUP

    def body(d_ref, w_ref, s_ref, dy_ref, du_ref, dw_ref, dsc_ref):
        g, b = pl.program_id(0), pl.program_id(1)
        dv = d_ref[...]
        dy = dy_ref[...].astype(F32)
        w = w_ref[...]
        ypre = jnp.dot(dv, w, preferred_element_type=F32)
        dyp = (dy * s_ref[...]).astype(BF16)
        dd = lax.dot_general(dyp, w, NT, preferred_element_type=F32)
        du_ref[...] = (_window_sum(dd / _pool_count(S, g), g, True) - dd).astype(BF16)
        dw = lax.dot_general(dv, dyp, TN, preferred_element_type=F32)
        dsc = jnp.sum(dy * ypre, axis=0, keepdims=True)

        @pl.when(b == 0)
        def _():
            dw_ref[...] = dw
            dsc_ref[...] = dsc

        @pl.when(b > 0)
        def _():
            dw_ref[...] += dw
            dsc_ref[...] += dsc

    blk = pl.BlockSpec((None, S, POOL_GROUP), lambda g, b: (b, 0, g))
    wspec = pl.BlockSpec((None, POOL_GROUP, POOL_GROUP), lambda g, b: (g, 0, 0))
    sspec = pl.BlockSpec((1, POOL_GROUP), lambda g, b: (0, g))
    return pl.pallas_call(
        body, name=name, grid=(len(POOL_WINDOWS), B),
        in_specs=[blk, wspec, sspec, pl.BlockSpec((None, S, POOL_GROUP), lambda g, b: (b, 0, c0 + g))],
        out_specs=[blk, wspec, sspec],
        out_shape=[jax.ShapeDtypeStruct((B, S, D_POOL), BF16),
                   jax.ShapeDtypeStruct((len(POOL_WINDOWS), POOL_GROUP, POOL_GROUP), F32),
                   jax.ShapeDtypeStruct((1, D_POOL), F32)],
        compiler_params=_params(dimension_semantics=("arbitrary", "arbitrary")),
    )(d, wp, pscale, dcat)


def _cross_softmax(q, k):
    s = lax.dot_general(q, k, NT, preferred_element_type=F32) * (CROSS_DIM ** -0.5)
    e = jnp.exp(s - jnp.max(s, axis=-1, keepdims=True))
    return e / jnp.sum(e, axis=-1, keepdims=True)


def _cross_fwd(name, qc, kv, tq=512):
    B, S, _ = qc.shape
    M = kv.shape[1]
    tq = _tile(S, tq)

    def body(q_ref, k_ref, v_ref, o_ref):
        p = _cross_softmax(q_ref[...], k_ref[...])
        o_ref[...] = jnp.dot(p.astype(BF16), v_ref[...], preferred_element_type=F32).astype(BF16)

    qspec = pl.BlockSpec((None, tq, CROSS_DIM), lambda b, h, i: (b, i, h))
    return pl.pallas_call(
        body, name=name, grid=(B, CROSS_HEADS, S // tq),
        in_specs=[qspec, pl.BlockSpec((None, M, CROSS_DIM), lambda b, h, i: (b, 0, h)),
                  pl.BlockSpec((None, M, CROSS_DIM), lambda b, h, i: (b, 0, CROSS_HEADS + h))],
        out_specs=qspec, out_shape=jax.ShapeDtypeStruct((B, S, D_CROSS), BF16),
        compiler_params=_params(dimension_semantics=("parallel", "parallel", "parallel")),
    )(qc, kv, kv)


def _cross_bwd(name, qc, kv, do, tq=512):
    B, S, _ = qc.shape
    M = kv.shape[1]
    tq = _tile(S, tq)
    nq = S // tq
    scale = CROSS_DIM ** -0.5

    def body(q_ref, k_ref, v_ref, do_ref, dq_ref, dk_ref, dv_ref, dka, dva):
        i = pl.program_id(2)
        q, k, v, dov = q_ref[...], k_ref[...], v_ref[...], do_ref[...]
        p = _cross_softmax(q, k)
        dp = lax.dot_general(dov, v, NT, preferred_element_type=F32)
        ds = ((p * (dp - jnp.sum(p * dp, axis=-1, keepdims=True))) * scale).astype(BF16)
        dq_ref[...] = jnp.dot(ds, k, preferred_element_type=F32).astype(BF16)
        dk = lax.dot_general(ds, q, TN, preferred_element_type=F32)
        dv = lax.dot_general(p.astype(BF16), dov, TN, preferred_element_type=F32)

        @pl.when(i == 0)
        def _():
            dka[...] = dk
            dva[...] = dv

        @pl.when(i > 0)
        def _():
            dka[...] += dk
            dva[...] += dv

        @pl.when(i == nq - 1)
        def _():
            dk_ref[...] = dka[...].astype(BF16)
            dv_ref[...] = dva[...].astype(BF16)

    qspec = pl.BlockSpec((None, tq, CROSS_DIM), lambda b, h, i: (b, i, h))
    kspec = pl.BlockSpec((None, M, CROSS_DIM), lambda b, h, i: (b, 0, h))
    return pl.pallas_call(
        body, name=name, grid=(B, CROSS_HEADS, nq),
        in_specs=[qspec, kspec, pl.BlockSpec((None, M, CROSS_DIM), lambda b, h, i: (b, 0, CROSS_HEADS + h)), qspec],
        out_specs=[qspec, kspec, kspec],
        out_shape=[jax.ShapeDtypeStruct((B, S, D_CROSS), BF16), jax.ShapeDtypeStruct((B, M, D_CROSS), BF16),
                   jax.ShapeDtypeStruct((B, M, D_CROSS), BF16)],
        scratch_shapes=[pltpu.VMEM((M, CROSS_DIM), F32), pltpu.VMEM((M, CROSS_DIM), F32)],
        compiler_params=_params(dimension_semantics=("parallel", "parallel", "arbitrary")),
    )(qc, kv, kv, do)


def _local_step(x, mem, target, small, weights, emit, start_token=None):
    B, S, D = x.shape
    T = B * S
    x2, t2 = x.reshape(T, D), target.reshape(T, D)
    mem2 = mem.reshape(-1, D)
    n_mem = mem.shape[1]
    wts = {}

    hn1 = _rms_fwd("norm_ffn1", x2, small["ffn1_norm"], after=start_token)
    wts.update(weights(0, hn1))
    g1 = _ffn_gate("ffn1_gate", hn1, wts["ffn1_w_gate"])
    wts.update(weights(1, g1))
    u1, a1 = _ffn_up_act("ffn1_up", hn1, wts["ffn1_w_up"], g1)
    wts.update(weights(2, a1))
    h1 = _ffn_down("ffn1_down", a1, wts["ffn1_w_down"], x2)
    hn2 = _rms_fwd("norm_mix", h1, small["mix_norm"])
    wts.update(weights(3, hn2))
    z = _mm_nn_cols("mix_in", hn2, wts["w_in"], F32).reshape(B, S, -1)
    bias = _bias_tile("bias_tile", small["rel_bias"])
    y_attn = _attn_fwd("attn_fwd", z, bias)
    d_pool, y_pool = _pool_fwd("pool_fwd", z, wts["w_pool"], small["pool_scale"])
    cat = jnp.concatenate([y_attn, y_pool], axis=-1).reshape(T, -1)
    h2 = _mm_nn("mix_out", cat, wts["w_out"], F32, res=h1)
    hn3 = _rms_fwd("norm_cross", h2, small["cross_norm"])
    memn = _rms_fwd("norm_mem", mem2, small["mem_norm"])
    wts.update(weights(4, hn3))
    qc = _mm_nn("cross_q", hn3, wts["w_cq"], BF16)
    kv = _mm_nn("cross_kv", memn, wts["w_ckv"], BF16)
    o = _cross_fwd("cross_fwd", qc.reshape(B, S, -1), kv.reshape(B, n_mem, -1)).reshape(T, -1)
    h3 = _mm_nn_cols("cross_out", o, wts["w_co"], F32, res=h2)
    hn4 = _rms_fwd("norm_ffn2", h3, small["ffn2_norm"])
    wts.update(weights(5, hn4))
    g2, u2, a2 = _ffn_up("ffn2_up", hn4, wts["ffn2_w_gate"], wts["ffn2_w_up"])
    h4 = _ffn_down("ffn2_down", a2, wts["ffn2_w_down"], h3)

    gs = {}
    loss_part, dh4, dh4b, gs["final_norm"] = _loss_and_grad("loss", h4, small["final_norm"], t2)

    def ffn_bwd(tag, dh, dhb, h_in, hn, g, u, a, wg, wu, wd, gain):
        dg, du = _ffn_dact(tag + "_dact", dhb, wd, g, u)
        tok = emit({tag + "_w_gate": _ffn_dwcol(tag + "_dwg", hn, dg)})
        tok = emit({tag + "_w_up": _ffn_dwcol(tag + "_dwu", hn, du, after=tok)})
        tok = emit({tag + "_w_down": _ffn_dwd(tag + "_dwd", a, dhb, after=tok)})
        dhn = _ffn_dhn(tag + "_dhn_g", dg, wg, after=tok)
        dhn = _ffn_dhn(tag + "_dhn_u", du, wu, res=dhn)
        return _rms_bwd(tag + "_dnorm", h_in, gain, dhn, skip=dh)

    dh3, dh3b, gs["ffn2_norm"] = ffn_bwd("ffn2", dh4, dh4b, h3, hn4, g2, u2, a2, wts["ffn2_w_gate"],
                                         wts["ffn2_w_up"], wts["ffn2_w_down"], small["ffn2_norm"])
    do = _mm_nt_cols("cross_do", dh3b, wts["w_co"], BF16, tn=D_CROSS)
    gw = {"w_co": _mm_tn("cross_dwo", o, dh3b, tm=D_CROSS, col_blocks=N_DEV)}
    dqc, dk, dv = _cross_bwd("cross_bwd", qc.reshape(B, S, -1), kv.reshape(B, n_mem, -1), do.reshape(B, S, -1))
    dqc = dqc.reshape(T, -1)
    dkv = jnp.concatenate([dk, dv], axis=-1).reshape(B * n_mem, -1)
    gw["w_cq"] = _mm_tn("cross_dwq", hn3, dqc, tn=D_CROSS)
    gw["w_ckv"] = _mm_tn("cross_dwkv", memn, dkv)
    tok = emit(gw)
    dhn3 = _mm_nt("cross_dhn", dqc, wts["w_cq"], F32, after=tok)
    dmemn = _mm_nt("cross_dmem", dkv, wts["w_ckv"], F32, tm=512)
    _, _, gs["mem_norm"] = _rms_bwd("mem_dnorm", mem2, small["mem_norm"], dmemn)
    dh2, dh2b, gs["cross_norm"] = _rms_bwd("cross_dnorm", h2, small["cross_norm"], dhn3, skip=dh3)
    dcat = _mm_nt("mix_dcat", dh2b, wts["w_out"], BF16)
    gw = {"w_out": _mm_tn("mix_dwout", cat, dh2b)}
    dcat3 = dcat.reshape(B, S, -1)
    dq, dkk, dvv, dbias = _attn_bwd("attn_bwd", z, bias, dcat3)
    du, gw["w_pool"], gs["pool_scale"] = _pool_bwd("pool_bwd", d_pool, wts["w_pool"], small["pool_scale"], dcat3)
    gs["rel_bias"] = _rel_grad_from_diagonals(_bias_grad("bias_grad", dbias)[:, 0, :])
    dz = jnp.concatenate([dq, dkk, dvv, du], axis=-1).reshape(T, -1)
    gw["w_in"] = _mm_tn("mix_dwin", hn2, dz, col_blocks=N_DEV)
    tok = emit(gw)
    dhn2 = _mm_nt_cols("mix_dhn", dz, wts["w_in"], F32, after=tok)
    dh1, dh1b, gs["mix_norm"] = _rms_bwd("mix_dnorm", h1, small["mix_norm"], dhn2, skip=dh2)
    dx, _, gs["ffn1_norm"] = ffn_bwd("ffn1", dh1, dh1b, x2, hn1, g1, u1, a1, wts["ffn1_w_gate"],
                                     wts["ffn1_w_up"], wts["ffn1_w_down"], small["ffn1_norm"])
    return loss_part, dx.reshape(B, S, D), gs


def _position():
    return lax.axis_index("x"), lax.axis_index("y"), lax.axis_index("c")


def _index(p):
    return 4 * p[0] + 2 * p[1] + p[2]


HBM_SPEC = pl.BlockSpec(memory_space=pltpu.HBM)
SEM_SPEC = pl.BlockSpec(memory_space=pltpu.SEMAPHORE)
ANY_SPEC = pl.BlockSpec(memory_space=pl.ANY)
ORDERED_EFFECT = pltpu.SideEffectType.DATAFLOW_SIDE_EFFECTING


N_COPIES = {"grads": N_DEV - 1, "spread": 4, "relay": 3}


def _copies(pattern, srcs, lands, send, recv):
    x, y, c = _position()
    me, sibling = _index((x, y, c)), (x, y, 1 - c)
    chips = [(1 - x, y), (x, 1 - y), (1 - x, 1 - y)]
    if pattern == "grads":
        targets = [(x ^ (k >> 2), y ^ ((k >> 1) & 1), c ^ (k & 1)) for k in range(1, N_DEV)]
    else:
        targets = [sibling] + [(*chip, c) for chip in chips]
    per, out = N_COPIES[pattern], []
    for k in range(per):
        for a in range(len(lands)):
            if pattern == "relay":
                src = dst = lands[a].at[_index((*chips[k], c))]
                to = sibling
            else:
                to = targets[k]
                src = srcs[a].at[_index(to)] if pattern == "grads" else srcs[a]
                dst = lands[a].at[me]
            out.append(pltpu.make_async_remote_copy(src_ref=src, dst_ref=dst, send_sem=send.at[a * per + k],
                                                    recv_sem=recv.at[a * per + k], device_id=to, device_id_type=MESH))
    return out


def _comm(name, srcs, lands, wait=None, start=None, after=None):
    ns, nl = len(srcs), len(lands)
    na = ns + nl
    arrays = list(srcs) + list(lands)
    n_wait = 2 if wait else 0
    n_start = 2 if start else 0

    def body(*refs):
        ins, lnd = refs[:ns], refs[ns:na]
        if wait:
            for cp in _copies(wait[0], ins, lnd, refs[na], refs[na + 1]):
                cp.wait_send()
                cp.wait_recv()
        if start:
            outs = refs[na + n_wait + (after is not None):]
            for cp in _copies(start, ins, lnd, outs[0], outs[1]):
                cp.start()
            refs[-1][...] = jnp.zeros((8, 128), F32)

    out_shape, out_specs = [], []
    if start:
        sems = pltpu.SemaphoreType.DMA((nl * N_COPIES[start],))
        out_shape += [sems, sems]
        out_specs += [SEM_SPEC, SEM_SPEC]
    out_shape += [pltpu.HBM(a.shape, a.dtype) for a in arrays]
    out_specs += [HBM_SPEC] * na
    if start:
        out_shape.append(jax.ShapeDtypeStruct((8, 128), F32))
        out_specs.append(pl.BlockSpec(memory_space=pltpu.VMEM))
    operands = [pltpu.with_memory_space_constraint(a, pltpu.HBM) for a in arrays]
    operands += list(wait[1:]) if wait else []
    operands += [after] if after is not None else []
    res = pl.pallas_call(
        body, name=name, out_shape=out_shape, out_specs=out_specs,
        in_specs=[HBM_SPEC] * na + [SEM_SPEC] * n_wait + [ANY_SPEC] * (after is not None),
        input_output_aliases={i: n_start + i for i in range(na)},
        compiler_params=pltpu.CompilerParams(has_side_effects=ORDERED_EFFECT),
    )(*operands)
    res = list(res)
    thru = res[n_start:n_start + na]
    return thru[:ns], thru[ns:], (tuple(res[:2]) if start else None), (res[-1] if start else None)


def _adamw_math(w, g, m, v):
    m = ADAM_B1 * m + (1.0 - ADAM_B1) * g
    v = ADAM_B2 * v + (1.0 - ADAM_B2) * (g * g)
    m_hat = m / (1.0 - ADAM_B1 ** ADAM_STEP)
    v_hat = v / (1.0 - ADAM_B2 ** ADAM_STEP)
    delta = -ADAM_LR * (m_hat / (jnp.sqrt(v_hat) + ADAM_EPS) + ADAM_WD * w)
    return delta, m, v


def _adamw(name, parts, w, m, v, tr=128):
    R, C = w.shape
    tr = _tile(R, tr)

    def body(p_ref, w_ref, m_ref, v_ref, g_out, d_out, m_out, v_out):
        g = p_ref[0].astype(F32)
        for d in range(1, N_DEV):
            g = g + p_ref[d].astype(F32)
        g_out[...] = g
        d_out[...], m_out[...], v_out[...] = _adamw_math(w_ref[...], g, m_ref[...], v_ref[...])

    row = pl.BlockSpec((tr, C), lambda i: (i, 0))
    out = jax.ShapeDtypeStruct((R, C), F32)
    return pl.pallas_call(
        body, name=name, grid=(R // tr,),
        in_specs=[pl.BlockSpec((N_DEV, tr, C), lambda i: (0, i, 0)), row, row, row],
        out_specs=[row] * 4, out_shape=[out] * 4,
        compiler_params=_params(dimension_semantics=("parallel",)),
    )(parts, w, m, v)


def _small_allreduce_adamw(name, g, w, m, v):
    R = g.shape[0]

    def body(g_ref, w_ref, m_ref, v_ref, g_out, d_out, m_out, v_out, land, send, recv):
        x, y, c = _position()
        me = _index((x, y, c))
        land[me] = g_ref[...]
        copies = []
        for k in range(1, N_DEV):
            peer = (x ^ (k >> 2), y ^ ((k >> 1) & 1), c ^ (k & 1))
            copies.append(pltpu.make_async_remote_copy(
                src_ref=g_ref, dst_ref=land.at[me], send_sem=send.at[k - 1], recv_sem=recv.at[k - 1],
                device_id=peer, device_id_type=MESH))
        for cp in copies:
            cp.start()
        for cp in copies:
            cp.wait()
        total = land[0]
        for d in range(1, N_DEV):
            total = total + land[d]
        g_out[...] = total
        d_out[...], m_out[...], v_out[...] = _adamw_math(w_ref[...], total, m_ref[...], v_ref[...])

    vm = pl.BlockSpec(memory_space=pltpu.VMEM)
    out = jax.ShapeDtypeStruct((R, 128), F32)
    return pl.pallas_call(
        body, name=name, in_specs=[vm] * 4, out_specs=[vm] * 4, out_shape=[out] * 4,
        scratch_shapes=[pltpu.VMEM((N_DEV, R, 128), F32), pltpu.SemaphoreType.DMA((7,)),
                        pltpu.SemaphoreType.DMA((7,))],
    )(g, w, m, v)


BIG = ("ffn1_w_gate", "ffn1_w_up", "ffn1_w_down", "w_in", "w_pool", "w_out", "w_cq", "w_ckv", "w_co",
       "ffn2_w_gate", "ffn2_w_up", "ffn2_w_down")
SMALL = ("ffn1_norm", "mix_norm", "rel_bias", "pool_scale", "cross_norm", "mem_norm", "ffn2_norm", "final_norm")
ORDER = ("ffn1_norm", "ffn1_w_gate", "ffn1_w_up", "ffn1_w_down", "mix_norm", "w_in", "rel_bias", "w_pool",
         "pool_scale", "w_out", "cross_norm", "mem_norm", "w_cq", "w_ckv", "w_co", "ffn2_norm", "ffn2_w_gate",
         "ffn2_w_up", "ffn2_w_down", "final_norm")
ROW_SHARDED = ("ffn1_w_down", "ffn2_w_down", "w_out", "w_cq", "w_ckv")
GATHER_GROUPS = (("ffn1_w_gate",), ("ffn1_w_up",), ("ffn1_w_down",), ("w_in", "w_pool", "w_out"),
                 ("w_cq", "w_ckv", "w_co"), ("ffn2_w_gate", "ffn2_w_up", "ffn2_w_down"))
RELAY_BEFORE_USE = ((0,), (1,), (2, 3), (4,), (5,), ())


def _pack(arrays):
    flat = jnp.concatenate([a.reshape(-1) for a in arrays])
    rows = -(-flat.shape[0] // 1024) * 8
    return jnp.pad(flat, (0, rows * 128 - flat.shape[0])).reshape(rows, 128)


def _unpack(packed, like):
    flat, out, at = packed.reshape(-1), [], 0
    for a in like:
        out.append(flat[at:at + a.size].reshape(a.shape))
        at += a.size
    return out


def _shard2d(a):
    a = a[0]
    return a.reshape(-1, a.shape[-1])


def kernel(x, mem, ffn1_norm, ffn1_w_gate, ffn1_w_up, ffn1_w_down, mix_norm, w_in, rel_bias, w_pool, pool_scale, w_out, cross_norm, mem_norm, w_cq, w_ckv, w_co, ffn2_norm, ffn2_w_gate, ffn2_w_up, ffn2_w_down, final_norm, loss_target, m_ffn1_norm, m_ffn1_w_gate, m_ffn1_w_up, m_ffn1_w_down, m_mix_norm, m_w_in, m_rel_bias, m_w_pool, m_pool_scale, m_w_out, m_cross_norm, m_mem_norm, m_w_cq, m_w_ckv, m_w_co, m_ffn2_norm, m_ffn2_w_gate, m_ffn2_w_up, m_ffn2_w_down, m_final_norm, v_ffn1_norm, v_ffn1_w_gate, v_ffn1_w_up, v_ffn1_w_down, v_mix_norm, v_w_in, v_rel_bias, v_w_pool, v_pool_scale, v_w_out, v_cross_norm, v_mem_norm, v_w_cq, v_w_ckv, v_w_co, v_ffn2_norm, v_ffn2_w_gate, v_ffn2_w_up, v_ffn2_w_down, v_final_norm):
    args = dict(locals())
    w_in_ = {n: args[n] for n in ORDER}
    m_in = {n: args["m_" + n] for n in ORDER}
    v_in = {n: args["v_" + n] for n in ORDER}

    me = 4 * lax.axis_index("x") + 2 * lax.axis_index("y") + lax.axis_index("c")
    n_g, rows = len(POOL_WINDOWS), POOL_GROUP // N_DEV

    def own_block_placed(block):
        return lax.dynamic_update_slice_in_dim(lax.empty((N_DEV,) + block.shape, block.dtype), block[None], me, 0)

    gathers, tok = [], None
    for gi, group in enumerate(GATHER_GROUPS):
        shards = [_shard2d(w_in_[n]).astype(BF16) for n in group]
        srcs, lands, sems, tok = _comm("gather_start_%d" % gi, shards, [own_block_placed(s) for s in shards],
                                       start="spread", after=tok)
        gathers.append((srcs, lands, sems))

    def weights(gi, after):
        for ri in RELAY_BEFORE_USE[gi]:
            srcs, lands, sems = gathers[ri]
            _, lands, sems, after = _comm("gather_relay_%d" % ri, srcs, lands, wait=("spread",) + sems,
                                          start="relay", after=after)
            gathers[ri] = (None, lands, sems)
        _, lands, sems = gathers[gi]
        _, lands, _, _ = _comm("gather_finish_%d" % gi, [], lands, wait=("relay",) + sems, after=after)
        out = {}
        for n, full in zip(GATHER_GROUPS[gi], lands):
            if n == "w_pool":
                full = full.reshape(N_DEV, n_g, rows, POOL_GROUP).transpose(1, 0, 2, 3).reshape(n_g, POOL_GROUP, POOL_GROUP)
            out[n] = full.reshape(-1, full.shape[-1]) if n in ROW_SHARDED else full
        return out

    scatters = []

    def emit(gw):
        names = list(gw)
        stacks = []
        for n in names:
            g = gw[n]
            if n == "w_pool":
                g = g.reshape(n_g, N_DEV, rows, POOL_GROUP).transpose(1, 0, 2, 3).astype(BF16)
            stacks.append(g.reshape((N_DEV,) + _shard2d(w_in_[n]).shape))
        lands = [own_block_placed(lax.dynamic_index_in_dim(s, me, 0, keepdims=False)) for s in stacks]
        srcs, lands, sems, token = _comm("grads_start_%d" % len(scatters), stacks, lands, start="grads")
        scatters.append((names, srcs, lands, sems))
        return token

    small = {n: w_in_[n].reshape(1, -1) for n in SMALL if n != "rel_bias"}
    small["rel_bias"] = rel_bias[0]
    loss_part, grad_x, gs = _local_step(x, mem, loss_target, small, weights, emit, start_token=tok)
    loss = lax.psum(loss_part[0, 0], ("x", "y", "c"))

    grad, delta, new_m, new_v = {}, {}, {}, {}
    after = grad_x
    for si, (names, srcs, lands, sems) in enumerate(scatters):
        _, landed, _, _ = _comm("grads_finish_%d" % si, srcs, lands, wait=("grads",) + sems, after=after)
        for n, parts in zip(names, landed):
            res = _adamw("adamw_" + n, parts, _shard2d(w_in_[n]), _shard2d(m_in[n]), _shard2d(v_in[n]))
            grad[n], delta[n], new_m[n], new_v[n] = [r.reshape(w_in_[n].shape) for r in res]
        after = res[0]

    like = [w_in_[n] for n in SMALL]
    gs["rel_bias"] = gs["rel_bias"].reshape(rel_bias.shape)
    res = _small_allreduce_adamw("small_params", _pack([gs[n] for n in SMALL]), _pack(like),
                                 _pack([m_in[n] for n in SMALL]), _pack([v_in[n] for n in SMALL]))
    for d, packed in zip((grad, delta, new_m, new_v), res):
        for n, a in zip(SMALL, _unpack(packed, like)):
            d[n] = a
    return (loss, grad_x, *[grad[n] for n in ORDER], *[delta[n] for n in ORDER],
            *[new_m[n] for n in ORDER], *[new_v[n] for n in ORDER])
```

```python
import functools

import jax
import jax.numpy as jnp
from jax import lax
from jax.experimental import pallas as pl
from jax.experimental.pallas import tpu as pltpu

F32 = jnp.float32
BF16 = jnp.bfloat16

N_DEV = 8
EPS = 1e-6
NEG_INF = -1e30
CHUNK = 64
LEFT_CHUNKS = 8
PAD = LEFT_CHUNKS * CHUNK
QBLK = 4 * CHUNK
KBAND = PAD + QBLK
REL_CLIP = 128
ATTN_HEADS = 16
HEAD_DIM = 64
D_ATTN = ATTN_HEADS * HEAD_DIM
POOL_WINDOWS = (2, 4, 8, 16)
POOL_GROUP = 256
D_POOL = len(POOL_WINDOWS) * POOL_GROUP
CROSS_HEADS = 4
CROSS_DIM = 128
D_CROSS = CROSS_HEADS * CROSS_DIM
FFN_RES = 0.5
ADAM_LR, ADAM_B1, ADAM_B2, ADAM_EPS, ADAM_WD, ADAM_STEP = 0.001, 0.9, 0.999, 1e-08, 0.01, 10

NN = (((1,), (0,)), ((), ()))
NT = (((1,), (1,)), ((), ()))
TN = (((0,), (0,)), ((), ()))
MESH = pl.DeviceIdType.MESH
VMEM_LIMIT = 56 * 1024 * 1024


def _params(**kw):
    return pltpu.CompilerParams(vmem_limit_bytes=VMEM_LIMIT, **kw)


def _bf(v):
    return v if v.dtype == BF16 else v.astype(BF16)


WHOLE = ((Ellipsis,), (Ellipsis,))


def _gemm(name, a, a_spec, b, b_spec, dims, grid, outs, chunks=(WHOLE,), extras=(), epilogue=None, after=None):
    nex, nout = len(extras), len(outs)
    first_out = 2 + nex + (after is not None)

    def body(*refs):
        a_ref, b_ref = refs[:2]
        total = None
        for ia, ib in chunks:
            d = lax.dot_general(_bf(a_ref[ia]), _bf(b_ref[ib]), dims, preferred_element_type=F32)
            total = d if total is None else total + d
        vals = epilogue(total, *[e[...] for e in refs[2:2 + nex]]) if epilogue is not None else (total,)
        for r, v in zip(refs[first_out:first_out + nout], vals):
            r[...] = v.astype(r.dtype)

    operands = [a, b] + [x for x, _, _ in extras]
    in_specs = [pl.BlockSpec(*a_spec), pl.BlockSpec(*b_spec)] + [pl.BlockSpec(blk, m) for _, blk, m in extras]
    if after is not None:
        operands.append(after)
        in_specs.append(pl.BlockSpec(after.shape, lambda i, j: (0, 0)))
    res = pl.pallas_call(
        body, name=name, grid=grid, in_specs=in_specs,
        out_specs=[pl.BlockSpec(blk, m) for _, _, blk, m in outs],
        out_shape=[jax.ShapeDtypeStruct(s, d) for s, d, _, _ in outs],
        compiler_params=_params(dimension_semantics=("parallel", "parallel")),
    )(*operands)
    return res[0] if nout == 1 else res


def _tile(n, want):
    for t in range(min(n, want), 15, -1):
        if n % t == 0 and t % 16 == 0:
            return t
    return n


def _mm_nn(name, a, b, out_dtype, res=None, tm=1024, tn=1024):
    M, K = a.shape
    N = b.shape[1]
    tm, tn = _tile(M, tm), _tile(N, tn)
    extras = [] if res is None else [(res, (tm, tn), lambda i, j: (i, j))]
    epi = None if res is None else (lambda t, r: (r + t,))
    return _gemm(name, a, ((tm, K), lambda i, j: (i, 0)), b, ((K, tn), lambda i, j: (0, j)), NN,
                 (M // tm, N // tn), [((M, N), out_dtype, (tm, tn), lambda i, j: (i, j))], extras=extras, epilogue=epi)


def _mm_nn_cols(name, a, bs, out_dtype, res=None, tm=1024):
    M, K = a.shape
    nb, _, w = bs.shape
    tm = _tile(M, tm)
    extras = [] if res is None else [(res, (tm, w), lambda i, j: (i, j))]
    epi = None if res is None else (lambda t, r: (r + t,))
    return _gemm(name, a, ((tm, K), lambda i, j: (i, 0)), bs, ((None, K, w), lambda i, j: (j, 0, 0)), NN,
                 (M // tm, nb), [((M, nb * w), out_dtype, (tm, w), lambda i, j: (i, j))], extras=extras, epilogue=epi)


def _mm_nt(name, a, b, out_dtype, tm=1024, tn=1024, after=None):
    M, K = a.shape
    N = b.shape[0]
    tm, tn = _tile(M, tm), _tile(N, tn)
    return _gemm(name, a, ((tm, K), lambda i, j: (i, 0)), b, ((tn, K), lambda i, j: (j, 0)), NT,
                 (M // tm, N // tn), [((M, N), out_dtype, (tm, tn), lambda i, j: (i, j))], after=after)


def _mm_nt_cols(name, a, bs, out_dtype, tm=1024, tn=512, after=None):
    M = a.shape[0]
    nb, N, w = bs.shape
    tm, tn = _tile(M, tm), _tile(N, tn)
    chunks = [((slice(None), pl.ds(c * w, w)), (c,)) for c in range(nb)]
    return _gemm(name, a, ((tm, nb * w), lambda i, j: (i, 0)), bs, ((nb, tn, w), lambda i, j: (0, j, 0)), NT,
                 (M // tm, N // tn), [((M, N), out_dtype, (tm, tn), lambda i, j: (i, j))], chunks, after=after)


def _mm_tn(name, a, b, tm=1024, tn=1024, col_blocks=None):
    T, Ka = a.shape
    Nb = b.shape[1]
    tm = _tile(Ka, tm)
    if col_blocks is None:
        tn = _tile(Nb, tn)
        out = ((Ka, Nb), BF16, (tm, tn), lambda i, j: (i, j))
    else:
        tn = Nb // col_blocks
        out = ((col_blocks, Ka, tn), BF16, (None, tm, tn), lambda i, j: (j, i, 0))
    return _gemm(name, a, ((T, tm), lambda i, j: (0, i)), b, ((T, tn), lambda i, j: (0, j)), TN,
                 (Ka // tm, Nb // tn), [out])


def _ffn_out(name, a, wt, res=None, scale=1.0, tm=1024, tn=512, after=None):
    nb, M, w = a.shape
    N = wt.shape[1]
    tm, tn = _tile(M, tm), _tile(N, tn)
    chunks = [((c,), (pl.ds(c * w, w),)) for c in range(nb)]
    extras = [] if res is None else [(res, (tm, tn), lambda i, j: (i, j))]
    epi = None if res is None else (lambda t, r: (r + scale * t,))
    return _gemm(name, a, ((nb, tm, w), lambda i, j: (0, i, 0)), wt, ((nb * w, tn), lambda i, j: (0, j)), NN,
                 (M // tm, N // tn), [((M, N), F32, (tm, tn), lambda i, j: (i, j))], chunks, extras, epi, after)


def _ffn_dact(name, dhb, wd, g, u, tm=1024):
    M, K = dhb.shape
    nb, _, w = g.shape
    tm = _tile(M, tm)
    tr = _tile(tm, 256)

    def body(dh_ref, wd_ref, g_ref, u_ref, dg_ref, du_ref):
        for r in range(tm // tr):
            rows = pl.ds(r * tr, tr)
            dact = lax.dot_general(dh_ref[rows, :], wd_ref[...], NT, preferred_element_type=F32)
            dg, du = _swiglu_bwd(dact, g_ref[rows, :], u_ref[rows, :])
            dg_ref[rows, :] = dg.astype(BF16)
            du_ref[rows, :] = du.astype(BF16)

    hid = pl.BlockSpec((None, tm, w), lambda i, j: (j, i, 0))
    return pl.pallas_call(
        body, name=name, grid=(M // tm, nb),
        in_specs=[pl.BlockSpec((tm, K), lambda i, j: (i, 0)), pl.BlockSpec((w, K), lambda i, j: (j, 0)), hid, hid],
        out_specs=[hid, hid], out_shape=[jax.ShapeDtypeStruct((nb, M, w), BF16)] * 2,
        compiler_params=_params(dimension_semantics=("parallel", "parallel")),
    )(dhb, wd, g, u)


def _ffn_dw(name, a, b, scale=1.0, tn=1024, after=None):
    nb, T, w = a.shape
    N = b.shape[1]
    tn = _tile(N, tn)
    epi = None if scale == 1.0 else (lambda t: (t * scale,))
    return _gemm(name, a, ((None, T, w), lambda i, j: (i, 0, 0)), b, ((T, tn), lambda i, j: (0, j)), TN,
                 (nb, N // tn), [((nb * w, N), BF16, (w, tn), lambda i, j: (i, j))], epilogue=epi, after=after)


def _ffn_gate(name, hn, wgt, nb, tm=1024):
    M, K = hn.shape
    w = wgt.shape[0] // nb
    tm = _tile(M, tm)
    return _gemm(name, hn, ((tm, K), lambda i, j: (i, 0)), wgt, ((w, K), lambda i, j: (j, 0)), NT,
                 (M // tm, nb), [((nb, M, w), BF16, (None, tm, w), lambda i, j: (j, i, 0))])


def _ffn_up_act(name, hn, wut, g, tm=1024):
    M, K = hn.shape
    nb, _, w = g.shape
    tm = _tile(M, tm)
    hid = ((None, tm, w), lambda i, j: (j, i, 0))

    def epilogue(u, gate):
        gate = gate.astype(F32)
        return u, gate * jax.nn.sigmoid(gate) * u

    return _gemm(name, hn, ((tm, K), lambda i, j: (i, 0)), wut, ((w, K), lambda i, j: (j, 0)), NT,
                 (M // tm, nb), [((nb, M, w), BF16) + hid] * 2, extras=[(g,) + hid], epilogue=epilogue)


def _ffn_up(name, hn, wgt, wut, nb, tm=1024):
    M, K = hn.shape
    w = wgt.shape[0] // nb
    tm = _tile(M, tm)

    def body(a_ref, g_ref, u_ref, og, ou, oa):
        a = a_ref[...]
        g = lax.dot_general(a, g_ref[...], NT, preferred_element_type=F32)
        u = lax.dot_general(a, u_ref[...], NT, preferred_element_type=F32)
        og[...] = g.astype(BF16)
        ou[...] = u.astype(BF16)
        oa[...] = (g * jax.nn.sigmoid(g) * u).astype(BF16)

    wspec = pl.BlockSpec((w, K), lambda i, j: (j, 0))
    ospec = pl.BlockSpec((None, tm, w), lambda i, j: (j, i, 0))
    return pl.pallas_call(
        body, name=name, grid=(M // tm, nb),
        in_specs=[pl.BlockSpec((tm, K), lambda i, j: (i, 0)), wspec, wspec],
        out_specs=[ospec] * 3, out_shape=[jax.ShapeDtypeStruct((nb, M, w), BF16)] * 3,
        compiler_params=_params(dimension_semantics=("parallel", "parallel")),
    )(hn, wgt, wut)


def _swiglu_bwd(dact, g, u):
    g = g.astype(F32)
    u = u.astype(F32)
    sig = jax.nn.sigmoid(g)
    silu = g * sig
    d = FFN_RES * dact
    return d * u * (sig * (1.0 + g * (1.0 - sig))), d * silu


def _rms_fwd(name, x, gain, tr=512, after=None):
    R, D = x.shape
    tr = _tile(R, tr)

    def body(x_ref, g_ref, *rest):
        xv = x_ref[...]
        y = xv * lax.rsqrt(jnp.mean(xv * xv, axis=-1, keepdims=True) + EPS)
        rest[-1][...] = (y * g_ref[...]).astype(BF16)

    tokens = [] if after is None else [after]
    return pl.pallas_call(
        body, name=name, grid=(R // tr,),
        in_specs=[pl.BlockSpec((tr, D), lambda i: (i, 0)), pl.BlockSpec((1, D), lambda i: (0, 0))]
        + [pl.BlockSpec(t.shape, lambda i: (0, 0)) for t in tokens],
        out_specs=pl.BlockSpec((tr, D), lambda i: (i, 0)), out_shape=jax.ShapeDtypeStruct((R, D), BF16),
        compiler_params=_params(dimension_semantics=("parallel",)),
    )(x, gain, *tokens)


def _rms_bwd_math(xv, gain, dy):
    rstd = lax.rsqrt(jnp.mean(xv * xv, axis=-1, keepdims=True) + EPS)
    xhat = xv * rstd
    dxh = dy * gain
    dx = rstd * (dxh - xhat * jnp.mean(dxh * xhat, axis=-1, keepdims=True))
    return dx, jnp.sum(dy * xhat, axis=0, keepdims=True)


def _rms_bwd(name, x, gain, dy, skip=None, tr=256):
    R, D = x.shape
    tr = _tile(R, tr)
    has_skip = skip is not None

    def body(*refs):
        x_ref, g_ref, dy_ref = refs[:3]
        dx_ref, dxb_ref, dg_ref = refs[-3:]
        dx, dg = _rms_bwd_math(x_ref[...], g_ref[...], dy_ref[...].astype(F32))
        if has_skip:
            dx = dx + refs[3][...]
        dx_ref[...] = dx
        dxb_ref[...] = dx.astype(BF16)

        @pl.when(pl.program_id(0) == 0)
        def _():
            dg_ref[...] = dg

        @pl.when(pl.program_id(0) > 0)
        def _():
            dg_ref[...] += dg

    row = pl.BlockSpec((tr, D), lambda i: (i, 0))
    vec = pl.BlockSpec((1, D), lambda i: (0, 0))
    return pl.pallas_call(
        body, name=name, grid=(R // tr,),
        in_specs=[row, vec, row] + ([row] if has_skip else []),
        out_specs=[row, row, vec],
        out_shape=[jax.ShapeDtypeStruct((R, D), F32), jax.ShapeDtypeStruct((R, D), BF16),
                   jax.ShapeDtypeStruct((1, D), F32)],
        compiler_params=_params(dimension_semantics=("arbitrary",)),
    )(*([x, gain, dy] + ([skip] if has_skip else [])))


def _loss_and_grad(name, h, gain, target, tr=256):
    R, D = h.shape
    tr = _tile(R, tr)

    def body(h_ref, g_ref, t_ref, loss_ref, dh_ref, dhb_ref, dg_ref):
        hv, gain_v = h_ref[...], g_ref[...]
        y = (hv * lax.rsqrt(jnp.mean(hv * hv, axis=-1, keepdims=True) + EPS)) * gain_v
        err = y - t_ref[...]
        part = jnp.full((8, 128), 0.5 * jnp.sum(jnp.mean(err * err, axis=-1, keepdims=True)), F32)
        dh, dg = _rms_bwd_math(hv, gain_v, err * (1.0 / D))
        dh_ref[...] = dh
        dhb_ref[...] = dh.astype(BF16)

        @pl.when(pl.program_id(0) == 0)
        def _():
            dg_ref[...] = dg
            loss_ref[...] = part

        @pl.when(pl.program_id(0) > 0)
        def _():
            dg_ref[...] += dg
            loss_ref[...] += part

    row = pl.BlockSpec((tr, D), lambda i: (i, 0))
    vec = pl.BlockSpec((1, D), lambda i: (0, 0))
    return pl.pallas_call(
        body, name=name, grid=(R // tr,), in_specs=[row, vec, row],
        out_specs=[pl.BlockSpec((8, 128), lambda i: (0, 0)), row, row, vec],
        out_shape=[jax.ShapeDtypeStruct((8, 128), F32), jax.ShapeDtypeStruct((R, D), F32),
                   jax.ShapeDtypeStruct((R, D), BF16), jax.ShapeDtypeStruct((1, D), F32)],
        compiler_params=_params(dimension_semantics=("arbitrary",)),
    )(h, gain, target)


def _bias_tile(name, rel):
    width = KBAND + QBLK
    sat = rel[:, 2 * REL_CLIP:]
    n_left = PAD - REL_CLIP + 1
    row0 = jnp.concatenate([jnp.broadcast_to(sat, (ATTN_HEADS, n_left)), rel[:, :2 * REL_CLIP][:, ::-1],
                            jnp.broadcast_to(sat, (ATTN_HEADS, width - n_left - 2 * REL_CLIP))], axis=1)

    def body(e_ref, o_ref):
        rows = pltpu.roll(jnp.broadcast_to(e_ref[...], (QBLK, width)), 0, 1, stride=1, stride_axis=0)
        i = lax.broadcasted_iota(jnp.int32, (QBLK, KBAND), 0) // CHUNK
        j = lax.broadcasted_iota(jnp.int32, (QBLK, KBAND), 1) // CHUNK
        o_ref[...] = jnp.where((j >= i) & (j <= i + LEFT_CHUNKS), rows[:, :KBAND], NEG_INF)

    return pl.pallas_call(
        body, name=name, grid=(ATTN_HEADS,),
        in_specs=[pl.BlockSpec((None, 1, width), lambda h: (h, 0, 0))],
        out_specs=pl.BlockSpec((None, QBLK, KBAND), lambda h: (h, 0, 0)),
        out_shape=jax.ShapeDtypeStruct((ATTN_HEADS, QBLK, KBAND), F32),
        compiler_params=_params(dimension_semantics=("parallel",)),
    )(row0.reshape(ATTN_HEADS, 1, width))


def _band_softmax(q_half, kb, bias, kpos):
    s = lax.dot_general(q_half, kb, NT, preferred_element_type=F32) * (HEAD_DIM ** -0.5) + bias
    s = jnp.where(kpos >= 0, s, NEG_INF)
    e = jnp.exp(s - jnp.max(s, axis=-1, keepdims=True))
    return e / jnp.sum(e, axis=-1, keepdims=True)


def _fill_padded(dst, src, S):
    dst[pl.ds(0, PAD), :] = jnp.zeros((PAD, dst.shape[1]), dst.dtype)
    dst[pl.ds(PAD, S), :] = src[...].astype(dst.dtype)


def _attn_fwd(name, z, bias):
    B, S, _ = z.shape
    nh2 = ATTN_HEADS // 2

    def body(q_ref, k_ref, v_ref, b_ref, o_ref, kp, vp):
        qb = pl.program_id(2)

        @pl.when(qb == 0)
        def _():
            _fill_padded(kp, k_ref, S)
            _fill_padded(vp, v_ref, S)

        start = pl.multiple_of(qb * QBLK, QBLK)
        kb, vb = kp[pl.ds(start, KBAND), :], vp[pl.ds(start, KBAND), :]
        q = q_ref[...].astype(BF16)
        first = lax.broadcasted_iota(jnp.int32, (QBLK, 2 * HEAD_DIM), 1) < HEAD_DIM
        kpos = qb * QBLK - PAD + lax.broadcasted_iota(jnp.int32, (1, KBAND), 1)
        o = []
        for a in range(2):
            p = _band_softmax(jnp.where(first == (a == 0), q, jnp.zeros_like(q)), kb, b_ref[a], kpos)
            o.append(jnp.dot(p.astype(BF16), vb, preferred_element_type=F32))
        o_ref[...] = jnp.where(first, o[0], o[1]).astype(BF16)

    return pl.pallas_call(
        body, name=name, grid=(B, nh2, S // QBLK),
        in_specs=[pl.BlockSpec((None, QBLK, 128), lambda b, h, i: (b, i, h)),
                  pl.BlockSpec((None, S, 128), lambda b, h, i: (b, 0, nh2 + h)),
                  pl.BlockSpec((None, S, 128), lambda b, h, i: (b, 0, 2 * nh2 + h)),
                  pl.BlockSpec((2, QBLK, KBAND), lambda b, h, i: (h, 0, 0))],
        out_specs=pl.BlockSpec((None, QBLK, 128), lambda b, h, i: (b, i, h)),
        out_shape=jax.ShapeDtypeStruct((B, S, D_ATTN), BF16),
        scratch_shapes=[pltpu.VMEM((PAD + S, 128), BF16), pltpu.VMEM((PAD + S, 128), BF16)],
        compiler_params=_params(dimension_semantics=("parallel", "parallel", "arbitrary")),
    )(z, z, z, bias)


def _attn_bwd(name, z, bias, dcat):
    B, S, _ = z.shape
    nh2 = ATTN_HEADS // 2
    nqb = S // QBLK
    scale = HEAD_DIM ** -0.5

    def body(q_ref, k_ref, v_ref, b_ref, do_ref, dq_ref, dk_ref, dv_ref, db_ref, kp, vp, dka, dva):
        b, qb = pl.program_id(1), pl.program_id(2)

        @pl.when(qb == 0)
        def _():
            _fill_padded(kp, k_ref, S)
            _fill_padded(vp, v_ref, S)
            dka[...] = jnp.zeros_like(dka)
            dva[...] = jnp.zeros_like(dva)

        @pl.when((qb == 0) & (b == 0))
        def _():
            db_ref[...] = jnp.zeros_like(db_ref)

        start = pl.multiple_of(qb * QBLK, QBLK)
        band = pl.ds(start, KBAND)
        kb, vb = kp[band, :], vp[band, :]
        q = q_ref[...].astype(BF16)
        do = do_ref[...]
        first = lax.broadcasted_iota(jnp.int32, (QBLK, 2 * HEAD_DIM), 1) < HEAD_DIM
        kpos = qb * QBLK - PAD + lax.broadcasted_iota(jnp.int32, (1, KBAND), 1)
        dq, dk, dv = [], None, None
        for a in range(2):
            mine = first == (a == 0)
            qa = jnp.where(mine, q, jnp.zeros_like(q))
            doa = jnp.where(mine, do, jnp.zeros_like(do))
            p = _band_softmax(qa, kb, b_ref[a], kpos)
            dp = lax.dot_general(doa, vb, NT, preferred_element_type=F32)
            ds = p * (dp - jnp.sum(p * dp, axis=-1, keepdims=True))
            db_ref[a] += ds
            dsb = (ds * scale).astype(BF16)
            dq.append(jnp.dot(dsb, kb, preferred_element_type=F32))
            dka_part = lax.dot_general(dsb, qa, TN, preferred_element_type=F32)
            dva_part = lax.dot_general(p.astype(BF16), doa, TN, preferred_element_type=F32)
            dk = dka_part if dk is None else dk + dka_part
            dv = dva_part if dv is None else dv + dva_part
        dq_ref[...] = jnp.where(first, dq[0], dq[1]).astype(BF16)
        dka[band, :] += dk
        dva[band, :] += dv

        @pl.when(qb == nqb - 1)
        def _():
            dk_ref[...] = dka[pl.ds(PAD, S), :].astype(BF16)
            dv_ref[...] = dva[pl.ds(PAD, S), :].astype(BF16)

    qspec = pl.BlockSpec((None, QBLK, 128), lambda h, b, i: (b, i, h))
    kvout = pl.BlockSpec((None, S, 128), lambda h, b, i: (b, 0, h))
    bspec = pl.BlockSpec((2, QBLK, KBAND), lambda h, b, i: (h, 0, 0))
    act = jax.ShapeDtypeStruct((B, S, D_ATTN), BF16)
    return pl.pallas_call(
        body, name=name, grid=(nh2, B, nqb),
        in_specs=[qspec,
                  pl.BlockSpec((None, S, 128), lambda h, b, i: (b, 0, nh2 + h)),
                  pl.BlockSpec((None, S, 128), lambda h, b, i: (b, 0, 2 * nh2 + h)),
                  bspec, qspec],
        out_specs=[qspec, kvout, kvout, bspec],
        out_shape=[act, act, act, jax.ShapeDtypeStruct((ATTN_HEADS, QBLK, KBAND), F32)],
        scratch_shapes=[pltpu.VMEM((PAD + S, 128), BF16), pltpu.VMEM((PAD + S, 128), BF16),
                        pltpu.VMEM((PAD + S, 128), F32), pltpu.VMEM((PAD + S, 128), F32)],
        compiler_params=_params(dimension_semantics=("arbitrary", "arbitrary", "arbitrary")),
    )(z, z, z, bias, dcat)


def _bias_grad(name, dbias):
    width = KBAND + QBLK

    def body(d_ref, o_ref):
        acc = jnp.zeros((1, width), F32)
        for i in range(QBLK):
            row = jnp.concatenate([d_ref[pl.ds(i, 1), :], jnp.zeros((1, QBLK), F32)], axis=1)
            shift = QBLK - 1 - i
            acc = acc + (pltpu.roll(row, shift, 1) if shift else row)
        o_ref[...] = acc

    return pl.pallas_call(
        body, name=name, grid=(ATTN_HEADS,),
        in_specs=[pl.BlockSpec((None, QBLK, KBAND), lambda h: (h, 0, 0))],
        out_specs=pl.BlockSpec((None, 1, width), lambda h: (h, 0, 0)),
        out_shape=jax.ShapeDtypeStruct((ATTN_HEADS, 1, width), F32),
        compiler_params=_params(dimension_semantics=("parallel",)),
    )(dbias)


def _rel_grad_from_diagonals(diag):
    top = PAD + QBLK - 1 - REL_CLIP
    sat = jnp.sum(diag[:, :top + 1], axis=1, keepdims=True)
    mid = diag[:, top + 1:top + 2 * REL_CLIP][:, ::-1]
    return jnp.concatenate([jnp.zeros_like(sat), mid, sat], axis=1)


def _shift_rows(x, k, forward):
    S = x.shape[0]
    t = lax.broadcasted_iota(jnp.int32, x.shape, 0)
    if forward:
        return jnp.where(t < S - k, pltpu.roll(x, S - k, 0), 0.0)
    return jnp.where(t >= k, pltpu.roll(x, k, 0), 0.0)


def _window_sum(x, g, forward):
    s = x + _shift_rows(x, 1, forward)
    out = s
    for n, k in enumerate((2, 4, 8)):
        s = s + _shift_rows(s, k, forward)
        out = jnp.where(g > n, s, out)
    return out


def _pool_count(S, g):
    t = lax.broadcasted_iota(jnp.int32, (S, 1), 0)
    w = jnp.left_shift(2, g)
    return jnp.minimum(t + 1, w).astype(F32)


def _pool_fwd(name, z, wp, pscale):
    B, S, _ = z.shape
    c0 = 3 * D_ATTN // POOL_GROUP

    def body(u_ref, w_ref, s_ref, d_ref, y_ref):
        g = pl.program_id(1)
        u = u_ref[...]
        d = (_window_sum(u, g, False) / _pool_count(S, g) - u).astype(BF16)
        d_ref[...] = d
        y_ref[...] = (jnp.dot(d, w_ref[...], preferred_element_type=F32) * s_ref[...]).astype(BF16)

    blk = pl.BlockSpec((None, S, POOL_GROUP), lambda b, g: (b, 0, g))
    out = jax.ShapeDtypeStruct((B, S, D_POOL), BF16)
    return pl.pallas_call(
        body, name=name, grid=(B, len(POOL_WINDOWS)),
        in_specs=[pl.BlockSpec((None, S, POOL_GROUP), lambda b, g: (b, 0, c0 + g)),
                  pl.BlockSpec((None, POOL_GROUP, POOL_GROUP), lambda b, g: (g, 0, 0)),
                  pl.BlockSpec((1, POOL_GROUP), lambda b, g: (0, g))],
        out_specs=[blk, blk], out_shape=[out, out],
        compiler_params=_params(dimension_semantics=("parallel", "parallel")),
    )(z, wp, pscale)


def _pool_bwd(name, d, wp, pscale, dcat):
    B, S, _ = d.shape
    c0 = D_ATTN // POOL_GROUP

    def body(d_ref, w_ref, s_ref, dy_ref, du_ref, dw_ref, dsc_ref):
        g, b = pl.program_id(0), pl.program_id(1)
        dv = d_ref[...]
        dy = dy_ref[...].astype(F32)
        w = w_ref[...]
        ypre = jnp.dot(dv, w, preferred_element_type=F32)
        dyp = (dy * s_ref[...]).astype(BF16)
        dd = lax.dot_general(dyp, w, NT, preferred_element_type=F32)
        du_ref[...] = (_window_sum(dd / _pool_count(S, g), g, True) - dd).astype(BF16)
        dw = lax.dot_general(dv, dyp, TN, preferred_element_type=F32)
        dsc = jnp.sum(dy * ypre, axis=0, keepdims=True)

        @pl.when(b == 0)
        def _():
            dw_ref[...] = dw
            dsc_ref[...] = dsc

        @pl.when(b > 0)
        def _():
            dw_ref[...] += dw
            dsc_ref[...] += dsc

    blk = pl.BlockSpec((None, S, POOL_GROUP), lambda g, b: (b, 0, g))
    wspec = pl.BlockSpec((None, POOL_GROUP, POOL_GROUP), lambda g, b: (g, 0, 0))
    sspec = pl.BlockSpec((1, POOL_GROUP), lambda g, b: (0, g))
    return pl.pallas_call(
        body, name=name, grid=(len(POOL_WINDOWS), B),
        in_specs=[blk, wspec, sspec, pl.BlockSpec((None, S, POOL_GROUP), lambda g, b: (b, 0, c0 + g))],
        out_specs=[blk, wspec, sspec],
        out_shape=[jax.ShapeDtypeStruct((B, S, D_POOL), BF16),
                   jax.ShapeDtypeStruct((len(POOL_WINDOWS), POOL_GROUP, POOL_GROUP), F32),
                   jax.ShapeDtypeStruct((1, D_POOL), F32)],
        compiler_params=_params(dimension_semantics=("arbitrary", "arbitrary")),
    )(d, wp, pscale, dcat)


def _cross_softmax(q, k):
    s = lax.dot_general(q, k, NT, preferred_element_type=F32) * (CROSS_DIM ** -0.5)
    e = jnp.exp(s - jnp.max(s, axis=-1, keepdims=True))
    return e / jnp.sum(e, axis=-1, keepdims=True)


def _cross_fwd(name, qc, kv, tq=512):
    B, S, _ = qc.shape
    M = kv.shape[1]
    tq = _tile(S, tq)

    def body(q_ref, k_ref, v_ref, o_ref):
        p = _cross_softmax(q_ref[...], k_ref[...])
        o_ref[...] = jnp.dot(p.astype(BF16), v_ref[...], preferred_element_type=F32).astype(BF16)

    qspec = pl.BlockSpec((None, tq, CROSS_DIM), lambda b, h, i: (b, i, h))
    return pl.pallas_call(
        body, name=name, grid=(B, CROSS_HEADS, S // tq),
        in_specs=[qspec, pl.BlockSpec((None, M, CROSS_DIM), lambda b, h, i: (b, 0, h)),
                  pl.BlockSpec((None, M, CROSS_DIM), lambda b, h, i: (b, 0, CROSS_HEADS + h))],
        out_specs=qspec, out_shape=jax.ShapeDtypeStruct((B, S, D_CROSS), BF16),
        compiler_params=_params(dimension_semantics=("parallel", "parallel", "parallel")),
    )(qc, kv, kv)


def _cross_bwd(name, qc, kv, do, tq=512):
    B, S, _ = qc.shape
    M = kv.shape[1]
    tq = _tile(S, tq)
    nq = S // tq
    scale = CROSS_DIM ** -0.5

    def body(q_ref, k_ref, v_ref, do_ref, dq_ref, dk_ref, dv_ref, dka, dva):
        i = pl.program_id(2)
        q, k, v, dov = q_ref[...], k_ref[...], v_ref[...], do_ref[...]
        p = _cross_softmax(q, k)
        dp = lax.dot_general(dov, v, NT, preferred_element_type=F32)
        ds = ((p * (dp - jnp.sum(p * dp, axis=-1, keepdims=True))) * scale).astype(BF16)
        dq_ref[...] = jnp.dot(ds, k, preferred_element_type=F32).astype(BF16)
        dk = lax.dot_general(ds, q, TN, preferred_element_type=F32)
        dv = lax.dot_general(p.astype(BF16), dov, TN, preferred_element_type=F32)

        @pl.when(i == 0)
        def _():
            dka[...] = dk
            dva[...] = dv

        @pl.when(i > 0)
        def _():
            dka[...] += dk
            dva[...] += dv

        @pl.when(i == nq - 1)
        def _():
            dk_ref[...] = dka[...].astype(BF16)
            dv_ref[...] = dva[...].astype(BF16)

    qspec = pl.BlockSpec((None, tq, CROSS_DIM), lambda b, h, i: (b, i, h))
    kspec = pl.BlockSpec((None, M, CROSS_DIM), lambda b, h, i: (b, 0, h))
    return pl.pallas_call(
        body, name=name, grid=(B, CROSS_HEADS, nq),
        in_specs=[qspec, kspec, pl.BlockSpec((None, M, CROSS_DIM), lambda b, h, i: (b, 0, CROSS_HEADS + h)), qspec],
        out_specs=[qspec, kspec, kspec],
        out_shape=[jax.ShapeDtypeStruct((B, S, D_CROSS), BF16), jax.ShapeDtypeStruct((B, M, D_CROSS), BF16),
                   jax.ShapeDtypeStruct((B, M, D_CROSS), BF16)],
        scratch_shapes=[pltpu.VMEM((M, CROSS_DIM), F32), pltpu.VMEM((M, CROSS_DIM), F32)],
        compiler_params=_params(dimension_semantics=("parallel", "parallel", "arbitrary")),
    )(qc, kv, kv, do)


def _local_step(x, mem, target, small, weights, emit, start_token=None):
    B, S, D = x.shape
    T = B * S
    x2, t2 = x.reshape(T, D), target.reshape(T, D)
    mem2 = mem.reshape(-1, D)
    n_mem = mem.shape[1]
    wts = {}

    hn1 = _rms_fwd("norm_ffn1", x2, small["ffn1_norm"], after=start_token)
    wts.update(weights(0, hn1))
    g1 = _ffn_gate("ffn1_gate", hn1, wts["ffn1_w_gate"], N_DEV)
    wts.update(weights(1, g1))
    u1, a1 = _ffn_up_act("ffn1_up", hn1, wts["ffn1_w_up"], g1)
    wts.update(weights(2, a1))
    h1 = _ffn_out("ffn1_down", a1, wts["ffn1_w_down"], res=x2, scale=FFN_RES)
    hn2 = _rms_fwd("norm_mix", h1, small["mix_norm"])
    wts.update(weights(3, hn2))
    z = _mm_nn_cols("mix_in", hn2, wts["w_in"], F32).reshape(B, S, -1)
    bias = _bias_tile("bias_tile", small["rel_bias"])
    y_attn = _attn_fwd("attn_fwd", z, bias)
    d_pool, y_pool = _pool_fwd("pool_fwd", z, wts["w_pool"], small["pool_scale"])
    cat = jnp.concatenate([y_attn, y_pool], axis=-1).reshape(T, -1)
    h2 = _mm_nn("mix_out", cat, wts["w_out"], F32, res=h1)
    hn3 = _rms_fwd("norm_cross", h2, small["cross_norm"])
    memn = _rms_fwd("norm_mem", mem2, small["mem_norm"])
    wts.update(weights(4, hn3))
    qc = _mm_nn("cross_q", hn3, wts["w_cq"], BF16)
    kv = _mm_nn("cross_kv", memn, wts["w_ckv"], BF16)
    o = _cross_fwd("cross_fwd", qc.reshape(B, S, -1), kv.reshape(B, n_mem, -1)).reshape(T, -1)
    h3 = _mm_nn_cols("cross_out", o, wts["w_co"], F32, res=h2)
    hn4 = _rms_fwd("norm_ffn2", h3, small["ffn2_norm"])
    wts.update(weights(5, hn4))
    g2, u2, a2 = _ffn_up("ffn2_up", hn4, wts["ffn2_w_gate"], wts["ffn2_w_up"], N_DEV)
    h4 = _ffn_out("ffn2_down", a2, wts["ffn2_w_down"], res=h3, scale=FFN_RES)

    gs = {}
    loss_part, dh4, dh4b, gs["final_norm"] = _loss_and_grad("loss", h4, small["final_norm"], t2)

    def ffn_bwd(tag, dh, dhb, h_in, hn, g, u, a, wg, wu, wd, gain):
        dg, du = _ffn_dact(tag + "_dact", dhb, wd, g, u)
        tok = emit({tag + "_w_gate": _ffn_dw(tag + "_dwg", dg, hn)})
        tok = emit({tag + "_w_up": _ffn_dw(tag + "_dwu", du, hn, after=tok)})
        tok = emit({tag + "_w_down": _ffn_dw(tag + "_dwd", a, dhb, scale=FFN_RES, after=tok)})
        dhn = _ffn_out(tag + "_dhn_g", dg, wg, after=tok)
        dhn = _ffn_out(tag + "_dhn_u", du, wu, res=dhn)
        return _rms_bwd(tag + "_dnorm", h_in, gain, dhn, skip=dh)

    dh3, dh3b, gs["ffn2_norm"] = ffn_bwd("ffn2", dh4, dh4b, h3, hn4, g2, u2, a2, wts["ffn2_w_gate"],
                                         wts["ffn2_w_up"], wts["ffn2_w_down"], small["ffn2_norm"])
    do = _mm_nt_cols("cross_do", dh3b, wts["w_co"], BF16, tn=D_CROSS)
    gw = {"w_co": _mm_tn("cross_dwo", o, dh3b, tm=D_CROSS, col_blocks=N_DEV)}
    dqc, dk, dv = _cross_bwd("cross_bwd", qc.reshape(B, S, -1), kv.reshape(B, n_mem, -1), do.reshape(B, S, -1))
    dqc = dqc.reshape(T, -1)
    dkv = jnp.concatenate([dk, dv], axis=-1).reshape(B * n_mem, -1)
    gw["w_cq"] = _mm_tn("cross_dwq", hn3, dqc, tn=D_CROSS)
    gw["w_ckv"] = _mm_tn("cross_dwkv", memn, dkv)
    tok = emit(gw)
    dhn3 = _mm_nt("cross_dhn", dqc, wts["w_cq"], F32, after=tok)
    dmemn = _mm_nt("cross_dmem", dkv, wts["w_ckv"], F32, tm=512)
    _, _, gs["mem_norm"] = _rms_bwd("mem_dnorm", mem2, small["mem_norm"], dmemn)
    dh2, dh2b, gs["cross_norm"] = _rms_bwd("cross_dnorm", h2, small["cross_norm"], dhn3, skip=dh3)
    dcat = _mm_nt("mix_dcat", dh2b, wts["w_out"], BF16)
    gw = {"w_out": _mm_tn("mix_dwout", cat, dh2b)}
    dcat3 = dcat.reshape(B, S, -1)
    dq, dkk, dvv, dbias = _attn_bwd("attn_bwd", z, bias, dcat3)
    du, gw["w_pool"], gs["pool_scale"] = _pool_bwd("pool_bwd", d_pool, wts["w_pool"], small["pool_scale"], dcat3)
    gs["rel_bias"] = _rel_grad_from_diagonals(_bias_grad("bias_grad", dbias)[:, 0, :])
    dz = jnp.concatenate([dq, dkk, dvv, du], axis=-1).reshape(T, -1)
    gw["w_in"] = _mm_tn("mix_dwin", hn2, dz, col_blocks=N_DEV)
    tok = emit(gw)
    dhn2 = _mm_nt_cols("mix_dhn", dz, wts["w_in"], F32, after=tok)
    dh1, dh1b, gs["mix_norm"] = _rms_bwd("mix_dnorm", h1, small["mix_norm"], dhn2, skip=dh2)
    dx, _, gs["ffn1_norm"] = ffn_bwd("ffn1", dh1, dh1b, x2, hn1, g1, u1, a1, wts["ffn1_w_gate"],
                                     wts["ffn1_w_up"], wts["ffn1_w_down"], small["ffn1_norm"])
    return loss_part, dx.reshape(B, S, D), gs


def _position():
    return lax.axis_index("x"), lax.axis_index("y"), lax.axis_index("c")


def _index(p):
    return 4 * p[0] + 2 * p[1] + p[2]


HBM_SPEC = pl.BlockSpec(memory_space=pltpu.HBM)
SEM_SPEC = pl.BlockSpec(memory_space=pltpu.SEMAPHORE)
ANY_SPEC = pl.BlockSpec(memory_space=pl.ANY)
ORDERED_EFFECT = pltpu.SideEffectType.DATAFLOW_SIDE_EFFECTING


N_COPIES = {"grads": N_DEV - 1, "spread": 4, "relay": 3}


def _copies(pattern, srcs, lands, send, recv):
    x, y, c = _position()
    me, sibling = _index((x, y, c)), (x, y, 1 - c)
    chips = [(1 - x, y), (x, 1 - y), (1 - x, 1 - y)]
    if pattern == "grads":
        targets = [(x ^ (k >> 2), y ^ ((k >> 1) & 1), c ^ (k & 1)) for k in range(1, N_DEV)]
    else:
        targets = [sibling] + [(*chip, c) for chip in chips]
    per, out = N_COPIES[pattern], []
    for k in range(per):
        for a in range(len(lands)):
            if pattern == "relay":
                src = dst = lands[a].at[_index((*chips[k], c))]
                to = sibling
            else:
                to = targets[k]
                src = srcs[a].at[_index(to)] if pattern == "grads" else srcs[a]
                dst = lands[a].at[me]
            out.append(pltpu.make_async_remote_copy(src_ref=src, dst_ref=dst, send_sem=send.at[a * per + k],
                                                    recv_sem=recv.at[a * per + k], device_id=to, device_id_type=MESH))
    return out


def _comm(name, srcs, lands, wait=None, start=None, after=None):
    ns, nl = len(srcs), len(lands)
    na = ns + nl
    arrays = list(srcs) + list(lands)
    n_wait = 2 if wait else 0
    n_start = 2 if start else 0

    def body(*refs):
        ins, lnd = refs[:ns], refs[ns:na]
        if wait:
            for cp in _copies(wait[0], ins, lnd, refs[na], refs[na + 1]):
                cp.wait_send()
                cp.wait_recv()
        if start:
            outs = refs[na + n_wait + (after is not None):]
            for cp in _copies(start, ins, lnd, outs[0], outs[1]):
                cp.start()
            refs[-1][...] = jnp.zeros((8, 128), F32)

    out_shape, out_specs = [], []
    if start:
        sems = pltpu.SemaphoreType.DMA((nl * N_COPIES[start],))
        out_shape += [sems, sems]
        out_specs += [SEM_SPEC, SEM_SPEC]
    out_shape += [pltpu.HBM(a.shape, a.dtype) for a in arrays]
    out_specs += [HBM_SPEC] * na
    if start:
        out_shape.append(jax.ShapeDtypeStruct((8, 128), F32))
        out_specs.append(pl.BlockSpec(memory_space=pltpu.VMEM))
    operands = [pltpu.with_memory_space_constraint(a, pltpu.HBM) for a in arrays]
    operands += list(wait[1:]) if wait else []
    operands += [after] if after is not None else []
    res = pl.pallas_call(
        body, name=name, out_shape=out_shape, out_specs=out_specs,
        in_specs=[HBM_SPEC] * na + [SEM_SPEC] * n_wait + [ANY_SPEC] * (after is not None),
        input_output_aliases={i: n_start + i for i in range(na)},
        compiler_params=pltpu.CompilerParams(has_side_effects=ORDERED_EFFECT),
    )(*operands)
    res = list(res)
    thru = res[n_start:n_start + na]
    return thru[:ns], thru[ns:], (tuple(res[:2]) if start else None), (res[-1] if start else None)


def _adamw_math(w, g, m, v):
    m = ADAM_B1 * m + (1.0 - ADAM_B1) * g
    v = ADAM_B2 * v + (1.0 - ADAM_B2) * (g * g)
    m_hat = m / (1.0 - ADAM_B1 ** ADAM_STEP)
    v_hat = v / (1.0 - ADAM_B2 ** ADAM_STEP)
    delta = -ADAM_LR * (m_hat / (jnp.sqrt(v_hat) + ADAM_EPS) + ADAM_WD * w)
    return delta, m, v


def _adamw(name, parts, w, m, v, tr=128):
    R, C = w.shape
    tr = _tile(R, tr)

    def body(p_ref, w_ref, m_ref, v_ref, g_out, d_out, m_out, v_out):
        g = p_ref[0].astype(F32)
        for d in range(1, N_DEV):
            g = g + p_ref[d].astype(F32)
        g_out[...] = g
        d_out[...], m_out[...], v_out[...] = _adamw_math(w_ref[...], g, m_ref[...], v_ref[...])

    row = pl.BlockSpec((tr, C), lambda i: (i, 0))
    out = jax.ShapeDtypeStruct((R, C), F32)
    return pl.pallas_call(
        body, name=name, grid=(R // tr,),
        in_specs=[pl.BlockSpec((N_DEV, tr, C), lambda i: (0, i, 0)), row, row, row],
        out_specs=[row] * 4, out_shape=[out] * 4,
        compiler_params=_params(dimension_semantics=("parallel",)),
    )(parts, w, m, v)


def _small_allreduce_adamw(name, g, w, m, v):
    R = g.shape[0]

    def body(g_ref, w_ref, m_ref, v_ref, g_out, d_out, m_out, v_out, land, send, recv):
        x, y, c = _position()
        me = _index((x, y, c))
        land[me] = g_ref[...]
        copies = []
        for k in range(1, N_DEV):
            peer = (x ^ (k >> 2), y ^ ((k >> 1) & 1), c ^ (k & 1))
            copies.append(pltpu.make_async_remote_copy(
                src_ref=g_ref, dst_ref=land.at[me], send_sem=send.at[k - 1], recv_sem=recv.at[k - 1],
                device_id=peer, device_id_type=MESH))
        for cp in copies:
            cp.start()
        for cp in copies:
            cp.wait()
        total = land[0]
        for d in range(1, N_DEV):
            total = total + land[d]
        g_out[...] = total
        d_out[...], m_out[...], v_out[...] = _adamw_math(w_ref[...], total, m_ref[...], v_ref[...])

    vm = pl.BlockSpec(memory_space=pltpu.VMEM)
    out = jax.ShapeDtypeStruct((R, 128), F32)
    return pl.pallas_call(
        body, name=name, in_specs=[vm] * 4, out_specs=[vm] * 4, out_shape=[out] * 4,
        scratch_shapes=[pltpu.VMEM((N_DEV, R, 128), F32), pltpu.SemaphoreType.DMA((7,)),
                        pltpu.SemaphoreType.DMA((7,))],
    )(g, w, m, v)


BIG = ("ffn1_w_gate", "ffn1_w_up", "ffn1_w_down", "w_in", "w_pool", "w_out", "w_cq", "w_ckv", "w_co",
       "ffn2_w_gate", "ffn2_w_up", "ffn2_w_down")
SMALL = ("ffn1_norm", "mix_norm", "rel_bias", "pool_scale", "cross_norm", "mem_norm", "ffn2_norm", "final_norm")
ORDER = ("ffn1_norm", "ffn1_w_gate", "ffn1_w_up", "ffn1_w_down", "mix_norm", "w_in", "rel_bias", "w_pool",
         "pool_scale", "w_out", "cross_norm", "mem_norm", "w_cq", "w_ckv", "w_co", "ffn2_norm", "ffn2_w_gate",
         "ffn2_w_up", "ffn2_w_down", "final_norm")
TRANSPOSED = ("ffn1_w_gate", "ffn1_w_up", "ffn2_w_gate", "ffn2_w_up")
ROW_SHARDED = TRANSPOSED + ("ffn1_w_down", "ffn2_w_down", "w_out", "w_cq", "w_ckv")
GATHER_GROUPS = (("ffn1_w_gate",), ("ffn1_w_up",), ("ffn1_w_down",), ("w_in", "w_pool", "w_out"),
                 ("w_cq", "w_ckv", "w_co"), ("ffn2_w_gate", "ffn2_w_up", "ffn2_w_down"))
RELAY_BEFORE_USE = ((0,), (1,), (2,), (3,), (4, 5), ())


def _pack(arrays):
    flat = jnp.concatenate([a.reshape(-1) for a in arrays])
    rows = -(-flat.shape[0] // 1024) * 8
    return jnp.pad(flat, (0, rows * 128 - flat.shape[0])).reshape(rows, 128)


def _unpack(packed, like):
    flat, out, at = packed.reshape(-1), [], 0
    for a in like:
        out.append(flat[at:at + a.size].reshape(a.shape))
        at += a.size
    return out


def _shard2d(a):
    a = a[0]
    return a.reshape(-1, a.shape[-1])


def kernel(x, mem, ffn1_norm, ffn1_w_gate, ffn1_w_up, ffn1_w_down, mix_norm, w_in, rel_bias, w_pool, pool_scale, w_out, cross_norm, mem_norm, w_cq, w_ckv, w_co, ffn2_norm, ffn2_w_gate, ffn2_w_up, ffn2_w_down, final_norm, loss_target, m_ffn1_norm, m_ffn1_w_gate, m_ffn1_w_up, m_ffn1_w_down, m_mix_norm, m_w_in, m_rel_bias, m_w_pool, m_pool_scale, m_w_out, m_cross_norm, m_mem_norm, m_w_cq, m_w_ckv, m_w_co, m_ffn2_norm, m_ffn2_w_gate, m_ffn2_w_up, m_ffn2_w_down, m_final_norm, v_ffn1_norm, v_ffn1_w_gate, v_ffn1_w_up, v_ffn1_w_down, v_mix_norm, v_w_in, v_rel_bias, v_w_pool, v_pool_scale, v_w_out, v_cross_norm, v_mem_norm, v_w_cq, v_w_ckv, v_w_co, v_ffn2_norm, v_ffn2_w_gate, v_ffn2_w_up, v_ffn2_w_down, v_final_norm):
    args = dict(locals())
    def view(n, a):
        return a.transpose(0, 2, 1) if n in TRANSPOSED else a

    w_in_ = {n: view(n, args[n]) for n in ORDER}
    m_in = {n: view(n, args["m_" + n]) for n in ORDER}
    v_in = {n: view(n, args["v_" + n]) for n in ORDER}

    me = 4 * lax.axis_index("x") + 2 * lax.axis_index("y") + lax.axis_index("c")
    n_g, rows = len(POOL_WINDOWS), POOL_GROUP // N_DEV

    def own_block_placed(block):
        return lax.dynamic_update_slice_in_dim(lax.empty((N_DEV,) + block.shape, block.dtype), block[None], me, 0)

    gathers, tok = [], None
    for gi, group in enumerate(GATHER_GROUPS):
        shards = [_shard2d(w_in_[n]).astype(BF16) for n in group]
        srcs, lands, sems, tok = _comm("gather_start_%d" % gi, shards, [own_block_placed(s) for s in shards],
                                       start="spread", after=tok)
        gathers.append((srcs, lands, sems))

    def weights(gi, after):
        for ri in RELAY_BEFORE_USE[gi]:
            srcs, lands, sems = gathers[ri]
            _, lands, sems, after = _comm("gather_relay_%d" % ri, srcs, lands, wait=("spread",) + sems,
                                          start="relay", after=after)
            gathers[ri] = (None, lands, sems)
        _, lands, sems = gathers[gi]
        _, lands, _, _ = _comm("gather_finish_%d" % gi, [], lands, wait=("relay",) + sems, after=after)
        out = {}
        for n, full in zip(GATHER_GROUPS[gi], lands):
            if n == "w_pool":
                full = full.reshape(N_DEV, n_g, rows, POOL_GROUP).transpose(1, 0, 2, 3).reshape(n_g, POOL_GROUP, POOL_GROUP)
            out[n] = full.reshape(-1, full.shape[-1]) if n in ROW_SHARDED else full
        return out

    scatters = []

    def emit(gw):
        names = list(gw)
        stacks = []
        for n in names:
            g = gw[n]
            if n == "w_pool":
                g = g.reshape(n_g, N_DEV, rows, POOL_GROUP).transpose(1, 0, 2, 3).astype(BF16)
            stacks.append(g.reshape((N_DEV,) + _shard2d(w_in_[n]).shape))
        lands = [own_block_placed(lax.dynamic_index_in_dim(s, me, 0, keepdims=False)) for s in stacks]
        srcs, lands, sems, token = _comm("grads_start_%d" % len(scatters), stacks, lands, start="grads")
        scatters.append((names, srcs, lands, sems))
        return token

    small = {n: w_in_[n].reshape(1, -1) for n in SMALL if n != "rel_bias"}
    small["rel_bias"] = rel_bias[0]
    loss_part, grad_x, gs = _local_step(x, mem, loss_target, small, weights, emit, start_token=tok)
    loss = lax.psum(loss_part[0, 0], ("x", "y", "c"))

    grad, delta, new_m, new_v = {}, {}, {}, {}
    after = grad_x
    for si, (names, srcs, lands, sems) in enumerate(scatters):
        _, landed, _, _ = _comm("grads_finish_%d" % si, srcs, lands, wait=("grads",) + sems, after=after)
        for n, parts in zip(names, landed):
            res = _adamw("adamw_" + n, parts, _shard2d(w_in_[n]), _shard2d(m_in[n]), _shard2d(v_in[n]))
            grad[n], delta[n], new_m[n], new_v[n] = [view(n, r.reshape(w_in_[n].shape)) for r in res]
        after = res[0]

    like = [w_in_[n] for n in SMALL]
    gs["rel_bias"] = gs["rel_bias"].reshape(rel_bias.shape)
    res = _small_allreduce_adamw("small_params", _pack([gs[n] for n in SMALL]), _pack(like),
                                 _pack([m_in[n] for n in SMALL]), _pack([v_in[n] for n in SMALL]))
    for d, packed in zip((grad, delta, new_m, new_v), res):
        for n, a in zip(SMALL, _unpack(packed, like)):
            d[n] = a
    return (loss, grad_x, *[grad[n] for n in ORDER], *[delta[n] for n in ORDER],
            *[new_m[n] for n in ORDER], *[new_v[n] for n in ORDER])
```

```python
import functools

import jax
import jax.numpy as jnp
from jax import lax
from jax.experimental import pallas as pl
from jax.experimental.pallas import tpu as pltpu

F32 = jnp.float32
BF16 = jnp.bfloat16

N_DEV = 8
EPS = 1e-6
NEG_INF = -1e30
CHUNK = 64
LEFT_CHUNKS = 8
PAD = LEFT_CHUNKS * CHUNK
QBLK = 4 * CHUNK
KBAND = PAD + QBLK
REL_CLIP = 128
ATTN_HEADS = 16
HEAD_DIM = 64
D_ATTN = ATTN_HEADS * HEAD_DIM
POOL_WINDOWS = (2, 4, 8, 16)
POOL_GROUP = 256
D_POOL = len(POOL_WINDOWS) * POOL_GROUP
CROSS_HEADS = 4
CROSS_DIM = 128
D_CROSS = CROSS_HEADS * CROSS_DIM
FFN_RES = 0.5
ADAM_LR, ADAM_B1, ADAM_B2, ADAM_EPS, ADAM_WD, ADAM_STEP = 0.001, 0.9, 0.999, 1e-08, 0.01, 10

NN = (((1,), (0,)), ((), ()))
NT = (((1,), (1,)), ((), ()))
TN = (((0,), (0,)), ((), ()))
MESH = pl.DeviceIdType.MESH
VMEM_LIMIT = 56 * 1024 * 1024


def _params(**kw):
    return pltpu.CompilerParams(vmem_limit_bytes=VMEM_LIMIT, **kw)


def _bf(v):
    return v if v.dtype == BF16 else v.astype(BF16)


WHOLE = ((Ellipsis,), (Ellipsis,))


def _gemm(name, a, a_spec, b, b_spec, dims, grid, outs, chunks=(WHOLE,), extras=(), epilogue=None, after=None):
    nex, nout = len(extras), len(outs)
    first_out = 2 + nex + (after is not None)

    def body(*refs):
        a_ref, b_ref = refs[:2]
        total = None
        for ia, ib in chunks:
            d = lax.dot_general(_bf(a_ref[ia]), _bf(b_ref[ib]), dims, preferred_element_type=F32)
            total = d if total is None else total + d
        vals = epilogue(total, *[e[...] for e in refs[2:2 + nex]]) if epilogue is not None else (total,)
        for r, v in zip(refs[first_out:first_out + nout], vals):
            r[...] = v.astype(r.dtype)

    operands = [a, b] + [x for x, _, _ in extras]
    in_specs = [pl.BlockSpec(*a_spec), pl.BlockSpec(*b_spec)] + [pl.BlockSpec(blk, m) for _, blk, m in extras]
    if after is not None:
        operands.append(after)
        in_specs.append(pl.BlockSpec(after.shape, lambda i, j: (0, 0)))
    res = pl.pallas_call(
        body, name=name, grid=grid, in_specs=in_specs,
        out_specs=[pl.BlockSpec(blk, m) for _, _, blk, m in outs],
        out_shape=[jax.ShapeDtypeStruct(s, d) for s, d, _, _ in outs],
        compiler_params=_params(dimension_semantics=("parallel", "parallel")),
    )(*operands)
    return res[0] if nout == 1 else res


def _tile(n, want):
    for t in range(min(n, want), 15, -1):
        if n % t == 0 and t % 16 == 0:
            return t
    return n


def _mm_nn(name, a, b, out_dtype, res=None, tm=1024, tn=1024):
    M, K = a.shape
    N = b.shape[1]
    tm, tn = _tile(M, tm), _tile(N, tn)
    extras = [] if res is None else [(res, (tm, tn), lambda i, j: (i, j))]
    epi = None if res is None else (lambda t, r: (r + t,))
    return _gemm(name, a, ((tm, K), lambda i, j: (i, 0)), b, ((K, tn), lambda i, j: (0, j)), NN,
                 (M // tm, N // tn), [((M, N), out_dtype, (tm, tn), lambda i, j: (i, j))], extras=extras, epilogue=epi)


def _mm_nn_cols(name, a, bs, out_dtype, res=None, tm=1024):
    M, K = a.shape
    nb, _, w = bs.shape
    tm = _tile(M, tm)
    extras = [] if res is None else [(res, (tm, w), lambda i, j: (i, j))]
    epi = None if res is None else (lambda t, r: (r + t,))
    return _gemm(name, a, ((tm, K), lambda i, j: (i, 0)), bs, ((None, K, w), lambda i, j: (j, 0, 0)), NN,
                 (M // tm, nb), [((M, nb * w), out_dtype, (tm, w), lambda i, j: (i, j))], extras=extras, epilogue=epi)


def _mm_nt(name, a, b, out_dtype, tm=1024, tn=1024, after=None):
    M, K = a.shape
    N = b.shape[0]
    tm, tn = _tile(M, tm), _tile(N, tn)
    return _gemm(name, a, ((tm, K), lambda i, j: (i, 0)), b, ((tn, K), lambda i, j: (j, 0)), NT,
                 (M // tm, N // tn), [((M, N), out_dtype, (tm, tn), lambda i, j: (i, j))], after=after)


def _mm_nt_cols(name, a, bs, out_dtype, tm=1024, tn=512, after=None):
    M = a.shape[0]
    nb, N, w = bs.shape
    tm, tn = _tile(M, tm), _tile(N, tn)
    chunks = [((slice(None), pl.ds(c * w, w)), (c,)) for c in range(nb)]
    return _gemm(name, a, ((tm, nb * w), lambda i, j: (i, 0)), bs, ((nb, tn, w), lambda i, j: (0, j, 0)), NT,
                 (M // tm, N // tn), [((M, N), out_dtype, (tm, tn), lambda i, j: (i, j))], chunks, after=after)


def _mm_tn(name, a, b, tm=1024, tn=1024, col_blocks=None):
    T, Ka = a.shape
    Nb = b.shape[1]
    tm = _tile(Ka, tm)
    if col_blocks is None:
        tn = _tile(Nb, tn)
        out = ((Ka, Nb), BF16, (tm, tn), lambda i, j: (i, j))
    else:
        tn = Nb // col_blocks
        out = ((col_blocks, Ka, tn), BF16, (None, tm, tn), lambda i, j: (j, i, 0))
    return _gemm(name, a, ((T, tm), lambda i, j: (0, i)), b, ((T, tn), lambda i, j: (0, j)), TN,
                 (Ka // tm, Nb // tn), [out])


def _ffn_out(name, a, wt, res=None, scale=1.0, tm=1024, tn=512, after=None):
    nb, M, w = a.shape
    N = wt.shape[1]
    tm, tn = _tile(M, tm), _tile(N, tn)
    chunks = [((c,), (pl.ds(c * w, w),)) for c in range(nb)]
    extras = [] if res is None else [(res, (tm, tn), lambda i, j: (i, j))]
    epi = None if res is None else (lambda t, r: (r + scale * t,))
    return _gemm(name, a, ((nb, tm, w), lambda i, j: (0, i, 0)), wt, ((nb * w, tn), lambda i, j: (0, j)), NN,
                 (M // tm, N // tn), [((M, N), F32, (tm, tn), lambda i, j: (i, j))], chunks, extras, epi, after)


def _ffn_dact(name, dhb, wd, g, u, tm=1024, after=None):
    M, K = dhb.shape
    nb, _, w = g.shape
    tm = _tile(M, tm)
    tr = _tile(tm, 512)
    tokens = [] if after is None else [after]

    def body(dh_ref, wd_ref, g_ref, u_ref, *rest):
        dg_ref, du_ref = rest[-2:]
        pieces = [pl.ds(r * tr, tr) for r in range(tm // tr)]
        dacts = [lax.dot_general(dh_ref[rows, :], wd_ref[...], NT, preferred_element_type=F32) for rows in pieces]
        for rows, dact in zip(pieces, dacts):
            dg, du = _swiglu_bwd(dact, g_ref[rows, :], u_ref[rows, :])
            dg_ref[rows, :] = dg.astype(BF16)
            du_ref[rows, :] = du.astype(BF16)

    hid = pl.BlockSpec((None, tm, w), lambda i, j: (j, i, 0))
    return pl.pallas_call(
        body, name=name, grid=(M // tm, nb),
        in_specs=[pl.BlockSpec((tm, K), lambda i, j: (i, 0)), pl.BlockSpec((w, K), lambda i, j: (j, 0)), hid, hid]
        + [pl.BlockSpec(t.shape, lambda i, j: (0, 0)) for t in tokens],
        out_specs=[hid, hid], out_shape=[jax.ShapeDtypeStruct((nb, M, w), BF16)] * 2,
        compiler_params=_params(dimension_semantics=("parallel", "parallel")),
    )(dhb, wd, g, u, *tokens)


def _ffn_dw(name, a, b, scale=1.0, tn=1024, after=None):
    nb, T, w = a.shape
    N = b.shape[1]
    tn = _tile(N, tn)
    epi = None if scale == 1.0 else (lambda t: (t * scale,))
    return _gemm(name, a, ((None, T, w), lambda i, j: (i, 0, 0)), b, ((T, tn), lambda i, j: (0, j)), TN,
                 (nb, N // tn), [((nb * w, N), BF16, (w, tn), lambda i, j: (i, j))], epilogue=epi, after=after)


def _ffn_gate(name, hn, wgt, nb, tm=1024):
    M, K = hn.shape
    w = wgt.shape[0] // nb
    tm = _tile(M, tm)
    return _gemm(name, hn, ((tm, K), lambda i, j: (i, 0)), wgt, ((w, K), lambda i, j: (j, 0)), NT,
                 (M // tm, nb), [((nb, M, w), BF16, (None, tm, w), lambda i, j: (j, i, 0))])


def _ffn_up_act(name, hn, wut, g, tm=1024):
    M, K = hn.shape
    nb, _, w = g.shape
    tm = _tile(M, tm)
    hid = ((None, tm, w), lambda i, j: (j, i, 0))

    def epilogue(u, gate):
        gate = gate.astype(F32)
        return u, gate * jax.nn.sigmoid(gate) * u

    return _gemm(name, hn, ((tm, K), lambda i, j: (i, 0)), wut, ((w, K), lambda i, j: (j, 0)), NT,
                 (M // tm, nb), [((nb, M, w), BF16) + hid] * 2, extras=[(g,) + hid], epilogue=epilogue)


def _ffn_up(name, hn, wgt, wut, nb, tm=1024):
    M, K = hn.shape
    w = wgt.shape[0] // nb
    tm = _tile(M, tm)

    def body(a_ref, g_ref, u_ref, og, ou, oa):
        a = a_ref[...]
        g = lax.dot_general(a, g_ref[...], NT, preferred_element_type=F32)
        u = lax.dot_general(a, u_ref[...], NT, preferred_element_type=F32)
        og[...] = g.astype(BF16)
        ou[...] = u.astype(BF16)
        oa[...] = (g * jax.nn.sigmoid(g) * u).astype(BF16)

    wspec = pl.BlockSpec((w, K), lambda i, j: (j, 0))
    ospec = pl.BlockSpec((None, tm, w), lambda i, j: (j, i, 0))
    return pl.pallas_call(
        body, name=name, grid=(M // tm, nb),
        in_specs=[pl.BlockSpec((tm, K), lambda i, j: (i, 0)), wspec, wspec],
        out_specs=[ospec] * 3, out_shape=[jax.ShapeDtypeStruct((nb, M, w), BF16)] * 3,
        compiler_params=_params(dimension_semantics=("parallel", "parallel")),
    )(hn, wgt, wut)


def _swiglu_bwd(dact, g, u):
    g = g.astype(F32)
    u = u.astype(F32)
    sig = jax.nn.sigmoid(g)
    silu = g * sig
    d = FFN_RES * dact
    return d * u * (sig * (1.0 + g * (1.0 - sig))), d * silu


def _rms_fwd(name, x, gain, tr=512, after=None):
    R, D = x.shape
    tr = _tile(R, tr)

    def body(x_ref, g_ref, *rest):
        xv = x_ref[...]
        y = xv * lax.rsqrt(jnp.mean(xv * xv, axis=-1, keepdims=True) + EPS)
        rest[-1][...] = (y * g_ref[...]).astype(BF16)

    tokens = [] if after is None else [after]
    return pl.pallas_call(
        body, name=name, grid=(R // tr,),
        in_specs=[pl.BlockSpec((tr, D), lambda i: (i, 0)), pl.BlockSpec((1, D), lambda i: (0, 0))]
        + [pl.BlockSpec(t.shape, lambda i: (0, 0)) for t in tokens],
        out_specs=pl.BlockSpec((tr, D), lambda i: (i, 0)), out_shape=jax.ShapeDtypeStruct((R, D), BF16),
        compiler_params=_params(dimension_semantics=("parallel",)),
    )(x, gain, *tokens)


def _rms_bwd_math(xv, gain, dy):
    rstd = lax.rsqrt(jnp.mean(xv * xv, axis=-1, keepdims=True) + EPS)
    xhat = xv * rstd
    dxh = dy * gain
    dx = rstd * (dxh - xhat * jnp.mean(dxh * xhat, axis=-1, keepdims=True))
    return dx, jnp.sum(dy * xhat, axis=0, keepdims=True)


def _rms_bwd(name, x, gain, dy, skip=None, tr=256):
    R, D = x.shape
    tr = _tile(R, tr)
    has_skip = skip is not None

    def body(*refs):
        x_ref, g_ref, dy_ref = refs[:3]
        dx_ref, dxb_ref, dg_ref = refs[-3:]
        dx, dg = _rms_bwd_math(x_ref[...], g_ref[...], dy_ref[...].astype(F32))
        if has_skip:
            dx = dx + refs[3][...]
        dx_ref[...] = dx
        dxb_ref[...] = dx.astype(BF16)

        @pl.when(pl.program_id(0) == 0)
        def _():
            dg_ref[...] = dg

        @pl.when(pl.program_id(0) > 0)
        def _():
            dg_ref[...] += dg

    row = pl.BlockSpec((tr, D), lambda i: (i, 0))
    vec = pl.BlockSpec((1, D), lambda i: (0, 0))
    return pl.pallas_call(
        body, name=name, grid=(R // tr,),
        in_specs=[row, vec, row] + ([row] if has_skip else []),
        out_specs=[row, row, vec],
        out_shape=[jax.ShapeDtypeStruct((R, D), F32), jax.ShapeDtypeStruct((R, D), BF16),
                   jax.ShapeDtypeStruct((1, D), F32)],
        compiler_params=_params(dimension_semantics=("arbitrary",)),
    )(*([x, gain, dy] + ([skip] if has_skip else [])))


def _loss_and_grad(name, h, gain, target, tr=256):
    R, D = h.shape
    tr = _tile(R, tr)

    def body(h_ref, g_ref, t_ref, loss_ref, dh_ref, dhb_ref, dg_ref):
        hv, gain_v = h_ref[...], g_ref[...]
        y = (hv * lax.rsqrt(jnp.mean(hv * hv, axis=-1, keepdims=True) + EPS)) * gain_v
        err = y - t_ref[...]
        part = jnp.full((8, 128), 0.5 * jnp.sum(jnp.mean(err * err, axis=-1, keepdims=True)), F32)
        dh, dg = _rms_bwd_math(hv, gain_v, err * (1.0 / D))
        dh_ref[...] = dh
        dhb_ref[...] = dh.astype(BF16)

        @pl.when(pl.program_id(0) == 0)
        def _():
            dg_ref[...] = dg
            loss_ref[...] = part

        @pl.when(pl.program_id(0) > 0)
        def _():
            dg_ref[...] += dg
            loss_ref[...] += part

    row = pl.BlockSpec((tr, D), lambda i: (i, 0))
    vec = pl.BlockSpec((1, D), lambda i: (0, 0))
    return pl.pallas_call(
        body, name=name, grid=(R // tr,), in_specs=[row, vec, row],
        out_specs=[pl.BlockSpec((8, 128), lambda i: (0, 0)), row, row, vec],
        out_shape=[jax.ShapeDtypeStruct((8, 128), F32), jax.ShapeDtypeStruct((R, D), F32),
                   jax.ShapeDtypeStruct((R, D), BF16), jax.ShapeDtypeStruct((1, D), F32)],
        compiler_params=_params(dimension_semantics=("arbitrary",)),
    )(h, gain, target)


def _bias_tile(name, rel):
    width = KBAND + QBLK
    sat = rel[:, 2 * REL_CLIP:]
    n_left = PAD - REL_CLIP + 1
    row0 = jnp.concatenate([jnp.broadcast_to(sat, (ATTN_HEADS, n_left)), rel[:, :2 * REL_CLIP][:, ::-1],
                            jnp.broadcast_to(sat, (ATTN_HEADS, width - n_left - 2 * REL_CLIP))], axis=1)

    def body(e_ref, o_ref):
        rows = pltpu.roll(jnp.broadcast_to(e_ref[...], (QBLK, width)), 0, 1, stride=1, stride_axis=0)
        i = lax.broadcasted_iota(jnp.int32, (QBLK, KBAND), 0) // CHUNK
        j = lax.broadcasted_iota(jnp.int32, (QBLK, KBAND), 1) // CHUNK
        o_ref[...] = jnp.where((j >= i) & (j <= i + LEFT_CHUNKS), rows[:, :KBAND], NEG_INF)

    return pl.pallas_call(
        body, name=name, grid=(ATTN_HEADS,),
        in_specs=[pl.BlockSpec((None, 1, width), lambda h: (h, 0, 0))],
        out_specs=pl.BlockSpec((None, QBLK, KBAND), lambda h: (h, 0, 0)),
        out_shape=jax.ShapeDtypeStruct((ATTN_HEADS, QBLK, KBAND), F32),
        compiler_params=_params(dimension_semantics=("parallel",)),
    )(row0.reshape(ATTN_HEADS, 1, width))


ATTN_SCALE = HEAD_DIM ** -0.5


def _stack_heads(x, first):
    zero = jnp.zeros_like(x)
    return jnp.concatenate([jnp.where(first, x, zero), jnp.where(first, zero, x)], axis=0)


def _band_softmax(q_half_scaled, kb, bias, left_mask):
    s = lax.dot_general(q_half_scaled, kb, NT, preferred_element_type=F32) + bias + left_mask
    e = jnp.exp(s - jnp.max(s, axis=-1, keepdims=True))
    return e * (1.0 / jnp.sum(e, axis=-1, keepdims=True))


def _left_mask(qb):
    kpos = qb * QBLK - PAD + lax.broadcasted_iota(jnp.int32, (1, KBAND), 1)
    return jnp.where(kpos >= 0, 0.0, NEG_INF).astype(F32)


def _fill_padded(dst, src, S):
    dst[pl.ds(0, PAD), :] = jnp.zeros((PAD, dst.shape[1]), dst.dtype)
    dst[pl.ds(PAD, S), :] = src[...].astype(dst.dtype)


def _attn_fwd(name, z, bias):
    B, S, _ = z.shape
    nh2 = ATTN_HEADS // 2

    def body(q_ref, k_ref, v_ref, b_ref, o_ref, kp, vp):
        qb = pl.program_id(2)

        @pl.when(qb == 0)
        def _():
            _fill_padded(kp, k_ref, S)
            _fill_padded(vp, v_ref, S)

        start = pl.multiple_of(qb * QBLK, QBLK)
        kb, vb = kp[pl.ds(start, KBAND), :], vp[pl.ds(start, KBAND), :]
        q = (q_ref[...] * ATTN_SCALE).astype(BF16)
        first = lax.broadcasted_iota(jnp.int32, (QBLK, 2 * HEAD_DIM), 1) < HEAD_DIM
        left = _left_mask(qb)
        p = _band_softmax(_stack_heads(q, first), kb, b_ref[...].reshape(2 * QBLK, KBAND), left)
        o = jnp.dot(p.astype(BF16), vb, preferred_element_type=F32)
        o_ref[...] = jnp.where(first, o[:QBLK], o[QBLK:]).astype(BF16)

    return pl.pallas_call(
        body, name=name, grid=(B, nh2, S // QBLK),
        in_specs=[pl.BlockSpec((None, QBLK, 128), lambda b, h, i: (b, i, h)),
                  pl.BlockSpec((None, S, 128), lambda b, h, i: (b, 0, nh2 + h)),
                  pl.BlockSpec((None, S, 128), lambda b, h, i: (b, 0, 2 * nh2 + h)),
                  pl.BlockSpec((2, QBLK, KBAND), lambda b, h, i: (h, 0, 0))],
        out_specs=pl.BlockSpec((None, QBLK, 128), lambda b, h, i: (b, i, h)),
        out_shape=jax.ShapeDtypeStruct((B, S, D_ATTN), BF16),
        scratch_shapes=[pltpu.VMEM((PAD + S, 128), BF16), pltpu.VMEM((PAD + S, 128), BF16)],
        compiler_params=_params(dimension_semantics=("parallel", "parallel", "arbitrary")),
    )(z, z, z, bias)


def _attn_bwd(name, z, bias, dcat):
    B, S, _ = z.shape
    nh2 = ATTN_HEADS // 2
    nqb = S // QBLK

    def body(q_ref, k_ref, v_ref, b_ref, do_ref, dq_ref, dk_ref, dv_ref, db_ref, kp, vp, dka, dva):
        b, qb = pl.program_id(1), pl.program_id(2)

        @pl.when(qb == 0)
        def _():
            _fill_padded(kp, k_ref, S)
            _fill_padded(vp, v_ref, S)
            dka[...] = jnp.zeros_like(dka)
            dva[...] = jnp.zeros_like(dva)

        @pl.when((qb == 0) & (b == 0))
        def _():
            db_ref[...] = jnp.zeros_like(db_ref)

        start = pl.multiple_of(qb * QBLK, QBLK)
        band = pl.ds(start, KBAND)
        kb, vb = kp[band, :], vp[band, :]
        q = (q_ref[...] * ATTN_SCALE).astype(BF16)
        do = do_ref[...]
        first = lax.broadcasted_iota(jnp.int32, (QBLK, 2 * HEAD_DIM), 1) < HEAD_DIM
        left = _left_mask(qb)
        q2, do2 = _stack_heads(q, first), _stack_heads(do, first)
        p = _band_softmax(q2, kb, b_ref[...].reshape(2 * QBLK, KBAND), left)
        dp = lax.dot_general(do2, vb, NT, preferred_element_type=F32)
        ds = p * (dp - jnp.sum(p * dp, axis=-1, keepdims=True))
        db_ref[...] += ds.reshape(2, QBLK, KBAND)
        dsb = ds.astype(BF16)
        dq = jnp.dot(dsb, kb, preferred_element_type=F32)
        dq_ref[...] = (jnp.where(first, dq[:QBLK], dq[QBLK:]) * ATTN_SCALE).astype(BF16)
        dka[band, :] += lax.dot_general(dsb, q2, TN, preferred_element_type=F32)
        dva[band, :] += lax.dot_general(p.astype(BF16), do2, TN, preferred_element_type=F32)

        @pl.when(qb == nqb - 1)
        def _():
            dk_ref[...] = dka[pl.ds(PAD, S), :].astype(BF16)
            dv_ref[...] = dva[pl.ds(PAD, S), :].astype(BF16)

    qspec = pl.BlockSpec((None, QBLK, 128), lambda h, b, i: (b, i, h))
    kvout = pl.BlockSpec((None, S, 128), lambda h, b, i: (b, 0, h))
    bspec = pl.BlockSpec((2, QBLK, KBAND), lambda h, b, i: (h, 0, 0))
    act = jax.ShapeDtypeStruct((B, S, D_ATTN), BF16)
    return pl.pallas_call(
        body, name=name, grid=(nh2, B, nqb),
        in_specs=[qspec,
                  pl.BlockSpec((None, S, 128), lambda h, b, i: (b, 0, nh2 + h)),
                  pl.BlockSpec((None, S, 128), lambda h, b, i: (b, 0, 2 * nh2 + h)),
                  bspec, qspec],
        out_specs=[qspec, kvout, kvout, bspec],
        out_shape=[act, act, act, jax.ShapeDtypeStruct((ATTN_HEADS, QBLK, KBAND), F32)],
        scratch_shapes=[pltpu.VMEM((PAD + S, 128), BF16), pltpu.VMEM((PAD + S, 128), BF16),
                        pltpu.VMEM((PAD + S, 128), F32), pltpu.VMEM((PAD + S, 128), F32)],
        compiler_params=_params(dimension_semantics=("arbitrary", "arbitrary", "arbitrary")),
    )(z, z, z, bias, dcat)


def _bias_grad(name, dbias):
    width = KBAND + QBLK

    def body(d_ref, o_ref):
        acc = jnp.zeros((1, width), F32)
        for i in range(QBLK):
            row = jnp.concatenate([d_ref[pl.ds(i, 1), :], jnp.zeros((1, QBLK), F32)], axis=1)
            shift = QBLK - 1 - i
            acc = acc + (pltpu.roll(row, shift, 1) if shift else row)
        o_ref[...] = acc

    return pl.pallas_call(
        body, name=name, grid=(ATTN_HEADS,),
        in_specs=[pl.BlockSpec((None, QBLK, KBAND), lambda h: (h, 0, 0))],
        out_specs=pl.BlockSpec((None, 1, width), lambda h: (h, 0, 0)),
        out_shape=jax.ShapeDtypeStruct((ATTN_HEADS, 1, width), F32),
        compiler_params=_params(dimension_semantics=("parallel",)),
    )(dbias)


def _rel_grad_from_diagonals(diag):
    top = PAD + QBLK - 1 - REL_CLIP
    sat = jnp.sum(diag[:, :top + 1], axis=1, keepdims=True)
    mid = diag[:, top + 1:top + 2 * REL_CLIP][:, ::-1]
    return jnp.concatenate([jnp.zeros_like(sat), mid, sat], axis=1)


def _shift_rows(x, k, forward):
    S = x.shape[0]
    t = lax.broadcasted_iota(jnp.int32, x.shape, 0)
    if forward:
        return jnp.where(t < S - k, pltpu.roll(x, S - k, 0), 0.0)
    return jnp.where(t >= k, pltpu.roll(x, k, 0), 0.0)


def _window_sum(x, g, forward):
    s = x + _shift_rows(x, 1, forward)
    out = s
    for n, k in enumerate((2, 4, 8)):
        s = s + _shift_rows(s, k, forward)
        out = jnp.where(g > n, s, out)
    return out


def _pool_count(S, g):
    t = lax.broadcasted_iota(jnp.int32, (S, 1), 0)
    w = jnp.left_shift(2, g)
    return jnp.minimum(t + 1, w).astype(F32)


def _pool_fwd(name, z, wp, pscale):
    B, S, _ = z.shape
    c0 = 3 * D_ATTN // POOL_GROUP

    def body(u_ref, w_ref, s_ref, d_ref, y_ref):
        g = pl.program_id(1)
        u = u_ref[...]
        d = (_window_sum(u, g, False) / _pool_count(S, g) - u).astype(BF16)
        d_ref[...] = d
        y_ref[...] = (jnp.dot(d, w_ref[...], preferred_element_type=F32) * s_ref[...]).astype(BF16)

    blk = pl.BlockSpec((None, S, POOL_GROUP), lambda b, g: (b, 0, g))
    out = jax.ShapeDtypeStruct((B, S, D_POOL), BF16)
    return pl.pallas_call(
        body, name=name, grid=(B, len(POOL_WINDOWS)),
        in_specs=[pl.BlockSpec((None, S, POOL_GROUP), lambda b, g: (b, 0, c0 + g)),
                  pl.BlockSpec((None, POOL_GROUP, POOL_GROUP), lambda b, g: (g, 0, 0)),
                  pl.BlockSpec((1, POOL_GROUP), lambda b, g: (0, g))],
        out_specs=[blk, blk], out_shape=[out, out],
        compiler_params=_params(dimension_semantics=("parallel", "parallel")),
    )(z, wp, pscale)


def _pool_bwd(name, d, wp, pscale, dcat):
    B, S, _ = d.shape
    c0 = D_ATTN // POOL_GROUP

    def body(d_ref, w_ref, s_ref, dy_ref, du_ref, dw_ref, dsc_ref):
        g, b = pl.program_id(0), pl.program_id(1)
        dv = d_ref[...]
        dy = dy_ref[...].astype(F32)
        w = w_ref[...]
        ypre = jnp.dot(dv, w, preferred_element_type=F32)
        dyp = (dy * s_ref[...]).astype(BF16)
        dd = lax.dot_general(dyp, w, NT, preferred_element_type=F32)
        du_ref[...] = (_window_sum(dd / _pool_count(S, g), g, True) - dd).astype(BF16)
        dw = lax.dot_general(dv, dyp, TN, preferred_element_type=F32)
        dsc = jnp.sum(dy * ypre, axis=0, keepdims=True)

        @pl.when(b == 0)
        def _():
            dw_ref[...] = dw
            dsc_ref[...] = dsc

        @pl.when(b > 0)
        def _():
            dw_ref[...] += dw
            dsc_ref[...] += dsc

    blk = pl.BlockSpec((None, S, POOL_GROUP), lambda g, b: (b, 0, g))
    wspec = pl.BlockSpec((None, POOL_GROUP, POOL_GROUP), lambda g, b: (g, 0, 0))
    sspec = pl.BlockSpec((1, POOL_GROUP), lambda g, b: (0, g))
    return pl.pallas_call(
        body, name=name, grid=(len(POOL_WINDOWS), B),
        in_specs=[blk, wspec, sspec, pl.BlockSpec((None, S, POOL_GROUP), lambda g, b: (b, 0, c0 + g))],
        out_specs=[blk, wspec, sspec],
        out_shape=[jax.ShapeDtypeStruct((B, S, D_POOL), BF16),
                   jax.ShapeDtypeStruct((len(POOL_WINDOWS), POOL_GROUP, POOL_GROUP), F32),
                   jax.ShapeDtypeStruct((1, D_POOL), F32)],
        compiler_params=_params(dimension_semantics=("arbitrary", "arbitrary")),
    )(d, wp, pscale, dcat)


def _cross_softmax(q, k):
    s = lax.dot_general(q, k, NT, preferred_element_type=F32) * (CROSS_DIM ** -0.5)
    e = jnp.exp(s - jnp.max(s, axis=-1, keepdims=True))
    return e * (1.0 / jnp.sum(e, axis=-1, keepdims=True))


def _cross_fwd(name, qc, kv, tq=512):
    B, S, _ = qc.shape
    M = kv.shape[1]
    tq = _tile(S, tq)

    def body(q_ref, k_ref, v_ref, o_ref):
        p = _cross_softmax(q_ref[...], k_ref[...])
        o_ref[...] = jnp.dot(p.astype(BF16), v_ref[...], preferred_element_type=F32).astype(BF16)

    qspec = pl.BlockSpec((None, tq, CROSS_DIM), lambda b, h, i: (b, i, h))
    return pl.pallas_call(
        body, name=name, grid=(B, CROSS_HEADS, S // tq),
        in_specs=[qspec, pl.BlockSpec((None, M, CROSS_DIM), lambda b, h, i: (b, 0, h)),
                  pl.BlockSpec((None, M, CROSS_DIM), lambda b, h, i: (b, 0, CROSS_HEADS + h))],
        out_specs=qspec, out_shape=jax.ShapeDtypeStruct((B, S, D_CROSS), BF16),
        compiler_params=_params(dimension_semantics=("parallel", "parallel", "parallel")),
    )(qc, kv, kv)


def _cross_bwd(name, qc, kv, do, tq=512):
    B, S, _ = qc.shape
    M = kv.shape[1]
    tq = _tile(S, tq)
    nq = S // tq
    scale = CROSS_DIM ** -0.5

    def body(q_ref, k_ref, v_ref, do_ref, dq_ref, dk_ref, dv_ref, dka, dva):
        i = pl.program_id(2)
        q, k, v, dov = q_ref[...], k_ref[...], v_ref[...], do_ref[...]
        p = _cross_softmax(q, k)
        dp = lax.dot_general(dov, v, NT, preferred_element_type=F32)
        ds = ((p * (dp - jnp.sum(p * dp, axis=-1, keepdims=True))) * scale).astype(BF16)
        dq_ref[...] = jnp.dot(ds, k, preferred_element_type=F32).astype(BF16)
        dk = lax.dot_general(ds, q, TN, preferred_element_type=F32)
        dv = lax.dot_general(p.astype(BF16), dov, TN, preferred_element_type=F32)

        @pl.when(i == 0)
        def _():
            dka[...] = dk
            dva[...] = dv

        @pl.when(i > 0)
        def _():
            dka[...] += dk
            dva[...] += dv

        @pl.when(i == nq - 1)
        def _():
            dk_ref[...] = dka[...].astype(BF16)
            dv_ref[...] = dva[...].astype(BF16)

    qspec = pl.BlockSpec((None, tq, CROSS_DIM), lambda b, h, i: (b, i, h))
    kspec = pl.BlockSpec((None, M, CROSS_DIM), lambda b, h, i: (b, 0, h))
    return pl.pallas_call(
        body, name=name, grid=(B, CROSS_HEADS, nq),
        in_specs=[qspec, kspec, pl.BlockSpec((None, M, CROSS_DIM), lambda b, h, i: (b, 0, CROSS_HEADS + h)), qspec],
        out_specs=[qspec, kspec, kspec],
        out_shape=[jax.ShapeDtypeStruct((B, S, D_CROSS), BF16), jax.ShapeDtypeStruct((B, M, D_CROSS), BF16),
                   jax.ShapeDtypeStruct((B, M, D_CROSS), BF16)],
        scratch_shapes=[pltpu.VMEM((M, CROSS_DIM), F32), pltpu.VMEM((M, CROSS_DIM), F32)],
        compiler_params=_params(dimension_semantics=("parallel", "parallel", "arbitrary")),
    )(qc, kv, kv, do)


def _local_step(x, mem, target, small, weights, emit, start_token=None):
    B, S, D = x.shape
    T = B * S
    x2, t2 = x.reshape(T, D), target.reshape(T, D)
    mem2 = mem.reshape(-1, D)
    n_mem = mem.shape[1]
    wts = {}

    hn1 = _rms_fwd("norm_ffn1", x2, small["ffn1_norm"], after=start_token)
    wts.update(weights(0, hn1))
    g1 = _ffn_gate("ffn1_gate", hn1, wts["ffn1_w_gate"], N_DEV)
    wts.update(weights(1, g1))
    u1, a1 = _ffn_up_act("ffn1_up", hn1, wts["ffn1_w_up"], g1)
    wts.update(weights(2, a1))
    h1 = _ffn_out("ffn1_down", a1, wts["ffn1_w_down"], res=x2, scale=FFN_RES)
    hn2 = _rms_fwd("norm_mix", h1, small["mix_norm"])
    wts.update(weights(3, hn2))
    z = _mm_nn_cols("mix_in", hn2, wts["w_in"], F32).reshape(B, S, -1)
    bias = _bias_tile("bias_tile", small["rel_bias"])
    y_attn = _attn_fwd("attn_fwd", z, bias)
    d_pool, y_pool = _pool_fwd("pool_fwd", z, wts["w_pool"], small["pool_scale"])
    cat = jnp.concatenate([y_attn, y_pool], axis=-1).reshape(T, -1)
    h2 = _mm_nn("mix_out", cat, wts["w_out"], F32, res=h1)
    hn3 = _rms_fwd("norm_cross", h2, small["cross_norm"])
    memn = _rms_fwd("norm_mem", mem2, small["mem_norm"])
    wts.update(weights(4, hn3))
    qc = _mm_nn("cross_q", hn3, wts["w_cq"], BF16)
    kv = _mm_nn("cross_kv", memn, wts["w_ckv"], BF16)
    o = _cross_fwd("cross_fwd", qc.reshape(B, S, -1), kv.reshape(B, n_mem, -1)).reshape(T, -1)
    h3 = _mm_nn_cols("cross_out", o, wts["w_co"], F32, res=h2)
    hn4 = _rms_fwd("norm_ffn2", h3, small["ffn2_norm"])
    wts.update(weights(5, hn4))
    g2, u2, a2 = _ffn_up("ffn2_up", hn4, wts["ffn2_w_gate"], wts["ffn2_w_up"], N_DEV)
    h4 = _ffn_out("ffn2_down", a2, wts["ffn2_w_down"], res=h3, scale=FFN_RES)

    gs = {}
    loss_part, dh4, dh4b, gs["final_norm"] = _loss_and_grad("loss", h4, small["final_norm"], t2)

    def ffn_bwd(tag, dh, dhb, h_in, hn, g, u, a, wg, wu, wd, gain):
        tok = emit({tag + "_w_down": _ffn_dw(tag + "_dwd", a, dhb, scale=FFN_RES)})
        dg, du = _ffn_dact(tag + "_dact", dhb, wd, g, u, after=tok)
        tok = emit({tag + "_w_gate": _ffn_dw(tag + "_dwg", dg, hn)})
        tok = emit({tag + "_w_up": _ffn_dw(tag + "_dwu", du, hn, after=tok)})
        dhn = _ffn_out(tag + "_dhn_g", dg, wg, after=tok)
        dhn = _ffn_out(tag + "_dhn_u", du, wu, res=dhn)
        return _rms_bwd(tag + "_dnorm", h_in, gain, dhn, skip=dh)

    dh3, dh3b, gs["ffn2_norm"] = ffn_bwd("ffn2", dh4, dh4b, h3, hn4, g2, u2, a2, wts["ffn2_w_gate"],
                                         wts["ffn2_w_up"], wts["ffn2_w_down"], small["ffn2_norm"])
    do = _mm_nt_cols("cross_do", dh3b, wts["w_co"], BF16, tn=D_CROSS)
    gw = {"w_co": _mm_tn("cross_dwo", o, dh3b, tm=D_CROSS, col_blocks=N_DEV)}
    dqc, dk, dv = _cross_bwd("cross_bwd", qc.reshape(B, S, -1), kv.reshape(B, n_mem, -1), do.reshape(B, S, -1))
    dqc = dqc.reshape(T, -1)
    dkv = jnp.concatenate([dk, dv], axis=-1).reshape(B * n_mem, -1)
    gw["w_cq"] = _mm_tn("cross_dwq", hn3, dqc, tn=D_CROSS)
    gw["w_ckv"] = _mm_tn("cross_dwkv", memn, dkv)
    tok = emit(gw)
    dhn3 = _mm_nt("cross_dhn", dqc, wts["w_cq"], F32, after=tok)
    dmemn = _mm_nt("cross_dmem", dkv, wts["w_ckv"], F32, tm=512)
    _, _, gs["mem_norm"] = _rms_bwd("mem_dnorm", mem2, small["mem_norm"], dmemn)
    dh2, dh2b, gs["cross_norm"] = _rms_bwd("cross_dnorm", h2, small["cross_norm"], dhn3, skip=dh3)
    dcat = _mm_nt("mix_dcat", dh2b, wts["w_out"], BF16)
    gw = {"w_out": _mm_tn("mix_dwout", cat, dh2b)}
    dcat3 = dcat.reshape(B, S, -1)
    dq, dkk, dvv, dbias = _attn_bwd("attn_bwd", z, bias, dcat3)
    du, gw["w_pool"], gs["pool_scale"] = _pool_bwd("pool_bwd", d_pool, wts["w_pool"], small["pool_scale"], dcat3)
    gs["rel_bias"] = _rel_grad_from_diagonals(_bias_grad("bias_grad", dbias)[:, 0, :])
    dz = jnp.concatenate([dq, dkk, dvv, du], axis=-1).reshape(T, -1)
    gw["w_in"] = _mm_tn("mix_dwin", hn2, dz, col_blocks=N_DEV)
    tok = emit(gw)
    dhn2 = _mm_nt_cols("mix_dhn", dz, wts["w_in"], F32, after=tok)
    dh1, dh1b, gs["mix_norm"] = _rms_bwd("mix_dnorm", h1, small["mix_norm"], dhn2, skip=dh2)
    dx, _, gs["ffn1_norm"] = ffn_bwd("ffn1", dh1, dh1b, x2, hn1, g1, u1, a1, wts["ffn1_w_gate"],
                                     wts["ffn1_w_up"], wts["ffn1_w_down"], small["ffn1_norm"])
    return loss_part, dx.reshape(B, S, D), gs


def _position():
    return lax.axis_index("x"), lax.axis_index("y"), lax.axis_index("c")


def _index(p):
    return 4 * p[0] + 2 * p[1] + p[2]


HBM_SPEC = pl.BlockSpec(memory_space=pltpu.HBM)
SEM_SPEC = pl.BlockSpec(memory_space=pltpu.SEMAPHORE)
ANY_SPEC = pl.BlockSpec(memory_space=pl.ANY)
ORDERED_EFFECT = pltpu.SideEffectType.DATAFLOW_SIDE_EFFECTING


N_COPIES = {"grads": N_DEV - 1, "spread": 4, "relay": 3}


def _copies(pattern, srcs, lands, send, recv):
    x, y, c = _position()
    me, sibling = _index((x, y, c)), (x, y, 1 - c)
    chips = [(1 - x, y), (x, 1 - y), (1 - x, 1 - y)]
    if pattern == "grads":
        targets = [(x ^ (k >> 2), y ^ ((k >> 1) & 1), c ^ (k & 1)) for k in range(1, N_DEV)]
    else:
        targets = [sibling] + [(*chip, c) for chip in chips]
    per, out = N_COPIES[pattern], []
    for k in range(per):
        for a in range(len(lands)):
            if pattern == "relay":
                src = dst = lands[a].at[_index((*chips[k], c))]
                to = sibling
            else:
                to = targets[k]
                src = srcs[a].at[_index(to)] if pattern == "grads" else srcs[a]
                dst = lands[a].at[me]
            out.append(pltpu.make_async_remote_copy(src_ref=src, dst_ref=dst, send_sem=send.at[a * per + k],
                                                    recv_sem=recv.at[a * per + k], device_id=to, device_id_type=MESH))
    return out


def _comm(name, srcs, lands, wait=None, start=None, after=None):
    ns, nl = len(srcs), len(lands)
    na = ns + nl
    arrays = list(srcs) + list(lands)
    n_wait = 2 if wait else 0
    n_start = 2 if start else 0

    def body(*refs):
        ins, lnd = refs[:ns], refs[ns:na]
        if wait:
            for cp in _copies(wait[0], ins, lnd, refs[na], refs[na + 1]):
                cp.wait_send()
                cp.wait_recv()
        if start:
            outs = refs[na + n_wait + (after is not None):]
            for cp in _copies(start, ins, lnd, outs[0], outs[1]):
                cp.start()
            refs[-1][...] = jnp.zeros((8, 128), F32)

    out_shape, out_specs = [], []
    if start:
        sems = pltpu.SemaphoreType.DMA((nl * N_COPIES[start],))
        out_shape += [sems, sems]
        out_specs += [SEM_SPEC, SEM_SPEC]
    out_shape += [pltpu.HBM(a.shape, a.dtype) for a in arrays]
    out_specs += [HBM_SPEC] * na
    if start:
        out_shape.append(jax.ShapeDtypeStruct((8, 128), F32))
        out_specs.append(pl.BlockSpec(memory_space=pltpu.VMEM))
    operands = [pltpu.with_memory_space_constraint(a, pltpu.HBM) for a in arrays]
    operands += list(wait[1:]) if wait else []
    operands += [after] if after is not None else []
    res = pl.pallas_call(
        body, name=name, out_shape=out_shape, out_specs=out_specs,
        in_specs=[HBM_SPEC] * na + [SEM_SPEC] * n_wait + [ANY_SPEC] * (after is not None),
        input_output_aliases={i: n_start + i for i in range(na)},
        compiler_params=pltpu.CompilerParams(has_side_effects=ORDERED_EFFECT),
    )(*operands)
    res = list(res)
    thru = res[n_start:n_start + na]
    return thru[:ns], thru[ns:], (tuple(res[:2]) if start else None), (res[-1] if start else None)


def _adamw_math(w, g, m, v):
    m = ADAM_B1 * m + (1.0 - ADAM_B1) * g
    v = ADAM_B2 * v + (1.0 - ADAM_B2) * (g * g)
    m_hat = m / (1.0 - ADAM_B1 ** ADAM_STEP)
    v_hat = v / (1.0 - ADAM_B2 ** ADAM_STEP)
    delta = -ADAM_LR * (m_hat / (jnp.sqrt(v_hat) + ADAM_EPS) + ADAM_WD * w)
    return delta, m, v


def _adamw(name, parts, w, m, v, tr=128):
    R, C = w.shape
    tr = _tile(R, tr)

    def body(p_ref, w_ref, m_ref, v_ref, g_out, d_out, m_out, v_out):
        g = p_ref[0].astype(F32)
        for d in range(1, N_DEV):
            g = g + p_ref[d].astype(F32)
        g_out[...] = g
        d_out[...], m_out[...], v_out[...] = _adamw_math(w_ref[...], g, m_ref[...], v_ref[...])

    row = pl.BlockSpec((tr, C), lambda i: (i, 0))
    out = jax.ShapeDtypeStruct((R, C), F32)
    return pl.pallas_call(
        body, name=name, grid=(R // tr,),
        in_specs=[pl.BlockSpec((N_DEV, tr, C), lambda i: (0, i, 0)), row, row, row],
        out_specs=[row] * 4, out_shape=[out] * 4,
        compiler_params=_params(dimension_semantics=("parallel",)),
    )(parts, w, m, v)


def _small_allreduce_adamw(name, g, w, m, v):
    R = g.shape[0]

    def body(g_ref, w_ref, m_ref, v_ref, g_out, d_out, m_out, v_out, land, send, recv):
        x, y, c = _position()
        me = _index((x, y, c))
        land[me] = g_ref[...]
        copies = []
        for k in range(1, N_DEV):
            peer = (x ^ (k >> 2), y ^ ((k >> 1) & 1), c ^ (k & 1))
            copies.append(pltpu.make_async_remote_copy(
                src_ref=g_ref, dst_ref=land.at[me], send_sem=send.at[k - 1], recv_sem=recv.at[k - 1],
                device_id=peer, device_id_type=MESH))
        for cp in copies:
            cp.start()
        for cp in copies:
            cp.wait()
        total = land[0]
        for d in range(1, N_DEV):
            total = total + land[d]
        g_out[...] = total
        d_out[...], m_out[...], v_out[...] = _adamw_math(w_ref[...], total, m_ref[...], v_ref[...])

    vm = pl.BlockSpec(memory_space=pltpu.VMEM)
    out = jax.ShapeDtypeStruct((R, 128), F32)
    return pl.pallas_call(
        body, name=name, in_specs=[vm] * 4, out_specs=[vm] * 4, out_shape=[out] * 4,
        scratch_shapes=[pltpu.VMEM((N_DEV, R, 128), F32), pltpu.SemaphoreType.DMA((7,)),
                        pltpu.SemaphoreType.DMA((7,))],
    )(g, w, m, v)


BIG = ("ffn1_w_gate", "ffn1_w_up", "ffn1_w_down", "w_in", "w_pool", "w_out", "w_cq", "w_ckv", "w_co",
       "ffn2_w_gate", "ffn2_w_up", "ffn2_w_down")
SMALL = ("ffn1_norm", "mix_norm", "rel_bias", "pool_scale", "cross_norm", "mem_norm", "ffn2_norm", "final_norm")
ORDER = ("ffn1_norm", "ffn1_w_gate", "ffn1_w_up", "ffn1_w_down", "mix_norm", "w_in", "rel_bias", "w_pool",
         "pool_scale", "w_out", "cross_norm", "mem_norm", "w_cq", "w_ckv", "w_co", "ffn2_norm", "ffn2_w_gate",
         "ffn2_w_up", "ffn2_w_down", "final_norm")
TRANSPOSED = ("ffn1_w_gate", "ffn1_w_up", "ffn2_w_gate", "ffn2_w_up")
ROW_SHARDED = TRANSPOSED + ("ffn1_w_down", "ffn2_w_down", "w_out", "w_cq", "w_ckv")
GATHER_GROUPS = (("ffn1_w_gate",), ("ffn1_w_up",), ("ffn1_w_down",), ("w_in", "w_pool", "w_out"),
                 ("w_cq", "w_ckv", "w_co"), ("ffn2_w_gate", "ffn2_w_up", "ffn2_w_down"))
RELAY_BEFORE_USE = ((0,), (1,), (2,), (3,), (4, 5), ())


def _pack(arrays):
    flat = jnp.concatenate([a.reshape(-1) for a in arrays])
    rows = -(-flat.shape[0] // 1024) * 8
    return jnp.pad(flat, (0, rows * 128 - flat.shape[0])).reshape(rows, 128)


def _unpack(packed, like):
    flat, out, at = packed.reshape(-1), [], 0
    for a in like:
        out.append(flat[at:at + a.size].reshape(a.shape))
        at += a.size
    return out


def _shard2d(a):
    a = a[0]
    return a.reshape(-1, a.shape[-1])


def kernel(x, mem, ffn1_norm, ffn1_w_gate, ffn1_w_up, ffn1_w_down, mix_norm, w_in, rel_bias, w_pool, pool_scale, w_out, cross_norm, mem_norm, w_cq, w_ckv, w_co, ffn2_norm, ffn2_w_gate, ffn2_w_up, ffn2_w_down, final_norm, loss_target, m_ffn1_norm, m_ffn1_w_gate, m_ffn1_w_up, m_ffn1_w_down, m_mix_norm, m_w_in, m_rel_bias, m_w_pool, m_pool_scale, m_w_out, m_cross_norm, m_mem_norm, m_w_cq, m_w_ckv, m_w_co, m_ffn2_norm, m_ffn2_w_gate, m_ffn2_w_up, m_ffn2_w_down, m_final_norm, v_ffn1_norm, v_ffn1_w_gate, v_ffn1_w_up, v_ffn1_w_down, v_mix_norm, v_w_in, v_rel_bias, v_w_pool, v_pool_scale, v_w_out, v_cross_norm, v_mem_norm, v_w_cq, v_w_ckv, v_w_co, v_ffn2_norm, v_ffn2_w_gate, v_ffn2_w_up, v_ffn2_w_down, v_final_norm):
    args = dict(locals())
    def view(n, a):
        return a.transpose(0, 2, 1) if n in TRANSPOSED else a

    w_in_ = {n: view(n, args[n]) for n in ORDER}
    m_in = {n: view(n, args["m_" + n]) for n in ORDER}
    v_in = {n: view(n, args["v_" + n]) for n in ORDER}

    me = 4 * lax.axis_index("x") + 2 * lax.axis_index("y") + lax.axis_index("c")
    n_g, rows = len(POOL_WINDOWS), POOL_GROUP // N_DEV

    def own_block_placed(block):
        return lax.dynamic_update_slice_in_dim(lax.empty((N_DEV,) + block.shape, block.dtype), block[None], me, 0)

    gathers, tok = [], None
    for gi, group in enumerate(GATHER_GROUPS):
        shards = [_shard2d(w_in_[n]).astype(BF16) for n in group]
        srcs, lands, sems, tok = _comm("gather_start_%d" % gi, shards, [own_block_placed(s) for s in shards],
                                       start="spread", after=tok)
        gathers.append((srcs, lands, sems))

    def weights(gi, after):
        for ri in RELAY_BEFORE_USE[gi]:
            srcs, lands, sems = gathers[ri]
            _, lands, sems, after = _comm("gather_relay_%d" % ri, srcs, lands, wait=("spread",) + sems,
                                          start="relay", after=after)
            gathers[ri] = (None, lands, sems)
        _, lands, sems = gathers[gi]
        _, lands, _, _ = _comm("gather_finish_%d" % gi, [], lands, wait=("relay",) + sems, after=after)
        out = {}
        for n, full in zip(GATHER_GROUPS[gi], lands):
            if n == "w_pool":
                full = full.reshape(N_DEV, n_g, rows, POOL_GROUP).transpose(1, 0, 2, 3).reshape(n_g, POOL_GROUP, POOL_GROUP)
            out[n] = full.reshape(-1, full.shape[-1]) if n in ROW_SHARDED else full
        return out

    scatters = []

    def emit(gw):
        names = list(gw)
        stacks = []
        for n in names:
            g = gw[n]
            if n == "w_pool":
                g = g.reshape(n_g, N_DEV, rows, POOL_GROUP).transpose(1, 0, 2, 3).astype(BF16)
            stacks.append(g.reshape((N_DEV,) + _shard2d(w_in_[n]).shape))
        lands = [own_block_placed(lax.dynamic_index_in_dim(s, me, 0, keepdims=False)) for s in stacks]
        srcs, lands, sems, token = _comm("grads_start_%d" % len(scatters), stacks, lands, start="grads")
        scatters.append((names, srcs, lands, sems))
        return token

    small = {n: w_in_[n].reshape(1, -1) for n in SMALL if n != "rel_bias"}
    small["rel_bias"] = rel_bias[0]
    loss_part, grad_x, gs = _local_step(x, mem, loss_target, small, weights, emit, start_token=tok)
    loss = lax.psum(loss_part[0, 0], ("x", "y", "c"))

    grad, delta, new_m, new_v = {}, {}, {}, {}
    after = grad_x
    for si, (names, srcs, lands, sems) in enumerate(scatters):
        _, landed, _, _ = _comm("grads_finish_%d" % si, srcs, lands, wait=("grads",) + sems, after=after)
        for n, parts in zip(names, landed):
            res = _adamw("adamw_" + n, parts, _shard2d(w_in_[n]), _shard2d(m_in[n]), _shard2d(v_in[n]))
            grad[n], delta[n], new_m[n], new_v[n] = [view(n, r.reshape(w_in_[n].shape)) for r in res]
        after = res[0]

    like = [w_in_[n] for n in SMALL]
    gs["rel_bias"] = gs["rel_bias"].reshape(rel_bias.shape)
    res = _small_allreduce_adamw("small_params", _pack([gs[n] for n in SMALL]), _pack(like),
                                 _pack([m_in[n] for n in SMALL]), _pack([v_in[n] for n in SMALL]))
    for d, packed in zip((grad, delta, new_m, new_v), res):
        for n, a in zip(SMALL, _unpack(packed, like)):
            d[n] = a
    return (loss, grad_x, *[grad[n] for n in ORDER], *[delta[n] for n in ORDER],
            *[new_m[n] for n in ORDER], *[new_v[n] for n in ORDER])
```

```python
import functools

import jax
import jax.numpy as jnp
from jax import lax
from jax.experimental import pallas as pl
from jax.experimental.pallas import tpu as pltpu

F32 = jnp.float32
BF16 = jnp.bfloat16

N_DEV = 8
EPS = 1e-6
NEG_INF = -1e30
CHUNK = 64
LEFT_CHUNKS = 8
PAD = LEFT_CHUNKS * CHUNK
QBLK = 4 * CHUNK
KBAND = PAD + QBLK
REL_CLIP = 128
ATTN_HEADS = 16
HEAD_DIM = 64
D_ATTN = ATTN_HEADS * HEAD_DIM
POOL_WINDOWS = (2, 4, 8, 16)
POOL_GROUP = 256
D_POOL = len(POOL_WINDOWS) * POOL_GROUP
CROSS_HEADS = 4
CROSS_DIM = 128
D_CROSS = CROSS_HEADS * CROSS_DIM
FFN_RES = 0.5
ADAM_LR, ADAM_B1, ADAM_B2, ADAM_EPS, ADAM_WD, ADAM_STEP = 0.001, 0.9, 0.999, 1e-08, 0.01, 10

NN = (((1,), (0,)), ((), ()))
NT = (((1,), (1,)), ((), ()))
TN = (((0,), (0,)), ((), ()))
MESH = pl.DeviceIdType.MESH
VMEM_LIMIT = 56 * 1024 * 1024


def _params(**kw):
    return pltpu.CompilerParams(vmem_limit_bytes=VMEM_LIMIT, **kw)


def _bf(v):
    return v if v.dtype == BF16 else v.astype(BF16)


WHOLE = ((Ellipsis,), (Ellipsis,))


def _rms(xv, gain):
    return (xv * lax.rsqrt(jnp.mean(xv * xv, axis=-1, keepdims=True) + EPS)) * gain


def _gemm(name, a, a_spec, b, b_spec, dims, grid, outs, chunks=(WHOLE,), extras=(), epilogue=None, after=None,
          norm=None):
    nex, nout = len(extras), len(outs)
    first_out = 2 + nex + (after is not None) + (norm is not None)

    def body(*refs):
        a_ref, b_ref = refs[:2]
        if norm is not None:
            hn_out, a_ref = refs[first_out + nout], refs[-1]

            @pl.when(pl.program_id(1) == 0)
            def _():
                hn = _rms(refs[0][...], refs[first_out - 1][...]).astype(BF16)
                a_ref[...] = hn
                hn_out[...] = hn

        total = None
        for ia, ib in chunks:
            d = lax.dot_general(_bf(a_ref[ia]), _bf(b_ref[ib]), dims, preferred_element_type=F32)
            total = d if total is None else total + d
        vals = epilogue(total, *[e[...] for e in refs[2:2 + nex]]) if epilogue is not None else (total,)
        for r, v in zip(refs[first_out:first_out + nout], vals):
            r[...] = v.astype(r.dtype)

    operands = [a, b] + [x for x, _, _ in extras]
    in_specs = [pl.BlockSpec(*a_spec), pl.BlockSpec(*b_spec)] + [pl.BlockSpec(blk, m) for _, blk, m in extras]
    if after is not None:
        operands.append(after)
        in_specs.append(pl.BlockSpec(after.shape, lambda i, j: (0, 0)))
    out_specs = [pl.BlockSpec(blk, m) for _, _, blk, m in outs]
    out_shape = [jax.ShapeDtypeStruct(s, d) for s, d, _, _ in outs]
    scratch = []
    if norm is not None:
        operands.append(norm)
        in_specs.append(pl.BlockSpec(norm.shape, lambda i, j: (0, 0)))
        out_specs.append(pl.BlockSpec(*a_spec))
        out_shape.append(jax.ShapeDtypeStruct(a.shape, BF16))
        scratch.append(pltpu.VMEM(a_spec[0], BF16))
    res = pl.pallas_call(
        body, name=name, grid=grid, in_specs=in_specs, out_specs=out_specs, out_shape=out_shape,
        scratch_shapes=scratch,
        compiler_params=_params(dimension_semantics=("parallel", "arbitrary" if norm is not None else "parallel")),
    )(*operands)
    return res[0] if len(res) == 1 else res


def _tile(n, want):
    for t in range(min(n, want), 15, -1):
        if n % t == 0 and t % 16 == 0:
            return t
    return n


def _mm_nn(name, a, b, out_dtype, res=None, tm=1024, tn=1024, norm=None):
    M, K = a.shape
    N = b.shape[1]
    tm, tn = _tile(M, tm), _tile(N, tn)
    extras = [] if res is None else [(res, (tm, tn), lambda i, j: (i, j))]
    epi = None if res is None else (lambda t, r: (r + t,))
    return _gemm(name, a, ((tm, K), lambda i, j: (i, 0)), b, ((K, tn), lambda i, j: (0, j)), NN,
                 (M // tm, N // tn), [((M, N), out_dtype, (tm, tn), lambda i, j: (i, j))], extras=extras, epilogue=epi,
                 norm=norm)


def _mm_nn_cols(name, a, bs, out_dtype, res=None, tm=1024, norm=None):
    M, K = a.shape
    nb, _, w = bs.shape
    tm = _tile(M, tm)
    extras = [] if res is None else [(res, (tm, w), lambda i, j: (i, j))]
    epi = None if res is None else (lambda t, r: (r + t,))
    return _gemm(name, a, ((tm, K), lambda i, j: (i, 0)), bs, ((None, K, w), lambda i, j: (j, 0, 0)), NN,
                 (M // tm, nb), [((M, nb * w), out_dtype, (tm, w), lambda i, j: (i, j))], extras=extras, epilogue=epi,
                 norm=norm)


def _mm_nt(name, a, b, out_dtype, tm=1024, tn=1024, after=None):
    M, K = a.shape
    N = b.shape[0]
    tm, tn = _tile(M, tm), _tile(N, tn)
    return _gemm(name, a, ((tm, K), lambda i, j: (i, 0)), b, ((tn, K), lambda i, j: (j, 0)), NT,
                 (M // tm, N // tn), [((M, N), out_dtype, (tm, tn), lambda i, j: (i, j))], after=after)


def _mm_nt_cols(name, a, bs, out_dtype, tm=1024, tn=512, after=None):
    M = a.shape[0]
    nb, N, w = bs.shape
    tm, tn = _tile(M, tm), _tile(N, tn)
    chunks = [((slice(None), pl.ds(c * w, w)), (c,)) for c in range(nb)]
    return _gemm(name, a, ((tm, nb * w), lambda i, j: (i, 0)), bs, ((nb, tn, w), lambda i, j: (0, j, 0)), NT,
                 (M // tm, N // tn), [((M, N), out_dtype, (tm, tn), lambda i, j: (i, j))], chunks, after=after)


def _mm_tn(name, a, b, tm=1024, tn=1024, col_blocks=None):
    T, Ka = a.shape
    Nb = b.shape[1]
    tm = _tile(Ka, tm)
    if col_blocks is None:
        tn = _tile(Nb, tn)
        out = ((Ka, Nb), BF16, (tm, tn), lambda i, j: (i, j))
    else:
        tn = Nb // col_blocks
        out = ((col_blocks, Ka, tn), BF16, (None, tm, tn), lambda i, j: (j, i, 0))
    return _gemm(name, a, ((T, tm), lambda i, j: (0, i)), b, ((T, tn), lambda i, j: (0, j)), TN,
                 (Ka // tm, Nb // tn), [out])


def _ffn_out(name, a, wt, res=None, scale=1.0, tm=1024, tn=512, after=None):
    nb, M, w = a.shape
    N = wt.shape[1]
    tm, tn = _tile(M, tm), _tile(N, tn)
    chunks = [((c,), (pl.ds(c * w, w),)) for c in range(nb)]
    extras = [] if res is None else [(res, (tm, tn), lambda i, j: (i, j))]
    epi = None if res is None else (lambda t, r: (r + scale * t,))
    return _gemm(name, a, ((nb, tm, w), lambda i, j: (0, i, 0)), wt, ((nb * w, tn), lambda i, j: (0, j)), NN,
                 (M // tm, N // tn), [((M, N), F32, (tm, tn), lambda i, j: (i, j))], chunks, extras, epi, after)


def _ffn_dact(name, dhb, wd, g, u, tm=1024, after=None):
    M, K = dhb.shape
    nb, _, w = g.shape
    tm = _tile(M, tm)
    tr = _tile(tm, 512)
    tokens = [] if after is None else [after]

    def body(dh_ref, wd_ref, g_ref, u_ref, *rest):
        dg_ref, du_ref = rest[-2:]
        pieces = [pl.ds(r * tr, tr) for r in range(tm // tr)]
        dacts = [lax.dot_general(dh_ref[rows, :], wd_ref[...], NT, preferred_element_type=F32) for rows in pieces]
        for rows, dact in zip(pieces, dacts):
            dg, du = _swiglu_bwd(dact, g_ref[rows, :], u_ref[rows, :])
            dg_ref[rows, :] = dg.astype(BF16)
            du_ref[rows, :] = du.astype(BF16)

    hid = pl.BlockSpec((None, tm, w), lambda i, j: (j, i, 0))
    return pl.pallas_call(
        body, name=name, grid=(M // tm, nb),
        in_specs=[pl.BlockSpec((tm, K), lambda i, j: (i, 0)), pl.BlockSpec((w, K), lambda i, j: (j, 0)), hid, hid]
        + [pl.BlockSpec(t.shape, lambda i, j: (0, 0)) for t in tokens],
        out_specs=[hid, hid], out_shape=[jax.ShapeDtypeStruct((nb, M, w), BF16)] * 2,
        compiler_params=_params(dimension_semantics=("parallel", "parallel")),
    )(dhb, wd, g, u, *tokens)


def _ffn_dw(name, a, b, scale=1.0, tn=1024, after=None):
    nb, T, w = a.shape
    N = b.shape[1]
    tn = _tile(N, tn)
    epi = None if scale == 1.0 else (lambda t: (t * scale,))
    return _gemm(name, a, ((None, T, w), lambda i, j: (i, 0, 0)), b, ((T, tn), lambda i, j: (0, j)), TN,
                 (nb, N // tn), [((nb * w, N), BF16, (w, tn), lambda i, j: (i, j))], epilogue=epi, after=after)


def _ffn_gate(name, hn, wgt, nb, tm=1024):
    M, K = hn.shape
    w = wgt.shape[0] // nb
    tm = _tile(M, tm)
    return _gemm(name, hn, ((tm, K), lambda i, j: (i, 0)), wgt, ((w, K), lambda i, j: (j, 0)), NT,
                 (M // tm, nb), [((nb, M, w), BF16, (None, tm, w), lambda i, j: (j, i, 0))])


def _ffn_up_act(name, hn, wut, g, tm=1024):
    M, K = hn.shape
    nb, _, w = g.shape
    tm = _tile(M, tm)
    hid = ((None, tm, w), lambda i, j: (j, i, 0))

    def epilogue(u, gate):
        gate = gate.astype(F32)
        return u, gate * jax.nn.sigmoid(gate) * u

    return _gemm(name, hn, ((tm, K), lambda i, j: (i, 0)), wut, ((w, K), lambda i, j: (j, 0)), NT,
                 (M // tm, nb), [((nb, M, w), BF16) + hid] * 2, extras=[(g,) + hid], epilogue=epilogue)


def _ffn_up(name, h, gain, wgt, wut, nb, tm=1024):
    M, K = h.shape
    w = wgt.shape[0] // nb
    tm = _tile(M, tm)

    def body(h_ref, gain_ref, g_ref, u_ref, hn_ref, og, ou, oa, a_ref):
        @pl.when(pl.program_id(1) == 0)
        def _():
            hn = _rms(h_ref[...], gain_ref[...]).astype(BF16)
            a_ref[...] = hn
            hn_ref[...] = hn

        a = a_ref[...]
        g = lax.dot_general(a, g_ref[...], NT, preferred_element_type=F32)
        u = lax.dot_general(a, u_ref[...], NT, preferred_element_type=F32)
        og[...] = g.astype(BF16)
        ou[...] = u.astype(BF16)
        oa[...] = (g * jax.nn.sigmoid(g) * u).astype(BF16)

    rows = pl.BlockSpec((tm, K), lambda i, j: (i, 0))
    wspec = pl.BlockSpec((w, K), lambda i, j: (j, 0))
    ospec = pl.BlockSpec((None, tm, w), lambda i, j: (j, i, 0))
    return pl.pallas_call(
        body, name=name, grid=(M // tm, nb),
        in_specs=[rows, pl.BlockSpec((1, K), lambda i, j: (0, 0)), wspec, wspec],
        out_specs=[rows] + [ospec] * 3,
        out_shape=[jax.ShapeDtypeStruct((M, K), BF16)] + [jax.ShapeDtypeStruct((nb, M, w), BF16)] * 3,
        scratch_shapes=[pltpu.VMEM((tm, K), BF16)],
        compiler_params=_params(dimension_semantics=("parallel", "arbitrary")),
    )(h, gain, wgt, wut)


def _swiglu_bwd(dact, g, u):
    g = g.astype(F32)
    u = u.astype(F32)
    sig = jax.nn.sigmoid(g)
    silu = g * sig
    d = FFN_RES * dact
    return d * u * (sig * (1.0 + g * (1.0 - sig))), d * silu


def _rms_fwd(name, x, gain, tr=512, after=None):
    R, D = x.shape
    tr = _tile(R, tr)

    def body(x_ref, g_ref, *rest):
        xv = x_ref[...]
        y = xv * lax.rsqrt(jnp.mean(xv * xv, axis=-1, keepdims=True) + EPS)
        rest[-1][...] = (y * g_ref[...]).astype(BF16)

    tokens = [] if after is None else [after]
    return pl.pallas_call(
        body, name=name, grid=(R // tr,),
        in_specs=[pl.BlockSpec((tr, D), lambda i: (i, 0)), pl.BlockSpec((1, D), lambda i: (0, 0))]
        + [pl.BlockSpec(t.shape, lambda i: (0, 0)) for t in tokens],
        out_specs=pl.BlockSpec((tr, D), lambda i: (i, 0)), out_shape=jax.ShapeDtypeStruct((R, D), BF16),
        compiler_params=_params(dimension_semantics=("parallel",)),
    )(x, gain, *tokens)


def _rms_bwd_math(xv, gain, dy):
    rstd = lax.rsqrt(jnp.mean(xv * xv, axis=-1, keepdims=True) + EPS)
    xhat = xv * rstd
    dxh = dy * gain
    dx = rstd * (dxh - xhat * jnp.mean(dxh * xhat, axis=-1, keepdims=True))
    return dx, jnp.sum(dy * xhat, axis=0, keepdims=True)


def _rms_bwd(name, x, gain, dy, skip=None, tr=256):
    R, D = x.shape
    tr = _tile(R, tr)
    has_skip = skip is not None

    def body(*refs):
        x_ref, g_ref, dy_ref = refs[:3]
        dx_ref, dxb_ref, dg_ref = refs[-3:]
        dx, dg = _rms_bwd_math(x_ref[...], g_ref[...], dy_ref[...].astype(F32))
        if has_skip:
            dx = dx + refs[3][...]
        dx_ref[...] = dx
        dxb_ref[...] = dx.astype(BF16)

        @pl.when(pl.program_id(0) == 0)
        def _():
            dg_ref[...] = dg

        @pl.when(pl.program_id(0) > 0)
        def _():
            dg_ref[...] += dg

    row = pl.BlockSpec((tr, D), lambda i: (i, 0))
    vec = pl.BlockSpec((1, D), lambda i: (0, 0))
    return pl.pallas_call(
        body, name=name, grid=(R // tr,),
        in_specs=[row, vec, row] + ([row] if has_skip else []),
        out_specs=[row, row, vec],
        out_shape=[jax.ShapeDtypeStruct((R, D), F32), jax.ShapeDtypeStruct((R, D), BF16),
                   jax.ShapeDtypeStruct((1, D), F32)],
        compiler_params=_params(dimension_semantics=("arbitrary",)),
    )(*([x, gain, dy] + ([skip] if has_skip else [])))


def _loss_and_grad(name, h, gain, target, tr=256):
    R, D = h.shape
    tr = _tile(R, tr)

    def body(h_ref, g_ref, t_ref, loss_ref, dh_ref, dhb_ref, dg_ref):
        hv, gain_v = h_ref[...], g_ref[...]
        y = (hv * lax.rsqrt(jnp.mean(hv * hv, axis=-1, keepdims=True) + EPS)) * gain_v
        err = y - t_ref[...]
        part = jnp.full((8, 128), 0.5 * jnp.sum(jnp.mean(err * err, axis=-1, keepdims=True)), F32)
        dh, dg = _rms_bwd_math(hv, gain_v, err * (1.0 / D))
        dh_ref[...] = dh
        dhb_ref[...] = dh.astype(BF16)

        @pl.when(pl.program_id(0) == 0)
        def _():
            dg_ref[...] = dg
            loss_ref[...] = part

        @pl.when(pl.program_id(0) > 0)
        def _():
            dg_ref[...] += dg
            loss_ref[...] += part

    row = pl.BlockSpec((tr, D), lambda i: (i, 0))
    vec = pl.BlockSpec((1, D), lambda i: (0, 0))
    return pl.pallas_call(
        body, name=name, grid=(R // tr,), in_specs=[row, vec, row],
        out_specs=[pl.BlockSpec((8, 128), lambda i: (0, 0)), row, row, vec],
        out_shape=[jax.ShapeDtypeStruct((8, 128), F32), jax.ShapeDtypeStruct((R, D), F32),
                   jax.ShapeDtypeStruct((R, D), BF16), jax.ShapeDtypeStruct((1, D), F32)],
        compiler_params=_params(dimension_semantics=("arbitrary",)),
    )(h, gain, target)


def _bias_tile(name, rel):
    width = KBAND + QBLK
    sat = rel[:, 2 * REL_CLIP:]
    n_left = PAD - REL_CLIP + 1
    row0 = jnp.concatenate([jnp.broadcast_to(sat, (ATTN_HEADS, n_left)), rel[:, :2 * REL_CLIP][:, ::-1],
                            jnp.broadcast_to(sat, (ATTN_HEADS, width - n_left - 2 * REL_CLIP))], axis=1)

    def body(e_ref, o_ref):
        rows = pltpu.roll(jnp.broadcast_to(e_ref[...], (QBLK, width)), 0, 1, stride=1, stride_axis=0)
        i = lax.broadcasted_iota(jnp.int32, (QBLK, KBAND), 0) // CHUNK
        j = lax.broadcasted_iota(jnp.int32, (QBLK, KBAND), 1) // CHUNK
        o_ref[...] = jnp.where((j >= i) & (j <= i + LEFT_CHUNKS), rows[:, :KBAND], NEG_INF)

    return pl.pallas_call(
        body, name=name, grid=(ATTN_HEADS,),
        in_specs=[pl.BlockSpec((None, 1, width), lambda h: (h, 0, 0))],
        out_specs=pl.BlockSpec((None, QBLK, KBAND), lambda h: (h, 0, 0)),
        out_shape=jax.ShapeDtypeStruct((ATTN_HEADS, QBLK, KBAND), F32),
        compiler_params=_params(dimension_semantics=("parallel",)),
    )(row0.reshape(ATTN_HEADS, 1, width))


ATTN_SCALE = HEAD_DIM ** -0.5


def _stack_heads(x, first):
    zero = jnp.zeros_like(x)
    return jnp.concatenate([jnp.where(first, x, zero), jnp.where(first, zero, x)], axis=0)


def _band_softmax(q_half_scaled, kb, bias, left_mask):
    s = lax.dot_general(q_half_scaled, kb, NT, preferred_element_type=F32) + bias + left_mask
    e = jnp.exp(s - jnp.max(s, axis=-1, keepdims=True))
    return e * (1.0 / jnp.sum(e, axis=-1, keepdims=True))


def _left_mask(qb):
    kpos = qb * QBLK - PAD + lax.broadcasted_iota(jnp.int32, (1, KBAND), 1)
    return jnp.where(kpos >= 0, 0.0, NEG_INF).astype(F32)


def _fill_padded(dst, src, S):
    dst[pl.ds(0, PAD), :] = jnp.zeros((PAD, dst.shape[1]), dst.dtype)
    dst[pl.ds(PAD, S), :] = src[...].astype(dst.dtype)


def _attn_fwd(name, z, bias):
    B, S, _ = z.shape
    nh2 = ATTN_HEADS // 2

    def body(q_ref, k_ref, v_ref, b_ref, o_ref, kp, vp):
        qb = pl.program_id(2)

        @pl.when(qb == 0)
        def _():
            _fill_padded(kp, k_ref, S)
            _fill_padded(vp, v_ref, S)

        start = pl.multiple_of(qb * QBLK, QBLK)
        kb, vb = kp[pl.ds(start, KBAND), :], vp[pl.ds(start, KBAND), :]
        q = (q_ref[...] * ATTN_SCALE).astype(BF16)
        first = lax.broadcasted_iota(jnp.int32, (QBLK, 2 * HEAD_DIM), 1) < HEAD_DIM
        left = _left_mask(qb)
        p = _band_softmax(_stack_heads(q, first), kb, b_ref[...].reshape(2 * QBLK, KBAND), left)
        o = jnp.dot(p.astype(BF16), vb, preferred_element_type=F32)
        o_ref[...] = jnp.where(first, o[:QBLK], o[QBLK:]).astype(BF16)

    return pl.pallas_call(
        body, name=name, grid=(B, nh2, S // QBLK),
        in_specs=[pl.BlockSpec((None, QBLK, 128), lambda b, h, i: (b, i, h)),
                  pl.BlockSpec((None, S, 128), lambda b, h, i: (b, 0, nh2 + h)),
                  pl.BlockSpec((None, S, 128), lambda b, h, i: (b, 0, 2 * nh2 + h)),
                  pl.BlockSpec((2, QBLK, KBAND), lambda b, h, i: (h, 0, 0))],
        out_specs=pl.BlockSpec((None, QBLK, 128), lambda b, h, i: (b, i, h)),
        out_shape=jax.ShapeDtypeStruct((B, S, D_ATTN + D_POOL), BF16),
        scratch_shapes=[pltpu.VMEM((PAD + S, 128), BF16), pltpu.VMEM((PAD + S, 128), BF16)],
        compiler_params=_params(dimension_semantics=("parallel", "parallel", "arbitrary")),
    )(z, z, z, bias)


def _attn_bwd(name, z, bias, dcat):
    B, S, _ = z.shape
    nh2 = ATTN_HEADS // 2
    nqb = S // QBLK

    def body(q_ref, k_ref, v_ref, b_ref, do_ref, dq_ref, dk_ref, dv_ref, db_ref, kp, vp, dka, dva):
        b, qb = pl.program_id(1), pl.program_id(2)

        @pl.when(qb == 0)
        def _():
            _fill_padded(kp, k_ref, S)
            _fill_padded(vp, v_ref, S)
            dka[...] = jnp.zeros_like(dka)
            dva[...] = jnp.zeros_like(dva)

        @pl.when((qb == 0) & (b == 0))
        def _():
            db_ref[...] = jnp.zeros_like(db_ref)

        start = pl.multiple_of(qb * QBLK, QBLK)
        band = pl.ds(start, KBAND)
        kb, vb = kp[band, :], vp[band, :]
        q = (q_ref[...] * ATTN_SCALE).astype(BF16)
        do = do_ref[...]
        first = lax.broadcasted_iota(jnp.int32, (QBLK, 2 * HEAD_DIM), 1) < HEAD_DIM
        left = _left_mask(qb)
        q2, do2 = _stack_heads(q, first), _stack_heads(do, first)
        p = _band_softmax(q2, kb, b_ref[...].reshape(2 * QBLK, KBAND), left)
        dp = lax.dot_general(do2, vb, NT, preferred_element_type=F32)
        ds = p * (dp - jnp.sum(p * dp, axis=-1, keepdims=True))
        db_ref[...] += ds.reshape(2, QBLK, KBAND)
        dsb = ds.astype(BF16)
        dq = jnp.dot(dsb, kb, preferred_element_type=F32)
        dq_ref[...] = (jnp.where(first, dq[:QBLK], dq[QBLK:]) * ATTN_SCALE).astype(BF16)
        dka[band, :] += lax.dot_general(dsb, q2, TN, preferred_element_type=F32)
        dva[band, :] += lax.dot_general(p.astype(BF16), do2, TN, preferred_element_type=F32)

        @pl.when(qb == nqb - 1)
        def _():
            dk_ref[...] = dka[pl.ds(PAD, S), :].astype(BF16)
            dv_ref[...] = dva[pl.ds(PAD, S), :].astype(BF16)

    qspec = pl.BlockSpec((None, QBLK, 128), lambda h, b, i: (b, i, h))
    kvout = pl.BlockSpec((None, S, 128), lambda h, b, i: (b, 0, h))
    bspec = pl.BlockSpec((2, QBLK, KBAND), lambda h, b, i: (h, 0, 0))
    act = jax.ShapeDtypeStruct((B, S, D_ATTN), BF16)
    return pl.pallas_call(
        body, name=name, grid=(nh2, B, nqb),
        in_specs=[qspec,
                  pl.BlockSpec((None, S, 128), lambda h, b, i: (b, 0, nh2 + h)),
                  pl.BlockSpec((None, S, 128), lambda h, b, i: (b, 0, 2 * nh2 + h)),
                  bspec, qspec],
        out_specs=[qspec, kvout, kvout, bspec],
        out_shape=[act, act, act, jax.ShapeDtypeStruct((ATTN_HEADS, QBLK, KBAND), F32)],
        scratch_shapes=[pltpu.VMEM((PAD + S, 128), BF16), pltpu.VMEM((PAD + S, 128), BF16),
                        pltpu.VMEM((PAD + S, 128), F32), pltpu.VMEM((PAD + S, 128), F32)],
        compiler_params=_params(dimension_semantics=("arbitrary", "arbitrary", "arbitrary")),
    )(z, z, z, bias, dcat)


def _bias_grad(name, dbias):
    width = KBAND + QBLK

    def body(d_ref, o_ref):
        acc = jnp.zeros((1, width), F32)
        for i in range(QBLK):
            row = jnp.concatenate([d_ref[pl.ds(i, 1), :], jnp.zeros((1, QBLK), F32)], axis=1)
            shift = QBLK - 1 - i
            acc = acc + (pltpu.roll(row, shift, 1) if shift else row)
        o_ref[...] = acc

    return pl.pallas_call(
        body, name=name, grid=(ATTN_HEADS,),
        in_specs=[pl.BlockSpec((None, QBLK, KBAND), lambda h: (h, 0, 0))],
        out_specs=pl.BlockSpec((None, 1, width), lambda h: (h, 0, 0)),
        out_shape=jax.ShapeDtypeStruct((ATTN_HEADS, 1, width), F32),
        compiler_params=_params(dimension_semantics=("parallel",)),
    )(dbias)


def _rel_grad_from_diagonals(diag):
    top = PAD + QBLK - 1 - REL_CLIP
    sat = jnp.sum(diag[:, :top + 1], axis=1, keepdims=True)
    mid = diag[:, top + 1:top + 2 * REL_CLIP][:, ::-1]
    return jnp.concatenate([jnp.zeros_like(sat), mid, sat], axis=1)


def _shift_rows(x, k, forward):
    S = x.shape[0]
    t = lax.broadcasted_iota(jnp.int32, x.shape, 0)
    if forward:
        return jnp.where(t < S - k, pltpu.roll(x, S - k, 0), 0.0)
    return jnp.where(t >= k, pltpu.roll(x, k, 0), 0.0)


def _window_sum(x, g, forward):
    s = x + _shift_rows(x, 1, forward)
    out = s
    for n, k in enumerate((2, 4, 8)):
        s = s + _shift_rows(s, k, forward)
        out = jnp.where(g > n, s, out)
    return out


def _pool_count(S, g):
    t = lax.broadcasted_iota(jnp.int32, (S, 1), 0)
    w = jnp.left_shift(2, g)
    return jnp.minimum(t + 1, w).astype(F32)


def _pool_fwd(name, z, wp, pscale, mixed):
    B, S, _ = z.shape
    c0 = 3 * D_ATTN // POOL_GROUP
    y0 = D_ATTN // POOL_GROUP

    def body(u_ref, w_ref, s_ref, mixed_ref, d_ref, y_ref):
        g = pl.program_id(1)
        u = u_ref[...]
        d = (_window_sum(u, g, False) / _pool_count(S, g) - u).astype(BF16)
        d_ref[...] = d
        y_ref[...] = (jnp.dot(d, w_ref[...], preferred_element_type=F32) * s_ref[...]).astype(BF16)

    return pl.pallas_call(
        body, name=name, grid=(B, len(POOL_WINDOWS)),
        in_specs=[pl.BlockSpec((None, S, POOL_GROUP), lambda b, g: (b, 0, c0 + g)),
                  pl.BlockSpec((None, POOL_GROUP, POOL_GROUP), lambda b, g: (g, 0, 0)),
                  pl.BlockSpec((1, POOL_GROUP), lambda b, g: (0, g)),
                  pl.BlockSpec(memory_space=pl.ANY)],
        out_specs=[pl.BlockSpec((None, S, POOL_GROUP), lambda b, g: (b, 0, g)),
                   pl.BlockSpec((None, S, POOL_GROUP), lambda b, g: (b, 0, y0 + g))],
        out_shape=[jax.ShapeDtypeStruct((B, S, D_POOL), BF16), jax.ShapeDtypeStruct(mixed.shape, BF16)],
        input_output_aliases={3: 1},
        compiler_params=_params(dimension_semantics=("parallel", "parallel")),
    )(z, wp, pscale, mixed)


def _pool_bwd(name, d, wp, pscale, dcat):
    B, S, _ = d.shape
    c0 = D_ATTN // POOL_GROUP

    def body(d_ref, w_ref, s_ref, dy_ref, du_ref, dw_ref, dsc_ref):
        g, b = pl.program_id(0), pl.program_id(1)
        dv = d_ref[...]
        dy = dy_ref[...].astype(F32)
        w = w_ref[...]
        ypre = jnp.dot(dv, w, preferred_element_type=F32)
        dyp = (dy * s_ref[...]).astype(BF16)
        dd = lax.dot_general(dyp, w, NT, preferred_element_type=F32)
        du_ref[...] = (_window_sum(dd / _pool_count(S, g), g, True) - dd).astype(BF16)
        dw = lax.dot_general(dv, dyp, TN, preferred_element_type=F32)
        dsc = jnp.sum(dy * ypre, axis=0, keepdims=True)

        @pl.when(b == 0)
        def _():
            dw_ref[...] = dw
            dsc_ref[...] = dsc

        @pl.when(b > 0)
        def _():
            dw_ref[...] += dw
            dsc_ref[...] += dsc

    blk = pl.BlockSpec((None, S, POOL_GROUP), lambda g, b: (b, 0, g))
    wspec = pl.BlockSpec((None, POOL_GROUP, POOL_GROUP), lambda g, b: (g, 0, 0))
    sspec = pl.BlockSpec((1, POOL_GROUP), lambda g, b: (0, g))
    return pl.pallas_call(
        body, name=name, grid=(len(POOL_WINDOWS), B),
        in_specs=[blk, wspec, sspec, pl.BlockSpec((None, S, POOL_GROUP), lambda g, b: (b, 0, c0 + g))],
        out_specs=[blk, wspec, sspec],
        out_shape=[jax.ShapeDtypeStruct((B, S, D_POOL), BF16),
                   jax.ShapeDtypeStruct((len(POOL_WINDOWS), POOL_GROUP, POOL_GROUP), F32),
                   jax.ShapeDtypeStruct((1, D_POOL), F32)],
        compiler_params=_params(dimension_semantics=("arbitrary", "arbitrary")),
    )(d, wp, pscale, dcat)


def _cross_softmax(q, k):
    s = lax.dot_general(q, k, NT, preferred_element_type=F32) * (CROSS_DIM ** -0.5)
    e = jnp.exp(s - jnp.max(s, axis=-1, keepdims=True))
    return e * (1.0 / jnp.sum(e, axis=-1, keepdims=True))


def _cross_fwd(name, qc, kv, tq=512):
    B, S, _ = qc.shape
    M = kv.shape[1]
    tq = _tile(S, tq)

    def body(q_ref, k_ref, v_ref, o_ref):
        p = _cross_softmax(q_ref[...], k_ref[...])
        o_ref[...] = jnp.dot(p.astype(BF16), v_ref[...], preferred_element_type=F32).astype(BF16)

    qspec = pl.BlockSpec((None, tq, CROSS_DIM), lambda b, h, i: (b, i, h))
    return pl.pallas_call(
        body, name=name, grid=(B, CROSS_HEADS, S // tq),
        in_specs=[qspec, pl.BlockSpec((None, M, CROSS_DIM), lambda b, h, i: (b, 0, h)),
                  pl.BlockSpec((None, M, CROSS_DIM), lambda b, h, i: (b, 0, CROSS_HEADS + h))],
        out_specs=qspec, out_shape=jax.ShapeDtypeStruct((B, S, D_CROSS), BF16),
        compiler_params=_params(dimension_semantics=("parallel", "parallel", "parallel")),
    )(qc, kv, kv)


def _cross_bwd(name, qc, kv, do, tq=512):
    B, S, _ = qc.shape
    M = kv.shape[1]
    tq = _tile(S, tq)
    nq = S // tq
    scale = CROSS_DIM ** -0.5

    def body(q_ref, k_ref, v_ref, do_ref, dq_ref, dk_ref, dv_ref, dka, dva):
        i = pl.program_id(2)
        q, k, v, dov = q_ref[...], k_ref[...], v_ref[...], do_ref[...]
        p = _cross_softmax(q, k)
        dp = lax.dot_general(dov, v, NT, preferred_element_type=F32)
        ds = ((p * (dp - jnp.sum(p * dp, axis=-1, keepdims=True))) * scale).astype(BF16)
        dq_ref[...] = jnp.dot(ds, k, preferred_element_type=F32).astype(BF16)
        dk = lax.dot_general(ds, q, TN, preferred_element_type=F32)
        dv = lax.dot_general(p.astype(BF16), dov, TN, preferred_element_type=F32)

        @pl.when(i == 0)
        def _():
            dka[...] = dk
            dva[...] = dv

        @pl.when(i > 0)
        def _():
            dka[...] += dk
            dva[...] += dv

        @pl.when(i == nq - 1)
        def _():
            dk_ref[...] = dka[...].astype(BF16)
            dv_ref[...] = dva[...].astype(BF16)

    qspec = pl.BlockSpec((None, tq, CROSS_DIM), lambda b, h, i: (b, i, h))
    kspec = pl.BlockSpec((None, M, CROSS_DIM), lambda b, h, i: (b, 0, h))
    return pl.pallas_call(
        body, name=name, grid=(B, CROSS_HEADS, nq),
        in_specs=[qspec, kspec, pl.BlockSpec((None, M, CROSS_DIM), lambda b, h, i: (b, 0, CROSS_HEADS + h)), qspec],
        out_specs=[qspec, kspec, kspec],
        out_shape=[jax.ShapeDtypeStruct((B, S, D_CROSS), BF16), jax.ShapeDtypeStruct((B, M, D_CROSS), BF16),
                   jax.ShapeDtypeStruct((B, M, D_CROSS), BF16)],
        scratch_shapes=[pltpu.VMEM((M, CROSS_DIM), F32), pltpu.VMEM((M, CROSS_DIM), F32)],
        compiler_params=_params(dimension_semantics=("parallel", "parallel", "arbitrary")),
    )(qc, kv, kv, do)


def _local_step(x, mem, target, small, weights, emit, start_token=None):
    B, S, D = x.shape
    T = B * S
    x2, t2 = x.reshape(T, D), target.reshape(T, D)
    mem2 = mem.reshape(-1, D)
    n_mem = mem.shape[1]
    wts = {}

    hn1 = _rms_fwd("norm_ffn1", x2, small["ffn1_norm"], after=start_token)
    wts.update(weights(0, hn1))
    g1 = _ffn_gate("ffn1_gate", hn1, wts["ffn1_w_gate"], N_DEV)
    wts.update(weights(1, g1))
    u1, a1 = _ffn_up_act("ffn1_up", hn1, wts["ffn1_w_up"], g1)
    wts.update(weights(2, a1))
    h1 = _ffn_out("ffn1_down", a1, wts["ffn1_w_down"], res=x2, scale=FFN_RES)
    wts.update(weights(3, h1))
    z, hn2 = _mm_nn_cols("mix_in", h1, wts["w_in"], F32, norm=small["mix_norm"])
    z = z.reshape(B, S, -1)
    bias = _bias_tile("bias_tile", small["rel_bias"])
    mixed = _attn_fwd("attn_fwd", z, bias)
    d_pool, mixed = _pool_fwd("pool_fwd", z, wts["w_pool"], small["pool_scale"], mixed)
    cat = mixed.reshape(T, -1)
    h2 = _mm_nn("mix_out", cat, wts["w_out"], F32, res=h1)
    memn = _rms_fwd("norm_mem", mem2, small["mem_norm"])
    wts.update(weights(4, h2))
    qc, hn3 = _mm_nn("cross_q", h2, wts["w_cq"], BF16, norm=small["cross_norm"])
    kv = _mm_nn("cross_kv", memn, wts["w_ckv"], BF16)
    o = _cross_fwd("cross_fwd", qc.reshape(B, S, -1), kv.reshape(B, n_mem, -1)).reshape(T, -1)
    h3 = _mm_nn_cols("cross_out", o, wts["w_co"], F32, res=h2)
    wts.update(weights(5, h3))
    hn4, g2, u2, a2 = _ffn_up("ffn2_up", h3, small["ffn2_norm"], wts["ffn2_w_gate"], wts["ffn2_w_up"], N_DEV, tm=512)
    h4 = _ffn_out("ffn2_down", a2, wts["ffn2_w_down"], res=h3, scale=FFN_RES)

    gs = {}
    loss_part, dh4, dh4b, gs["final_norm"] = _loss_and_grad("loss", h4, small["final_norm"], t2)

    def ffn_bwd(tag, dh, dhb, h_in, hn, g, u, a, wg, wu, wd, gain):
        tok = emit({tag + "_w_down": _ffn_dw(tag + "_dwd", a, dhb, scale=FFN_RES)})
        dg, du = _ffn_dact(tag + "_dact", dhb, wd, g, u, after=tok)
        tok = emit({tag + "_w_gate": _ffn_dw(tag + "_dwg", dg, hn)})
        tok = emit({tag + "_w_up": _ffn_dw(tag + "_dwu", du, hn, after=tok)})
        dhn = _ffn_out(tag + "_dhn_g", dg, wg, after=tok)
        dhn = _ffn_out(tag + "_dhn_u", du, wu, res=dhn)
        return _rms_bwd(tag + "_dnorm", h_in, gain, dhn, skip=dh)

    dh3, dh3b, gs["ffn2_norm"] = ffn_bwd("ffn2", dh4, dh4b, h3, hn4, g2, u2, a2, wts["ffn2_w_gate"],
                                         wts["ffn2_w_up"], wts["ffn2_w_down"], small["ffn2_norm"])
    do = _mm_nt_cols("cross_do", dh3b, wts["w_co"], BF16, tn=D_CROSS)
    gw = {"w_co": _mm_tn("cross_dwo", o, dh3b, tm=D_CROSS, col_blocks=N_DEV)}
    dqc, dk, dv = _cross_bwd("cross_bwd", qc.reshape(B, S, -1), kv.reshape(B, n_mem, -1), do.reshape(B, S, -1))
    dqc = dqc.reshape(T, -1)
    dkv = jnp.concatenate([dk, dv], axis=-1).reshape(B * n_mem, -1)
    gw["w_cq"] = _mm_tn("cross_dwq", hn3, dqc, tn=D_CROSS)
    gw["w_ckv"] = _mm_tn("cross_dwkv", memn, dkv)
    tok = emit(gw)
    dhn3 = _mm_nt("cross_dhn", dqc, wts["w_cq"], F32, after=tok)
    dmemn = _mm_nt("cross_dmem", dkv, wts["w_ckv"], F32, tm=512)
    _, _, gs["mem_norm"] = _rms_bwd("mem_dnorm", mem2, small["mem_norm"], dmemn)
    dh2, dh2b, gs["cross_norm"] = _rms_bwd("cross_dnorm", h2, small["cross_norm"], dhn3, skip=dh3)
    dcat = _mm_nt("mix_dcat", dh2b, wts["w_out"], BF16)
    gw = {"w_out": _mm_tn("mix_dwout", cat, dh2b)}
    dcat3 = dcat.reshape(B, S, -1)
    dq, dkk, dvv, dbias = _attn_bwd("attn_bwd", z, bias, dcat3)
    du, gw["w_pool"], gs["pool_scale"] = _pool_bwd("pool_bwd", d_pool, wts["w_pool"], small["pool_scale"], dcat3)
    gs["rel_bias"] = _rel_grad_from_diagonals(_bias_grad("bias_grad", dbias)[:, 0, :])
    dz = jnp.concatenate([dq, dkk, dvv, du], axis=-1).reshape(T, -1)
    gw["w_in"] = _mm_tn("mix_dwin", hn2, dz, col_blocks=N_DEV)
    tok = emit(gw)
    dhn2 = _mm_nt_cols("mix_dhn", dz, wts["w_in"], F32, after=tok)
    dh1, dh1b, gs["mix_norm"] = _rms_bwd("mix_dnorm", h1, small["mix_norm"], dhn2, skip=dh2)
    dx, _, gs["ffn1_norm"] = ffn_bwd("ffn1", dh1, dh1b, x2, hn1, g1, u1, a1, wts["ffn1_w_gate"],
                                     wts["ffn1_w_up"], wts["ffn1_w_down"], small["ffn1_norm"])
    return loss_part, dx.reshape(B, S, D), gs


def _position():
    return lax.axis_index("x"), lax.axis_index("y"), lax.axis_index("c")


def _index(p):
    return 4 * p[0] + 2 * p[1] + p[2]


HBM_SPEC = pl.BlockSpec(memory_space=pltpu.HBM)
SEM_SPEC = pl.BlockSpec(memory_space=pltpu.SEMAPHORE)
ANY_SPEC = pl.BlockSpec(memory_space=pl.ANY)
ORDERED_EFFECT = pltpu.SideEffectType.DATAFLOW_SIDE_EFFECTING


N_COPIES = {"grads": N_DEV - 1, "spread": 4, "relay": 3}
N_SEMS = {"grads": N_DEV, "spread": 5, "relay": 3}


def _copies(pattern, srcs, lands, send, recv):
    x, y, c = _position()
    me, sibling = _index((x, y, c)), (x, y, 1 - c)
    chips = [(1 - x, y), (x, 1 - y), (1 - x, 1 - y)]
    if pattern == "grads":
        targets = [(x ^ (k >> 2), y ^ ((k >> 1) & 1), c ^ (k & 1)) for k in range(1, N_DEV)]
    else:
        targets = [sibling] + [(*chip, c) for chip in chips]
    per, slots, out = N_COPIES[pattern], N_SEMS[pattern], []
    for k in range(per):
        for a in range(len(lands)):
            if pattern == "relay":
                src = dst = lands[a].at[_index((*chips[k], c))]
                to = sibling
            else:
                to = targets[k]
                src = srcs[a].at[_index(to)] if pattern == "grads" else srcs[a]
                dst = lands[a].at[me]
            out.append(pltpu.make_async_remote_copy(src_ref=src, dst_ref=dst, send_sem=send.at[a * slots + k],
                                                    recv_sem=recv.at[a * slots + k], device_id=to, device_id_type=MESH))
    return out


def _own_copies(pattern, srcs, lands, send):
    if pattern == "relay":
        return []
    x, y, c = _position()
    me = _index((x, y, c))
    slots = N_SEMS[pattern]
    return [pltpu.make_async_copy(srcs[a].at[me] if pattern == "grads" else srcs[a], lands[a].at[me],
                                  send.at[a * slots + slots - 1]) for a in range(len(lands))]


def _comm(name, srcs, lands, wait=None, start=None, after=None):
    ns, nl = len(srcs), len(lands)
    na = ns + nl
    arrays = list(srcs) + list(lands)
    n_wait = 2 if wait else 0
    n_start = 2 if start else 0

    def body(*refs):
        ins, lnd = refs[:ns], refs[ns:na]
        if wait:
            for cp in _copies(wait[0], ins, lnd, refs[na], refs[na + 1]):
                cp.wait_send()
                cp.wait_recv()
            for cp in _own_copies(wait[0], ins, lnd, refs[na]):
                cp.wait()
        if start:
            outs = refs[na + n_wait + (after is not None):]
            for cp in _copies(start, ins, lnd, outs[0], outs[1]) + _own_copies(start, ins, lnd, outs[0]):
                cp.start()
            refs[-1][...] = jnp.zeros((8, 128), F32)

    out_shape, out_specs = [], []
    if start:
        sems = pltpu.SemaphoreType.DMA((nl * N_SEMS[start],))
        out_shape += [sems, sems]
        out_specs += [SEM_SPEC, SEM_SPEC]
    out_shape += [pltpu.HBM(a.shape, a.dtype) for a in arrays]
    out_specs += [HBM_SPEC] * na
    if start:
        out_shape.append(jax.ShapeDtypeStruct((8, 128), F32))
        out_specs.append(pl.BlockSpec(memory_space=pltpu.VMEM))
    operands = [pltpu.with_memory_space_constraint(a, pltpu.HBM) for a in arrays]
    operands += list(wait[1:]) if wait else []
    operands += [after] if after is not None else []
    res = pl.pallas_call(
        body, name=name, out_shape=out_shape, out_specs=out_specs,
        in_specs=[HBM_SPEC] * na + [SEM_SPEC] * n_wait + [ANY_SPEC] * (after is not None),
        input_output_aliases={i: n_start + i for i in range(na)},
        compiler_params=pltpu.CompilerParams(has_side_effects=ORDERED_EFFECT),
    )(*operands)
    res = list(res)
    thru = res[n_start:n_start + na]
    return thru[:ns], thru[ns:], (tuple(res[:2]) if start else None), (res[-1] if start else None)


def _adamw_math(w, g, m, v):
    m = ADAM_B1 * m + (1.0 - ADAM_B1) * g
    v = ADAM_B2 * v + (1.0 - ADAM_B2) * (g * g)
    m_hat = m / (1.0 - ADAM_B1 ** ADAM_STEP)
    v_hat = v / (1.0 - ADAM_B2 ** ADAM_STEP)
    delta = -ADAM_LR * (m_hat / (jnp.sqrt(v_hat) + ADAM_EPS) + ADAM_WD * w)
    return delta, m, v


def _adamw(name, parts, w, m, v, tr=128):
    R, C = w.shape
    tr = _tile(R, tr)

    def body(p_ref, w_ref, m_ref, v_ref, g_out, d_out, m_out, v_out):
        g = p_ref[0].astype(F32)
        for d in range(1, N_DEV):
            g = g + p_ref[d].astype(F32)
        g_out[...] = g
        d_out[...], m_out[...], v_out[...] = _adamw_math(w_ref[...], g, m_ref[...], v_ref[...])

    row = pl.BlockSpec((tr, C), lambda i: (i, 0))
    out = jax.ShapeDtypeStruct((R, C), F32)
    return pl.pallas_call(
        body, name=name, grid=(R // tr,),
        in_specs=[pl.BlockSpec((N_DEV, tr, C), lambda i: (0, i, 0)), row, row, row],
        out_specs=[row] * 4, out_shape=[out] * 4,
        compiler_params=_params(dimension_semantics=("parallel",)),
    )(parts, w, m, v)


def _small_allreduce_adamw(name, g, w, m, v):
    R = g.shape[0]

    def body(g_ref, w_ref, m_ref, v_ref, g_out, d_out, m_out, v_out, land, send, recv):
        x, y, c = _position()
        me = _index((x, y, c))
        land[me] = g_ref[...]
        copies = []
        for k in range(1, N_DEV):
            peer = (x ^ (k >> 2), y ^ ((k >> 1) & 1), c ^ (k & 1))
            copies.append(pltpu.make_async_remote_copy(
                src_ref=g_ref, dst_ref=land.at[me], send_sem=send.at[k - 1], recv_sem=recv.at[k - 1],
                device_id=peer, device_id_type=MESH))
        for cp in copies:
            cp.start()
        for cp in copies:
            cp.wait()
        total = land[0]
        for d in range(1, N_DEV):
            total = total + land[d]
        g_out[...] = total
        d_out[...], m_out[...], v_out[...] = _adamw_math(w_ref[...], total, m_ref[...], v_ref[...])

    vm = pl.BlockSpec(memory_space=pltpu.VMEM)
    out = jax.ShapeDtypeStruct((R, 128), F32)
    return pl.pallas_call(
        body, name=name, in_specs=[vm] * 4, out_specs=[vm] * 4, out_shape=[out] * 4,
        scratch_shapes=[pltpu.VMEM((N_DEV, R, 128), F32), pltpu.SemaphoreType.DMA((7,)),
                        pltpu.SemaphoreType.DMA((7,))],
    )(g, w, m, v)


BIG = ("ffn1_w_gate", "ffn1_w_up", "ffn1_w_down", "w_in", "w_pool", "w_out", "w_cq", "w_ckv", "w_co",
       "ffn2_w_gate", "ffn2_w_up", "ffn2_w_down")
SMALL = ("ffn1_norm", "mix_norm", "rel_bias", "pool_scale", "cross_norm", "mem_norm", "ffn2_norm", "final_norm")
ORDER = ("ffn1_norm", "ffn1_w_gate", "ffn1_w_up", "ffn1_w_down", "mix_norm", "w_in", "rel_bias", "w_pool",
         "pool_scale", "w_out", "cross_norm", "mem_norm", "w_cq", "w_ckv", "w_co", "ffn2_norm", "ffn2_w_gate",
         "ffn2_w_up", "ffn2_w_down", "final_norm")
TRANSPOSED = ("ffn1_w_gate", "ffn1_w_up", "ffn2_w_gate", "ffn2_w_up")
ROW_SHARDED = TRANSPOSED + ("ffn1_w_down", "ffn2_w_down", "w_out", "w_cq", "w_ckv")
GATHER_GROUPS = (("ffn1_w_gate",), ("ffn1_w_up",), ("ffn1_w_down",), ("w_in", "w_pool", "w_out"),
                 ("w_cq", "w_ckv", "w_co"), ("ffn2_w_gate", "ffn2_w_up", "ffn2_w_down"))
RELAY_BEFORE_USE = ((0,), (1,), (2,), (3,), (4, 5), ())


def _pack(arrays):
    flat = jnp.concatenate([a.reshape(-1) for a in arrays])
    rows = -(-flat.shape[0] // 1024) * 8
    return jnp.pad(flat, (0, rows * 128 - flat.shape[0])).reshape(rows, 128)


def _unpack(packed, like):
    flat, out, at = packed.reshape(-1), [], 0
    for a in like:
        out.append(flat[at:at + a.size].reshape(a.shape))
        at += a.size
    return out


def _shard2d(a):
    a = a[0]
    return a.reshape(-1, a.shape[-1])


def kernel(x, mem, ffn1_norm, ffn1_w_gate, ffn1_w_up, ffn1_w_down, mix_norm, w_in, rel_bias, w_pool, pool_scale, w_out, cross_norm, mem_norm, w_cq, w_ckv, w_co, ffn2_norm, ffn2_w_gate, ffn2_w_up, ffn2_w_down, final_norm, loss_target, m_ffn1_norm, m_ffn1_w_gate, m_ffn1_w_up, m_ffn1_w_down, m_mix_norm, m_w_in, m_rel_bias, m_w_pool, m_pool_scale, m_w_out, m_cross_norm, m_mem_norm, m_w_cq, m_w_ckv, m_w_co, m_ffn2_norm, m_ffn2_w_gate, m_ffn2_w_up, m_ffn2_w_down, m_final_norm, v_ffn1_norm, v_ffn1_w_gate, v_ffn1_w_up, v_ffn1_w_down, v_mix_norm, v_w_in, v_rel_bias, v_w_pool, v_pool_scale, v_w_out, v_cross_norm, v_mem_norm, v_w_cq, v_w_ckv, v_w_co, v_ffn2_norm, v_ffn2_w_gate, v_ffn2_w_up, v_ffn2_w_down, v_final_norm):
    args = dict(locals())
    def view(n, a):
        return a.transpose(0, 2, 1) if n in TRANSPOSED else a

    w_in_ = {n: view(n, args[n]) for n in ORDER}
    m_in = {n: view(n, args["m_" + n]) for n in ORDER}
    v_in = {n: view(n, args["v_" + n]) for n in ORDER}

    me = 4 * lax.axis_index("x") + 2 * lax.axis_index("y") + lax.axis_index("c")
    n_g, rows = len(POOL_WINDOWS), POOL_GROUP // N_DEV

    def landing(block_shape, dtype):
        return lax.empty((N_DEV,) + tuple(block_shape), dtype)

    gathers, tok = [], None
    for gi, group in enumerate(GATHER_GROUPS):
        shards = [_shard2d(w_in_[n]).astype(BF16) for n in group]
        srcs, lands, sems, tok = _comm("gather_start_%d" % gi, shards, [landing(s.shape, BF16) for s in shards],
                                       start="spread", after=tok)
        gathers.append((srcs, lands, sems))

    def weights(gi, after):
        for ri in RELAY_BEFORE_USE[gi]:
            srcs, lands, sems = gathers[ri]
            _, lands, sems, after = _comm("gather_relay_%d" % ri, srcs, lands, wait=("spread",) + sems,
                                          start="relay", after=after)
            gathers[ri] = (None, lands, sems)
        _, lands, sems = gathers[gi]
        _, lands, _, _ = _comm("gather_finish_%d" % gi, [], lands, wait=("relay",) + sems, after=after)
        out = {}
        for n, full in zip(GATHER_GROUPS[gi], lands):
            if n == "w_pool":
                full = full.reshape(N_DEV, n_g, rows, POOL_GROUP).transpose(1, 0, 2, 3).reshape(n_g, POOL_GROUP, POOL_GROUP)
            out[n] = full.reshape(-1, full.shape[-1]) if n in ROW_SHARDED else full
        return out

    scatters = []

    def emit(gw):
        names = list(gw)
        stacks = []
        for n in names:
            g = gw[n]
            if n == "w_pool":
                g = g.reshape(n_g, N_DEV, rows, POOL_GROUP).transpose(1, 0, 2, 3).astype(BF16)
            stacks.append(g.reshape((N_DEV,) + _shard2d(w_in_[n]).shape))
        lands = [landing(s.shape[1:], s.dtype) for s in stacks]
        srcs, lands, sems, token = _comm("grads_start_%d" % len(scatters), stacks, lands, start="grads")
        scatters.append((names, srcs, lands, sems))
        return token

    small = {n: w_in_[n].reshape(1, -1) for n in SMALL if n != "rel_bias"}
    small["rel_bias"] = rel_bias[0]
    loss_part, grad_x, gs = _local_step(x, mem, loss_target, small, weights, emit, start_token=tok)
    loss = lax.psum(loss_part[0, 0], ("x", "y", "c"))

    grad, delta, new_m, new_v = {}, {}, {}, {}
    after = grad_x
    for si, (names, srcs, lands, sems) in enumerate(scatters):
        _, landed, _, _ = _comm("grads_finish_%d" % si, srcs, lands, wait=("grads",) + sems, after=after)
        for n, parts in zip(names, landed):
            res = _adamw("adamw_" + n, parts, _shard2d(w_in_[n]), _shard2d(m_in[n]), _shard2d(v_in[n]))
            grad[n], delta[n], new_m[n], new_v[n] = [view(n, r.reshape(w_in_[n].shape)) for r in res]
        after = res[0]

    like = [w_in_[n] for n in SMALL]
    gs["rel_bias"] = gs["rel_bias"].reshape(rel_bias.shape)
    res = _small_allreduce_adamw("small_params", _pack([gs[n] for n in SMALL]), _pack(like),
                                 _pack([m_in[n] for n in SMALL]), _pack([v_in[n] for n in SMALL]))
    for d, packed in zip((grad, delta, new_m, new_v), res):
        for n, a in zip(SMALL, _unpack(packed, like)):
            d[n] = a
    return (loss, grad_x, *[grad[n] for n in ORDER], *[delta[n] for n in ORDER],
            *[new_m[n] for n in ORDER], *[new_v[n] for n in ORDER])
```

```python
import functools

import jax
import jax.numpy as jnp
from jax import lax
from jax.experimental import pallas as pl
from jax.experimental.pallas import tpu as pltpu

F32 = jnp.float32
BF16 = jnp.bfloat16

N_DEV = 8
EPS = 1e-6
NEG_INF = -1e30
CHUNK = 64
LEFT_CHUNKS = 8
PAD = LEFT_CHUNKS * CHUNK
QBLK = 4 * CHUNK
KBAND = PAD + QBLK
REL_CLIP = 128
ATTN_HEADS = 16
HEAD_DIM = 64
D_ATTN = ATTN_HEADS * HEAD_DIM
POOL_WINDOWS = (2, 4, 8, 16)
POOL_GROUP = 256
D_POOL = len(POOL_WINDOWS) * POOL_GROUP
CROSS_HEADS = 4
CROSS_DIM = 128
D_CROSS = CROSS_HEADS * CROSS_DIM
FFN_RES = 0.5
ADAM_LR, ADAM_B1, ADAM_B2, ADAM_EPS, ADAM_WD, ADAM_STEP = 0.001, 0.9, 0.999, 1e-08, 0.01, 10

NN = (((1,), (0,)), ((), ()))
NT = (((1,), (1,)), ((), ()))
TN = (((0,), (0,)), ((), ()))
MESH = pl.DeviceIdType.MESH
VMEM_LIMIT = 56 * 1024 * 1024


def _params(**kw):
    return pltpu.CompilerParams(vmem_limit_bytes=VMEM_LIMIT, **kw)


def _bf(v):
    return v if v.dtype == BF16 else v.astype(BF16)


WHOLE = ((Ellipsis,), (Ellipsis,))


def _rms(xv, gain):
    return (xv * lax.rsqrt(jnp.mean(xv * xv, axis=-1, keepdims=True) + EPS)) * gain


def _gemm(name, a, a_spec, b, b_spec, dims, grid, outs, chunks=(WHOLE,), extras=(), epilogue=None, after=None,
          norm=None):
    nex, nout = len(extras), len(outs)
    first_out = 2 + nex + (after is not None) + (norm is not None)

    def body(*refs):
        a_ref, b_ref = refs[:2]
        if norm is not None:
            hn_out, a_ref = refs[first_out + nout], refs[-1]

            @pl.when(pl.program_id(1) == 0)
            def _():
                hn = _rms(refs[0][...], refs[first_out - 1][...]).astype(BF16)
                a_ref[...] = hn
                hn_out[...] = hn

        total = None
        for ia, ib in chunks:
            d = lax.dot_general(_bf(a_ref[ia]), _bf(b_ref[ib]), dims, preferred_element_type=F32)
            total = d if total is None else total + d
        vals = epilogue(total, *[e[...] for e in refs[2:2 + nex]]) if epilogue is not None else (total,)
        for r, v in zip(refs[first_out:first_out + nout], vals):
            r[...] = v.astype(r.dtype)

    operands = [a, b] + [x for x, _, _ in extras]
    in_specs = [pl.BlockSpec(*a_spec), pl.BlockSpec(*b_spec)] + [pl.BlockSpec(blk, m) for _, blk, m in extras]
    if after is not None:
        operands.append(after)
        in_specs.append(pl.BlockSpec(after.shape, lambda i, j: (0, 0)))
    out_specs = [pl.BlockSpec(blk, m) for _, _, blk, m in outs]
    out_shape = [jax.ShapeDtypeStruct(s, d) for s, d, _, _ in outs]
    scratch = []
    if norm is not None:
        operands.append(norm)
        in_specs.append(pl.BlockSpec(norm.shape, lambda i, j: (0, 0)))
        out_specs.append(pl.BlockSpec(*a_spec))
        out_shape.append(jax.ShapeDtypeStruct(a.shape, BF16))
        scratch.append(pltpu.VMEM(a_spec[0], BF16))
    res = pl.pallas_call(
        body, name=name, grid=grid, in_specs=in_specs, out_specs=out_specs, out_shape=out_shape,
        scratch_shapes=scratch,
        compiler_params=_params(dimension_semantics=("parallel", "arbitrary" if norm is not None else "parallel")),
    )(*operands)
    return res[0] if len(res) == 1 else res


def _tile(n, want):
    for t in range(min(n, want), 15, -1):
        if n % t == 0 and t % 16 == 0:
            return t
    return n


def _mm_nn(name, a, b, out_dtype, res=None, tm=1024, tn=1024, norm=None):
    M, K = a.shape
    N = b.shape[1]
    tm, tn = _tile(M, tm), _tile(N, tn)
    extras = [] if res is None else [(res, (tm, tn), lambda i, j: (i, j))]
    epi = None if res is None else (lambda t, r: (r + t,))
    return _gemm(name, a, ((tm, K), lambda i, j: (i, 0)), b, ((K, tn), lambda i, j: (0, j)), NN,
                 (M // tm, N // tn), [((M, N), out_dtype, (tm, tn), lambda i, j: (i, j))], extras=extras, epilogue=epi,
                 norm=norm)


def _mm_nn_cols(name, a, bs, out_dtype, res=None, tm=1024, norm=None):
    M, K = a.shape
    nb, _, w = bs.shape
    tm = _tile(M, tm)
    extras = [] if res is None else [(res, (tm, w), lambda i, j: (i, j))]
    epi = None if res is None else (lambda t, r: (r + t,))
    return _gemm(name, a, ((tm, K), lambda i, j: (i, 0)), bs, ((None, K, w), lambda i, j: (j, 0, 0)), NN,
                 (M // tm, nb), [((M, nb * w), out_dtype, (tm, w), lambda i, j: (i, j))], extras=extras, epilogue=epi,
                 norm=norm)


def _mm_nt(name, a, b, out_dtype, tm=1024, tn=1024, after=None):
    M, K = a.shape
    N = b.shape[0]
    tm, tn = _tile(M, tm), _tile(N, tn)
    return _gemm(name, a, ((tm, K), lambda i, j: (i, 0)), b, ((tn, K), lambda i, j: (j, 0)), NT,
                 (M // tm, N // tn), [((M, N), out_dtype, (tm, tn), lambda i, j: (i, j))], after=after)


def _mm_nt_cols(name, a, bs, out_dtype, tm=1024, tn=512, after=None):
    M = a.shape[0]
    nb, N, w = bs.shape
    tm, tn = _tile(M, tm), _tile(N, tn)
    chunks = [((slice(None), pl.ds(c * w, w)), (c,)) for c in range(nb)]
    return _gemm(name, a, ((tm, nb * w), lambda i, j: (i, 0)), bs, ((nb, tn, w), lambda i, j: (0, j, 0)), NT,
                 (M // tm, N // tn), [((M, N), out_dtype, (tm, tn), lambda i, j: (i, j))], chunks, after=after)


def _mm_tn(name, a, b, tm=1024, tn=1024, col_blocks=None):
    T, Ka = a.shape
    Nb = b.shape[1]
    tm = _tile(Ka, tm)
    if col_blocks is None:
        tn = _tile(Nb, tn)
        out = ((Ka, Nb), BF16, (tm, tn), lambda i, j: (i, j))
    else:
        tn = Nb // col_blocks
        out = ((col_blocks, Ka, tn), BF16, (None, tm, tn), lambda i, j: (j, i, 0))
    return _gemm(name, a, ((T, tm), lambda i, j: (0, i)), b, ((T, tn), lambda i, j: (0, j)), TN,
                 (Ka // tm, Nb // tn), [out])


def _ffn_out(name, a, wt, res=None, scale=1.0, tm=1024, tn=512, after=None):
    nb, M, w = a.shape
    N = wt.shape[1]
    tm, tn = _tile(M, tm), _tile(N, tn)
    chunks = [((c,), (pl.ds(c * w, w),)) for c in range(nb)]
    extras = [] if res is None else [(res, (tm, tn), lambda i, j: (i, j))]
    epi = None if res is None else (lambda t, r: (r + scale * t,))
    return _gemm(name, a, ((nb, tm, w), lambda i, j: (0, i, 0)), wt, ((nb * w, tn), lambda i, j: (0, j)), NN,
                 (M // tm, N // tn), [((M, N), F32, (tm, tn), lambda i, j: (i, j))], chunks, extras, epi, after)


def _ffn_dact(name, dhb, wd, g, u, tm=1024, after=None):
    M, K = dhb.shape
    nb, _, w = g.shape
    tm = _tile(M, tm)
    tr = _tile(tm, 256)
    tokens = [] if after is None else [after]

    def body(dh_ref, wd_ref, g_ref, u_ref, *rest):
        dg_ref, du_ref = rest[-2:]
        pieces = [pl.ds(r * tr, tr) for r in range(tm // tr)]

        def product(rows):
            return lax.dot_general(dh_ref[rows, :], wd_ref[...], NT, preferred_element_type=F32)

        results = []
        dact = product(pieces[0])
        for n, rows in enumerate(pieces):
            ahead = product(pieces[n + 1]) if n + 1 < len(pieces) else None
            results.append(_swiglu_bwd(dact, g_ref[rows, :], u_ref[rows, :]))
            dact = ahead
        for rows, (dg, du) in zip(pieces, results):
            dg_ref[rows, :] = dg.astype(BF16)
            du_ref[rows, :] = du.astype(BF16)

    hid = pl.BlockSpec((None, tm, w), lambda i, j: (j, i, 0))
    return pl.pallas_call(
        body, name=name, grid=(M // tm, nb),
        in_specs=[pl.BlockSpec((tm, K), lambda i, j: (i, 0)), pl.BlockSpec((w, K), lambda i, j: (j, 0)), hid, hid]
        + [pl.BlockSpec(t.shape, lambda i, j: (0, 0)) for t in tokens],
        out_specs=[hid, hid], out_shape=[jax.ShapeDtypeStruct((nb, M, w), BF16)] * 2,
        compiler_params=_params(dimension_semantics=("parallel", "parallel")),
    )(dhb, wd, g, u, *tokens)


def _ffn_dw(name, a, b, scale=1.0, tn=1024, after=None):
    nb, T, w = a.shape
    N = b.shape[1]
    tn = _tile(N, tn)
    epi = None if scale == 1.0 else (lambda t: (t * scale,))
    return _gemm(name, a, ((None, T, w), lambda i, j: (i, 0, 0)), b, ((T, tn), lambda i, j: (0, j)), TN,
                 (nb, N // tn), [((nb * w, N), BF16, (w, tn), lambda i, j: (i, j))], epilogue=epi, after=after)


def _ffn_gate(name, hn, wgt, nb, tm=1024):
    M, K = hn.shape
    w = wgt.shape[0] // nb
    tm = _tile(M, tm)
    return _gemm(name, hn, ((tm, K), lambda i, j: (i, 0)), wgt, ((w, K), lambda i, j: (j, 0)), NT,
                 (M // tm, nb), [((nb, M, w), BF16, (None, tm, w), lambda i, j: (j, i, 0))])


def _ffn_up_act(name, hn, wut, g, tm=1024):
    M, K = hn.shape
    nb, _, w = g.shape
    tm = _tile(M, tm)
    hid = ((None, tm, w), lambda i, j: (j, i, 0))

    def epilogue(u, gate):
        gate = gate.astype(F32)
        return u, gate * jax.nn.sigmoid(gate) * u

    return _gemm(name, hn, ((tm, K), lambda i, j: (i, 0)), wut, ((w, K), lambda i, j: (j, 0)), NT,
                 (M // tm, nb), [((nb, M, w), BF16) + hid] * 2, extras=[(g,) + hid], epilogue=epilogue)


def _ffn_up(name, h, gain, wgt, wut, nb, tm=1024):
    M, K = h.shape
    w = wgt.shape[0] // nb
    tm = _tile(M, tm)

    def body(h_ref, gain_ref, g_ref, u_ref, hn_ref, og, ou, oa, a_ref):
        @pl.when(pl.program_id(1) == 0)
        def _():
            hn = _rms(h_ref[...], gain_ref[...]).astype(BF16)
            a_ref[...] = hn
            hn_ref[...] = hn

        a = a_ref[...]
        g = lax.dot_general(a, g_ref[...], NT, preferred_element_type=F32)
        u = lax.dot_general(a, u_ref[...], NT, preferred_element_type=F32)
        og[...] = g.astype(BF16)
        ou[...] = u.astype(BF16)
        oa[...] = (g * jax.nn.sigmoid(g) * u).astype(BF16)

    rows = pl.BlockSpec((tm, K), lambda i, j: (i, 0))
    wspec = pl.BlockSpec((w, K), lambda i, j: (j, 0))
    ospec = pl.BlockSpec((None, tm, w), lambda i, j: (j, i, 0))
    return pl.pallas_call(
        body, name=name, grid=(M // tm, nb),
        in_specs=[rows, pl.BlockSpec((1, K), lambda i, j: (0, 0)), wspec, wspec],
        out_specs=[rows] + [ospec] * 3,
        out_shape=[jax.ShapeDtypeStruct((M, K), BF16)] + [jax.ShapeDtypeStruct((nb, M, w), BF16)] * 3,
        scratch_shapes=[pltpu.VMEM((tm, K), BF16)],
        compiler_params=_params(dimension_semantics=("parallel", "arbitrary")),
    )(h, gain, wgt, wut)


def _swiglu_bwd(dact, g, u):
    g = g.astype(F32)
    u = u.astype(F32)
    sig = jax.nn.sigmoid(g)
    silu = g * sig
    d = FFN_RES * dact
    return d * u * (sig * (1.0 + g * (1.0 - sig))), d * silu


def _rms_fwd(name, x, gain, tr=512, after=None):
    R, D = x.shape
    tr = _tile(R, tr)

    def body(x_ref, g_ref, *rest):
        xv = x_ref[...]
        y = xv * lax.rsqrt(jnp.mean(xv * xv, axis=-1, keepdims=True) + EPS)
        rest[-1][...] = (y * g_ref[...]).astype(BF16)

    tokens = [] if after is None else [after]
    return pl.pallas_call(
        body, name=name, grid=(R // tr,),
        in_specs=[pl.BlockSpec((tr, D), lambda i: (i, 0)), pl.BlockSpec((1, D), lambda i: (0, 0))]
        + [pl.BlockSpec(t.shape, lambda i: (0, 0)) for t in tokens],
        out_specs=pl.BlockSpec((tr, D), lambda i: (i, 0)), out_shape=jax.ShapeDtypeStruct((R, D), BF16),
        compiler_params=_params(dimension_semantics=("parallel",)),
    )(x, gain, *tokens)


def _rms_bwd_math(xv, gain, dy):
    rstd = lax.rsqrt(jnp.mean(xv * xv, axis=-1, keepdims=True) + EPS)
    xhat = xv * rstd
    dxh = dy * gain
    dx = rstd * (dxh - xhat * jnp.mean(dxh * xhat, axis=-1, keepdims=True))
    return dx, jnp.sum(dy * xhat, axis=0, keepdims=True)


def _rms_bwd(name, x, gain, dy, skip=None, tr=256):
    R, D = x.shape
    tr = _tile(R, tr)
    has_skip = skip is not None

    def body(*refs):
        x_ref, g_ref, dy_ref = refs[:3]
        dx_ref, dxb_ref, dg_ref = refs[-3:]
        dx, dg = _rms_bwd_math(x_ref[...], g_ref[...], dy_ref[...].astype(F32))
        if has_skip:
            dx = dx + refs[3][...]
        dx_ref[...] = dx
        dxb_ref[...] = dx.astype(BF16)

        @pl.when(pl.program_id(0) == 0)
        def _():
            dg_ref[...] = dg

        @pl.when(pl.program_id(0) > 0)
        def _():
            dg_ref[...] += dg

    row = pl.BlockSpec((tr, D), lambda i: (i, 0))
    vec = pl.BlockSpec((1, D), lambda i: (0, 0))
    return pl.pallas_call(
        body, name=name, grid=(R // tr,),
        in_specs=[row, vec, row] + ([row] if has_skip else []),
        out_specs=[row, row, vec],
        out_shape=[jax.ShapeDtypeStruct((R, D), F32), jax.ShapeDtypeStruct((R, D), BF16),
                   jax.ShapeDtypeStruct((1, D), F32)],
        compiler_params=_params(dimension_semantics=("arbitrary",)),
    )(*([x, gain, dy] + ([skip] if has_skip else [])))


def _loss_and_grad(name, h, gain, target, tr=256):
    R, D = h.shape
    tr = _tile(R, tr)

    def body(h_ref, g_ref, t_ref, loss_ref, dh_ref, dhb_ref, dg_ref):
        hv, gain_v = h_ref[...], g_ref[...]
        y = (hv * lax.rsqrt(jnp.mean(hv * hv, axis=-1, keepdims=True) + EPS)) * gain_v
        err = y - t_ref[...]
        part = jnp.full((8, 128), 0.5 * jnp.sum(jnp.mean(err * err, axis=-1, keepdims=True)), F32)
        dh, dg = _rms_bwd_math(hv, gain_v, err * (1.0 / D))
        dh_ref[...] = dh
        dhb_ref[...] = dh.astype(BF16)

        @pl.when(pl.program_id(0) == 0)
        def _():
            dg_ref[...] = dg
            loss_ref[...] = part

        @pl.when(pl.program_id(0) > 0)
        def _():
            dg_ref[...] += dg
            loss_ref[...] += part

    row = pl.BlockSpec((tr, D), lambda i: (i, 0))
    vec = pl.BlockSpec((1, D), lambda i: (0, 0))
    return pl.pallas_call(
        body, name=name, grid=(R // tr,), in_specs=[row, vec, row],
        out_specs=[pl.BlockSpec((8, 128), lambda i: (0, 0)), row, row, vec],
        out_shape=[jax.ShapeDtypeStruct((8, 128), F32), jax.ShapeDtypeStruct((R, D), F32),
                   jax.ShapeDtypeStruct((R, D), BF16), jax.ShapeDtypeStruct((1, D), F32)],
        compiler_params=_params(dimension_semantics=("arbitrary",)),
    )(h, gain, target)


def _bias_tile(name, rel):
    width = KBAND + QBLK
    sat = rel[:, 2 * REL_CLIP:]
    n_left = PAD - REL_CLIP + 1
    row0 = jnp.concatenate([jnp.broadcast_to(sat, (ATTN_HEADS, n_left)), rel[:, :2 * REL_CLIP][:, ::-1],
                            jnp.broadcast_to(sat, (ATTN_HEADS, width - n_left - 2 * REL_CLIP))], axis=1)

    def body(e_ref, o_ref):
        rows = pltpu.roll(jnp.broadcast_to(e_ref[...], (QBLK, width)), 0, 1, stride=1, stride_axis=0)
        i = lax.broadcasted_iota(jnp.int32, (QBLK, KBAND), 0) // CHUNK
        j = lax.broadcasted_iota(jnp.int32, (QBLK, KBAND), 1) // CHUNK
        o_ref[...] = jnp.where((j >= i) & (j <= i + LEFT_CHUNKS), rows[:, :KBAND], NEG_INF)

    return pl.pallas_call(
        body, name=name, grid=(ATTN_HEADS,),
        in_specs=[pl.BlockSpec((None, 1, width), lambda h: (h, 0, 0))],
        out_specs=pl.BlockSpec((None, QBLK, KBAND), lambda h: (h, 0, 0)),
        out_shape=jax.ShapeDtypeStruct((ATTN_HEADS, QBLK, KBAND), F32),
        compiler_params=_params(dimension_semantics=("parallel",)),
    )(row0.reshape(ATTN_HEADS, 1, width))


ATTN_SCALE = HEAD_DIM ** -0.5
ROW_PIECES = 1


def _stack_heads(x, first):
    zero = jnp.zeros_like(x)
    return jnp.concatenate([jnp.where(first, x, zero), jnp.where(first, zero, x)], axis=0)


def _band_softmax(q_half_scaled, kb, bias, left_mask):
    s = lax.dot_general(q_half_scaled, kb, NT, preferred_element_type=F32) + bias + left_mask
    e = jnp.exp(s - jnp.max(s, axis=-1, keepdims=True))
    return e * (1.0 / jnp.sum(e, axis=-1, keepdims=True))


def _left_mask(qb):
    kpos = qb * QBLK - PAD + lax.broadcasted_iota(jnp.int32, (1, KBAND), 1)
    return jnp.where(kpos >= 0, 0.0, NEG_INF).astype(F32)


def _fill_padded(dst, src, S):
    dst[pl.ds(0, PAD), :] = jnp.zeros((PAD, dst.shape[1]), dst.dtype)
    dst[pl.ds(PAD, S), :] = src[...].astype(dst.dtype)


def _attn_fwd(name, z, bias):
    B, S, _ = z.shape
    nh2 = ATTN_HEADS // 2

    def body(q_ref, k_ref, v_ref, b_ref, o_ref, kp, vp):
        qb = pl.program_id(2)

        @pl.when(qb == 0)
        def _():
            _fill_padded(kp, k_ref, S)
            _fill_padded(vp, v_ref, S)

        start = pl.multiple_of(qb * QBLK, QBLK)
        kb, vb = kp[pl.ds(start, KBAND), :], vp[pl.ds(start, KBAND), :]
        q = (q_ref[...] * ATTN_SCALE).astype(BF16)
        first = lax.broadcasted_iota(jnp.int32, (QBLK, 2 * HEAD_DIM), 1) < HEAD_DIM
        left = _left_mask(qb)
        zero = jnp.zeros_like(q)
        qh = [jnp.where(first, q, zero), jnp.where(first, zero, q)]
        rp = QBLK // ROW_PIECES
        chains = [(a, r) for r in range(ROW_PIECES) for a in range(2)]
        ss = [lax.dot_general(qh[a][r * rp:(r + 1) * rp], kb, NT, preferred_element_type=F32) for a, r in chains]
        ps = []
        for (a, r), s in zip(chains, ss):
            s = s + b_ref[a, pl.ds(r * rp, rp), :] + left
            e = jnp.exp(s - jnp.max(s, axis=-1, keepdims=True))
            ps.append((e * (1.0 / jnp.sum(e, axis=-1, keepdims=True))).astype(BF16))
        os_ = [jnp.dot(p, vb, preferred_element_type=F32) for p in ps]
        for r in range(ROW_PIECES):
            o_ref[pl.ds(r * rp, rp), :] = jnp.where(first[:rp], os_[2 * r], os_[2 * r + 1]).astype(BF16)

    return pl.pallas_call(
        body, name=name, grid=(B, nh2, S // QBLK),
        in_specs=[pl.BlockSpec((None, QBLK, 128), lambda b, h, i: (b, i, h)),
                  pl.BlockSpec((None, S, 128), lambda b, h, i: (b, 0, nh2 + h)),
                  pl.BlockSpec((None, S, 128), lambda b, h, i: (b, 0, 2 * nh2 + h)),
                  pl.BlockSpec((2, QBLK, KBAND), lambda b, h, i: (h, 0, 0))],
        out_specs=pl.BlockSpec((None, QBLK, 128), lambda b, h, i: (b, i, h)),
        out_shape=jax.ShapeDtypeStruct((B, S, D_ATTN + D_POOL), BF16),
        scratch_shapes=[pltpu.VMEM((PAD + S, 128), BF16), pltpu.VMEM((PAD + S, 128), BF16)],
        compiler_params=_params(dimension_semantics=("parallel", "parallel", "arbitrary")),
    )(z, z, z, bias)


def _attn_bwd(name, z, bias, dcat):
    B, S, _ = z.shape
    nh2 = ATTN_HEADS // 2
    nqb = S // QBLK

    def body(q_ref, k_ref, v_ref, b_ref, do_ref, dq_ref, dk_ref, dv_ref, db_ref, kp, vp, dka, dva):
        b, qb = pl.program_id(1), pl.program_id(2)

        @pl.when(qb == 0)
        def _():
            _fill_padded(kp, k_ref, S)
            _fill_padded(vp, v_ref, S)
            dka[...] = jnp.zeros_like(dka)
            dva[...] = jnp.zeros_like(dva)

        @pl.when((qb == 0) & (b == 0))
        def _():
            db_ref[...] = jnp.zeros_like(db_ref)

        start = pl.multiple_of(qb * QBLK, QBLK)
        band = pl.ds(start, KBAND)
        kb, vb = kp[band, :], vp[band, :]
        q = (q_ref[...] * ATTN_SCALE).astype(BF16)
        do = do_ref[...]
        first = lax.broadcasted_iota(jnp.int32, (QBLK, 2 * HEAD_DIM), 1) < HEAD_DIM
        left = _left_mask(qb)
        q2, do2 = _stack_heads(q, first), _stack_heads(do, first)
        p = _band_softmax(q2, kb, b_ref[...].reshape(2 * QBLK, KBAND), left)
        dp = lax.dot_general(do2, vb, NT, preferred_element_type=F32)
        ds = p * (dp - jnp.sum(p * dp, axis=-1, keepdims=True))
        db_ref[...] += ds.reshape(2, QBLK, KBAND)
        dsb = ds.astype(BF16)
        dq = jnp.dot(dsb, kb, preferred_element_type=F32)
        dq_ref[...] = (jnp.where(first, dq[:QBLK], dq[QBLK:]) * ATTN_SCALE).astype(BF16)
        dka[band, :] += lax.dot_general(dsb, q2, TN, preferred_element_type=F32)
        dva[band, :] += lax.dot_general(p.astype(BF16), do2, TN, preferred_element_type=F32)

        @pl.when(qb == nqb - 1)
        def _():
            dk_ref[...] = dka[pl.ds(PAD, S), :].astype(BF16)
            dv_ref[...] = dva[pl.ds(PAD, S), :].astype(BF16)

    qspec = pl.BlockSpec((None, QBLK, 128), lambda h, b, i: (b, i, h))
    kvout = pl.BlockSpec((None, S, 128), lambda h, b, i: (b, 0, h))
    bspec = pl.BlockSpec((2, QBLK, KBAND), lambda h, b, i: (h, 0, 0))
    act = jax.ShapeDtypeStruct((B, S, D_ATTN), BF16)
    return pl.pallas_call(
        body, name=name, grid=(nh2, B, nqb),
        in_specs=[qspec,
                  pl.BlockSpec((None, S, 128), lambda h, b, i: (b, 0, nh2 + h)),
                  pl.BlockSpec((None, S, 128), lambda h, b, i: (b, 0, 2 * nh2 + h)),
                  bspec, qspec],
        out_specs=[qspec, kvout, kvout, bspec],
        out_shape=[act, act, act, jax.ShapeDtypeStruct((ATTN_HEADS, QBLK, KBAND), F32)],
        scratch_shapes=[pltpu.VMEM((PAD + S, 128), BF16), pltpu.VMEM((PAD + S, 128), BF16),
                        pltpu.VMEM((PAD + S, 128), F32), pltpu.VMEM((PAD + S, 128), F32)],
        compiler_params=_params(dimension_semantics=("arbitrary", "arbitrary", "arbitrary")),
    )(z, z, z, bias, dcat)


def _bias_grad(name, dbias):
    width = KBAND + QBLK

    def body(d_ref, o_ref):
        acc = jnp.zeros((1, width), F32)
        for i in range(QBLK):
            row = jnp.concatenate([d_ref[pl.ds(i, 1), :], jnp.zeros((1, QBLK), F32)], axis=1)
            shift = QBLK - 1 - i
            acc = acc + (pltpu.roll(row, shift, 1) if shift else row)
        o_ref[...] = acc

    return pl.pallas_call(
        body, name=name, grid=(ATTN_HEADS,),
        in_specs=[pl.BlockSpec((None, QBLK, KBAND), lambda h: (h, 0, 0))],
        out_specs=pl.BlockSpec((None, 1, width), lambda h: (h, 0, 0)),
        out_shape=jax.ShapeDtypeStruct((ATTN_HEADS, 1, width), F32),
        compiler_params=_params(dimension_semantics=("parallel",)),
    )(dbias)


def _rel_grad_from_diagonals(diag):
    top = PAD + QBLK - 1 - REL_CLIP
    sat = jnp.sum(diag[:, :top + 1], axis=1, keepdims=True)
    mid = diag[:, top + 1:top + 2 * REL_CLIP][:, ::-1]
    return jnp.concatenate([jnp.zeros_like(sat), mid, sat], axis=1)


def _shift_rows(x, k, forward):
    S = x.shape[0]
    t = lax.broadcasted_iota(jnp.int32, x.shape, 0)
    if forward:
        return jnp.where(t < S - k, pltpu.roll(x, S - k, 0), 0.0)
    return jnp.where(t >= k, pltpu.roll(x, k, 0), 0.0)


def _window_sum(x, g, forward):
    s = x + _shift_rows(x, 1, forward)
    out = s
    for n, k in enumerate((2, 4, 8)):
        s = s + _shift_rows(s, k, forward)
        out = jnp.where(g > n, s, out)
    return out


def _pool_count(S, g):
    t = lax.broadcasted_iota(jnp.int32, (S, 1), 0)
    w = jnp.left_shift(2, g)
    return jnp.minimum(t + 1, w).astype(F32)


def _pool_fwd(name, z, wp, pscale, mixed):
    B, S, _ = z.shape
    c0 = 3 * D_ATTN // POOL_GROUP
    y0 = D_ATTN // POOL_GROUP

    def body(u_ref, w_ref, s_ref, mixed_ref, d_ref, y_ref):
        g = pl.program_id(1)
        u = u_ref[...]
        d = (_window_sum(u, g, False) / _pool_count(S, g) - u).astype(BF16)
        d_ref[...] = d
        y_ref[...] = (jnp.dot(d, w_ref[...], preferred_element_type=F32) * s_ref[...]).astype(BF16)

    return pl.pallas_call(
        body, name=name, grid=(B, len(POOL_WINDOWS)),
        in_specs=[pl.BlockSpec((None, S, POOL_GROUP), lambda b, g: (b, 0, c0 + g)),
                  pl.BlockSpec((None, POOL_GROUP, POOL_GROUP), lambda b, g: (g, 0, 0)),
                  pl.BlockSpec((1, POOL_GROUP), lambda b, g: (0, g)),
                  pl.BlockSpec(memory_space=pl.ANY)],
        out_specs=[pl.BlockSpec((None, S, POOL_GROUP), lambda b, g: (b, 0, g)),
                   pl.BlockSpec((None, S, POOL_GROUP), lambda b, g: (b, 0, y0 + g))],
        out_shape=[jax.ShapeDtypeStruct((B, S, D_POOL), BF16), jax.ShapeDtypeStruct(mixed.shape, BF16)],
        input_output_aliases={3: 1},
        compiler_params=_params(dimension_semantics=("parallel", "parallel")),
    )(z, wp, pscale, mixed)


def _pool_bwd(name, d, wp, pscale, dcat):
    B, S, _ = d.shape
    c0 = D_ATTN // POOL_GROUP

    def body(d_ref, w_ref, s_ref, dy_ref, du_ref, dw_ref, dsc_ref):
        g, b = pl.program_id(0), pl.program_id(1)
        dv = d_ref[...]
        dy = dy_ref[...].astype(F32)
        w = w_ref[...]
        ypre = jnp.dot(dv, w, preferred_element_type=F32)
        dyp = (dy * s_ref[...]).astype(BF16)
        dd = lax.dot_general(dyp, w, NT, preferred_element_type=F32)
        du_ref[...] = (_window_sum(dd / _pool_count(S, g), g, True) - dd).astype(BF16)
        dw = lax.dot_general(dv, dyp, TN, preferred_element_type=F32)
        dsc = jnp.sum(dy * ypre, axis=0, keepdims=True)

        @pl.when(b == 0)
        def _():
            dw_ref[...] = dw
            dsc_ref[...] = dsc

        @pl.when(b > 0)
        def _():
            dw_ref[...] += dw
            dsc_ref[...] += dsc

    blk = pl.BlockSpec((None, S, POOL_GROUP), lambda g, b: (b, 0, g))
    wspec = pl.BlockSpec((None, POOL_GROUP, POOL_GROUP), lambda g, b: (g, 0, 0))
    sspec = pl.BlockSpec((1, POOL_GROUP), lambda g, b: (0, g))
    return pl.pallas_call(
        body, name=name, grid=(len(POOL_WINDOWS), B),
        in_specs=[blk, wspec, sspec, pl.BlockSpec((None, S, POOL_GROUP), lambda g, b: (b, 0, c0 + g))],
        out_specs=[blk, wspec, sspec],
        out_shape=[jax.ShapeDtypeStruct((B, S, D_POOL), BF16),
                   jax.ShapeDtypeStruct((len(POOL_WINDOWS), POOL_GROUP, POOL_GROUP), F32),
                   jax.ShapeDtypeStruct((1, D_POOL), F32)],
        compiler_params=_params(dimension_semantics=("arbitrary", "arbitrary")),
    )(d, wp, pscale, dcat)


def _cross_softmax(q, k):
    s = lax.dot_general(q, k, NT, preferred_element_type=F32) * (CROSS_DIM ** -0.5)
    e = jnp.exp(s - jnp.max(s, axis=-1, keepdims=True))
    return e * (1.0 / jnp.sum(e, axis=-1, keepdims=True))


def _cross_fwd(name, qc, kv, tq=512):
    B, S, _ = qc.shape
    M = kv.shape[1]
    tq = _tile(S, tq)

    def body(q_ref, k_ref, v_ref, o_ref):
        p = _cross_softmax(q_ref[...], k_ref[...])
        o_ref[...] = jnp.dot(p.astype(BF16), v_ref[...], preferred_element_type=F32).astype(BF16)

    qspec = pl.BlockSpec((None, tq, CROSS_DIM), lambda b, h, i: (b, i, h))
    return pl.pallas_call(
        body, name=name, grid=(B, CROSS_HEADS, S // tq),
        in_specs=[qspec, pl.BlockSpec((None, M, CROSS_DIM), lambda b, h, i: (b, 0, h)),
                  pl.BlockSpec((None, M, CROSS_DIM), lambda b, h, i: (b, 0, CROSS_HEADS + h))],
        out_specs=qspec, out_shape=jax.ShapeDtypeStruct((B, S, D_CROSS), BF16),
        compiler_params=_params(dimension_semantics=("parallel", "parallel", "parallel")),
    )(qc, kv, kv)


def _cross_bwd(name, qc, kv, do, tq=512):
    B, S, _ = qc.shape
    M = kv.shape[1]
    tq = _tile(S, tq)
    nq = S // tq
    scale = CROSS_DIM ** -0.5

    def body(q_ref, k_ref, v_ref, do_ref, dq_ref, dk_ref, dv_ref, dka, dva):
        i = pl.program_id(2)
        q, k, v, dov = q_ref[...], k_ref[...], v_ref[...], do_ref[...]
        p = _cross_softmax(q, k)
        dp = lax.dot_general(dov, v, NT, preferred_element_type=F32)
        ds = ((p * (dp - jnp.sum(p * dp, axis=-1, keepdims=True))) * scale).astype(BF16)
        dq_ref[...] = jnp.dot(ds, k, preferred_element_type=F32).astype(BF16)
        dk = lax.dot_general(ds, q, TN, preferred_element_type=F32)
        dv = lax.dot_general(p.astype(BF16), dov, TN, preferred_element_type=F32)

        @pl.when(i == 0)
        def _():
            dka[...] = dk
            dva[...] = dv

        @pl.when(i > 0)
        def _():
            dka[...] += dk
            dva[...] += dv

        @pl.when(i == nq - 1)
        def _():
            dk_ref[...] = dka[...].astype(BF16)
            dv_ref[...] = dva[...].astype(BF16)

    qspec = pl.BlockSpec((None, tq, CROSS_DIM), lambda b, h, i: (b, i, h))
    kspec = pl.BlockSpec((None, M, CROSS_DIM), lambda b, h, i: (b, 0, h))
    return pl.pallas_call(
        body, name=name, grid=(B, CROSS_HEADS, nq),
        in_specs=[qspec, kspec, pl.BlockSpec((None, M, CROSS_DIM), lambda b, h, i: (b, 0, CROSS_HEADS + h)), qspec],
        out_specs=[qspec, kspec, kspec],
        out_shape=[jax.ShapeDtypeStruct((B, S, D_CROSS), BF16), jax.ShapeDtypeStruct((B, M, D_CROSS), BF16),
                   jax.ShapeDtypeStruct((B, M, D_CROSS), BF16)],
        scratch_shapes=[pltpu.VMEM((M, CROSS_DIM), F32), pltpu.VMEM((M, CROSS_DIM), F32)],
        compiler_params=_params(dimension_semantics=("parallel", "parallel", "arbitrary")),
    )(qc, kv, kv, do)


def _local_step(x, mem, target, small, weights, emit, start_token=None):
    B, S, D = x.shape
    T = B * S
    x2, t2 = x.reshape(T, D), target.reshape(T, D)
    mem2 = mem.reshape(-1, D)
    n_mem = mem.shape[1]
    wts = {}

    hn1 = _rms_fwd("norm_ffn1", x2, small["ffn1_norm"], after=start_token)
    wts.update(weights(0, hn1))
    g1 = _ffn_gate("ffn1_gate", hn1, wts["ffn1_w_gate"], N_DEV)
    wts.update(weights(1, g1))
    u1, a1 = _ffn_up_act("ffn1_up", hn1, wts["ffn1_w_up"], g1)
    wts.update(weights(2, a1))
    h1 = _ffn_out("ffn1_down", a1, wts["ffn1_w_down"], res=x2, scale=FFN_RES)
    wts.update(weights(3, h1))
    z, hn2 = _mm_nn_cols("mix_in", h1, wts["w_in"], F32, norm=small["mix_norm"])
    z = z.reshape(B, S, -1)
    bias = _bias_tile("bias_tile", small["rel_bias"])
    mixed = _attn_fwd("attn_fwd", z, bias)
    d_pool, mixed = _pool_fwd("pool_fwd", z, wts["w_pool"], small["pool_scale"], mixed)
    cat = mixed.reshape(T, -1)
    h2 = _mm_nn("mix_out", cat, wts["w_out"], F32, res=h1)
    memn = _rms_fwd("norm_mem", mem2, small["mem_norm"])
    wts.update(weights(4, h2))
    qc, hn3 = _mm_nn("cross_q", h2, wts["w_cq"], BF16, norm=small["cross_norm"])
    kv = _mm_nn("cross_kv", memn, wts["w_ckv"], BF16)
    o = _cross_fwd("cross_fwd", qc.reshape(B, S, -1), kv.reshape(B, n_mem, -1)).reshape(T, -1)
    h3 = _mm_nn_cols("cross_out", o, wts["w_co"], F32, res=h2)
    wts.update(weights(5, h3))
    hn4, g2, u2, a2 = _ffn_up("ffn2_up", h3, small["ffn2_norm"], wts["ffn2_w_gate"], wts["ffn2_w_up"], N_DEV)
    h4 = _ffn_out("ffn2_down", a2, wts["ffn2_w_down"], res=h3, scale=FFN_RES)

    gs = {}
    loss_part, dh4, dh4b, gs["final_norm"] = _loss_and_grad("loss", h4, small["final_norm"], t2)

    def ffn_bwd(tag, dh, dhb, h_in, hn, g, u, a, wg, wu, wd, gain):
        tok = emit({tag + "_w_down": _ffn_dw(tag + "_dwd", a, dhb, scale=FFN_RES)})
        dg, du = _ffn_dact(tag + "_dact", dhb, wd, g, u, after=tok)
        tok = emit({tag + "_w_gate": _ffn_dw(tag + "_dwg", dg, hn)})
        tok = emit({tag + "_w_up": _ffn_dw(tag + "_dwu", du, hn, after=tok)})
        dhn = _ffn_out(tag + "_dhn_g", dg, wg, after=tok)
        dhn = _ffn_out(tag + "_dhn_u", du, wu, res=dhn)
        return _rms_bwd(tag + "_dnorm", h_in, gain, dhn, skip=dh)

    dh3, dh3b, gs["ffn2_norm"] = ffn_bwd("ffn2", dh4, dh4b, h3, hn4, g2, u2, a2, wts["ffn2_w_gate"],
                                         wts["ffn2_w_up"], wts["ffn2_w_down"], small["ffn2_norm"])
    do = _mm_nt_cols("cross_do", dh3b, wts["w_co"], BF16, tn=D_CROSS)
    gw = {"w_co": _mm_tn("cross_dwo", o, dh3b, tm=D_CROSS, col_blocks=N_DEV)}
    dqc, dk, dv = _cross_bwd("cross_bwd", qc.reshape(B, S, -1), kv.reshape(B, n_mem, -1), do.reshape(B, S, -1))
    dqc = dqc.reshape(T, -1)
    dkv = jnp.concatenate([dk, dv], axis=-1).reshape(B * n_mem, -1)
    gw["w_cq"] = _mm_tn("cross_dwq", hn3, dqc, tn=D_CROSS)
    gw["w_ckv"] = _mm_tn("cross_dwkv", memn, dkv)
    tok = emit(gw)
    dhn3 = _mm_nt("cross_dhn", dqc, wts["w_cq"], F32, after=tok)
    dmemn = _mm_nt("cross_dmem", dkv, wts["w_ckv"], F32, tm=512)
    _, _, gs["mem_norm"] = _rms_bwd("mem_dnorm", mem2, small["mem_norm"], dmemn)
    dh2, dh2b, gs["cross_norm"] = _rms_bwd("cross_dnorm", h2, small["cross_norm"], dhn3, skip=dh3)
    dcat = _mm_nt("mix_dcat", dh2b, wts["w_out"], BF16)
    gw = {"w_out": _mm_tn("mix_dwout", cat, dh2b)}
    dcat3 = dcat.reshape(B, S, -1)
    dq, dkk, dvv, dbias = _attn_bwd("attn_bwd", z, bias, dcat3)
    du, gw["w_pool"], gs["pool_scale"] = _pool_bwd("pool_bwd", d_pool, wts["w_pool"], small["pool_scale"], dcat3)
    gs["rel_bias"] = _rel_grad_from_diagonals(_bias_grad("bias_grad", dbias)[:, 0, :])
    dz = jnp.concatenate([dq, dkk, dvv, du], axis=-1).reshape(T, -1)
    gw["w_in"] = _mm_tn("mix_dwin", hn2, dz, col_blocks=N_DEV)
    tok = emit(gw)
    dhn2 = _mm_nt_cols("mix_dhn", dz, wts["w_in"], F32, after=tok)
    dh1, dh1b, gs["mix_norm"] = _rms_bwd("mix_dnorm", h1, small["mix_norm"], dhn2, skip=dh2)
    dx, _, gs["ffn1_norm"] = ffn_bwd("ffn1", dh1, dh1b, x2, hn1, g1, u1, a1, wts["ffn1_w_gate"],
                                     wts["ffn1_w_up"], wts["ffn1_w_down"], small["ffn1_norm"])
    return loss_part, dx.reshape(B, S, D), gs


def _position():
    return lax.axis_index("x"), lax.axis_index("y"), lax.axis_index("c")


def _index(p):
    return 4 * p[0] + 2 * p[1] + p[2]


HBM_SPEC = pl.BlockSpec(memory_space=pltpu.HBM)
SEM_SPEC = pl.BlockSpec(memory_space=pltpu.SEMAPHORE)
ANY_SPEC = pl.BlockSpec(memory_space=pl.ANY)
ORDERED_EFFECT = pltpu.SideEffectType.DATAFLOW_SIDE_EFFECTING


N_COPIES = {"grads": N_DEV - 1, "spread": 4, "relay": 3}
N_SEMS = {"grads": N_DEV, "spread": 5, "relay": 3}


def _copies(pattern, srcs, lands, send, recv):
    x, y, c = _position()
    me, sibling = _index((x, y, c)), (x, y, 1 - c)
    chips = [(1 - x, y), (x, 1 - y), (1 - x, 1 - y)]
    if pattern == "grads":
        targets = [(x ^ (k >> 2), y ^ ((k >> 1) & 1), c ^ (k & 1)) for k in range(1, N_DEV)]
    else:
        targets = [sibling] + [(*chip, c) for chip in chips]
    per, slots, out = N_COPIES[pattern], N_SEMS[pattern], []
    for k in range(per):
        for a in range(len(lands)):
            if pattern == "relay":
                src = dst = lands[a].at[_index((*chips[k], c))]
                to = sibling
            else:
                to = targets[k]
                src = srcs[a].at[_index(to)] if pattern == "grads" else srcs[a]
                dst = lands[a].at[me]
            out.append(pltpu.make_async_remote_copy(src_ref=src, dst_ref=dst, send_sem=send.at[a * slots + k],
                                                    recv_sem=recv.at[a * slots + k], device_id=to, device_id_type=MESH))
    return out


def _own_copies(pattern, srcs, lands, send):
    if pattern == "relay":
        return []
    x, y, c = _position()
    me = _index((x, y, c))
    slots = N_SEMS[pattern]
    return [pltpu.make_async_copy(srcs[a].at[me] if pattern == "grads" else srcs[a], lands[a].at[me],
                                  send.at[a * slots + slots - 1]) for a in range(len(lands))]


def _comm(name, srcs, lands, wait=None, start=None, after=None):
    ns, nl = len(srcs), len(lands)
    na = ns + nl
    arrays = list(srcs) + list(lands)
    n_wait = 2 if wait else 0
    n_start = 2 if start else 0

    def body(*refs):
        ins, lnd = refs[:ns], refs[ns:na]
        if wait:
            for cp in _copies(wait[0], ins, lnd, refs[na], refs[na + 1]):
                cp.wait_send()
                cp.wait_recv()
            for cp in _own_copies(wait[0], ins, lnd, refs[na]):
                cp.wait()
        if start:
            outs = refs[na + n_wait + (after is not None):]
            for cp in _copies(start, ins, lnd, outs[0], outs[1]) + _own_copies(start, ins, lnd, outs[0]):
                cp.start()
            refs[-1][...] = jnp.zeros((8, 128), F32)

    out_shape, out_specs = [], []
    if start:
        sems = pltpu.SemaphoreType.DMA((nl * N_SEMS[start],))
        out_shape += [sems, sems]
        out_specs += [SEM_SPEC, SEM_SPEC]
    out_shape += [pltpu.HBM(a.shape, a.dtype) for a in arrays]
    out_specs += [HBM_SPEC] * na
    if start:
        out_shape.append(jax.ShapeDtypeStruct((8, 128), F32))
        out_specs.append(pl.BlockSpec(memory_space=pltpu.VMEM))
    operands = [pltpu.with_memory_space_constraint(a, pltpu.HBM) for a in arrays]
    operands += list(wait[1:]) if wait else []
    operands += [after] if after is not None else []
    res = pl.pallas_call(
        body, name=name, out_shape=out_shape, out_specs=out_specs,
        in_specs=[HBM_SPEC] * na + [SEM_SPEC] * n_wait + [ANY_SPEC] * (after is not None),
        input_output_aliases={i: n_start + i for i in range(na)},
        compiler_params=pltpu.CompilerParams(has_side_effects=ORDERED_EFFECT),
    )(*operands)
    res = list(res)
    thru = res[n_start:n_start + na]
    return thru[:ns], thru[ns:], (tuple(res[:2]) if start else None), (res[-1] if start else None)


def _adamw_math(w, g, m, v):
    m = ADAM_B1 * m + (1.0 - ADAM_B1) * g
    v = ADAM_B2 * v + (1.0 - ADAM_B2) * (g * g)
    m_hat = m / (1.0 - ADAM_B1 ** ADAM_STEP)
    v_hat = v / (1.0 - ADAM_B2 ** ADAM_STEP)
    delta = -ADAM_LR * (m_hat / (jnp.sqrt(v_hat) + ADAM_EPS) + ADAM_WD * w)
    return delta, m, v


def _adamw(name, parts, w, m, v, tr=128):
    R, C = w.shape
    tr = _tile(R, tr)

    def body(p_ref, w_ref, m_ref, v_ref, g_out, d_out, m_out, v_out):
        g = p_ref[0].astype(F32)
        for d in range(1, N_DEV):
            g = g + p_ref[d].astype(F32)
        g_out[...] = g
        d_out[...], m_out[...], v_out[...] = _adamw_math(w_ref[...], g, m_ref[...], v_ref[...])

    row = pl.BlockSpec((tr, C), lambda i: (i, 0))
    out = jax.ShapeDtypeStruct((R, C), F32)
    return pl.pallas_call(
        body, name=name, grid=(R // tr,),
        in_specs=[pl.BlockSpec((N_DEV, tr, C), lambda i: (0, i, 0)), row, row, row],
        out_specs=[row] * 4, out_shape=[out] * 4,
        compiler_params=_params(dimension_semantics=("parallel",)),
    )(parts, w, m, v)


def _small_allreduce_adamw(name, g, w, m, v):
    R = g.shape[0]

    def body(g_ref, w_ref, m_ref, v_ref, g_out, d_out, m_out, v_out, land, send, recv):
        x, y, c = _position()
        me = _index((x, y, c))
        land[me] = g_ref[...]
        copies = []
        for k in range(1, N_DEV):
            peer = (x ^ (k >> 2), y ^ ((k >> 1) & 1), c ^ (k & 1))
            copies.append(pltpu.make_async_remote_copy(
                src_ref=g_ref, dst_ref=land.at[me], send_sem=send.at[k - 1], recv_sem=recv.at[k - 1],
                device_id=peer, device_id_type=MESH))
        for cp in copies:
            cp.start()
        for cp in copies:
            cp.wait()
        total = land[0]
        for d in range(1, N_DEV):
            total = total + land[d]
        g_out[...] = total
        d_out[...], m_out[...], v_out[...] = _adamw_math(w_ref[...], total, m_ref[...], v_ref[...])

    vm = pl.BlockSpec(memory_space=pltpu.VMEM)
    out = jax.ShapeDtypeStruct((R, 128), F32)
    return pl.pallas_call(
        body, name=name, in_specs=[vm] * 4, out_specs=[vm] * 4, out_shape=[out] * 4,
        scratch_shapes=[pltpu.VMEM((N_DEV, R, 128), F32), pltpu.SemaphoreType.DMA((7,)),
                        pltpu.SemaphoreType.DMA((7,))],
    )(g, w, m, v)


BIG = ("ffn1_w_gate", "ffn1_w_up", "ffn1_w_down", "w_in", "w_pool", "w_out", "w_cq", "w_ckv", "w_co",
       "ffn2_w_gate", "ffn2_w_up", "ffn2_w_down")
SMALL = ("ffn1_norm", "mix_norm", "rel_bias", "pool_scale", "cross_norm", "mem_norm", "ffn2_norm", "final_norm")
ORDER = ("ffn1_norm", "ffn1_w_gate", "ffn1_w_up", "ffn1_w_down", "mix_norm", "w_in", "rel_bias", "w_pool",
         "pool_scale", "w_out", "cross_norm", "mem_norm", "w_cq", "w_ckv", "w_co", "ffn2_norm", "ffn2_w_gate",
         "ffn2_w_up", "ffn2_w_down", "final_norm")
TRANSPOSED = ("ffn1_w_gate", "ffn1_w_up", "ffn2_w_gate", "ffn2_w_up")
ROW_SHARDED = TRANSPOSED + ("ffn1_w_down", "ffn2_w_down", "w_out", "w_cq", "w_ckv")
GATHER_GROUPS = (("ffn1_w_gate",), ("ffn1_w_up",), ("ffn1_w_down",), ("w_in", "w_pool", "w_out"),
                 ("w_cq", "w_ckv", "w_co"), ("ffn2_w_gate", "ffn2_w_up", "ffn2_w_down"))
RELAY_BEFORE_USE = ((0,), (1,), (2,), (3,), (4, 5), ())


def _pack(arrays):
    flat = jnp.concatenate([a.reshape(-1) for a in arrays])
    rows = -(-flat.shape[0] // 1024) * 8
    return jnp.pad(flat, (0, rows * 128 - flat.shape[0])).reshape(rows, 128)


def _unpack(packed, like):
    flat, out, at = packed.reshape(-1), [], 0
    for a in like:
        out.append(flat[at:at + a.size].reshape(a.shape))
        at += a.size
    return out


def _shard2d(a):
    a = a[0]
    return a.reshape(-1, a.shape[-1])


def kernel(x, mem, ffn1_norm, ffn1_w_gate, ffn1_w_up, ffn1_w_down, mix_norm, w_in, rel_bias, w_pool, pool_scale, w_out, cross_norm, mem_norm, w_cq, w_ckv, w_co, ffn2_norm, ffn2_w_gate, ffn2_w_up, ffn2_w_down, final_norm, loss_target, m_ffn1_norm, m_ffn1_w_gate, m_ffn1_w_up, m_ffn1_w_down, m_mix_norm, m_w_in, m_rel_bias, m_w_pool, m_pool_scale, m_w_out, m_cross_norm, m_mem_norm, m_w_cq, m_w_ckv, m_w_co, m_ffn2_norm, m_ffn2_w_gate, m_ffn2_w_up, m_ffn2_w_down, m_final_norm, v_ffn1_norm, v_ffn1_w_gate, v_ffn1_w_up, v_ffn1_w_down, v_mix_norm, v_w_in, v_rel_bias, v_w_pool, v_pool_scale, v_w_out, v_cross_norm, v_mem_norm, v_w_cq, v_w_ckv, v_w_co, v_ffn2_norm, v_ffn2_w_gate, v_ffn2_w_up, v_ffn2_w_down, v_final_norm):
    args = dict(locals())
    def view(n, a):
        return a.transpose(0, 2, 1) if n in TRANSPOSED else a

    w_in_ = {n: view(n, args[n]) for n in ORDER}
    m_in = {n: view(n, args["m_" + n]) for n in ORDER}
    v_in = {n: view(n, args["v_" + n]) for n in ORDER}

    me = 4 * lax.axis_index("x") + 2 * lax.axis_index("y") + lax.axis_index("c")
    n_g, rows = len(POOL_WINDOWS), POOL_GROUP // N_DEV

    def landing(block_shape, dtype):
        return lax.empty((N_DEV,) + tuple(block_shape), dtype)

    gathers, tok = [], None
    for gi, group in enumerate(GATHER_GROUPS):
        shards = [_shard2d(w_in_[n]).astype(BF16) for n in group]
        srcs, lands, sems, tok = _comm("gather_start_%d" % gi, shards, [landing(s.shape, BF16) for s in shards],
                                       start="spread", after=tok)
        gathers.append((srcs, lands, sems))

    def weights(gi, after):
        for ri in RELAY_BEFORE_USE[gi]:
            srcs, lands, sems = gathers[ri]
            _, lands, sems, after = _comm("gather_relay_%d" % ri, srcs, lands, wait=("spread",) + sems,
                                          start="relay", after=after)
            gathers[ri] = (None, lands, sems)
        _, lands, sems = gathers[gi]
        _, lands, _, _ = _comm("gather_finish_%d" % gi, [], lands, wait=("relay",) + sems, after=after)
        out = {}
        for n, full in zip(GATHER_GROUPS[gi], lands):
            if n == "w_pool":
                full = full.reshape(N_DEV, n_g, rows, POOL_GROUP).transpose(1, 0, 2, 3).reshape(n_g, POOL_GROUP, POOL_GROUP)
            out[n] = full.reshape(-1, full.shape[-1]) if n in ROW_SHARDED else full
        return out

    scatters = []

    def emit(gw):
        names = list(gw)
        stacks = []
        for n in names:
            g = gw[n]
            if n == "w_pool":
                g = g.reshape(n_g, N_DEV, rows, POOL_GROUP).transpose(1, 0, 2, 3).astype(BF16)
            stacks.append(g.reshape((N_DEV,) + _shard2d(w_in_[n]).shape))
        lands = [landing(s.shape[1:], s.dtype) for s in stacks]
        srcs, lands, sems, token = _comm("grads_start_%d" % len(scatters), stacks, lands, start="grads")
        scatters.append((names, srcs, lands, sems))
        return token

    small = {n: w_in_[n].reshape(1, -1) for n in SMALL if n != "rel_bias"}
    small["rel_bias"] = rel_bias[0]
    loss_part, grad_x, gs = _local_step(x, mem, loss_target, small, weights, emit, start_token=tok)
    loss = lax.psum(loss_part[0, 0], ("x", "y", "c"))

    grad, delta, new_m, new_v = {}, {}, {}, {}
    after = grad_x
    for si, (names, srcs, lands, sems) in enumerate(scatters):
        _, landed, _, _ = _comm("grads_finish_%d" % si, srcs, lands, wait=("grads",) + sems, after=after)
        for n, parts in zip(names, landed):
            res = _adamw("adamw_" + n, parts, _shard2d(w_in_[n]), _shard2d(m_in[n]), _shard2d(v_in[n]))
            grad[n], delta[n], new_m[n], new_v[n] = [view(n, r.reshape(w_in_[n].shape)) for r in res]
        after = res[0]

    like = [w_in_[n] for n in SMALL]
    gs["rel_bias"] = gs["rel_bias"].reshape(rel_bias.shape)
    res = _small_allreduce_adamw("small_params", _pack([gs[n] for n in SMALL]), _pack(like),
                                 _pack([m_in[n] for n in SMALL]), _pack([v_in[n] for n in SMALL]))
    for d, packed in zip((grad, delta, new_m, new_v), res):
        for n, a in zip(SMALL, _unpack(packed, like)):
            d[n] = a
    return (loss, grad_x, *[grad[n] for n in ORDER], *[delta[n] for n in ORDER],
            *[new_m[n] for n in ORDER], *[new_v[n] for n in ORDER])
```

```python
import functools

import jax
import jax.numpy as jnp
from jax import lax
from jax.experimental import pallas as pl
from jax.experimental.pallas import tpu as pltpu

F32 = jnp.float32
BF16 = jnp.bfloat16

N_DEV = 8
EPS = 1e-6
NEG_INF = -1e30
CHUNK = 64
LEFT_CHUNKS = 8
PAD = LEFT_CHUNKS * CHUNK
QBLK = 4 * CHUNK
KBAND = PAD + QBLK
REL_CLIP = 128
ATTN_HEADS = 16
HEAD_DIM = 64
D_ATTN = ATTN_HEADS * HEAD_DIM
POOL_WINDOWS = (2, 4, 8, 16)
POOL_GROUP = 256
D_POOL = len(POOL_WINDOWS) * POOL_GROUP
CROSS_HEADS = 4
CROSS_DIM = 128
D_CROSS = CROSS_HEADS * CROSS_DIM
FFN_RES = 0.5
ADAM_LR, ADAM_B1, ADAM_B2, ADAM_EPS, ADAM_WD, ADAM_STEP = 0.001, 0.9, 0.999, 1e-08, 0.01, 10

NN = (((1,), (0,)), ((), ()))
NT = (((1,), (1,)), ((), ()))
TN = (((0,), (0,)), ((), ()))
MESH = pl.DeviceIdType.MESH
VMEM_LIMIT = 56 * 1024 * 1024


def _params(**kw):
    return pltpu.CompilerParams(vmem_limit_bytes=VMEM_LIMIT, **kw)


def _bf(v):
    return v if v.dtype == BF16 else v.astype(BF16)


WHOLE = ((Ellipsis,), (Ellipsis,))


def _rms(xv, gain):
    return (xv * lax.rsqrt(jnp.mean(xv * xv, axis=-1, keepdims=True) + EPS)) * gain


def _gemm(name, a, a_spec, b, b_spec, dims, grid, outs, chunks=(WHOLE,), extras=(), epilogue=None, after=None,
          norm=None):
    nex, nout = len(extras), len(outs)
    first_out = 2 + nex + (after is not None) + (norm is not None)

    def body(*refs):
        a_ref, b_ref = refs[:2]
        if norm is not None:
            hn_out, a_ref = refs[first_out + nout], refs[-1]

            @pl.when(pl.program_id(1) == 0)
            def _():
                hn = _rms(refs[0][...], refs[first_out - 1][...]).astype(BF16)
                a_ref[...] = hn
                hn_out[...] = hn

        total = None
        for ia, ib in chunks:
            d = lax.dot_general(_bf(a_ref[ia]), _bf(b_ref[ib]), dims, preferred_element_type=F32)
            total = d if total is None else total + d
        vals = epilogue(total, *[e[...] for e in refs[2:2 + nex]]) if epilogue is not None else (total,)
        for r, v in zip(refs[first_out:first_out + nout], vals):
            r[...] = v.astype(r.dtype)

    operands = [a, b] + [x for x, _, _ in extras]
    in_specs = [pl.BlockSpec(*a_spec), pl.BlockSpec(*b_spec)] + [pl.BlockSpec(blk, m) for _, blk, m in extras]
    if after is not None:
        operands.append(after)
        in_specs.append(pl.BlockSpec(after.shape, lambda i, j: (0, 0)))
    out_specs = [pl.BlockSpec(blk, m) for _, _, blk, m in outs]
    out_shape = [jax.ShapeDtypeStruct(s, d) for s, d, _, _ in outs]
    scratch = []
    if norm is not None:
        operands.append(norm)
        in_specs.append(pl.BlockSpec(norm.shape, lambda i, j: (0, 0)))
        out_specs.append(pl.BlockSpec(*a_spec))
        out_shape.append(jax.ShapeDtypeStruct(a.shape, BF16))
        scratch.append(pltpu.VMEM(a_spec[0], BF16))
    res = pl.pallas_call(
        body, name=name, grid=grid, in_specs=in_specs, out_specs=out_specs, out_shape=out_shape,
        scratch_shapes=scratch,
        compiler_params=_params(dimension_semantics=("parallel", "arbitrary" if norm is not None else "parallel")),
    )(*operands)
    return res[0] if len(res) == 1 else res


def _tile(n, want):
    for t in range(min(n, want), 15, -1):
        if n % t == 0 and t % 16 == 0:
            return t
    return n


def _mm_nn(name, a, b, out_dtype, res=None, tm=1024, tn=1024, norm=None):
    M, K = a.shape
    N = b.shape[1]
    tm, tn = _tile(M, tm), _tile(N, tn)
    extras = [] if res is None else [(res, (tm, tn), lambda i, j: (i, j))]
    epi = None if res is None else (lambda t, r: (r + t,))
    return _gemm(name, a, ((tm, K), lambda i, j: (i, 0)), b, ((K, tn), lambda i, j: (0, j)), NN,
                 (M // tm, N // tn), [((M, N), out_dtype, (tm, tn), lambda i, j: (i, j))], extras=extras, epilogue=epi,
                 norm=norm)


def _mm_nn_cols(name, a, bs, out_dtype, res=None, tm=1024, norm=None):
    M, K = a.shape
    nb, _, w = bs.shape
    tm = _tile(M, tm)
    extras = [] if res is None else [(res, (tm, w), lambda i, j: (i, j))]
    epi = None if res is None else (lambda t, r: (r + t,))
    return _gemm(name, a, ((tm, K), lambda i, j: (i, 0)), bs, ((None, K, w), lambda i, j: (j, 0, 0)), NN,
                 (M // tm, nb), [((M, nb * w), out_dtype, (tm, w), lambda i, j: (i, j))], extras=extras, epilogue=epi,
                 norm=norm)


def _mm_nt(name, a, b, out_dtype, tm=1024, tn=1024, after=None):
    M, K = a.shape
    N = b.shape[0]
    tm, tn = _tile(M, tm), _tile(N, tn)
    return _gemm(name, a, ((tm, K), lambda i, j: (i, 0)), b, ((tn, K), lambda i, j: (j, 0)), NT,
                 (M // tm, N // tn), [((M, N), out_dtype, (tm, tn), lambda i, j: (i, j))], after=after)


def _mm_nt_cols(name, a, bs, out_dtype, tm=1024, tn=512, after=None):
    M = a.shape[0]
    nb, N, w = bs.shape
    tm, tn = _tile(M, tm), _tile(N, tn)
    chunks = [((slice(None), pl.ds(c * w, w)), (c,)) for c in range(nb)]
    return _gemm(name, a, ((tm, nb * w), lambda i, j: (i, 0)), bs, ((nb, tn, w), lambda i, j: (0, j, 0)), NT,
                 (M // tm, N // tn), [((M, N), out_dtype, (tm, tn), lambda i, j: (i, j))], chunks, after=after)


def _mm_tn(name, a, b, tm=1024, tn=1024, col_blocks=None):
    T, Ka = a.shape
    Nb = b.shape[1]
    tm = _tile(Ka, tm)
    if col_blocks is None:
        tn = _tile(Nb, tn)
        out = ((Ka, Nb), BF16, (tm, tn), lambda i, j: (i, j))
    else:
        tn = Nb // col_blocks
        out = ((col_blocks, Ka, tn), BF16, (None, tm, tn), lambda i, j: (j, i, 0))
    return _gemm(name, a, ((T, tm), lambda i, j: (0, i)), b, ((T, tn), lambda i, j: (0, j)), TN,
                 (Ka // tm, Nb // tn), [out])


def _ffn_out(name, a, wt, res=None, scale=1.0, tm=1024, tn=512, after=None):
    nb, M, w = a.shape
    N = wt.shape[1]
    tm, tn = _tile(M, tm), _tile(N, tn)
    chunks = [((c,), (pl.ds(c * w, w),)) for c in range(nb)]
    extras = [] if res is None else [(res, (tm, tn), lambda i, j: (i, j))]
    epi = None if res is None else (lambda t, r: (r + scale * t,))
    return _gemm(name, a, ((nb, tm, w), lambda i, j: (0, i, 0)), wt, ((nb * w, tn), lambda i, j: (0, j)), NN,
                 (M // tm, N // tn), [((M, N), F32, (tm, tn), lambda i, j: (i, j))], chunks, extras, epi, after)


def _ffn_dact(name, dhb, wd, g, u, tm=1024, after=None):
    M, K = dhb.shape
    nb, _, w = g.shape
    tm = _tile(M, tm)
    tr = _tile(tm, 256)
    tokens = [] if after is None else [after]

    def body(dh_ref, wd_ref, g_ref, u_ref, *rest):
        dg_ref, du_ref = rest[-2:]
        pieces = [pl.ds(r * tr, tr) for r in range(tm // tr)]

        def product(rows):
            return lax.dot_general(dh_ref[rows, :], wd_ref[...], NT, preferred_element_type=F32)

        results = []
        dact = product(pieces[0])
        for n, rows in enumerate(pieces):
            ahead = product(pieces[n + 1]) if n + 1 < len(pieces) else None
            results.append(_swiglu_bwd(dact, g_ref[rows, :], u_ref[rows, :]))
            dact = ahead
        for rows, (dg, du) in zip(pieces, results):
            dg_ref[rows, :] = dg.astype(BF16)
            du_ref[rows, :] = du.astype(BF16)

    hid = pl.BlockSpec((None, tm, w), lambda i, j: (j, i, 0))
    return pl.pallas_call(
        body, name=name, grid=(M // tm, nb),
        in_specs=[pl.BlockSpec((tm, K), lambda i, j: (i, 0)), pl.BlockSpec((w, K), lambda i, j: (j, 0)), hid, hid]
        + [pl.BlockSpec(t.shape, lambda i, j: (0, 0)) for t in tokens],
        out_specs=[hid, hid], out_shape=[jax.ShapeDtypeStruct((nb, M, w), BF16)] * 2,
        compiler_params=_params(dimension_semantics=("parallel", "parallel")),
    )(dhb, wd, g, u, *tokens)


def _ffn_dw(name, a, b, scale=1.0, tn=1024, after=None):
    nb, T, w = a.shape
    N = b.shape[1]
    tn = _tile(N, tn)
    epi = None if scale == 1.0 else (lambda t: (t * scale,))
    return _gemm(name, a, ((None, T, w), lambda i, j: (i, 0, 0)), b, ((T, tn), lambda i, j: (0, j)), TN,
                 (nb, N // tn), [((nb * w, N), BF16, (w, tn), lambda i, j: (i, j))], epilogue=epi, after=after)


def _ffn_gate(name, hn, wgt, nb, tm=1024):
    M, K = hn.shape
    w = wgt.shape[0] // nb
    tm = _tile(M, tm)
    return _gemm(name, hn, ((tm, K), lambda i, j: (i, 0)), wgt, ((w, K), lambda i, j: (j, 0)), NT,
                 (M // tm, nb), [((nb, M, w), BF16, (None, tm, w), lambda i, j: (j, i, 0))])


def _ffn_up_act(name, hn, wut, g, tm=1024):
    M, K = hn.shape
    nb, _, w = g.shape
    tm = _tile(M, tm)
    hid = ((None, tm, w), lambda i, j: (j, i, 0))

    def epilogue(u, gate):
        gate = gate.astype(F32)
        return u, gate * jax.nn.sigmoid(gate) * u

    return _gemm(name, hn, ((tm, K), lambda i, j: (i, 0)), wut, ((w, K), lambda i, j: (j, 0)), NT,
                 (M // tm, nb), [((nb, M, w), BF16) + hid] * 2, extras=[(g,) + hid], epilogue=epilogue)


def _ffn_up(name, h, gain, wgt, wut, nb, tm=1024):
    M, K = h.shape
    w = wgt.shape[0] // nb
    tm = _tile(M, tm)

    def body(h_ref, gain_ref, g_ref, u_ref, hn_ref, og, ou, oa, a_ref):
        @pl.when(pl.program_id(1) == 0)
        def _():
            hn = _rms(h_ref[...], gain_ref[...]).astype(BF16)
            a_ref[...] = hn
            hn_ref[...] = hn

        a = a_ref[...]
        g = lax.dot_general(a, g_ref[...], NT, preferred_element_type=F32)
        u = lax.dot_general(a, u_ref[...], NT, preferred_element_type=F32)
        og[...] = g.astype(BF16)
        ou[...] = u.astype(BF16)
        oa[...] = (g * jax.nn.sigmoid(g) * u).astype(BF16)

    rows = pl.BlockSpec((tm, K), lambda i, j: (i, 0))
    wspec = pl.BlockSpec((w, K), lambda i, j: (j, 0))
    ospec = pl.BlockSpec((None, tm, w), lambda i, j: (j, i, 0))
    return pl.pallas_call(
        body, name=name, grid=(M // tm, nb),
        in_specs=[rows, pl.BlockSpec((1, K), lambda i, j: (0, 0)), wspec, wspec],
        out_specs=[rows] + [ospec] * 3,
        out_shape=[jax.ShapeDtypeStruct((M, K), BF16)] + [jax.ShapeDtypeStruct((nb, M, w), BF16)] * 3,
        scratch_shapes=[pltpu.VMEM((tm, K), BF16)],
        compiler_params=_params(dimension_semantics=("parallel", "arbitrary")),
    )(h, gain, wgt, wut)


def _swiglu_bwd(dact, g, u):
    g = g.astype(F32)
    u = u.astype(F32)
    sig = jax.nn.sigmoid(g)
    silu = g * sig
    d = FFN_RES * dact
    return d * u * (sig * (1.0 + g * (1.0 - sig))), d * silu


def _rms_fwd(name, x, gain, tr=512, after=None):
    R, D = x.shape
    tr = _tile(R, tr)

    def body(x_ref, g_ref, *rest):
        xv = x_ref[...]
        y = xv * lax.rsqrt(jnp.mean(xv * xv, axis=-1, keepdims=True) + EPS)
        rest[-1][...] = (y * g_ref[...]).astype(BF16)

    tokens = [] if after is None else [after]
    return pl.pallas_call(
        body, name=name, grid=(R // tr,),
        in_specs=[pl.BlockSpec((tr, D), lambda i: (i, 0)), pl.BlockSpec((1, D), lambda i: (0, 0))]
        + [pl.BlockSpec(t.shape, lambda i: (0, 0)) for t in tokens],
        out_specs=pl.BlockSpec((tr, D), lambda i: (i, 0)), out_shape=jax.ShapeDtypeStruct((R, D), BF16),
        compiler_params=_params(dimension_semantics=("parallel",)),
    )(x, gain, *tokens)


def _rms_bwd_math(xv, gain, dy):
    rstd = lax.rsqrt(jnp.mean(xv * xv, axis=-1, keepdims=True) + EPS)
    xhat = xv * rstd
    dxh = dy * gain
    dx = rstd * (dxh - xhat * jnp.mean(dxh * xhat, axis=-1, keepdims=True))
    return dx, jnp.sum(dy * xhat, axis=0, keepdims=True)


def _rms_bwd(name, x, gain, dy, skip=None, tr=256):
    R, D = x.shape
    tr = _tile(R, tr)
    has_skip = skip is not None

    def body(*refs):
        x_ref, g_ref, dy_ref = refs[:3]
        dx_ref, dxb_ref, dg_ref = refs[-3:]
        dx, dg = _rms_bwd_math(x_ref[...], g_ref[...], dy_ref[...].astype(F32))
        if has_skip:
            dx = dx + refs[3][...]
        dx_ref[...] = dx
        dxb_ref[...] = dx.astype(BF16)

        @pl.when(pl.program_id(0) == 0)
        def _():
            dg_ref[...] = dg

        @pl.when(pl.program_id(0) > 0)
        def _():
            dg_ref[...] += dg

    row = pl.BlockSpec((tr, D), lambda i: (i, 0))
    vec = pl.BlockSpec((1, D), lambda i: (0, 0))
    return pl.pallas_call(
        body, name=name, grid=(R // tr,),
        in_specs=[row, vec, row] + ([row] if has_skip else []),
        out_specs=[row, row, vec],
        out_shape=[jax.ShapeDtypeStruct((R, D), F32), jax.ShapeDtypeStruct((R, D), BF16),
                   jax.ShapeDtypeStruct((1, D), F32)],
        compiler_params=_params(dimension_semantics=("arbitrary",)),
    )(*([x, gain, dy] + ([skip] if has_skip else [])))


def _loss_and_grad(name, h, gain, target, tr=256):
    R, D = h.shape
    tr = _tile(R, tr)

    def body(h_ref, g_ref, t_ref, loss_ref, dh_ref, dhb_ref, dg_ref):
        hv, gain_v = h_ref[...], g_ref[...]
        y = (hv * lax.rsqrt(jnp.mean(hv * hv, axis=-1, keepdims=True) + EPS)) * gain_v
        err = y - t_ref[...]
        part = jnp.full((8, 128), 0.5 * jnp.sum(jnp.mean(err * err, axis=-1, keepdims=True)), F32)
        dh, dg = _rms_bwd_math(hv, gain_v, err * (1.0 / D))
        dh_ref[...] = dh
        dhb_ref[...] = dh.astype(BF16)

        @pl.when(pl.program_id(0) == 0)
        def _():
            dg_ref[...] = dg
            loss_ref[...] = part

        @pl.when(pl.program_id(0) > 0)
        def _():
            dg_ref[...] += dg
            loss_ref[...] += part

    row = pl.BlockSpec((tr, D), lambda i: (i, 0))
    vec = pl.BlockSpec((1, D), lambda i: (0, 0))
    return pl.pallas_call(
        body, name=name, grid=(R // tr,), in_specs=[row, vec, row],
        out_specs=[pl.BlockSpec((8, 128), lambda i: (0, 0)), row, row, vec],
        out_shape=[jax.ShapeDtypeStruct((8, 128), F32), jax.ShapeDtypeStruct((R, D), F32),
                   jax.ShapeDtypeStruct((R, D), BF16), jax.ShapeDtypeStruct((1, D), F32)],
        compiler_params=_params(dimension_semantics=("arbitrary",)),
    )(h, gain, target)


def _bias_tile(name, rel):
    width = KBAND + QBLK
    sat = rel[:, 2 * REL_CLIP:]
    n_left = PAD - REL_CLIP + 1
    row0 = jnp.concatenate([jnp.broadcast_to(sat, (ATTN_HEADS, n_left)), rel[:, :2 * REL_CLIP][:, ::-1],
                            jnp.broadcast_to(sat, (ATTN_HEADS, width - n_left - 2 * REL_CLIP))], axis=1)

    def body(e_ref, o_ref):
        rows = pltpu.roll(jnp.broadcast_to(e_ref[...], (QBLK, width)), 0, 1, stride=1, stride_axis=0)
        i = lax.broadcasted_iota(jnp.int32, (QBLK, KBAND), 0) // CHUNK
        j = lax.broadcasted_iota(jnp.int32, (QBLK, KBAND), 1) // CHUNK
        o_ref[...] = jnp.where((j >= i) & (j <= i + LEFT_CHUNKS), rows[:, :KBAND], NEG_INF)

    return pl.pallas_call(
        body, name=name, grid=(ATTN_HEADS,),
        in_specs=[pl.BlockSpec((None, 1, width), lambda h: (h, 0, 0))],
        out_specs=pl.BlockSpec((None, QBLK, KBAND), lambda h: (h, 0, 0)),
        out_shape=jax.ShapeDtypeStruct((ATTN_HEADS, QBLK, KBAND), F32),
        compiler_params=_params(dimension_semantics=("parallel",)),
    )(row0.reshape(ATTN_HEADS, 1, width))


ATTN_SCALE = HEAD_DIM ** -0.5
ROW_PIECES = 1


def _stack_heads(x, first):
    zero = jnp.zeros_like(x)
    return jnp.concatenate([jnp.where(first, x, zero), jnp.where(first, zero, x)], axis=0)


def _band_softmax(q_half_scaled, kb, bias, left_mask):
    s = lax.dot_general(q_half_scaled, kb, NT, preferred_element_type=F32) + bias + left_mask
    e = jnp.exp(s - jnp.max(s, axis=-1, keepdims=True))
    return e * (1.0 / jnp.sum(e, axis=-1, keepdims=True))


def _left_mask(qb):
    kpos = qb * QBLK - PAD + lax.broadcasted_iota(jnp.int32, (1, KBAND), 1)
    return jnp.where(kpos >= 0, 0.0, NEG_INF).astype(F32)


def _fill_padded(dst, src, S):
    dst[pl.ds(0, PAD), :] = jnp.zeros((PAD, dst.shape[1]), dst.dtype)
    dst[pl.ds(PAD, S), :] = src[...].astype(dst.dtype)


def _attn_fwd(name, z, bias):
    B, S, _ = z.shape
    nh2 = ATTN_HEADS // 2

    def body(q_ref, k_ref, v_ref, b_ref, o_ref, kp, vp):
        qb = pl.program_id(2)

        @pl.when(qb == 0)
        def _():
            _fill_padded(kp, k_ref, S)
            _fill_padded(vp, v_ref, S)

        start = pl.multiple_of(qb * QBLK, QBLK)
        kb, vb = kp[pl.ds(start, KBAND), :], vp[pl.ds(start, KBAND), :]
        q = (q_ref[...] * ATTN_SCALE).astype(BF16)
        first = lax.broadcasted_iota(jnp.int32, (QBLK, 2 * HEAD_DIM), 1) < HEAD_DIM
        left = _left_mask(qb)
        zero = jnp.zeros_like(q)
        qh = [jnp.where(first, q, zero), jnp.where(first, zero, q)]
        rp = QBLK // ROW_PIECES
        chains = [(a, r) for r in range(ROW_PIECES) for a in range(2)]
        ss = [lax.dot_general(qh[a][r * rp:(r + 1) * rp], kb, NT, preferred_element_type=F32) for a, r in chains]
        ps = []
        for (a, r), s in zip(chains, ss):
            s = s + b_ref[a, pl.ds(r * rp, rp), :] + left
            e = jnp.exp(s - jnp.max(s, axis=-1, keepdims=True))
            ps.append((e * (1.0 / jnp.sum(e, axis=-1, keepdims=True))).astype(BF16))
        os_ = [jnp.dot(p, vb, preferred_element_type=F32) for p in ps]
        for r in range(ROW_PIECES):
            o_ref[pl.ds(r * rp, rp), :] = jnp.where(first[:rp], os_[2 * r], os_[2 * r + 1]).astype(BF16)

    return pl.pallas_call(
        body, name=name, grid=(B, nh2, S // QBLK),
        in_specs=[pl.BlockSpec((None, QBLK, 128), lambda b, h, i: (b, i, h)),
                  pl.BlockSpec((None, S, 128), lambda b, h, i: (b, 0, nh2 + h)),
                  pl.BlockSpec((None, S, 128), lambda b, h, i: (b, 0, 2 * nh2 + h)),
                  pl.BlockSpec((2, QBLK, KBAND), lambda b, h, i: (h, 0, 0))],
        out_specs=pl.BlockSpec((None, QBLK, 128), lambda b, h, i: (b, i, h)),
        out_shape=jax.ShapeDtypeStruct((B, S, D_ATTN + D_POOL), BF16),
        scratch_shapes=[pltpu.VMEM((PAD + S, 128), BF16), pltpu.VMEM((PAD + S, 128), BF16)],
        compiler_params=_params(dimension_semantics=("parallel", "parallel", "arbitrary")),
    )(z, z, z, bias)


def _attn_bwd(name, z, bias, dcat):
    B, S, _ = z.shape
    nh2 = ATTN_HEADS // 2
    nqb = S // QBLK

    def body(q_ref, k_ref, v_ref, b_ref, do_ref, dq_ref, dk_ref, dv_ref, db_ref, kp, vp, dka, dva):
        b, qb = pl.program_id(1), pl.program_id(2)

        @pl.when(qb == 0)
        def _():
            _fill_padded(kp, k_ref, S)
            _fill_padded(vp, v_ref, S)
            dka[...] = jnp.zeros_like(dka)
            dva[...] = jnp.zeros_like(dva)

        @pl.when((qb == 0) & (b == 0))
        def _():
            db_ref[...] = jnp.zeros_like(db_ref)

        start = pl.multiple_of(qb * QBLK, QBLK)
        band = pl.ds(start, KBAND)
        kb, vb = kp[band, :], vp[band, :]
        q = (q_ref[...] * ATTN_SCALE).astype(BF16)
        do = do_ref[...]
        first = lax.broadcasted_iota(jnp.int32, (QBLK, 2 * HEAD_DIM), 1) < HEAD_DIM
        left = _left_mask(qb)
        q2, do2 = _stack_heads(q, first), _stack_heads(do, first)
        p = _band_softmax(q2, kb, b_ref[...].reshape(2 * QBLK, KBAND), left)
        dp = lax.dot_general(do2, vb, NT, preferred_element_type=F32)
        ds = p * (dp - jnp.sum(p * dp, axis=-1, keepdims=True))
        db_ref[...] += ds.reshape(2, QBLK, KBAND)
        dsb = ds.astype(BF16)
        dq = jnp.dot(dsb, kb, preferred_element_type=F32)
        dq_ref[...] = (jnp.where(first, dq[:QBLK], dq[QBLK:]) * ATTN_SCALE).astype(BF16)
        dka[band, :] += lax.dot_general(dsb, q2, TN, preferred_element_type=F32)
        dva[band, :] += lax.dot_general(p.astype(BF16), do2, TN, preferred_element_type=F32)

        @pl.when(qb == nqb - 1)
        def _():
            dk_ref[...] = dka[pl.ds(PAD, S), :].astype(BF16)
            dv_ref[...] = dva[pl.ds(PAD, S), :].astype(BF16)

    qspec = pl.BlockSpec((None, QBLK, 128), lambda h, b, i: (b, i, h))
    kvout = pl.BlockSpec((None, S, 128), lambda h, b, i: (b, 0, h))
    bspec = pl.BlockSpec((2, QBLK, KBAND), lambda h, b, i: (h, 0, 0))
    act = jax.ShapeDtypeStruct((B, S, D_ATTN), BF16)
    return pl.pallas_call(
        body, name=name, grid=(nh2, B, nqb),
        in_specs=[qspec,
                  pl.BlockSpec((None, S, 128), lambda h, b, i: (b, 0, nh2 + h)),
                  pl.BlockSpec((None, S, 128), lambda h, b, i: (b, 0, 2 * nh2 + h)),
                  bspec, qspec],
        out_specs=[qspec, kvout, kvout, bspec],
        out_shape=[act, act, act, jax.ShapeDtypeStruct((ATTN_HEADS, QBLK, KBAND), F32)],
        scratch_shapes=[pltpu.VMEM((PAD + S, 128), BF16), pltpu.VMEM((PAD + S, 128), BF16),
                        pltpu.VMEM((PAD + S, 128), F32), pltpu.VMEM((PAD + S, 128), F32)],
        compiler_params=_params(dimension_semantics=("arbitrary", "arbitrary", "arbitrary")),
    )(z, z, z, bias, dcat)


def _bias_grad(name, dbias):
    width = KBAND + QBLK

    def body(d_ref, o_ref):
        acc = jnp.zeros((1, width), F32)
        for i in range(QBLK):
            row = jnp.concatenate([d_ref[pl.ds(i, 1), :], jnp.zeros((1, QBLK), F32)], axis=1)
            shift = QBLK - 1 - i
            acc = acc + (pltpu.roll(row, shift, 1) if shift else row)
        o_ref[...] = acc

    return pl.pallas_call(
        body, name=name, grid=(ATTN_HEADS,),
        in_specs=[pl.BlockSpec((None, QBLK, KBAND), lambda h: (h, 0, 0))],
        out_specs=pl.BlockSpec((None, 1, width), lambda h: (h, 0, 0)),
        out_shape=jax.ShapeDtypeStruct((ATTN_HEADS, 1, width), F32),
        compiler_params=_params(dimension_semantics=("parallel",)),
    )(dbias)


def _rel_grad_from_diagonals(diag):
    top = PAD + QBLK - 1 - REL_CLIP
    sat = jnp.sum(diag[:, :top + 1], axis=1, keepdims=True)
    mid = diag[:, top + 1:top + 2 * REL_CLIP][:, ::-1]
    return jnp.concatenate([jnp.zeros_like(sat), mid, sat], axis=1)


def _shift_rows(x, k, forward):
    S = x.shape[0]
    t = lax.broadcasted_iota(jnp.int32, x.shape, 0)
    if forward:
        return jnp.where(t < S - k, pltpu.roll(x, S - k, 0), 0.0)
    return jnp.where(t >= k, pltpu.roll(x, k, 0), 0.0)


def _window_sum(x, g, forward):
    s = x + _shift_rows(x, 1, forward)
    out = s
    for n, k in enumerate((2, 4, 8)):
        s = s + _shift_rows(s, k, forward)
        out = jnp.where(g > n, s, out)
    return out


def _pool_count(S, g):
    t = lax.broadcasted_iota(jnp.int32, (S, 1), 0)
    w = jnp.left_shift(2, g)
    return jnp.minimum(t + 1, w).astype(F32)


def _pool_fwd(name, z, wp, pscale, mixed):
    B, S, _ = z.shape
    c0 = 3 * D_ATTN // POOL_GROUP
    y0 = D_ATTN // POOL_GROUP

    def body(u_ref, w_ref, s_ref, mixed_ref, d_ref, y_ref):
        g = pl.program_id(1)
        u = u_ref[...]
        d = (_window_sum(u, g, False) / _pool_count(S, g) - u).astype(BF16)
        d_ref[...] = d
        y_ref[...] = (jnp.dot(d, w_ref[...], preferred_element_type=F32) * s_ref[...]).astype(BF16)

    return pl.pallas_call(
        body, name=name, grid=(B, len(POOL_WINDOWS)),
        in_specs=[pl.BlockSpec((None, S, POOL_GROUP), lambda b, g: (b, 0, c0 + g)),
                  pl.BlockSpec((None, POOL_GROUP, POOL_GROUP), lambda b, g: (g, 0, 0)),
                  pl.BlockSpec((1, POOL_GROUP), lambda b, g: (0, g)),
                  pl.BlockSpec(memory_space=pl.ANY)],
        out_specs=[pl.BlockSpec((None, S, POOL_GROUP), lambda b, g: (b, 0, g)),
                   pl.BlockSpec((None, S, POOL_GROUP), lambda b, g: (b, 0, y0 + g))],
        out_shape=[jax.ShapeDtypeStruct((B, S, D_POOL), BF16), jax.ShapeDtypeStruct(mixed.shape, BF16)],
        input_output_aliases={3: 1},
        compiler_params=_params(dimension_semantics=("parallel", "parallel")),
    )(z, wp, pscale, mixed)


def _pool_bwd(name, d, wp, pscale, dcat):
    B, S, _ = d.shape
    c0 = D_ATTN // POOL_GROUP

    def body(d_ref, w_ref, s_ref, dy_ref, du_ref, dw_ref, dsc_ref):
        g, b = pl.program_id(0), pl.program_id(1)
        dv = d_ref[...]
        dy = dy_ref[...].astype(F32)
        w = w_ref[...]
        ypre = jnp.dot(dv, w, preferred_element_type=F32)
        dyp = (dy * s_ref[...]).astype(BF16)
        dd = lax.dot_general(dyp, w, NT, preferred_element_type=F32)
        du_ref[...] = (_window_sum(dd / _pool_count(S, g), g, True) - dd).astype(BF16)
        dw = lax.dot_general(dv, dyp, TN, preferred_element_type=F32)
        dsc = jnp.sum(dy * ypre, axis=0, keepdims=True)

        @pl.when(b == 0)
        def _():
            dw_ref[...] = dw
            dsc_ref[...] = dsc

        @pl.when(b > 0)
        def _():
            dw_ref[...] += dw
            dsc_ref[...] += dsc

    blk = pl.BlockSpec((None, S, POOL_GROUP), lambda g, b: (b, 0, g))
    wspec = pl.BlockSpec((None, POOL_GROUP, POOL_GROUP), lambda g, b: (g, 0, 0))
    sspec = pl.BlockSpec((1, POOL_GROUP), lambda g, b: (0, g))
    return pl.pallas_call(
        body, name=name, grid=(len(POOL_WINDOWS), B),
        in_specs=[blk, wspec, sspec, pl.BlockSpec((None, S, POOL_GROUP), lambda g, b: (b, 0, c0 + g))],
        out_specs=[blk, wspec, sspec],
        out_shape=[jax.ShapeDtypeStruct((B, S, D_POOL), BF16),
                   jax.ShapeDtypeStruct((len(POOL_WINDOWS), POOL_GROUP, POOL_GROUP), F32),
                   jax.ShapeDtypeStruct((1, D_POOL), F32)],
        compiler_params=_params(dimension_semantics=("arbitrary", "arbitrary")),
    )(d, wp, pscale, dcat)


def _cross_softmax(q, k):
    s = lax.dot_general(q, k, NT, preferred_element_type=F32) * (CROSS_DIM ** -0.5)
    e = jnp.exp(s - jnp.max(s, axis=-1, keepdims=True))
    return e * (1.0 / jnp.sum(e, axis=-1, keepdims=True))


def _cross_fwd(name, qc, kv, tq=512):
    B, S, _ = qc.shape
    M = kv.shape[1]
    tq = _tile(S, tq)

    def body(q_ref, k_ref, v_ref, o_ref):
        p = _cross_softmax(q_ref[...], k_ref[...])
        o_ref[...] = jnp.dot(p.astype(BF16), v_ref[...], preferred_element_type=F32).astype(BF16)

    qspec = pl.BlockSpec((None, tq, CROSS_DIM), lambda b, h, i: (b, i, h))
    return pl.pallas_call(
        body, name=name, grid=(B, CROSS_HEADS, S // tq),
        in_specs=[qspec, pl.BlockSpec((None, M, CROSS_DIM), lambda b, h, i: (b, 0, h)),
                  pl.BlockSpec((None, M, CROSS_DIM), lambda b, h, i: (b, 0, CROSS_HEADS + h))],
        out_specs=qspec, out_shape=jax.ShapeDtypeStruct((B, S, D_CROSS), BF16),
        compiler_params=_params(dimension_semantics=("parallel", "parallel", "parallel")),
    )(qc, kv, kv)


def _cross_bwd(name, qc, kv, do, tq=512):
    B, S, _ = qc.shape
    M = kv.shape[1]
    tq = _tile(S, tq)
    nq = S // tq
    scale = CROSS_DIM ** -0.5

    def body(q_ref, k_ref, v_ref, do_ref, dq_ref, dk_ref, dv_ref, dka, dva):
        i = pl.program_id(2)
        q, k, v, dov = q_ref[...], k_ref[...], v_ref[...], do_ref[...]
        p = _cross_softmax(q, k)
        dp = lax.dot_general(dov, v, NT, preferred_element_type=F32)
        ds = ((p * (dp - jnp.sum(p * dp, axis=-1, keepdims=True))) * scale).astype(BF16)
        dq_ref[...] = jnp.dot(ds, k, preferred_element_type=F32).astype(BF16)
        dk = lax.dot_general(ds, q, TN, preferred_element_type=F32)
        dv = lax.dot_general(p.astype(BF16), dov, TN, preferred_element_type=F32)

        @pl.when(i == 0)
        def _():
            dka[...] = dk
            dva[...] = dv

        @pl.when(i > 0)
        def _():
            dka[...] += dk
            dva[...] += dv

        @pl.when(i == nq - 1)
        def _():
            dk_ref[...] = dka[...].astype(BF16)
            dv_ref[...] = dva[...].astype(BF16)

    qspec = pl.BlockSpec((None, tq, CROSS_DIM), lambda b, h, i: (b, i, h))
    kspec = pl.BlockSpec((None, M, CROSS_DIM), lambda b, h, i: (b, 0, h))
    return pl.pallas_call(
        body, name=name, grid=(B, CROSS_HEADS, nq),
        in_specs=[qspec, kspec, pl.BlockSpec((None, M, CROSS_DIM), lambda b, h, i: (b, 0, CROSS_HEADS + h)), qspec],
        out_specs=[qspec, kspec, kspec],
        out_shape=[jax.ShapeDtypeStruct((B, S, D_CROSS), BF16), jax.ShapeDtypeStruct((B, M, D_CROSS), BF16),
                   jax.ShapeDtypeStruct((B, M, D_CROSS), BF16)],
        scratch_shapes=[pltpu.VMEM((M, CROSS_DIM), F32), pltpu.VMEM((M, CROSS_DIM), F32)],
        compiler_params=_params(dimension_semantics=("parallel", "parallel", "arbitrary")),
    )(qc, kv, kv, do)


def _local_step(x, mem, target, small, weights, emit, start_token=None):
    B, S, D = x.shape
    T = B * S
    x2, t2 = x.reshape(T, D), target.reshape(T, D)
    mem2 = mem.reshape(-1, D)
    n_mem = mem.shape[1]
    wts = {}

    hn1 = _rms_fwd("norm_ffn1", x2, small["ffn1_norm"], after=start_token)
    memn = _rms_fwd("norm_mem", mem2, small["mem_norm"])
    bias = _bias_tile("bias_tile", small["rel_bias"])
    wts.update(weights(0, [hn1, memn, bias]))
    g1 = _ffn_gate("ffn1_gate", hn1, wts["ffn1_w_gate"], N_DEV)
    wts.update(weights(1, g1))
    u1, a1 = _ffn_up_act("ffn1_up", hn1, wts["ffn1_w_up"], g1)
    wts.update(weights(2, a1))
    h1 = _ffn_out("ffn1_down", a1, wts["ffn1_w_down"], res=x2, scale=FFN_RES)
    wts.update(weights(3, h1))
    z, hn2 = _mm_nn_cols("mix_in", h1, wts["w_in"], F32, norm=small["mix_norm"])
    z = z.reshape(B, S, -1)
    mixed = _attn_fwd("attn_fwd", z, bias)
    d_pool, mixed = _pool_fwd("pool_fwd", z, wts["w_pool"], small["pool_scale"], mixed)
    cat = mixed.reshape(T, -1)
    h2 = _mm_nn("mix_out", cat, wts["w_out"], F32, res=h1)
    wts.update(weights(4, h2))
    qc, hn3 = _mm_nn("cross_q", h2, wts["w_cq"], BF16, norm=small["cross_norm"])
    kv = _mm_nn("cross_kv", memn, wts["w_ckv"], BF16)
    o = _cross_fwd("cross_fwd", qc.reshape(B, S, -1), kv.reshape(B, n_mem, -1)).reshape(T, -1)
    h3 = _mm_nn_cols("cross_out", o, wts["w_co"], F32, res=h2)
    wts.update(weights(5, h3))
    hn4, g2, u2, a2 = _ffn_up("ffn2_up", h3, small["ffn2_norm"], wts["ffn2_w_gate"], wts["ffn2_w_up"], N_DEV)
    h4 = _ffn_out("ffn2_down", a2, wts["ffn2_w_down"], res=h3, scale=FFN_RES)

    gs = {}
    loss_part, dh4, dh4b, gs["final_norm"] = _loss_and_grad("loss", h4, small["final_norm"], t2)

    def ffn_bwd(tag, dh, dhb, h_in, hn, g, u, a, wg, wu, wd, gain):
        tok = emit({tag + "_w_down": _ffn_dw(tag + "_dwd", a, dhb, scale=FFN_RES)})
        dg, du = _ffn_dact(tag + "_dact", dhb, wd, g, u, after=tok)
        tok = emit({tag + "_w_gate": _ffn_dw(tag + "_dwg", dg, hn)})
        tok = emit({tag + "_w_up": _ffn_dw(tag + "_dwu", du, hn, after=tok)})
        dhn = _ffn_out(tag + "_dhn_g", dg, wg, after=tok)
        dhn = _ffn_out(tag + "_dhn_u", du, wu, res=dhn)
        return _rms_bwd(tag + "_dnorm", h_in, gain, dhn, skip=dh)

    dh3, dh3b, gs["ffn2_norm"] = ffn_bwd("ffn2", dh4, dh4b, h3, hn4, g2, u2, a2, wts["ffn2_w_gate"],
                                         wts["ffn2_w_up"], wts["ffn2_w_down"], small["ffn2_norm"])
    do = _mm_nt_cols("cross_do", dh3b, wts["w_co"], BF16, tn=D_CROSS)
    gw = {"w_co": _mm_tn("cross_dwo", o, dh3b, tm=D_CROSS, col_blocks=N_DEV)}
    dqc, dk, dv = _cross_bwd("cross_bwd", qc.reshape(B, S, -1), kv.reshape(B, n_mem, -1), do.reshape(B, S, -1))
    dqc = dqc.reshape(T, -1)
    dkv = jnp.concatenate([dk, dv], axis=-1).reshape(B * n_mem, -1)
    gw["w_cq"] = _mm_tn("cross_dwq", hn3, dqc, tn=D_CROSS)
    gw["w_ckv"] = _mm_tn("cross_dwkv", memn, dkv)
    tok = emit(gw)
    dhn3 = _mm_nt("cross_dhn", dqc, wts["w_cq"], F32, after=tok)
    dmemn = _mm_nt("cross_dmem", dkv, wts["w_ckv"], F32, tm=512)
    _, _, gs["mem_norm"] = _rms_bwd("mem_dnorm", mem2, small["mem_norm"], dmemn)
    dh2, dh2b, gs["cross_norm"] = _rms_bwd("cross_dnorm", h2, small["cross_norm"], dhn3, skip=dh3)
    dcat = _mm_nt("mix_dcat", dh2b, wts["w_out"], BF16)
    gw = {"w_out": _mm_tn("mix_dwout", cat, dh2b)}
    dcat3 = dcat.reshape(B, S, -1)
    dq, dkk, dvv, dbias = _attn_bwd("attn_bwd", z, bias, dcat3)
    du, gw["w_pool"], gs["pool_scale"] = _pool_bwd("pool_bwd", d_pool, wts["w_pool"], small["pool_scale"], dcat3)
    gs["rel_bias"] = _rel_grad_from_diagonals(_bias_grad("bias_grad", dbias)[:, 0, :])
    dz = jnp.concatenate([dq, dkk, dvv, du], axis=-1).reshape(T, -1)
    gw["w_in"] = _mm_tn("mix_dwin", hn2, dz, col_blocks=N_DEV)
    tok = emit(gw)
    dhn2 = _mm_nt_cols("mix_dhn", dz, wts["w_in"], F32, after=tok)
    dh1, dh1b, gs["mix_norm"] = _rms_bwd("mix_dnorm", h1, small["mix_norm"], dhn2, skip=dh2)
    dx, _, gs["ffn1_norm"] = ffn_bwd("ffn1", dh1, dh1b, x2, hn1, g1, u1, a1, wts["ffn1_w_gate"],
                                     wts["ffn1_w_up"], wts["ffn1_w_down"], small["ffn1_norm"])
    return loss_part, dx.reshape(B, S, D), gs


def _position():
    return lax.axis_index("x"), lax.axis_index("y"), lax.axis_index("c")


def _index(p):
    return 4 * p[0] + 2 * p[1] + p[2]


HBM_SPEC = pl.BlockSpec(memory_space=pltpu.HBM)
SEM_SPEC = pl.BlockSpec(memory_space=pltpu.SEMAPHORE)
ANY_SPEC = pl.BlockSpec(memory_space=pl.ANY)
ORDERED_EFFECT = pltpu.SideEffectType.DATAFLOW_SIDE_EFFECTING


N_COPIES = {"grads": N_DEV - 1, "spread": 4, "relay": 3}
N_SEMS = {"grads": N_DEV, "spread": 5, "relay": 3}


def _copies(pattern, srcs, lands, send, recv, base=0):
    x, y, c = _position()
    me, sibling = _index((x, y, c)), (x, y, 1 - c)
    chips = [(1 - x, y), (x, 1 - y), (1 - x, 1 - y)]
    if pattern == "grads":
        targets = [(x ^ (k >> 2), y ^ ((k >> 1) & 1), c ^ (k & 1)) for k in range(1, N_DEV)]
    else:
        targets = [sibling] + [(*chip, c) for chip in chips]
    per, slots, out = N_COPIES[pattern], N_SEMS[pattern], []
    for k in range(per):
        for a in range(len(lands)):
            if pattern == "relay":
                src = dst = lands[a].at[_index((*chips[k], c))]
                to = sibling
            else:
                to = targets[k]
                src = srcs[a].at[_index(to)] if pattern == "grads" else srcs[a]
                dst = lands[a].at[me]
            slot = base + a * slots + k
            out.append(pltpu.make_async_remote_copy(src_ref=src, dst_ref=dst, send_sem=send.at[slot],
                                                    recv_sem=recv.at[slot], device_id=to, device_id_type=MESH))
    return out


def _own_copies(pattern, srcs, lands, send, base=0):
    if pattern == "relay":
        return []
    x, y, c = _position()
    me = _index((x, y, c))
    slots = N_SEMS[pattern]
    return [pltpu.make_async_copy(srcs[a].at[me] if pattern == "grads" else srcs[a], lands[a].at[me],
                                  send.at[base + a * slots + slots - 1]) for a in range(len(lands))]


def _comm(name, srcs, lands, wait=None, start=None, after=None):
    after = [] if after is None else list(after) if isinstance(after, (list, tuple)) else [after]
    ns, nl = len(srcs), len(lands)
    na = ns + nl
    arrays = list(srcs) + list(lands)
    n_wait = 2 if wait else 0
    n_start = 2 if start else 0

    def body(*refs):
        ins, lnd = refs[:ns], refs[ns:na]
        if wait:
            base = wait[3] if len(wait) > 3 else 0
            for cp in _copies(wait[0], ins, lnd, refs[na], refs[na + 1], base):
                cp.wait_send()
                cp.wait_recv()
            for cp in _own_copies(wait[0], ins, lnd, refs[na], base):
                cp.wait()
        if start:
            outs = refs[na + n_wait + len(after):]
            for cp in _copies(start, ins, lnd, outs[0], outs[1]) + _own_copies(start, ins, lnd, outs[0]):
                cp.start()
            refs[-1][...] = jnp.zeros((8, 128), F32)

    out_shape, out_specs = [], []
    if start:
        sems = pltpu.SemaphoreType.DMA((nl * N_SEMS[start],))
        out_shape += [sems, sems]
        out_specs += [SEM_SPEC, SEM_SPEC]
    out_shape += [pltpu.HBM(a.shape, a.dtype) for a in arrays]
    out_specs += [HBM_SPEC] * na
    if start:
        out_shape.append(jax.ShapeDtypeStruct((8, 128), F32))
        out_specs.append(pl.BlockSpec(memory_space=pltpu.VMEM))
    operands = [pltpu.with_memory_space_constraint(a, pltpu.HBM) for a in arrays]
    operands += list(wait[1:3]) if wait else []
    operands += after
    res = pl.pallas_call(
        body, name=name, out_shape=out_shape, out_specs=out_specs,
        in_specs=[HBM_SPEC] * na + [SEM_SPEC] * n_wait + [ANY_SPEC] * len(after),
        input_output_aliases={i: n_start + i for i in range(na)},
        compiler_params=pltpu.CompilerParams(has_side_effects=ORDERED_EFFECT),
    )(*operands)
    res = list(res)
    thru = res[n_start:n_start + na]
    return thru[:ns], thru[ns:], (tuple(res[:2]) if start else None), (res[-1] if start else None)


def _adamw_math(w, g, m, v):
    m = ADAM_B1 * m + (1.0 - ADAM_B1) * g
    v = ADAM_B2 * v + (1.0 - ADAM_B2) * (g * g)
    m_hat = m / (1.0 - ADAM_B1 ** ADAM_STEP)
    v_hat = v / (1.0 - ADAM_B2 ** ADAM_STEP)
    delta = -ADAM_LR * (m_hat / (jnp.sqrt(v_hat) + ADAM_EPS) + ADAM_WD * w)
    return delta, m, v


def _adamw(name, parts, w, m, v, tr=128):
    R, C = w.shape
    tr = _tile(R, tr)

    def body(p_ref, w_ref, m_ref, v_ref, g_out, d_out, m_out, v_out):
        g = p_ref[0].astype(F32)
        for d in range(1, N_DEV):
            g = g + p_ref[d].astype(F32)
        g_out[...] = g
        d_out[...], m_out[...], v_out[...] = _adamw_math(w_ref[...], g, m_ref[...], v_ref[...])

    row = pl.BlockSpec((tr, C), lambda i: (i, 0))
    out = jax.ShapeDtypeStruct((R, C), F32)
    return pl.pallas_call(
        body, name=name, grid=(R // tr,),
        in_specs=[pl.BlockSpec((N_DEV, tr, C), lambda i: (0, i, 0)), row, row, row],
        out_specs=[row] * 4, out_shape=[out] * 4,
        compiler_params=_params(dimension_semantics=("parallel",)),
    )(parts, w, m, v)


def _small_allreduce_adamw(name, g, w, m, v):
    R = g.shape[0]

    def body(g_ref, w_ref, m_ref, v_ref, g_out, d_out, m_out, v_out, land, send, recv):
        x, y, c = _position()
        me = _index((x, y, c))
        land[me] = g_ref[...]
        copies = []
        for k in range(1, N_DEV):
            peer = (x ^ (k >> 2), y ^ ((k >> 1) & 1), c ^ (k & 1))
            copies.append(pltpu.make_async_remote_copy(
                src_ref=g_ref, dst_ref=land.at[me], send_sem=send.at[k - 1], recv_sem=recv.at[k - 1],
                device_id=peer, device_id_type=MESH))
        for cp in copies:
            cp.start()
        for cp in copies:
            cp.wait()
        total = land[0]
        for d in range(1, N_DEV):
            total = total + land[d]
        g_out[...] = total
        d_out[...], m_out[...], v_out[...] = _adamw_math(w_ref[...], total, m_ref[...], v_ref[...])

    vm = pl.BlockSpec(memory_space=pltpu.VMEM)
    out = jax.ShapeDtypeStruct((R, 128), F32)
    return pl.pallas_call(
        body, name=name, in_specs=[vm] * 4, out_specs=[vm] * 4, out_shape=[out] * 4,
        scratch_shapes=[pltpu.VMEM((N_DEV, R, 128), F32), pltpu.SemaphoreType.DMA((7,)),
                        pltpu.SemaphoreType.DMA((7,))],
    )(g, w, m, v)


BIG = ("ffn1_w_gate", "ffn1_w_up", "ffn1_w_down", "w_in", "w_pool", "w_out", "w_cq", "w_ckv", "w_co",
       "ffn2_w_gate", "ffn2_w_up", "ffn2_w_down")
SMALL = ("ffn1_norm", "mix_norm", "rel_bias", "pool_scale", "cross_norm", "mem_norm", "ffn2_norm", "final_norm")
ORDER = ("ffn1_norm", "ffn1_w_gate", "ffn1_w_up", "ffn1_w_down", "mix_norm", "w_in", "rel_bias", "w_pool",
         "pool_scale", "w_out", "cross_norm", "mem_norm", "w_cq", "w_ckv", "w_co", "ffn2_norm", "ffn2_w_gate",
         "ffn2_w_up", "ffn2_w_down", "final_norm")
TRANSPOSED = ("ffn1_w_gate", "ffn1_w_up", "ffn2_w_gate", "ffn2_w_up")
ROW_SHARDED = TRANSPOSED + ("ffn1_w_down", "ffn2_w_down", "w_out", "w_cq", "w_ckv")
GATHER_GROUPS = (("ffn1_w_gate",), ("ffn1_w_up",), ("ffn1_w_down",), ("w_in", "w_pool", "w_out"),
                 ("w_cq", "w_ckv", "w_co"), ("ffn2_w_gate", "ffn2_w_up", "ffn2_w_down"))
RELAY_BEFORE_USE = ((0,), (1,), (2,), (3,), (4, 5), ())


def _pack(arrays):
    flat = jnp.concatenate([a.reshape(-1) for a in arrays])
    rows = -(-flat.shape[0] // 1024) * 8
    return jnp.pad(flat, (0, rows * 128 - flat.shape[0])).reshape(rows, 128)


def _unpack(packed, like):
    flat, out, at = packed.reshape(-1), [], 0
    for a in like:
        out.append(flat[at:at + a.size].reshape(a.shape))
        at += a.size
    return out


def _shard2d(a):
    a = a[0]
    return a.reshape(-1, a.shape[-1])


def kernel(x, mem, ffn1_norm, ffn1_w_gate, ffn1_w_up, ffn1_w_down, mix_norm, w_in, rel_bias, w_pool, pool_scale, w_out, cross_norm, mem_norm, w_cq, w_ckv, w_co, ffn2_norm, ffn2_w_gate, ffn2_w_up, ffn2_w_down, final_norm, loss_target, m_ffn1_norm, m_ffn1_w_gate, m_ffn1_w_up, m_ffn1_w_down, m_mix_norm, m_w_in, m_rel_bias, m_w_pool, m_pool_scale, m_w_out, m_cross_norm, m_mem_norm, m_w_cq, m_w_ckv, m_w_co, m_ffn2_norm, m_ffn2_w_gate, m_ffn2_w_up, m_ffn2_w_down, m_final_norm, v_ffn1_norm, v_ffn1_w_gate, v_ffn1_w_up, v_ffn1_w_down, v_mix_norm, v_w_in, v_rel_bias, v_w_pool, v_pool_scale, v_w_out, v_cross_norm, v_mem_norm, v_w_cq, v_w_ckv, v_w_co, v_ffn2_norm, v_ffn2_w_gate, v_ffn2_w_up, v_ffn2_w_down, v_final_norm):
    args = dict(locals())
    def view(n, a):
        return a.transpose(0, 2, 1) if n in TRANSPOSED else a

    w_in_ = {n: view(n, args[n]) for n in ORDER}
    m_in = {n: view(n, args["m_" + n]) for n in ORDER}
    v_in = {n: view(n, args["v_" + n]) for n in ORDER}

    me = 4 * lax.axis_index("x") + 2 * lax.axis_index("y") + lax.axis_index("c")
    n_g, rows = len(POOL_WINDOWS), POOL_GROUP // N_DEV

    def landing(block_shape, dtype):
        return lax.empty((N_DEV,) + tuple(block_shape), dtype)

    shards = [_shard2d(w_in_[n]).astype(BF16) for group in GATHER_GROUPS for n in group]
    srcs, lands, sems, tok = _comm("gather_start", shards, [landing(s.shape, BF16) for s in shards], start="spread")
    gathers, at = [], 0
    for group in GATHER_GROUPS:
        gathers.append((srcs[at:at + len(group)], lands[at:at + len(group)], sems + (at * N_SEMS["spread"],)))
        at += len(group)

    def weights(gi, after):
        for ri in RELAY_BEFORE_USE[gi]:
            srcs, lands, sems = gathers[ri]
            _, lands, sems, after = _comm("gather_relay_%d" % ri, srcs, lands, wait=("spread",) + sems,
                                          start="relay", after=after)
            gathers[ri] = (None, lands, sems)
        _, lands, sems = gathers[gi]
        _, lands, _, _ = _comm("gather_finish_%d" % gi, [], lands, wait=("relay",) + sems, after=after)
        out = {}
        for n, full in zip(GATHER_GROUPS[gi], lands):
            if n == "w_pool":
                full = full.reshape(N_DEV, n_g, rows, POOL_GROUP).transpose(1, 0, 2, 3).reshape(n_g, POOL_GROUP, POOL_GROUP)
            out[n] = full.reshape(-1, full.shape[-1]) if n in ROW_SHARDED else full
        return out

    scatters = []

    def emit(gw):
        names = list(gw)
        stacks = []
        for n in names:
            g = gw[n]
            if n == "w_pool":
                g = g.reshape(n_g, N_DEV, rows, POOL_GROUP).transpose(1, 0, 2, 3).astype(BF16)
            stacks.append(g.reshape((N_DEV,) + _shard2d(w_in_[n]).shape))
        lands = [landing(s.shape[1:], s.dtype) for s in stacks]
        srcs, lands, sems, token = _comm("grads_start_%d" % len(scatters), stacks, lands, start="grads")
        scatters.append((names, srcs, lands, sems))
        return token

    small = {n: w_in_[n].reshape(1, -1) for n in SMALL if n != "rel_bias"}
    small["rel_bias"] = rel_bias[0]
    loss_part, grad_x, gs = _local_step(x, mem, loss_target, small, weights, emit, start_token=tok)
    loss = lax.psum(loss_part[0, 0], ("x", "y", "c"))

    grad, delta, new_m, new_v = {}, {}, {}, {}
    after = grad_x
    for si, (names, srcs, lands, sems) in enumerate(scatters):
        _, landed, _, _ = _comm("grads_finish_%d" % si, srcs, lands, wait=("grads",) + sems, after=after)
        for n, parts in zip(names, landed):
            res = _adamw("adamw_" + n, parts, _shard2d(w_in_[n]), _shard2d(m_in[n]), _shard2d(v_in[n]))
            grad[n], delta[n], new_m[n], new_v[n] = [view(n, r.reshape(w_in_[n].shape)) for r in res]
        after = res[0]

    like = [w_in_[n] for n in SMALL]
    gs["rel_bias"] = gs["rel_bias"].reshape(rel_bias.shape)
    res = _small_allreduce_adamw("small_params", _pack([gs[n] for n in SMALL]), _pack(like),
                                 _pack([m_in[n] for n in SMALL]), _pack([v_in[n] for n in SMALL]))
    for d, packed in zip((grad, delta, new_m, new_v), res):
        for n, a in zip(SMALL, _unpack(packed, like)):
            d[n] = a
    return (loss, grad_x, *[grad[n] for n in ORDER], *[delta[n] for n in ORDER],
            *[new_m[n] for n in ORDER], *[new_v[n] for n in ORDER])
```

```python
import functools

import jax
import jax.numpy as jnp
from jax import lax
from jax.experimental import pallas as pl
from jax.experimental.pallas import tpu as pltpu

F32 = jnp.float32
BF16 = jnp.bfloat16

N_DEV = 8
EPS = 1e-6
NEG_INF = -1e30
CHUNK = 64
LEFT_CHUNKS = 8
PAD = LEFT_CHUNKS * CHUNK
QBLK = 4 * CHUNK
KBAND = PAD + QBLK
REL_CLIP = 128
ATTN_HEADS = 16
HEAD_DIM = 64
D_ATTN = ATTN_HEADS * HEAD_DIM
POOL_WINDOWS = (2, 4, 8, 16)
POOL_GROUP = 256
D_POOL = len(POOL_WINDOWS) * POOL_GROUP
CROSS_HEADS = 4
CROSS_DIM = 128
D_CROSS = CROSS_HEADS * CROSS_DIM
FFN_RES = 0.5
ADAM_LR, ADAM_B1, ADAM_B2, ADAM_EPS, ADAM_WD, ADAM_STEP = 0.001, 0.9, 0.999, 1e-08, 0.01, 10

NN = (((1,), (0,)), ((), ()))
NT = (((1,), (1,)), ((), ()))
TN = (((0,), (0,)), ((), ()))
MESH = pl.DeviceIdType.MESH
VMEM_LIMIT = 56 * 1024 * 1024


def _params(**kw):
    return pltpu.CompilerParams(vmem_limit_bytes=VMEM_LIMIT, **kw)


def _bf(v):
    return v if v.dtype == BF16 else v.astype(BF16)


WHOLE = ((Ellipsis,), (Ellipsis,))


def _rms(xv, gain):
    return (xv * lax.rsqrt(jnp.mean(xv * xv, axis=-1, keepdims=True) + EPS)) * gain


def _gemm(name, a, a_spec, b, b_spec, dims, grid, outs, chunks=(WHOLE,), extras=(), epilogue=None, after=None,
          norm=None):
    nex, nout = len(extras), len(outs)
    first_out = 2 + nex + (after is not None) + (norm is not None)

    def body(*refs):
        a_ref, b_ref = refs[:2]
        if norm is not None:
            hn_out, a_ref = refs[first_out + nout], refs[-1]

            @pl.when(pl.program_id(1) == 0)
            def _():
                hn = _rms(refs[0][...], refs[first_out - 1][...]).astype(BF16)
                a_ref[...] = hn
                hn_out[...] = hn

        total = None
        for ia, ib in chunks:
            d = lax.dot_general(_bf(a_ref[ia]), _bf(b_ref[ib]), dims, preferred_element_type=F32)
            total = d if total is None else total + d
        vals = epilogue(total, *[e[...] for e in refs[2:2 + nex]]) if epilogue is not None else (total,)
        for r, v in zip(refs[first_out:first_out + nout], vals):
            r[...] = v.astype(r.dtype)

    operands = [a, b] + [x for x, _, _ in extras]
    in_specs = [pl.BlockSpec(*a_spec), pl.BlockSpec(*b_spec)] + [pl.BlockSpec(blk, m) for _, blk, m in extras]
    if after is not None:
        operands.append(after)
        in_specs.append(pl.BlockSpec(after.shape, lambda i, j: (0, 0)))
    out_specs = [pl.BlockSpec(blk, m) for _, _, blk, m in outs]
    out_shape = [jax.ShapeDtypeStruct(s, d) for s, d, _, _ in outs]
    scratch = []
    if norm is not None:
        operands.append(norm)
        in_specs.append(pl.BlockSpec(norm.shape, lambda i, j: (0, 0)))
        out_specs.append(pl.BlockSpec(*a_spec))
        out_shape.append(jax.ShapeDtypeStruct(a.shape, BF16))
        scratch.append(pltpu.VMEM(a_spec[0], BF16))
    res = pl.pallas_call(
        body, name=name, grid=grid, in_specs=in_specs, out_specs=out_specs, out_shape=out_shape,
        scratch_shapes=scratch,
        compiler_params=_params(dimension_semantics=("parallel", "arbitrary" if norm is not None else "parallel")),
    )(*operands)
    return res[0] if len(res) == 1 else res


def _tile(n, want):
    for t in range(min(n, want), 15, -1):
        if n % t == 0 and t % 16 == 0:
            return t
    return n


def _mm_nn(name, a, b, out_dtype, res=None, tm=1024, tn=1024, norm=None):
    M, K = a.shape
    N = b.shape[1]
    tm, tn = _tile(M, tm), _tile(N, tn)
    extras = [] if res is None else [(res, (tm, tn), lambda i, j: (i, j))]
    epi = None if res is None else (lambda t, r: (r + t,))
    return _gemm(name, a, ((tm, K), lambda i, j: (i, 0)), b, ((K, tn), lambda i, j: (0, j)), NN,
                 (M // tm, N // tn), [((M, N), out_dtype, (tm, tn), lambda i, j: (i, j))], extras=extras, epilogue=epi,
                 norm=norm)


def _mm_nn_cols(name, a, bs, out_dtype, res=None, tm=1024, norm=None):
    M, K = a.shape
    nb, _, w = bs.shape
    tm = _tile(M, tm)
    extras = [] if res is None else [(res, (tm, w), lambda i, j: (i, j))]
    epi = None if res is None else (lambda t, r: (r + t,))
    return _gemm(name, a, ((tm, K), lambda i, j: (i, 0)), bs, ((None, K, w), lambda i, j: (j, 0, 0)), NN,
                 (M // tm, nb), [((M, nb * w), out_dtype, (tm, w), lambda i, j: (i, j))], extras=extras, epilogue=epi,
                 norm=norm)


def _mm_nt(name, a, b, out_dtype, tm=1024, tn=1024, after=None):
    M, K = a.shape
    N = b.shape[0]
    tm, tn = _tile(M, tm), _tile(N, tn)
    return _gemm(name, a, ((tm, K), lambda i, j: (i, 0)), b, ((tn, K), lambda i, j: (j, 0)), NT,
                 (M // tm, N // tn), [((M, N), out_dtype, (tm, tn), lambda i, j: (i, j))], after=after)


def _mm_nt_cols(name, a, bs, out_dtype, tm=1024, tn=512, after=None):
    M = a.shape[0]
    nb, N, w = bs.shape
    tm, tn = _tile(M, tm), _tile(N, tn)
    chunks = [((slice(None), pl.ds(c * w, w)), (c,)) for c in range(nb)]
    return _gemm(name, a, ((tm, nb * w), lambda i, j: (i, 0)), bs, ((nb, tn, w), lambda i, j: (0, j, 0)), NT,
                 (M // tm, N // tn), [((M, N), out_dtype, (tm, tn), lambda i, j: (i, j))], chunks, after=after)


def _mm_tn(name, a, b, tm=1024, tn=1024, col_blocks=None):
    T, Ka = a.shape
    Nb = b.shape[1]
    tm = _tile(Ka, tm)
    if col_blocks is None:
        tn = _tile(Nb, tn)
        out = ((Ka, Nb), BF16, (tm, tn), lambda i, j: (i, j))
    else:
        tn = Nb // col_blocks
        out = ((col_blocks, Ka, tn), BF16, (None, tm, tn), lambda i, j: (j, i, 0))
    return _gemm(name, a, ((T, tm), lambda i, j: (0, i)), b, ((T, tn), lambda i, j: (0, j)), TN,
                 (Ka // tm, Nb // tn), [out])


def _ffn_out(name, a, wt, res=None, scale=1.0, tm=1024, tn=512, after=None):
    nb, M, w = a.shape
    N = wt.shape[1]
    tm, tn = _tile(M, tm), _tile(N, tn)
    chunks = [((c,), (pl.ds(c * w, w),)) for c in range(nb)]
    extras = [] if res is None else [(res, (tm, tn), lambda i, j: (i, j))]
    epi = None if res is None else (lambda t, r: (r + scale * t,))
    return _gemm(name, a, ((nb, tm, w), lambda i, j: (0, i, 0)), wt, ((nb * w, tn), lambda i, j: (0, j)), NN,
                 (M // tm, N // tn), [((M, N), F32, (tm, tn), lambda i, j: (i, j))], chunks, extras, epi, after)


def _ffn_dact(name, dhb, wd, g, u, tm=1024, after=None):
    M, K = dhb.shape
    nb, _, w = g.shape
    tm = _tile(M, tm)
    tr = _tile(tm, 256)
    tokens = [] if after is None else [after]

    def body(dh_ref, wd_ref, g_ref, u_ref, *rest):
        dg_ref, du_ref = rest[-2:]
        pieces = [pl.ds(r * tr, tr) for r in range(tm // tr)]

        def product(rows):
            return lax.dot_general(dh_ref[rows, :], wd_ref[...], NT, preferred_element_type=F32)

        results = []
        dact = product(pieces[0])
        for n, rows in enumerate(pieces):
            ahead = product(pieces[n + 1]) if n + 1 < len(pieces) else None
            results.append(_swiglu_bwd(dact, g_ref[rows, :], u_ref[rows, :]))
            dact = ahead
        for rows, (dg, du) in zip(pieces, results):
            dg_ref[rows, :] = dg.astype(BF16)
            du_ref[rows, :] = du.astype(BF16)

    hid = pl.BlockSpec((None, tm, w), lambda i, j: (j, i, 0))
    return pl.pallas_call(
        body, name=name, grid=(M // tm, nb),
        in_specs=[pl.BlockSpec((tm, K), lambda i, j: (i, 0)), pl.BlockSpec((w, K), lambda i, j: (j, 0)), hid, hid]
        + [pl.BlockSpec(t.shape, lambda i, j: (0, 0)) for t in tokens],
        out_specs=[hid, hid], out_shape=[jax.ShapeDtypeStruct((nb, M, w), BF16)] * 2,
        compiler_params=_params(dimension_semantics=("parallel", "parallel")),
    )(dhb, wd, g, u, *tokens)


def _ffn_dw(name, a, b, scale=1.0, tn=1024, after=None):
    nb, T, w = a.shape
    N = b.shape[1]
    tn = _tile(N, tn)
    epi = None if scale == 1.0 else (lambda t: (t * scale,))
    return _gemm(name, a, ((None, T, w), lambda i, j: (i, 0, 0)), b, ((T, tn), lambda i, j: (0, j)), TN,
                 (nb, N // tn), [((nb * w, N), BF16, (w, tn), lambda i, j: (i, j))], epilogue=epi, after=after)


def _ffn_gate(name, hn, wgt, nb, tm=1024):
    M, K = hn.shape
    w = wgt.shape[0] // nb
    tm = _tile(M, tm)
    return _gemm(name, hn, ((tm, K), lambda i, j: (i, 0)), wgt, ((w, K), lambda i, j: (j, 0)), NT,
                 (M // tm, nb), [((nb, M, w), BF16, (None, tm, w), lambda i, j: (j, i, 0))])


def _ffn_up_act(name, hn, wut, g, tm=1024):
    M, K = hn.shape
    nb, _, w = g.shape
    tm = _tile(M, tm)
    hid = ((None, tm, w), lambda i, j: (j, i, 0))

    def epilogue(u, gate):
        gate = gate.astype(F32)
        return u, gate * jax.nn.sigmoid(gate) * u

    return _gemm(name, hn, ((tm, K), lambda i, j: (i, 0)), wut, ((w, K), lambda i, j: (j, 0)), NT,
                 (M // tm, nb), [((nb, M, w), BF16) + hid] * 2, extras=[(g,) + hid], epilogue=epilogue)


def _ffn_up(name, h, gain, wgt, wut, nb, tm=1024):
    M, K = h.shape
    w = wgt.shape[0] // nb
    tm = _tile(M, tm)

    def body(h_ref, gain_ref, g_ref, u_ref, hn_ref, og, ou, oa, a_ref):
        @pl.when(pl.program_id(1) == 0)
        def _():
            hn = _rms(h_ref[...], gain_ref[...]).astype(BF16)
            a_ref[...] = hn
            hn_ref[...] = hn

        a = a_ref[...]
        g = lax.dot_general(a, g_ref[...], NT, preferred_element_type=F32)
        u = lax.dot_general(a, u_ref[...], NT, preferred_element_type=F32)
        og[...] = g.astype(BF16)
        ou[...] = u.astype(BF16)
        oa[...] = (g * jax.nn.sigmoid(g) * u).astype(BF16)

    rows = pl.BlockSpec((tm, K), lambda i, j: (i, 0))
    wspec = pl.BlockSpec((w, K), lambda i, j: (j, 0))
    ospec = pl.BlockSpec((None, tm, w), lambda i, j: (j, i, 0))
    return pl.pallas_call(
        body, name=name, grid=(M // tm, nb),
        in_specs=[rows, pl.BlockSpec((1, K), lambda i, j: (0, 0)), wspec, wspec],
        out_specs=[rows] + [ospec] * 3,
        out_shape=[jax.ShapeDtypeStruct((M, K), BF16)] + [jax.ShapeDtypeStruct((nb, M, w), BF16)] * 3,
        scratch_shapes=[pltpu.VMEM((tm, K), BF16)],
        compiler_params=_params(dimension_semantics=("parallel", "arbitrary")),
    )(h, gain, wgt, wut)


def _swiglu_bwd(dact, g, u):
    g = g.astype(F32)
    u = u.astype(F32)
    sig = jax.nn.sigmoid(g)
    silu = g * sig
    d = FFN_RES * dact
    return d * u * (sig * (1.0 + g * (1.0 - sig))), d * silu


def _rms_fwd(name, x, gain, tr=512, after=None):
    R, D = x.shape
    tr = _tile(R, tr)

    def body(x_ref, g_ref, *rest):
        xv = x_ref[...]
        y = xv * lax.rsqrt(jnp.mean(xv * xv, axis=-1, keepdims=True) + EPS)
        rest[-1][...] = (y * g_ref[...]).astype(BF16)

    tokens = [] if after is None else [after]
    return pl.pallas_call(
        body, name=name, grid=(R // tr,),
        in_specs=[pl.BlockSpec((tr, D), lambda i: (i, 0)), pl.BlockSpec((1, D), lambda i: (0, 0))]
        + [pl.BlockSpec(t.shape, lambda i: (0, 0)) for t in tokens],
        out_specs=pl.BlockSpec((tr, D), lambda i: (i, 0)), out_shape=jax.ShapeDtypeStruct((R, D), BF16),
        compiler_params=_params(dimension_semantics=("parallel",)),
    )(x, gain, *tokens)


def _rms_bwd_math(xv, gain, dy):
    rstd = lax.rsqrt(jnp.mean(xv * xv, axis=-1, keepdims=True) + EPS)
    xhat = xv * rstd
    dxh = dy * gain
    dx = rstd * (dxh - xhat * jnp.mean(dxh * xhat, axis=-1, keepdims=True))
    return dx, jnp.sum(dy * xhat, axis=0, keepdims=True)


def _rms_bwd(name, x, gain, dy, skip=None, tr=256):
    R, D = x.shape
    tr = _tile(R, tr)
    has_skip = skip is not None

    def body(*refs):
        x_ref, g_ref, dy_ref = refs[:3]
        dx_ref, dxb_ref, dg_ref = refs[-3:]
        dx, dg = _rms_bwd_math(x_ref[...], g_ref[...], dy_ref[...].astype(F32))
        if has_skip:
            dx = dx + refs[3][...]
        dx_ref[...] = dx
        dxb_ref[...] = dx.astype(BF16)

        @pl.when(pl.program_id(0) == 0)
        def _():
            dg_ref[...] = dg

        @pl.when(pl.program_id(0) > 0)
        def _():
            dg_ref[...] += dg

    row = pl.BlockSpec((tr, D), lambda i: (i, 0))
    vec = pl.BlockSpec((1, D), lambda i: (0, 0))
    return pl.pallas_call(
        body, name=name, grid=(R // tr,),
        in_specs=[row, vec, row] + ([row] if has_skip else []),
        out_specs=[row, row, vec],
        out_shape=[jax.ShapeDtypeStruct((R, D), F32), jax.ShapeDtypeStruct((R, D), BF16),
                   jax.ShapeDtypeStruct((1, D), F32)],
        compiler_params=_params(dimension_semantics=("arbitrary",)),
    )(*([x, gain, dy] + ([skip] if has_skip else [])))


def _loss_and_grad(name, h, gain, target, tr=256):
    R, D = h.shape
    tr = _tile(R, tr)

    def body(h_ref, g_ref, t_ref, loss_ref, dh_ref, dhb_ref, dg_ref):
        hv, gain_v = h_ref[...], g_ref[...]
        y = (hv * lax.rsqrt(jnp.mean(hv * hv, axis=-1, keepdims=True) + EPS)) * gain_v
        err = y - t_ref[...]
        part = jnp.full((8, 128), 0.5 * jnp.sum(jnp.mean(err * err, axis=-1, keepdims=True)), F32)
        dh, dg = _rms_bwd_math(hv, gain_v, err * (1.0 / D))
        dh_ref[...] = dh
        dhb_ref[...] = dh.astype(BF16)

        @pl.when(pl.program_id(0) == 0)
        def _():
            dg_ref[...] = dg
            loss_ref[...] = part

        @pl.when(pl.program_id(0) > 0)
        def _():
            dg_ref[...] += dg
            loss_ref[...] += part

    row = pl.BlockSpec((tr, D), lambda i: (i, 0))
    vec = pl.BlockSpec((1, D), lambda i: (0, 0))
    return pl.pallas_call(
        body, name=name, grid=(R // tr,), in_specs=[row, vec, row],
        out_specs=[pl.BlockSpec((8, 128), lambda i: (0, 0)), row, row, vec],
        out_shape=[jax.ShapeDtypeStruct((8, 128), F32), jax.ShapeDtypeStruct((R, D), F32),
                   jax.ShapeDtypeStruct((R, D), BF16), jax.ShapeDtypeStruct((1, D), F32)],
        compiler_params=_params(dimension_semantics=("arbitrary",)),
    )(h, gain, target)


def _bias_tile(name, rel):
    width = KBAND + QBLK
    sat = rel[:, 2 * REL_CLIP:]
    n_left = PAD - REL_CLIP + 1
    row0 = jnp.concatenate([jnp.broadcast_to(sat, (ATTN_HEADS, n_left)), rel[:, :2 * REL_CLIP][:, ::-1],
                            jnp.broadcast_to(sat, (ATTN_HEADS, width - n_left - 2 * REL_CLIP))], axis=1)

    def body(e_ref, o_ref):
        rows = pltpu.roll(jnp.broadcast_to(e_ref[...], (QBLK, width)), 0, 1, stride=1, stride_axis=0)
        i = lax.broadcasted_iota(jnp.int32, (QBLK, KBAND), 0) // CHUNK
        j = lax.broadcasted_iota(jnp.int32, (QBLK, KBAND), 1) // CHUNK
        o_ref[...] = jnp.where((j >= i) & (j <= i + LEFT_CHUNKS), rows[:, :KBAND], NEG_INF)

    return pl.pallas_call(
        body, name=name, grid=(ATTN_HEADS,),
        in_specs=[pl.BlockSpec((None, 1, width), lambda h: (h, 0, 0))],
        out_specs=pl.BlockSpec((None, QBLK, KBAND), lambda h: (h, 0, 0)),
        out_shape=jax.ShapeDtypeStruct((ATTN_HEADS, QBLK, KBAND), F32),
        compiler_params=_params(dimension_semantics=("parallel",)),
    )(row0.reshape(ATTN_HEADS, 1, width))


ATTN_SCALE = HEAD_DIM ** -0.5
ROW_PIECES = 1


def _stack_heads(x, first):
    zero = jnp.zeros_like(x)
    return jnp.concatenate([jnp.where(first, x, zero), jnp.where(first, zero, x)], axis=0)


def _band_softmax(q_half_scaled, kb, bias, left_mask):
    s = lax.dot_general(q_half_scaled, kb, NT, preferred_element_type=F32) + bias + left_mask
    e = jnp.exp(s - jnp.max(s, axis=-1, keepdims=True))
    return e * (1.0 / jnp.sum(e, axis=-1, keepdims=True))


def _left_mask(qb):
    kpos = qb * QBLK - PAD + lax.broadcasted_iota(jnp.int32, (1, KBAND), 1)
    return jnp.where(kpos >= 0, 0.0, NEG_INF).astype(F32)


def _fill_padded(dst, src, S):
    dst[pl.ds(0, PAD), :] = jnp.zeros((PAD, dst.shape[1]), dst.dtype)
    dst[pl.ds(PAD, S), :] = src[...].astype(dst.dtype)


def _attn_fwd(name, z, bias):
    B, S, _ = z.shape
    nh2 = ATTN_HEADS // 2

    def body(q_ref, k_ref, v_ref, b_ref, o_ref, kp, vp):
        qb = pl.program_id(2)

        @pl.when(qb == 0)
        def _():
            _fill_padded(kp, k_ref, S)
            _fill_padded(vp, v_ref, S)

        start = pl.multiple_of(qb * QBLK, QBLK)
        kb, vb = kp[pl.ds(start, KBAND), :], vp[pl.ds(start, KBAND), :]
        q = (q_ref[...] * ATTN_SCALE).astype(BF16)
        first = lax.broadcasted_iota(jnp.int32, (QBLK, 2 * HEAD_DIM), 1) < HEAD_DIM
        left = _left_mask(qb)
        zero = jnp.zeros_like(q)
        qh = [jnp.where(first, q, zero), jnp.where(first, zero, q)]
        rp = QBLK // ROW_PIECES
        chains = [(a, r) for r in range(ROW_PIECES) for a in range(2)]
        ss = [lax.dot_general(qh[a][r * rp:(r + 1) * rp], kb, NT, preferred_element_type=F32) for a, r in chains]
        ps = []
        for (a, r), s in zip(chains, ss):
            s = s + b_ref[a, pl.ds(r * rp, rp), :] + left
            e = jnp.exp(s - jnp.max(s, axis=-1, keepdims=True))
            ps.append((e * (1.0 / jnp.sum(e, axis=-1, keepdims=True))).astype(BF16))
        os_ = [jnp.dot(p, vb, preferred_element_type=F32) for p in ps]
        for r in range(ROW_PIECES):
            o_ref[pl.ds(r * rp, rp), :] = jnp.where(first[:rp], os_[2 * r], os_[2 * r + 1]).astype(BF16)

    return pl.pallas_call(
        body, name=name, grid=(B, nh2, S // QBLK),
        in_specs=[pl.BlockSpec((None, QBLK, 128), lambda b, h, i: (b, i, h)),
                  pl.BlockSpec((None, S, 128), lambda b, h, i: (b, 0, nh2 + h)),
                  pl.BlockSpec((None, S, 128), lambda b, h, i: (b, 0, 2 * nh2 + h)),
                  pl.BlockSpec((2, QBLK, KBAND), lambda b, h, i: (h, 0, 0))],
        out_specs=pl.BlockSpec((None, QBLK, 128), lambda b, h, i: (b, i, h)),
        out_shape=jax.ShapeDtypeStruct((B, S, D_ATTN + D_POOL), BF16),
        scratch_shapes=[pltpu.VMEM((PAD + S, 128), BF16), pltpu.VMEM((PAD + S, 128), BF16)],
        compiler_params=_params(dimension_semantics=("parallel", "parallel", "arbitrary")),
    )(z, z, z, bias)


def _attn_bwd(name, z, bias, dcat):
    B, S, _ = z.shape
    nh2 = ATTN_HEADS // 2
    nqb = S // QBLK

    def body(q_ref, k_ref, v_ref, b_ref, do_ref, dq_ref, dk_ref, dv_ref, db_ref, kp, vp, dka, dva):
        b, qb = pl.program_id(1), pl.program_id(2)

        @pl.when(qb == 0)
        def _():
            _fill_padded(kp, k_ref, S)
            _fill_padded(vp, v_ref, S)
            dka[...] = jnp.zeros_like(dka)
            dva[...] = jnp.zeros_like(dva)

        @pl.when((qb == 0) & (b == 0))
        def _():
            db_ref[...] = jnp.zeros_like(db_ref)

        start = pl.multiple_of(qb * QBLK, QBLK)
        band = pl.ds(start, KBAND)
        kb, vb = kp[band, :], vp[band, :]
        q = (q_ref[...] * ATTN_SCALE).astype(BF16)
        do = do_ref[...]
        first = lax.broadcasted_iota(jnp.int32, (QBLK, 2 * HEAD_DIM), 1) < HEAD_DIM
        left = _left_mask(qb)
        q2, do2 = _stack_heads(q, first), _stack_heads(do, first)
        p = _band_softmax(q2, kb, b_ref[...].reshape(2 * QBLK, KBAND), left)
        dp = lax.dot_general(do2, vb, NT, preferred_element_type=F32)
        ds = p * (dp - jnp.sum(p * dp, axis=-1, keepdims=True))
        db_ref[...] += ds.reshape(2, QBLK, KBAND)
        dsb = ds.astype(BF16)
        dq = jnp.dot(dsb, kb, preferred_element_type=F32)
        dq_ref[...] = (jnp.where(first, dq[:QBLK], dq[QBLK:]) * ATTN_SCALE).astype(BF16)
        dka[band, :] += lax.dot_general(dsb, q2, TN, preferred_element_type=F32)
        dva[band, :] += lax.dot_general(p.astype(BF16), do2, TN, preferred_element_type=F32)

        @pl.when(qb == nqb - 1)
        def _():
            dk_ref[...] = dka[pl.ds(PAD, S), :].astype(BF16)
            dv_ref[...] = dva[pl.ds(PAD, S), :].astype(BF16)

    qspec = pl.BlockSpec((None, QBLK, 128), lambda h, b, i: (b, i, h))
    kvout = pl.BlockSpec((None, S, 128), lambda h, b, i: (b, 0, h))
    bspec = pl.BlockSpec((2, QBLK, KBAND), lambda h, b, i: (h, 0, 0))
    act = jax.ShapeDtypeStruct((B, S, D_ATTN), BF16)
    return pl.pallas_call(
        body, name=name, grid=(nh2, B, nqb),
        in_specs=[qspec,
                  pl.BlockSpec((None, S, 128), lambda h, b, i: (b, 0, nh2 + h)),
                  pl.BlockSpec((None, S, 128), lambda h, b, i: (b, 0, 2 * nh2 + h)),
                  bspec, qspec],
        out_specs=[qspec, kvout, kvout, bspec],
        out_shape=[act, act, act, jax.ShapeDtypeStruct((ATTN_HEADS, QBLK, KBAND), F32)],
        scratch_shapes=[pltpu.VMEM((PAD + S, 128), BF16), pltpu.VMEM((PAD + S, 128), BF16),
                        pltpu.VMEM((PAD + S, 128), F32), pltpu.VMEM((PAD + S, 128), F32)],
        compiler_params=_params(dimension_semantics=("arbitrary", "arbitrary", "arbitrary")),
    )(z, z, z, bias, dcat)


def _bias_grad(name, dbias):
    width = KBAND + QBLK

    def body(d_ref, o_ref):
        acc = jnp.zeros((1, width), F32)
        for i in range(QBLK):
            row = jnp.concatenate([d_ref[pl.ds(i, 1), :], jnp.zeros((1, QBLK), F32)], axis=1)
            shift = QBLK - 1 - i
            acc = acc + (pltpu.roll(row, shift, 1) if shift else row)
        o_ref[...] = acc

    return pl.pallas_call(
        body, name=name, grid=(ATTN_HEADS,),
        in_specs=[pl.BlockSpec((None, QBLK, KBAND), lambda h: (h, 0, 0))],
        out_specs=pl.BlockSpec((None, 1, width), lambda h: (h, 0, 0)),
        out_shape=jax.ShapeDtypeStruct((ATTN_HEADS, 1, width), F32),
        compiler_params=_params(dimension_semantics=("parallel",)),
    )(dbias)


def _rel_grad_from_diagonals(diag):
    top = PAD + QBLK - 1 - REL_CLIP
    sat = jnp.sum(diag[:, :top + 1], axis=1, keepdims=True)
    mid = diag[:, top + 1:top + 2 * REL_CLIP][:, ::-1]
    return jnp.concatenate([jnp.zeros_like(sat), mid, sat], axis=1)


def _shift_rows(x, k, forward):
    S = x.shape[0]
    t = lax.broadcasted_iota(jnp.int32, x.shape, 0)
    if forward:
        return jnp.where(t < S - k, pltpu.roll(x, S - k, 0), 0.0)
    return jnp.where(t >= k, pltpu.roll(x, k, 0), 0.0)


def _window_sum(x, g, forward):
    s = x + _shift_rows(x, 1, forward)
    out = s
    for n, k in enumerate((2, 4, 8)):
        s = s + _shift_rows(s, k, forward)
        out = jnp.where(g > n, s, out)
    return out


def _pool_count(S, g):
    t = lax.broadcasted_iota(jnp.int32, (S, 1), 0)
    w = jnp.left_shift(2, g)
    return jnp.minimum(t + 1, w).astype(F32)


def _pool_fwd(name, z, wp, pscale, mixed):
    B, S, _ = z.shape
    c0 = 3 * D_ATTN // POOL_GROUP
    y0 = D_ATTN // POOL_GROUP

    def body(u_ref, w_ref, s_ref, mixed_ref, d_ref, y_ref):
        g = pl.program_id(1)
        u = u_ref[...]
        d = (_window_sum(u, g, False) / _pool_count(S, g) - u).astype(BF16)
        d_ref[...] = d
        y_ref[...] = (jnp.dot(d, w_ref[...], preferred_element_type=F32) * s_ref[...]).astype(BF16)

    return pl.pallas_call(
        body, name=name, grid=(B, len(POOL_WINDOWS)),
        in_specs=[pl.BlockSpec((None, S, POOL_GROUP), lambda b, g: (b, 0, c0 + g)),
                  pl.BlockSpec((None, POOL_GROUP, POOL_GROUP), lambda b, g: (g, 0, 0)),
                  pl.BlockSpec((1, POOL_GROUP), lambda b, g: (0, g)),
                  pl.BlockSpec(memory_space=pl.ANY)],
        out_specs=[pl.BlockSpec((None, S, POOL_GROUP), lambda b, g: (b, 0, g)),
                   pl.BlockSpec((None, S, POOL_GROUP), lambda b, g: (b, 0, y0 + g))],
        out_shape=[jax.ShapeDtypeStruct((B, S, D_POOL), BF16), jax.ShapeDtypeStruct(mixed.shape, BF16)],
        input_output_aliases={3: 1},
        compiler_params=_params(dimension_semantics=("parallel", "parallel")),
    )(z, wp, pscale, mixed)


def _pool_bwd(name, d, wp, pscale, dcat):
    B, S, _ = d.shape
    c0 = D_ATTN // POOL_GROUP

    def body(d_ref, w_ref, s_ref, dy_ref, du_ref, dw_ref, dsc_ref):
        g, b = pl.program_id(0), pl.program_id(1)
        dv = d_ref[...]
        dy = dy_ref[...].astype(F32)
        w = w_ref[...]
        ypre = jnp.dot(dv, w, preferred_element_type=F32)
        dyp = (dy * s_ref[...]).astype(BF16)
        dd = lax.dot_general(dyp, w, NT, preferred_element_type=F32)
        du_ref[...] = (_window_sum(dd / _pool_count(S, g), g, True) - dd).astype(BF16)
        dw = lax.dot_general(dv, dyp, TN, preferred_element_type=F32)
        dsc = jnp.sum(dy * ypre, axis=0, keepdims=True)

        @pl.when(b == 0)
        def _():
            dw_ref[...] = dw
            dsc_ref[...] = dsc

        @pl.when(b > 0)
        def _():
            dw_ref[...] += dw
            dsc_ref[...] += dsc

    blk = pl.BlockSpec((None, S, POOL_GROUP), lambda g, b: (b, 0, g))
    wspec = pl.BlockSpec((None, POOL_GROUP, POOL_GROUP), lambda g, b: (g, 0, 0))
    sspec = pl.BlockSpec((1, POOL_GROUP), lambda g, b: (0, g))
    return pl.pallas_call(
        body, name=name, grid=(len(POOL_WINDOWS), B),
        in_specs=[blk, wspec, sspec, pl.BlockSpec((None, S, POOL_GROUP), lambda g, b: (b, 0, c0 + g))],
        out_specs=[blk, wspec, sspec],
        out_shape=[jax.ShapeDtypeStruct((B, S, D_POOL), BF16),
                   jax.ShapeDtypeStruct((len(POOL_WINDOWS), POOL_GROUP, POOL_GROUP), F32),
                   jax.ShapeDtypeStruct((1, D_POOL), F32)],
        compiler_params=_params(dimension_semantics=("arbitrary", "arbitrary")),
    )(d, wp, pscale, dcat)


def _cross_softmax(q, k):
    s = lax.dot_general(q, k, NT, preferred_element_type=F32) * (CROSS_DIM ** -0.5)
    e = jnp.exp(s - jnp.max(s, axis=-1, keepdims=True))
    return e * (1.0 / jnp.sum(e, axis=-1, keepdims=True))


def _cross_fwd(name, qc, kv, tq=512):
    B, S, _ = qc.shape
    M = kv.shape[1]
    tq = _tile(S, tq)

    def body(q_ref, k_ref, v_ref, o_ref):
        p = _cross_softmax(q_ref[...], k_ref[...])
        o_ref[...] = jnp.dot(p.astype(BF16), v_ref[...], preferred_element_type=F32).astype(BF16)

    qspec = pl.BlockSpec((None, tq, CROSS_DIM), lambda b, h, i: (b, i, h))
    return pl.pallas_call(
        body, name=name, grid=(B, CROSS_HEADS, S // tq),
        in_specs=[qspec, pl.BlockSpec((None, M, CROSS_DIM), lambda b, h, i: (b, 0, h)),
                  pl.BlockSpec((None, M, CROSS_DIM), lambda b, h, i: (b, 0, CROSS_HEADS + h))],
        out_specs=qspec, out_shape=jax.ShapeDtypeStruct((B, S, D_CROSS), BF16),
        compiler_params=_params(dimension_semantics=("parallel", "parallel", "parallel")),
    )(qc, kv, kv)


def _cross_bwd(name, qc, kv, do, tq=512):
    B, S, _ = qc.shape
    M = kv.shape[1]
    tq = _tile(S, tq)
    nq = S // tq
    scale = CROSS_DIM ** -0.5

    def body(q_ref, k_ref, v_ref, do_ref, dq_ref, dk_ref, dv_ref, dka, dva):
        i = pl.program_id(2)
        q, k, v, dov = q_ref[...], k_ref[...], v_ref[...], do_ref[...]
        p = _cross_softmax(q, k)
        dp = lax.dot_general(dov, v, NT, preferred_element_type=F32)
        ds = ((p * (dp - jnp.sum(p * dp, axis=-1, keepdims=True))) * scale).astype(BF16)
        dq_ref[...] = jnp.dot(ds, k, preferred_element_type=F32).astype(BF16)
        dk = lax.dot_general(ds, q, TN, preferred_element_type=F32)
        dv = lax.dot_general(p.astype(BF16), dov, TN, preferred_element_type=F32)

        @pl.when(i == 0)
        def _():
            dka[...] = dk
            dva[...] = dv

        @pl.when(i > 0)
        def _():
            dka[...] += dk
            dva[...] += dv

        @pl.when(i == nq - 1)
        def _():
            dk_ref[...] = dka[...].astype(BF16)
            dv_ref[...] = dva[...].astype(BF16)

    qspec = pl.BlockSpec((None, tq, CROSS_DIM), lambda b, h, i: (b, i, h))
    kspec = pl.BlockSpec((None, M, CROSS_DIM), lambda b, h, i: (b, 0, h))
    return pl.pallas_call(
        body, name=name, grid=(B, CROSS_HEADS, nq),
        in_specs=[qspec, kspec, pl.BlockSpec((None, M, CROSS_DIM), lambda b, h, i: (b, 0, CROSS_HEADS + h)), qspec],
        out_specs=[qspec, kspec, kspec],
        out_shape=[jax.ShapeDtypeStruct((B, S, D_CROSS), BF16), jax.ShapeDtypeStruct((B, M, D_CROSS), BF16),
                   jax.ShapeDtypeStruct((B, M, D_CROSS), BF16)],
        scratch_shapes=[pltpu.VMEM((M, CROSS_DIM), F32), pltpu.VMEM((M, CROSS_DIM), F32)],
        compiler_params=_params(dimension_semantics=("parallel", "parallel", "arbitrary")),
    )(qc, kv, kv, do)


def _local_step(x, mem, target, small, weights, emit, start_token=None):
    B, S, D = x.shape
    T = B * S
    x2, t2 = x.reshape(T, D), target.reshape(T, D)
    mem2 = mem.reshape(-1, D)
    n_mem = mem.shape[1]
    wts = {}

    hn1 = _rms_fwd("norm_ffn1", x2, small["ffn1_norm"], after=start_token)
    memn = _rms_fwd("norm_mem", mem2, small["mem_norm"])
    bias = _bias_tile("bias_tile", small["rel_bias"])
    wts.update(weights(0, [hn1, memn, bias]))
    g1 = _ffn_gate("ffn1_gate", hn1, wts["ffn1_w_gate"], N_DEV)
    wts.update(weights(1, g1))
    u1, a1 = _ffn_up_act("ffn1_up", hn1, wts["ffn1_w_up"], g1)
    wts.update(weights(2, a1))
    h1 = _ffn_out("ffn1_down", a1, wts["ffn1_w_down"], res=x2, scale=FFN_RES)
    wts.update(weights(3, h1))
    z, hn2 = _mm_nn_cols("mix_in", h1, wts["w_in"], F32, norm=small["mix_norm"])
    z = z.reshape(B, S, -1)
    mixed = _attn_fwd("attn_fwd", z, bias)
    d_pool, mixed = _pool_fwd("pool_fwd", z, wts["w_pool"], small["pool_scale"], mixed)
    cat = mixed.reshape(T, -1)
    h2 = _mm_nn("mix_out", cat, wts["w_out"], F32, res=h1)
    wts.update(weights(4, h2))
    qc, hn3 = _mm_nn("cross_q", h2, wts["w_cq"], BF16, norm=small["cross_norm"])
    kv = _mm_nn("cross_kv", memn, wts["w_ckv"], BF16)
    o = _cross_fwd("cross_fwd", qc.reshape(B, S, -1), kv.reshape(B, n_mem, -1)).reshape(T, -1)
    h3 = _mm_nn_cols("cross_out", o, wts["w_co"], F32, res=h2)
    wts.update(weights(5, h3))
    hn4, g2, u2, a2 = _ffn_up("ffn2_up", h3, small["ffn2_norm"], wts["ffn2_w_gate"], wts["ffn2_w_up"], N_DEV)
    h4 = _ffn_out("ffn2_down", a2, wts["ffn2_w_down"], res=h3, scale=FFN_RES)

    gs = {}
    loss_part, dh4, dh4b, gs["final_norm"] = _loss_and_grad("loss", h4, small["final_norm"], t2)

    def ffn_bwd(tag, dh, dhb, h_in, hn, g, u, a, wg, wu, wd, gain):
        tok = emit({tag + "_w_down": _ffn_dw(tag + "_dwd", a, dhb, scale=FFN_RES)})
        dg, du = _ffn_dact(tag + "_dact", dhb, wd, g, u, after=tok)
        tok = emit({tag + "_w_gate": _ffn_dw(tag + "_dwg", dg, hn)})
        tok = emit({tag + "_w_up": _ffn_dw(tag + "_dwu", du, hn, after=tok)})
        dhn = _ffn_out(tag + "_dhn_g", dg, wg, after=tok)
        dhn = _ffn_out(tag + "_dhn_u", du, wu, res=dhn)
        return _rms_bwd(tag + "_dnorm", h_in, gain, dhn, skip=dh)

    dh3, dh3b, gs["ffn2_norm"] = ffn_bwd("ffn2", dh4, dh4b, h3, hn4, g2, u2, a2, wts["ffn2_w_gate"],
                                         wts["ffn2_w_up"], wts["ffn2_w_down"], small["ffn2_norm"])
    do = _mm_nt_cols("cross_do", dh3b, wts["w_co"], BF16, tn=D_CROSS)
    gw = {"w_co": _mm_tn("cross_dwo", o, dh3b, tm=D_CROSS, col_blocks=N_DEV)}
    dqc, dk, dv = _cross_bwd("cross_bwd", qc.reshape(B, S, -1), kv.reshape(B, n_mem, -1), do.reshape(B, S, -1))
    dqc = dqc.reshape(T, -1)
    dkv = jnp.concatenate([dk, dv], axis=-1).reshape(B * n_mem, -1)
    gw["w_cq"] = _mm_tn("cross_dwq", hn3, dqc, tn=D_CROSS)
    gw["w_ckv"] = _mm_tn("cross_dwkv", memn, dkv)
    tok = emit(gw)
    dhn3 = _mm_nt("cross_dhn", dqc, wts["w_cq"], F32, after=tok)
    dmemn = _mm_nt("cross_dmem", dkv, wts["w_ckv"], F32, tm=512)
    _, _, gs["mem_norm"] = _rms_bwd("mem_dnorm", mem2, small["mem_norm"], dmemn)
    dh2, dh2b, gs["cross_norm"] = _rms_bwd("cross_dnorm", h2, small["cross_norm"], dhn3, skip=dh3)
    dcat = _mm_nt("mix_dcat", dh2b, wts["w_out"], BF16)
    gw = {"w_out": _mm_tn("mix_dwout", cat, dh2b)}
    dcat3 = dcat.reshape(B, S, -1)
    dq, dkk, dvv, dbias = _attn_bwd("attn_bwd", z, bias, dcat3)
    du, gw["w_pool"], gs["pool_scale"] = _pool_bwd("pool_bwd", d_pool, wts["w_pool"], small["pool_scale"], dcat3)
    gs["rel_bias"] = _rel_grad_from_diagonals(_bias_grad("bias_grad", dbias)[:, 0, :])
    dz = jnp.concatenate([dq, dkk, dvv, du], axis=-1).reshape(T, -1)
    gw["w_in"] = _mm_tn("mix_dwin", hn2, dz, col_blocks=N_DEV)
    tok = emit(gw)
    dhn2 = _mm_nt_cols("mix_dhn", dz, wts["w_in"], F32, after=tok)
    dh1, dh1b, gs["mix_norm"] = _rms_bwd("mix_dnorm", h1, small["mix_norm"], dhn2, skip=dh2)
    dx, _, gs["ffn1_norm"] = ffn_bwd("ffn1", dh1, dh1b, x2, hn1, g1, u1, a1, wts["ffn1_w_gate"],
                                     wts["ffn1_w_up"], wts["ffn1_w_down"], small["ffn1_norm"])
    return loss_part, dx.reshape(B, S, D), gs


def _position():
    return lax.axis_index("x"), lax.axis_index("y"), lax.axis_index("c")


def _index(p):
    return 4 * p[0] + 2 * p[1] + p[2]


HBM_SPEC = pl.BlockSpec(memory_space=pltpu.HBM)
SEM_SPEC = pl.BlockSpec(memory_space=pltpu.SEMAPHORE)
ANY_SPEC = pl.BlockSpec(memory_space=pl.ANY)
ORDERED_EFFECT = pltpu.SideEffectType.DATAFLOW_SIDE_EFFECTING


N_COPIES = {"grads": N_DEV - 1, "spread": 4, "relay": 3}
N_SEMS = {"grads": N_DEV, "spread": 5, "relay": 3}


def _copies(pattern, srcs, lands, send, recv, base=0):
    x, y, c = _position()
    me, sibling = _index((x, y, c)), (x, y, 1 - c)
    chips = [(1 - x, y), (x, 1 - y), (1 - x, 1 - y)]
    if pattern == "grads":
        targets = [(x ^ (k >> 2), y ^ ((k >> 1) & 1), c ^ (k & 1)) for k in range(1, N_DEV)]
    else:
        targets = [sibling] + [(*chip, c) for chip in chips]
    per, slots, out = N_COPIES[pattern], N_SEMS[pattern], []
    for a in range(len(lands)):
        for k in range(per):
            if pattern == "relay":
                src = dst = lands[a].at[_index((*chips[k], c))]
                to = sibling
            else:
                to = targets[k]
                src = srcs[a].at[_index(to)] if pattern == "grads" else srcs[a]
                dst = lands[a].at[me]
            slot = base + a * slots + k
            out.append(pltpu.make_async_remote_copy(src_ref=src, dst_ref=dst, send_sem=send.at[slot],
                                                    recv_sem=recv.at[slot], device_id=to, device_id_type=MESH))
    return out


def _own_copies(pattern, srcs, lands, send, base=0):
    if pattern == "relay":
        return []
    x, y, c = _position()
    me = _index((x, y, c))
    slots = N_SEMS[pattern]
    return [pltpu.make_async_copy(srcs[a].at[me] if pattern == "grads" else srcs[a], lands[a].at[me],
                                  send.at[base + a * slots + slots - 1]) for a in range(len(lands))]


def _comm(name, srcs, lands, wait=None, start=None, after=None):
    after = [] if after is None else list(after) if isinstance(after, (list, tuple)) else [after]
    ns, nl = len(srcs), len(lands)
    na = ns + nl
    arrays = list(srcs) + list(lands)
    n_wait = 2 if wait else 0
    n_start = 2 if start else 0

    def body(*refs):
        ins, lnd = refs[:ns], refs[ns:na]
        if wait:
            base = wait[3] if len(wait) > 3 else 0
            for cp in _copies(wait[0], ins, lnd, refs[na], refs[na + 1], base):
                cp.wait_send()
                cp.wait_recv()
            for cp in _own_copies(wait[0], ins, lnd, refs[na], base):
                cp.wait()
        if start:
            outs = refs[na + n_wait + len(after):]
            for cp in _copies(start, ins, lnd, outs[0], outs[1]) + _own_copies(start, ins, lnd, outs[0]):
                cp.start()
            refs[-1][...] = jnp.zeros((8, 128), F32)

    out_shape, out_specs = [], []
    if start:
        sems = pltpu.SemaphoreType.DMA((nl * N_SEMS[start],))
        out_shape += [sems, sems]
        out_specs += [SEM_SPEC, SEM_SPEC]
    out_shape += [pltpu.HBM(a.shape, a.dtype) for a in arrays]
    out_specs += [HBM_SPEC] * na
    if start:
        out_shape.append(jax.ShapeDtypeStruct((8, 128), F32))
        out_specs.append(pl.BlockSpec(memory_space=pltpu.VMEM))
    operands = [pltpu.with_memory_space_constraint(a, pltpu.HBM) for a in arrays]
    operands += list(wait[1:3]) if wait else []
    operands += after
    res = pl.pallas_call(
        body, name=name, out_shape=out_shape, out_specs=out_specs,
        in_specs=[HBM_SPEC] * na + [SEM_SPEC] * n_wait + [ANY_SPEC] * len(after),
        input_output_aliases={i: n_start + i for i in range(na)},
        compiler_params=pltpu.CompilerParams(has_side_effects=ORDERED_EFFECT),
    )(*operands)
    res = list(res)
    thru = res[n_start:n_start + na]
    return thru[:ns], thru[ns:], (tuple(res[:2]) if start else None), (res[-1] if start else None)


def _adamw_math(w, g, m, v):
    m = ADAM_B1 * m + (1.0 - ADAM_B1) * g
    v = ADAM_B2 * v + (1.0 - ADAM_B2) * (g * g)
    m_hat = m / (1.0 - ADAM_B1 ** ADAM_STEP)
    v_hat = v / (1.0 - ADAM_B2 ** ADAM_STEP)
    delta = -ADAM_LR * (m_hat / (jnp.sqrt(v_hat) + ADAM_EPS) + ADAM_WD * w)
    return delta, m, v


def _adamw(name, parts, w, m, v, tr=128):
    R, C = w.shape
    tr = _tile(R, tr)

    def body(p_ref, w_ref, m_ref, v_ref, g_out, d_out, m_out, v_out):
        g = p_ref[0].astype(F32)
        for d in range(1, N_DEV):
            g = g + p_ref[d].astype(F32)
        g_out[...] = g
        d_out[...], m_out[...], v_out[...] = _adamw_math(w_ref[...], g, m_ref[...], v_ref[...])

    row = pl.BlockSpec((tr, C), lambda i: (i, 0))
    out = jax.ShapeDtypeStruct((R, C), F32)
    return pl.pallas_call(
        body, name=name, grid=(R // tr,),
        in_specs=[pl.BlockSpec((N_DEV, tr, C), lambda i: (0, i, 0)), row, row, row],
        out_specs=[row] * 4, out_shape=[out] * 4,
        compiler_params=_params(dimension_semantics=("parallel",)),
    )(parts, w, m, v)


def _small_allreduce_adamw(name, g, w, m, v):
    R = g.shape[0]

    def body(g_ref, w_ref, m_ref, v_ref, g_out, d_out, m_out, v_out, land, send, recv):
        x, y, c = _position()
        me = _index((x, y, c))
        land[me] = g_ref[...]
        copies = []
        for k in range(1, N_DEV):
            peer = (x ^ (k >> 2), y ^ ((k >> 1) & 1), c ^ (k & 1))
            copies.append(pltpu.make_async_remote_copy(
                src_ref=g_ref, dst_ref=land.at[me], send_sem=send.at[k - 1], recv_sem=recv.at[k - 1],
                device_id=peer, device_id_type=MESH))
        for cp in copies:
            cp.start()
        for cp in copies:
            cp.wait()
        total = land[0]
        for d in range(1, N_DEV):
            total = total + land[d]
        g_out[...] = total
        d_out[...], m_out[...], v_out[...] = _adamw_math(w_ref[...], total, m_ref[...], v_ref[...])

    vm = pl.BlockSpec(memory_space=pltpu.VMEM)
    out = jax.ShapeDtypeStruct((R, 128), F32)
    return pl.pallas_call(
        body, name=name, in_specs=[vm] * 4, out_specs=[vm] * 4, out_shape=[out] * 4,
        scratch_shapes=[pltpu.VMEM((N_DEV, R, 128), F32), pltpu.SemaphoreType.DMA((7,)),
                        pltpu.SemaphoreType.DMA((7,))],
    )(g, w, m, v)


BIG = ("ffn1_w_gate", "ffn1_w_up", "ffn1_w_down", "w_in", "w_pool", "w_out", "w_cq", "w_ckv", "w_co",
       "ffn2_w_gate", "ffn2_w_up", "ffn2_w_down")
SMALL = ("ffn1_norm", "mix_norm", "rel_bias", "pool_scale", "cross_norm", "mem_norm", "ffn2_norm", "final_norm")
ORDER = ("ffn1_norm", "ffn1_w_gate", "ffn1_w_up", "ffn1_w_down", "mix_norm", "w_in", "rel_bias", "w_pool",
         "pool_scale", "w_out", "cross_norm", "mem_norm", "w_cq", "w_ckv", "w_co", "ffn2_norm", "ffn2_w_gate",
         "ffn2_w_up", "ffn2_w_down", "final_norm")
TRANSPOSED = ("ffn1_w_gate", "ffn1_w_up", "ffn2_w_gate", "ffn2_w_up")
ROW_SHARDED = TRANSPOSED + ("ffn1_w_down", "ffn2_w_down", "w_out", "w_cq", "w_ckv")
GATHER_GROUPS = (("ffn1_w_gate",), ("ffn1_w_up",), ("ffn1_w_down",), ("w_in", "w_pool", "w_out"),
                 ("w_cq", "w_ckv", "w_co"), ("ffn2_w_gate", "ffn2_w_up", "ffn2_w_down"))
RELAY_BEFORE_USE = ((0,), (1,), (2,), (3,), (4, 5), ())


def _pack(arrays):
    flat = jnp.concatenate([a.reshape(-1) for a in arrays])
    rows = -(-flat.shape[0] // 1024) * 8
    return jnp.pad(flat, (0, rows * 128 - flat.shape[0])).reshape(rows, 128)


def _unpack(packed, like):
    flat, out, at = packed.reshape(-1), [], 0
    for a in like:
        out.append(flat[at:at + a.size].reshape(a.shape))
        at += a.size
    return out


def _shard2d(a):
    a = a[0]
    return a.reshape(-1, a.shape[-1])


def kernel(x, mem, ffn1_norm, ffn1_w_gate, ffn1_w_up, ffn1_w_down, mix_norm, w_in, rel_bias, w_pool, pool_scale, w_out, cross_norm, mem_norm, w_cq, w_ckv, w_co, ffn2_norm, ffn2_w_gate, ffn2_w_up, ffn2_w_down, final_norm, loss_target, m_ffn1_norm, m_ffn1_w_gate, m_ffn1_w_up, m_ffn1_w_down, m_mix_norm, m_w_in, m_rel_bias, m_w_pool, m_pool_scale, m_w_out, m_cross_norm, m_mem_norm, m_w_cq, m_w_ckv, m_w_co, m_ffn2_norm, m_ffn2_w_gate, m_ffn2_w_up, m_ffn2_w_down, m_final_norm, v_ffn1_norm, v_ffn1_w_gate, v_ffn1_w_up, v_ffn1_w_down, v_mix_norm, v_w_in, v_rel_bias, v_w_pool, v_pool_scale, v_w_out, v_cross_norm, v_mem_norm, v_w_cq, v_w_ckv, v_w_co, v_ffn2_norm, v_ffn2_w_gate, v_ffn2_w_up, v_ffn2_w_down, v_final_norm):
    args = dict(locals())
    def view(n, a):
        return a.transpose(0, 2, 1) if n in TRANSPOSED else a

    w_in_ = {n: view(n, args[n]) for n in ORDER}
    m_in = {n: view(n, args["m_" + n]) for n in ORDER}
    v_in = {n: view(n, args["v_" + n]) for n in ORDER}

    me = 4 * lax.axis_index("x") + 2 * lax.axis_index("y") + lax.axis_index("c")
    n_g, rows = len(POOL_WINDOWS), POOL_GROUP // N_DEV

    def landing(block_shape, dtype):
        return lax.empty((N_DEV,) + tuple(block_shape), dtype)

    shards = [_shard2d(w_in_[n]).astype(BF16) for group in GATHER_GROUPS for n in group]
    srcs, lands, sems, tok = _comm("gather_start", shards, [landing(s.shape, BF16) for s in shards], start="spread")
    gathers, at = [], 0
    for group in GATHER_GROUPS:
        gathers.append((srcs[at:at + len(group)], lands[at:at + len(group)], sems + (at * N_SEMS["spread"],)))
        at += len(group)

    def weights(gi, after):
        for ri in RELAY_BEFORE_USE[gi]:
            srcs, lands, sems = gathers[ri]
            _, lands, sems, after = _comm("gather_relay_%d" % ri, srcs, lands, wait=("spread",) + sems,
                                          start="relay", after=after)
            gathers[ri] = (None, lands, sems)
        _, lands, sems = gathers[gi]
        _, lands, _, _ = _comm("gather_finish_%d" % gi, [], lands, wait=("relay",) + sems, after=after)
        out = {}
        for n, full in zip(GATHER_GROUPS[gi], lands):
            if n == "w_pool":
                full = full.reshape(N_DEV, n_g, rows, POOL_GROUP).transpose(1, 0, 2, 3).reshape(n_g, POOL_GROUP, POOL_GROUP)
            out[n] = full.reshape(-1, full.shape[-1]) if n in ROW_SHARDED else full
        return out

    scatters = []

    def emit(gw):
        names = list(gw)
        stacks = []
        for n in names:
            g = gw[n]
            if n == "w_pool":
                g = g.reshape(n_g, N_DEV, rows, POOL_GROUP).transpose(1, 0, 2, 3).astype(BF16)
            stacks.append(g.reshape((N_DEV,) + _shard2d(w_in_[n]).shape))
        lands = [landing(s.shape[1:], s.dtype) for s in stacks]
        srcs, lands, sems, token = _comm("grads_start_%d" % len(scatters), stacks, lands, start="grads")
        scatters.append((names, srcs, lands, sems))
        return token

    small = {n: w_in_[n].reshape(1, -1) for n in SMALL if n != "rel_bias"}
    small["rel_bias"] = rel_bias[0]
    loss_part, grad_x, gs = _local_step(x, mem, loss_target, small, weights, emit, start_token=tok)
    loss = lax.psum(loss_part[0, 0], ("x", "y", "c"))

    grad, delta, new_m, new_v = {}, {}, {}, {}
    after = grad_x
    for si, (names, srcs, lands, sems) in enumerate(scatters):
        _, landed, _, _ = _comm("grads_finish_%d" % si, srcs, lands, wait=("grads",) + sems, after=after)
        for n, parts in zip(names, landed):
            res = _adamw("adamw_" + n, parts, _shard2d(w_in_[n]), _shard2d(m_in[n]), _shard2d(v_in[n]))
            grad[n], delta[n], new_m[n], new_v[n] = [view(n, r.reshape(w_in_[n].shape)) for r in res]
        after = res[0]

    like = [w_in_[n] for n in SMALL]
    gs["rel_bias"] = gs["rel_bias"].reshape(rel_bias.shape)
    res = _small_allreduce_adamw("small_params", _pack([gs[n] for n in SMALL]), _pack(like),
                                 _pack([m_in[n] for n in SMALL]), _pack([v_in[n] for n in SMALL]))
    for d, packed in zip((grad, delta, new_m, new_v), res):
        for n, a in zip(SMALL, _unpack(packed, like)):
            d[n] = a
    return (loss, grad_x, *[grad[n] for n in ORDER], *[delta[n] for n in ORDER],
            *[new_m[n] for n in ORDER], *[new_v[n] for n in ORDER])
```

```python
import functools

import jax
import jax.numpy as jnp
from jax import lax
from jax.experimental import pallas as pl
from jax.experimental.pallas import tpu as pltpu

F32 = jnp.float32
BF16 = jnp.bfloat16

N_DEV = 8
EPS = 1e-6
NEG_INF = -1e30
CHUNK = 64
LEFT_CHUNKS = 8
PAD = LEFT_CHUNKS * CHUNK
QBLK = 4 * CHUNK
KBAND = PAD + QBLK
REL_CLIP = 128
ATTN_HEADS = 16
HEAD_DIM = 64
D_ATTN = ATTN_HEADS * HEAD_DIM
POOL_WINDOWS = (2, 4, 8, 16)
POOL_GROUP = 256
D_POOL = len(POOL_WINDOWS) * POOL_GROUP
CROSS_HEADS = 4
CROSS_DIM = 128
D_CROSS = CROSS_HEADS * CROSS_DIM
FFN_RES = 0.5
ADAM_LR, ADAM_B1, ADAM_B2, ADAM_EPS, ADAM_WD, ADAM_STEP = 0.001, 0.9, 0.999, 1e-08, 0.01, 10

NN = (((1,), (0,)), ((), ()))
NT = (((1,), (1,)), ((), ()))
TN = (((0,), (0,)), ((), ()))
MESH = pl.DeviceIdType.MESH
VMEM_LIMIT = 56 * 1024 * 1024


def _params(**kw):
    return pltpu.CompilerParams(vmem_limit_bytes=VMEM_LIMIT, **kw)


def _bf(v):
    return v if v.dtype == BF16 else v.astype(BF16)


WHOLE = ((Ellipsis,), (Ellipsis,))


def _rms(xv, gain):
    return (xv * lax.rsqrt(jnp.mean(xv * xv, axis=-1, keepdims=True) + EPS)) * gain


def _gemm(name, a, a_spec, b, b_spec, dims, grid, outs, chunks=(WHOLE,), extras=(), epilogue=None, after=None,
          norm=None):
    nex, nout = len(extras), len(outs)
    first_out = 2 + nex + (after is not None) + (norm is not None)

    def body(*refs):
        a_ref, b_ref = refs[:2]
        if norm is not None:
            hn_out, a_ref = refs[first_out + nout], refs[-1]

            @pl.when(pl.program_id(1) == 0)
            def _():
                hn = _rms(refs[0][...], refs[first_out - 1][...]).astype(BF16)
                a_ref[...] = hn
                hn_out[...] = hn

        total = None
        for ia, ib in chunks:
            d = lax.dot_general(_bf(a_ref[ia]), _bf(b_ref[ib]), dims, preferred_element_type=F32)
            total = d if total is None else total + d
        vals = epilogue(total, *[e[...] for e in refs[2:2 + nex]]) if epilogue is not None else (total,)
        for r, v in zip(refs[first_out:first_out + nout], vals):
            r[...] = v.astype(r.dtype)

    operands = [a, b] + [x for x, _, _ in extras]
    in_specs = [pl.BlockSpec(*a_spec), pl.BlockSpec(*b_spec)] + [pl.BlockSpec(blk, m) for _, blk, m in extras]
    if after is not None:
        operands.append(after)
        in_specs.append(pl.BlockSpec(after.shape, lambda i, j: (0, 0)))
    out_specs = [pl.BlockSpec(blk, m) for _, _, blk, m in outs]
    out_shape = [jax.ShapeDtypeStruct(s, d) for s, d, _, _ in outs]
    scratch = []
    if norm is not None:
        operands.append(norm)
        in_specs.append(pl.BlockSpec(norm.shape, lambda i, j: (0, 0)))
        out_specs.append(pl.BlockSpec(*a_spec))
        out_shape.append(jax.ShapeDtypeStruct(a.shape, BF16))
        scratch.append(pltpu.VMEM(a_spec[0], BF16))
    res = pl.pallas_call(
        body, name=name, grid=grid, in_specs=in_specs, out_specs=out_specs, out_shape=out_shape,
        scratch_shapes=scratch,
        compiler_params=_params(dimension_semantics=("parallel", "arbitrary" if norm is not None else "parallel")),
    )(*operands)
    return res[0] if len(res) == 1 else res


def _tile(n, want):
    for t in range(min(n, want), 15, -1):
        if n % t == 0 and t % 16 == 0:
            return t
    return n


def _mm_nn(name, a, b, out_dtype, res=None, tm=1024, tn=1024, norm=None):
    M, K = a.shape
    N = b.shape[1]
    tm, tn = _tile(M, tm), _tile(N, tn)
    extras = [] if res is None else [(res, (tm, tn), lambda i, j: (i, j))]
    epi = None if res is None else (lambda t, r: (r + t,))
    return _gemm(name, a, ((tm, K), lambda i, j: (i, 0)), b, ((K, tn), lambda i, j: (0, j)), NN,
                 (M // tm, N // tn), [((M, N), out_dtype, (tm, tn), lambda i, j: (i, j))], extras=extras, epilogue=epi,
                 norm=norm)


def _mm_nn_cols(name, a, bs, out_dtype, res=None, tm=1024, norm=None):
    M, K = a.shape
    nb, _, w = bs.shape
    tm = _tile(M, tm)
    extras = [] if res is None else [(res, (tm, w), lambda i, j: (i, j))]
    epi = None if res is None else (lambda t, r: (r + t,))
    return _gemm(name, a, ((tm, K), lambda i, j: (i, 0)), bs, ((None, K, w), lambda i, j: (j, 0, 0)), NN,
                 (M // tm, nb), [((M, nb * w), out_dtype, (tm, w), lambda i, j: (i, j))], extras=extras, epilogue=epi,
                 norm=norm)


def _mm_nt(name, a, b, out_dtype, tm=1024, tn=1024, after=None):
    M, K = a.shape
    N = b.shape[0]
    tm, tn = _tile(M, tm), _tile(N, tn)
    return _gemm(name, a, ((tm, K), lambda i, j: (i, 0)), b, ((tn, K), lambda i, j: (j, 0)), NT,
                 (M // tm, N // tn), [((M, N), out_dtype, (tm, tn), lambda i, j: (i, j))], after=after)


def _mm_nt_cols(name, a, bs, out_dtype, tm=1024, tn=512, after=None):
    M = a.shape[0]
    nb, N, w = bs.shape
    tm, tn = _tile(M, tm), _tile(N, tn)
    chunks = [((slice(None), pl.ds(c * w, w)), (c,)) for c in range(nb)]
    return _gemm(name, a, ((tm, nb * w), lambda i, j: (i, 0)), bs, ((nb, tn, w), lambda i, j: (0, j, 0)), NT,
                 (M // tm, N // tn), [((M, N), out_dtype, (tm, tn), lambda i, j: (i, j))], chunks, after=after)


def _mm_tn(name, a, b, tm=1024, tn=1024, col_blocks=None):
    T, Ka = a.shape
    Nb = b.shape[1]
    tm = _tile(Ka, tm)
    if col_blocks is None:
        tn = _tile(Nb, tn)
        out = ((Ka, Nb), BF16, (tm, tn), lambda i, j: (i, j))
    else:
        tn = Nb // col_blocks
        out = ((col_blocks, Ka, tn), BF16, (None, tm, tn), lambda i, j: (j, i, 0))
    return _gemm(name, a, ((T, tm), lambda i, j: (0, i)), b, ((T, tn), lambda i, j: (0, j)), TN,
                 (Ka // tm, Nb // tn), [out])


def _ffn_out(name, a, wt, res=None, scale=1.0, tm=1024, tn=512, after=None):
    nb, M, w = a.shape
    N = wt.shape[1]
    tm, tn = _tile(M, tm), _tile(N, tn)
    chunks = [((c,), (pl.ds(c * w, w),)) for c in range(nb)]
    extras = [] if res is None else [(res, (tm, tn), lambda i, j: (i, j))]
    epi = None if res is None else (lambda t, r: (r + scale * t,))
    return _gemm(name, a, ((nb, tm, w), lambda i, j: (0, i, 0)), wt, ((nb * w, tn), lambda i, j: (0, j)), NN,
                 (M // tm, N // tn), [((M, N), F32, (tm, tn), lambda i, j: (i, j))], chunks, extras, epi, after)


def _ffn_dact(name, dhb, wd, g, u, tm=1024, after=None):
    M, K = dhb.shape
    nb, _, w = g.shape
    tm = _tile(M, tm)
    tr = _tile(tm, 256)
    tokens = [] if after is None else [after]

    def body(dh_ref, wd_ref, g_ref, u_ref, *rest):
        dg_ref, du_ref = rest[-2:]
        pieces = [pl.ds(r * tr, tr) for r in range(tm // tr)]

        def product(rows):
            return lax.dot_general(dh_ref[rows, :], wd_ref[...], NT, preferred_element_type=F32)

        results = []
        dact = product(pieces[0])
        for n, rows in enumerate(pieces):
            ahead = product(pieces[n + 1]) if n + 1 < len(pieces) else None
            results.append(_swiglu_bwd(dact, g_ref[rows, :], u_ref[rows, :]))
            dact = ahead
        for rows, (dg, du) in zip(pieces, results):
            dg_ref[rows, :] = dg.astype(BF16)
            du_ref[rows, :] = du.astype(BF16)

    hid = pl.BlockSpec((None, tm, w), lambda i, j: (j, i, 0))
    return pl.pallas_call(
        body, name=name, grid=(M // tm, nb),
        in_specs=[pl.BlockSpec((tm, K), lambda i, j: (i, 0)), pl.BlockSpec((w, K), lambda i, j: (j, 0)), hid, hid]
        + [pl.BlockSpec(t.shape, lambda i, j: (0, 0)) for t in tokens],
        out_specs=[hid, hid], out_shape=[jax.ShapeDtypeStruct((nb, M, w), BF16)] * 2,
        compiler_params=_params(dimension_semantics=("parallel", "parallel")),
    )(dhb, wd, g, u, *tokens)


def _ffn_dw(name, a, b, scale=1.0, tn=1024, after=None):
    nb, T, w = a.shape
    N = b.shape[1]
    tn = _tile(N, tn)
    epi = None if scale == 1.0 else (lambda t: (t * scale,))
    return _gemm(name, a, ((None, T, w), lambda i, j: (i, 0, 0)), b, ((T, tn), lambda i, j: (0, j)), TN,
                 (nb, N // tn), [((nb * w, N), BF16, (w, tn), lambda i, j: (i, j))], epilogue=epi, after=after)


def _ffn_gate(name, hn, wgt, nb, tm=1024):
    M, K = hn.shape
    w = wgt.shape[0] // nb
    tm = _tile(M, tm)
    return _gemm(name, hn, ((tm, K), lambda i, j: (i, 0)), wgt, ((w, K), lambda i, j: (j, 0)), NT,
                 (M // tm, nb), [((nb, M, w), BF16, (None, tm, w), lambda i, j: (j, i, 0))])


def _ffn_up_act(name, hn, wut, g, tm=1024):
    M, K = hn.shape
    nb, _, w = g.shape
    tm = _tile(M, tm)
    hid = ((None, tm, w), lambda i, j: (j, i, 0))

    def epilogue(u, gate):
        gate = gate.astype(F32)
        return u, gate * jax.nn.sigmoid(gate) * u

    return _gemm(name, hn, ((tm, K), lambda i, j: (i, 0)), wut, ((w, K), lambda i, j: (j, 0)), NT,
                 (M // tm, nb), [((nb, M, w), BF16) + hid] * 2, extras=[(g,) + hid], epilogue=epilogue)


def _ffn_up(name, h, gain, wgt, wut, nb, tm=1024):
    M, K = h.shape
    w = wgt.shape[0] // nb
    tm = _tile(M, tm)

    def body(h_ref, gain_ref, g_ref, u_ref, hn_ref, og, ou, oa, a_ref):
        @pl.when(pl.program_id(1) == 0)
        def _():
            hn = _rms(h_ref[...], gain_ref[...]).astype(BF16)
            a_ref[...] = hn
            hn_ref[...] = hn

        a = a_ref[...]
        g = lax.dot_general(a, g_ref[...], NT, preferred_element_type=F32)
        u = lax.dot_general(a, u_ref[...], NT, preferred_element_type=F32)
        og[...] = g.astype(BF16)
        ou[...] = u.astype(BF16)
        oa[...] = (g * jax.nn.sigmoid(g) * u).astype(BF16)

    rows = pl.BlockSpec((tm, K), lambda i, j: (i, 0))
    wspec = pl.BlockSpec((w, K), lambda i, j: (j, 0))
    ospec = pl.BlockSpec((None, tm, w), lambda i, j: (j, i, 0))
    return pl.pallas_call(
        body, name=name, grid=(M // tm, nb),
        in_specs=[rows, pl.BlockSpec((1, K), lambda i, j: (0, 0)), wspec, wspec],
        out_specs=[rows] + [ospec] * 3,
        out_shape=[jax.ShapeDtypeStruct((M, K), BF16)] + [jax.ShapeDtypeStruct((nb, M, w), BF16)] * 3,
        scratch_shapes=[pltpu.VMEM((tm, K), BF16)],
        compiler_params=_params(dimension_semantics=("parallel", "arbitrary")),
    )(h, gain, wgt, wut)


def _swiglu_bwd(dact, g, u):
    g = g.astype(F32)
    u = u.astype(F32)
    sig = jax.nn.sigmoid(g)
    silu = g * sig
    d = FFN_RES * dact
    return d * u * (sig * (1.0 + g * (1.0 - sig))), d * silu


def _rms_fwd(name, x, gain, tr=512, after=None):
    R, D = x.shape
    tr = _tile(R, tr)

    def body(x_ref, g_ref, *rest):
        xv = x_ref[...]
        y = xv * lax.rsqrt(jnp.mean(xv * xv, axis=-1, keepdims=True) + EPS)
        rest[-1][...] = (y * g_ref[...]).astype(BF16)

    tokens = [] if after is None else [after]
    return pl.pallas_call(
        body, name=name, grid=(R // tr,),
        in_specs=[pl.BlockSpec((tr, D), lambda i: (i, 0)), pl.BlockSpec((1, D), lambda i: (0, 0))]
        + [pl.BlockSpec(t.shape, lambda i: (0, 0)) for t in tokens],
        out_specs=pl.BlockSpec((tr, D), lambda i: (i, 0)), out_shape=jax.ShapeDtypeStruct((R, D), BF16),
        compiler_params=_params(dimension_semantics=("parallel",)),
    )(x, gain, *tokens)


def _rms_bwd_math(xv, gain, dy):
    rstd = lax.rsqrt(jnp.mean(xv * xv, axis=-1, keepdims=True) + EPS)
    xhat = xv * rstd
    dxh = dy * gain
    dx = rstd * (dxh - xhat * jnp.mean(dxh * xhat, axis=-1, keepdims=True))
    return dx, jnp.sum(dy * xhat, axis=0, keepdims=True)


def _rms_bwd(name, x, gain, dy, skip=None, tr=256):
    R, D = x.shape
    tr = _tile(R, tr)
    has_skip = skip is not None

    def body(*refs):
        x_ref, g_ref, dy_ref = refs[:3]
        dx_ref, dxb_ref, dg_ref = refs[-3:]
        dx, dg = _rms_bwd_math(x_ref[...], g_ref[...], dy_ref[...].astype(F32))
        if has_skip:
            dx = dx + refs[3][...]
        dx_ref[...] = dx
        dxb_ref[...] = dx.astype(BF16)

        @pl.when(pl.program_id(0) == 0)
        def _():
            dg_ref[...] = dg

        @pl.when(pl.program_id(0) > 0)
        def _():
            dg_ref[...] += dg

    row = pl.BlockSpec((tr, D), lambda i: (i, 0))
    vec = pl.BlockSpec((1, D), lambda i: (0, 0))
    return pl.pallas_call(
        body, name=name, grid=(R // tr,),
        in_specs=[row, vec, row] + ([row] if has_skip else []),
        out_specs=[row, row, vec],
        out_shape=[jax.ShapeDtypeStruct((R, D), F32), jax.ShapeDtypeStruct((R, D), BF16),
                   jax.ShapeDtypeStruct((1, D), F32)],
        compiler_params=_params(dimension_semantics=("arbitrary",)),
    )(*([x, gain, dy] + ([skip] if has_skip else [])))


def _loss_and_grad(name, h, gain, target, tr=256):
    R, D = h.shape
    tr = _tile(R, tr)

    def body(h_ref, g_ref, t_ref, loss_ref, dh_ref, dhb_ref, dg_ref):
        hv, gain_v = h_ref[...], g_ref[...]
        y = (hv * lax.rsqrt(jnp.mean(hv * hv, axis=-1, keepdims=True) + EPS)) * gain_v
        err = y - t_ref[...]
        part = jnp.full((8, 128), 0.5 * jnp.sum(jnp.mean(err * err, axis=-1, keepdims=True)), F32)
        dh, dg = _rms_bwd_math(hv, gain_v, err * (1.0 / D))
        dh_ref[...] = dh
        dhb_ref[...] = dh.astype(BF16)

        @pl.when(pl.program_id(0) == 0)
        def _():
            dg_ref[...] = dg
            loss_ref[...] = part

        @pl.when(pl.program_id(0) > 0)
        def _():
            dg_ref[...] += dg
            loss_ref[...] += part

    row = pl.BlockSpec((tr, D), lambda i: (i, 0))
    vec = pl.BlockSpec((1, D), lambda i: (0, 0))
    return pl.pallas_call(
        body, name=name, grid=(R // tr,), in_specs=[row, vec, row],
        out_specs=[pl.BlockSpec((8, 128), lambda i: (0, 0)), row, row, vec],
        out_shape=[jax.ShapeDtypeStruct((8, 128), F32), jax.ShapeDtypeStruct((R, D), F32),
                   jax.ShapeDtypeStruct((R, D), BF16), jax.ShapeDtypeStruct((1, D), F32)],
        compiler_params=_params(dimension_semantics=("arbitrary",)),
    )(h, gain, target)


def _bias_tile(name, rel):
    width = KBAND + QBLK
    sat = rel[:, 2 * REL_CLIP:]
    n_left = PAD - REL_CLIP + 1
    row0 = jnp.concatenate([jnp.broadcast_to(sat, (ATTN_HEADS, n_left)), rel[:, :2 * REL_CLIP][:, ::-1],
                            jnp.broadcast_to(sat, (ATTN_HEADS, width - n_left - 2 * REL_CLIP))], axis=1)

    def body(e_ref, o_ref):
        rows = pltpu.roll(jnp.broadcast_to(e_ref[...], (QBLK, width)), 0, 1, stride=1, stride_axis=0)
        i = lax.broadcasted_iota(jnp.int32, (QBLK, KBAND), 0) // CHUNK
        j = lax.broadcasted_iota(jnp.int32, (QBLK, KBAND), 1) // CHUNK
        o_ref[...] = jnp.where((j >= i) & (j <= i + LEFT_CHUNKS), rows[:, :KBAND], NEG_INF)

    return pl.pallas_call(
        body, name=name, grid=(ATTN_HEADS,),
        in_specs=[pl.BlockSpec((None, 1, width), lambda h: (h, 0, 0))],
        out_specs=pl.BlockSpec((None, QBLK, KBAND), lambda h: (h, 0, 0)),
        out_shape=jax.ShapeDtypeStruct((ATTN_HEADS, QBLK, KBAND), F32),
        compiler_params=_params(dimension_semantics=("parallel",)),
    )(row0.reshape(ATTN_HEADS, 1, width))


ATTN_SCALE = HEAD_DIM ** -0.5
ROW_PIECES = 1


def _stack_heads(x, first):
    zero = jnp.zeros_like(x)
    return jnp.concatenate([jnp.where(first, x, zero), jnp.where(first, zero, x)], axis=0)


def _band_softmax(q_half_scaled, kb, bias, left_mask):
    s = lax.dot_general(q_half_scaled, kb, NT, preferred_element_type=F32) + bias + left_mask
    e = jnp.exp(s - jnp.max(s, axis=-1, keepdims=True))
    return e * (1.0 / jnp.sum(e, axis=-1, keepdims=True))


def _left_mask(qb):
    kpos = qb * QBLK - PAD + lax.broadcasted_iota(jnp.int32, (1, KBAND), 1)
    return jnp.where(kpos >= 0, 0.0, NEG_INF).astype(F32)


def _fill_padded(dst, src, S):
    dst[pl.ds(0, PAD), :] = jnp.zeros((PAD, dst.shape[1]), dst.dtype)
    dst[pl.ds(PAD, S), :] = src[...].astype(dst.dtype)


def _attn_fwd(name, z, bias):
    B, S, _ = z.shape
    nh2 = ATTN_HEADS // 2

    def body(q_ref, k_ref, v_ref, b_ref, o_ref, kp, vp):
        qb = pl.program_id(2)

        @pl.when(qb == 0)
        def _():
            _fill_padded(kp, k_ref, S)
            _fill_padded(vp, v_ref, S)

        start = pl.multiple_of(qb * QBLK, QBLK)
        kb, vb = kp[pl.ds(start, KBAND), :], vp[pl.ds(start, KBAND), :]
        q = (q_ref[...] * ATTN_SCALE).astype(BF16)
        first = lax.broadcasted_iota(jnp.int32, (QBLK, 2 * HEAD_DIM), 1) < HEAD_DIM
        left = _left_mask(qb)
        zero = jnp.zeros_like(q)
        qh = [jnp.where(first, q, zero), jnp.where(first, zero, q)]
        rp = QBLK // ROW_PIECES
        chains = [(a, r) for r in range(ROW_PIECES) for a in range(2)]
        ss = [lax.dot_general(qh[a][r * rp:(r + 1) * rp], kb, NT, preferred_element_type=F32) for a, r in chains]
        ps = []
        for (a, r), s in zip(chains, ss):
            s = s + b_ref[a, pl.ds(r * rp, rp), :] + left
            e = jnp.exp(s - jnp.max(s, axis=-1, keepdims=True))
            ps.append((e * (1.0 / jnp.sum(e, axis=-1, keepdims=True))).astype(BF16))
        os_ = [jnp.dot(p, vb, preferred_element_type=F32) for p in ps]
        for r in range(ROW_PIECES):
            o_ref[pl.ds(r * rp, rp), :] = jnp.where(first[:rp], os_[2 * r], os_[2 * r + 1]).astype(BF16)

    return pl.pallas_call(
        body, name=name, grid=(B, nh2, S // QBLK),
        in_specs=[pl.BlockSpec((None, QBLK, 128), lambda b, h, i: (b, i, h)),
                  pl.BlockSpec((None, S, 128), lambda b, h, i: (b, 0, nh2 + h)),
                  pl.BlockSpec((None, S, 128), lambda b, h, i: (b, 0, 2 * nh2 + h)),
                  pl.BlockSpec((2, QBLK, KBAND), lambda b, h, i: (h, 0, 0))],
        out_specs=pl.BlockSpec((None, QBLK, 128), lambda b, h, i: (b, i, h)),
        out_shape=jax.ShapeDtypeStruct((B, S, D_ATTN + D_POOL), BF16),
        scratch_shapes=[pltpu.VMEM((PAD + S, 128), BF16), pltpu.VMEM((PAD + S, 128), BF16)],
        compiler_params=_params(dimension_semantics=("parallel", "parallel", "arbitrary")),
    )(z, z, z, bias)


def _attn_bwd(name, z, bias, dcat):
    B, S, _ = z.shape
    nh2 = ATTN_HEADS // 2
    nqb = S // QBLK

    def body(q_ref, k_ref, v_ref, b_ref, do_ref, dq_ref, dk_ref, dv_ref, db_ref, kp, vp, dka, dva):
        b, qb = pl.program_id(1), pl.program_id(2)

        @pl.when(qb == 0)
        def _():
            _fill_padded(kp, k_ref, S)
            _fill_padded(vp, v_ref, S)
            dka[...] = jnp.zeros_like(dka)
            dva[...] = jnp.zeros_like(dva)

        @pl.when((qb == 0) & (b == 0))
        def _():
            db_ref[...] = jnp.zeros_like(db_ref)

        start = pl.multiple_of(qb * QBLK, QBLK)
        band = pl.ds(start, KBAND)
        kb, vb = kp[band, :], vp[band, :]
        q = (q_ref[...] * ATTN_SCALE).astype(BF16)
        do = do_ref[...]
        first = lax.broadcasted_iota(jnp.int32, (QBLK, 2 * HEAD_DIM), 1) < HEAD_DIM
        left = _left_mask(qb)
        q2, do2 = _stack_heads(q, first), _stack_heads(do, first)
        p = _band_softmax(q2, kb, b_ref[...].reshape(2 * QBLK, KBAND), left)
        dp = lax.dot_general(do2, vb, NT, preferred_element_type=F32)
        ds = p * (dp - jnp.sum(p * dp, axis=-1, keepdims=True))
        db_ref[...] += ds.reshape(2, QBLK, KBAND)
        dsb = ds.astype(BF16)
        dq = jnp.dot(dsb, kb, preferred_element_type=F32)
        dq_ref[...] = (jnp.where(first, dq[:QBLK], dq[QBLK:]) * ATTN_SCALE).astype(BF16)
        dka[band, :] += lax.dot_general(dsb, q2, TN, preferred_element_type=F32)
        dva[band, :] += lax.dot_general(p.astype(BF16), do2, TN, preferred_element_type=F32)

        @pl.when(qb == nqb - 1)
        def _():
            dk_ref[...] = dka[pl.ds(PAD, S), :].astype(BF16)
            dv_ref[...] = dva[pl.ds(PAD, S), :].astype(BF16)

    qspec = pl.BlockSpec((None, QBLK, 128), lambda h, b, i: (b, i, h))
    kvout = pl.BlockSpec((None, S, 128), lambda h, b, i: (b, 0, h))
    bspec = pl.BlockSpec((2, QBLK, KBAND), lambda h, b, i: (h, 0, 0))
    act = jax.ShapeDtypeStruct((B, S, D_ATTN), BF16)
    return pl.pallas_call(
        body, name=name, grid=(nh2, B, nqb),
        in_specs=[qspec,
                  pl.BlockSpec((None, S, 128), lambda h, b, i: (b, 0, nh2 + h)),
                  pl.BlockSpec((None, S, 128), lambda h, b, i: (b, 0, 2 * nh2 + h)),
                  bspec, qspec],
        out_specs=[qspec, kvout, kvout, bspec],
        out_shape=[act, act, act, jax.ShapeDtypeStruct((ATTN_HEADS, QBLK, KBAND), F32)],
        scratch_shapes=[pltpu.VMEM((PAD + S, 128), BF16), pltpu.VMEM((PAD + S, 128), BF16),
                        pltpu.VMEM((PAD + S, 128), F32), pltpu.VMEM((PAD + S, 128), F32)],
        compiler_params=_params(dimension_semantics=("arbitrary", "arbitrary", "arbitrary")),
    )(z, z, z, bias, dcat)


def _bias_grad(name, dbias):
    width = KBAND + QBLK

    def body(d_ref, o_ref):
        acc = jnp.zeros((1, width), F32)
        for i in range(QBLK):
            row = jnp.concatenate([d_ref[pl.ds(i, 1), :], jnp.zeros((1, QBLK), F32)], axis=1)
            shift = QBLK - 1 - i
            acc = acc + (pltpu.roll(row, shift, 1) if shift else row)
        o_ref[...] = acc

    return pl.pallas_call(
        body, name=name, grid=(ATTN_HEADS,),
        in_specs=[pl.BlockSpec((None, QBLK, KBAND), lambda h: (h, 0, 0))],
        out_specs=pl.BlockSpec((None, 1, width), lambda h: (h, 0, 0)),
        out_shape=jax.ShapeDtypeStruct((ATTN_HEADS, 1, width), F32),
        compiler_params=_params(dimension_semantics=("parallel",)),
    )(dbias)


def _rel_grad_from_diagonals(diag):
    top = PAD + QBLK - 1 - REL_CLIP
    sat = jnp.sum(diag[:, :top + 1], axis=1, keepdims=True)
    mid = diag[:, top + 1:top + 2 * REL_CLIP][:, ::-1]
    return jnp.concatenate([jnp.zeros_like(sat), mid, sat], axis=1)


def _shift_rows(x, k, forward):
    S = x.shape[0]
    t = lax.broadcasted_iota(jnp.int32, x.shape, 0)
    if forward:
        return jnp.where(t < S - k, pltpu.roll(x, S - k, 0), 0.0)
    return jnp.where(t >= k, pltpu.roll(x, k, 0), 0.0)


def _window_sum(x, g, forward):
    s = x + _shift_rows(x, 1, forward)
    out = s
    for n, k in enumerate((2, 4, 8)):
        s = s + _shift_rows(s, k, forward)
        out = jnp.where(g > n, s, out)
    return out


def _pool_count(S, g):
    t = lax.broadcasted_iota(jnp.int32, (S, 1), 0)
    w = jnp.left_shift(2, g)
    return jnp.minimum(t + 1, w).astype(F32)


def _pool_fwd(name, z, wp, pscale, mixed):
    B, S, _ = z.shape
    c0 = 3 * D_ATTN // POOL_GROUP
    y0 = D_ATTN // POOL_GROUP

    def body(u_ref, w_ref, s_ref, mixed_ref, d_ref, y_ref):
        g = pl.program_id(1)
        u = u_ref[...]
        d = (_window_sum(u, g, False) / _pool_count(S, g) - u).astype(BF16)
        d_ref[...] = d
        y_ref[...] = (jnp.dot(d, w_ref[...], preferred_element_type=F32) * s_ref[...]).astype(BF16)

    return pl.pallas_call(
        body, name=name, grid=(B, len(POOL_WINDOWS)),
        in_specs=[pl.BlockSpec((None, S, POOL_GROUP), lambda b, g: (b, 0, c0 + g)),
                  pl.BlockSpec((None, POOL_GROUP, POOL_GROUP), lambda b, g: (g, 0, 0)),
                  pl.BlockSpec((1, POOL_GROUP), lambda b, g: (0, g)),
                  pl.BlockSpec(memory_space=pl.ANY)],
        out_specs=[pl.BlockSpec((None, S, POOL_GROUP), lambda b, g: (b, 0, g)),
                   pl.BlockSpec((None, S, POOL_GROUP), lambda b, g: (b, 0, y0 + g))],
        out_shape=[jax.ShapeDtypeStruct((B, S, D_POOL), BF16), jax.ShapeDtypeStruct(mixed.shape, BF16)],
        input_output_aliases={3: 1},
        compiler_params=_params(dimension_semantics=("parallel", "parallel")),
    )(z, wp, pscale, mixed)


def _pool_bwd(name, d, wp, pscale, dcat):
    B, S, _ = d.shape
    c0 = D_ATTN // POOL_GROUP

    def body(d_ref, w_ref, s_ref, dy_ref, du_ref, dw_ref, dsc_ref):
        g, b = pl.program_id(0), pl.program_id(1)
        dv = d_ref[...]
        dy = dy_ref[...].astype(F32)
        w = w_ref[...]
        ypre = jnp.dot(dv, w, preferred_element_type=F32)
        dyp = (dy * s_ref[...]).astype(BF16)
        dd = lax.dot_general(dyp, w, NT, preferred_element_type=F32)
        du_ref[...] = (_window_sum(dd / _pool_count(S, g), g, True) - dd).astype(BF16)
        dw = lax.dot_general(dv, dyp, TN, preferred_element_type=F32)
        dsc = jnp.sum(dy * ypre, axis=0, keepdims=True)

        @pl.when(b == 0)
        def _():
            dw_ref[...] = dw
            dsc_ref[...] = dsc

        @pl.when(b > 0)
        def _():
            dw_ref[...] += dw
            dsc_ref[...] += dsc

    blk = pl.BlockSpec((None, S, POOL_GROUP), lambda g, b: (b, 0, g))
    wspec = pl.BlockSpec((None, POOL_GROUP, POOL_GROUP), lambda g, b: (g, 0, 0))
    sspec = pl.BlockSpec((1, POOL_GROUP), lambda g, b: (0, g))
    return pl.pallas_call(
        body, name=name, grid=(len(POOL_WINDOWS), B),
        in_specs=[blk, wspec, sspec, pl.BlockSpec((None, S, POOL_GROUP), lambda g, b: (b, 0, c0 + g))],
        out_specs=[blk, wspec, sspec],
        out_shape=[jax.ShapeDtypeStruct((B, S, D_POOL), BF16),
                   jax.ShapeDtypeStruct((len(POOL_WINDOWS), POOL_GROUP, POOL_GROUP), F32),
                   jax.ShapeDtypeStruct((1, D_POOL), F32)],
        compiler_params=_params(dimension_semantics=("arbitrary", "arbitrary")),
    )(d, wp, pscale, dcat)


def _cross_softmax(q, k):
    s = lax.dot_general(q, k, NT, preferred_element_type=F32) * (CROSS_DIM ** -0.5)
    e = jnp.exp(s - jnp.max(s, axis=-1, keepdims=True))
    return e * (1.0 / jnp.sum(e, axis=-1, keepdims=True))


def _cross_fwd(name, qc, kv, tq=512):
    B, S, _ = qc.shape
    M = kv.shape[1]
    tq = _tile(S, tq)

    def body(q_ref, k_ref, v_ref, o_ref):
        p = _cross_softmax(q_ref[...], k_ref[...])
        o_ref[...] = jnp.dot(p.astype(BF16), v_ref[...], preferred_element_type=F32).astype(BF16)

    qspec = pl.BlockSpec((None, tq, CROSS_DIM), lambda b, h, i: (b, i, h))
    return pl.pallas_call(
        body, name=name, grid=(B, CROSS_HEADS, S // tq),
        in_specs=[qspec, pl.BlockSpec((None, M, CROSS_DIM), lambda b, h, i: (b, 0, h)),
                  pl.BlockSpec((None, M, CROSS_DIM), lambda b, h, i: (b, 0, CROSS_HEADS + h))],
        out_specs=qspec, out_shape=jax.ShapeDtypeStruct((B, S, D_CROSS), BF16),
        compiler_params=_params(dimension_semantics=("parallel", "parallel", "parallel")),
    )(qc, kv, kv)


def _cross_bwd(name, qc, kv, do, tq=512):
    B, S, _ = qc.shape
    M = kv.shape[1]
    tq = _tile(S, tq)
    nq = S // tq
    scale = CROSS_DIM ** -0.5

    def body(q_ref, k_ref, v_ref, do_ref, dq_ref, dk_ref, dv_ref, dka, dva):
        i = pl.program_id(2)
        q, k, v, dov = q_ref[...], k_ref[...], v_ref[...], do_ref[...]
        p = _cross_softmax(q, k)
        dp = lax.dot_general(dov, v, NT, preferred_element_type=F32)
        ds = ((p * (dp - jnp.sum(p * dp, axis=-1, keepdims=True))) * scale).astype(BF16)
        dq_ref[...] = jnp.dot(ds, k, preferred_element_type=F32).astype(BF16)
        dk = lax.dot_general(ds, q, TN, preferred_element_type=F32)
        dv = lax.dot_general(p.astype(BF16), dov, TN, preferred_element_type=F32)

        @pl.when(i == 0)
        def _():
            dka[...] = dk
            dva[...] = dv

        @pl.when(i > 0)
        def _():
            dka[...] += dk
            dva[...] += dv

        @pl.when(i == nq - 1)
        def _():
            dk_ref[...] = dka[...].astype(BF16)
            dv_ref[...] = dva[...].astype(BF16)

    qspec = pl.BlockSpec((None, tq, CROSS_DIM), lambda b, h, i: (b, i, h))
    kspec = pl.BlockSpec((None, M, CROSS_DIM), lambda b, h, i: (b, 0, h))
    return pl.pallas_call(
        body, name=name, grid=(B, CROSS_HEADS, nq),
        in_specs=[qspec, kspec, pl.BlockSpec((None, M, CROSS_DIM), lambda b, h, i: (b, 0, CROSS_HEADS + h)), qspec],
        out_specs=[qspec, kspec, kspec],
        out_shape=[jax.ShapeDtypeStruct((B, S, D_CROSS), BF16), jax.ShapeDtypeStruct((B, M, D_CROSS), BF16),
                   jax.ShapeDtypeStruct((B, M, D_CROSS), BF16)],
        scratch_shapes=[pltpu.VMEM((M, CROSS_DIM), F32), pltpu.VMEM((M, CROSS_DIM), F32)],
        compiler_params=_params(dimension_semantics=("parallel", "parallel", "arbitrary")),
    )(qc, kv, kv, do)


def _local_step(x, mem, target, small, weights, emit, start_token=None):
    B, S, D = x.shape
    T = B * S
    x2, t2 = x.reshape(T, D), target.reshape(T, D)
    mem2 = mem.reshape(-1, D)
    n_mem = mem.shape[1]
    wts = {}

    hn1 = _rms_fwd("norm_ffn1", x2, small["ffn1_norm"], after=start_token)
    memn = _rms_fwd("norm_mem", mem2, small["mem_norm"])
    bias = _bias_tile("bias_tile", small["rel_bias"])
    wts.update(weights(0, [hn1, memn, bias]))
    g1 = _ffn_gate("ffn1_gate", hn1, wts["ffn1_w_gate"], N_DEV)
    wts.update(weights(1, g1))
    u1, a1 = _ffn_up_act("ffn1_up", hn1, wts["ffn1_w_up"], g1)
    wts.update(weights(2, a1))
    h1 = _ffn_out("ffn1_down", a1, wts["ffn1_w_down"], res=x2, scale=FFN_RES)
    wts.update(weights(3, h1))
    z, hn2 = _mm_nn_cols("mix_in", h1, wts["w_in"], F32, norm=small["mix_norm"])
    z = z.reshape(B, S, -1)
    mixed = _attn_fwd("attn_fwd", z, bias)
    d_pool, mixed = _pool_fwd("pool_fwd", z, wts["w_pool"], small["pool_scale"], mixed)
    cat = mixed.reshape(T, -1)
    h2 = _mm_nn("mix_out", cat, wts["w_out"], F32, res=h1)
    wts.update(weights(4, h2))
    qc, hn3 = _mm_nn("cross_q", h2, wts["w_cq"], BF16, norm=small["cross_norm"])
    kv = _mm_nn("cross_kv", memn, wts["w_ckv"], BF16)
    o = _cross_fwd("cross_fwd", qc.reshape(B, S, -1), kv.reshape(B, n_mem, -1)).reshape(T, -1)
    h3 = _mm_nn_cols("cross_out", o, wts["w_co"], F32, res=h2)
    wts.update(weights(5, h3))
    hn4, g2, u2, a2 = _ffn_up("ffn2_up", h3, small["ffn2_norm"], wts["ffn2_w_gate"], wts["ffn2_w_up"], N_DEV)
    h4 = _ffn_out("ffn2_down", a2, wts["ffn2_w_down"], res=h3, scale=FFN_RES)

    gs = {}
    loss_part, dh4, dh4b, gs["final_norm"] = _loss_and_grad("loss", h4, small["final_norm"], t2)

    def ffn_bwd(tag, dh, dhb, h_in, hn, g, u, a, wg, wu, wd, gain):
        tok = emit({tag + "_w_down": _ffn_dw(tag + "_dwd", a, dhb, scale=FFN_RES)})
        dg, du = _ffn_dact(tag + "_dact", dhb, wd, g, u, after=tok)
        tok = emit({tag + "_w_gate": _ffn_dw(tag + "_dwg", dg, hn)})
        tok = emit({tag + "_w_up": _ffn_dw(tag + "_dwu", du, hn, after=tok)})
        dhn = _ffn_out(tag + "_dhn_g", dg, wg, after=tok)
        dhn = _ffn_out(tag + "_dhn_u", du, wu, res=dhn)
        return _rms_bwd(tag + "_dnorm", h_in, gain, dhn, skip=dh)

    dh3, dh3b, gs["ffn2_norm"] = ffn_bwd("ffn2", dh4, dh4b, h3, hn4, g2, u2, a2, wts["ffn2_w_gate"],
                                         wts["ffn2_w_up"], wts["ffn2_w_down"], small["ffn2_norm"])
    do = _mm_nt_cols("cross_do", dh3b, wts["w_co"], BF16, tn=D_CROSS)
    gw = {"w_co": _mm_tn("cross_dwo", o, dh3b, tm=D_CROSS, col_blocks=N_DEV)}
    dqc, dk, dv = _cross_bwd("cross_bwd", qc.reshape(B, S, -1), kv.reshape(B, n_mem, -1), do.reshape(B, S, -1))
    dqc = dqc.reshape(T, -1)
    dkv = jnp.concatenate([dk, dv], axis=-1).reshape(B * n_mem, -1)
    gw["w_cq"] = _mm_tn("cross_dwq", hn3, dqc, tn=D_CROSS)
    gw["w_ckv"] = _mm_tn("cross_dwkv", memn, dkv)
    tok = emit(gw)
    dhn3 = _mm_nt("cross_dhn", dqc, wts["w_cq"], F32, after=tok)
    dmemn = _mm_nt("cross_dmem", dkv, wts["w_ckv"], F32, tm=512)
    _, _, gs["mem_norm"] = _rms_bwd("mem_dnorm", mem2, small["mem_norm"], dmemn)
    dh2, dh2b, gs["cross_norm"] = _rms_bwd("cross_dnorm", h2, small["cross_norm"], dhn3, skip=dh3)
    dcat = _mm_nt("mix_dcat", dh2b, wts["w_out"], BF16)
    gw = {"w_out": _mm_tn("mix_dwout", cat, dh2b)}
    dcat3 = dcat.reshape(B, S, -1)
    dq, dkk, dvv, dbias = _attn_bwd("attn_bwd", z, bias, dcat3)
    du, gw["w_pool"], gs["pool_scale"] = _pool_bwd("pool_bwd", d_pool, wts["w_pool"], small["pool_scale"], dcat3)
    gs["rel_bias"] = _rel_grad_from_diagonals(_bias_grad("bias_grad", dbias)[:, 0, :])
    dz = jnp.concatenate([dq, dkk, dvv, du], axis=-1).reshape(T, -1)
    gw["w_in"] = _mm_tn("mix_dwin", hn2, dz, col_blocks=N_DEV)
    tok = emit(gw)
    dhn2 = _mm_nt_cols("mix_dhn", dz, wts["w_in"], F32, after=tok)
    dh1, dh1b, gs["mix_norm"] = _rms_bwd("mix_dnorm", h1, small["mix_norm"], dhn2, skip=dh2)
    dx, _, gs["ffn1_norm"] = ffn_bwd("ffn1", dh1, dh1b, x2, hn1, g1, u1, a1, wts["ffn1_w_gate"],
                                     wts["ffn1_w_up"], wts["ffn1_w_down"], small["ffn1_norm"])
    return loss_part, dx.reshape(B, S, D), gs


def _position():
    return lax.axis_index("x"), lax.axis_index("y"), lax.axis_index("c")


def _index(p):
    return 4 * p[0] + 2 * p[1] + p[2]


HBM_SPEC = pl.BlockSpec(memory_space=pltpu.HBM)
SEM_SPEC = pl.BlockSpec(memory_space=pltpu.SEMAPHORE)
ANY_SPEC = pl.BlockSpec(memory_space=pl.ANY)
ORDERED_EFFECT = pltpu.SideEffectType.DATAFLOW_SIDE_EFFECTING


N_COPIES = {"grads": N_DEV - 1, "spread": 4, "relay": 3}
N_SEMS = {"grads": N_DEV, "spread": 5, "relay": 3}


def _copies(pattern, srcs, lands, send, recv, base=0):
    x, y, c = _position()
    me, sibling = _index((x, y, c)), (x, y, 1 - c)
    chips = [(1 - x, y), (x, 1 - y), (1 - x, 1 - y)]
    if pattern == "grads":
        targets = [(x ^ (k >> 2), y ^ ((k >> 1) & 1), c ^ (k & 1)) for k in range(1, N_DEV)]
    else:
        targets = [sibling] + [(*chip, c) for chip in chips]
    per, slots, out = N_COPIES[pattern], N_SEMS[pattern], []
    for a in range(len(lands)):
        for k in range(per):
            if pattern == "relay":
                src = dst = lands[a].at[_index((*chips[k], c))]
                to = sibling
            else:
                to = targets[k]
                src = srcs[a].at[_index(to)] if pattern == "grads" else srcs[a]
                dst = lands[a].at[me]
            slot = base + a * slots + k
            out.append(pltpu.make_async_remote_copy(src_ref=src, dst_ref=dst, send_sem=send.at[slot],
                                                    recv_sem=recv.at[slot], device_id=to, device_id_type=MESH))
    return out


def _own_copies(pattern, srcs, lands, send, base=0):
    if pattern == "relay":
        return []
    x, y, c = _position()
    me = _index((x, y, c))
    slots = N_SEMS[pattern]
    return [pltpu.make_async_copy(srcs[a].at[me] if pattern == "grads" else srcs[a], lands[a].at[me],
                                  send.at[base + a * slots + slots - 1]) for a in range(len(lands))]


def _comm(name, srcs, lands, wait=None, start=None, after=None):
    after = [] if after is None else list(after) if isinstance(after, (list, tuple)) else [after]
    ns, nl = len(srcs), len(lands)
    na = ns + nl
    arrays = list(srcs) + list(lands)
    n_wait = 2 if wait else 0
    n_start = 2 if start else 0

    def body(*refs):
        ins, lnd = refs[:ns], refs[ns:na]
        if wait:
            base = wait[3] if len(wait) > 3 else 0
            for cp in _copies(wait[0], ins, lnd, refs[na], refs[na + 1], base):
                cp.wait_send()
                cp.wait_recv()
            for cp in _own_copies(wait[0], ins, lnd, refs[na], base):
                cp.wait()
        if start:
            outs = refs[na + n_wait + len(after):]
            for cp in _copies(start, ins, lnd, outs[0], outs[1]) + _own_copies(start, ins, lnd, outs[0]):
                cp.start()
            refs[-1][...] = jnp.zeros((8, 128), F32)

    out_shape, out_specs = [], []
    if start:
        sems = pltpu.SemaphoreType.DMA((nl * N_SEMS[start],))
        out_shape += [sems, sems]
        out_specs += [SEM_SPEC, SEM_SPEC]
    out_shape += [pltpu.HBM(a.shape, a.dtype) for a in arrays]
    out_specs += [HBM_SPEC] * na
    if start:
        out_shape.append(jax.ShapeDtypeStruct((8, 128), F32))
        out_specs.append(pl.BlockSpec(memory_space=pltpu.VMEM))
    operands = [pltpu.with_memory_space_constraint(a, pltpu.HBM) for a in arrays]
    operands += list(wait[1:3]) if wait else []
    operands += after
    res = pl.pallas_call(
        body, name=name, out_shape=out_shape, out_specs=out_specs,
        in_specs=[HBM_SPEC] * na + [SEM_SPEC] * n_wait + [ANY_SPEC] * len(after),
        input_output_aliases={i: n_start + i for i in range(na)},
        compiler_params=pltpu.CompilerParams(has_side_effects=ORDERED_EFFECT),
    )(*operands)
    res = list(res)
    thru = res[n_start:n_start + na]
    return thru[:ns], thru[ns:], (tuple(res[:2]) if start else None), (res[-1] if start else None)


def _adamw_math(w, g, m, v):
    m = ADAM_B1 * m + (1.0 - ADAM_B1) * g
    v = ADAM_B2 * v + (1.0 - ADAM_B2) * (g * g)
    m_hat = m / (1.0 - ADAM_B1 ** ADAM_STEP)
    v_hat = v / (1.0 - ADAM_B2 ** ADAM_STEP)
    delta = -ADAM_LR * (m_hat / (jnp.sqrt(v_hat) + ADAM_EPS) + ADAM_WD * w)
    return delta, m, v


def _adamw(name, parts, w, m, v, tr=128):
    R, C = w.shape
    tr = _tile(R, tr)

    def body(p_ref, w_ref, m_ref, v_ref, g_out, d_out, m_out, v_out):
        g = p_ref[0].astype(F32)
        for d in range(1, N_DEV):
            g = g + p_ref[d].astype(F32)
        g_out[...] = g
        d_out[...], m_out[...], v_out[...] = _adamw_math(w_ref[...], g, m_ref[...], v_ref[...])

    row = pl.BlockSpec((tr, C), lambda i: (i, 0))
    out = jax.ShapeDtypeStruct((R, C), F32)
    return pl.pallas_call(
        body, name=name, grid=(R // tr,),
        in_specs=[pl.BlockSpec((N_DEV, tr, C), lambda i: (0, i, 0)), row, row, row],
        out_specs=[row] * 4, out_shape=[out] * 4,
        compiler_params=_params(dimension_semantics=("parallel",)),
    )(parts, w, m, v)


def _small_allreduce_adamw(name, g, w, m, v):
    R = g.shape[0]

    def body(g_ref, w_ref, m_ref, v_ref, g_out, d_out, m_out, v_out, land, send, recv):
        x, y, c = _position()
        me = _index((x, y, c))
        land[me] = g_ref[...]
        copies = []
        for k in range(1, N_DEV):
            peer = (x ^ (k >> 2), y ^ ((k >> 1) & 1), c ^ (k & 1))
            copies.append(pltpu.make_async_remote_copy(
                src_ref=g_ref, dst_ref=land.at[me], send_sem=send.at[k - 1], recv_sem=recv.at[k - 1],
                device_id=peer, device_id_type=MESH))
        for cp in copies:
            cp.start()
        for cp in copies:
            cp.wait()
        total = land[0]
        for d in range(1, N_DEV):
            total = total + land[d]
        g_out[...] = total
        d_out[...], m_out[...], v_out[...] = _adamw_math(w_ref[...], total, m_ref[...], v_ref[...])

    vm = pl.BlockSpec(memory_space=pltpu.VMEM)
    out = jax.ShapeDtypeStruct((R, 128), F32)
    return pl.pallas_call(
        body, name=name, in_specs=[vm] * 4, out_specs=[vm] * 4, out_shape=[out] * 4,
        scratch_shapes=[pltpu.VMEM((N_DEV, R, 128), F32), pltpu.SemaphoreType.DMA((7,)),
                        pltpu.SemaphoreType.DMA((7,))],
    )(g, w, m, v)


BIG = ("ffn1_w_gate", "ffn1_w_up", "ffn1_w_down", "w_in", "w_pool", "w_out", "w_cq", "w_ckv", "w_co",
       "ffn2_w_gate", "ffn2_w_up", "ffn2_w_down")
SMALL = ("ffn1_norm", "mix_norm", "rel_bias", "pool_scale", "cross_norm", "mem_norm", "ffn2_norm", "final_norm")
ORDER = ("ffn1_norm", "ffn1_w_gate", "ffn1_w_up", "ffn1_w_down", "mix_norm", "w_in", "rel_bias", "w_pool",
         "pool_scale", "w_out", "cross_norm", "mem_norm", "w_cq", "w_ckv", "w_co", "ffn2_norm", "ffn2_w_gate",
         "ffn2_w_up", "ffn2_w_down", "final_norm")
TRANSPOSED = ("ffn1_w_gate", "ffn1_w_up", "ffn2_w_gate", "ffn2_w_up")
ROW_SHARDED = TRANSPOSED + ("ffn1_w_down", "ffn2_w_down", "w_out", "w_cq", "w_ckv")
GATHER_GROUPS = (("ffn1_w_gate",), ("ffn1_w_up",), ("ffn1_w_down",), ("w_in", "w_pool", "w_out"),
                 ("w_cq", "w_ckv", "w_co"), ("ffn2_w_gate", "ffn2_w_up", "ffn2_w_down"))
RELAY_BEFORE_USE = ((0,), (1,), (2,), (3,), (4, 5), ())


def _pack(arrays):
    flat = jnp.concatenate([a.reshape(-1) for a in arrays])
    rows = -(-flat.shape[0] // 1024) * 8
    return jnp.pad(flat, (0, rows * 128 - flat.shape[0])).reshape(rows, 128)


def _unpack(packed, like):
    flat, out, at = packed.reshape(-1), [], 0
    for a in like:
        out.append(flat[at:at + a.size].reshape(a.shape))
        at += a.size
    return out


def _shard2d(a):
    a = a[0]
    return a.reshape(-1, a.shape[-1])


def kernel(x, mem, ffn1_norm, ffn1_w_gate, ffn1_w_up, ffn1_w_down, mix_norm, w_in, rel_bias, w_pool, pool_scale, w_out, cross_norm, mem_norm, w_cq, w_ckv, w_co, ffn2_norm, ffn2_w_gate, ffn2_w_up, ffn2_w_down, final_norm, loss_target, m_ffn1_norm, m_ffn1_w_gate, m_ffn1_w_up, m_ffn1_w_down, m_mix_norm, m_w_in, m_rel_bias, m_w_pool, m_pool_scale, m_w_out, m_cross_norm, m_mem_norm, m_w_cq, m_w_ckv, m_w_co, m_ffn2_norm, m_ffn2_w_gate, m_ffn2_w_up, m_ffn2_w_down, m_final_norm, v_ffn1_norm, v_ffn1_w_gate, v_ffn1_w_up, v_ffn1_w_down, v_mix_norm, v_w_in, v_rel_bias, v_w_pool, v_pool_scale, v_w_out, v_cross_norm, v_mem_norm, v_w_cq, v_w_ckv, v_w_co, v_ffn2_norm, v_ffn2_w_gate, v_ffn2_w_up, v_ffn2_w_down, v_final_norm):
    args = dict(locals())
    def view(n, a):
        return a.transpose(0, 2, 1) if n in TRANSPOSED else a

    w_in_ = {n: view(n, args[n]) for n in ORDER}
    m_in = {n: view(n, args["m_" + n]) for n in ORDER}
    v_in = {n: view(n, args["v_" + n]) for n in ORDER}

    me = 4 * lax.axis_index("x") + 2 * lax.axis_index("y") + lax.axis_index("c")
    n_g, rows = len(POOL_WINDOWS), POOL_GROUP // N_DEV

    def landing(block_shape, dtype):
        return lax.empty((N_DEV,) + tuple(block_shape), dtype)

    gathers, tok = [], None
    for first in (True, False):
        groups = GATHER_GROUPS[:1] if first else GATHER_GROUPS[1:]
        shards = [_shard2d(w_in_[n]) for group in groups for n in group]
        shards = [(s if tok is None else s + tok[0, 0]).astype(BF16) for s in shards]
        srcs, lands, sems, tok = _comm("gather_start_%d" % (not first), shards,
                                       [landing(s.shape, BF16) for s in shards], start="spread", after=tok)
        at = 0
        for group in groups:
            gathers.append((srcs[at:at + len(group)], lands[at:at + len(group)], sems + (at * N_SEMS["spread"],)))
            at += len(group)

    def weights(gi, after):
        for ri in RELAY_BEFORE_USE[gi]:
            srcs, lands, sems = gathers[ri]
            _, lands, sems, after = _comm("gather_relay_%d" % ri, srcs, lands, wait=("spread",) + sems,
                                          start="relay", after=after)
            gathers[ri] = (None, lands, sems)
        _, lands, sems = gathers[gi]
        _, lands, _, _ = _comm("gather_finish_%d" % gi, [], lands, wait=("relay",) + sems, after=after)
        out = {}
        for n, full in zip(GATHER_GROUPS[gi], lands):
            if n == "w_pool":
                full = full.reshape(N_DEV, n_g, rows, POOL_GROUP).transpose(1, 0, 2, 3).reshape(n_g, POOL_GROUP, POOL_GROUP)
            out[n] = full.reshape(-1, full.shape[-1]) if n in ROW_SHARDED else full
        return out

    scatters = []

    def emit(gw):
        names = list(gw)
        stacks = []
        for n in names:
            g = gw[n]
            if n == "w_pool":
                g = g.reshape(n_g, N_DEV, rows, POOL_GROUP).transpose(1, 0, 2, 3).astype(BF16)
            stacks.append(g.reshape((N_DEV,) + _shard2d(w_in_[n]).shape))
        lands = [landing(s.shape[1:], s.dtype) for s in stacks]
        srcs, lands, sems, token = _comm("grads_start_%d" % len(scatters), stacks, lands, start="grads")
        scatters.append((names, srcs, lands, sems))
        return token

    small = {n: w_in_[n].reshape(1, -1) for n in SMALL if n != "rel_bias"}
    small["rel_bias"] = rel_bias[0]
    loss_part, grad_x, gs = _local_step(x, mem, loss_target, small, weights, emit, start_token=tok)
    loss = lax.psum(loss_part[0, 0], ("x", "y", "c"))

    grad, delta, new_m, new_v = {}, {}, {}, {}
    after = grad_x
    for si, (names, srcs, lands, sems) in enumerate(scatters):
        _, landed, _, _ = _comm("grads_finish_%d" % si, srcs, lands, wait=("grads",) + sems, after=after)
        for n, parts in zip(names, landed):
            res = _adamw("adamw_" + n, parts, _shard2d(w_in_[n]), _shard2d(m_in[n]), _shard2d(v_in[n]))
            grad[n], delta[n], new_m[n], new_v[n] = [view(n, r.reshape(w_in_[n].shape)) for r in res]
        after = res[0]

    like = [w_in_[n] for n in SMALL]
    gs["rel_bias"] = gs["rel_bias"].reshape(rel_bias.shape)
    res = _small_allreduce_adamw("small_params", _pack([gs[n] for n in SMALL]), _pack(like),
                                 _pack([m_in[n] for n in SMALL]), _pack([v_in[n] for n in SMALL]))
    for d, packed in zip((grad, delta, new_m, new_v), res):
        for n, a in zip(SMALL, _unpack(packed, like)):
            d[n] = a
    return (loss, grad_x, *[grad[n] for n in ORDER], *[delta[n] for n in ORDER],
            *[new_m[n] for n in ORDER], *[new_v[n] for n in ORDER])
```

```python
import jax
import jax.numpy as jnp
from jax import lax
from jax.experimental import pallas as pl
from jax.experimental.pallas import tpu as pltpu

F32 = jnp.float32
BF16 = jnp.bfloat16

N_DEV = 8
EPS = 1e-6
NEG_INF = -1e30
CHUNK = 64
LEFT_CHUNKS = 8
PAD = LEFT_CHUNKS * CHUNK
QBLK = 4 * CHUNK
KBAND = PAD + QBLK
REL_CLIP = 128
ATTN_HEADS = 16
HEAD_DIM = 64
D_ATTN = ATTN_HEADS * HEAD_DIM
POOL_WINDOWS = (2, 4, 8, 16)
POOL_GROUP = 256
D_POOL = len(POOL_WINDOWS) * POOL_GROUP
CROSS_HEADS = 4
CROSS_DIM = 128
D_CROSS = CROSS_HEADS * CROSS_DIM
FFN_RES = 0.5
ADAM_LR, ADAM_B1, ADAM_B2, ADAM_EPS, ADAM_WD, ADAM_STEP = 0.001, 0.9, 0.999, 1e-08, 0.01, 10

NN = (((1,), (0,)), ((), ()))
NT = (((1,), (1,)), ((), ()))
TN = (((0,), (0,)), ((), ()))
MESH = pl.DeviceIdType.MESH
VMEM_LIMIT = 56 * 1024 * 1024


def _params(**kw):
    return pltpu.CompilerParams(vmem_limit_bytes=VMEM_LIMIT, **kw)


def _bf(v):
    return v if v.dtype == BF16 else v.astype(BF16)


WHOLE = ((Ellipsis,), (Ellipsis,))


def _rms(xv, gain):
    return (xv * lax.rsqrt(jnp.mean(xv * xv, axis=-1, keepdims=True) + EPS)) * gain


def _gemm(name, a, a_spec, b, b_spec, dims, grid, outs, chunks=(WHOLE,), extras=(), epilogue=None, after=None,
          norm=None):
    nex, nout = len(extras), len(outs)
    first_out = 2 + nex + (after is not None) + (norm is not None)

    def body(*refs):
        a_ref, b_ref = refs[:2]
        if norm is not None:
            hn_out, a_ref = refs[first_out + nout], refs[-1]

            @pl.when(pl.program_id(1) == 0)
            def _():
                hn = _rms(refs[0][...], refs[first_out - 1][...]).astype(BF16)
                a_ref[...] = hn
                hn_out[...] = hn

        total = None
        for ia, ib in chunks:
            d = lax.dot_general(_bf(a_ref[ia]), _bf(b_ref[ib]), dims, preferred_element_type=F32)
            total = d if total is None else total + d
        vals = epilogue(total, *[e[...] for e in refs[2:2 + nex]]) if epilogue is not None else (total,)
        for r, v in zip(refs[first_out:first_out + nout], vals):
            r[...] = v.astype(r.dtype)

    operands = [a, b] + [x for x, _, _ in extras]
    in_specs = [pl.BlockSpec(*a_spec), pl.BlockSpec(*b_spec)] + [pl.BlockSpec(blk, m) for _, blk, m in extras]
    if after is not None:
        operands.append(after)
        in_specs.append(pl.BlockSpec(after.shape, lambda i, j: (0, 0)))
    out_specs = [pl.BlockSpec(blk, m) for _, _, blk, m in outs]
    out_shape = [jax.ShapeDtypeStruct(s, d) for s, d, _, _ in outs]
    scratch = []
    if norm is not None:
        operands.append(norm)
        in_specs.append(pl.BlockSpec(norm.shape, lambda i, j: (0, 0)))
        out_specs.append(pl.BlockSpec(*a_spec))
        out_shape.append(jax.ShapeDtypeStruct(a.shape, BF16))
        scratch.append(pltpu.VMEM(a_spec[0], BF16))
    res = pl.pallas_call(
        body, name=name, grid=grid, in_specs=in_specs, out_specs=out_specs, out_shape=out_shape,
        scratch_shapes=scratch,
        compiler_params=_params(dimension_semantics=("parallel", "arbitrary" if norm is not None else "parallel")),
    )(*operands)
    return res[0] if len(res) == 1 else res


def _tile(n, want):
    for t in range(min(n, want), 15, -1):
        if n % t == 0 and t % 16 == 0:
            return t
    return n


def _mm_nn(name, a, b, out_dtype, res=None, tm=1024, tn=1024, norm=None):
    M, K = a.shape
    N = b.shape[1]
    tm, tn = _tile(M, tm), _tile(N, tn)
    extras = [] if res is None else [(res, (tm, tn), lambda i, j: (i, j))]
    epi = None if res is None else (lambda t, r: (r + t,))
    return _gemm(name, a, ((tm, K), lambda i, j: (i, 0)), b, ((K, tn), lambda i, j: (0, j)), NN,
                 (M // tm, N // tn), [((M, N), out_dtype, (tm, tn), lambda i, j: (i, j))], extras=extras, epilogue=epi,
                 norm=norm)


def _mm_nn_cols(name, a, bs, out_dtype, res=None, tm=1024, norm=None):
    M, K = a.shape
    nb, _, w = bs.shape
    tm = _tile(M, tm)
    extras = [] if res is None else [(res, (tm, w), lambda i, j: (i, j))]
    epi = None if res is None else (lambda t, r: (r + t,))
    return _gemm(name, a, ((tm, K), lambda i, j: (i, 0)), bs, ((None, K, w), lambda i, j: (j, 0, 0)), NN,
                 (M // tm, nb), [((M, nb * w), out_dtype, (tm, w), lambda i, j: (i, j))], extras=extras, epilogue=epi,
                 norm=norm)


def _mm_nt(name, a, b, out_dtype, tm=1024, tn=1024, after=None):
    M, K = a.shape
    N = b.shape[0]
    tm, tn = _tile(M, tm), _tile(N, tn)
    return _gemm(name, a, ((tm, K), lambda i, j: (i, 0)), b, ((tn, K), lambda i, j: (j, 0)), NT,
                 (M // tm, N // tn), [((M, N), out_dtype, (tm, tn), lambda i, j: (i, j))], after=after)


def _mm_nt_cols(name, a, bs, out_dtype, tm=1024, tn=512, after=None):
    M = a.shape[0]
    nb, N, w = bs.shape
    tm, tn = _tile(M, tm), _tile(N, tn)
    chunks = [((slice(None), pl.ds(c * w, w)), (c,)) for c in range(nb)]
    return _gemm(name, a, ((tm, nb * w), lambda i, j: (i, 0)), bs, ((nb, tn, w), lambda i, j: (0, j, 0)), NT,
                 (M // tm, N // tn), [((M, N), out_dtype, (tm, tn), lambda i, j: (i, j))], chunks, after=after)


def _mm_tn(name, a, b, tm=1024, tn=1024, col_blocks=None):
    T, Ka = a.shape
    Nb = b.shape[1]
    tm = _tile(Ka, tm)
    if col_blocks is None:
        tn = _tile(Nb, tn)
        out = ((Ka, Nb), BF16, (tm, tn), lambda i, j: (i, j))
    else:
        tn = Nb // col_blocks
        out = ((col_blocks, Ka, tn), BF16, (None, tm, tn), lambda i, j: (j, i, 0))
    return _gemm(name, a, ((T, tm), lambda i, j: (0, i)), b, ((T, tn), lambda i, j: (0, j)), TN,
                 (Ka // tm, Nb // tn), [out])


def _ffn_out(name, a, wt, res=None, scale=1.0, tm=1024, tn=512, after=None):
    nb, M, w = a.shape
    N = wt.shape[1]
    tm, tn = _tile(M, tm), _tile(N, tn)
    chunks = [((c,), (pl.ds(c * w, w),)) for c in range(nb)]
    extras = [] if res is None else [(res, (tm, tn), lambda i, j: (i, j))]
    epi = None if res is None else (lambda t, r: (r + scale * t,))
    return _gemm(name, a, ((nb, tm, w), lambda i, j: (0, i, 0)), wt, ((nb * w, tn), lambda i, j: (0, j)), NN,
                 (M // tm, N // tn), [((M, N), F32, (tm, tn), lambda i, j: (i, j))], chunks, extras, epi, after)


def _ffn_dact(name, dhb, wd, g, u, tm=1024, after=None):
    M, K = dhb.shape
    nb, _, w = g.shape
    tm = _tile(M, tm)
    tr = _tile(tm, 256)
    tokens = [] if after is None else [after]

    def body(dh_ref, wd_ref, g_ref, u_ref, *rest):
        dg_ref, du_ref = rest[-2:]
        pieces = [pl.ds(r * tr, tr) for r in range(tm // tr)]

        def product(rows):
            return lax.dot_general(dh_ref[rows, :], wd_ref[...], NT, preferred_element_type=F32)

        results = []
        dact = product(pieces[0])
        for n, rows in enumerate(pieces):
            ahead = product(pieces[n + 1]) if n + 1 < len(pieces) else None
            results.append(_swiglu_bwd(dact, g_ref[rows, :], u_ref[rows, :]))
            dact = ahead
        for rows, (dg, du) in zip(pieces, results):
            dg_ref[rows, :] = dg.astype(BF16)
            du_ref[rows, :] = du.astype(BF16)

    hid = pl.BlockSpec((None, tm, w), lambda i, j: (j, i, 0))
    return pl.pallas_call(
        body, name=name, grid=(M // tm, nb),
        in_specs=[pl.BlockSpec((tm, K), lambda i, j: (i, 0)), pl.BlockSpec((w, K), lambda i, j: (j, 0)), hid, hid]
        + [pl.BlockSpec(t.shape, lambda i, j: (0, 0)) for t in tokens],
        out_specs=[hid, hid], out_shape=[jax.ShapeDtypeStruct((nb, M, w), BF16)] * 2,
        compiler_params=_params(dimension_semantics=("parallel", "parallel")),
    )(dhb, wd, g, u, *tokens)


def _ffn_dw(name, a, b, scale=1.0, tn=1024, after=None):
    nb, T, w = a.shape
    N = b.shape[1]
    tn = _tile(N, tn)
    epi = None if scale == 1.0 else (lambda t: (t * scale,))
    return _gemm(name, a, ((None, T, w), lambda i, j: (i, 0, 0)), b, ((T, tn), lambda i, j: (0, j)), TN,
                 (nb, N // tn), [((nb * w, N), BF16, (w, tn), lambda i, j: (i, j))], epilogue=epi, after=after)


def _ffn_gate(name, hn, wgt, nb, tm=1024):
    M, K = hn.shape
    w = wgt.shape[0] // nb
    tm = _tile(M, tm)
    return _gemm(name, hn, ((tm, K), lambda i, j: (i, 0)), wgt, ((w, K), lambda i, j: (j, 0)), NT,
                 (M // tm, nb), [((nb, M, w), BF16, (None, tm, w), lambda i, j: (j, i, 0))])


def _ffn_up_act(name, hn, wut, g, tm=1024):
    M, K = hn.shape
    nb, _, w = g.shape
    tm = _tile(M, tm)
    hid = ((None, tm, w), lambda i, j: (j, i, 0))

    def epilogue(u, gate):
        gate = gate.astype(F32)
        return u, gate * jax.nn.sigmoid(gate) * u

    return _gemm(name, hn, ((tm, K), lambda i, j: (i, 0)), wut, ((w, K), lambda i, j: (j, 0)), NT,
                 (M // tm, nb), [((nb, M, w), BF16) + hid] * 2, extras=[(g,) + hid], epilogue=epilogue)


def _ffn_up(name, h, gain, wgt, wut, nb, tm=1024):
    M, K = h.shape
    w = wgt.shape[0] // nb
    tm = _tile(M, tm)

    def body(h_ref, gain_ref, g_ref, u_ref, hn_ref, og, ou, oa, a_ref):
        @pl.when(pl.program_id(1) == 0)
        def _():
            hn = _rms(h_ref[...], gain_ref[...]).astype(BF16)
            a_ref[...] = hn
            hn_ref[...] = hn

        a = a_ref[...]
        g = lax.dot_general(a, g_ref[...], NT, preferred_element_type=F32)
        u = lax.dot_general(a, u_ref[...], NT, preferred_element_type=F32)
        og[...] = g.astype(BF16)
        ou[...] = u.astype(BF16)
        oa[...] = (g * jax.nn.sigmoid(g) * u).astype(BF16)

    rows = pl.BlockSpec((tm, K), lambda i, j: (i, 0))
    wspec = pl.BlockSpec((w, K), lambda i, j: (j, 0))
    ospec = pl.BlockSpec((None, tm, w), lambda i, j: (j, i, 0))
    return pl.pallas_call(
        body, name=name, grid=(M // tm, nb),
        in_specs=[rows, pl.BlockSpec((1, K), lambda i, j: (0, 0)), wspec, wspec],
        out_specs=[rows] + [ospec] * 3,
        out_shape=[jax.ShapeDtypeStruct((M, K), BF16)] + [jax.ShapeDtypeStruct((nb, M, w), BF16)] * 3,
        scratch_shapes=[pltpu.VMEM((tm, K), BF16)],
        compiler_params=_params(dimension_semantics=("parallel", "arbitrary")),
    )(h, gain, wgt, wut)


def _swiglu_bwd(dact, g, u):
    g = g.astype(F32)
    u = u.astype(F32)
    sig = jax.nn.sigmoid(g)
    silu = g * sig
    d = FFN_RES * dact
    return d * u * (sig * (1.0 + g * (1.0 - sig))), d * silu


def _rms_fwd(name, x, gain, tr=512, after=None):
    R, D = x.shape
    tr = _tile(R, tr)

    def body(x_ref, g_ref, *rest):
        xv = x_ref[...]
        y = xv * lax.rsqrt(jnp.mean(xv * xv, axis=-1, keepdims=True) + EPS)
        rest[-1][...] = (y * g_ref[...]).astype(BF16)

    tokens = [] if after is None else [after]
    return pl.pallas_call(
        body, name=name, grid=(R // tr,),
        in_specs=[pl.BlockSpec((tr, D), lambda i: (i, 0)), pl.BlockSpec((1, D), lambda i: (0, 0))]
        + [pl.BlockSpec(t.shape, lambda i: (0, 0)) for t in tokens],
        out_specs=pl.BlockSpec((tr, D), lambda i: (i, 0)), out_shape=jax.ShapeDtypeStruct((R, D), BF16),
        compiler_params=_params(dimension_semantics=("parallel",)),
    )(x, gain, *tokens)


def _rms_bwd_math(xv, gain, dy):
    rstd = lax.rsqrt(jnp.mean(xv * xv, axis=-1, keepdims=True) + EPS)
    xhat = xv * rstd
    dxh = dy * gain
    dx = rstd * (dxh - xhat * jnp.mean(dxh * xhat, axis=-1, keepdims=True))
    return dx, jnp.sum(dy * xhat, axis=0, keepdims=True)


def _rms_bwd(name, x, gain, dy, skip=None, tr=256):
    R, D = x.shape
    tr = _tile(R, tr)
    has_skip = skip is not None

    def body(*refs):
        x_ref, g_ref, dy_ref = refs[:3]
        dx_ref, dxb_ref, dg_ref = refs[-3:]
        dx, dg = _rms_bwd_math(x_ref[...], g_ref[...], dy_ref[...].astype(F32))
        if has_skip:
            dx = dx + refs[3][...]
        dx_ref[...] = dx
        dxb_ref[...] = dx.astype(BF16)

        @pl.when(pl.program_id(0) == 0)
        def _():
            dg_ref[...] = dg

        @pl.when(pl.program_id(0) > 0)
        def _():
            dg_ref[...] += dg

    row = pl.BlockSpec((tr, D), lambda i: (i, 0))
    vec = pl.BlockSpec((1, D), lambda i: (0, 0))
    return pl.pallas_call(
        body, name=name, grid=(R // tr,),
        in_specs=[row, vec, row] + ([row] if has_skip else []),
        out_specs=[row, row, vec],
        out_shape=[jax.ShapeDtypeStruct((R, D), F32), jax.ShapeDtypeStruct((R, D), BF16),
                   jax.ShapeDtypeStruct((1, D), F32)],
        compiler_params=_params(dimension_semantics=("arbitrary",)),
    )(*([x, gain, dy] + ([skip] if has_skip else [])))


def _loss_and_grad(name, h, gain, target, tr=256):
    R, D = h.shape
    tr = _tile(R, tr)

    def body(h_ref, g_ref, t_ref, loss_ref, dh_ref, dhb_ref, dg_ref):
        hv, gain_v = h_ref[...], g_ref[...]
        y = (hv * lax.rsqrt(jnp.mean(hv * hv, axis=-1, keepdims=True) + EPS)) * gain_v
        err = y - t_ref[...]
        part = jnp.full((8, 128), 0.5 * jnp.sum(jnp.mean(err * err, axis=-1, keepdims=True)), F32)
        dh, dg = _rms_bwd_math(hv, gain_v, err * (1.0 / D))
        dh_ref[...] = dh
        dhb_ref[...] = dh.astype(BF16)

        @pl.when(pl.program_id(0) == 0)
        def _():
            dg_ref[...] = dg
            loss_ref[...] = part

        @pl.when(pl.program_id(0) > 0)
        def _():
            dg_ref[...] += dg
            loss_ref[...] += part

    row = pl.BlockSpec((tr, D), lambda i: (i, 0))
    vec = pl.BlockSpec((1, D), lambda i: (0, 0))
    return pl.pallas_call(
        body, name=name, grid=(R // tr,), in_specs=[row, vec, row],
        out_specs=[pl.BlockSpec((8, 128), lambda i: (0, 0)), row, row, vec],
        out_shape=[jax.ShapeDtypeStruct((8, 128), F32), jax.ShapeDtypeStruct((R, D), F32),
                   jax.ShapeDtypeStruct((R, D), BF16), jax.ShapeDtypeStruct((1, D), F32)],
        compiler_params=_params(dimension_semantics=("arbitrary",)),
    )(h, gain, target)


def _bias_tile(name, rel):
    width = KBAND + QBLK
    sat = rel[:, 2 * REL_CLIP:]
    n_left = PAD - REL_CLIP + 1
    row0 = jnp.concatenate([jnp.broadcast_to(sat, (ATTN_HEADS, n_left)), rel[:, :2 * REL_CLIP][:, ::-1],
                            jnp.broadcast_to(sat, (ATTN_HEADS, width - n_left - 2 * REL_CLIP))], axis=1)

    def body(e_ref, o_ref):
        rows = pltpu.roll(jnp.broadcast_to(e_ref[...], (QBLK, width)), 0, 1, stride=1, stride_axis=0)
        i = lax.broadcasted_iota(jnp.int32, (QBLK, KBAND), 0) // CHUNK
        j = lax.broadcasted_iota(jnp.int32, (QBLK, KBAND), 1) // CHUNK
        o_ref[...] = jnp.where((j >= i) & (j <= i + LEFT_CHUNKS), rows[:, :KBAND], NEG_INF)

    return pl.pallas_call(
        body, name=name, grid=(ATTN_HEADS,),
        in_specs=[pl.BlockSpec((None, 1, width), lambda h: (h, 0, 0))],
        out_specs=pl.BlockSpec((None, QBLK, KBAND), lambda h: (h, 0, 0)),
        out_shape=jax.ShapeDtypeStruct((ATTN_HEADS, QBLK, KBAND), F32),
        compiler_params=_params(dimension_semantics=("parallel",)),
    )(row0.reshape(ATTN_HEADS, 1, width))


ATTN_SCALE = HEAD_DIM ** -0.5
ROW_PIECES = 1


def _stack_heads(x, first):
    zero = jnp.zeros_like(x)
    return jnp.concatenate([jnp.where(first, x, zero), jnp.where(first, zero, x)], axis=0)


def _band_softmax(q_half_scaled, kb, bias, left_mask):
    s = lax.dot_general(q_half_scaled, kb, NT, preferred_element_type=F32) + bias + left_mask
    e = jnp.exp(s - jnp.max(s, axis=-1, keepdims=True))
    return e * (1.0 / jnp.sum(e, axis=-1, keepdims=True))


def _left_mask(qb):
    kpos = qb * QBLK - PAD + lax.broadcasted_iota(jnp.int32, (1, KBAND), 1)
    return jnp.where(kpos >= 0, 0.0, NEG_INF).astype(F32)


def _fill_padded(dst, src, S):
    dst[pl.ds(0, PAD), :] = jnp.zeros((PAD, dst.shape[1]), dst.dtype)
    dst[pl.ds(PAD, S), :] = src[...].astype(dst.dtype)


def _attn_fwd(name, z, bias):
    B, S, _ = z.shape
    nh2 = ATTN_HEADS // 2

    def body(q_ref, k_ref, v_ref, b_ref, o_ref, kp, vp):
        qb = pl.program_id(2)

        @pl.when(qb == 0)
        def _():
            _fill_padded(kp, k_ref, S)
            _fill_padded(vp, v_ref, S)

        start = pl.multiple_of(qb * QBLK, QBLK)
        kb, vb = kp[pl.ds(start, KBAND), :], vp[pl.ds(start, KBAND), :]
        q = (q_ref[...] * ATTN_SCALE).astype(BF16)
        first = lax.broadcasted_iota(jnp.int32, (QBLK, 2 * HEAD_DIM), 1) < HEAD_DIM
        left = _left_mask(qb)
        zero = jnp.zeros_like(q)
        qh = [jnp.where(first, q, zero), jnp.where(first, zero, q)]
        rp = QBLK // ROW_PIECES
        chains = [(a, r) for r in range(ROW_PIECES) for a in range(2)]
        ss = [lax.dot_general(qh[a][r * rp:(r + 1) * rp], kb, NT, preferred_element_type=F32) for a, r in chains]
        ps = []
        for (a, r), s in zip(chains, ss):
            s = s + b_ref[a, pl.ds(r * rp, rp), :] + left
            e = jnp.exp(s - jnp.max(s, axis=-1, keepdims=True))
            ps.append((e * (1.0 / jnp.sum(e, axis=-1, keepdims=True))).astype(BF16))
        os_ = [jnp.dot(p, vb, preferred_element_type=F32) for p in ps]
        for r in range(ROW_PIECES):
            o_ref[pl.ds(r * rp, rp), :] = jnp.where(first[:rp], os_[2 * r], os_[2 * r + 1]).astype(BF16)

    return pl.pallas_call(
        body, name=name, grid=(B, nh2, S // QBLK),
        in_specs=[pl.BlockSpec((None, QBLK, 128), lambda b, h, i: (b, i, h)),
                  pl.BlockSpec((None, S, 128), lambda b, h, i: (b, 0, nh2 + h)),
                  pl.BlockSpec((None, S, 128), lambda b, h, i: (b, 0, 2 * nh2 + h)),
                  pl.BlockSpec((2, QBLK, KBAND), lambda b, h, i: (h, 0, 0))],
        out_specs=pl.BlockSpec((None, QBLK, 128), lambda b, h, i: (b, i, h)),
        out_shape=jax.ShapeDtypeStruct((B, S, D_ATTN + D_POOL), BF16),
        scratch_shapes=[pltpu.VMEM((PAD + S, 128), BF16), pltpu.VMEM((PAD + S, 128), BF16)],
        compiler_params=_params(dimension_semantics=("parallel", "parallel", "arbitrary")),
    )(z, z, z, bias)


def _attn_bwd(name, z, bias, dcat):
    B, S, _ = z.shape
    nh2 = ATTN_HEADS // 2
    nqb = S // QBLK

    def body(q_ref, k_ref, v_ref, b_ref, do_ref, dq_ref, dk_ref, dv_ref, db_ref, kp, vp, dka, dva):
        b, qb = pl.program_id(1), pl.program_id(2)

        @pl.when(qb == 0)
        def _():
            _fill_padded(kp, k_ref, S)
            _fill_padded(vp, v_ref, S)
            dka[...] = jnp.zeros_like(dka)
            dva[...] = jnp.zeros_like(dva)

        @pl.when((qb == 0) & (b == 0))
        def _():
            db_ref[...] = jnp.zeros_like(db_ref)

        start = pl.multiple_of(qb * QBLK, QBLK)
        band = pl.ds(start, KBAND)
        kb, vb = kp[band, :], vp[band, :]
        q = (q_ref[...] * ATTN_SCALE).astype(BF16)
        do = do_ref[...]
        first = lax.broadcasted_iota(jnp.int32, (QBLK, 2 * HEAD_DIM), 1) < HEAD_DIM
        left = _left_mask(qb)
        q2, do2 = _stack_heads(q, first), _stack_heads(do, first)
        p = _band_softmax(q2, kb, b_ref[...].reshape(2 * QBLK, KBAND), left)
        dp = lax.dot_general(do2, vb, NT, preferred_element_type=F32)
        ds = p * (dp - jnp.sum(p * dp, axis=-1, keepdims=True))
        db_ref[...] += ds.reshape(2, QBLK, KBAND)
        dsb = ds.astype(BF16)
        dq = jnp.dot(dsb, kb, preferred_element_type=F32)
        dq_ref[...] = (jnp.where(first, dq[:QBLK], dq[QBLK:]) * ATTN_SCALE).astype(BF16)
        dka[band, :] += lax.dot_general(dsb, q2, TN, preferred_element_type=F32)
        dva[band, :] += lax.dot_general(p.astype(BF16), do2, TN, preferred_element_type=F32)

        @pl.when(qb == nqb - 1)
        def _():
            dk_ref[...] = dka[pl.ds(PAD, S), :].astype(BF16)
            dv_ref[...] = dva[pl.ds(PAD, S), :].astype(BF16)

    qspec = pl.BlockSpec((None, QBLK, 128), lambda h, b, i: (b, i, h))
    kvout = pl.BlockSpec((None, S, 128), lambda h, b, i: (b, 0, h))
    bspec = pl.BlockSpec((2, QBLK, KBAND), lambda h, b, i: (h, 0, 0))
    act = jax.ShapeDtypeStruct((B, S, D_ATTN), BF16)
    return pl.pallas_call(
        body, name=name, grid=(nh2, B, nqb),
        in_specs=[qspec,
                  pl.BlockSpec((None, S, 128), lambda h, b, i: (b, 0, nh2 + h)),
                  pl.BlockSpec((None, S, 128), lambda h, b, i: (b, 0, 2 * nh2 + h)),
                  bspec, qspec],
        out_specs=[qspec, kvout, kvout, bspec],
        out_shape=[act, act, act, jax.ShapeDtypeStruct((ATTN_HEADS, QBLK, KBAND), F32)],
        scratch_shapes=[pltpu.VMEM((PAD + S, 128), BF16), pltpu.VMEM((PAD + S, 128), BF16),
                        pltpu.VMEM((PAD + S, 128), F32), pltpu.VMEM((PAD + S, 128), F32)],
        compiler_params=_params(dimension_semantics=("arbitrary", "arbitrary", "arbitrary")),
    )(z, z, z, bias, dcat)


def _bias_grad(name, dbias):
    width = KBAND + QBLK

    def body(d_ref, o_ref):
        acc = jnp.zeros((1, width), F32)
        for i in range(QBLK):
            row = jnp.concatenate([d_ref[pl.ds(i, 1), :], jnp.zeros((1, QBLK), F32)], axis=1)
            shift = QBLK - 1 - i
            acc = acc + (pltpu.roll(row, shift, 1) if shift else row)
        o_ref[...] = acc

    return pl.pallas_call(
        body, name=name, grid=(ATTN_HEADS,),
        in_specs=[pl.BlockSpec((None, QBLK, KBAND), lambda h: (h, 0, 0))],
        out_specs=pl.BlockSpec((None, 1, width), lambda h: (h, 0, 0)),
        out_shape=jax.ShapeDtypeStruct((ATTN_HEADS, 1, width), F32),
        compiler_params=_params(dimension_semantics=("parallel",)),
    )(dbias)


def _rel_grad_from_diagonals(diag):
    top = PAD + QBLK - 1 - REL_CLIP
    sat = jnp.sum(diag[:, :top + 1], axis=1, keepdims=True)
    mid = diag[:, top + 1:top + 2 * REL_CLIP][:, ::-1]
    return jnp.concatenate([jnp.zeros_like(sat), mid, sat], axis=1)


def _shift_rows(x, k, forward):
    S = x.shape[0]
    t = lax.broadcasted_iota(jnp.int32, x.shape, 0)
    if forward:
        return jnp.where(t < S - k, pltpu.roll(x, S - k, 0), 0.0)
    return jnp.where(t >= k, pltpu.roll(x, k, 0), 0.0)


def _window_sum(x, g, forward):
    s = x + _shift_rows(x, 1, forward)
    out = s
    for n, k in enumerate((2, 4, 8)):
        s = s + _shift_rows(s, k, forward)
        out = jnp.where(g > n, s, out)
    return out


def _pool_count(S, g):
    t = lax.broadcasted_iota(jnp.int32, (S, 1), 0)
    w = jnp.left_shift(2, g)
    return jnp.minimum(t + 1, w).astype(F32)


def _pool_fwd(name, z, wp, pscale, mixed):
    B, S, _ = z.shape
    c0 = 3 * D_ATTN // POOL_GROUP
    y0 = D_ATTN // POOL_GROUP

    def body(u_ref, w_ref, s_ref, mixed_ref, d_ref, y_ref):
        g = pl.program_id(1)
        u = u_ref[...]
        d = (_window_sum(u, g, False) / _pool_count(S, g) - u).astype(BF16)
        d_ref[...] = d
        y_ref[...] = (jnp.dot(d, w_ref[...], preferred_element_type=F32) * s_ref[...]).astype(BF16)

    return pl.pallas_call(
        body, name=name, grid=(B, len(POOL_WINDOWS)),
        in_specs=[pl.BlockSpec((None, S, POOL_GROUP), lambda b, g: (b, 0, c0 + g)),
                  pl.BlockSpec((None, POOL_GROUP, POOL_GROUP), lambda b, g: (g, 0, 0)),
                  pl.BlockSpec((1, POOL_GROUP), lambda b, g: (0, g)),
                  pl.BlockSpec(memory_space=pl.ANY)],
        out_specs=[pl.BlockSpec((None, S, POOL_GROUP), lambda b, g: (b, 0, g)),
                   pl.BlockSpec((None, S, POOL_GROUP), lambda b, g: (b, 0, y0 + g))],
        out_shape=[jax.ShapeDtypeStruct((B, S, D_POOL), BF16), jax.ShapeDtypeStruct(mixed.shape, BF16)],
        input_output_aliases={3: 1},
        compiler_params=_params(dimension_semantics=("parallel", "parallel")),
    )(z, wp, pscale, mixed)


def _pool_bwd(name, d, wp, pscale, dcat):
    B, S, _ = d.shape
    c0 = D_ATTN // POOL_GROUP

    def body(d_ref, w_ref, s_ref, dy_ref, du_ref, dw_ref, dsc_ref):
        g, b = pl.program_id(0), pl.program_id(1)
        dv = d_ref[...]
        dy = dy_ref[...].astype(F32)
        w = w_ref[...]
        ypre = jnp.dot(dv, w, preferred_element_type=F32)
        dyp = (dy * s_ref[...]).astype(BF16)
        dd = lax.dot_general(dyp, w, NT, preferred_element_type=F32)
        du_ref[...] = (_window_sum(dd / _pool_count(S, g), g, True) - dd).astype(BF16)
        dw = lax.dot_general(dv, dyp, TN, preferred_element_type=F32)
        dsc = jnp.sum(dy * ypre, axis=0, keepdims=True)

        @pl.when(b == 0)
        def _():
            dw_ref[...] = dw
            dsc_ref[...] = dsc

        @pl.when(b > 0)
        def _():
            dw_ref[...] += dw
            dsc_ref[...] += dsc

    blk = pl.BlockSpec((None, S, POOL_GROUP), lambda g, b: (b, 0, g))
    wspec = pl.BlockSpec((None, POOL_GROUP, POOL_GROUP), lambda g, b: (g, 0, 0))
    sspec = pl.BlockSpec((1, POOL_GROUP), lambda g, b: (0, g))
    return pl.pallas_call(
        body, name=name, grid=(len(POOL_WINDOWS), B),
        in_specs=[blk, wspec, sspec, pl.BlockSpec((None, S, POOL_GROUP), lambda g, b: (b, 0, c0 + g))],
        out_specs=[blk, wspec, sspec],
        out_shape=[jax.ShapeDtypeStruct((B, S, D_POOL), BF16),
                   jax.ShapeDtypeStruct((len(POOL_WINDOWS), POOL_GROUP, POOL_GROUP), F32),
                   jax.ShapeDtypeStruct((1, D_POOL), F32)],
        compiler_params=_params(dimension_semantics=("arbitrary", "arbitrary")),
    )(d, wp, pscale, dcat)


def _cross_softmax(q, k):
    s = lax.dot_general(q, k, NT, preferred_element_type=F32) * (CROSS_DIM ** -0.5)
    e = jnp.exp(s - jnp.max(s, axis=-1, keepdims=True))
    return e * (1.0 / jnp.sum(e, axis=-1, keepdims=True))


def _cross_fwd(name, qc, kv, tq=1024):
    B, S, _ = qc.shape
    M = kv.shape[1]
    tq = _tile(S, tq)

    def body(q_ref, k_ref, v_ref, o_ref):
        p = _cross_softmax(q_ref[...], k_ref[...])
        o_ref[...] = jnp.dot(p.astype(BF16), v_ref[...], preferred_element_type=F32).astype(BF16)

    qspec = pl.BlockSpec((None, tq, CROSS_DIM), lambda b, h, i: (b, i, h))
    return pl.pallas_call(
        body, name=name, grid=(B, CROSS_HEADS, S // tq),
        in_specs=[qspec, pl.BlockSpec((None, M, CROSS_DIM), lambda b, h, i: (b, 0, h)),
                  pl.BlockSpec((None, M, CROSS_DIM), lambda b, h, i: (b, 0, CROSS_HEADS + h))],
        out_specs=qspec, out_shape=jax.ShapeDtypeStruct((B, S, D_CROSS), BF16),
        compiler_params=_params(dimension_semantics=("parallel", "parallel", "parallel")),
    )(qc, kv, kv)


def _cross_bwd(name, qc, kv, do, tq=1024):
    B, S, _ = qc.shape
    M = kv.shape[1]
    tq = _tile(S, tq)
    nq = S // tq
    scale = CROSS_DIM ** -0.5

    def body(q_ref, k_ref, v_ref, do_ref, dq_ref, dk_ref, dv_ref, dka, dva):
        i = pl.program_id(2)
        q, k, v, dov = q_ref[...], k_ref[...], v_ref[...], do_ref[...]
        p = _cross_softmax(q, k)
        dp = lax.dot_general(dov, v, NT, preferred_element_type=F32)
        ds = ((p * (dp - jnp.sum(p * dp, axis=-1, keepdims=True))) * scale).astype(BF16)
        dq_ref[...] = jnp.dot(ds, k, preferred_element_type=F32).astype(BF16)
        dk = lax.dot_general(ds, q, TN, preferred_element_type=F32)
        dv = lax.dot_general(p.astype(BF16), dov, TN, preferred_element_type=F32)

        @pl.when(i == 0)
        def _():
            dka[...] = dk
            dva[...] = dv

        @pl.when(i > 0)
        def _():
            dka[...] += dk
            dva[...] += dv

        @pl.when(i == nq - 1)
        def _():
            dk_ref[...] = dka[...].astype(BF16)
            dv_ref[...] = dva[...].astype(BF16)

    qspec = pl.BlockSpec((None, tq, CROSS_DIM), lambda b, h, i: (b, i, h))
    kspec = pl.BlockSpec((None, M, CROSS_DIM), lambda b, h, i: (b, 0, h))
    return pl.pallas_call(
        body, name=name, grid=(B, CROSS_HEADS, nq),
        in_specs=[qspec, kspec, pl.BlockSpec((None, M, CROSS_DIM), lambda b, h, i: (b, 0, CROSS_HEADS + h)), qspec],
        out_specs=[qspec, kspec, kspec],
        out_shape=[jax.ShapeDtypeStruct((B, S, D_CROSS), BF16), jax.ShapeDtypeStruct((B, M, D_CROSS), BF16),
                   jax.ShapeDtypeStruct((B, M, D_CROSS), BF16)],
        scratch_shapes=[pltpu.VMEM((M, CROSS_DIM), F32), pltpu.VMEM((M, CROSS_DIM), F32)],
        compiler_params=_params(dimension_semantics=("parallel", "parallel", "arbitrary")),
    )(qc, kv, kv, do)


def _local_step(x, mem, target, small, weights, emit, start_token=None):
    B, S, D = x.shape
    T = B * S
    x2, t2 = x.reshape(T, D), target.reshape(T, D)
    mem2 = mem.reshape(-1, D)
    n_mem = mem.shape[1]
    wts = {}

    hn1 = _rms_fwd("norm_ffn1", x2, small["ffn1_norm"], after=start_token)
    memn = _rms_fwd("norm_mem", mem2, small["mem_norm"])
    bias = _bias_tile("bias_tile", small["rel_bias"])
    wts.update(weights(0, [hn1, memn, bias]))
    g1 = _ffn_gate("ffn1_gate", hn1, wts["ffn1_w_gate"], N_DEV)
    wts.update(weights(1, g1))
    u1, a1 = _ffn_up_act("ffn1_up", hn1, wts["ffn1_w_up"], g1)
    wts.update(weights(2, a1))
    h1 = _ffn_out("ffn1_down", a1, wts["ffn1_w_down"], res=x2, scale=FFN_RES)
    wts.update(weights(3, h1))
    z, hn2 = _mm_nn_cols("mix_in", h1, wts["w_in"], F32, norm=small["mix_norm"])
    z = z.reshape(B, S, -1)
    mixed = _attn_fwd("attn_fwd", z, bias)
    d_pool, mixed = _pool_fwd("pool_fwd", z, wts["w_pool"], small["pool_scale"], mixed)
    cat = mixed.reshape(T, -1)
    h2 = _mm_nn("mix_out", cat, wts["w_out"], F32, res=h1)
    wts.update(weights(4, h2))
    qc, hn3 = _mm_nn("cross_q", h2, wts["w_cq"], BF16, norm=small["cross_norm"])
    kv = _mm_nn("cross_kv", memn, wts["w_ckv"], BF16)
    o = _cross_fwd("cross_fwd", qc.reshape(B, S, -1), kv.reshape(B, n_mem, -1)).reshape(T, -1)
    h3 = _mm_nn_cols("cross_out", o, wts["w_co"], F32, res=h2)
    wts.update(weights(5, h3))
    hn4, g2, u2, a2 = _ffn_up("ffn2_up", h3, small["ffn2_norm"], wts["ffn2_w_gate"], wts["ffn2_w_up"], N_DEV)
    h4 = _ffn_out("ffn2_down", a2, wts["ffn2_w_down"], res=h3, scale=FFN_RES)

    gs = {}
    loss_part, dh4, dh4b, gs["final_norm"] = _loss_and_grad("loss", h4, small["final_norm"], t2)

    def ffn_bwd(tag, dh, dhb, h_in, hn, g, u, a, wg, wu, wd, gain):
        tok = emit({tag + "_w_down": _ffn_dw(tag + "_dwd", a, dhb, scale=FFN_RES)})
        dg, du = _ffn_dact(tag + "_dact", dhb, wd, g, u, after=tok)
        tok = emit({tag + "_w_gate": _ffn_dw(tag + "_dwg", dg, hn)})
        tok = emit({tag + "_w_up": _ffn_dw(tag + "_dwu", du, hn, after=tok)})
        dhn = _ffn_out(tag + "_dhn_g", dg, wg, after=tok)
        dhn = _ffn_out(tag + "_dhn_u", du, wu, res=dhn)
        return _rms_bwd(tag + "_dnorm", h_in, gain, dhn, skip=dh)

    dh3, dh3b, gs["ffn2_norm"] = ffn_bwd("ffn2", dh4, dh4b, h3, hn4, g2, u2, a2, wts["ffn2_w_gate"],
                                         wts["ffn2_w_up"], wts["ffn2_w_down"], small["ffn2_norm"])
    do = _mm_nt_cols("cross_do", dh3b, wts["w_co"], BF16, tn=D_CROSS)
    gw = {"w_co": _mm_tn("cross_dwo", o, dh3b, tm=D_CROSS, col_blocks=N_DEV)}
    dqc, dk, dv = _cross_bwd("cross_bwd", qc.reshape(B, S, -1), kv.reshape(B, n_mem, -1), do.reshape(B, S, -1))
    dqc = dqc.reshape(T, -1)
    dkv = jnp.concatenate([dk, dv], axis=-1).reshape(B * n_mem, -1)
    gw["w_cq"] = _mm_tn("cross_dwq", hn3, dqc, tn=D_CROSS)
    gw["w_ckv"] = _mm_tn("cross_dwkv", memn, dkv)
    tok = emit(gw)
    dhn3 = _mm_nt("cross_dhn", dqc, wts["w_cq"], F32, after=tok)
    dmemn = _mm_nt("cross_dmem", dkv, wts["w_ckv"], F32, tm=512)
    _, _, gs["mem_norm"] = _rms_bwd("mem_dnorm", mem2, small["mem_norm"], dmemn)
    dh2, dh2b, gs["cross_norm"] = _rms_bwd("cross_dnorm", h2, small["cross_norm"], dhn3, skip=dh3)
    dcat = _mm_nt("mix_dcat", dh2b, wts["w_out"], BF16)
    gw = {"w_out": _mm_tn("mix_dwout", cat, dh2b)}
    dcat3 = dcat.reshape(B, S, -1)
    dq, dkk, dvv, dbias = _attn_bwd("attn_bwd", z, bias, dcat3)
    du, gw["w_pool"], gs["pool_scale"] = _pool_bwd("pool_bwd", d_pool, wts["w_pool"], small["pool_scale"], dcat3)
    gs["rel_bias"] = _rel_grad_from_diagonals(_bias_grad("bias_grad", dbias)[:, 0, :])
    dz = jnp.concatenate([dq, dkk, dvv, du], axis=-1).reshape(T, -1)
    gw["w_in"] = _mm_tn("mix_dwin", hn2, dz, col_blocks=N_DEV)
    tok = emit(gw)
    dhn2 = _mm_nt_cols("mix_dhn", dz, wts["w_in"], F32, after=tok)
    dh1, dh1b, gs["mix_norm"] = _rms_bwd("mix_dnorm", h1, small["mix_norm"], dhn2, skip=dh2)
    dx, _, gs["ffn1_norm"] = ffn_bwd("ffn1", dh1, dh1b, x2, hn1, g1, u1, a1, wts["ffn1_w_gate"],
                                     wts["ffn1_w_up"], wts["ffn1_w_down"], small["ffn1_norm"])
    return loss_part, dx.reshape(B, S, D), gs


def _position():
    return lax.axis_index("x"), lax.axis_index("y"), lax.axis_index("c")


def _index(p):
    return 4 * p[0] + 2 * p[1] + p[2]


HBM_SPEC = pl.BlockSpec(memory_space=pltpu.HBM)
SEM_SPEC = pl.BlockSpec(memory_space=pltpu.SEMAPHORE)
ANY_SPEC = pl.BlockSpec(memory_space=pl.ANY)
ORDERED_EFFECT = pltpu.SideEffectType.DATAFLOW_SIDE_EFFECTING


N_COPIES = {"grads": N_DEV - 1, "spread": 4, "relay": 3}
N_SEMS = {"grads": N_DEV, "spread": 5, "relay": 3}


def _copies(pattern, srcs, lands, send, recv, base=0):
    x, y, c = _position()
    me, sibling = _index((x, y, c)), (x, y, 1 - c)
    chips = [(1 - x, y), (x, 1 - y), (1 - x, 1 - y)]
    if pattern == "grads":
        targets = [(x ^ (k >> 2), y ^ ((k >> 1) & 1), c ^ (k & 1)) for k in range(1, N_DEV)]
    else:
        targets = [sibling] + [(*chip, c) for chip in chips]
    per, slots, out = N_COPIES[pattern], N_SEMS[pattern], []
    for a in range(len(lands)):
        for k in range(per):
            if pattern == "relay":
                src = dst = lands[a].at[_index((*chips[k], c))]
                to = sibling
            else:
                to = targets[k]
                src = srcs[a].at[_index(to)] if pattern == "grads" else srcs[a]
                dst = lands[a].at[me]
            slot = base + a * slots + k
            out.append(pltpu.make_async_remote_copy(src_ref=src, dst_ref=dst, send_sem=send.at[slot],
                                                    recv_sem=recv.at[slot], device_id=to, device_id_type=MESH))
    return out


def _own_copies(pattern, srcs, lands, send, base=0):
    if pattern == "relay":
        return []
    x, y, c = _position()
    me = _index((x, y, c))
    slots = N_SEMS[pattern]
    return [pltpu.make_async_copy(srcs[a].at[me] if pattern == "grads" else srcs[a], lands[a].at[me],
                                  send.at[base + a * slots + slots - 1]) for a in range(len(lands))]


def _comm(name, srcs, lands, wait=None, start=None, after=None):
    after = [] if after is None else list(after) if isinstance(after, (list, tuple)) else [after]
    ns, nl = len(srcs), len(lands)
    na = ns + nl
    arrays = list(srcs) + list(lands)
    n_wait = 2 if wait else 0
    n_start = 2 if start else 0

    def body(*refs):
        ins, lnd = refs[:ns], refs[ns:na]
        if wait:
            base = wait[3] if len(wait) > 3 else 0
            for cp in _copies(wait[0], ins, lnd, refs[na], refs[na + 1], base):
                cp.wait_send()
                cp.wait_recv()
            for cp in _own_copies(wait[0], ins, lnd, refs[na], base):
                cp.wait()
        if start:
            outs = refs[na + n_wait + len(after):]
            for cp in _copies(start, ins, lnd, outs[0], outs[1]) + _own_copies(start, ins, lnd, outs[0]):
                cp.start()
            refs[-1][...] = jnp.zeros((8, 128), F32)

    out_shape, out_specs = [], []
    if start:
        sems = pltpu.SemaphoreType.DMA((nl * N_SEMS[start],))
        out_shape += [sems, sems]
        out_specs += [SEM_SPEC, SEM_SPEC]
    out_shape += [pltpu.HBM(a.shape, a.dtype) for a in arrays]
    out_specs += [HBM_SPEC] * na
    if start:
        out_shape.append(jax.ShapeDtypeStruct((8, 128), F32))
        out_specs.append(pl.BlockSpec(memory_space=pltpu.VMEM))
    operands = [pltpu.with_memory_space_constraint(a, pltpu.HBM) for a in arrays]
    operands += list(wait[1:3]) if wait else []
    operands += after
    res = pl.pallas_call(
        body, name=name, out_shape=out_shape, out_specs=out_specs,
        in_specs=[HBM_SPEC] * na + [SEM_SPEC] * n_wait + [ANY_SPEC] * len(after),
        input_output_aliases={i: n_start + i for i in range(na)},
        compiler_params=pltpu.CompilerParams(has_side_effects=ORDERED_EFFECT),
    )(*operands)
    res = list(res)
    thru = res[n_start:n_start + na]
    return thru[:ns], thru[ns:], (tuple(res[:2]) if start else None), (res[-1] if start else None)


def _adamw_math(w, g, m, v):
    m = ADAM_B1 * m + (1.0 - ADAM_B1) * g
    v = ADAM_B2 * v + (1.0 - ADAM_B2) * (g * g)
    m_hat = m / (1.0 - ADAM_B1 ** ADAM_STEP)
    v_hat = v / (1.0 - ADAM_B2 ** ADAM_STEP)
    delta = -ADAM_LR * (m_hat / (jnp.sqrt(v_hat) + ADAM_EPS) + ADAM_WD * w)
    return delta, m, v


def _adamw(name, parts, w, m, v, tr=128):
    R, C = w.shape
    tr = _tile(R, tr)

    def body(p_ref, w_ref, m_ref, v_ref, g_out, d_out, m_out, v_out):
        g = p_ref[0].astype(F32)
        for d in range(1, N_DEV):
            g = g + p_ref[d].astype(F32)
        g_out[...] = g
        d_out[...], m_out[...], v_out[...] = _adamw_math(w_ref[...], g, m_ref[...], v_ref[...])

    row = pl.BlockSpec((tr, C), lambda i: (i, 0))
    out = jax.ShapeDtypeStruct((R, C), F32)
    return pl.pallas_call(
        body, name=name, grid=(R // tr,),
        in_specs=[pl.BlockSpec((N_DEV, tr, C), lambda i: (0, i, 0)), row, row, row],
        out_specs=[row] * 4, out_shape=[out] * 4,
        compiler_params=_params(dimension_semantics=("parallel",)),
    )(parts, w, m, v)


def _small_allreduce_adamw(name, g, w, m, v):
    R = g.shape[0]

    def body(g_ref, w_ref, m_ref, v_ref, g_out, d_out, m_out, v_out, land, send, recv):
        x, y, c = _position()
        me = _index((x, y, c))
        land[me] = g_ref[...]
        copies = []
        for k in range(1, N_DEV):
            peer = (x ^ (k >> 2), y ^ ((k >> 1) & 1), c ^ (k & 1))
            copies.append(pltpu.make_async_remote_copy(
                src_ref=g_ref, dst_ref=land.at[me], send_sem=send.at[k - 1], recv_sem=recv.at[k - 1],
                device_id=peer, device_id_type=MESH))
        for cp in copies:
            cp.start()
        for cp in copies:
            cp.wait()
        total = land[0]
        for d in range(1, N_DEV):
            total = total + land[d]
        g_out[...] = total
        d_out[...], m_out[...], v_out[...] = _adamw_math(w_ref[...], total, m_ref[...], v_ref[...])

    vm = pl.BlockSpec(memory_space=pltpu.VMEM)
    out = jax.ShapeDtypeStruct((R, 128), F32)
    return pl.pallas_call(
        body, name=name, in_specs=[vm] * 4, out_specs=[vm] * 4, out_shape=[out] * 4,
        scratch_shapes=[pltpu.VMEM((N_DEV, R, 128), F32), pltpu.SemaphoreType.DMA((7,)),
                        pltpu.SemaphoreType.DMA((7,))],
    )(g, w, m, v)


BIG = ("ffn1_w_gate", "ffn1_w_up", "ffn1_w_down", "w_in", "w_pool", "w_out", "w_cq", "w_ckv", "w_co",
       "ffn2_w_gate", "ffn2_w_up", "ffn2_w_down")
SMALL = ("ffn1_norm", "mix_norm", "rel_bias", "pool_scale", "cross_norm", "mem_norm", "ffn2_norm", "final_norm")
ORDER = ("ffn1_norm", "ffn1_w_gate", "ffn1_w_up", "ffn1_w_down", "mix_norm", "w_in", "rel_bias", "w_pool",
         "pool_scale", "w_out", "cross_norm", "mem_norm", "w_cq", "w_ckv", "w_co", "ffn2_norm", "ffn2_w_gate",
         "ffn2_w_up", "ffn2_w_down", "final_norm")
TRANSPOSED = ("ffn1_w_gate", "ffn1_w_up", "ffn2_w_gate", "ffn2_w_up")
ROW_SHARDED = TRANSPOSED + ("ffn1_w_down", "ffn2_w_down", "w_out", "w_cq", "w_ckv")
GATHER_GROUPS = (("ffn1_w_gate",), ("ffn1_w_up",), ("ffn1_w_down",), ("w_in", "w_pool", "w_out"),
                 ("w_cq", "w_ckv", "w_co"), ("ffn2_w_gate", "ffn2_w_up", "ffn2_w_down"))
RELAY_BEFORE_USE = ((0,), (1,), (2,), (3,), (4, 5), ())


def _pack(arrays):
    flat = jnp.concatenate([a.reshape(-1) for a in arrays])
    rows = -(-flat.shape[0] // 1024) * 8
    return jnp.pad(flat, (0, rows * 128 - flat.shape[0])).reshape(rows, 128)


def _unpack(packed, like):
    flat, out, at = packed.reshape(-1), [], 0
    for a in like:
        out.append(flat[at:at + a.size].reshape(a.shape))
        at += a.size
    return out


def _shard2d(a):
    a = a[0]
    return a.reshape(-1, a.shape[-1])


def kernel(x, mem, ffn1_norm, ffn1_w_gate, ffn1_w_up, ffn1_w_down, mix_norm, w_in, rel_bias, w_pool, pool_scale, w_out, cross_norm, mem_norm, w_cq, w_ckv, w_co, ffn2_norm, ffn2_w_gate, ffn2_w_up, ffn2_w_down, final_norm, loss_target, m_ffn1_norm, m_ffn1_w_gate, m_ffn1_w_up, m_ffn1_w_down, m_mix_norm, m_w_in, m_rel_bias, m_w_pool, m_pool_scale, m_w_out, m_cross_norm, m_mem_norm, m_w_cq, m_w_ckv, m_w_co, m_ffn2_norm, m_ffn2_w_gate, m_ffn2_w_up, m_ffn2_w_down, m_final_norm, v_ffn1_norm, v_ffn1_w_gate, v_ffn1_w_up, v_ffn1_w_down, v_mix_norm, v_w_in, v_rel_bias, v_w_pool, v_pool_scale, v_w_out, v_cross_norm, v_mem_norm, v_w_cq, v_w_ckv, v_w_co, v_ffn2_norm, v_ffn2_w_gate, v_ffn2_w_up, v_ffn2_w_down, v_final_norm):
    args = dict(locals())
    def view(n, a):
        return a.transpose(0, 2, 1) if n in TRANSPOSED else a

    w_in_ = {n: view(n, args[n]) for n in ORDER}
    m_in = {n: view(n, args["m_" + n]) for n in ORDER}
    v_in = {n: view(n, args["v_" + n]) for n in ORDER}

    n_g, rows = len(POOL_WINDOWS), POOL_GROUP // N_DEV

    def landing(block_shape, dtype):
        return lax.empty((N_DEV,) + tuple(block_shape), dtype)

    gathers, tok = [], None
    for first in (True, False):
        groups = GATHER_GROUPS[:1] if first else GATHER_GROUPS[1:]
        shards = [_shard2d(w_in_[n]) for group in groups for n in group]
        shards = [(s if tok is None else s + tok[0, 0]).astype(BF16) for s in shards]
        srcs, lands, sems, tok = _comm("gather_start_%d" % (not first), shards,
                                       [landing(s.shape, BF16) for s in shards], start="spread", after=tok)
        at = 0
        for group in groups:
            gathers.append((srcs[at:at + len(group)], lands[at:at + len(group)], sems + (at * N_SEMS["spread"],)))
            at += len(group)

    def weights(gi, after):
        for ri in RELAY_BEFORE_USE[gi]:
            srcs, lands, sems = gathers[ri]
            _, lands, sems, after = _comm("gather_relay_%d" % ri, srcs, lands, wait=("spread",) + sems,
                                          start="relay", after=after)
            gathers[ri] = (None, lands, sems)
        _, lands, sems = gathers[gi]
        _, lands, _, _ = _comm("gather_finish_%d" % gi, [], lands, wait=("relay",) + sems, after=after)
        out = {}
        for n, full in zip(GATHER_GROUPS[gi], lands):
            if n == "w_pool":
                full = full.reshape(N_DEV, n_g, rows, POOL_GROUP).transpose(1, 0, 2, 3).reshape(n_g, POOL_GROUP, POOL_GROUP)
            out[n] = full.reshape(-1, full.shape[-1]) if n in ROW_SHARDED else full
        return out

    scatters = []

    def emit(gw):
        names = list(gw)
        stacks = []
        for n in names:
            g = gw[n]
            if n == "w_pool":
                g = g.reshape(n_g, N_DEV, rows, POOL_GROUP).transpose(1, 0, 2, 3).astype(BF16)
            stacks.append(g.reshape((N_DEV,) + _shard2d(w_in_[n]).shape))
        lands = [landing(s.shape[1:], s.dtype) for s in stacks]
        srcs, lands, sems, token = _comm("grads_start_%d" % len(scatters), stacks, lands, start="grads")
        scatters.append((names, srcs, lands, sems))
        return token

    small = {n: w_in_[n].reshape(1, -1) for n in SMALL if n != "rel_bias"}
    small["rel_bias"] = rel_bias[0]
    loss_part, grad_x, gs = _local_step(x, mem, loss_target, small, weights, emit, start_token=tok)

    grad, delta, new_m, new_v = {}, {}, {}, {}
    after = grad_x
    for si, (names, srcs, lands, sems) in enumerate(scatters):
        _, landed, _, _ = _comm("grads_finish_%d" % si, srcs, lands, wait=("grads",) + sems, after=after)
        for n, parts in zip(names, landed):
            res = _adamw("adamw_" + n, parts, _shard2d(w_in_[n]), _shard2d(m_in[n]), _shard2d(v_in[n]))
            grad[n], delta[n], new_m[n], new_v[n] = [view(n, r.reshape(w_in_[n].shape)) for r in res]
        after = res[0]

    slot = jnp.zeros((1,), F32)
    like = [w_in_[n] for n in SMALL] + [slot]
    gs["rel_bias"] = gs["rel_bias"].reshape(rel_bias.shape)
    res = _small_allreduce_adamw("small_params", _pack([gs[n] for n in SMALL] + [loss_part[0, :1]]), _pack(like),
                                 _pack([m_in[n] for n in SMALL] + [slot]), _pack([v_in[n] for n in SMALL] + [slot]))
    for d, packed in zip((grad, delta, new_m, new_v), res):
        for n, a in zip(SMALL, _unpack(packed, like)):
            d[n] = a
    loss = _unpack(res[0], like)[-1][0]
    return (loss, grad_x, *[grad[n] for n in ORDER], *[delta[n] for n in ORDER],
            *[new_m[n] for n in ORDER], *[new_v[n] for n in ORDER])
```

```python
import jax
import jax.numpy as jnp
from jax import lax
from jax.experimental import pallas as pl
from jax.experimental.pallas import tpu as pltpu

F32 = jnp.float32
BF16 = jnp.bfloat16

N_DEV = 8
EPS = 1e-6
NEG_INF = -1e30
CHUNK = 64
LEFT_CHUNKS = 8
PAD = LEFT_CHUNKS * CHUNK
QBLK = 4 * CHUNK
KBAND = PAD + QBLK
REL_CLIP = 128
ATTN_HEADS = 16
HEAD_DIM = 64
D_ATTN = ATTN_HEADS * HEAD_DIM
POOL_WINDOWS = (2, 4, 8, 16)
POOL_GROUP = 256
D_POOL = len(POOL_WINDOWS) * POOL_GROUP
CROSS_HEADS = 4
CROSS_DIM = 128
D_CROSS = CROSS_HEADS * CROSS_DIM
FFN_RES = 0.5
ADAM_LR, ADAM_B1, ADAM_B2, ADAM_EPS, ADAM_WD, ADAM_STEP = 0.001, 0.9, 0.999, 1e-08, 0.01, 10

NN = (((1,), (0,)), ((), ()))
NT = (((1,), (1,)), ((), ()))
TN = (((0,), (0,)), ((), ()))
MESH = pl.DeviceIdType.MESH
VMEM_LIMIT = 56 * 1024 * 1024


def _params(**kw):
    return pltpu.CompilerParams(vmem_limit_bytes=VMEM_LIMIT, **kw)


def _bf(v):
    return v if v.dtype == BF16 else v.astype(BF16)


WHOLE = ((Ellipsis,), (Ellipsis,))


def _rms(xv, gain):
    return (xv * lax.rsqrt(jnp.mean(xv * xv, axis=-1, keepdims=True) + EPS)) * gain


def _gemm(name, a, a_spec, b, b_spec, dims, grid, outs, chunks=(WHOLE,), extras=(), epilogue=None, after=None,
          norm=None):
    nex, nout = len(extras), len(outs)
    first_out = 2 + nex + (after is not None) + (norm is not None)

    def body(*refs):
        a_ref, b_ref = refs[:2]
        if norm is not None:
            hn_out, a_ref = refs[first_out + nout], refs[-1]

            @pl.when(pl.program_id(1) == 0)
            def _():
                hn = _rms(refs[0][...], refs[first_out - 1][...]).astype(BF16)
                a_ref[...] = hn
                hn_out[...] = hn

        total = None
        for ia, ib in chunks:
            d = lax.dot_general(_bf(a_ref[ia]), _bf(b_ref[ib]), dims, preferred_element_type=F32)
            total = d if total is None else total + d
        vals = epilogue(total, *[e[...] for e in refs[2:2 + nex]]) if epilogue is not None else (total,)
        for r, v in zip(refs[first_out:first_out + nout], vals):
            r[...] = v.astype(r.dtype)

    operands = [a, b] + [x for x, _, _ in extras]
    in_specs = [pl.BlockSpec(*a_spec), pl.BlockSpec(*b_spec)] + [pl.BlockSpec(blk, m) for _, blk, m in extras]
    if after is not None:
        operands.append(after)
        in_specs.append(pl.BlockSpec(after.shape, lambda i, j: (0, 0)))
    out_specs = [pl.BlockSpec(blk, m) for _, _, blk, m in outs]
    out_shape = [jax.ShapeDtypeStruct(s, d) for s, d, _, _ in outs]
    scratch = []
    if norm is not None:
        operands.append(norm)
        in_specs.append(pl.BlockSpec(norm.shape, lambda i, j: (0, 0)))
        out_specs.append(pl.BlockSpec(*a_spec))
        out_shape.append(jax.ShapeDtypeStruct(a.shape, BF16))
        scratch.append(pltpu.VMEM(a_spec[0], BF16))
    res = pl.pallas_call(
        body, name=name, grid=grid, in_specs=in_specs, out_specs=out_specs, out_shape=out_shape,
        scratch_shapes=scratch,
        compiler_params=_params(dimension_semantics=("parallel", "arbitrary" if norm is not None else "parallel")),
    )(*operands)
    return res[0] if len(res) == 1 else res


def _tile(n, want):
    for t in range(min(n, want), 15, -1):
        if n % t == 0 and t % 16 == 0:
            return t
    return n


def _mm_nn(name, a, b, out_dtype, res=None, tm=1024, tn=1024, norm=None):
    M, K = a.shape
    N = b.shape[1]
    tm, tn = _tile(M, tm), _tile(N, tn)
    extras = [] if res is None else [(res, (tm, tn), lambda i, j: (i, j))]
    epi = None if res is None else (lambda t, r: (r + t,))
    return _gemm(name, a, ((tm, K), lambda i, j: (i, 0)), b, ((K, tn), lambda i, j: (0, j)), NN,
                 (M // tm, N // tn), [((M, N), out_dtype, (tm, tn), lambda i, j: (i, j))], extras=extras, epilogue=epi,
                 norm=norm)


def _mm_nn_cols(name, a, bs, out_dtype, res=None, tm=1024, norm=None):
    M, K = a.shape
    nb, _, w = bs.shape
    tm = _tile(M, tm)
    extras = [] if res is None else [(res, (tm, w), lambda i, j: (i, j))]
    epi = None if res is None else (lambda t, r: (r + t,))
    return _gemm(name, a, ((tm, K), lambda i, j: (i, 0)), bs, ((None, K, w), lambda i, j: (j, 0, 0)), NN,
                 (M // tm, nb), [((M, nb * w), out_dtype, (tm, w), lambda i, j: (i, j))], extras=extras, epilogue=epi,
                 norm=norm)


def _mm_nt(name, a, b, out_dtype, tm=1024, tn=1024, after=None):
    M, K = a.shape
    N = b.shape[0]
    tm, tn = _tile(M, tm), _tile(N, tn)
    return _gemm(name, a, ((tm, K), lambda i, j: (i, 0)), b, ((tn, K), lambda i, j: (j, 0)), NT,
                 (M // tm, N // tn), [((M, N), out_dtype, (tm, tn), lambda i, j: (i, j))], after=after)


def _mm_nt_cols(name, a, bs, out_dtype, tm=1024, tn=512, after=None):
    M = a.shape[0]
    nb, N, w = bs.shape
    tm, tn = _tile(M, tm), _tile(N, tn)
    chunks = [((slice(None), pl.ds(c * w, w)), (c,)) for c in range(nb)]
    return _gemm(name, a, ((tm, nb * w), lambda i, j: (i, 0)), bs, ((nb, tn, w), lambda i, j: (0, j, 0)), NT,
                 (M // tm, N // tn), [((M, N), out_dtype, (tm, tn), lambda i, j: (i, j))], chunks, after=after)


def _mm_tn(name, a, b, tm=1024, tn=1024, col_blocks=None):
    T, Ka = a.shape
    Nb = b.shape[1]
    tm = _tile(Ka, tm)
    if col_blocks is None:
        tn = _tile(Nb, tn)
        out = ((Ka, Nb), BF16, (tm, tn), lambda i, j: (i, j))
    else:
        tn = Nb // col_blocks
        out = ((col_blocks, Ka, tn), BF16, (None, tm, tn), lambda i, j: (j, i, 0))
    return _gemm(name, a, ((T, tm), lambda i, j: (0, i)), b, ((T, tn), lambda i, j: (0, j)), TN,
                 (Ka // tm, Nb // tn), [out])


def _ffn_out(name, a, wt, res=None, scale=1.0, tm=1024, tn=512, after=None):
    nb, M, w = a.shape
    N = wt.shape[1]
    tm, tn = _tile(M, tm), _tile(N, tn)
    chunks = [((c,), (pl.ds(c * w, w),)) for c in range(nb)]
    extras = [] if res is None else [(res, (tm, tn), lambda i, j: (i, j))]
    epi = None if res is None else (lambda t, r: (r + scale * t,))
    return _gemm(name, a, ((nb, tm, w), lambda i, j: (0, i, 0)), wt, ((nb * w, tn), lambda i, j: (0, j)), NN,
                 (M // tm, N // tn), [((M, N), F32, (tm, tn), lambda i, j: (i, j))], chunks, extras, epi, after)


def _ffn_dact(name, dhb, wd, g, u, tm=1024, after=None):
    M, K = dhb.shape
    nb, _, w = g.shape
    tm = _tile(M, tm)
    tr = _tile(tm, 256)
    tokens = [] if after is None else [after]

    def body(dh_ref, wd_ref, g_ref, u_ref, *rest):
        dg_ref, du_ref = rest[-2:]
        pieces = [pl.ds(r * tr, tr) for r in range(tm // tr)]

        def product(rows):
            return lax.dot_general(dh_ref[rows, :], wd_ref[...], NT, preferred_element_type=F32)

        results = []
        dact = product(pieces[0])
        for n, rows in enumerate(pieces):
            ahead = product(pieces[n + 1]) if n + 1 < len(pieces) else None
            results.append(_swiglu_bwd(dact, g_ref[rows, :], u_ref[rows, :]))
            dact = ahead
        for rows, (dg, du) in zip(pieces, results):
            dg_ref[rows, :] = dg.astype(BF16)
            du_ref[rows, :] = du.astype(BF16)

    hid = pl.BlockSpec((None, tm, w), lambda i, j: (j, i, 0))
    return pl.pallas_call(
        body, name=name, grid=(M // tm, nb),
        in_specs=[pl.BlockSpec((tm, K), lambda i, j: (i, 0)), pl.BlockSpec((w, K), lambda i, j: (j, 0)), hid, hid]
        + [pl.BlockSpec(t.shape, lambda i, j: (0, 0)) for t in tokens],
        out_specs=[hid, hid], out_shape=[jax.ShapeDtypeStruct((nb, M, w), BF16)] * 2,
        compiler_params=_params(dimension_semantics=("parallel", "parallel")),
    )(dhb, wd, g, u, *tokens)


def _ffn_dw(name, a, b, scale=1.0, tn=1024, after=None):
    nb, T, w = a.shape
    N = b.shape[1]
    tn = _tile(N, tn)
    epi = None if scale == 1.0 else (lambda t: (t * scale,))
    return _gemm(name, a, ((None, T, w), lambda i, j: (i, 0, 0)), b, ((T, tn), lambda i, j: (0, j)), TN,
                 (nb, N // tn), [((nb * w, N), BF16, (w, tn), lambda i, j: (i, j))], epilogue=epi, after=after)


def _ffn_gate(name, hn, wgt, nb, tm=1024):
    M, K = hn.shape
    w = wgt.shape[0] // nb
    tm = _tile(M, tm)
    return _gemm(name, hn, ((tm, K), lambda i, j: (i, 0)), wgt, ((w, K), lambda i, j: (j, 0)), NT,
                 (M // tm, nb), [((nb, M, w), BF16, (None, tm, w), lambda i, j: (j, i, 0))])


def _ffn_up_act(name, hn, wut, g, tm=1024):
    M, K = hn.shape
    nb, _, w = g.shape
    tm = _tile(M, tm)
    hid = ((None, tm, w), lambda i, j: (j, i, 0))

    def epilogue(u, gate):
        gate = gate.astype(F32)
        return u, gate * jax.nn.sigmoid(gate) * u

    return _gemm(name, hn, ((tm, K), lambda i, j: (i, 0)), wut, ((w, K), lambda i, j: (j, 0)), NT,
                 (M // tm, nb), [((nb, M, w), BF16) + hid] * 2, extras=[(g,) + hid], epilogue=epilogue)


def _ffn_up(name, h, gain, wgt, wut, nb, tm=1024):
    M, K = h.shape
    w = wgt.shape[0] // nb
    tm = _tile(M, tm)

    def body(h_ref, gain_ref, g_ref, u_ref, hn_ref, og, ou, oa, a_ref):
        @pl.when(pl.program_id(1) == 0)
        def _():
            hn = _rms(h_ref[...], gain_ref[...]).astype(BF16)
            a_ref[...] = hn
            hn_ref[...] = hn

        a = a_ref[...]
        g = lax.dot_general(a, g_ref[...], NT, preferred_element_type=F32)
        u = lax.dot_general(a, u_ref[...], NT, preferred_element_type=F32)
        og[...] = g.astype(BF16)
        ou[...] = u.astype(BF16)
        oa[...] = (g * jax.nn.sigmoid(g) * u).astype(BF16)

    rows = pl.BlockSpec((tm, K), lambda i, j: (i, 0))
    wspec = pl.BlockSpec((w, K), lambda i, j: (j, 0))
    ospec = pl.BlockSpec((None, tm, w), lambda i, j: (j, i, 0))
    return pl.pallas_call(
        body, name=name, grid=(M // tm, nb),
        in_specs=[rows, pl.BlockSpec((1, K), lambda i, j: (0, 0)), wspec, wspec],
        out_specs=[rows] + [ospec] * 3,
        out_shape=[jax.ShapeDtypeStruct((M, K), BF16)] + [jax.ShapeDtypeStruct((nb, M, w), BF16)] * 3,
        scratch_shapes=[pltpu.VMEM((tm, K), BF16)],
        compiler_params=_params(dimension_semantics=("parallel", "arbitrary")),
    )(h, gain, wgt, wut)


def _swiglu_bwd(dact, g, u):
    g = g.astype(F32)
    u = u.astype(F32)
    sig = jax.nn.sigmoid(g)
    silu = g * sig
    d = FFN_RES * dact
    return d * u * (sig * (1.0 + g * (1.0 - sig))), d * silu


def _rms_fwd(name, x, gain, tr=512, after=None):
    R, D = x.shape
    tr = _tile(R, tr)

    def body(x_ref, g_ref, *rest):
        xv = x_ref[...]
        y = xv * lax.rsqrt(jnp.mean(xv * xv, axis=-1, keepdims=True) + EPS)
        rest[-1][...] = (y * g_ref[...]).astype(BF16)

    tokens = [] if after is None else [after]
    return pl.pallas_call(
        body, name=name, grid=(R // tr,),
        in_specs=[pl.BlockSpec((tr, D), lambda i: (i, 0)), pl.BlockSpec((1, D), lambda i: (0, 0))]
        + [pl.BlockSpec(t.shape, lambda i: (0, 0)) for t in tokens],
        out_specs=pl.BlockSpec((tr, D), lambda i: (i, 0)), out_shape=jax.ShapeDtypeStruct((R, D), BF16),
        compiler_params=_params(dimension_semantics=("parallel",)),
    )(x, gain, *tokens)


def _rms_bwd_math(xv, gain, dy):
    rstd = lax.rsqrt(jnp.mean(xv * xv, axis=-1, keepdims=True) + EPS)
    xhat = xv * rstd
    dxh = dy * gain
    dx = rstd * (dxh - xhat * jnp.mean(dxh * xhat, axis=-1, keepdims=True))
    return dx, jnp.sum(dy * xhat, axis=0, keepdims=True)


def _rms_bwd(name, x, gain, dy, skip=None, tr=256):
    R, D = x.shape
    tr = _tile(R, tr)
    has_skip = skip is not None

    def body(*refs):
        x_ref, g_ref, dy_ref = refs[:3]
        dx_ref, dxb_ref, dg_ref = refs[-3:]
        dx, dg = _rms_bwd_math(x_ref[...], g_ref[...], dy_ref[...].astype(F32))
        if has_skip:
            dx = dx + refs[3][...]
        dx_ref[...] = dx
        dxb_ref[...] = dx.astype(BF16)

        @pl.when(pl.program_id(0) == 0)
        def _():
            dg_ref[...] = dg

        @pl.when(pl.program_id(0) > 0)
        def _():
            dg_ref[...] += dg

    row = pl.BlockSpec((tr, D), lambda i: (i, 0))
    vec = pl.BlockSpec((1, D), lambda i: (0, 0))
    return pl.pallas_call(
        body, name=name, grid=(R // tr,),
        in_specs=[row, vec, row] + ([row] if has_skip else []),
        out_specs=[row, row, vec],
        out_shape=[jax.ShapeDtypeStruct((R, D), F32), jax.ShapeDtypeStruct((R, D), BF16),
                   jax.ShapeDtypeStruct((1, D), F32)],
        compiler_params=_params(dimension_semantics=("arbitrary",)),
    )(*([x, gain, dy] + ([skip] if has_skip else [])))


def _loss_and_grad(name, h, gain, target, tr=256):
    R, D = h.shape
    tr = _tile(R, tr)

    def body(h_ref, g_ref, t_ref, loss_ref, dh_ref, dhb_ref, dg_ref):
        hv, gain_v = h_ref[...], g_ref[...]
        y = (hv * lax.rsqrt(jnp.mean(hv * hv, axis=-1, keepdims=True) + EPS)) * gain_v
        err = y - t_ref[...]
        part = jnp.full((8, 128), 0.5 * jnp.sum(jnp.mean(err * err, axis=-1, keepdims=True)), F32)
        dh, dg = _rms_bwd_math(hv, gain_v, err * (1.0 / D))
        dh_ref[...] = dh
        dhb_ref[...] = dh.astype(BF16)

        @pl.when(pl.program_id(0) == 0)
        def _():
            dg_ref[...] = dg
            loss_ref[...] = part

        @pl.when(pl.program_id(0) > 0)
        def _():
            dg_ref[...] += dg
            loss_ref[...] += part

    row = pl.BlockSpec((tr, D), lambda i: (i, 0))
    vec = pl.BlockSpec((1, D), lambda i: (0, 0))
    return pl.pallas_call(
        body, name=name, grid=(R // tr,), in_specs=[row, vec, row],
        out_specs=[pl.BlockSpec((8, 128), lambda i: (0, 0)), row, row, vec],
        out_shape=[jax.ShapeDtypeStruct((8, 128), F32), jax.ShapeDtypeStruct((R, D), F32),
                   jax.ShapeDtypeStruct((R, D), BF16), jax.ShapeDtypeStruct((1, D), F32)],
        compiler_params=_params(dimension_semantics=("arbitrary",)),
    )(h, gain, target)


def _bias_tile(name, rel):
    width = KBAND + QBLK
    sat = rel[:, 2 * REL_CLIP:]
    n_left = PAD - REL_CLIP + 1
    row0 = jnp.concatenate([jnp.broadcast_to(sat, (ATTN_HEADS, n_left)), rel[:, :2 * REL_CLIP][:, ::-1],
                            jnp.broadcast_to(sat, (ATTN_HEADS, width - n_left - 2 * REL_CLIP))], axis=1)

    def body(e_ref, o_ref):
        rows = pltpu.roll(jnp.broadcast_to(e_ref[...], (QBLK, width)), 0, 1, stride=1, stride_axis=0)
        i = lax.broadcasted_iota(jnp.int32, (QBLK, KBAND), 0) // CHUNK
        j = lax.broadcasted_iota(jnp.int32, (QBLK, KBAND), 1) // CHUNK
        o_ref[...] = jnp.where((j >= i) & (j <= i + LEFT_CHUNKS), rows[:, :KBAND], NEG_INF)

    return pl.pallas_call(
        body, name=name, grid=(ATTN_HEADS,),
        in_specs=[pl.BlockSpec((None, 1, width), lambda h: (h, 0, 0))],
        out_specs=pl.BlockSpec((None, QBLK, KBAND), lambda h: (h, 0, 0)),
        out_shape=jax.ShapeDtypeStruct((ATTN_HEADS, QBLK, KBAND), F32),
        compiler_params=_params(dimension_semantics=("parallel",)),
    )(row0.reshape(ATTN_HEADS, 1, width))


ATTN_SCALE = HEAD_DIM ** -0.5
ROW_PIECES = 1


def _stack_heads(x, first):
    zero = jnp.zeros_like(x)
    return jnp.concatenate([jnp.where(first, x, zero), jnp.where(first, zero, x)], axis=0)


def _band_softmax(q_half_scaled, kb, bias, left_mask):
    s = lax.dot_general(q_half_scaled, kb, NT, preferred_element_type=F32) + bias + left_mask
    e = jnp.exp(s - jnp.max(s, axis=-1, keepdims=True))
    return e * (1.0 / jnp.sum(e, axis=-1, keepdims=True))


def _left_mask(qb):
    kpos = qb * QBLK - PAD + lax.broadcasted_iota(jnp.int32, (1, KBAND), 1)
    return jnp.where(kpos >= 0, 0.0, NEG_INF).astype(F32)


def _fill_padded(dst, src, S):
    dst[pl.ds(0, PAD), :] = jnp.zeros((PAD, dst.shape[1]), dst.dtype)
    dst[pl.ds(PAD, S), :] = src[...].astype(dst.dtype)


def _attn_fwd(name, z, bias):
    B, S, _ = z.shape
    nh2 = ATTN_HEADS // 2

    def body(q_ref, k_ref, v_ref, b_ref, o_ref, kp, vp):
        qb = pl.program_id(2)

        @pl.when(qb == 0)
        def _():
            _fill_padded(kp, k_ref, S)
            _fill_padded(vp, v_ref, S)

        start = pl.multiple_of(qb * QBLK, QBLK)
        kb, vb = kp[pl.ds(start, KBAND), :], vp[pl.ds(start, KBAND), :]
        q = (q_ref[...] * ATTN_SCALE).astype(BF16)
        first = lax.broadcasted_iota(jnp.int32, (QBLK, 2 * HEAD_DIM), 1) < HEAD_DIM
        left = _left_mask(qb)
        zero = jnp.zeros_like(q)
        qh = [jnp.where(first, q, zero), jnp.where(first, zero, q)]
        rp = QBLK // ROW_PIECES
        chains = [(a, r) for r in range(ROW_PIECES) for a in range(2)]
        ss = [lax.dot_general(qh[a][r * rp:(r + 1) * rp], kb, NT, preferred_element_type=F32) for a, r in chains]
        ps = []
        for (a, r), s in zip(chains, ss):
            s = s + b_ref[a, pl.ds(r * rp, rp), :] + left
            e = jnp.exp(s - jnp.max(s, axis=-1, keepdims=True))
            ps.append((e * (1.0 / jnp.sum(e, axis=-1, keepdims=True))).astype(BF16))
        os_ = [jnp.dot(p, vb, preferred_element_type=F32) for p in ps]
        for r in range(ROW_PIECES):
            o_ref[pl.ds(r * rp, rp), :] = jnp.where(first[:rp], os_[2 * r], os_[2 * r + 1]).astype(BF16)

    return pl.pallas_call(
        body, name=name, grid=(B, nh2, S // QBLK),
        in_specs=[pl.BlockSpec((None, QBLK, 128), lambda b, h, i: (b, i, h)),
                  pl.BlockSpec((None, S, 128), lambda b, h, i: (b, 0, nh2 + h)),
                  pl.BlockSpec((None, S, 128), lambda b, h, i: (b, 0, 2 * nh2 + h)),
                  pl.BlockSpec((2, QBLK, KBAND), lambda b, h, i: (h, 0, 0))],
        out_specs=pl.BlockSpec((None, QBLK, 128), lambda b, h, i: (b, i, h)),
        out_shape=jax.ShapeDtypeStruct((B, S, D_ATTN + D_POOL), BF16),
        scratch_shapes=[pltpu.VMEM((PAD + S, 128), BF16), pltpu.VMEM((PAD + S, 128), BF16)],
        compiler_params=_params(dimension_semantics=("parallel", "parallel", "arbitrary")),
    )(z, z, z, bias)


def _attn_bwd(name, z, bias, dcat):
    B, S, _ = z.shape
    nh2 = ATTN_HEADS // 2
    nqb = S // QBLK

    def body(q_ref, k_ref, v_ref, b_ref, do_ref, dq_ref, dk_ref, dv_ref, db_ref, kp, vp, dka, dva):
        b, qb = pl.program_id(1), pl.program_id(2)

        @pl.when(qb == 0)
        def _():
            _fill_padded(kp, k_ref, S)
            _fill_padded(vp, v_ref, S)
            dka[...] = jnp.zeros_like(dka)
            dva[...] = jnp.zeros_like(dva)

        @pl.when((qb == 0) & (b == 0))
        def _():
            db_ref[...] = jnp.zeros_like(db_ref)

        start = pl.multiple_of(qb * QBLK, QBLK)
        band = pl.ds(start, KBAND)
        kb, vb = kp[band, :], vp[band, :]
        q = (q_ref[...] * ATTN_SCALE).astype(BF16)
        do = do_ref[...]
        first = lax.broadcasted_iota(jnp.int32, (QBLK, 2 * HEAD_DIM), 1) < HEAD_DIM
        left = _left_mask(qb)
        q2, do2 = _stack_heads(q, first), _stack_heads(do, first)
        p = _band_softmax(q2, kb, b_ref[...].reshape(2 * QBLK, KBAND), left)
        dp = lax.dot_general(do2, vb, NT, preferred_element_type=F32)
        ds = p * (dp - jnp.sum(p * dp, axis=-1, keepdims=True))
        db_ref[...] += ds.reshape(2, QBLK, KBAND)
        dsb = ds.astype(BF16)
        dq = jnp.dot(dsb, kb, preferred_element_type=F32)
        dq_ref[...] = (jnp.where(first, dq[:QBLK], dq[QBLK:]) * ATTN_SCALE).astype(BF16)
        dka[:, band] += lax.dot_general(q2, dsb, TN, preferred_element_type=F32)
        dva[:, band] += lax.dot_general(do2, p.astype(BF16), TN, preferred_element_type=F32)

        @pl.when(qb == nqb - 1)
        def _():
            dk_ref[...] = dka[:, pl.ds(PAD, S)].T.astype(BF16)
            dv_ref[...] = dva[:, pl.ds(PAD, S)].T.astype(BF16)

    qspec = pl.BlockSpec((None, QBLK, 128), lambda h, b, i: (b, i, h))
    kvout = pl.BlockSpec((None, S, 128), lambda h, b, i: (b, 0, h))
    bspec = pl.BlockSpec((2, QBLK, KBAND), lambda h, b, i: (h, 0, 0))
    act = jax.ShapeDtypeStruct((B, S, D_ATTN), BF16)
    return pl.pallas_call(
        body, name=name, grid=(nh2, B, nqb),
        in_specs=[qspec,
                  pl.BlockSpec((None, S, 128), lambda h, b, i: (b, 0, nh2 + h)),
                  pl.BlockSpec((None, S, 128), lambda h, b, i: (b, 0, 2 * nh2 + h)),
                  bspec, qspec],
        out_specs=[qspec, kvout, kvout, bspec],
        out_shape=[act, act, act, jax.ShapeDtypeStruct((ATTN_HEADS, QBLK, KBAND), F32)],
        scratch_shapes=[pltpu.VMEM((PAD + S, 128), BF16), pltpu.VMEM((PAD + S, 128), BF16),
                        pltpu.VMEM((128, PAD + S), F32), pltpu.VMEM((128, PAD + S), F32)],
        compiler_params=_params(dimension_semantics=("arbitrary", "arbitrary", "arbitrary")),
    )(z, z, z, bias, dcat)


def _bias_grad(name, dbias):
    width = KBAND + QBLK

    def body(d_ref, o_ref):
        acc = jnp.zeros((1, width), F32)
        for i in range(QBLK):
            row = jnp.concatenate([d_ref[pl.ds(i, 1), :], jnp.zeros((1, QBLK), F32)], axis=1)
            shift = QBLK - 1 - i
            acc = acc + (pltpu.roll(row, shift, 1) if shift else row)
        o_ref[...] = acc

    return pl.pallas_call(
        body, name=name, grid=(ATTN_HEADS,),
        in_specs=[pl.BlockSpec((None, QBLK, KBAND), lambda h: (h, 0, 0))],
        out_specs=pl.BlockSpec((None, 1, width), lambda h: (h, 0, 0)),
        out_shape=jax.ShapeDtypeStruct((ATTN_HEADS, 1, width), F32),
        compiler_params=_params(dimension_semantics=("parallel",)),
    )(dbias)


def _rel_grad_from_diagonals(diag):
    top = PAD + QBLK - 1 - REL_CLIP
    sat = jnp.sum(diag[:, :top + 1], axis=1, keepdims=True)
    mid = diag[:, top + 1:top + 2 * REL_CLIP][:, ::-1]
    return jnp.concatenate([jnp.zeros_like(sat), mid, sat], axis=1)


def _shift_rows(x, k, forward):
    S = x.shape[0]
    t = lax.broadcasted_iota(jnp.int32, x.shape, 0)
    if forward:
        return jnp.where(t < S - k, pltpu.roll(x, S - k, 0), 0.0)
    return jnp.where(t >= k, pltpu.roll(x, k, 0), 0.0)


def _window_sum(x, g, forward):
    s = x + _shift_rows(x, 1, forward)
    out = s
    for n, k in enumerate((2, 4, 8)):
        s = s + _shift_rows(s, k, forward)
        out = jnp.where(g > n, s, out)
    return out


def _pool_count(S, g):
    t = lax.broadcasted_iota(jnp.int32, (S, 1), 0)
    w = jnp.left_shift(2, g)
    return jnp.minimum(t + 1, w).astype(F32)


def _pool_fwd(name, z, wp, pscale, mixed):
    B, S, _ = z.shape
    c0 = 3 * D_ATTN // POOL_GROUP
    y0 = D_ATTN // POOL_GROUP

    def body(u_ref, w_ref, s_ref, mixed_ref, d_ref, y_ref):
        g = pl.program_id(1)
        u = u_ref[...]
        d = (_window_sum(u, g, False) / _pool_count(S, g) - u).astype(BF16)
        d_ref[...] = d
        y_ref[...] = (jnp.dot(d, w_ref[...], preferred_element_type=F32) * s_ref[...]).astype(BF16)

    return pl.pallas_call(
        body, name=name, grid=(B, len(POOL_WINDOWS)),
        in_specs=[pl.BlockSpec((None, S, POOL_GROUP), lambda b, g: (b, 0, c0 + g)),
                  pl.BlockSpec((None, POOL_GROUP, POOL_GROUP), lambda b, g: (g, 0, 0)),
                  pl.BlockSpec((1, POOL_GROUP), lambda b, g: (0, g)),
                  pl.BlockSpec(memory_space=pl.ANY)],
        out_specs=[pl.BlockSpec((None, S, POOL_GROUP), lambda b, g: (b, 0, g)),
                   pl.BlockSpec((None, S, POOL_GROUP), lambda b, g: (b, 0, y0 + g))],
        out_shape=[jax.ShapeDtypeStruct((B, S, D_POOL), BF16), jax.ShapeDtypeStruct(mixed.shape, BF16)],
        input_output_aliases={3: 1},
        compiler_params=_params(dimension_semantics=("parallel", "parallel")),
    )(z, wp, pscale, mixed)


def _pool_bwd(name, d, wp, pscale, dcat):
    B, S, _ = d.shape
    c0 = D_ATTN // POOL_GROUP

    def body(d_ref, w_ref, s_ref, dy_ref, du_ref, dw_ref, dsc_ref):
        g, b = pl.program_id(0), pl.program_id(1)
        dv = d_ref[...]
        dy = dy_ref[...].astype(F32)
        w = w_ref[...]
        ypre = jnp.dot(dv, w, preferred_element_type=F32)
        dyp = (dy * s_ref[...]).astype(BF16)
        dd = lax.dot_general(dyp, w, NT, preferred_element_type=F32)
        du_ref[...] = (_window_sum(dd / _pool_count(S, g), g, True) - dd).astype(BF16)
        dw = lax.dot_general(dv, dyp, TN, preferred_element_type=F32)
        dsc = jnp.sum(dy * ypre, axis=0, keepdims=True)

        @pl.when(b == 0)
        def _():
            dw_ref[...] = dw
            dsc_ref[...] = dsc

        @pl.when(b > 0)
        def _():
            dw_ref[...] += dw
            dsc_ref[...] += dsc

    blk = pl.BlockSpec((None, S, POOL_GROUP), lambda g, b: (b, 0, g))
    wspec = pl.BlockSpec((None, POOL_GROUP, POOL_GROUP), lambda g, b: (g, 0, 0))
    sspec = pl.BlockSpec((1, POOL_GROUP), lambda g, b: (0, g))
    return pl.pallas_call(
        body, name=name, grid=(len(POOL_WINDOWS), B),
        in_specs=[blk, wspec, sspec, pl.BlockSpec((None, S, POOL_GROUP), lambda g, b: (b, 0, c0 + g))],
        out_specs=[blk, wspec, sspec],
        out_shape=[jax.ShapeDtypeStruct((B, S, D_POOL), BF16),
                   jax.ShapeDtypeStruct((len(POOL_WINDOWS), POOL_GROUP, POOL_GROUP), F32),
                   jax.ShapeDtypeStruct((1, D_POOL), F32)],
        compiler_params=_params(dimension_semantics=("arbitrary", "arbitrary")),
    )(d, wp, pscale, dcat)


def _cross_softmax(q, k):
    s = lax.dot_general(q, k, NT, preferred_element_type=F32) * (CROSS_DIM ** -0.5)
    e = jnp.exp(s - jnp.max(s, axis=-1, keepdims=True))
    return e * (1.0 / jnp.sum(e, axis=-1, keepdims=True))


def _cross_fwd(name, qc, kv, tq=1024):
    B, S, _ = qc.shape
    M = kv.shape[1]
    tq = _tile(S, tq)

    def body(q_ref, k_ref, v_ref, o_ref):
        p = _cross_softmax(q_ref[...], k_ref[...])
        o_ref[...] = jnp.dot(p.astype(BF16), v_ref[...], preferred_element_type=F32).astype(BF16)

    qspec = pl.BlockSpec((None, tq, CROSS_DIM), lambda b, h, i: (b, i, h))
    return pl.pallas_call(
        body, name=name, grid=(B, CROSS_HEADS, S // tq),
        in_specs=[qspec, pl.BlockSpec((None, M, CROSS_DIM), lambda b, h, i: (b, 0, h)),
                  pl.BlockSpec((None, M, CROSS_DIM), lambda b, h, i: (b, 0, CROSS_HEADS + h))],
        out_specs=qspec, out_shape=jax.ShapeDtypeStruct((B, S, D_CROSS), BF16),
        compiler_params=_params(dimension_semantics=("parallel", "parallel", "parallel")),
    )(qc, kv, kv)


def _cross_bwd(name, qc, kv, do, tq=1024):
    B, S, _ = qc.shape
    M = kv.shape[1]
    tq = _tile(S, tq)
    nq = S // tq
    scale = CROSS_DIM ** -0.5

    def body(q_ref, k_ref, v_ref, do_ref, dq_ref, dk_ref, dv_ref, dka, dva):
        i = pl.program_id(2)
        q, k, v, dov = q_ref[...], k_ref[...], v_ref[...], do_ref[...]
        p = _cross_softmax(q, k)
        dp = lax.dot_general(dov, v, NT, preferred_element_type=F32)
        ds = ((p * (dp - jnp.sum(p * dp, axis=-1, keepdims=True))) * scale).astype(BF16)
        dq_ref[...] = jnp.dot(ds, k, preferred_element_type=F32).astype(BF16)
        dk = lax.dot_general(ds, q, TN, preferred_element_type=F32)
        dv = lax.dot_general(p.astype(BF16), dov, TN, preferred_element_type=F32)

        @pl.when(i == 0)
        def _():
            dka[...] = dk
            dva[...] = dv

        @pl.when(i > 0)
        def _():
            dka[...] += dk
            dva[...] += dv

        @pl.when(i == nq - 1)
        def _():
            dk_ref[...] = dka[...].astype(BF16)
            dv_ref[...] = dva[...].astype(BF16)

    qspec = pl.BlockSpec((None, tq, CROSS_DIM), lambda b, h, i: (b, i, h))
    kspec = pl.BlockSpec((None, M, CROSS_DIM), lambda b, h, i: (b, 0, h))
    return pl.pallas_call(
        body, name=name, grid=(B, CROSS_HEADS, nq),
        in_specs=[qspec, kspec, pl.BlockSpec((None, M, CROSS_DIM), lambda b, h, i: (b, 0, CROSS_HEADS + h)), qspec],
        out_specs=[qspec, kspec, kspec],
        out_shape=[jax.ShapeDtypeStruct((B, S, D_CROSS), BF16), jax.ShapeDtypeStruct((B, M, D_CROSS), BF16),
                   jax.ShapeDtypeStruct((B, M, D_CROSS), BF16)],
        scratch_shapes=[pltpu.VMEM((M, CROSS_DIM), F32), pltpu.VMEM((M, CROSS_DIM), F32)],
        compiler_params=_params(dimension_semantics=("parallel", "parallel", "arbitrary")),
    )(qc, kv, kv, do)


def _local_step(x, mem, target, small, weights, emit, start_token=None):
    B, S, D = x.shape
    T = B * S
    x2, t2 = x.reshape(T, D), target.reshape(T, D)
    mem2 = mem.reshape(-1, D)
    n_mem = mem.shape[1]
    wts = {}

    hn1 = _rms_fwd("norm_ffn1", x2, small["ffn1_norm"], after=start_token)
    memn = _rms_fwd("norm_mem", mem2, small["mem_norm"])
    bias = _bias_tile("bias_tile", small["rel_bias"])
    wts.update(weights(0, [hn1, memn, bias]))
    g1 = _ffn_gate("ffn1_gate", hn1, wts["ffn1_w_gate"], N_DEV)
    wts.update(weights(1, g1))
    u1, a1 = _ffn_up_act("ffn1_up", hn1, wts["ffn1_w_up"], g1)
    wts.update(weights(2, a1))
    h1 = _ffn_out("ffn1_down", a1, wts["ffn1_w_down"], res=x2, scale=FFN_RES)
    wts.update(weights(3, h1))
    z, hn2 = _mm_nn_cols("mix_in", h1, wts["w_in"], F32, norm=small["mix_norm"])
    z = z.reshape(B, S, -1)
    mixed = _attn_fwd("attn_fwd", z, bias)
    d_pool, mixed = _pool_fwd("pool_fwd", z, wts["w_pool"], small["pool_scale"], mixed)
    cat = mixed.reshape(T, -1)
    h2 = _mm_nn("mix_out", cat, wts["w_out"], F32, res=h1)
    wts.update(weights(4, h2))
    qc, hn3 = _mm_nn("cross_q", h2, wts["w_cq"], BF16, norm=small["cross_norm"])
    kv = _mm_nn("cross_kv", memn, wts["w_ckv"], BF16)
    o = _cross_fwd("cross_fwd", qc.reshape(B, S, -1), kv.reshape(B, n_mem, -1)).reshape(T, -1)
    h3 = _mm_nn_cols("cross_out", o, wts["w_co"], F32, res=h2)
    wts.update(weights(5, h3))
    hn4, g2, u2, a2 = _ffn_up("ffn2_up", h3, small["ffn2_norm"], wts["ffn2_w_gate"], wts["ffn2_w_up"], N_DEV)
    h4 = _ffn_out("ffn2_down", a2, wts["ffn2_w_down"], res=h3, scale=FFN_RES)

    gs = {}
    loss_part, dh4, dh4b, gs["final_norm"] = _loss_and_grad("loss", h4, small["final_norm"], t2)

    def ffn_bwd(tag, dh, dhb, h_in, hn, g, u, a, wg, wu, wd, gain):
        tok = emit({tag + "_w_down": _ffn_dw(tag + "_dwd", a, dhb, scale=FFN_RES)})
        dg, du = _ffn_dact(tag + "_dact", dhb, wd, g, u, after=tok)
        tok = emit({tag + "_w_gate": _ffn_dw(tag + "_dwg", dg, hn)})
        tok = emit({tag + "_w_up": _ffn_dw(tag + "_dwu", du, hn, after=tok)})
        dhn = _ffn_out(tag + "_dhn_g", dg, wg, after=tok)
        dhn = _ffn_out(tag + "_dhn_u", du, wu, res=dhn)
        return _rms_bwd(tag + "_dnorm", h_in, gain, dhn, skip=dh)

    dh3, dh3b, gs["ffn2_norm"] = ffn_bwd("ffn2", dh4, dh4b, h3, hn4, g2, u2, a2, wts["ffn2_w_gate"],
                                         wts["ffn2_w_up"], wts["ffn2_w_down"], small["ffn2_norm"])
    do = _mm_nt_cols("cross_do", dh3b, wts["w_co"], BF16, tn=D_CROSS)
    gw = {"w_co": _mm_tn("cross_dwo", o, dh3b, tm=D_CROSS, col_blocks=N_DEV)}
    dqc, dk, dv = _cross_bwd("cross_bwd", qc.reshape(B, S, -1), kv.reshape(B, n_mem, -1), do.reshape(B, S, -1))
    dqc = dqc.reshape(T, -1)
    dkv = jnp.concatenate([dk, dv], axis=-1).reshape(B * n_mem, -1)
    gw["w_cq"] = _mm_tn("cross_dwq", hn3, dqc, tn=D_CROSS)
    gw["w_ckv"] = _mm_tn("cross_dwkv", memn, dkv)
    tok = emit(gw)
    dhn3 = _mm_nt("cross_dhn", dqc, wts["w_cq"], F32, after=tok)
    dmemn = _mm_nt("cross_dmem", dkv, wts["w_ckv"], F32, tm=512)
    _, _, gs["mem_norm"] = _rms_bwd("mem_dnorm", mem2, small["mem_norm"], dmemn)
    dh2, dh2b, gs["cross_norm"] = _rms_bwd("cross_dnorm", h2, small["cross_norm"], dhn3, skip=dh3)
    dcat = _mm_nt("mix_dcat", dh2b, wts["w_out"], BF16)
    gw = {"w_out": _mm_tn("mix_dwout", cat, dh2b)}
    dcat3 = dcat.reshape(B, S, -1)
    dq, dkk, dvv, dbias = _attn_bwd("attn_bwd", z, bias, dcat3)
    du, gw["w_pool"], gs["pool_scale"] = _pool_bwd("pool_bwd", d_pool, wts["w_pool"], small["pool_scale"], dcat3)
    gs["rel_bias"] = _rel_grad_from_diagonals(_bias_grad("bias_grad", dbias)[:, 0, :])
    dz = jnp.concatenate([dq, dkk, dvv, du], axis=-1).reshape(T, -1)
    gw["w_in"] = _mm_tn("mix_dwin", hn2, dz, col_blocks=N_DEV)
    tok = emit(gw)
    dhn2 = _mm_nt_cols("mix_dhn", dz, wts["w_in"], F32, after=tok)
    dh1, dh1b, gs["mix_norm"] = _rms_bwd("mix_dnorm", h1, small["mix_norm"], dhn2, skip=dh2)
    dx, _, gs["ffn1_norm"] = ffn_bwd("ffn1", dh1, dh1b, x2, hn1, g1, u1, a1, wts["ffn1_w_gate"],
                                     wts["ffn1_w_up"], wts["ffn1_w_down"], small["ffn1_norm"])
    return loss_part, dx.reshape(B, S, D), gs


def _position():
    return lax.axis_index("x"), lax.axis_index("y"), lax.axis_index("c")


def _index(p):
    return 4 * p[0] + 2 * p[1] + p[2]


HBM_SPEC = pl.BlockSpec(memory_space=pltpu.HBM)
SEM_SPEC = pl.BlockSpec(memory_space=pltpu.SEMAPHORE)
ANY_SPEC = pl.BlockSpec(memory_space=pl.ANY)
ORDERED_EFFECT = pltpu.SideEffectType.DATAFLOW_SIDE_EFFECTING


N_COPIES = {"grads": N_DEV - 1, "spread": 4, "relay": 3}
N_SEMS = {"grads": N_DEV, "spread": 5, "relay": 3}


def _copies(pattern, srcs, lands, send, recv, base=0):
    x, y, c = _position()
    me, sibling = _index((x, y, c)), (x, y, 1 - c)
    chips = [(1 - x, y), (x, 1 - y), (1 - x, 1 - y)]
    if pattern == "grads":
        targets = [(x ^ (k >> 2), y ^ ((k >> 1) & 1), c ^ (k & 1)) for k in range(1, N_DEV)]
    else:
        targets = [sibling] + [(*chip, c) for chip in chips]
    per, slots, out = N_COPIES[pattern], N_SEMS[pattern], []
    for a in range(len(lands)):
        for k in range(per):
            if pattern == "relay":
                src = dst = lands[a].at[_index((*chips[k], c))]
                to = sibling
            else:
                to = targets[k]
                src = srcs[a].at[_index(to)] if pattern == "grads" else srcs[a]
                dst = lands[a].at[me]
            slot = base + a * slots + k
            out.append(pltpu.make_async_remote_copy(src_ref=src, dst_ref=dst, send_sem=send.at[slot],
                                                    recv_sem=recv.at[slot], device_id=to, device_id_type=MESH))
    return out


def _own_copies(pattern, srcs, lands, send, base=0):
    if pattern == "relay":
        return []
    x, y, c = _position()
    me = _index((x, y, c))
    slots = N_SEMS[pattern]
    return [pltpu.make_async_copy(srcs[a].at[me] if pattern == "grads" else srcs[a], lands[a].at[me],
                                  send.at[base + a * slots + slots - 1]) for a in range(len(lands))]


def _comm(name, srcs, lands, wait=None, start=None, after=None):
    after = [] if after is None else list(after) if isinstance(after, (list, tuple)) else [after]
    ns, nl = len(srcs), len(lands)
    na = ns + nl
    arrays = list(srcs) + list(lands)
    n_wait = 2 if wait else 0
    n_start = 2 if start else 0

    def body(*refs):
        ins, lnd = refs[:ns], refs[ns:na]
        if wait:
            base = wait[3] if len(wait) > 3 else 0
            for cp in _copies(wait[0], ins, lnd, refs[na], refs[na + 1], base):
                cp.wait_send()
                cp.wait_recv()
            for cp in _own_copies(wait[0], ins, lnd, refs[na], base):
                cp.wait()
        if start:
            outs = refs[na + n_wait + len(after):]
            for cp in _copies(start, ins, lnd, outs[0], outs[1]) + _own_copies(start, ins, lnd, outs[0]):
                cp.start()
            refs[-1][...] = jnp.zeros((8, 128), F32)

    out_shape, out_specs = [], []
    if start:
        sems = pltpu.SemaphoreType.DMA((nl * N_SEMS[start],))
        out_shape += [sems, sems]
        out_specs += [SEM_SPEC, SEM_SPEC]
    out_shape += [pltpu.HBM(a.shape, a.dtype) for a in arrays]
    out_specs += [HBM_SPEC] * na
    if start:
        out_shape.append(jax.ShapeDtypeStruct((8, 128), F32))
        out_specs.append(pl.BlockSpec(memory_space=pltpu.VMEM))
    operands = [pltpu.with_memory_space_constraint(a, pltpu.HBM) for a in arrays]
    operands += list(wait[1:3]) if wait else []
    operands += after
    res = pl.pallas_call(
        body, name=name, out_shape=out_shape, out_specs=out_specs,
        in_specs=[HBM_SPEC] * na + [SEM_SPEC] * n_wait + [ANY_SPEC] * len(after),
        input_output_aliases={i: n_start + i for i in range(na)},
        compiler_params=pltpu.CompilerParams(has_side_effects=ORDERED_EFFECT),
    )(*operands)
    res = list(res)
    thru = res[n_start:n_start + na]
    return thru[:ns], thru[ns:], (tuple(res[:2]) if start else None), (res[-1] if start else None)


def _adamw_math(w, g, m, v):
    m = ADAM_B1 * m + (1.0 - ADAM_B1) * g
    v = ADAM_B2 * v + (1.0 - ADAM_B2) * (g * g)
    m_hat = m / (1.0 - ADAM_B1 ** ADAM_STEP)
    v_hat = v / (1.0 - ADAM_B2 ** ADAM_STEP)
    delta = -ADAM_LR * (m_hat / (jnp.sqrt(v_hat) + ADAM_EPS) + ADAM_WD * w)
    return delta, m, v


def _adamw(name, parts, w, m, v, tr=128):
    R, C = w.shape
    tr = _tile(R, tr)

    def body(p_ref, w_ref, m_ref, v_ref, g_out, d_out, m_out, v_out):
        g = p_ref[0].astype(F32)
        for d in range(1, N_DEV):
            g = g + p_ref[d].astype(F32)
        g_out[...] = g
        d_out[...], m_out[...], v_out[...] = _adamw_math(w_ref[...], g, m_ref[...], v_ref[...])

    row = pl.BlockSpec((tr, C), lambda i: (i, 0))
    out = jax.ShapeDtypeStruct((R, C), F32)
    return pl.pallas_call(
        body, name=name, grid=(R // tr,),
        in_specs=[pl.BlockSpec((N_DEV, tr, C), lambda i: (0, i, 0)), row, row, row],
        out_specs=[row] * 4, out_shape=[out] * 4,
        compiler_params=_params(dimension_semantics=("parallel",)),
    )(parts, w, m, v)


def _small_allreduce_adamw(name, g, w, m, v):
    R = g.shape[0]

    def body(g_ref, w_ref, m_ref, v_ref, g_out, d_out, m_out, v_out, land, send, recv):
        x, y, c = _position()
        me = _index((x, y, c))
        land[me] = g_ref[...]
        copies = []
        for k in range(1, N_DEV):
            peer = (x ^ (k >> 2), y ^ ((k >> 1) & 1), c ^ (k & 1))
            copies.append(pltpu.make_async_remote_copy(
                src_ref=g_ref, dst_ref=land.at[me], send_sem=send.at[k - 1], recv_sem=recv.at[k - 1],
                device_id=peer, device_id_type=MESH))
        for cp in copies:
            cp.start()
        for cp in copies:
            cp.wait()
        total = land[0]
        for d in range(1, N_DEV):
            total = total + land[d]
        g_out[...] = total
        d_out[...], m_out[...], v_out[...] = _adamw_math(w_ref[...], total, m_ref[...], v_ref[...])

    vm = pl.BlockSpec(memory_space=pltpu.VMEM)
    out = jax.ShapeDtypeStruct((R, 128), F32)
    return pl.pallas_call(
        body, name=name, in_specs=[vm] * 4, out_specs=[vm] * 4, out_shape=[out] * 4,
        scratch_shapes=[pltpu.VMEM((N_DEV, R, 128), F32), pltpu.SemaphoreType.DMA((7,)),
                        pltpu.SemaphoreType.DMA((7,))],
    )(g, w, m, v)


BIG = ("ffn1_w_gate", "ffn1_w_up", "ffn1_w_down", "w_in", "w_pool", "w_out", "w_cq", "w_ckv", "w_co",
       "ffn2_w_gate", "ffn2_w_up", "ffn2_w_down")
SMALL = ("ffn1_norm", "mix_norm", "rel_bias", "pool_scale", "cross_norm", "mem_norm", "ffn2_norm", "final_norm")
ORDER = ("ffn1_norm", "ffn1_w_gate", "ffn1_w_up", "ffn1_w_down", "mix_norm", "w_in", "rel_bias", "w_pool",
         "pool_scale", "w_out", "cross_norm", "mem_norm", "w_cq", "w_ckv", "w_co", "ffn2_norm", "ffn2_w_gate",
         "ffn2_w_up", "ffn2_w_down", "final_norm")
TRANSPOSED = ("ffn1_w_gate", "ffn1_w_up", "ffn2_w_gate", "ffn2_w_up")
ROW_SHARDED = TRANSPOSED + ("ffn1_w_down", "ffn2_w_down", "w_out", "w_cq", "w_ckv")
GATHER_GROUPS = (("ffn1_w_gate",), ("ffn1_w_up",), ("ffn1_w_down",), ("w_in", "w_pool", "w_out"),
                 ("w_cq", "w_ckv", "w_co"), ("ffn2_w_gate", "ffn2_w_up", "ffn2_w_down"))
RELAY_BEFORE_USE = ((0,), (1,), (2,), (3,), (4, 5), ())


def _pack(arrays):
    flat = jnp.concatenate([a.reshape(-1) for a in arrays])
    rows = -(-flat.shape[0] // 1024) * 8
    return jnp.pad(flat, (0, rows * 128 - flat.shape[0])).reshape(rows, 128)


def _unpack(packed, like):
    flat, out, at = packed.reshape(-1), [], 0
    for a in like:
        out.append(flat[at:at + a.size].reshape(a.shape))
        at += a.size
    return out


def _shard2d(a):
    a = a[0]
    return a.reshape(-1, a.shape[-1])


def kernel(x, mem, ffn1_norm, ffn1_w_gate, ffn1_w_up, ffn1_w_down, mix_norm, w_in, rel_bias, w_pool, pool_scale, w_out, cross_norm, mem_norm, w_cq, w_ckv, w_co, ffn2_norm, ffn2_w_gate, ffn2_w_up, ffn2_w_down, final_norm, loss_target, m_ffn1_norm, m_ffn1_w_gate, m_ffn1_w_up, m_ffn1_w_down, m_mix_norm, m_w_in, m_rel_bias, m_w_pool, m_pool_scale, m_w_out, m_cross_norm, m_mem_norm, m_w_cq, m_w_ckv, m_w_co, m_ffn2_norm, m_ffn2_w_gate, m_ffn2_w_up, m_ffn2_w_down, m_final_norm, v_ffn1_norm, v_ffn1_w_gate, v_ffn1_w_up, v_ffn1_w_down, v_mix_norm, v_w_in, v_rel_bias, v_w_pool, v_pool_scale, v_w_out, v_cross_norm, v_mem_norm, v_w_cq, v_w_ckv, v_w_co, v_ffn2_norm, v_ffn2_w_gate, v_ffn2_w_up, v_ffn2_w_down, v_final_norm):
    args = dict(locals())
    def view(n, a):
        return a.transpose(0, 2, 1) if n in TRANSPOSED else a

    w_in_ = {n: view(n, args[n]) for n in ORDER}
    m_in = {n: view(n, args["m_" + n]) for n in ORDER}
    v_in = {n: view(n, args["v_" + n]) for n in ORDER}

    n_g, rows = len(POOL_WINDOWS), POOL_GROUP // N_DEV

    def landing(block_shape, dtype):
        return lax.empty((N_DEV,) + tuple(block_shape), dtype)

    gathers, tok = [], None
    for first in (True, False):
        groups = GATHER_GROUPS[:1] if first else GATHER_GROUPS[1:]
        shards = [_shard2d(w_in_[n]) for group in groups for n in group]
        shards = [(s if tok is None else s + tok[0, 0]).astype(BF16) for s in shards]
        srcs, lands, sems, tok = _comm("gather_start_%d" % (not first), shards,
                                       [landing(s.shape, BF16) for s in shards], start="spread", after=tok)
        at = 0
        for group in groups:
            gathers.append((srcs[at:at + len(group)], lands[at:at + len(group)], sems + (at * N_SEMS["spread"],)))
            at += len(group)

    def weights(gi, after):
        for ri in RELAY_BEFORE_USE[gi]:
            srcs, lands, sems = gathers[ri]
            _, lands, sems, after = _comm("gather_relay_%d" % ri, srcs, lands, wait=("spread",) + sems,
                                          start="relay", after=after)
            gathers[ri] = (None, lands, sems)
        _, lands, sems = gathers[gi]
        _, lands, _, _ = _comm("gather_finish_%d" % gi, [], lands, wait=("relay",) + sems, after=after)
        out = {}
        for n, full in zip(GATHER_GROUPS[gi], lands):
            if n == "w_pool":
                full = full.reshape(N_DEV, n_g, rows, POOL_GROUP).transpose(1, 0, 2, 3).reshape(n_g, POOL_GROUP, POOL_GROUP)
            out[n] = full.reshape(-1, full.shape[-1]) if n in ROW_SHARDED else full
        return out

    scatters = []

    def emit(gw):
        names = list(gw)
        stacks = []
        for n in names:
            g = gw[n]
            if n == "w_pool":
                g = g.reshape(n_g, N_DEV, rows, POOL_GROUP).transpose(1, 0, 2, 3).astype(BF16)
            stacks.append(g.reshape((N_DEV,) + _shard2d(w_in_[n]).shape))
        lands = [landing(s.shape[1:], s.dtype) for s in stacks]
        srcs, lands, sems, token = _comm("grads_start_%d" % len(scatters), stacks, lands, start="grads")
        scatters.append((names, srcs, lands, sems))
        return token

    small = {n: w_in_[n].reshape(1, -1) for n in SMALL if n != "rel_bias"}
    small["rel_bias"] = rel_bias[0]
    loss_part, grad_x, gs = _local_step(x, mem, loss_target, small, weights, emit, start_token=tok)

    grad, delta, new_m, new_v = {}, {}, {}, {}
    after = grad_x
    for si, (names, srcs, lands, sems) in enumerate(scatters):
        _, landed, _, _ = _comm("grads_finish_%d" % si, srcs, lands, wait=("grads",) + sems, after=after)
        for n, parts in zip(names, landed):
            res = _adamw("adamw_" + n, parts, _shard2d(w_in_[n]), _shard2d(m_in[n]), _shard2d(v_in[n]))
            grad[n], delta[n], new_m[n], new_v[n] = [view(n, r.reshape(w_in_[n].shape)) for r in res]
        after = res[0]

    slot = jnp.zeros((1,), F32)
    like = [w_in_[n] for n in SMALL] + [slot]
    gs["rel_bias"] = gs["rel_bias"].reshape(rel_bias.shape)
    res = _small_allreduce_adamw("small_params", _pack([gs[n] for n in SMALL] + [loss_part[0, :1]]), _pack(like),
                                 _pack([m_in[n] for n in SMALL] + [slot]), _pack([v_in[n] for n in SMALL] + [slot]))
    for d, packed in zip((grad, delta, new_m, new_v), res):
        for n, a in zip(SMALL, _unpack(packed, like)):
            d[n] = a
    loss = _unpack(res[0], like)[-1][0]
    return (loss, grad_x, *[grad[n] for n in ORDER], *[delta[n] for n in ORDER],
            *[new_m[n] for n in ORDER], *[new_v[n] for n in ORDER])
```

```python
import jax
import jax.numpy as jnp
from jax import lax
from jax.experimental import pallas as pl
from jax.experimental.pallas import tpu as pltpu

F32 = jnp.float32
BF16 = jnp.bfloat16

N_DEV = 8
EPS = 1e-6
NEG_INF = -1e30
CHUNK = 64
LEFT_CHUNKS = 8
PAD = LEFT_CHUNKS * CHUNK
QBLK = 4 * CHUNK
KBAND = PAD + QBLK
REL_CLIP = 128
ATTN_HEADS = 16
HEAD_DIM = 64
D_ATTN = ATTN_HEADS * HEAD_DIM
POOL_WINDOWS = (2, 4, 8, 16)
POOL_GROUP = 256
D_POOL = len(POOL_WINDOWS) * POOL_GROUP
CROSS_HEADS = 4
CROSS_DIM = 128
D_CROSS = CROSS_HEADS * CROSS_DIM
FFN_RES = 0.5
ADAM_LR, ADAM_B1, ADAM_B2, ADAM_EPS, ADAM_WD, ADAM_STEP = 0.001, 0.9, 0.999, 1e-08, 0.01, 10

NN = (((1,), (0,)), ((), ()))
NT = (((1,), (1,)), ((), ()))
TN = (((0,), (0,)), ((), ()))
MESH = pl.DeviceIdType.MESH
VMEM_LIMIT = 56 * 1024 * 1024


def _params(**kw):
    return pltpu.CompilerParams(vmem_limit_bytes=VMEM_LIMIT, **kw)


def _bf(v):
    return v if v.dtype == BF16 else v.astype(BF16)


WHOLE = ((Ellipsis,), (Ellipsis,))


def _rms(xv, gain):
    return (xv * lax.rsqrt(jnp.mean(xv * xv, axis=-1, keepdims=True) + EPS)) * gain


def _gemm(name, a, a_spec, b, b_spec, dims, grid, outs, chunks=(WHOLE,), extras=(), epilogue=None, after=None,
          norm=None):
    nex, nout = len(extras), len(outs)
    first_out = 2 + nex + (after is not None) + (norm is not None)

    def body(*refs):
        a_ref, b_ref = refs[:2]
        if norm is not None:
            hn_out, a_ref = refs[first_out + nout], refs[-1]

            @pl.when(pl.program_id(1) == 0)
            def _():
                hn = _rms(refs[0][...], refs[first_out - 1][...]).astype(BF16)
                a_ref[...] = hn
                hn_out[...] = hn

        total = None
        for ia, ib in chunks:
            d = lax.dot_general(_bf(a_ref[ia]), _bf(b_ref[ib]), dims, preferred_element_type=F32)
            total = d if total is None else total + d
        vals = epilogue(total, *[e[...] for e in refs[2:2 + nex]]) if epilogue is not None else (total,)
        for r, v in zip(refs[first_out:first_out + nout], vals):
            r[...] = v.astype(r.dtype)

    operands = [a, b] + [x for x, _, _ in extras]
    in_specs = [pl.BlockSpec(*a_spec), pl.BlockSpec(*b_spec)] + [pl.BlockSpec(blk, m) for _, blk, m in extras]
    if after is not None:
        operands.append(after)
        in_specs.append(pl.BlockSpec(after.shape, lambda i, j: (0, 0)))
    out_specs = [pl.BlockSpec(blk, m) for _, _, blk, m in outs]
    out_shape = [jax.ShapeDtypeStruct(s, d) for s, d, _, _ in outs]
    scratch = []
    if norm is not None:
        operands.append(norm)
        in_specs.append(pl.BlockSpec(norm.shape, lambda i, j: (0, 0)))
        out_specs.append(pl.BlockSpec(*a_spec))
        out_shape.append(jax.ShapeDtypeStruct(a.shape, BF16))
        scratch.append(pltpu.VMEM(a_spec[0], BF16))
    res = pl.pallas_call(
        body, name=name, grid=grid, in_specs=in_specs, out_specs=out_specs, out_shape=out_shape,
        scratch_shapes=scratch,
        compiler_params=_params(dimension_semantics=("parallel", "arbitrary" if norm is not None else "parallel")),
    )(*operands)
    return res[0] if len(res) == 1 else res


def _tile(n, want):
    for t in range(min(n, want), 15, -1):
        if n % t == 0 and t % 16 == 0:
            return t
    return n


def _mm_nn(name, a, b, out_dtype, res=None, tm=1024, tn=1024, norm=None):
    M, K = a.shape
    N = b.shape[1]
    tm, tn = _tile(M, tm), _tile(N, tn)
    extras = [] if res is None else [(res, (tm, tn), lambda i, j: (i, j))]
    epi = None if res is None else (lambda t, r: (r + t,))
    return _gemm(name, a, ((tm, K), lambda i, j: (i, 0)), b, ((K, tn), lambda i, j: (0, j)), NN,
                 (M // tm, N // tn), [((M, N), out_dtype, (tm, tn), lambda i, j: (i, j))], extras=extras, epilogue=epi,
                 norm=norm)


def _mm_nn_cols(name, a, bs, out_dtype, res=None, tm=1024, norm=None):
    M, K = a.shape
    nb, _, w = bs.shape
    tm = _tile(M, tm)
    extras = [] if res is None else [(res, (tm, w), lambda i, j: (i, j))]
    epi = None if res is None else (lambda t, r: (r + t,))
    return _gemm(name, a, ((tm, K), lambda i, j: (i, 0)), bs, ((None, K, w), lambda i, j: (j, 0, 0)), NN,
                 (M // tm, nb), [((M, nb * w), out_dtype, (tm, w), lambda i, j: (i, j))], extras=extras, epilogue=epi,
                 norm=norm)


def _mm_nt(name, a, b, out_dtype, tm=1024, tn=1024, after=None):
    M, K = a.shape
    N = b.shape[0]
    tm, tn = _tile(M, tm), _tile(N, tn)
    return _gemm(name, a, ((tm, K), lambda i, j: (i, 0)), b, ((tn, K), lambda i, j: (j, 0)), NT,
                 (M // tm, N // tn), [((M, N), out_dtype, (tm, tn), lambda i, j: (i, j))], after=after)


def _mm_nt_cols(name, a, bs, out_dtype, tm=1024, tn=512, after=None):
    M = a.shape[0]
    nb, N, w = bs.shape
    tm, tn = _tile(M, tm), _tile(N, tn)
    chunks = [((slice(None), pl.ds(c * w, w)), (c,)) for c in range(nb)]
    return _gemm(name, a, ((tm, nb * w), lambda i, j: (i, 0)), bs, ((nb, tn, w), lambda i, j: (0, j, 0)), NT,
                 (M // tm, N // tn), [((M, N), out_dtype, (tm, tn), lambda i, j: (i, j))], chunks, after=after)


def _mm_tn(name, a, b, tm=1024, tn=1024, col_blocks=None):
    T, Ka = a.shape
    Nb = b.shape[1]
    tm = _tile(Ka, tm)
    if col_blocks is None:
        tn = _tile(Nb, tn)
        out = ((Ka, Nb), BF16, (tm, tn), lambda i, j: (i, j))
    else:
        tn = Nb // col_blocks
        out = ((col_blocks, Ka, tn), BF16, (None, tm, tn), lambda i, j: (j, i, 0))
    return _gemm(name, a, ((T, tm), lambda i, j: (0, i)), b, ((T, tn), lambda i, j: (0, j)), TN,
                 (Ka // tm, Nb // tn), [out])


def _ffn_out(name, a, wt, res=None, scale=1.0, tm=1024, tn=512, after=None):
    nb, M, w = a.shape
    N = wt.shape[1]
    tm, tn = _tile(M, tm), _tile(N, tn)
    chunks = [((c,), (pl.ds(c * w, w),)) for c in range(nb)]
    extras = [] if res is None else [(res, (tm, tn), lambda i, j: (i, j))]
    epi = None if res is None else (lambda t, r: (r + scale * t,))
    return _gemm(name, a, ((nb, tm, w), lambda i, j: (0, i, 0)), wt, ((nb * w, tn), lambda i, j: (0, j)), NN,
                 (M // tm, N // tn), [((M, N), F32, (tm, tn), lambda i, j: (i, j))], chunks, extras, epi, after)


def _ffn_dact(name, dhb, wd, g, u, tm=1024, after=None):
    M, K = dhb.shape
    nb, _, w = g.shape
    tm = _tile(M, tm)
    tr = _tile(tm, 256)
    tokens = [] if after is None else [after]

    def body(dh_ref, wd_ref, g_ref, u_ref, *rest):
        dg_ref, du_ref = rest[-2:]
        pieces = [pl.ds(r * tr, tr) for r in range(tm // tr)]

        def product(rows):
            return lax.dot_general(dh_ref[rows, :], wd_ref[...], NT, preferred_element_type=F32)

        results = []
        dact = product(pieces[0])
        for n, rows in enumerate(pieces):
            ahead = product(pieces[n + 1]) if n + 1 < len(pieces) else None
            results.append(_swiglu_bwd(dact, g_ref[rows, :], u_ref[rows, :]))
            dact = ahead
        for rows, (dg, du) in zip(pieces, results):
            dg_ref[rows, :] = dg.astype(BF16)
            du_ref[rows, :] = du.astype(BF16)

    hid = pl.BlockSpec((None, tm, w), lambda i, j: (j, i, 0))
    return pl.pallas_call(
        body, name=name, grid=(M // tm, nb),
        in_specs=[pl.BlockSpec((tm, K), lambda i, j: (i, 0)), pl.BlockSpec((w, K), lambda i, j: (j, 0)), hid, hid]
        + [pl.BlockSpec(t.shape, lambda i, j: (0, 0)) for t in tokens],
        out_specs=[hid, hid], out_shape=[jax.ShapeDtypeStruct((nb, M, w), BF16)] * 2,
        compiler_params=_params(dimension_semantics=("parallel", "parallel")),
    )(dhb, wd, g, u, *tokens)


def _ffn_dw(name, a, b, scale=1.0, tn=1024, after=None):
    nb, T, w = a.shape
    N = b.shape[1]
    tn = _tile(N, tn)
    epi = None if scale == 1.0 else (lambda t: (t * scale,))
    return _gemm(name, a, ((None, T, w), lambda i, j: (i, 0, 0)), b, ((T, tn), lambda i, j: (0, j)), TN,
                 (nb, N // tn), [((nb * w, N), BF16, (w, tn), lambda i, j: (i, j))], epilogue=epi, after=after)


def _ffn_gate(name, hn, wgt, nb, tm=1024):
    M, K = hn.shape
    w = wgt.shape[0] // nb
    tm = _tile(M, tm)
    return _gemm(name, hn, ((tm, K), lambda i, j: (i, 0)), wgt, ((w, K), lambda i, j: (j, 0)), NT,
                 (M // tm, nb), [((nb, M, w), BF16, (None, tm, w), lambda i, j: (j, i, 0))])


def _ffn_up_act(name, hn, wut, g, tm=1024):
    M, K = hn.shape
    nb, _, w = g.shape
    tm = _tile(M, tm)
    hid = ((None, tm, w), lambda i, j: (j, i, 0))

    def epilogue(u, gate):
        gate = gate.astype(F32)
        return u, gate * jax.nn.sigmoid(gate) * u

    return _gemm(name, hn, ((tm, K), lambda i, j: (i, 0)), wut, ((w, K), lambda i, j: (j, 0)), NT,
                 (M // tm, nb), [((nb, M, w), BF16) + hid] * 2, extras=[(g,) + hid], epilogue=epilogue)


def _ffn_up(name, h, gain, wgt, wut, nb, tm=1024):
    M, K = h.shape
    w = wgt.shape[0] // nb
    tm = _tile(M, tm)

    def body(h_ref, gain_ref, g_ref, u_ref, hn_ref, og, ou, oa, a_ref):
        @pl.when(pl.program_id(1) == 0)
        def _():
            hn = _rms(h_ref[...], gain_ref[...]).astype(BF16)
            a_ref[...] = hn
            hn_ref[...] = hn

        a = a_ref[...]
        g = lax.dot_general(a, g_ref[...], NT, preferred_element_type=F32)
        u = lax.dot_general(a, u_ref[...], NT, preferred_element_type=F32)
        og[...] = g.astype(BF16)
        ou[...] = u.astype(BF16)
        oa[...] = (g * jax.nn.sigmoid(g) * u).astype(BF16)

    rows = pl.BlockSpec((tm, K), lambda i, j: (i, 0))
    wspec = pl.BlockSpec((w, K), lambda i, j: (j, 0))
    ospec = pl.BlockSpec((None, tm, w), lambda i, j: (j, i, 0))
    return pl.pallas_call(
        body, name=name, grid=(M // tm, nb),
        in_specs=[rows, pl.BlockSpec((1, K), lambda i, j: (0, 0)), wspec, wspec],
        out_specs=[rows] + [ospec] * 3,
        out_shape=[jax.ShapeDtypeStruct((M, K), BF16)] + [jax.ShapeDtypeStruct((nb, M, w), BF16)] * 3,
        scratch_shapes=[pltpu.VMEM((tm, K), BF16)],
        compiler_params=_params(dimension_semantics=("parallel", "arbitrary")),
    )(h, gain, wgt, wut)


def _swiglu_bwd(dact, g, u):
    g = g.astype(F32)
    u = u.astype(F32)
    sig = jax.nn.sigmoid(g)
    silu = g * sig
    d = FFN_RES * dact
    return d * u * (sig * (1.0 + g * (1.0 - sig))), d * silu


def _rms_fwd(name, x, gain, tr=512, after=None):
    R, D = x.shape
    tr = _tile(R, tr)

    def body(x_ref, g_ref, *rest):
        xv = x_ref[...]
        y = xv * lax.rsqrt(jnp.mean(xv * xv, axis=-1, keepdims=True) + EPS)
        rest[-1][...] = (y * g_ref[...]).astype(BF16)

    tokens = [] if after is None else [after]
    return pl.pallas_call(
        body, name=name, grid=(R // tr,),
        in_specs=[pl.BlockSpec((tr, D), lambda i: (i, 0)), pl.BlockSpec((1, D), lambda i: (0, 0))]
        + [pl.BlockSpec(t.shape, lambda i: (0, 0)) for t in tokens],
        out_specs=pl.BlockSpec((tr, D), lambda i: (i, 0)), out_shape=jax.ShapeDtypeStruct((R, D), BF16),
        compiler_params=_params(dimension_semantics=("parallel",)),
    )(x, gain, *tokens)


def _rms_bwd_math(xv, gain, dy):
    rstd = lax.rsqrt(jnp.mean(xv * xv, axis=-1, keepdims=True) + EPS)
    xhat = xv * rstd
    dxh = dy * gain
    dx = rstd * (dxh - xhat * jnp.mean(dxh * xhat, axis=-1, keepdims=True))
    return dx, jnp.sum(dy * xhat, axis=0, keepdims=True)


def _rms_bwd(name, x, gain, dy, skip=None, out_dtype=BF16, tr=256):
    R, D = x.shape
    tr = _tile(R, tr)
    has_skip = skip is not None

    def body(*refs):
        x_ref, g_ref, dy_ref = refs[:3]
        dx_ref, dg_ref = refs[-2:]
        dx, dg = _rms_bwd_math(x_ref[...], g_ref[...], dy_ref[...].astype(F32))
        if has_skip:
            dx = dx + refs[3][...].astype(F32)
        dx_ref[...] = dx.astype(out_dtype)

        @pl.when(pl.program_id(0) == 0)
        def _():
            dg_ref[...] = dg

        @pl.when(pl.program_id(0) > 0)
        def _():
            dg_ref[...] += dg

    row = pl.BlockSpec((tr, D), lambda i: (i, 0))
    vec = pl.BlockSpec((1, D), lambda i: (0, 0))
    return pl.pallas_call(
        body, name=name, grid=(R // tr,),
        in_specs=[row, vec, row] + ([row] if has_skip else []),
        out_specs=[row, vec],
        out_shape=[jax.ShapeDtypeStruct((R, D), out_dtype), jax.ShapeDtypeStruct((1, D), F32)],
        compiler_params=_params(dimension_semantics=("arbitrary",)),
    )(*([x, gain, dy] + ([skip] if has_skip else [])))


def _loss_and_grad(name, h, gain, target, tr=256):
    R, D = h.shape
    tr = _tile(R, tr)

    def body(h_ref, g_ref, t_ref, loss_ref, dhb_ref, dg_ref):
        hv, gain_v = h_ref[...], g_ref[...]
        y = (hv * lax.rsqrt(jnp.mean(hv * hv, axis=-1, keepdims=True) + EPS)) * gain_v
        err = y - t_ref[...]
        part = jnp.full((8, 128), 0.5 * jnp.sum(jnp.mean(err * err, axis=-1, keepdims=True)), F32)
        dh, dg = _rms_bwd_math(hv, gain_v, err * (1.0 / D))
        dhb_ref[...] = dh.astype(BF16)

        @pl.when(pl.program_id(0) == 0)
        def _():
            dg_ref[...] = dg
            loss_ref[...] = part

        @pl.when(pl.program_id(0) > 0)
        def _():
            dg_ref[...] += dg
            loss_ref[...] += part

    row = pl.BlockSpec((tr, D), lambda i: (i, 0))
    vec = pl.BlockSpec((1, D), lambda i: (0, 0))
    return pl.pallas_call(
        body, name=name, grid=(R // tr,), in_specs=[row, vec, row],
        out_specs=[pl.BlockSpec((8, 128), lambda i: (0, 0)), row, vec],
        out_shape=[jax.ShapeDtypeStruct((8, 128), F32), jax.ShapeDtypeStruct((R, D), BF16),
                   jax.ShapeDtypeStruct((1, D), F32)],
        compiler_params=_params(dimension_semantics=("arbitrary",)),
    )(h, gain, target)


def _bias_tile(name, rel):
    width = KBAND + QBLK
    sat = rel[:, 2 * REL_CLIP:]
    n_left = PAD - REL_CLIP + 1
    row0 = jnp.concatenate([jnp.broadcast_to(sat, (ATTN_HEADS, n_left)), rel[:, :2 * REL_CLIP][:, ::-1],
                            jnp.broadcast_to(sat, (ATTN_HEADS, width - n_left - 2 * REL_CLIP))], axis=1)

    def body(e_ref, o_ref):
        rows = pltpu.roll(jnp.broadcast_to(e_ref[...], (QBLK, width)), 0, 1, stride=1, stride_axis=0)
        i = lax.broadcasted_iota(jnp.int32, (QBLK, KBAND), 0) // CHUNK
        j = lax.broadcasted_iota(jnp.int32, (QBLK, KBAND), 1) // CHUNK
        o_ref[...] = jnp.where((j >= i) & (j <= i + LEFT_CHUNKS), rows[:, :KBAND], NEG_INF)

    return pl.pallas_call(
        body, name=name, grid=(ATTN_HEADS,),
        in_specs=[pl.BlockSpec((None, 1, width), lambda h: (h, 0, 0))],
        out_specs=pl.BlockSpec((None, QBLK, KBAND), lambda h: (h, 0, 0)),
        out_shape=jax.ShapeDtypeStruct((ATTN_HEADS, QBLK, KBAND), F32),
        compiler_params=_params(dimension_semantics=("parallel",)),
    )(row0.reshape(ATTN_HEADS, 1, width))


ATTN_SCALE = HEAD_DIM ** -0.5
ROW_PIECES = 1


def _stack_heads(x, first):
    zero = jnp.zeros_like(x)
    return jnp.concatenate([jnp.where(first, x, zero), jnp.where(first, zero, x)], axis=0)


def _band_softmax(q_half_scaled, kb, bias, left_mask):
    s = lax.dot_general(q_half_scaled, kb, NT, preferred_element_type=F32) + bias + left_mask
    e = jnp.exp(s - jnp.max(s, axis=-1, keepdims=True))
    return e * (1.0 / jnp.sum(e, axis=-1, keepdims=True))


def _left_mask(qb):
    kpos = qb * QBLK - PAD + lax.broadcasted_iota(jnp.int32, (1, KBAND), 1)
    return jnp.where(kpos >= 0, 0.0, NEG_INF).astype(F32)


def _fill_padded(dst, src, S):
    dst[pl.ds(0, PAD), :] = jnp.zeros((PAD, dst.shape[1]), dst.dtype)
    dst[pl.ds(PAD, S), :] = src[...].astype(dst.dtype)


def _attn_fwd(name, z, bias):
    B, S, _ = z.shape
    nh2 = ATTN_HEADS // 2

    def body(q_ref, k_ref, v_ref, b_ref, o_ref, kp, vp):
        qb = pl.program_id(2)

        @pl.when(qb == 0)
        def _():
            _fill_padded(kp, k_ref, S)
            _fill_padded(vp, v_ref, S)

        start = pl.multiple_of(qb * QBLK, QBLK)
        kb, vb = kp[pl.ds(start, KBAND), :], vp[pl.ds(start, KBAND), :]
        q = (q_ref[...] * ATTN_SCALE).astype(BF16)
        first = lax.broadcasted_iota(jnp.int32, (QBLK, 2 * HEAD_DIM), 1) < HEAD_DIM
        left = _left_mask(qb)
        zero = jnp.zeros_like(q)
        qh = [jnp.where(first, q, zero), jnp.where(first, zero, q)]
        rp = QBLK // ROW_PIECES
        chains = [(a, r) for r in range(ROW_PIECES) for a in range(2)]
        ss = [lax.dot_general(qh[a][r * rp:(r + 1) * rp], kb, NT, preferred_element_type=F32) for a, r in chains]
        ps = []
        for (a, r), s in zip(chains, ss):
            s = s + b_ref[a, pl.ds(r * rp, rp), :] + left
            e = jnp.exp(s - jnp.max(s, axis=-1, keepdims=True))
            ps.append((e * (1.0 / jnp.sum(e, axis=-1, keepdims=True))).astype(BF16))
        os_ = [jnp.dot(p, vb, preferred_element_type=F32) for p in ps]
        for r in range(ROW_PIECES):
            o_ref[pl.ds(r * rp, rp), :] = jnp.where(first[:rp], os_[2 * r], os_[2 * r + 1]).astype(BF16)

    return pl.pallas_call(
        body, name=name, grid=(B, nh2, S // QBLK),
        in_specs=[pl.BlockSpec((None, QBLK, 128), lambda b, h, i: (b, i, h)),
                  pl.BlockSpec((None, S, 128), lambda b, h, i: (b, 0, nh2 + h)),
                  pl.BlockSpec((None, S, 128), lambda b, h, i: (b, 0, 2 * nh2 + h)),
                  pl.BlockSpec((2, QBLK, KBAND), lambda b, h, i: (h, 0, 0))],
        out_specs=pl.BlockSpec((None, QBLK, 128), lambda b, h, i: (b, i, h)),
        out_shape=jax.ShapeDtypeStruct((B, S, D_ATTN + D_POOL), BF16),
        scratch_shapes=[pltpu.VMEM((PAD + S, 128), BF16), pltpu.VMEM((PAD + S, 128), BF16)],
        compiler_params=_params(dimension_semantics=("parallel", "parallel", "arbitrary")),
    )(z, z, z, bias)


def _attn_bwd(name, z, bias, dcat):
    B, S, _ = z.shape
    nh2 = ATTN_HEADS // 2
    nqb = S // QBLK

    def body(q_ref, k_ref, v_ref, b_ref, do_ref, dq_ref, dk_ref, dv_ref, db_ref, kp, vp, dka, dva):
        b, qb = pl.program_id(1), pl.program_id(2)

        @pl.when(qb == 0)
        def _():
            _fill_padded(kp, k_ref, S)
            _fill_padded(vp, v_ref, S)
            dka[...] = jnp.zeros_like(dka)
            dva[...] = jnp.zeros_like(dva)

        @pl.when((qb == 0) & (b == 0))
        def _():
            db_ref[...] = jnp.zeros_like(db_ref)

        start = pl.multiple_of(qb * QBLK, QBLK)
        band = pl.ds(start, KBAND)
        kb, vb = kp[band, :], vp[band, :]
        q = (q_ref[...] * ATTN_SCALE).astype(BF16)
        do = do_ref[...]
        first = lax.broadcasted_iota(jnp.int32, (QBLK, 2 * HEAD_DIM), 1) < HEAD_DIM
        left = _left_mask(qb)
        q2, do2 = _stack_heads(q, first), _stack_heads(do, first)
        p = _band_softmax(q2, kb, b_ref[...].reshape(2 * QBLK, KBAND), left)
        dp = lax.dot_general(do2, vb, NT, preferred_element_type=F32)
        ds = p * (dp - jnp.sum(p * dp, axis=-1, keepdims=True))
        db_ref[...] += ds.reshape(2, QBLK, KBAND)
        dsb = ds.astype(BF16)
        dq = jnp.dot(dsb, kb, preferred_element_type=F32)
        dq_ref[...] = (jnp.where(first, dq[:QBLK], dq[QBLK:]) * ATTN_SCALE).astype(BF16)
        dka[:, band] += lax.dot_general(q2, dsb, TN, preferred_element_type=F32)
        dva[:, band] += lax.dot_general(do2, p.astype(BF16), TN, preferred_element_type=F32)

        @pl.when(qb == nqb - 1)
        def _():
            dk_ref[...] = dka[:, pl.ds(PAD, S)].T.astype(BF16)
            dv_ref[...] = dva[:, pl.ds(PAD, S)].T.astype(BF16)

    qspec = pl.BlockSpec((None, QBLK, 128), lambda h, b, i: (b, i, h))
    kvout = pl.BlockSpec((None, S, 128), lambda h, b, i: (b, 0, h))
    bspec = pl.BlockSpec((2, QBLK, KBAND), lambda h, b, i: (h, 0, 0))
    act = jax.ShapeDtypeStruct((B, S, D_ATTN), BF16)
    return pl.pallas_call(
        body, name=name, grid=(nh2, B, nqb),
        in_specs=[qspec,
                  pl.BlockSpec((None, S, 128), lambda h, b, i: (b, 0, nh2 + h)),
                  pl.BlockSpec((None, S, 128), lambda h, b, i: (b, 0, 2 * nh2 + h)),
                  bspec, qspec],
        out_specs=[qspec, kvout, kvout, bspec],
        out_shape=[act, act, act, jax.ShapeDtypeStruct((ATTN_HEADS, QBLK, KBAND), F32)],
        scratch_shapes=[pltpu.VMEM((PAD + S, 128), BF16), pltpu.VMEM((PAD + S, 128), BF16),
                        pltpu.VMEM((128, PAD + S), F32), pltpu.VMEM((128, PAD + S), F32)],
        compiler_params=_params(dimension_semantics=("arbitrary", "arbitrary", "arbitrary")),
    )(z, z, z, bias, dcat)


def _bias_grad(name, dbias):
    width = KBAND + QBLK

    def body(d_ref, o_ref):
        acc = jnp.zeros((1, width), F32)
        for i in range(QBLK):
            row = jnp.concatenate([d_ref[pl.ds(i, 1), :], jnp.zeros((1, QBLK), F32)], axis=1)
            shift = QBLK - 1 - i
            acc = acc + (pltpu.roll(row, shift, 1) if shift else row)
        o_ref[...] = acc

    return pl.pallas_call(
        body, name=name, grid=(ATTN_HEADS,),
        in_specs=[pl.BlockSpec((None, QBLK, KBAND), lambda h: (h, 0, 0))],
        out_specs=pl.BlockSpec((None, 1, width), lambda h: (h, 0, 0)),
        out_shape=jax.ShapeDtypeStruct((ATTN_HEADS, 1, width), F32),
        compiler_params=_params(dimension_semantics=("parallel",)),
    )(dbias)


def _rel_grad_from_diagonals(diag):
    top = PAD + QBLK - 1 - REL_CLIP
    sat = jnp.sum(diag[:, :top + 1], axis=1, keepdims=True)
    mid = diag[:, top + 1:top + 2 * REL_CLIP][:, ::-1]
    return jnp.concatenate([jnp.zeros_like(sat), mid, sat], axis=1)


def _shift_rows(x, k, forward):
    S = x.shape[0]
    t = lax.broadcasted_iota(jnp.int32, x.shape, 0)
    if forward:
        return jnp.where(t < S - k, pltpu.roll(x, S - k, 0), 0.0)
    return jnp.where(t >= k, pltpu.roll(x, k, 0), 0.0)


def _window_sum(x, g, forward):
    s = x + _shift_rows(x, 1, forward)
    out = s
    for n, k in enumerate((2, 4, 8)):
        s = s + _shift_rows(s, k, forward)
        out = jnp.where(g > n, s, out)
    return out


def _pool_count(S, g):
    t = lax.broadcasted_iota(jnp.int32, (S, 1), 0)
    w = jnp.left_shift(2, g)
    return jnp.minimum(t + 1, w).astype(F32)


def _pool_fwd(name, z, wp, pscale, mixed):
    B, S, _ = z.shape
    c0 = 3 * D_ATTN // POOL_GROUP
    y0 = D_ATTN // POOL_GROUP

    def body(u_ref, w_ref, s_ref, mixed_ref, d_ref, y_ref):
        g = pl.program_id(1)
        u = u_ref[...]
        d = (_window_sum(u, g, False) / _pool_count(S, g) - u).astype(BF16)
        d_ref[...] = d
        y_ref[...] = (jnp.dot(d, w_ref[...], preferred_element_type=F32) * s_ref[...]).astype(BF16)

    return pl.pallas_call(
        body, name=name, grid=(B, len(POOL_WINDOWS)),
        in_specs=[pl.BlockSpec((None, S, POOL_GROUP), lambda b, g: (b, 0, c0 + g)),
                  pl.BlockSpec((None, POOL_GROUP, POOL_GROUP), lambda b, g: (g, 0, 0)),
                  pl.BlockSpec((1, POOL_GROUP), lambda b, g: (0, g)),
                  pl.BlockSpec(memory_space=pl.ANY)],
        out_specs=[pl.BlockSpec((None, S, POOL_GROUP), lambda b, g: (b, 0, g)),
                   pl.BlockSpec((None, S, POOL_GROUP), lambda b, g: (b, 0, y0 + g))],
        out_shape=[jax.ShapeDtypeStruct((B, S, D_POOL), BF16), jax.ShapeDtypeStruct(mixed.shape, BF16)],
        input_output_aliases={3: 1},
        compiler_params=_params(dimension_semantics=("parallel", "parallel")),
    )(z, wp, pscale, mixed)


def _pool_bwd(name, d, wp, pscale, dcat):
    B, S, _ = d.shape
    c0 = D_ATTN // POOL_GROUP

    def body(d_ref, w_ref, s_ref, dy_ref, du_ref, dw_ref, dsc_ref):
        g, b = pl.program_id(0), pl.program_id(1)
        dv = d_ref[...]
        dy = dy_ref[...].astype(F32)
        w = w_ref[...]
        ypre = jnp.dot(dv, w, preferred_element_type=F32)
        dyp = (dy * s_ref[...]).astype(BF16)
        dd = lax.dot_general(dyp, w, NT, preferred_element_type=F32)
        du_ref[...] = (_window_sum(dd / _pool_count(S, g), g, True) - dd).astype(BF16)
        dw = lax.dot_general(dv, dyp, TN, preferred_element_type=F32)
        dsc = jnp.sum(dy * ypre, axis=0, keepdims=True)

        @pl.when(b == 0)
        def _():
            dw_ref[...] = dw
            dsc_ref[...] = dsc

        @pl.when(b > 0)
        def _():
            dw_ref[...] += dw
            dsc_ref[...] += dsc

    blk = pl.BlockSpec((None, S, POOL_GROUP), lambda g, b: (b, 0, g))
    wspec = pl.BlockSpec((None, POOL_GROUP, POOL_GROUP), lambda g, b: (g, 0, 0))
    sspec = pl.BlockSpec((1, POOL_GROUP), lambda g, b: (0, g))
    return pl.pallas_call(
        body, name=name, grid=(len(POOL_WINDOWS), B),
        in_specs=[blk, wspec, sspec, pl.BlockSpec((None, S, POOL_GROUP), lambda g, b: (b, 0, c0 + g))],
        out_specs=[blk, wspec, sspec],
        out_shape=[jax.ShapeDtypeStruct((B, S, D_POOL), BF16),
                   jax.ShapeDtypeStruct((len(POOL_WINDOWS), POOL_GROUP, POOL_GROUP), F32),
                   jax.ShapeDtypeStruct((1, D_POOL), F32)],
        compiler_params=_params(dimension_semantics=("arbitrary", "arbitrary")),
    )(d, wp, pscale, dcat)


def _cross_softmax(q, k):
    s = lax.dot_general(q, k, NT, preferred_element_type=F32) * (CROSS_DIM ** -0.5)
    e = jnp.exp(s - jnp.max(s, axis=-1, keepdims=True))
    return e * (1.0 / jnp.sum(e, axis=-1, keepdims=True))


def _cross_fwd(name, qc, kv, tq=1024):
    B, S, _ = qc.shape
    M = kv.shape[1]
    tq = _tile(S, tq)

    def body(q_ref, k_ref, v_ref, o_ref):
        p = _cross_softmax(q_ref[...], k_ref[...])
        o_ref[...] = jnp.dot(p.astype(BF16), v_ref[...], preferred_element_type=F32).astype(BF16)

    qspec = pl.BlockSpec((None, tq, CROSS_DIM), lambda b, h, i: (b, i, h))
    return pl.pallas_call(
        body, name=name, grid=(B, CROSS_HEADS, S // tq),
        in_specs=[qspec, pl.BlockSpec((None, M, CROSS_DIM), lambda b, h, i: (b, 0, h)),
                  pl.BlockSpec((None, M, CROSS_DIM), lambda b, h, i: (b, 0, CROSS_HEADS + h))],
        out_specs=qspec, out_shape=jax.ShapeDtypeStruct((B, S, D_CROSS), BF16),
        compiler_params=_params(dimension_semantics=("parallel", "parallel", "parallel")),
    )(qc, kv, kv)


def _cross_bwd(name, qc, kv, do, tq=1024):
    B, S, _ = qc.shape
    M = kv.shape[1]
    tq = _tile(S, tq)
    nq = S // tq
    scale = CROSS_DIM ** -0.5

    def body(q_ref, k_ref, v_ref, do_ref, dq_ref, dk_ref, dv_ref, dka, dva):
        i = pl.program_id(2)
        q, k, v, dov = q_ref[...], k_ref[...], v_ref[...], do_ref[...]
        p = _cross_softmax(q, k)
        dp = lax.dot_general(dov, v, NT, preferred_element_type=F32)
        ds = ((p * (dp - jnp.sum(p * dp, axis=-1, keepdims=True))) * scale).astype(BF16)
        dq_ref[...] = jnp.dot(ds, k, preferred_element_type=F32).astype(BF16)
        dk = lax.dot_general(ds, q, TN, preferred_element_type=F32)
        dv = lax.dot_general(p.astype(BF16), dov, TN, preferred_element_type=F32)

        @pl.when(i == 0)
        def _():
            dka[...] = dk
            dva[...] = dv

        @pl.when(i > 0)
        def _():
            dka[...] += dk
            dva[...] += dv

        @pl.when(i == nq - 1)
        def _():
            dk_ref[...] = dka[...].astype(BF16)
            dv_ref[...] = dva[...].astype(BF16)

    qspec = pl.BlockSpec((None, tq, CROSS_DIM), lambda b, h, i: (b, i, h))
    kspec = pl.BlockSpec((None, M, CROSS_DIM), lambda b, h, i: (b, 0, h))
    return pl.pallas_call(
        body, name=name, grid=(B, CROSS_HEADS, nq),
        in_specs=[qspec, kspec, pl.BlockSpec((None, M, CROSS_DIM), lambda b, h, i: (b, 0, CROSS_HEADS + h)), qspec],
        out_specs=[qspec, kspec, kspec],
        out_shape=[jax.ShapeDtypeStruct((B, S, D_CROSS), BF16), jax.ShapeDtypeStruct((B, M, D_CROSS), BF16),
                   jax.ShapeDtypeStruct((B, M, D_CROSS), BF16)],
        scratch_shapes=[pltpu.VMEM((M, CROSS_DIM), F32), pltpu.VMEM((M, CROSS_DIM), F32)],
        compiler_params=_params(dimension_semantics=("parallel", "parallel", "arbitrary")),
    )(qc, kv, kv, do)


def _local_step(x, mem, target, small, weights, emit, start_token=None):
    B, S, D = x.shape
    T = B * S
    x2, t2 = x.reshape(T, D), target.reshape(T, D)
    mem2 = mem.reshape(-1, D)
    n_mem = mem.shape[1]
    wts = {}

    hn1 = _rms_fwd("norm_ffn1", x2, small["ffn1_norm"], after=start_token)
    memn = _rms_fwd("norm_mem", mem2, small["mem_norm"])
    bias = _bias_tile("bias_tile", small["rel_bias"])
    wts.update(weights(0, [hn1, memn, bias]))
    g1 = _ffn_gate("ffn1_gate", hn1, wts["ffn1_w_gate"], N_DEV)
    wts.update(weights(1, g1))
    u1, a1 = _ffn_up_act("ffn1_up", hn1, wts["ffn1_w_up"], g1)
    wts.update(weights(2, a1))
    h1 = _ffn_out("ffn1_down", a1, wts["ffn1_w_down"], res=x2, scale=FFN_RES)
    wts.update(weights(3, h1))
    z, hn2 = _mm_nn_cols("mix_in", h1, wts["w_in"], F32, norm=small["mix_norm"])
    z = z.reshape(B, S, -1)
    mixed = _attn_fwd("attn_fwd", z, bias)
    d_pool, mixed = _pool_fwd("pool_fwd", z, wts["w_pool"], small["pool_scale"], mixed)
    cat = mixed.reshape(T, -1)
    h2 = _mm_nn("mix_out", cat, wts["w_out"], F32, res=h1)
    wts.update(weights(4, h2))
    qc, hn3 = _mm_nn("cross_q", h2, wts["w_cq"], BF16, norm=small["cross_norm"])
    kv = _mm_nn("cross_kv", memn, wts["w_ckv"], BF16)
    o = _cross_fwd("cross_fwd", qc.reshape(B, S, -1), kv.reshape(B, n_mem, -1)).reshape(T, -1)
    h3 = _mm_nn_cols("cross_out", o, wts["w_co"], F32, res=h2)
    wts.update(weights(5, h3))
    hn4, g2, u2, a2 = _ffn_up("ffn2_up", h3, small["ffn2_norm"], wts["ffn2_w_gate"], wts["ffn2_w_up"], N_DEV)
    h4 = _ffn_out("ffn2_down", a2, wts["ffn2_w_down"], res=h3, scale=FFN_RES)

    gs = {}
    loss_part, dh4, gs["final_norm"] = _loss_and_grad("loss", h4, small["final_norm"], t2)

    def ffn_bwd(tag, dhb, h_in, hn, g, u, a, wg, wu, wd, gain, out_dtype=BF16):
        tok = emit({tag + "_w_down": _ffn_dw(tag + "_dwd", a, dhb, scale=FFN_RES)})
        dg, du = _ffn_dact(tag + "_dact", dhb, wd, g, u, after=tok)
        tok = emit({tag + "_w_gate": _ffn_dw(tag + "_dwg", dg, hn)})
        tok = emit({tag + "_w_up": _ffn_dw(tag + "_dwu", du, hn, after=tok)})
        dhn = _ffn_out(tag + "_dhn_g", dg, wg, after=tok)
        dhn = _ffn_out(tag + "_dhn_u", du, wu, res=dhn)
        return _rms_bwd(tag + "_dnorm", h_in, gain, dhn, skip=dhb, out_dtype=out_dtype)

    dh3b, gs["ffn2_norm"] = ffn_bwd("ffn2", dh4, h3, hn4, g2, u2, a2, wts["ffn2_w_gate"],
                                    wts["ffn2_w_up"], wts["ffn2_w_down"], small["ffn2_norm"])
    do = _mm_nt_cols("cross_do", dh3b, wts["w_co"], BF16, tn=D_CROSS)
    gw = {"w_co": _mm_tn("cross_dwo", o, dh3b, tm=D_CROSS, col_blocks=N_DEV)}
    dqc, dk, dv = _cross_bwd("cross_bwd", qc.reshape(B, S, -1), kv.reshape(B, n_mem, -1), do.reshape(B, S, -1))
    dqc = dqc.reshape(T, -1)
    dkv = jnp.concatenate([dk, dv], axis=-1).reshape(B * n_mem, -1)
    gw["w_cq"] = _mm_tn("cross_dwq", hn3, dqc, tn=D_CROSS)
    gw["w_ckv"] = _mm_tn("cross_dwkv", memn, dkv)
    tok = emit(gw)
    dhn3 = _mm_nt("cross_dhn", dqc, wts["w_cq"], F32, after=tok)
    dmemn = _mm_nt("cross_dmem", dkv, wts["w_ckv"], F32, tm=512)
    _, gs["mem_norm"] = _rms_bwd("mem_dnorm", mem2, small["mem_norm"], dmemn)
    dh2b, gs["cross_norm"] = _rms_bwd("cross_dnorm", h2, small["cross_norm"], dhn3, skip=dh3b)
    dcat = _mm_nt("mix_dcat", dh2b, wts["w_out"], BF16)
    gw = {"w_out": _mm_tn("mix_dwout", cat, dh2b)}
    dcat3 = dcat.reshape(B, S, -1)
    dq, dkk, dvv, dbias = _attn_bwd("attn_bwd", z, bias, dcat3)
    du, gw["w_pool"], gs["pool_scale"] = _pool_bwd("pool_bwd", d_pool, wts["w_pool"], small["pool_scale"], dcat3)
    gs["rel_bias"] = _rel_grad_from_diagonals(_bias_grad("bias_grad", dbias)[:, 0, :])
    dz = jnp.concatenate([dq, dkk, dvv, du], axis=-1).reshape(T, -1)
    gw["w_in"] = _mm_tn("mix_dwin", hn2, dz, col_blocks=N_DEV)
    tok = emit(gw)
    dhn2 = _mm_nt_cols("mix_dhn", dz, wts["w_in"], F32, after=tok)
    dh1b, gs["mix_norm"] = _rms_bwd("mix_dnorm", h1, small["mix_norm"], dhn2, skip=dh2b)
    dx, gs["ffn1_norm"] = ffn_bwd("ffn1", dh1b, x2, hn1, g1, u1, a1, wts["ffn1_w_gate"],
                                  wts["ffn1_w_up"], wts["ffn1_w_down"], small["ffn1_norm"], out_dtype=F32)
    return loss_part, dx.reshape(B, S, D), gs


def _position():
    return lax.axis_index("x"), lax.axis_index("y"), lax.axis_index("c")


def _index(p):
    return 4 * p[0] + 2 * p[1] + p[2]


HBM_SPEC = pl.BlockSpec(memory_space=pltpu.HBM)
SEM_SPEC = pl.BlockSpec(memory_space=pltpu.SEMAPHORE)
ANY_SPEC = pl.BlockSpec(memory_space=pl.ANY)
ORDERED_EFFECT = pltpu.SideEffectType.DATAFLOW_SIDE_EFFECTING


N_COPIES = {"grads": N_DEV - 1, "spread": 4, "relay": 3}
N_SEMS = {"grads": N_DEV, "spread": 5, "relay": 3}


def _copies(pattern, srcs, lands, send, recv, base=0):
    x, y, c = _position()
    me, sibling = _index((x, y, c)), (x, y, 1 - c)
    chips = [(1 - x, y), (x, 1 - y), (1 - x, 1 - y)]
    if pattern == "grads":
        targets = [(x ^ (k >> 2), y ^ ((k >> 1) & 1), c ^ (k & 1)) for k in range(1, N_DEV)]
    else:
        targets = [sibling] + [(*chip, c) for chip in chips]
    per, slots, out = N_COPIES[pattern], N_SEMS[pattern], []
    for a in range(len(lands)):
        for k in range(per):
            if pattern == "relay":
                src = dst = lands[a].at[_index((*chips[k], c))]
                to = sibling
            else:
                to = targets[k]
                src = srcs[a].at[_index(to)] if pattern == "grads" else srcs[a]
                dst = lands[a].at[me]
            slot = base + a * slots + k
            out.append(pltpu.make_async_remote_copy(src_ref=src, dst_ref=dst, send_sem=send.at[slot],
                                                    recv_sem=recv.at[slot], device_id=to, device_id_type=MESH))
    return out


def _own_copies(pattern, srcs, lands, send, base=0):
    if pattern == "relay":
        return []
    x, y, c = _position()
    me = _index((x, y, c))
    slots = N_SEMS[pattern]
    return [pltpu.make_async_copy(srcs[a].at[me] if pattern == "grads" else srcs[a], lands[a].at[me],
                                  send.at[base + a * slots + slots - 1]) for a in range(len(lands))]


def _comm(name, srcs, lands, wait=None, start=None, after=None):
    after = [] if after is None else list(after) if isinstance(after, (list, tuple)) else [after]
    ns, nl = len(srcs), len(lands)
    na = ns + nl
    arrays = list(srcs) + list(lands)
    n_wait = 2 if wait else 0
    n_start = 2 if start else 0

    def body(*refs):
        ins, lnd = refs[:ns], refs[ns:na]
        if wait:
            base = wait[3] if len(wait) > 3 else 0
            for cp in _copies(wait[0], ins, lnd, refs[na], refs[na + 1], base):
                cp.wait_send()
                cp.wait_recv()
            for cp in _own_copies(wait[0], ins, lnd, refs[na], base):
                cp.wait()
        if start:
            outs = refs[na + n_wait + len(after):]
            for cp in _copies(start, ins, lnd, outs[0], outs[1]) + _own_copies(start, ins, lnd, outs[0]):
                cp.start()
            refs[-1][...] = jnp.zeros((8, 128), F32)

    out_shape, out_specs = [], []
    if start:
        sems = pltpu.SemaphoreType.DMA((nl * N_SEMS[start],))
        out_shape += [sems, sems]
        out_specs += [SEM_SPEC, SEM_SPEC]
    out_shape += [pltpu.HBM(a.shape, a.dtype) for a in arrays]
    out_specs += [HBM_SPEC] * na
    if start:
        out_shape.append(jax.ShapeDtypeStruct((8, 128), F32))
        out_specs.append(pl.BlockSpec(memory_space=pltpu.VMEM))
    operands = [pltpu.with_memory_space_constraint(a, pltpu.HBM) for a in arrays]
    operands += list(wait[1:3]) if wait else []
    operands += after
    res = pl.pallas_call(
        body, name=name, out_shape=out_shape, out_specs=out_specs,
        in_specs=[HBM_SPEC] * na + [SEM_SPEC] * n_wait + [ANY_SPEC] * len(after),
        input_output_aliases={i: n_start + i for i in range(na)},
        compiler_params=pltpu.CompilerParams(has_side_effects=ORDERED_EFFECT),
    )(*operands)
    res = list(res)
    thru = res[n_start:n_start + na]
    return thru[:ns], thru[ns:], (tuple(res[:2]) if start else None), (res[-1] if start else None)


def _adamw_math(w, g, m, v):
    m = ADAM_B1 * m + (1.0 - ADAM_B1) * g
    v = ADAM_B2 * v + (1.0 - ADAM_B2) * (g * g)
    m_hat = m / (1.0 - ADAM_B1 ** ADAM_STEP)
    v_hat = v / (1.0 - ADAM_B2 ** ADAM_STEP)
    delta = -ADAM_LR * (m_hat / (jnp.sqrt(v_hat) + ADAM_EPS) + ADAM_WD * w)
    return delta, m, v


def _adamw(name, parts, w, m, v, tr=128):
    R, C = w.shape
    tr = _tile(R, tr)

    def body(p_ref, w_ref, m_ref, v_ref, g_out, d_out, m_out, v_out):
        g = p_ref[0].astype(F32)
        for d in range(1, N_DEV):
            g = g + p_ref[d].astype(F32)
        g_out[...] = g
        d_out[...], m_out[...], v_out[...] = _adamw_math(w_ref[...], g, m_ref[...], v_ref[...])

    row = pl.BlockSpec((tr, C), lambda i: (i, 0))
    out = jax.ShapeDtypeStruct((R, C), F32)
    return pl.pallas_call(
        body, name=name, grid=(R // tr,),
        in_specs=[pl.BlockSpec((N_DEV, tr, C), lambda i: (0, i, 0)), row, row, row],
        out_specs=[row] * 4, out_shape=[out] * 4,
        compiler_params=_params(dimension_semantics=("parallel",)),
    )(parts, w, m, v)


def _small_allreduce_adamw(name, g, w, m, v):
    R = g.shape[0]

    def body(g_ref, w_ref, m_ref, v_ref, g_out, d_out, m_out, v_out, land, send, recv):
        x, y, c = _position()
        me = _index((x, y, c))
        land[me] = g_ref[...]
        copies = []
        for k in range(1, N_DEV):
            peer = (x ^ (k >> 2), y ^ ((k >> 1) & 1), c ^ (k & 1))
            copies.append(pltpu.make_async_remote_copy(
                src_ref=g_ref, dst_ref=land.at[me], send_sem=send.at[k - 1], recv_sem=recv.at[k - 1],
                device_id=peer, device_id_type=MESH))
        for cp in copies:
            cp.start()
        for cp in copies:
            cp.wait()
        total = land[0]
        for d in range(1, N_DEV):
            total = total + land[d]
        g_out[...] = total
        d_out[...], m_out[...], v_out[...] = _adamw_math(w_ref[...], total, m_ref[...], v_ref[...])

    vm = pl.BlockSpec(memory_space=pltpu.VMEM)
    out = jax.ShapeDtypeStruct((R, 128), F32)
    return pl.pallas_call(
        body, name=name, in_specs=[vm] * 4, out_specs=[vm] * 4, out_shape=[out] * 4,
        scratch_shapes=[pltpu.VMEM((N_DEV, R, 128), F32), pltpu.SemaphoreType.DMA((7,)),
                        pltpu.SemaphoreType.DMA((7,))],
    )(g, w, m, v)


BIG = ("ffn1_w_gate", "ffn1_w_up", "ffn1_w_down", "w_in", "w_pool", "w_out", "w_cq", "w_ckv", "w_co",
       "ffn2_w_gate", "ffn2_w_up", "ffn2_w_down")
SMALL = ("ffn1_norm", "mix_norm", "rel_bias", "pool_scale", "cross_norm", "mem_norm", "ffn2_norm", "final_norm")
ORDER = ("ffn1_norm", "ffn1_w_gate", "ffn1_w_up", "ffn1_w_down", "mix_norm", "w_in", "rel_bias", "w_pool",
         "pool_scale", "w_out", "cross_norm", "mem_norm", "w_cq", "w_ckv", "w_co", "ffn2_norm", "ffn2_w_gate",
         "ffn2_w_up", "ffn2_w_down", "final_norm")
TRANSPOSED = ("ffn1_w_gate", "ffn1_w_up", "ffn2_w_gate", "ffn2_w_up")
ROW_SHARDED = TRANSPOSED + ("ffn1_w_down", "ffn2_w_down", "w_out", "w_cq", "w_ckv")
GATHER_GROUPS = (("ffn1_w_gate",), ("ffn1_w_up",), ("ffn1_w_down",), ("w_in", "w_pool", "w_out"),
                 ("w_cq", "w_ckv", "w_co"), ("ffn2_w_gate", "ffn2_w_up", "ffn2_w_down"))
RELAY_BEFORE_USE = ((0,), (1,), (2,), (3,), (4, 5), ())


def _pack(arrays):
    flat = jnp.concatenate([a.reshape(-1) for a in arrays])
    rows = -(-flat.shape[0] // 1024) * 8
    return jnp.pad(flat, (0, rows * 128 - flat.shape[0])).reshape(rows, 128)


def _unpack(packed, like):
    flat, out, at = packed.reshape(-1), [], 0
    for a in like:
        out.append(flat[at:at + a.size].reshape(a.shape))
        at += a.size
    return out


def _shard2d(a):
    a = a[0]
    return a.reshape(-1, a.shape[-1])


def kernel(x, mem, ffn1_norm, ffn1_w_gate, ffn1_w_up, ffn1_w_down, mix_norm, w_in, rel_bias, w_pool, pool_scale, w_out, cross_norm, mem_norm, w_cq, w_ckv, w_co, ffn2_norm, ffn2_w_gate, ffn2_w_up, ffn2_w_down, final_norm, loss_target, m_ffn1_norm, m_ffn1_w_gate, m_ffn1_w_up, m_ffn1_w_down, m_mix_norm, m_w_in, m_rel_bias, m_w_pool, m_pool_scale, m_w_out, m_cross_norm, m_mem_norm, m_w_cq, m_w_ckv, m_w_co, m_ffn2_norm, m_ffn2_w_gate, m_ffn2_w_up, m_ffn2_w_down, m_final_norm, v_ffn1_norm, v_ffn1_w_gate, v_ffn1_w_up, v_ffn1_w_down, v_mix_norm, v_w_in, v_rel_bias, v_w_pool, v_pool_scale, v_w_out, v_cross_norm, v_mem_norm, v_w_cq, v_w_ckv, v_w_co, v_ffn2_norm, v_ffn2_w_gate, v_ffn2_w_up, v_ffn2_w_down, v_final_norm):
    args = dict(locals())
    def view(n, a):
        return a.transpose(0, 2, 1) if n in TRANSPOSED else a

    w_in_ = {n: view(n, args[n]) for n in ORDER}
    m_in = {n: view(n, args["m_" + n]) for n in ORDER}
    v_in = {n: view(n, args["v_" + n]) for n in ORDER}

    n_g, rows = len(POOL_WINDOWS), POOL_GROUP // N_DEV

    def landing(block_shape, dtype):
        return lax.empty((N_DEV,) + tuple(block_shape), dtype)

    gathers, tok = [], None
    for first in (True, False):
        groups = GATHER_GROUPS[:1] if first else GATHER_GROUPS[1:]
        shards = [_shard2d(w_in_[n]) for group in groups for n in group]
        shards = [(s if tok is None else s + tok[0, 0]).astype(BF16) for s in shards]
        srcs, lands, sems, tok = _comm("gather_start_%d" % (not first), shards,
                                       [landing(s.shape, BF16) for s in shards], start="spread", after=tok)
        at = 0
        for group in groups:
            gathers.append((srcs[at:at + len(group)], lands[at:at + len(group)], sems + (at * N_SEMS["spread"],)))
            at += len(group)

    def weights(gi, after):
        for ri in RELAY_BEFORE_USE[gi]:
            srcs, lands, sems = gathers[ri]
            _, lands, sems, after = _comm("gather_relay_%d" % ri, srcs, lands, wait=("spread",) + sems,
                                          start="relay", after=after)
            gathers[ri] = (None, lands, sems)
        _, lands, sems = gathers[gi]
        _, lands, _, _ = _comm("gather_finish_%d" % gi, [], lands, wait=("relay",) + sems, after=after)
        out = {}
        for n, full in zip(GATHER_GROUPS[gi], lands):
            if n == "w_pool":
                full = full.reshape(N_DEV, n_g, rows, POOL_GROUP).transpose(1, 0, 2, 3).reshape(n_g, POOL_GROUP, POOL_GROUP)
            out[n] = full.reshape(-1, full.shape[-1]) if n in ROW_SHARDED else full
        return out

    scatters = []

    def emit(gw):
        names = list(gw)
        stacks = []
        for n in names:
            g = gw[n]
            if n == "w_pool":
                g = g.reshape(n_g, N_DEV, rows, POOL_GROUP).transpose(1, 0, 2, 3).astype(BF16)
            stacks.append(g.reshape((N_DEV,) + _shard2d(w_in_[n]).shape))
        lands = [landing(s.shape[1:], s.dtype) for s in stacks]
        srcs, lands, sems, token = _comm("grads_start_%d" % len(scatters), stacks, lands, start="grads")
        scatters.append((names, srcs, lands, sems))
        return token

    small = {n: w_in_[n].reshape(1, -1) for n in SMALL if n != "rel_bias"}
    small["rel_bias"] = rel_bias[0]
    loss_part, grad_x, gs = _local_step(x, mem, loss_target, small, weights, emit, start_token=tok)

    grad, delta, new_m, new_v = {}, {}, {}, {}
    after = grad_x
    for si, (names, srcs, lands, sems) in enumerate(scatters):
        _, landed, _, _ = _comm("grads_finish_%d" % si, srcs, lands, wait=("grads",) + sems, after=after)
        for n, parts in zip(names, landed):
            res = _adamw("adamw_" + n, parts, _shard2d(w_in_[n]), _shard2d(m_in[n]), _shard2d(v_in[n]))
            grad[n], delta[n], new_m[n], new_v[n] = [view(n, r.reshape(w_in_[n].shape)) for r in res]
        after = res[0]

    slot = jnp.zeros((1,), F32)
    like = [w_in_[n] for n in SMALL] + [slot]
    gs["rel_bias"] = gs["rel_bias"].reshape(rel_bias.shape)
    res = _small_allreduce_adamw("small_params", _pack([gs[n] for n in SMALL] + [loss_part[0, :1]]), _pack(like),
                                 _pack([m_in[n] for n in SMALL] + [slot]), _pack([v_in[n] for n in SMALL] + [slot]))
    for d, packed in zip((grad, delta, new_m, new_v), res):
        for n, a in zip(SMALL, _unpack(packed, like)):
            d[n] = a
    loss = _unpack(res[0], like)[-1][0]
    return (loss, grad_x, *[grad[n] for n in ORDER], *[delta[n] for n in ORDER],
            *[new_m[n] for n in ORDER], *[new_v[n] for n in ORDER])
```

```python
import jax
import jax.numpy as jnp
from jax import lax
from jax.experimental import pallas as pl
from jax.experimental.pallas import tpu as pltpu

F32 = jnp.float32
BF16 = jnp.bfloat16

N_DEV = 8
EPS = 1e-6
NEG_INF = -1e30
CHUNK = 64
LEFT_CHUNKS = 8
PAD = LEFT_CHUNKS * CHUNK
QBLK = 4 * CHUNK
KBAND = PAD + QBLK
REL_CLIP = 128
ATTN_HEADS = 16
HEAD_DIM = 64
D_ATTN = ATTN_HEADS * HEAD_DIM
POOL_WINDOWS = (2, 4, 8, 16)
POOL_GROUP = 256
D_POOL = len(POOL_WINDOWS) * POOL_GROUP
CROSS_HEADS = 4
CROSS_DIM = 128
D_CROSS = CROSS_HEADS * CROSS_DIM
FFN_RES = 0.5
ADAM_LR, ADAM_B1, ADAM_B2, ADAM_EPS, ADAM_WD, ADAM_STEP = 0.001, 0.9, 0.999, 1e-08, 0.01, 10

NN = (((1,), (0,)), ((), ()))
NT = (((1,), (1,)), ((), ()))
TN = (((0,), (0,)), ((), ()))
MESH = pl.DeviceIdType.MESH
VMEM_LIMIT = 56 * 1024 * 1024


def _params(**kw):
    return pltpu.CompilerParams(vmem_limit_bytes=VMEM_LIMIT, **kw)


def _bf(v):
    return v if v.dtype == BF16 else v.astype(BF16)


WHOLE = ((Ellipsis,), (Ellipsis,))


def _rms(xv, gain):
    return (xv * lax.rsqrt(jnp.mean(xv * xv, axis=-1, keepdims=True) + EPS)) * gain


def _gemm(name, a, a_spec, b, b_spec, dims, grid, outs, chunks=(WHOLE,), extras=(), epilogue=None, after=None,
          norm=None):
    nex, nout = len(extras), len(outs)
    first_out = 2 + nex + (after is not None) + (norm is not None)

    def body(*refs):
        a_ref, b_ref = refs[:2]
        if norm is not None:
            hn_out, a_ref = refs[first_out + nout], refs[-1]

            @pl.when(pl.program_id(1) == 0)
            def _():
                hn = _rms(refs[0][...], refs[first_out - 1][...]).astype(BF16)
                a_ref[...] = hn
                hn_out[...] = hn

        total = None
        for ia, ib in chunks:
            d = lax.dot_general(_bf(a_ref[ia]), _bf(b_ref[ib]), dims, preferred_element_type=F32)
            total = d if total is None else total + d
        vals = epilogue(total, *[e[...] for e in refs[2:2 + nex]]) if epilogue is not None else (total,)
        for r, v in zip(refs[first_out:first_out + nout], vals):
            r[...] = v.astype(r.dtype)

    operands = [a, b] + [x for x, _, _ in extras]
    in_specs = [pl.BlockSpec(*a_spec), pl.BlockSpec(*b_spec)] + [pl.BlockSpec(blk, m) for _, blk, m in extras]
    if after is not None:
        operands.append(after)
        in_specs.append(pl.BlockSpec(after.shape, lambda i, j: (0, 0)))
    out_specs = [pl.BlockSpec(blk, m) for _, _, blk, m in outs]
    out_shape = [jax.ShapeDtypeStruct(s, d) for s, d, _, _ in outs]
    scratch = []
    if norm is not None:
        operands.append(norm)
        in_specs.append(pl.BlockSpec(norm.shape, lambda i, j: (0, 0)))
        out_specs.append(pl.BlockSpec(*a_spec))
        out_shape.append(jax.ShapeDtypeStruct(a.shape, BF16))
        scratch.append(pltpu.VMEM(a_spec[0], BF16))
    res = pl.pallas_call(
        body, name=name, grid=grid, in_specs=in_specs, out_specs=out_specs, out_shape=out_shape,
        scratch_shapes=scratch,
        compiler_params=_params(dimension_semantics=("parallel", "arbitrary" if norm is not None else "parallel")),
    )(*operands)
    return res[0] if len(res) == 1 else res


def _tile(n, want):
    for t in range(min(n, want), 15, -1):
        if n % t == 0 and t % 16 == 0:
            return t
    return n


def _mm_nn(name, a, b, out_dtype, res=None, tm=1024, tn=1024, norm=None):
    M, K = a.shape
    N = b.shape[1]
    tm, tn = _tile(M, tm), _tile(N, tn)
    extras = [] if res is None else [(res, (tm, tn), lambda i, j: (i, j))]
    epi = None if res is None else (lambda t, r: (r + t,))
    return _gemm(name, a, ((tm, K), lambda i, j: (i, 0)), b, ((K, tn), lambda i, j: (0, j)), NN,
                 (M // tm, N // tn), [((M, N), out_dtype, (tm, tn), lambda i, j: (i, j))], extras=extras, epilogue=epi,
                 norm=norm)


def _mm_nn_cols(name, a, bs, out_dtype, res=None, tm=1024, norm=None):
    M, K = a.shape
    nb, _, w = bs.shape
    tm = _tile(M, tm)
    extras = [] if res is None else [(res, (tm, w), lambda i, j: (i, j))]
    epi = None if res is None else (lambda t, r: (r + t,))
    return _gemm(name, a, ((tm, K), lambda i, j: (i, 0)), bs, ((None, K, w), lambda i, j: (j, 0, 0)), NN,
                 (M // tm, nb), [((M, nb * w), out_dtype, (tm, w), lambda i, j: (i, j))], extras=extras, epilogue=epi,
                 norm=norm)


def _mm_nt(name, a, b, out_dtype, tm=1024, tn=1024, after=None):
    M, K = a.shape
    N = b.shape[0]
    tm, tn = _tile(M, tm), _tile(N, tn)
    return _gemm(name, a, ((tm, K), lambda i, j: (i, 0)), b, ((tn, K), lambda i, j: (j, 0)), NT,
                 (M // tm, N // tn), [((M, N), out_dtype, (tm, tn), lambda i, j: (i, j))], after=after)


def _mm_nt_cols(name, a, bs, out_dtype, tm=1024, tn=512, after=None):
    M = a.shape[0]
    nb, N, w = bs.shape
    tm, tn = _tile(M, tm), _tile(N, tn)
    chunks = [((slice(None), pl.ds(c * w, w)), (c,)) for c in range(nb)]
    return _gemm(name, a, ((tm, nb * w), lambda i, j: (i, 0)), bs, ((nb, tn, w), lambda i, j: (0, j, 0)), NT,
                 (M // tm, N // tn), [((M, N), out_dtype, (tm, tn), lambda i, j: (i, j))], chunks, after=after)


def _mm_tn(name, a, b, tm=1024, tn=1024, col_blocks=None):
    T, Ka = a.shape
    Nb = b.shape[1]
    tm = _tile(Ka, tm)
    if col_blocks is None:
        tn = _tile(Nb, tn)
        out = ((Ka, Nb), BF16, (tm, tn), lambda i, j: (i, j))
    else:
        tn = Nb // col_blocks
        out = ((col_blocks, Ka, tn), BF16, (None, tm, tn), lambda i, j: (j, i, 0))
    return _gemm(name, a, ((T, tm), lambda i, j: (0, i)), b, ((T, tn), lambda i, j: (0, j)), TN,
                 (Ka // tm, Nb // tn), [out])


def _ffn_out(name, a, wt, res=None, scale=1.0, tm=1024, tn=512, after=None, out_dtype=F32):
    nb, M, w = a.shape
    N = wt.shape[1]
    tm, tn = _tile(M, tm), _tile(N, tn)
    chunks = [((c,), (pl.ds(c * w, w),)) for c in range(nb)]
    extras = [] if res is None else [(res, (tm, tn), lambda i, j: (i, j))]
    epi = None if res is None else (lambda t, r: (r + scale * t,))
    return _gemm(name, a, ((nb, tm, w), lambda i, j: (0, i, 0)), wt, ((nb * w, tn), lambda i, j: (0, j)), NN,
                 (M // tm, N // tn), [((M, N), out_dtype, (tm, tn), lambda i, j: (i, j))], chunks, extras, epi, after)


def _ffn_dact(name, dhb, wd, g, u, tm=1024, after=None):
    M, K = dhb.shape
    nb, _, w = g.shape
    tm = _tile(M, tm)
    tr = _tile(tm, 256)
    tokens = [] if after is None else [after]

    def body(dh_ref, wd_ref, g_ref, u_ref, *rest):
        dg_ref, du_ref = rest[-2:]
        pieces = [pl.ds(r * tr, tr) for r in range(tm // tr)]

        def product(rows):
            return lax.dot_general(dh_ref[rows, :], wd_ref[...], NT, preferred_element_type=F32)

        results = []
        dact = product(pieces[0])
        for n, rows in enumerate(pieces):
            ahead = product(pieces[n + 1]) if n + 1 < len(pieces) else None
            results.append(_swiglu_bwd(dact, g_ref[rows, :], u_ref[rows, :]))
            dact = ahead
        for rows, (dg, du) in zip(pieces, results):
            dg_ref[rows, :] = dg.astype(BF16)
            du_ref[rows, :] = du.astype(BF16)

    hid = pl.BlockSpec((None, tm, w), lambda i, j: (j, i, 0))
    return pl.pallas_call(
        body, name=name, grid=(M // tm, nb),
        in_specs=[pl.BlockSpec((tm, K), lambda i, j: (i, 0)), pl.BlockSpec((w, K), lambda i, j: (j, 0)), hid, hid]
        + [pl.BlockSpec(t.shape, lambda i, j: (0, 0)) for t in tokens],
        out_specs=[hid, hid], out_shape=[jax.ShapeDtypeStruct((nb, M, w), BF16)] * 2,
        compiler_params=_params(dimension_semantics=("parallel", "parallel")),
    )(dhb, wd, g, u, *tokens)


def _ffn_dw(name, a, b, scale=1.0, tn=1024, after=None):
    nb, T, w = a.shape
    N = b.shape[1]
    tn = _tile(N, tn)
    epi = None if scale == 1.0 else (lambda t: (t * scale,))
    return _gemm(name, a, ((None, T, w), lambda i, j: (i, 0, 0)), b, ((T, tn), lambda i, j: (0, j)), TN,
                 (nb, N // tn), [((nb * w, N), BF16, (w, tn), lambda i, j: (i, j))], epilogue=epi, after=after)


def _ffn_gate(name, hn, wgt, nb, tm=1024):
    M, K = hn.shape
    w = wgt.shape[0] // nb
    tm = _tile(M, tm)
    return _gemm(name, hn, ((tm, K), lambda i, j: (i, 0)), wgt, ((w, K), lambda i, j: (j, 0)), NT,
                 (M // tm, nb), [((nb, M, w), BF16, (None, tm, w), lambda i, j: (j, i, 0))])


def _ffn_up_act(name, hn, wut, g, tm=1024):
    M, K = hn.shape
    nb, _, w = g.shape
    tm = _tile(M, tm)
    hid = ((None, tm, w), lambda i, j: (j, i, 0))

    def epilogue(u, gate):
        gate = gate.astype(F32)
        return u, gate * jax.nn.sigmoid(gate) * u

    return _gemm(name, hn, ((tm, K), lambda i, j: (i, 0)), wut, ((w, K), lambda i, j: (j, 0)), NT,
                 (M // tm, nb), [((nb, M, w), BF16) + hid] * 2, extras=[(g,) + hid], epilogue=epilogue)


def _ffn_up(name, h, gain, wgt, wut, nb, tm=1024):
    M, K = h.shape
    w = wgt.shape[0] // nb
    tm = _tile(M, tm)

    def body(h_ref, gain_ref, g_ref, u_ref, hn_ref, og, ou, oa, a_ref):
        @pl.when(pl.program_id(1) == 0)
        def _():
            hn = _rms(h_ref[...], gain_ref[...]).astype(BF16)
            a_ref[...] = hn
            hn_ref[...] = hn

        a = a_ref[...]
        g = lax.dot_general(a, g_ref[...], NT, preferred_element_type=F32)
        u = lax.dot_general(a, u_ref[...], NT, preferred_element_type=F32)
        og[...] = g.astype(BF16)
        ou[...] = u.astype(BF16)
        oa[...] = (g * jax.nn.sigmoid(g) * u).astype(BF16)

    rows = pl.BlockSpec((tm, K), lambda i, j: (i, 0))
    wspec = pl.BlockSpec((w, K), lambda i, j: (j, 0))
    ospec = pl.BlockSpec((None, tm, w), lambda i, j: (j, i, 0))
    return pl.pallas_call(
        body, name=name, grid=(M // tm, nb),
        in_specs=[rows, pl.BlockSpec((1, K), lambda i, j: (0, 0)), wspec, wspec],
        out_specs=[rows] + [ospec] * 3,
        out_shape=[jax.ShapeDtypeStruct((M, K), BF16)] + [jax.ShapeDtypeStruct((nb, M, w), BF16)] * 3,
        scratch_shapes=[pltpu.VMEM((tm, K), BF16)],
        compiler_params=_params(dimension_semantics=("parallel", "arbitrary")),
    )(h, gain, wgt, wut)


def _swiglu_bwd(dact, g, u):
    g = g.astype(F32)
    u = u.astype(F32)
    sig = jax.nn.sigmoid(g)
    silu = g * sig
    d = FFN_RES * dact
    return d * u * (sig * (1.0 + g * (1.0 - sig))), d * silu


def _rms_fwd(name, x, gain, tr=512, after=None):
    R, D = x.shape
    tr = _tile(R, tr)

    def body(x_ref, g_ref, *rest):
        xv = x_ref[...]
        y = xv * lax.rsqrt(jnp.mean(xv * xv, axis=-1, keepdims=True) + EPS)
        rest[-1][...] = (y * g_ref[...]).astype(BF16)

    tokens = [] if after is None else [after]
    return pl.pallas_call(
        body, name=name, grid=(R // tr,),
        in_specs=[pl.BlockSpec((tr, D), lambda i: (i, 0)), pl.BlockSpec((1, D), lambda i: (0, 0))]
        + [pl.BlockSpec(t.shape, lambda i: (0, 0)) for t in tokens],
        out_specs=pl.BlockSpec((tr, D), lambda i: (i, 0)), out_shape=jax.ShapeDtypeStruct((R, D), BF16),
        compiler_params=_params(dimension_semantics=("parallel",)),
    )(x, gain, *tokens)


def _rms_bwd_math(xv, gain, dy):
    rstd = lax.rsqrt(jnp.mean(xv * xv, axis=-1, keepdims=True) + EPS)
    xhat = xv * rstd
    dxh = dy * gain
    dx = rstd * (dxh - xhat * jnp.mean(dxh * xhat, axis=-1, keepdims=True))
    return dx, jnp.sum(dy * xhat, axis=0, keepdims=True)


def _rms_bwd(name, x, gain, dy, skip=None, out_dtype=BF16, tr=512):
    R, D = x.shape
    tr = _tile(R, tr)
    has_skip = skip is not None

    def body(*refs):
        x_ref, g_ref, dy_ref = refs[:3]
        dx_ref, dg_ref = refs[-2:]
        dx, dg = _rms_bwd_math(x_ref[...], g_ref[...], dy_ref[...].astype(F32))
        if has_skip:
            dx = dx + refs[3][...].astype(F32)
        dx_ref[...] = dx.astype(out_dtype)

        @pl.when(pl.program_id(0) == 0)
        def _():
            dg_ref[...] = dg

        @pl.when(pl.program_id(0) > 0)
        def _():
            dg_ref[...] += dg

    row = pl.BlockSpec((tr, D), lambda i: (i, 0))
    vec = pl.BlockSpec((1, D), lambda i: (0, 0))
    return pl.pallas_call(
        body, name=name, grid=(R // tr,),
        in_specs=[row, vec, row] + ([row] if has_skip else []),
        out_specs=[row, vec],
        out_shape=[jax.ShapeDtypeStruct((R, D), out_dtype), jax.ShapeDtypeStruct((1, D), F32)],
        compiler_params=_params(dimension_semantics=("arbitrary",)),
    )(*([x, gain, dy] + ([skip] if has_skip else [])))


def _loss_and_grad(name, h, gain, target, tr=512):
    R, D = h.shape
    tr = _tile(R, tr)

    def body(h_ref, g_ref, t_ref, loss_ref, dhb_ref, dg_ref):
        hv, gain_v = h_ref[...], g_ref[...]
        y = (hv * lax.rsqrt(jnp.mean(hv * hv, axis=-1, keepdims=True) + EPS)) * gain_v
        err = y - t_ref[...]
        part = jnp.full((8, 128), 0.5 * jnp.sum(jnp.mean(err * err, axis=-1, keepdims=True)), F32)
        dh, dg = _rms_bwd_math(hv, gain_v, err * (1.0 / D))
        dhb_ref[...] = dh.astype(BF16)

        @pl.when(pl.program_id(0) == 0)
        def _():
            dg_ref[...] = dg
            loss_ref[...] = part

        @pl.when(pl.program_id(0) > 0)
        def _():
            dg_ref[...] += dg
            loss_ref[...] += part

    row = pl.BlockSpec((tr, D), lambda i: (i, 0))
    vec = pl.BlockSpec((1, D), lambda i: (0, 0))
    return pl.pallas_call(
        body, name=name, grid=(R // tr,), in_specs=[row, vec, row],
        out_specs=[pl.BlockSpec((8, 128), lambda i: (0, 0)), row, vec],
        out_shape=[jax.ShapeDtypeStruct((8, 128), F32), jax.ShapeDtypeStruct((R, D), BF16),
                   jax.ShapeDtypeStruct((1, D), F32)],
        compiler_params=_params(dimension_semantics=("arbitrary",)),
    )(h, gain, target)


def _bias_tile(name, rel):
    width = KBAND + QBLK
    sat = rel[:, 2 * REL_CLIP:]
    n_left = PAD - REL_CLIP + 1
    row0 = jnp.concatenate([jnp.broadcast_to(sat, (ATTN_HEADS, n_left)), rel[:, :2 * REL_CLIP][:, ::-1],
                            jnp.broadcast_to(sat, (ATTN_HEADS, width - n_left - 2 * REL_CLIP))], axis=1)

    def body(e_ref, o_ref):
        rows = pltpu.roll(jnp.broadcast_to(e_ref[...], (QBLK, width)), 0, 1, stride=1, stride_axis=0)
        i = lax.broadcasted_iota(jnp.int32, (QBLK, KBAND), 0) // CHUNK
        j = lax.broadcasted_iota(jnp.int32, (QBLK, KBAND), 1) // CHUNK
        o_ref[...] = jnp.where((j >= i) & (j <= i + LEFT_CHUNKS), rows[:, :KBAND], NEG_INF)

    return pl.pallas_call(
        body, name=name, grid=(ATTN_HEADS,),
        in_specs=[pl.BlockSpec((None, 1, width), lambda h: (h, 0, 0))],
        out_specs=pl.BlockSpec((None, QBLK, KBAND), lambda h: (h, 0, 0)),
        out_shape=jax.ShapeDtypeStruct((ATTN_HEADS, QBLK, KBAND), F32),
        compiler_params=_params(dimension_semantics=("parallel",)),
    )(row0.reshape(ATTN_HEADS, 1, width))


ATTN_SCALE = HEAD_DIM ** -0.5
ROW_PIECES = 1


def _stack_heads(x, first):
    zero = jnp.zeros_like(x)
    return jnp.concatenate([jnp.where(first, x, zero), jnp.where(first, zero, x)], axis=0)


def _band_softmax(q_half_scaled, kb, bias, left_mask):
    s = lax.dot_general(q_half_scaled, kb, NT, preferred_element_type=F32) + bias + left_mask
    e = jnp.exp(s - jnp.max(s, axis=-1, keepdims=True))
    return e * (1.0 / jnp.sum(e, axis=-1, keepdims=True))


def _left_mask(qb):
    kpos = qb * QBLK - PAD + lax.broadcasted_iota(jnp.int32, (1, KBAND), 1)
    return jnp.where(kpos >= 0, 0.0, NEG_INF).astype(F32)


def _fill_padded(dst, src, S):
    dst[pl.ds(0, PAD), :] = jnp.zeros((PAD, dst.shape[1]), dst.dtype)
    dst[pl.ds(PAD, S), :] = src[...].astype(dst.dtype)


def _attn_fwd(name, z, bias):
    B, S, _ = z.shape
    nh2 = ATTN_HEADS // 2

    def body(q_ref, k_ref, v_ref, b_ref, o_ref, kp, vp):
        qb = pl.program_id(2)

        @pl.when(qb == 0)
        def _():
            _fill_padded(kp, k_ref, S)
            _fill_padded(vp, v_ref, S)

        start = pl.multiple_of(qb * QBLK, QBLK)
        kb, vb = kp[pl.ds(start, KBAND), :], vp[pl.ds(start, KBAND), :]
        q = (q_ref[...] * ATTN_SCALE).astype(BF16)
        first = lax.broadcasted_iota(jnp.int32, (QBLK, 2 * HEAD_DIM), 1) < HEAD_DIM
        left = _left_mask(qb)
        zero = jnp.zeros_like(q)
        qh = [jnp.where(first, q, zero), jnp.where(first, zero, q)]
        rp = QBLK // ROW_PIECES
        chains = [(a, r) for r in range(ROW_PIECES) for a in range(2)]
        ss = [lax.dot_general(qh[a][r * rp:(r + 1) * rp], kb, NT, preferred_element_type=F32) for a, r in chains]
        ps = []
        for (a, r), s in zip(chains, ss):
            s = s + b_ref[a, pl.ds(r * rp, rp), :] + left
            e = jnp.exp(s - jnp.max(s, axis=-1, keepdims=True))
            ps.append((e * (1.0 / jnp.sum(e, axis=-1, keepdims=True))).astype(BF16))
        os_ = [jnp.dot(p, vb, preferred_element_type=F32) for p in ps]
        for r in range(ROW_PIECES):
            o_ref[pl.ds(r * rp, rp), :] = jnp.where(first[:rp], os_[2 * r], os_[2 * r + 1]).astype(BF16)

    return pl.pallas_call(
        body, name=name, grid=(B, nh2, S // QBLK),
        in_specs=[pl.BlockSpec((None, QBLK, 128), lambda b, h, i: (b, i, h)),
                  pl.BlockSpec((None, S, 128), lambda b, h, i: (b, 0, nh2 + h)),
                  pl.BlockSpec((None, S, 128), lambda b, h, i: (b, 0, 2 * nh2 + h)),
                  pl.BlockSpec((2, QBLK, KBAND), lambda b, h, i: (h, 0, 0))],
        out_specs=pl.BlockSpec((None, QBLK, 128), lambda b, h, i: (b, i, h)),
        out_shape=jax.ShapeDtypeStruct((B, S, D_ATTN + D_POOL), BF16),
        scratch_shapes=[pltpu.VMEM((PAD + S, 128), BF16), pltpu.VMEM((PAD + S, 128), BF16)],
        compiler_params=_params(dimension_semantics=("parallel", "parallel", "arbitrary")),
    )(z, z, z, bias)


def _attn_bwd(name, z, bias, dcat):
    B, S, _ = z.shape
    nh2 = ATTN_HEADS // 2
    nqb = S // QBLK

    def body(q_ref, k_ref, v_ref, b_ref, do_ref, dq_ref, dk_ref, dv_ref, db_ref, kp, vp, dka, dva):
        b, qb = pl.program_id(1), pl.program_id(2)

        @pl.when(qb == 0)
        def _():
            _fill_padded(kp, k_ref, S)
            _fill_padded(vp, v_ref, S)
            dka[...] = jnp.zeros_like(dka)
            dva[...] = jnp.zeros_like(dva)

        @pl.when((qb == 0) & (b == 0))
        def _():
            db_ref[...] = jnp.zeros_like(db_ref)

        start = pl.multiple_of(qb * QBLK, QBLK)
        band = pl.ds(start, KBAND)
        kb, vb = kp[band, :], vp[band, :]
        q = (q_ref[...] * ATTN_SCALE).astype(BF16)
        do = do_ref[...]
        first = lax.broadcasted_iota(jnp.int32, (QBLK, 2 * HEAD_DIM), 1) < HEAD_DIM
        left = _left_mask(qb)
        q2, do2 = _stack_heads(q, first), _stack_heads(do, first)
        p = _band_softmax(q2, kb, b_ref[...].reshape(2 * QBLK, KBAND), left)
        dp = lax.dot_general(do2, vb, NT, preferred_element_type=F32)
        ds = p * (dp - jnp.sum(p * dp, axis=-1, keepdims=True))
        db_ref[...] += ds.reshape(2, QBLK, KBAND)
        dsb = ds.astype(BF16)
        dq = jnp.dot(dsb, kb, preferred_element_type=F32)
        dq_ref[...] = (jnp.where(first, dq[:QBLK], dq[QBLK:]) * ATTN_SCALE).astype(BF16)
        dka[:, band] += lax.dot_general(q2, dsb, TN, preferred_element_type=F32)
        dva[:, band] += lax.dot_general(do2, p.astype(BF16), TN, preferred_element_type=F32)

        @pl.when(qb == nqb - 1)
        def _():
            dk_ref[...] = dka[:, pl.ds(PAD, S)].T.astype(BF16)
            dv_ref[...] = dva[:, pl.ds(PAD, S)].T.astype(BF16)

    qspec = pl.BlockSpec((None, QBLK, 128), lambda h, b, i: (b, i, h))
    kvout = pl.BlockSpec((None, S, 128), lambda h, b, i: (b, 0, h))
    bspec = pl.BlockSpec((2, QBLK, KBAND), lambda h, b, i: (h, 0, 0))
    act = jax.ShapeDtypeStruct((B, S, D_ATTN), BF16)
    return pl.pallas_call(
        body, name=name, grid=(nh2, B, nqb),
        in_specs=[qspec,
                  pl.BlockSpec((None, S, 128), lambda h, b, i: (b, 0, nh2 + h)),
                  pl.BlockSpec((None, S, 128), lambda h, b, i: (b, 0, 2 * nh2 + h)),
                  bspec, qspec],
        out_specs=[qspec, kvout, kvout, bspec],
        out_shape=[act, act, act, jax.ShapeDtypeStruct((ATTN_HEADS, QBLK, KBAND), F32)],
        scratch_shapes=[pltpu.VMEM((PAD + S, 128), BF16), pltpu.VMEM((PAD + S, 128), BF16),
                        pltpu.VMEM((128, PAD + S), F32), pltpu.VMEM((128, PAD + S), F32)],
        compiler_params=_params(dimension_semantics=("arbitrary", "arbitrary", "arbitrary")),
    )(z, z, z, bias, dcat)


def _bias_grad(name, dbias):
    width = KBAND + QBLK

    def body(d_ref, o_ref):
        acc = jnp.zeros((1, width), F32)
        for i in range(QBLK):
            row = jnp.concatenate([d_ref[pl.ds(i, 1), :], jnp.zeros((1, QBLK), F32)], axis=1)
            shift = QBLK - 1 - i
            acc = acc + (pltpu.roll(row, shift, 1) if shift else row)
        o_ref[...] = acc

    return pl.pallas_call(
        body, name=name, grid=(ATTN_HEADS,),
        in_specs=[pl.BlockSpec((None, QBLK, KBAND), lambda h: (h, 0, 0))],
        out_specs=pl.BlockSpec((None, 1, width), lambda h: (h, 0, 0)),
        out_shape=jax.ShapeDtypeStruct((ATTN_HEADS, 1, width), F32),
        compiler_params=_params(dimension_semantics=("parallel",)),
    )(dbias)


def _rel_grad_from_diagonals(diag):
    top = PAD + QBLK - 1 - REL_CLIP
    sat = jnp.sum(diag[:, :top + 1], axis=1, keepdims=True)
    mid = diag[:, top + 1:top + 2 * REL_CLIP][:, ::-1]
    return jnp.concatenate([jnp.zeros_like(sat), mid, sat], axis=1)


def _shift_rows(x, k, forward):
    S = x.shape[0]
    t = lax.broadcasted_iota(jnp.int32, x.shape, 0)
    if forward:
        return jnp.where(t < S - k, pltpu.roll(x, S - k, 0), 0.0)
    return jnp.where(t >= k, pltpu.roll(x, k, 0), 0.0)


def _window_sum(x, g, forward):
    s = x + _shift_rows(x, 1, forward)
    out = s
    for n, k in enumerate((2, 4, 8)):
        s = s + _shift_rows(s, k, forward)
        out = jnp.where(g > n, s, out)
    return out


def _pool_count(S, g):
    t = lax.broadcasted_iota(jnp.int32, (S, 1), 0)
    w = jnp.left_shift(2, g)
    return jnp.minimum(t + 1, w).astype(F32)


def _pool_fwd(name, z, wp, pscale, mixed):
    B, S, _ = z.shape
    c0 = 3 * D_ATTN // POOL_GROUP
    y0 = D_ATTN // POOL_GROUP

    def body(u_ref, w_ref, s_ref, mixed_ref, d_ref, y_ref):
        g = pl.program_id(1)
        u = u_ref[...]
        d = (_window_sum(u, g, False) / _pool_count(S, g) - u).astype(BF16)
        d_ref[...] = d
        y_ref[...] = (jnp.dot(d, w_ref[...], preferred_element_type=F32) * s_ref[...]).astype(BF16)

    return pl.pallas_call(
        body, name=name, grid=(B, len(POOL_WINDOWS)),
        in_specs=[pl.BlockSpec((None, S, POOL_GROUP), lambda b, g: (b, 0, c0 + g)),
                  pl.BlockSpec((None, POOL_GROUP, POOL_GROUP), lambda b, g: (g, 0, 0)),
                  pl.BlockSpec((1, POOL_GROUP), lambda b, g: (0, g)),
                  pl.BlockSpec(memory_space=pl.ANY)],
        out_specs=[pl.BlockSpec((None, S, POOL_GROUP), lambda b, g: (b, 0, g)),
                   pl.BlockSpec((None, S, POOL_GROUP), lambda b, g: (b, 0, y0 + g))],
        out_shape=[jax.ShapeDtypeStruct((B, S, D_POOL), BF16), jax.ShapeDtypeStruct(mixed.shape, BF16)],
        input_output_aliases={3: 1},
        compiler_params=_params(dimension_semantics=("parallel", "parallel")),
    )(z, wp, pscale, mixed)


def _pool_bwd(name, d, wp, pscale, dcat):
    B, S, _ = d.shape
    c0 = D_ATTN // POOL_GROUP

    def body(d_ref, w_ref, s_ref, dy_ref, du_ref, dw_ref, dsc_ref):
        g, b = pl.program_id(0), pl.program_id(1)
        dv = d_ref[...]
        dy = dy_ref[...].astype(F32)
        w = w_ref[...]
        ypre = jnp.dot(dv, w, preferred_element_type=F32)
        dyp = (dy * s_ref[...]).astype(BF16)
        dd = lax.dot_general(dyp, w, NT, preferred_element_type=F32)
        du_ref[...] = (_window_sum(dd / _pool_count(S, g), g, True) - dd).astype(BF16)
        dw = lax.dot_general(dv, dyp, TN, preferred_element_type=F32)
        dsc = jnp.sum(dy * ypre, axis=0, keepdims=True)

        @pl.when(b == 0)
        def _():
            dw_ref[...] = dw
            dsc_ref[...] = dsc

        @pl.when(b > 0)
        def _():
            dw_ref[...] += dw
            dsc_ref[...] += dsc

    blk = pl.BlockSpec((None, S, POOL_GROUP), lambda g, b: (b, 0, g))
    wspec = pl.BlockSpec((None, POOL_GROUP, POOL_GROUP), lambda g, b: (g, 0, 0))
    sspec = pl.BlockSpec((1, POOL_GROUP), lambda g, b: (0, g))
    return pl.pallas_call(
        body, name=name, grid=(len(POOL_WINDOWS), B),
        in_specs=[blk, wspec, sspec, pl.BlockSpec((None, S, POOL_GROUP), lambda g, b: (b, 0, c0 + g))],
        out_specs=[blk, wspec, sspec],
        out_shape=[jax.ShapeDtypeStruct((B, S, D_POOL), BF16),
                   jax.ShapeDtypeStruct((len(POOL_WINDOWS), POOL_GROUP, POOL_GROUP), F32),
                   jax.ShapeDtypeStruct((1, D_POOL), F32)],
        compiler_params=_params(dimension_semantics=("arbitrary", "arbitrary")),
    )(d, wp, pscale, dcat)


def _cross_softmax(q, k):
    s = lax.dot_general(q, k, NT, preferred_element_type=F32) * (CROSS_DIM ** -0.5)
    e = jnp.exp(s - jnp.max(s, axis=-1, keepdims=True))
    return e * (1.0 / jnp.sum(e, axis=-1, keepdims=True))


def _cross_fwd(name, qc, kv, tq=1024):
    B, S, _ = qc.shape
    M = kv.shape[1]
    tq = _tile(S, tq)

    def body(q_ref, k_ref, v_ref, o_ref):
        p = _cross_softmax(q_ref[...], k_ref[...])
        o_ref[...] = jnp.dot(p.astype(BF16), v_ref[...], preferred_element_type=F32).astype(BF16)

    qspec = pl.BlockSpec((None, tq, CROSS_DIM), lambda b, h, i: (b, i, h))
    return pl.pallas_call(
        body, name=name, grid=(B, CROSS_HEADS, S // tq),
        in_specs=[qspec, pl.BlockSpec((None, M, CROSS_DIM), lambda b, h, i: (b, 0, h)),
                  pl.BlockSpec((None, M, CROSS_DIM), lambda b, h, i: (b, 0, CROSS_HEADS + h))],
        out_specs=qspec, out_shape=jax.ShapeDtypeStruct((B, S, D_CROSS), BF16),
        compiler_params=_params(dimension_semantics=("parallel", "parallel", "parallel")),
    )(qc, kv, kv)


def _cross_bwd(name, qc, kv, do, tq=1024):
    B, S, _ = qc.shape
    M = kv.shape[1]
    tq = _tile(S, tq)
    nq = S // tq
    scale = CROSS_DIM ** -0.5

    def body(q_ref, k_ref, v_ref, do_ref, dq_ref, dk_ref, dv_ref, dka, dva):
        i = pl.program_id(2)
        q, k, v, dov = q_ref[...], k_ref[...], v_ref[...], do_ref[...]
        p = _cross_softmax(q, k)
        dp = lax.dot_general(dov, v, NT, preferred_element_type=F32)
        ds = ((p * (dp - jnp.sum(p * dp, axis=-1, keepdims=True))) * scale).astype(BF16)
        dq_ref[...] = jnp.dot(ds, k, preferred_element_type=F32).astype(BF16)
        dk = lax.dot_general(ds, q, TN, preferred_element_type=F32)
        dv = lax.dot_general(p.astype(BF16), dov, TN, preferred_element_type=F32)

        @pl.when(i == 0)
        def _():
            dka[...] = dk
            dva[...] = dv

        @pl.when(i > 0)
        def _():
            dka[...] += dk
            dva[...] += dv

        @pl.when(i == nq - 1)
        def _():
            dk_ref[...] = dka[...].astype(BF16)
            dv_ref[...] = dva[...].astype(BF16)

    qspec = pl.BlockSpec((None, tq, CROSS_DIM), lambda b, h, i: (b, i, h))
    kspec = pl.BlockSpec((None, M, CROSS_DIM), lambda b, h, i: (b, 0, h))
    return pl.pallas_call(
        body, name=name, grid=(B, CROSS_HEADS, nq),
        in_specs=[qspec, kspec, pl.BlockSpec((None, M, CROSS_DIM), lambda b, h, i: (b, 0, CROSS_HEADS + h)), qspec],
        out_specs=[qspec, kspec, kspec],
        out_shape=[jax.ShapeDtypeStruct((B, S, D_CROSS), BF16), jax.ShapeDtypeStruct((B, M, D_CROSS), BF16),
                   jax.ShapeDtypeStruct((B, M, D_CROSS), BF16)],
        scratch_shapes=[pltpu.VMEM((M, CROSS_DIM), F32), pltpu.VMEM((M, CROSS_DIM), F32)],
        compiler_params=_params(dimension_semantics=("parallel", "parallel", "arbitrary")),
    )(qc, kv, kv, do)


def _local_step(x, mem, target, small, weights, emit, start_token=None):
    B, S, D = x.shape
    T = B * S
    x2, t2 = x.reshape(T, D), target.reshape(T, D)
    mem2 = mem.reshape(-1, D)
    n_mem = mem.shape[1]
    wts = {}

    hn1 = _rms_fwd("norm_ffn1", x2, small["ffn1_norm"], after=start_token)
    memn = _rms_fwd("norm_mem", mem2, small["mem_norm"])
    bias = _bias_tile("bias_tile", small["rel_bias"])
    wts.update(weights(0, [hn1, memn, bias]))
    g1 = _ffn_gate("ffn1_gate", hn1, wts["ffn1_w_gate"], N_DEV)
    wts.update(weights(1, g1))
    u1, a1 = _ffn_up_act("ffn1_up", hn1, wts["ffn1_w_up"], g1)
    wts.update(weights(2, a1))
    h1 = _ffn_out("ffn1_down", a1, wts["ffn1_w_down"], res=x2, scale=FFN_RES)
    wts.update(weights(3, h1))
    z, hn2 = _mm_nn_cols("mix_in", h1, wts["w_in"], F32, norm=small["mix_norm"])
    z = z.reshape(B, S, -1)
    mixed = _attn_fwd("attn_fwd", z, bias)
    d_pool, mixed = _pool_fwd("pool_fwd", z, wts["w_pool"], small["pool_scale"], mixed)
    cat = mixed.reshape(T, -1)
    h2 = _mm_nn("mix_out", cat, wts["w_out"], F32, res=h1)
    wts.update(weights(4, h2))
    qc, hn3 = _mm_nn("cross_q", h2, wts["w_cq"], BF16, norm=small["cross_norm"])
    kv = _mm_nn("cross_kv", memn, wts["w_ckv"], BF16)
    o = _cross_fwd("cross_fwd", qc.reshape(B, S, -1), kv.reshape(B, n_mem, -1)).reshape(T, -1)
    h3 = _mm_nn_cols("cross_out", o, wts["w_co"], F32, res=h2)
    wts.update(weights(5, h3))
    hn4, g2, u2, a2 = _ffn_up("ffn2_up", h3, small["ffn2_norm"], wts["ffn2_w_gate"], wts["ffn2_w_up"], N_DEV)
    h4 = _ffn_out("ffn2_down", a2, wts["ffn2_w_down"], res=h3, scale=FFN_RES)

    gs = {}
    loss_part, dh4, gs["final_norm"] = _loss_and_grad("loss", h4, small["final_norm"], t2)

    def ffn_bwd(tag, dhb, h_in, hn, g, u, a, wg, wu, wd, gain, out_dtype=BF16):
        tok = emit({tag + "_w_down": _ffn_dw(tag + "_dwd", a, dhb, scale=FFN_RES)})
        dg, du = _ffn_dact(tag + "_dact", dhb, wd, g, u, after=tok)
        tok = emit({tag + "_w_gate": _ffn_dw(tag + "_dwg", dg, hn)})
        tok = emit({tag + "_w_up": _ffn_dw(tag + "_dwu", du, hn, after=tok)})
        dhn = _ffn_out(tag + "_dhn_g", dg, wg, after=tok)
        dhn = _ffn_out(tag + "_dhn_u", du, wu, res=dhn, out_dtype=BF16)
        return _rms_bwd(tag + "_dnorm", h_in, gain, dhn, skip=dhb, out_dtype=out_dtype)

    dh3b, gs["ffn2_norm"] = ffn_bwd("ffn2", dh4, h3, hn4, g2, u2, a2, wts["ffn2_w_gate"],
                                    wts["ffn2_w_up"], wts["ffn2_w_down"], small["ffn2_norm"])
    do = _mm_nt_cols("cross_do", dh3b, wts["w_co"], BF16, tn=D_CROSS)
    gw = {"w_co": _mm_tn("cross_dwo", o, dh3b, tm=D_CROSS, col_blocks=N_DEV)}
    dqc, dk, dv = _cross_bwd("cross_bwd", qc.reshape(B, S, -1), kv.reshape(B, n_mem, -1), do.reshape(B, S, -1))
    dqc = dqc.reshape(T, -1)
    dkv = jnp.concatenate([dk, dv], axis=-1).reshape(B * n_mem, -1)
    gw["w_cq"] = _mm_tn("cross_dwq", hn3, dqc, tn=D_CROSS)
    gw["w_ckv"] = _mm_tn("cross_dwkv", memn, dkv)
    tok = emit(gw)
    dhn3 = _mm_nt("cross_dhn", dqc, wts["w_cq"], BF16, after=tok)
    dmemn = _mm_nt("cross_dmem", dkv, wts["w_ckv"], F32, tm=512)
    _, gs["mem_norm"] = _rms_bwd("mem_dnorm", mem2, small["mem_norm"], dmemn)
    dh2b, gs["cross_norm"] = _rms_bwd("cross_dnorm", h2, small["cross_norm"], dhn3, skip=dh3b)
    dcat = _mm_nt("mix_dcat", dh2b, wts["w_out"], BF16)
    gw = {"w_out": _mm_tn("mix_dwout", cat, dh2b)}
    dcat3 = dcat.reshape(B, S, -1)
    dq, dkk, dvv, dbias = _attn_bwd("attn_bwd", z, bias, dcat3)
    du, gw["w_pool"], gs["pool_scale"] = _pool_bwd("pool_bwd", d_pool, wts["w_pool"], small["pool_scale"], dcat3)
    gs["rel_bias"] = _rel_grad_from_diagonals(_bias_grad("bias_grad", dbias)[:, 0, :])
    dz = jnp.concatenate([dq, dkk, dvv, du], axis=-1).reshape(T, -1)
    gw["w_in"] = _mm_tn("mix_dwin", hn2, dz, col_blocks=N_DEV)
    tok = emit(gw)
    dhn2 = _mm_nt_cols("mix_dhn", dz, wts["w_in"], BF16, after=tok)
    dh1b, gs["mix_norm"] = _rms_bwd("mix_dnorm", h1, small["mix_norm"], dhn2, skip=dh2b)
    dx, gs["ffn1_norm"] = ffn_bwd("ffn1", dh1b, x2, hn1, g1, u1, a1, wts["ffn1_w_gate"],
                                  wts["ffn1_w_up"], wts["ffn1_w_down"], small["ffn1_norm"], out_dtype=F32)
    return loss_part, dx.reshape(B, S, D), gs


def _position():
    return lax.axis_index("x"), lax.axis_index("y"), lax.axis_index("c")


def _index(p):
    return 4 * p[0] + 2 * p[1] + p[2]


HBM_SPEC = pl.BlockSpec(memory_space=pltpu.HBM)
SEM_SPEC = pl.BlockSpec(memory_space=pltpu.SEMAPHORE)
ANY_SPEC = pl.BlockSpec(memory_space=pl.ANY)
ORDERED_EFFECT = pltpu.SideEffectType.DATAFLOW_SIDE_EFFECTING


N_COPIES = {"grads": N_DEV - 1, "spread": 4, "relay": 3}
N_SEMS = {"grads": N_DEV, "spread": 5, "relay": 3}


def _copies(pattern, srcs, lands, send, recv, base=0):
    x, y, c = _position()
    me, sibling = _index((x, y, c)), (x, y, 1 - c)
    chips = [(1 - x, y), (x, 1 - y), (1 - x, 1 - y)]
    if pattern == "grads":
        targets = [(x ^ (k >> 2), y ^ ((k >> 1) & 1), c ^ (k & 1)) for k in range(1, N_DEV)]
    else:
        targets = [sibling] + [(*chip, c) for chip in chips]
    per, slots, out = N_COPIES[pattern], N_SEMS[pattern], []
    for a in range(len(lands)):
        for k in range(per):
            if pattern == "relay":
                src = dst = lands[a].at[_index((*chips[k], c))]
                to = sibling
            else:
                to = targets[k]
                src = srcs[a].at[_index(to)] if pattern == "grads" else srcs[a]
                dst = lands[a].at[me]
            slot = base + a * slots + k
            out.append(pltpu.make_async_remote_copy(src_ref=src, dst_ref=dst, send_sem=send.at[slot],
                                                    recv_sem=recv.at[slot], device_id=to, device_id_type=MESH))
    return out


def _own_copies(pattern, srcs, lands, send, base=0):
    if pattern == "relay":
        return []
    x, y, c = _position()
    me = _index((x, y, c))
    slots = N_SEMS[pattern]
    return [pltpu.make_async_copy(srcs[a].at[me] if pattern == "grads" else srcs[a], lands[a].at[me],
                                  send.at[base + a * slots + slots - 1]) for a in range(len(lands))]


def _comm(name, srcs, lands, wait=None, start=None, after=None):
    after = [] if after is None else list(after) if isinstance(after, (list, tuple)) else [after]
    ns, nl = len(srcs), len(lands)
    na = ns + nl
    arrays = list(srcs) + list(lands)
    n_wait = 2 if wait else 0
    n_start = 2 if start else 0

    def body(*refs):
        ins, lnd = refs[:ns], refs[ns:na]
        if wait:
            base = wait[3] if len(wait) > 3 else 0
            for cp in _copies(wait[0], ins, lnd, refs[na], refs[na + 1], base):
                cp.wait_send()
                cp.wait_recv()
            for cp in _own_copies(wait[0], ins, lnd, refs[na], base):
                cp.wait()
        if start:
            outs = refs[na + n_wait + len(after):]
            for cp in _copies(start, ins, lnd, outs[0], outs[1]) + _own_copies(start, ins, lnd, outs[0]):
                cp.start()
            refs[-1][...] = jnp.zeros((8, 128), F32)

    out_shape, out_specs = [], []
    if start:
        sems = pltpu.SemaphoreType.DMA((nl * N_SEMS[start],))
        out_shape += [sems, sems]
        out_specs += [SEM_SPEC, SEM_SPEC]
    out_shape += [pltpu.HBM(a.shape, a.dtype) for a in arrays]
    out_specs += [HBM_SPEC] * na
    if start:
        out_shape.append(jax.ShapeDtypeStruct((8, 128), F32))
        out_specs.append(pl.BlockSpec(memory_space=pltpu.VMEM))
    operands = [pltpu.with_memory_space_constraint(a, pltpu.HBM) for a in arrays]
    operands += list(wait[1:3]) if wait else []
    operands += after
    res = pl.pallas_call(
        body, name=name, out_shape=out_shape, out_specs=out_specs,
        in_specs=[HBM_SPEC] * na + [SEM_SPEC] * n_wait + [ANY_SPEC] * len(after),
        input_output_aliases={i: n_start + i for i in range(na)},
        compiler_params=pltpu.CompilerParams(has_side_effects=ORDERED_EFFECT),
    )(*operands)
    res = list(res)
    thru = res[n_start:n_start + na]
    return thru[:ns], thru[ns:], (tuple(res[:2]) if start else None), (res[-1] if start else None)


def _adamw_math(w, g, m, v):
    m = ADAM_B1 * m + (1.0 - ADAM_B1) * g
    v = ADAM_B2 * v + (1.0 - ADAM_B2) * (g * g)
    m_hat = m / (1.0 - ADAM_B1 ** ADAM_STEP)
    v_hat = v / (1.0 - ADAM_B2 ** ADAM_STEP)
    delta = -ADAM_LR * (m_hat / (jnp.sqrt(v_hat) + ADAM_EPS) + ADAM_WD * w)
    return delta, m, v


def _adamw(name, parts, w, m, v, tr=128):
    R, C = w.shape
    tr = _tile(R, tr)

    def body(p_ref, w_ref, m_ref, v_ref, g_out, d_out, m_out, v_out):
        g = p_ref[0].astype(F32)
        for d in range(1, N_DEV):
            g = g + p_ref[d].astype(F32)
        g_out[...] = g
        d_out[...], m_out[...], v_out[...] = _adamw_math(w_ref[...], g, m_ref[...], v_ref[...])

    row = pl.BlockSpec((tr, C), lambda i: (i, 0))
    out = jax.ShapeDtypeStruct((R, C), F32)
    return pl.pallas_call(
        body, name=name, grid=(R // tr,),
        in_specs=[pl.BlockSpec((N_DEV, tr, C), lambda i: (0, i, 0)), row, row, row],
        out_specs=[row] * 4, out_shape=[out] * 4,
        compiler_params=_params(dimension_semantics=("parallel",)),
    )(parts, w, m, v)


def _small_allreduce_adamw(name, g, w, m, v):
    R = g.shape[0]

    def body(g_ref, w_ref, m_ref, v_ref, g_out, d_out, m_out, v_out, land, send, recv):
        x, y, c = _position()
        me = _index((x, y, c))
        land[me] = g_ref[...]
        copies = []
        for k in range(1, N_DEV):
            peer = (x ^ (k >> 2), y ^ ((k >> 1) & 1), c ^ (k & 1))
            copies.append(pltpu.make_async_remote_copy(
                src_ref=g_ref, dst_ref=land.at[me], send_sem=send.at[k - 1], recv_sem=recv.at[k - 1],
                device_id=peer, device_id_type=MESH))
        for cp in copies:
            cp.start()
        for cp in copies:
            cp.wait()
        total = land[0]
        for d in range(1, N_DEV):
            total = total + land[d]
        g_out[...] = total
        d_out[...], m_out[...], v_out[...] = _adamw_math(w_ref[...], total, m_ref[...], v_ref[...])

    vm = pl.BlockSpec(memory_space=pltpu.VMEM)
    out = jax.ShapeDtypeStruct((R, 128), F32)
    return pl.pallas_call(
        body, name=name, in_specs=[vm] * 4, out_specs=[vm] * 4, out_shape=[out] * 4,
        scratch_shapes=[pltpu.VMEM((N_DEV, R, 128), F32), pltpu.SemaphoreType.DMA((7,)),
                        pltpu.SemaphoreType.DMA((7,))],
    )(g, w, m, v)


BIG = ("ffn1_w_gate", "ffn1_w_up", "ffn1_w_down", "w_in", "w_pool", "w_out", "w_cq", "w_ckv", "w_co",
       "ffn2_w_gate", "ffn2_w_up", "ffn2_w_down")
SMALL = ("ffn1_norm", "mix_norm", "rel_bias", "pool_scale", "cross_norm", "mem_norm", "ffn2_norm", "final_norm")
ORDER = ("ffn1_norm", "ffn1_w_gate", "ffn1_w_up", "ffn1_w_down", "mix_norm", "w_in", "rel_bias", "w_pool",
         "pool_scale", "w_out", "cross_norm", "mem_norm", "w_cq", "w_ckv", "w_co", "ffn2_norm", "ffn2_w_gate",
         "ffn2_w_up", "ffn2_w_down", "final_norm")
TRANSPOSED = ("ffn1_w_gate", "ffn1_w_up", "ffn2_w_gate", "ffn2_w_up")
ROW_SHARDED = TRANSPOSED + ("ffn1_w_down", "ffn2_w_down", "w_out", "w_cq", "w_ckv")
GATHER_GROUPS = (("ffn1_w_gate",), ("ffn1_w_up",), ("ffn1_w_down",), ("w_in", "w_pool", "w_out"),
                 ("w_cq", "w_ckv", "w_co"), ("ffn2_w_gate", "ffn2_w_up", "ffn2_w_down"))
RELAY_BEFORE_USE = ((0,), (1,), (2,), (3,), (4, 5), ())


def _pack(arrays):
    flat = jnp.concatenate([a.reshape(-1) for a in arrays])
    rows = -(-flat.shape[0] // 1024) * 8
    return jnp.pad(flat, (0, rows * 128 - flat.shape[0])).reshape(rows, 128)


def _unpack(packed, like):
    flat, out, at = packed.reshape(-1), [], 0
    for a in like:
        out.append(flat[at:at + a.size].reshape(a.shape))
        at += a.size
    return out


def _shard2d(a):
    a = a[0]
    return a.reshape(-1, a.shape[-1])


def kernel(x, mem, ffn1_norm, ffn1_w_gate, ffn1_w_up, ffn1_w_down, mix_norm, w_in, rel_bias, w_pool, pool_scale, w_out, cross_norm, mem_norm, w_cq, w_ckv, w_co, ffn2_norm, ffn2_w_gate, ffn2_w_up, ffn2_w_down, final_norm, loss_target, m_ffn1_norm, m_ffn1_w_gate, m_ffn1_w_up, m_ffn1_w_down, m_mix_norm, m_w_in, m_rel_bias, m_w_pool, m_pool_scale, m_w_out, m_cross_norm, m_mem_norm, m_w_cq, m_w_ckv, m_w_co, m_ffn2_norm, m_ffn2_w_gate, m_ffn2_w_up, m_ffn2_w_down, m_final_norm, v_ffn1_norm, v_ffn1_w_gate, v_ffn1_w_up, v_ffn1_w_down, v_mix_norm, v_w_in, v_rel_bias, v_w_pool, v_pool_scale, v_w_out, v_cross_norm, v_mem_norm, v_w_cq, v_w_ckv, v_w_co, v_ffn2_norm, v_ffn2_w_gate, v_ffn2_w_up, v_ffn2_w_down, v_final_norm):
    args = dict(locals())
    def view(n, a):
        return a.transpose(0, 2, 1) if n in TRANSPOSED else a

    w_in_ = {n: view(n, args[n]) for n in ORDER}
    m_in = {n: view(n, args["m_" + n]) for n in ORDER}
    v_in = {n: view(n, args["v_" + n]) for n in ORDER}

    n_g, rows = len(POOL_WINDOWS), POOL_GROUP // N_DEV

    def landing(block_shape, dtype):
        return lax.empty((N_DEV,) + tuple(block_shape), dtype)

    gathers, tok = [], None
    for first in (True, False):
        groups = GATHER_GROUPS[:1] if first else GATHER_GROUPS[1:]
        shards = [_shard2d(w_in_[n]) for group in groups for n in group]
        shards = [(s if tok is None else s + tok[0, 0]).astype(BF16) for s in shards]
        srcs, lands, sems, tok = _comm("gather_start_%d" % (not first), shards,
                                       [landing(s.shape, BF16) for s in shards], start="spread", after=tok)
        at = 0
        for group in groups:
            gathers.append((srcs[at:at + len(group)], lands[at:at + len(group)], sems + (at * N_SEMS["spread"],)))
            at += len(group)

    def weights(gi, after):
        for ri in RELAY_BEFORE_USE[gi]:
            srcs, lands, sems = gathers[ri]
            _, lands, sems, after = _comm("gather_relay_%d" % ri, srcs, lands, wait=("spread",) + sems,
                                          start="relay", after=after)
            gathers[ri] = (None, lands, sems)
        _, lands, sems = gathers[gi]
        _, lands, _, _ = _comm("gather_finish_%d" % gi, [], lands, wait=("relay",) + sems, after=after)
        out = {}
        for n, full in zip(GATHER_GROUPS[gi], lands):
            if n == "w_pool":
                full = full.reshape(N_DEV, n_g, rows, POOL_GROUP).transpose(1, 0, 2, 3).reshape(n_g, POOL_GROUP, POOL_GROUP)
            out[n] = full.reshape(-1, full.shape[-1]) if n in ROW_SHARDED else full
        return out

    scatters = []

    def emit(gw):
        names = list(gw)
        stacks = []
        for n in names:
            g = gw[n]
            if n == "w_pool":
                g = g.reshape(n_g, N_DEV, rows, POOL_GROUP).transpose(1, 0, 2, 3).astype(BF16)
            stacks.append(g.reshape((N_DEV,) + _shard2d(w_in_[n]).shape))
        lands = [landing(s.shape[1:], s.dtype) for s in stacks]
        srcs, lands, sems, token = _comm("grads_start_%d" % len(scatters), stacks, lands, start="grads")
        scatters.append((names, srcs, lands, sems))
        return token

    small = {n: w_in_[n].reshape(1, -1) for n in SMALL if n != "rel_bias"}
    small["rel_bias"] = rel_bias[0]
    loss_part, grad_x, gs = _local_step(x, mem, loss_target, small, weights, emit, start_token=tok)

    grad, delta, new_m, new_v = {}, {}, {}, {}
    after = grad_x
    for si, (names, srcs, lands, sems) in enumerate(scatters):
        _, landed, _, _ = _comm("grads_finish_%d" % si, srcs, lands, wait=("grads",) + sems, after=after)
        for n, parts in zip(names, landed):
            res = _adamw("adamw_" + n, parts, _shard2d(w_in_[n]), _shard2d(m_in[n]), _shard2d(v_in[n]))
            grad[n], delta[n], new_m[n], new_v[n] = [view(n, r.reshape(w_in_[n].shape)) for r in res]
        after = res[0]

    slot = jnp.zeros((1,), F32)
    like = [w_in_[n] for n in SMALL] + [slot]
    gs["rel_bias"] = gs["rel_bias"].reshape(rel_bias.shape)
    res = _small_allreduce_adamw("small_params", _pack([gs[n] for n in SMALL] + [loss_part[0, :1]]), _pack(like),
                                 _pack([m_in[n] for n in SMALL] + [slot]), _pack([v_in[n] for n in SMALL] + [slot]))
    for d, packed in zip((grad, delta, new_m, new_v), res):
        for n, a in zip(SMALL, _unpack(packed, like)):
            d[n] = a
    loss = _unpack(res[0], like)[-1][0]
    return (loss, grad_x, *[grad[n] for n in ORDER], *[delta[n] for n in ORDER],
            *[new_m[n] for n in ORDER], *[new_v[n] for n in ORDER])
```

```python
import jax
import jax.numpy as jnp
from jax import lax
from jax.experimental import pallas as pl
from jax.experimental.pallas import tpu as pltpu

F32 = jnp.float32
BF16 = jnp.bfloat16

N_DEV = 8
EPS = 1e-6
NEG_INF = -1e30
CHUNK = 64
LEFT_CHUNKS = 8
PAD = LEFT_CHUNKS * CHUNK
QBLK = 4 * CHUNK
KBAND = PAD + QBLK
REL_CLIP = 128
ATTN_HEADS = 16
HEAD_DIM = 64
D_ATTN = ATTN_HEADS * HEAD_DIM
POOL_WINDOWS = (2, 4, 8, 16)
POOL_GROUP = 256
D_POOL = len(POOL_WINDOWS) * POOL_GROUP
CROSS_HEADS = 4
CROSS_DIM = 128
D_CROSS = CROSS_HEADS * CROSS_DIM
FFN_RES = 0.5
ADAM_LR, ADAM_B1, ADAM_B2, ADAM_EPS, ADAM_WD, ADAM_STEP = 0.001, 0.9, 0.999, 1e-08, 0.01, 10

NN = (((1,), (0,)), ((), ()))
NT = (((1,), (1,)), ((), ()))
TN = (((0,), (0,)), ((), ()))
MESH = pl.DeviceIdType.MESH
VMEM_LIMIT = 56 * 1024 * 1024
ADAM_STEP_ELEMS = 384 * 1024


def _params(**kw):
    return pltpu.CompilerParams(vmem_limit_bytes=VMEM_LIMIT, **kw)


def _bf(v):
    return v if v.dtype == BF16 else v.astype(BF16)


WHOLE = ((Ellipsis,), (Ellipsis,))


def _rms(xv, gain):
    return (xv * lax.rsqrt(jnp.mean(xv * xv, axis=-1, keepdims=True) + EPS)) * gain


def _gemm(name, a, a_spec, b, b_spec, dims, grid, outs, chunks=(WHOLE,), extras=(), epilogue=None, after=None,
          norm=None):
    nex, nout = len(extras), len(outs)
    first_out = 2 + nex + (after is not None) + (norm is not None)

    def body(*refs):
        a_ref, b_ref = refs[:2]
        if norm is not None:
            hn_out, a_ref = refs[first_out + nout], refs[-1]

            @pl.when(pl.program_id(1) == 0)
            def _():
                hn = _rms(refs[0][...], refs[first_out - 1][...]).astype(BF16)
                a_ref[...] = hn
                hn_out[...] = hn

        total = None
        for ia, ib in chunks:
            d = lax.dot_general(_bf(a_ref[ia]), _bf(b_ref[ib]), dims, preferred_element_type=F32)
            total = d if total is None else total + d
        vals = epilogue(total, *[e[...] for e in refs[2:2 + nex]]) if epilogue is not None else (total,)
        for r, v in zip(refs[first_out:first_out + nout], vals):
            r[...] = v.astype(r.dtype)

    operands = [a, b] + [x for x, _, _ in extras]
    in_specs = [pl.BlockSpec(*a_spec), pl.BlockSpec(*b_spec)] + [pl.BlockSpec(blk, m) for _, blk, m in extras]
    if after is not None:
        operands.append(after)
        in_specs.append(pl.BlockSpec(after.shape, lambda i, j: (0, 0)))
    out_specs = [pl.BlockSpec(blk, m) for _, _, blk, m in outs]
    out_shape = [jax.ShapeDtypeStruct(s, d) for s, d, _, _ in outs]
    scratch = []
    if norm is not None:
        operands.append(norm)
        in_specs.append(pl.BlockSpec(norm.shape, lambda i, j: (0, 0)))
        out_specs.append(pl.BlockSpec(*a_spec))
        out_shape.append(jax.ShapeDtypeStruct(a.shape, BF16))
        scratch.append(pltpu.VMEM(a_spec[0], BF16))
    res = pl.pallas_call(
        body, name=name, grid=grid, in_specs=in_specs, out_specs=out_specs, out_shape=out_shape,
        scratch_shapes=scratch,
        compiler_params=_params(dimension_semantics=("parallel", "arbitrary" if norm is not None else "parallel")),
    )(*operands)
    return res[0] if len(res) == 1 else res


def _tile(n, want):
    for t in range(min(n, want), 15, -1):
        if n % t == 0 and t % 16 == 0:
            return t
    return n


def _mm_nn(name, a, b, out_dtype, res=None, tm=1024, tn=1024, norm=None):
    M, K = a.shape
    N = b.shape[1]
    tm, tn = _tile(M, tm), _tile(N, tn)
    extras = [] if res is None else [(res, (tm, tn), lambda i, j: (i, j))]
    epi = None if res is None else (lambda t, r: (r + t,))
    return _gemm(name, a, ((tm, K), lambda i, j: (i, 0)), b, ((K, tn), lambda i, j: (0, j)), NN,
                 (M // tm, N // tn), [((M, N), out_dtype, (tm, tn), lambda i, j: (i, j))], extras=extras, epilogue=epi,
                 norm=norm)


def _mm_nn_cols(name, a, bs, out_dtype, res=None, tm=1024, norm=None):
    M, K = a.shape
    nb, _, w = bs.shape
    tm = _tile(M, tm)
    extras = [] if res is None else [(res, (tm, w), lambda i, j: (i, j))]
    epi = None if res is None else (lambda t, r: (r + t,))
    return _gemm(name, a, ((tm, K), lambda i, j: (i, 0)), bs, ((None, K, w), lambda i, j: (j, 0, 0)), NN,
                 (M // tm, nb), [((M, nb * w), out_dtype, (tm, w), lambda i, j: (i, j))], extras=extras, epilogue=epi,
                 norm=norm)


def _mm_nt(name, a, b, out_dtype, tm=1024, tn=1024, after=None):
    M, K = a.shape
    N = b.shape[0]
    tm, tn = _tile(M, tm), _tile(N, tn)
    return _gemm(name, a, ((tm, K), lambda i, j: (i, 0)), b, ((tn, K), lambda i, j: (j, 0)), NT,
                 (M // tm, N // tn), [((M, N), out_dtype, (tm, tn), lambda i, j: (i, j))], after=after)


def _mm_nt_cols(name, a, bs, out_dtype, tm=1024, tn=512, after=None):
    M = a.shape[0]
    nb, N, w = bs.shape
    tm, tn = _tile(M, tm), _tile(N, tn)
    chunks = [((slice(None), pl.ds(c * w, w)), (c,)) for c in range(nb)]
    return _gemm(name, a, ((tm, nb * w), lambda i, j: (i, 0)), bs, ((nb, tn, w), lambda i, j: (0, j, 0)), NT,
                 (M // tm, N // tn), [((M, N), out_dtype, (tm, tn), lambda i, j: (i, j))], chunks, after=after)


def _mm_tn(name, a, b, tm=1024, tn=1024, col_blocks=None):
    T, Ka = a.shape
    Nb = b.shape[1]
    tm = _tile(Ka, tm)
    if col_blocks is None:
        tn = _tile(Nb, tn)
        out = ((Ka, Nb), BF16, (tm, tn), lambda i, j: (i, j))
    else:
        tn = Nb // col_blocks
        out = ((col_blocks, Ka, tn), BF16, (None, tm, tn), lambda i, j: (j, i, 0))
    return _gemm(name, a, ((T, tm), lambda i, j: (0, i)), b, ((T, tn), lambda i, j: (0, j)), TN,
                 (Ka // tm, Nb // tn), [out])


def _ffn_out(name, a, wt, res=None, scale=1.0, tm=1024, tn=512, after=None, out_dtype=F32):
    nb, M, w = a.shape
    N = wt.shape[1]
    tm, tn = _tile(M, tm), _tile(N, tn)
    chunks = [((c,), (pl.ds(c * w, w),)) for c in range(nb)]
    extras = [] if res is None else [(res, (tm, tn), lambda i, j: (i, j))]
    epi = None if res is None else (lambda t, r: (r + scale * t,))
    return _gemm(name, a, ((nb, tm, w), lambda i, j: (0, i, 0)), wt, ((nb * w, tn), lambda i, j: (0, j)), NN,
                 (M // tm, N // tn), [((M, N), out_dtype, (tm, tn), lambda i, j: (i, j))], chunks, extras, epi, after)


def _ffn_dact(name, dhb, wd, g, u, tm=1024, after=None):
    M, K = dhb.shape
    nb, _, w = g.shape
    tm = _tile(M, tm)
    tr = _tile(tm, 256)
    tokens = [] if after is None else [after]

    def body(dh_ref, wd_ref, g_ref, u_ref, *rest):
        dg_ref, du_ref = rest[-2:]
        pieces = [pl.ds(r * tr, tr) for r in range(tm // tr)]

        def product(rows):
            return lax.dot_general(dh_ref[rows, :], wd_ref[...], NT, preferred_element_type=F32)

        results = []
        dact = product(pieces[0])
        for n, rows in enumerate(pieces):
            ahead = product(pieces[n + 1]) if n + 1 < len(pieces) else None
            results.append(_swiglu_bwd(dact, g_ref[rows, :], u_ref[rows, :]))
            dact = ahead
        for rows, (dg, du) in zip(pieces, results):
            dg_ref[rows, :] = dg.astype(BF16)
            du_ref[rows, :] = du.astype(BF16)

    hid = pl.BlockSpec((None, tm, w), lambda i, j: (j, i, 0))
    return pl.pallas_call(
        body, name=name, grid=(M // tm, nb),
        in_specs=[pl.BlockSpec((tm, K), lambda i, j: (i, 0)), pl.BlockSpec((w, K), lambda i, j: (j, 0)), hid, hid]
        + [pl.BlockSpec(t.shape, lambda i, j: (0, 0)) for t in tokens],
        out_specs=[hid, hid], out_shape=[jax.ShapeDtypeStruct((nb, M, w), BF16)] * 2,
        compiler_params=_params(dimension_semantics=("parallel", "parallel")),
    )(dhb, wd, g, u, *tokens)


def _ffn_dw(name, a, b, scale=1.0, tn=1024, after=None):
    nb, T, w = a.shape
    N = b.shape[1]
    tn = _tile(N, tn)
    epi = None if scale == 1.0 else (lambda t: (t * scale,))
    return _gemm(name, a, ((None, T, w), lambda i, j: (i, 0, 0)), b, ((T, tn), lambda i, j: (0, j)), TN,
                 (nb, N // tn), [((nb * w, N), BF16, (w, tn), lambda i, j: (i, j))], epilogue=epi, after=after)


def _ffn_gate(name, hn, wgt, nb, tm=1024):
    M, K = hn.shape
    w = wgt.shape[0] // nb
    tm = _tile(M, tm)
    return _gemm(name, hn, ((tm, K), lambda i, j: (i, 0)), wgt, ((w, K), lambda i, j: (j, 0)), NT,
                 (M // tm, nb), [((nb, M, w), BF16, (None, tm, w), lambda i, j: (j, i, 0))])


def _ffn_up_act(name, hn, wut, g, tm=1024):
    M, K = hn.shape
    nb, _, w = g.shape
    tm = _tile(M, tm)
    hid = ((None, tm, w), lambda i, j: (j, i, 0))

    def epilogue(u, gate):
        gate = gate.astype(F32)
        return u, gate * jax.nn.sigmoid(gate) * u

    return _gemm(name, hn, ((tm, K), lambda i, j: (i, 0)), wut, ((w, K), lambda i, j: (j, 0)), NT,
                 (M // tm, nb), [((nb, M, w), BF16) + hid] * 2, extras=[(g,) + hid], epilogue=epilogue)


def _ffn_up(name, h, gain, wgt, wut, nb, tm=1024):
    M, K = h.shape
    w = wgt.shape[0] // nb
    tm = _tile(M, tm)

    def body(h_ref, gain_ref, g_ref, u_ref, hn_ref, og, ou, oa, a_ref):
        @pl.when(pl.program_id(1) == 0)
        def _():
            hn = _rms(h_ref[...], gain_ref[...]).astype(BF16)
            a_ref[...] = hn
            hn_ref[...] = hn

        a = a_ref[...]
        g = lax.dot_general(a, g_ref[...], NT, preferred_element_type=F32)
        u = lax.dot_general(a, u_ref[...], NT, preferred_element_type=F32)
        og[...] = g.astype(BF16)
        ou[...] = u.astype(BF16)
        oa[...] = (g * jax.nn.sigmoid(g) * u).astype(BF16)

    rows = pl.BlockSpec((tm, K), lambda i, j: (i, 0))
    wspec = pl.BlockSpec((w, K), lambda i, j: (j, 0))
    ospec = pl.BlockSpec((None, tm, w), lambda i, j: (j, i, 0))
    return pl.pallas_call(
        body, name=name, grid=(M // tm, nb),
        in_specs=[rows, pl.BlockSpec((1, K), lambda i, j: (0, 0)), wspec, wspec],
        out_specs=[rows] + [ospec] * 3,
        out_shape=[jax.ShapeDtypeStruct((M, K), BF16)] + [jax.ShapeDtypeStruct((nb, M, w), BF16)] * 3,
        scratch_shapes=[pltpu.VMEM((tm, K), BF16)],
        compiler_params=_params(dimension_semantics=("parallel", "arbitrary")),
    )(h, gain, wgt, wut)


def _swiglu_bwd(dact, g, u):
    g = g.astype(F32)
    u = u.astype(F32)
    sig = jax.nn.sigmoid(g)
    silu = g * sig
    d = FFN_RES * dact
    return d * u * (sig * (1.0 + g * (1.0 - sig))), d * silu


def _rms_fwd(name, x, gain, tr=512, after=None):
    R, D = x.shape
    tr = _tile(R, tr)

    def body(x_ref, g_ref, *rest):
        rest[-1][...] = _rms(x_ref[...], g_ref[...]).astype(BF16)

    tokens = [] if after is None else [after]
    return pl.pallas_call(
        body, name=name, grid=(R // tr,),
        in_specs=[pl.BlockSpec((tr, D), lambda i: (i, 0)), pl.BlockSpec((1, D), lambda i: (0, 0))]
        + [pl.BlockSpec(t.shape, lambda i: (0, 0)) for t in tokens],
        out_specs=pl.BlockSpec((tr, D), lambda i: (i, 0)), out_shape=jax.ShapeDtypeStruct((R, D), BF16),
        compiler_params=_params(dimension_semantics=("parallel",)),
    )(x, gain, *tokens)


def _rms_bwd_math(xv, gain, dy):
    rstd = lax.rsqrt(jnp.mean(xv * xv, axis=-1, keepdims=True) + EPS)
    xhat = xv * rstd
    dxh = dy * gain
    dx = rstd * (dxh - xhat * jnp.mean(dxh * xhat, axis=-1, keepdims=True))
    return dx, jnp.sum(dy * xhat, axis=0, keepdims=True)


def _rms_bwd(name, x, gain, dy, skip=None, out_dtype=BF16, tr=512):
    R, D = x.shape
    tr = _tile(R, tr)
    has_skip = skip is not None

    def body(*refs):
        x_ref, g_ref, dy_ref = refs[:3]
        dx_ref, dg_ref = refs[-2:]
        dx, dg = _rms_bwd_math(x_ref[...], g_ref[...], dy_ref[...].astype(F32))
        if has_skip:
            dx = dx + refs[3][...].astype(F32)
        dx_ref[...] = dx.astype(out_dtype)

        @pl.when(pl.program_id(0) == 0)
        def _():
            dg_ref[...] = dg

        @pl.when(pl.program_id(0) > 0)
        def _():
            dg_ref[...] += dg

    row = pl.BlockSpec((tr, D), lambda i: (i, 0))
    vec = pl.BlockSpec((1, D), lambda i: (0, 0))
    return pl.pallas_call(
        body, name=name, grid=(R // tr,),
        in_specs=[row, vec, row] + ([row] if has_skip else []),
        out_specs=[row, vec],
        out_shape=[jax.ShapeDtypeStruct((R, D), out_dtype), jax.ShapeDtypeStruct((1, D), F32)],
        compiler_params=_params(dimension_semantics=("arbitrary",)),
    )(*([x, gain, dy] + ([skip] if has_skip else [])))


def _loss_and_grad(name, h, gain, target, tr=512):
    R, D = h.shape
    tr = _tile(R, tr)

    def body(h_ref, g_ref, t_ref, loss_ref, dhb_ref, dg_ref):
        hv, gain_v = h_ref[...], g_ref[...]
        y = (hv * lax.rsqrt(jnp.mean(hv * hv, axis=-1, keepdims=True) + EPS)) * gain_v
        err = y - t_ref[...]
        part = jnp.full((8, 128), 0.5 * jnp.sum(jnp.mean(err * err, axis=-1, keepdims=True)), F32)
        dh, dg = _rms_bwd_math(hv, gain_v, err * (1.0 / D))
        dhb_ref[...] = dh.astype(BF16)

        @pl.when(pl.program_id(0) == 0)
        def _():
            dg_ref[...] = dg
            loss_ref[...] = part

        @pl.when(pl.program_id(0) > 0)
        def _():
            dg_ref[...] += dg
            loss_ref[...] += part

    row = pl.BlockSpec((tr, D), lambda i: (i, 0))
    vec = pl.BlockSpec((1, D), lambda i: (0, 0))
    return pl.pallas_call(
        body, name=name, grid=(R // tr,), in_specs=[row, vec, row],
        out_specs=[pl.BlockSpec((8, 128), lambda i: (0, 0)), row, vec],
        out_shape=[jax.ShapeDtypeStruct((8, 128), F32), jax.ShapeDtypeStruct((R, D), BF16),
                   jax.ShapeDtypeStruct((1, D), F32)],
        compiler_params=_params(dimension_semantics=("arbitrary",)),
    )(h, gain, target)


def _bias_tile(name, rel):
    width = KBAND + QBLK
    sat = rel[:, 2 * REL_CLIP:]
    n_left = PAD - REL_CLIP + 1
    row0 = jnp.concatenate([jnp.broadcast_to(sat, (ATTN_HEADS, n_left)), rel[:, :2 * REL_CLIP][:, ::-1],
                            jnp.broadcast_to(sat, (ATTN_HEADS, width - n_left - 2 * REL_CLIP))], axis=1)

    def body(e_ref, o_ref):
        rows = pltpu.roll(jnp.broadcast_to(e_ref[...], (QBLK, width)), 0, 1, stride=1, stride_axis=0)
        i = lax.broadcasted_iota(jnp.int32, (QBLK, KBAND), 0) // CHUNK
        j = lax.broadcasted_iota(jnp.int32, (QBLK, KBAND), 1) // CHUNK
        o_ref[...] = jnp.where((j >= i) & (j <= i + LEFT_CHUNKS), rows[:, :KBAND], NEG_INF)

    return pl.pallas_call(
        body, name=name, grid=(ATTN_HEADS,),
        in_specs=[pl.BlockSpec((None, 1, width), lambda h: (h, 0, 0))],
        out_specs=pl.BlockSpec((None, QBLK, KBAND), lambda h: (h, 0, 0)),
        out_shape=jax.ShapeDtypeStruct((ATTN_HEADS, QBLK, KBAND), F32),
        compiler_params=_params(dimension_semantics=("parallel",)),
    )(row0.reshape(ATTN_HEADS, 1, width))


ATTN_SCALE = HEAD_DIM ** -0.5
ROW_PIECES = 1


def _stack_heads(x, first):
    zero = jnp.zeros_like(x)
    return jnp.concatenate([jnp.where(first, x, zero), jnp.where(first, zero, x)], axis=0)


def _band_softmax(q_half_scaled, kb, bias, left_mask):
    s = lax.dot_general(q_half_scaled, kb, NT, preferred_element_type=F32) + bias + left_mask
    e = jnp.exp(s - jnp.max(s, axis=-1, keepdims=True))
    return e * (1.0 / jnp.sum(e, axis=-1, keepdims=True))


def _left_mask(qb):
    kpos = qb * QBLK - PAD + lax.broadcasted_iota(jnp.int32, (1, KBAND), 1)
    return jnp.where(kpos >= 0, 0.0, NEG_INF).astype(F32)


def _fill_padded(dst, src, S):
    dst[pl.ds(0, PAD), :] = jnp.zeros((PAD, dst.shape[1]), dst.dtype)
    dst[pl.ds(PAD, S), :] = src[...].astype(dst.dtype)


def _attn_fwd(name, z, bias):
    B, S, _ = z.shape
    nh2 = ATTN_HEADS // 2

    def body(q_ref, k_ref, v_ref, b_ref, o_ref, kp, vp):
        qb = pl.program_id(2)

        @pl.when(qb == 0)
        def _():
            _fill_padded(kp, k_ref, S)
            _fill_padded(vp, v_ref, S)

        start = pl.multiple_of(qb * QBLK, QBLK)
        kb, vb = kp[pl.ds(start, KBAND), :], vp[pl.ds(start, KBAND), :]
        q = (q_ref[...] * ATTN_SCALE).astype(BF16)
        first = lax.broadcasted_iota(jnp.int32, (QBLK, 2 * HEAD_DIM), 1) < HEAD_DIM
        left = _left_mask(qb)
        zero = jnp.zeros_like(q)
        qh = [jnp.where(first, q, zero), jnp.where(first, zero, q)]
        rp = QBLK // ROW_PIECES
        chains = [(a, r) for r in range(ROW_PIECES) for a in range(2)]
        ss = [lax.dot_general(qh[a][r * rp:(r + 1) * rp], kb, NT, preferred_element_type=F32) for a, r in chains]
        ps = []
        for (a, r), s in zip(chains, ss):
            s = s + b_ref[a, pl.ds(r * rp, rp), :] + left
            e = jnp.exp(s - jnp.max(s, axis=-1, keepdims=True))
            ps.append((e * (1.0 / jnp.sum(e, axis=-1, keepdims=True))).astype(BF16))
        os_ = [jnp.dot(p, vb, preferred_element_type=F32) for p in ps]
        for r in range(ROW_PIECES):
            o_ref[pl.ds(r * rp, rp), :] = jnp.where(first[:rp], os_[2 * r], os_[2 * r + 1]).astype(BF16)

    return pl.pallas_call(
        body, name=name, grid=(B, nh2, S // QBLK),
        in_specs=[pl.BlockSpec((None, QBLK, 128), lambda b, h, i: (b, i, h)),
                  pl.BlockSpec((None, S, 128), lambda b, h, i: (b, 0, nh2 + h)),
                  pl.BlockSpec((None, S, 128), lambda b, h, i: (b, 0, 2 * nh2 + h)),
                  pl.BlockSpec((2, QBLK, KBAND), lambda b, h, i: (h, 0, 0))],
        out_specs=pl.BlockSpec((None, QBLK, 128), lambda b, h, i: (b, i, h)),
        out_shape=jax.ShapeDtypeStruct((B, S, D_ATTN + D_POOL), BF16),
        scratch_shapes=[pltpu.VMEM((PAD + S, 128), BF16), pltpu.VMEM((PAD + S, 128), BF16)],
        compiler_params=_params(dimension_semantics=("parallel", "parallel", "arbitrary")),
    )(z, z, z, bias)


def _attn_bwd(name, z, bias, dcat):
    B, S, _ = z.shape
    nh2 = ATTN_HEADS // 2
    nqb = S // QBLK

    def body(q_ref, k_ref, v_ref, b_ref, do_ref, dq_ref, dk_ref, dv_ref, db_ref, kp, vp, dka, dva):
        b, qb = pl.program_id(1), pl.program_id(2)

        @pl.when(qb == 0)
        def _():
            _fill_padded(kp, k_ref, S)
            _fill_padded(vp, v_ref, S)
            dka[...] = jnp.zeros_like(dka)
            dva[...] = jnp.zeros_like(dva)

        @pl.when((qb == 0) & (b == 0))
        def _():
            db_ref[...] = jnp.zeros_like(db_ref)

        start = pl.multiple_of(qb * QBLK, QBLK)
        band = pl.ds(start, KBAND)
        kb, vb = kp[band, :], vp[band, :]
        q = (q_ref[...] * ATTN_SCALE).astype(BF16)
        do = do_ref[...]
        first = lax.broadcasted_iota(jnp.int32, (QBLK, 2 * HEAD_DIM), 1) < HEAD_DIM
        left = _left_mask(qb)
        q2, do2 = _stack_heads(q, first), _stack_heads(do, first)
        p = _band_softmax(q2, kb, b_ref[...].reshape(2 * QBLK, KBAND), left)
        dp = lax.dot_general(do2, vb, NT, preferred_element_type=F32)
        ds = p * (dp - jnp.sum(p * dp, axis=-1, keepdims=True))
        db_ref[...] += ds.reshape(2, QBLK, KBAND)
        dsb = ds.astype(BF16)
        dq = jnp.dot(dsb, kb, preferred_element_type=F32)
        dq_ref[...] = (jnp.where(first, dq[:QBLK], dq[QBLK:]) * ATTN_SCALE).astype(BF16)
        dka[:, band] += lax.dot_general(q2, dsb, TN, preferred_element_type=F32)
        dva[:, band] += lax.dot_general(do2, p.astype(BF16), TN, preferred_element_type=F32)

        @pl.when(qb == nqb - 1)
        def _():
            dk_ref[...] = dka[:, pl.ds(PAD, S)].T.astype(BF16)
            dv_ref[...] = dva[:, pl.ds(PAD, S)].T.astype(BF16)

    qspec = pl.BlockSpec((None, QBLK, 128), lambda h, b, i: (b, i, h))
    kvout = pl.BlockSpec((None, S, 128), lambda h, b, i: (b, 0, h))
    bspec = pl.BlockSpec((2, QBLK, KBAND), lambda h, b, i: (h, 0, 0))
    act = jax.ShapeDtypeStruct((B, S, D_ATTN), BF16)
    return pl.pallas_call(
        body, name=name, grid=(nh2, B, nqb),
        in_specs=[qspec,
                  pl.BlockSpec((None, S, 128), lambda h, b, i: (b, 0, nh2 + h)),
                  pl.BlockSpec((None, S, 128), lambda h, b, i: (b, 0, 2 * nh2 + h)),
                  bspec, qspec],
        out_specs=[qspec, kvout, kvout, bspec],
        out_shape=[act, act, act, jax.ShapeDtypeStruct((ATTN_HEADS, QBLK, KBAND), F32)],
        scratch_shapes=[pltpu.VMEM((PAD + S, 128), BF16), pltpu.VMEM((PAD + S, 128), BF16),
                        pltpu.VMEM((128, PAD + S), F32), pltpu.VMEM((128, PAD + S), F32)],
        compiler_params=_params(dimension_semantics=("arbitrary", "arbitrary", "arbitrary")),
    )(z, z, z, bias, dcat)


def _bias_grad(name, dbias):
    width = KBAND + QBLK

    def body(d_ref, o_ref):
        acc = jnp.zeros((1, width), F32)
        for i in range(QBLK):
            row = jnp.concatenate([d_ref[pl.ds(i, 1), :], jnp.zeros((1, QBLK), F32)], axis=1)
            shift = QBLK - 1 - i
            acc = acc + (pltpu.roll(row, shift, 1) if shift else row)
        o_ref[...] = acc

    return pl.pallas_call(
        body, name=name, grid=(ATTN_HEADS,),
        in_specs=[pl.BlockSpec((None, QBLK, KBAND), lambda h: (h, 0, 0))],
        out_specs=pl.BlockSpec((None, 1, width), lambda h: (h, 0, 0)),
        out_shape=jax.ShapeDtypeStruct((ATTN_HEADS, 1, width), F32),
        compiler_params=_params(dimension_semantics=("parallel",)),
    )(dbias)


def _rel_grad_from_diagonals(diag):
    top = PAD + QBLK - 1 - REL_CLIP
    sat = jnp.sum(diag[:, :top + 1], axis=1, keepdims=True)
    mid = diag[:, top + 1:top + 2 * REL_CLIP][:, ::-1]
    return jnp.concatenate([jnp.zeros_like(sat), mid, sat], axis=1)


def _shift_rows(x, k, forward):
    S = x.shape[0]
    t = lax.broadcasted_iota(jnp.int32, x.shape, 0)
    if forward:
        return jnp.where(t < S - k, pltpu.roll(x, S - k, 0), 0.0)
    return jnp.where(t >= k, pltpu.roll(x, k, 0), 0.0)


def _window_sum(x, g, forward):
    s = x + _shift_rows(x, 1, forward)
    out = s
    for n, k in enumerate((2, 4, 8)):
        s = s + _shift_rows(s, k, forward)
        out = jnp.where(g > n, s, out)
    return out


def _pool_count(S, g):
    t = lax.broadcasted_iota(jnp.int32, (S, 1), 0)
    w = jnp.left_shift(2, g)
    return jnp.minimum(t + 1, w).astype(F32)


def _pool_fwd(name, z, wp, pscale, mixed):
    B, S, _ = z.shape
    c0 = 3 * D_ATTN // POOL_GROUP
    y0 = D_ATTN // POOL_GROUP

    def body(u_ref, w_ref, s_ref, mixed_ref, d_ref, y_ref):
        g = pl.program_id(1)
        u = u_ref[...]
        d = (_window_sum(u, g, False) / _pool_count(S, g) - u).astype(BF16)
        d_ref[...] = d
        y_ref[...] = (jnp.dot(d, w_ref[...], preferred_element_type=F32) * s_ref[...]).astype(BF16)

    return pl.pallas_call(
        body, name=name, grid=(B, len(POOL_WINDOWS)),
        in_specs=[pl.BlockSpec((None, S, POOL_GROUP), lambda b, g: (b, 0, c0 + g)),
                  pl.BlockSpec((None, POOL_GROUP, POOL_GROUP), lambda b, g: (g, 0, 0)),
                  pl.BlockSpec((1, POOL_GROUP), lambda b, g: (0, g)),
                  pl.BlockSpec(memory_space=pl.ANY)],
        out_specs=[pl.BlockSpec((None, S, POOL_GROUP), lambda b, g: (b, 0, g)),
                   pl.BlockSpec((None, S, POOL_GROUP), lambda b, g: (b, 0, y0 + g))],
        out_shape=[jax.ShapeDtypeStruct((B, S, D_POOL), BF16), jax.ShapeDtypeStruct(mixed.shape, BF16)],
        input_output_aliases={3: 1},
        compiler_params=_params(dimension_semantics=("parallel", "parallel")),
    )(z, wp, pscale, mixed)


def _pool_bwd(name, d, wp, pscale, dcat):
    B, S, _ = d.shape
    c0 = D_ATTN // POOL_GROUP

    def body(d_ref, w_ref, s_ref, dy_ref, du_ref, dw_ref, dsc_ref):
        g, b = pl.program_id(0), pl.program_id(1)
        dv = d_ref[...]
        dy = dy_ref[...].astype(F32)
        w = w_ref[...]
        ypre = jnp.dot(dv, w, preferred_element_type=F32)
        dyp = (dy * s_ref[...]).astype(BF16)
        dd = lax.dot_general(dyp, w, NT, preferred_element_type=F32)
        du_ref[...] = (_window_sum(dd / _pool_count(S, g), g, True) - dd).astype(BF16)
        dw = lax.dot_general(dv, dyp, TN, preferred_element_type=F32)
        dsc = jnp.sum(dy * ypre, axis=0, keepdims=True)

        @pl.when(b == 0)
        def _():
            dw_ref[...] = dw
            dsc_ref[...] = dsc

        @pl.when(b > 0)
        def _():
            dw_ref[...] += dw
            dsc_ref[...] += dsc

    blk = pl.BlockSpec((None, S, POOL_GROUP), lambda g, b: (b, 0, g))
    wspec = pl.BlockSpec((None, POOL_GROUP, POOL_GROUP), lambda g, b: (g, 0, 0))
    sspec = pl.BlockSpec((1, POOL_GROUP), lambda g, b: (0, g))
    return pl.pallas_call(
        body, name=name, grid=(len(POOL_WINDOWS), B),
        in_specs=[blk, wspec, sspec, pl.BlockSpec((None, S, POOL_GROUP), lambda g, b: (b, 0, c0 + g))],
        out_specs=[blk, wspec, sspec],
        out_shape=[jax.ShapeDtypeStruct((B, S, D_POOL), BF16),
                   jax.ShapeDtypeStruct((len(POOL_WINDOWS), POOL_GROUP, POOL_GROUP), F32),
                   jax.ShapeDtypeStruct((1, D_POOL), F32)],
        compiler_params=_params(dimension_semantics=("arbitrary", "arbitrary")),
    )(d, wp, pscale, dcat)


def _cross_softmax(q, k):
    s = lax.dot_general(q, k, NT, preferred_element_type=F32) * (CROSS_DIM ** -0.5)
    e = jnp.exp(s - jnp.max(s, axis=-1, keepdims=True))
    return e * (1.0 / jnp.sum(e, axis=-1, keepdims=True))


def _cross_fwd(name, qc, kv, tq=1024):
    B, S, _ = qc.shape
    M = kv.shape[1]
    tq = _tile(S, tq)

    def body(q_ref, k_ref, v_ref, o_ref):
        p = _cross_softmax(q_ref[...], k_ref[...])
        o_ref[...] = jnp.dot(p.astype(BF16), v_ref[...], preferred_element_type=F32).astype(BF16)

    qspec = pl.BlockSpec((None, tq, CROSS_DIM), lambda b, h, i: (b, i, h))
    return pl.pallas_call(
        body, name=name, grid=(B, CROSS_HEADS, S // tq),
        in_specs=[qspec, pl.BlockSpec((None, M, CROSS_DIM), lambda b, h, i: (b, 0, h)),
                  pl.BlockSpec((None, M, CROSS_DIM), lambda b, h, i: (b, 0, CROSS_HEADS + h))],
        out_specs=qspec, out_shape=jax.ShapeDtypeStruct((B, S, D_CROSS), BF16),
        compiler_params=_params(dimension_semantics=("parallel", "parallel", "parallel")),
    )(qc, kv, kv)


def _cross_bwd(name, qc, kv, do, tq=1024):
    B, S, _ = qc.shape
    M = kv.shape[1]
    tq = _tile(S, tq)
    nq = S // tq
    scale = CROSS_DIM ** -0.5

    def body(q_ref, k_ref, v_ref, do_ref, dq_ref, dk_ref, dv_ref, dka, dva):
        i = pl.program_id(2)
        q, k, v, dov = q_ref[...], k_ref[...], v_ref[...], do_ref[...]
        p = _cross_softmax(q, k)
        dp = lax.dot_general(dov, v, NT, preferred_element_type=F32)
        ds = ((p * (dp - jnp.sum(p * dp, axis=-1, keepdims=True))) * scale).astype(BF16)
        dq_ref[...] = jnp.dot(ds, k, preferred_element_type=F32).astype(BF16)
        dk = lax.dot_general(ds, q, TN, preferred_element_type=F32)
        dv = lax.dot_general(p.astype(BF16), dov, TN, preferred_element_type=F32)

        @pl.when(i == 0)
        def _():
            dka[...] = dk
            dva[...] = dv

        @pl.when(i > 0)
        def _():
            dka[...] += dk
            dva[...] += dv

        @pl.when(i == nq - 1)
        def _():
            dk_ref[...] = dka[...].astype(BF16)
            dv_ref[...] = dva[...].astype(BF16)

    qspec = pl.BlockSpec((None, tq, CROSS_DIM), lambda b, h, i: (b, i, h))
    kspec = pl.BlockSpec((None, M, CROSS_DIM), lambda b, h, i: (b, 0, h))
    return pl.pallas_call(
        body, name=name, grid=(B, CROSS_HEADS, nq),
        in_specs=[qspec, kspec, pl.BlockSpec((None, M, CROSS_DIM), lambda b, h, i: (b, 0, CROSS_HEADS + h)), qspec],
        out_specs=[qspec, kspec, kspec],
        out_shape=[jax.ShapeDtypeStruct((B, S, D_CROSS), BF16), jax.ShapeDtypeStruct((B, M, D_CROSS), BF16),
                   jax.ShapeDtypeStruct((B, M, D_CROSS), BF16)],
        scratch_shapes=[pltpu.VMEM((M, CROSS_DIM), F32), pltpu.VMEM((M, CROSS_DIM), F32)],
        compiler_params=_params(dimension_semantics=("parallel", "parallel", "arbitrary")),
    )(qc, kv, kv, do)


def _local_step(x, mem, target, small, weights, emit, start_token=None):
    B, S, D = x.shape
    T = B * S
    x2, t2 = x.reshape(T, D), target.reshape(T, D)
    mem2 = mem.reshape(-1, D)
    n_mem = mem.shape[1]
    wts = {}

    hn1 = _rms_fwd("norm_ffn1", x2, small["ffn1_norm"], after=start_token)
    memn = _rms_fwd("norm_mem", mem2, small["mem_norm"])
    bias = _bias_tile("bias_tile", small["rel_bias"])
    wts.update(weights(0, [hn1, memn, bias]))
    g1 = _ffn_gate("ffn1_gate", hn1, wts["ffn1_w_gate"], N_DEV)
    wts.update(weights(1, g1))
    u1, a1 = _ffn_up_act("ffn1_up", hn1, wts["ffn1_w_up"], g1)
    wts.update(weights(2, a1))
    h1 = _ffn_out("ffn1_down", a1, wts["ffn1_w_down"], res=x2, scale=FFN_RES)
    wts.update(weights(3, h1))
    z, hn2 = _mm_nn_cols("mix_in", h1, wts["w_in"], F32, norm=small["mix_norm"])
    z = z.reshape(B, S, -1)
    mixed = _attn_fwd("attn_fwd", z, bias)
    d_pool, mixed = _pool_fwd("pool_fwd", z, wts["w_pool"], small["pool_scale"], mixed)
    cat = mixed.reshape(T, -1)
    h2 = _mm_nn("mix_out", cat, wts["w_out"], F32, res=h1)
    wts.update(weights(4, h2))
    qc, hn3 = _mm_nn("cross_q", h2, wts["w_cq"], BF16, norm=small["cross_norm"])
    kv = _mm_nn("cross_kv", memn, wts["w_ckv"], BF16)
    o = _cross_fwd("cross_fwd", qc.reshape(B, S, -1), kv.reshape(B, n_mem, -1)).reshape(T, -1)
    h3 = _mm_nn_cols("cross_out", o, wts["w_co"], F32, res=h2)
    wts.update(weights(5, h3))
    hn4, g2, u2, a2 = _ffn_up("ffn2_up", h3, small["ffn2_norm"], wts["ffn2_w_gate"], wts["ffn2_w_up"], N_DEV)
    h4 = _ffn_out("ffn2_down", a2, wts["ffn2_w_down"], res=h3, scale=FFN_RES)

    gs = {}
    loss_part, dh4, gs["final_norm"] = _loss_and_grad("loss", h4, small["final_norm"], t2)

    def ffn_bwd(tag, dhb, h_in, hn, g, u, a, wg, wu, wd, gain, out_dtype=BF16):
        tok = emit({tag + "_w_down": _ffn_dw(tag + "_dwd", a, dhb, scale=FFN_RES)})
        dg, du = _ffn_dact(tag + "_dact", dhb, wd, g, u, after=tok)
        tok = emit({tag + "_w_gate": _ffn_dw(tag + "_dwg", dg, hn)})
        tok = emit({tag + "_w_up": _ffn_dw(tag + "_dwu", du, hn, after=tok)})
        dhn = _ffn_out(tag + "_dhn_g", dg, wg, after=tok)
        dhn = _ffn_out(tag + "_dhn_u", du, wu, res=dhn, out_dtype=BF16)
        return _rms_bwd(tag + "_dnorm", h_in, gain, dhn, skip=dhb, out_dtype=out_dtype)

    dh3b, gs["ffn2_norm"] = ffn_bwd("ffn2", dh4, h3, hn4, g2, u2, a2, wts["ffn2_w_gate"],
                                    wts["ffn2_w_up"], wts["ffn2_w_down"], small["ffn2_norm"])
    do = _mm_nt_cols("cross_do", dh3b, wts["w_co"], BF16, tn=D_CROSS)
    gw = {"w_co": _mm_tn("cross_dwo", o, dh3b, tm=D_CROSS, col_blocks=N_DEV)}
    dqc, dk, dv = _cross_bwd("cross_bwd", qc.reshape(B, S, -1), kv.reshape(B, n_mem, -1), do.reshape(B, S, -1))
    dqc = dqc.reshape(T, -1)
    dkv = jnp.concatenate([dk, dv], axis=-1).reshape(B * n_mem, -1)
    gw["w_cq"] = _mm_tn("cross_dwq", hn3, dqc, tn=D_CROSS)
    gw["w_ckv"] = _mm_tn("cross_dwkv", memn, dkv)
    tok = emit(gw)
    dhn3 = _mm_nt("cross_dhn", dqc, wts["w_cq"], BF16, after=tok)
    dmemn = _mm_nt("cross_dmem", dkv, wts["w_ckv"], F32, tm=512)
    _, gs["mem_norm"] = _rms_bwd("mem_dnorm", mem2, small["mem_norm"], dmemn)
    dh2b, gs["cross_norm"] = _rms_bwd("cross_dnorm", h2, small["cross_norm"], dhn3, skip=dh3b)
    dcat = _mm_nt("mix_dcat", dh2b, wts["w_out"], BF16)
    gw = {"w_out": _mm_tn("mix_dwout", cat, dh2b)}
    dcat3 = dcat.reshape(B, S, -1)
    dq, dkk, dvv, dbias = _attn_bwd("attn_bwd", z, bias, dcat3)
    du, gw["w_pool"], gs["pool_scale"] = _pool_bwd("pool_bwd", d_pool, wts["w_pool"], small["pool_scale"], dcat3)
    gs["rel_bias"] = _rel_grad_from_diagonals(_bias_grad("bias_grad", dbias)[:, 0, :])
    dz = jnp.concatenate([dq, dkk, dvv, du], axis=-1).reshape(T, -1)
    gw["w_in"] = _mm_tn("mix_dwin", hn2, dz, col_blocks=N_DEV)
    tok = emit(gw)
    dhn2 = _mm_nt_cols("mix_dhn", dz, wts["w_in"], BF16, after=tok)
    dh1b, gs["mix_norm"] = _rms_bwd("mix_dnorm", h1, small["mix_norm"], dhn2, skip=dh2b)
    dx, gs["ffn1_norm"] = ffn_bwd("ffn1", dh1b, x2, hn1, g1, u1, a1, wts["ffn1_w_gate"],
                                  wts["ffn1_w_up"], wts["ffn1_w_down"], small["ffn1_norm"], out_dtype=F32)
    return loss_part, dx.reshape(B, S, D), gs


def _position():
    return lax.axis_index("x"), lax.axis_index("y"), lax.axis_index("c")


def _index(p):
    return 4 * p[0] + 2 * p[1] + p[2]


HBM_SPEC = pl.BlockSpec(memory_space=pltpu.HBM)
SEM_SPEC = pl.BlockSpec(memory_space=pltpu.SEMAPHORE)
ANY_SPEC = pl.BlockSpec(memory_space=pl.ANY)
ORDERED_EFFECT = pltpu.SideEffectType.DATAFLOW_SIDE_EFFECTING


N_COPIES = {"grads": N_DEV - 1, "spread": 4, "relay": 3}
N_SEMS = {"grads": N_DEV, "spread": 5, "relay": 3}


def _copies(pattern, srcs, lands, send, recv, base=0):
    x, y, c = _position()
    me, sibling = _index((x, y, c)), (x, y, 1 - c)
    chips = [(1 - x, y), (x, 1 - y), (1 - x, 1 - y)]
    if pattern == "grads":
        targets = [(x ^ (k >> 2), y ^ ((k >> 1) & 1), c ^ (k & 1)) for k in range(1, N_DEV)]
    else:
        targets = [sibling] + [(*chip, c) for chip in chips]
    per, slots, out = N_COPIES[pattern], N_SEMS[pattern], []
    for a in range(len(lands)):
        for k in range(per):
            if pattern == "relay":
                src = dst = lands[a].at[_index((*chips[k], c))]
                to = sibling
            else:
                to = targets[k]
                src = srcs[a].at[_index(to)] if pattern == "grads" else srcs[a]
                dst = lands[a].at[me]
            slot = base + a * slots + k
            out.append(pltpu.make_async_remote_copy(src_ref=src, dst_ref=dst, send_sem=send.at[slot],
                                                    recv_sem=recv.at[slot], device_id=to, device_id_type=MESH))
    return out


def _own_copies(pattern, srcs, lands, send, base=0):
    if pattern == "relay":
        return []
    x, y, c = _position()
    me = _index((x, y, c))
    slots = N_SEMS[pattern]
    return [pltpu.make_async_copy(srcs[a].at[me] if pattern == "grads" else srcs[a], lands[a].at[me],
                                  send.at[base + a * slots + slots - 1]) for a in range(len(lands))]


def _comm(name, srcs, lands, wait=None, start=None, after=None):
    after = [] if after is None else list(after) if isinstance(after, (list, tuple)) else [after]
    ns, nl = len(srcs), len(lands)
    na = ns + nl
    arrays = list(srcs) + list(lands)
    n_wait = 2 if wait else 0
    n_start = 2 if start else 0

    def body(*refs):
        ins, lnd = refs[:ns], refs[ns:na]
        if wait:
            base = wait[3] if len(wait) > 3 else 0
            for cp in _copies(wait[0], ins, lnd, refs[na], refs[na + 1], base):
                cp.wait_send()
                cp.wait_recv()
            for cp in _own_copies(wait[0], ins, lnd, refs[na], base):
                cp.wait()
        if start:
            outs = refs[na + n_wait + len(after):]
            for cp in _copies(start, ins, lnd, outs[0], outs[1]) + _own_copies(start, ins, lnd, outs[0]):
                cp.start()
            refs[-1][...] = jnp.zeros((8, 128), F32)

    out_shape, out_specs = [], []
    if start:
        sems = pltpu.SemaphoreType.DMA((nl * N_SEMS[start],))
        out_shape += [sems, sems]
        out_specs += [SEM_SPEC, SEM_SPEC]
    out_shape += [pltpu.HBM(a.shape, a.dtype) for a in arrays]
    out_specs += [HBM_SPEC] * na
    if start:
        out_shape.append(jax.ShapeDtypeStruct((8, 128), F32))
        out_specs.append(pl.BlockSpec(memory_space=pltpu.VMEM))
    operands = [pltpu.with_memory_space_constraint(a, pltpu.HBM) for a in arrays]
    operands += list(wait[1:3]) if wait else []
    operands += after
    res = pl.pallas_call(
        body, name=name, out_shape=out_shape, out_specs=out_specs,
        in_specs=[HBM_SPEC] * na + [SEM_SPEC] * n_wait + [ANY_SPEC] * len(after),
        input_output_aliases={i: n_start + i for i in range(na)},
        compiler_params=pltpu.CompilerParams(has_side_effects=ORDERED_EFFECT),
    )(*operands)
    res = list(res)
    thru = res[n_start:n_start + na]
    return thru[:ns], thru[ns:], (tuple(res[:2]) if start else None), (res[-1] if start else None)


def _adamw_math(w, g, m, v):
    m = ADAM_B1 * m + (1.0 - ADAM_B1) * g
    v = ADAM_B2 * v + (1.0 - ADAM_B2) * (g * g)
    m_hat = m / (1.0 - ADAM_B1 ** ADAM_STEP)
    v_hat = v / (1.0 - ADAM_B2 ** ADAM_STEP)
    delta = -ADAM_LR * (m_hat / (jnp.sqrt(v_hat) + ADAM_EPS) + ADAM_WD * w)
    return delta, m, v


def _adamw(name, parts, w, m, v):
    R, C = w.shape
    tr = _tile(R, max(16, ADAM_STEP_ELEMS // C))

    def body(p_ref, w_ref, m_ref, v_ref, g_out, d_out, m_out, v_out):
        g = p_ref[0].astype(F32)
        for d in range(1, N_DEV):
            g = g + p_ref[d].astype(F32)
        g_out[...] = g
        d_out[...], m_out[...], v_out[...] = _adamw_math(w_ref[...], g, m_ref[...], v_ref[...])

    row = pl.BlockSpec((tr, C), lambda i: (i, 0))
    out = jax.ShapeDtypeStruct((R, C), F32)
    return pl.pallas_call(
        body, name=name, grid=(R // tr,),
        in_specs=[pl.BlockSpec((N_DEV, tr, C), lambda i: (0, i, 0)), row, row, row],
        out_specs=[row] * 4, out_shape=[out] * 4,
        compiler_params=_params(dimension_semantics=("parallel",)),
    )(parts, w, m, v)


def _small_allreduce_adamw(name, g, w, m, v):
    R = g.shape[0]

    def body(g_ref, w_ref, m_ref, v_ref, g_out, d_out, m_out, v_out, land, send, recv):
        x, y, c = _position()
        me = _index((x, y, c))
        land[me] = g_ref[...]
        copies = []
        for k in range(1, N_DEV):
            peer = (x ^ (k >> 2), y ^ ((k >> 1) & 1), c ^ (k & 1))
            copies.append(pltpu.make_async_remote_copy(
                src_ref=g_ref, dst_ref=land.at[me], send_sem=send.at[k - 1], recv_sem=recv.at[k - 1],
                device_id=peer, device_id_type=MESH))
        for cp in copies:
            cp.start()
        for cp in copies:
            cp.wait()
        total = land[0]
        for d in range(1, N_DEV):
            total = total + land[d]
        g_out[...] = total
        d_out[...], m_out[...], v_out[...] = _adamw_math(w_ref[...], total, m_ref[...], v_ref[...])

    vm = pl.BlockSpec(memory_space=pltpu.VMEM)
    out = jax.ShapeDtypeStruct((R, 128), F32)
    return pl.pallas_call(
        body, name=name, in_specs=[vm] * 4, out_specs=[vm] * 4, out_shape=[out] * 4,
        scratch_shapes=[pltpu.VMEM((N_DEV, R, 128), F32), pltpu.SemaphoreType.DMA((7,)),
                        pltpu.SemaphoreType.DMA((7,))],
    )(g, w, m, v)


BIG = ("ffn1_w_gate", "ffn1_w_up", "ffn1_w_down", "w_in", "w_pool", "w_out", "w_cq", "w_ckv", "w_co",
       "ffn2_w_gate", "ffn2_w_up", "ffn2_w_down")
SMALL = ("ffn1_norm", "mix_norm", "rel_bias", "pool_scale", "cross_norm", "mem_norm", "ffn2_norm", "final_norm")
ORDER = ("ffn1_norm", "ffn1_w_gate", "ffn1_w_up", "ffn1_w_down", "mix_norm", "w_in", "rel_bias", "w_pool",
         "pool_scale", "w_out", "cross_norm", "mem_norm", "w_cq", "w_ckv", "w_co", "ffn2_norm", "ffn2_w_gate",
         "ffn2_w_up", "ffn2_w_down", "final_norm")
TRANSPOSED = ("ffn1_w_gate", "ffn1_w_up", "ffn2_w_gate", "ffn2_w_up")
ROW_SHARDED = TRANSPOSED + ("ffn1_w_down", "ffn2_w_down", "w_out", "w_cq", "w_ckv")
GATHER_GROUPS = (("ffn1_w_gate",), ("ffn1_w_up",), ("ffn1_w_down",), ("w_in", "w_pool", "w_out"),
                 ("w_cq", "w_ckv", "w_co"), ("ffn2_w_gate", "ffn2_w_up", "ffn2_w_down"))
RELAY_BEFORE_USE = ((0,), (1,), (2,), (3,), (4, 5), ())


def _pack(arrays):
    flat = jnp.concatenate([a.reshape(-1) for a in arrays])
    rows = -(-flat.shape[0] // 1024) * 8
    return jnp.pad(flat, (0, rows * 128 - flat.shape[0])).reshape(rows, 128)


def _unpack(packed, like):
    flat, out, at = packed.reshape(-1), [], 0
    for a in like:
        out.append(flat[at:at + a.size].reshape(a.shape))
        at += a.size
    return out


def _shard2d(a):
    a = a[0]
    return a.reshape(-1, a.shape[-1])


def kernel(x, mem, ffn1_norm, ffn1_w_gate, ffn1_w_up, ffn1_w_down, mix_norm, w_in, rel_bias, w_pool, pool_scale, w_out, cross_norm, mem_norm, w_cq, w_ckv, w_co, ffn2_norm, ffn2_w_gate, ffn2_w_up, ffn2_w_down, final_norm, loss_target, m_ffn1_norm, m_ffn1_w_gate, m_ffn1_w_up, m_ffn1_w_down, m_mix_norm, m_w_in, m_rel_bias, m_w_pool, m_pool_scale, m_w_out, m_cross_norm, m_mem_norm, m_w_cq, m_w_ckv, m_w_co, m_ffn2_norm, m_ffn2_w_gate, m_ffn2_w_up, m_ffn2_w_down, m_final_norm, v_ffn1_norm, v_ffn1_w_gate, v_ffn1_w_up, v_ffn1_w_down, v_mix_norm, v_w_in, v_rel_bias, v_w_pool, v_pool_scale, v_w_out, v_cross_norm, v_mem_norm, v_w_cq, v_w_ckv, v_w_co, v_ffn2_norm, v_ffn2_w_gate, v_ffn2_w_up, v_ffn2_w_down, v_final_norm):
    args = dict(locals())
    def view(n, a):
        return a.transpose(0, 2, 1) if n in TRANSPOSED else a

    w_in_ = {n: view(n, args[n]) for n in ORDER}
    m_in = {n: view(n, args["m_" + n]) for n in ORDER}
    v_in = {n: view(n, args["v_" + n]) for n in ORDER}

    n_g, rows = len(POOL_WINDOWS), POOL_GROUP // N_DEV

    def landing(block_shape, dtype):
        return lax.empty((N_DEV,) + tuple(block_shape), dtype)

    gathers, tok = [], None
    for first in (True, False):
        groups = GATHER_GROUPS[:1] if first else GATHER_GROUPS[1:]
        shards = [_shard2d(w_in_[n]) for group in groups for n in group]
        shards = [(s if tok is None else s + tok[0, 0]).astype(BF16) for s in shards]
        srcs, lands, sems, tok = _comm("gather_start_%d" % (not first), shards,
                                       [landing(s.shape, BF16) for s in shards], start="spread", after=tok)
        at = 0
        for group in groups:
            gathers.append((srcs[at:at + len(group)], lands[at:at + len(group)], sems + (at * N_SEMS["spread"],)))
            at += len(group)

    def weights(gi, after):
        for ri in RELAY_BEFORE_USE[gi]:
            srcs, lands, sems = gathers[ri]
            _, lands, sems, after = _comm("gather_relay_%d" % ri, srcs, lands, wait=("spread",) + sems,
                                          start="relay", after=after)
            gathers[ri] = (None, lands, sems)
        _, lands, sems = gathers[gi]
        _, lands, _, _ = _comm("gather_finish_%d" % gi, [], lands, wait=("relay",) + sems, after=after)
        out = {}
        for n, full in zip(GATHER_GROUPS[gi], lands):
            if n == "w_pool":
                full = full.reshape(N_DEV, n_g, rows, POOL_GROUP).transpose(1, 0, 2, 3).reshape(n_g, POOL_GROUP, POOL_GROUP)
            out[n] = full.reshape(-1, full.shape[-1]) if n in ROW_SHARDED else full
        return out

    scatters = []

    def emit(gw):
        names = list(gw)
        stacks = []
        for n in names:
            g = gw[n]
            if n == "w_pool":
                g = g.reshape(n_g, N_DEV, rows, POOL_GROUP).transpose(1, 0, 2, 3).astype(BF16)
            stacks.append(g.reshape((N_DEV,) + _shard2d(w_in_[n]).shape))
        lands = [landing(s.shape[1:], s.dtype) for s in stacks]
        srcs, lands, sems, token = _comm("grads_start_%d" % len(scatters), stacks, lands, start="grads")
        scatters.append((names, srcs, lands, sems))
        return token

    small = {n: w_in_[n].reshape(1, -1) for n in SMALL if n != "rel_bias"}
    small["rel_bias"] = rel_bias[0]
    loss_part, grad_x, gs = _local_step(x, mem, loss_target, small, weights, emit, start_token=tok)

    grad, delta, new_m, new_v = {}, {}, {}, {}
    after = grad_x
    for si, (names, srcs, lands, sems) in enumerate(scatters):
        _, landed, _, _ = _comm("grads_finish_%d" % si, srcs, lands, wait=("grads",) + sems, after=after)
        for n, parts in zip(names, landed):
            res = _adamw("adamw_" + n, parts, _shard2d(w_in_[n]), _shard2d(m_in[n]), _shard2d(v_in[n]))
            grad[n], delta[n], new_m[n], new_v[n] = [view(n, r.reshape(w_in_[n].shape)) for r in res]
        after = res[0]

    slot = jnp.zeros((1,), F32)
    like = [w_in_[n] for n in SMALL] + [slot]
    gs["rel_bias"] = gs["rel_bias"].reshape(rel_bias.shape)
    res = _small_allreduce_adamw("small_params", _pack([gs[n] for n in SMALL] + [loss_part[0, :1]]), _pack(like),
                                 _pack([m_in[n] for n in SMALL] + [slot]), _pack([v_in[n] for n in SMALL] + [slot]))
    for d, packed in zip((grad, delta, new_m, new_v), res):
        for n, a in zip(SMALL, _unpack(packed, like)):
            d[n] = a
    loss = _unpack(res[0], like)[-1][0]
    return (loss, grad_x, *[grad[n] for n in ORDER], *[delta[n] for n in ORDER],
            *[new_m[n] for n in ORDER], *[new_v[n] for n in ORDER])
```

```python
import jax
import jax.numpy as jnp
from jax import lax
from jax.experimental import pallas as pl
from jax.experimental.pallas import tpu as pltpu

F32 = jnp.float32
BF16 = jnp.bfloat16

N_DEV = 8
EPS = 1e-6
NEG_INF = -1e30
CHUNK = 64
LEFT_CHUNKS = 8
PAD = LEFT_CHUNKS * CHUNK
QBLK = 4 * CHUNK
KBAND = PAD + QBLK
REL_CLIP = 128
ATTN_HEADS = 16
HEAD_DIM = 64
D_ATTN = ATTN_HEADS * HEAD_DIM
POOL_WINDOWS = (2, 4, 8, 16)
POOL_GROUP = 256
D_POOL = len(POOL_WINDOWS) * POOL_GROUP
CROSS_HEADS = 4
CROSS_DIM = 128
D_CROSS = CROSS_HEADS * CROSS_DIM
FFN_RES = 0.5
ADAM_LR, ADAM_B1, ADAM_B2, ADAM_EPS, ADAM_WD, ADAM_STEP = 0.001, 0.9, 0.999, 1e-08, 0.01, 10

NN = (((1,), (0,)), ((), ()))
NT = (((1,), (1,)), ((), ()))
TN = (((0,), (0,)), ((), ()))
MESH = pl.DeviceIdType.MESH
VMEM_LIMIT = 56 * 1024 * 1024
ADAM_STEP_ELEMS = 384 * 1024


def _params(**kw):
    return pltpu.CompilerParams(vmem_limit_bytes=VMEM_LIMIT, **kw)


def _bf(v):
    return v if v.dtype == BF16 else v.astype(BF16)


WHOLE = ((Ellipsis,), (Ellipsis,))


def _rms(xv, gain):
    return (xv * lax.rsqrt(jnp.mean(xv * xv, axis=-1, keepdims=True) + EPS)) * gain


def _gemm(name, a, a_spec, b, b_spec, dims, grid, outs, chunks=(WHOLE,), extras=(), epilogue=None, after=None,
          norm=None):
    nex, nout = len(extras), len(outs)
    first_out = 2 + nex + (after is not None) + (norm is not None)

    def body(*refs):
        a_ref, b_ref = refs[:2]
        if norm is not None:
            hn_out, a_ref = refs[first_out + nout], refs[-1]

            @pl.when(pl.program_id(1) == 0)
            def _():
                hn = _rms(refs[0][...], refs[first_out - 1][...]).astype(BF16)
                a_ref[...] = hn
                hn_out[...] = hn

        total = None
        for ia, ib in chunks:
            d = lax.dot_general(_bf(a_ref[ia]), _bf(b_ref[ib]), dims, preferred_element_type=F32)
            total = d if total is None else total + d
        vals = epilogue(total, *[e[...] for e in refs[2:2 + nex]]) if epilogue is not None else (total,)
        for r, v in zip(refs[first_out:first_out + nout], vals):
            r[...] = v.astype(r.dtype)

    operands = [a, b] + [x for x, _, _ in extras]
    in_specs = [pl.BlockSpec(*a_spec), pl.BlockSpec(*b_spec)] + [pl.BlockSpec(blk, m) for _, blk, m in extras]
    if after is not None:
        operands.append(after)
        in_specs.append(pl.BlockSpec(after.shape, lambda i, j: (0, 0)))
    out_specs = [pl.BlockSpec(blk, m) for _, _, blk, m in outs]
    out_shape = [jax.ShapeDtypeStruct(s, d) for s, d, _, _ in outs]
    scratch = []
    if norm is not None:
        operands.append(norm)
        in_specs.append(pl.BlockSpec(norm.shape, lambda i, j: (0, 0)))
        out_specs.append(pl.BlockSpec(*a_spec))
        out_shape.append(jax.ShapeDtypeStruct(a.shape, BF16))
        scratch.append(pltpu.VMEM(a_spec[0], BF16))
    res = pl.pallas_call(
        body, name=name, grid=grid, in_specs=in_specs, out_specs=out_specs, out_shape=out_shape,
        scratch_shapes=scratch,
        compiler_params=_params(dimension_semantics=("parallel", "arbitrary" if norm is not None else "parallel")),
    )(*operands)
    return res[0] if len(res) == 1 else res


def _tile(n, want):
    for t in range(min(n, want), 15, -1):
        if n % t == 0 and t % 16 == 0:
            return t
    return n


def _mm_nn(name, a, b, out_dtype, res=None, tm=1024, tn=1024, norm=None):
    M, K = a.shape
    N = b.shape[1]
    tm, tn = _tile(M, tm), _tile(N, tn)
    extras = [] if res is None else [(res, (tm, tn), lambda i, j: (i, j))]
    epi = None if res is None else (lambda t, r: (r + t,))
    return _gemm(name, a, ((tm, K), lambda i, j: (i, 0)), b, ((K, tn), lambda i, j: (0, j)), NN,
                 (M // tm, N // tn), [((M, N), out_dtype, (tm, tn), lambda i, j: (i, j))], extras=extras, epilogue=epi,
                 norm=norm)


def _mm_nn_cols(name, a, bs, out_dtype, res=None, tm=1024, norm=None):
    M, K = a.shape
    nb, _, w = bs.shape
    tm = _tile(M, tm)
    extras = [] if res is None else [(res, (tm, w), lambda i, j: (i, j))]
    epi = None if res is None else (lambda t, r: (r + t,))
    return _gemm(name, a, ((tm, K), lambda i, j: (i, 0)), bs, ((None, K, w), lambda i, j: (j, 0, 0)), NN,
                 (M // tm, nb), [((M, nb * w), out_dtype, (tm, w), lambda i, j: (i, j))], extras=extras, epilogue=epi,
                 norm=norm)


def _mm_nt(name, a, b, out_dtype, tm=1024, tn=1024, after=None):
    M, K = a.shape
    N = b.shape[0]
    tm, tn = _tile(M, tm), _tile(N, tn)
    return _gemm(name, a, ((tm, K), lambda i, j: (i, 0)), b, ((tn, K), lambda i, j: (j, 0)), NT,
                 (M // tm, N // tn), [((M, N), out_dtype, (tm, tn), lambda i, j: (i, j))], after=after)


def _mm_nt_cols(name, a, bs, out_dtype, tm=1024, tn=512, after=None):
    M = a.shape[0]
    nb, N, w = bs.shape
    tm, tn = _tile(M, tm), _tile(N, tn)
    chunks = [((slice(None), pl.ds(c * w, w)), (c,)) for c in range(nb)]
    return _gemm(name, a, ((tm, nb * w), lambda i, j: (i, 0)), bs, ((nb, tn, w), lambda i, j: (0, j, 0)), NT,
                 (M // tm, N // tn), [((M, N), out_dtype, (tm, tn), lambda i, j: (i, j))], chunks, after=after)


def _mm_tn(name, a, b, tm=1024, tn=1024, col_blocks=None):
    T, Ka = a.shape
    Nb = b.shape[1]
    tm = _tile(Ka, tm)
    if col_blocks is None:
        tn = _tile(Nb, tn)
        out = ((Ka, Nb), BF16, (tm, tn), lambda i, j: (i, j))
    else:
        tn = Nb // col_blocks
        out = ((col_blocks, Ka, tn), BF16, (None, tm, tn), lambda i, j: (j, i, 0))
    return _gemm(name, a, ((T, tm), lambda i, j: (0, i)), b, ((T, tn), lambda i, j: (0, j)), TN,
                 (Ka // tm, Nb // tn), [out])


def _hidden_block(wt):
    return 2 * wt.shape[0] // N_DEV


def _ffn_out(name, a, wt, res=None, scale=1.0, tm=1024, tn=512, after=None, out_dtype=F32):
    M, F = a.shape
    N = wt.shape[1]
    tm, tn = _tile(M, tm), _tile(N, tn)
    extras = [] if res is None else [(res, (tm, tn), lambda i, j: (i, j))]
    epi = None if res is None else (lambda t, r: (r + scale * t,))
    return _gemm(name, a, ((tm, F), lambda i, j: (i, 0)), wt, ((F, tn), lambda i, j: (0, j)), NN,
                 (M // tm, N // tn), [((M, N), out_dtype, (tm, tn), lambda i, j: (i, j))], extras=extras,
                 epilogue=epi, after=after)


def _ffn_dact(name, dhb, wd, g, u, tm=1024, after=None):
    M, K = dhb.shape
    F = wd.shape[0]
    w = _hidden_block(wd)
    tm = _tile(M, tm)
    tr = _tile(tm, 256)
    tokens = [] if after is None else [after]

    def body(dh_ref, wd_ref, g_ref, u_ref, *rest):
        dg_ref, du_ref = rest[-2:]
        pieces = [pl.ds(r * tr, tr) for r in range(tm // tr)]

        def product(rows):
            return lax.dot_general(dh_ref[rows, :], wd_ref[...], NT, preferred_element_type=F32)

        results = []
        dact = product(pieces[0])
        for n, rows in enumerate(pieces):
            ahead = product(pieces[n + 1]) if n + 1 < len(pieces) else None
            results.append(_swiglu_bwd(dact, g_ref[rows, :], u_ref[rows, :]))
            dact = ahead
        for rows, (dg, du) in zip(pieces, results):
            dg_ref[rows, :] = dg.astype(BF16)
            du_ref[rows, :] = du.astype(BF16)

    hid = pl.BlockSpec((tm, w), lambda i, j: (i, j))
    return pl.pallas_call(
        body, name=name, grid=(M // tm, F // w),
        in_specs=[pl.BlockSpec((tm, K), lambda i, j: (i, 0)), pl.BlockSpec((w, K), lambda i, j: (j, 0)), hid, hid]
        + [pl.BlockSpec(t.shape, lambda i, j: (0, 0)) for t in tokens],
        out_specs=[hid, hid], out_shape=[jax.ShapeDtypeStruct((M, F), BF16)] * 2,
        compiler_params=_params(dimension_semantics=("parallel", "parallel")),
    )(dhb, wd, g, u, *tokens)


def _ffn_dw(name, a, b, scale=1.0, tn=512, after=None):
    T, F = a.shape
    N = b.shape[1]
    w = 2 * F // N_DEV
    tn = _tile(N, tn)
    epi = None if scale == 1.0 else (lambda t: (t * scale,))
    return _gemm(name, a, ((T, w), lambda i, j: (0, i)), b, ((T, tn), lambda i, j: (0, j)), TN,
                 (F // w, N // tn), [((F, N), BF16, (w, tn), lambda i, j: (i, j))], epilogue=epi, after=after)


def _ffn_gate(name, hn, wgt, tm=1024):
    M, K = hn.shape
    F = wgt.shape[0]
    w = _hidden_block(wgt)
    tm = _tile(M, tm)
    return _gemm(name, hn, ((tm, K), lambda i, j: (i, 0)), wgt, ((w, K), lambda i, j: (j, 0)), NT,
                 (M // tm, F // w), [((M, F), BF16, (tm, w), lambda i, j: (i, j))])


def _ffn_up_act(name, hn, wut, g, tm=1024):
    M, K = hn.shape
    F = wut.shape[0]
    w = _hidden_block(wut)
    tm = _tile(M, tm)
    hid = ((tm, w), lambda i, j: (i, j))

    def epilogue(u, gate):
        gate = gate.astype(F32)
        return u, gate * jax.nn.sigmoid(gate) * u

    return _gemm(name, hn, ((tm, K), lambda i, j: (i, 0)), wut, ((w, K), lambda i, j: (j, 0)), NT,
                 (M // tm, F // w), [((M, F), BF16) + hid] * 2, extras=[(g,) + hid], epilogue=epilogue)


def _ffn_up(name, h, gain, wgt, wut, tm=512):
    M, K = h.shape
    F = wgt.shape[0]
    w = _hidden_block(wgt)
    tm = _tile(M, tm)

    def body(h_ref, gain_ref, g_ref, u_ref, hn_ref, og, ou, oa, a_ref):
        @pl.when(pl.program_id(1) == 0)
        def _():
            hn = _rms(h_ref[...], gain_ref[...]).astype(BF16)
            a_ref[...] = hn
            hn_ref[...] = hn

        a = a_ref[...]
        g = lax.dot_general(a, g_ref[...], NT, preferred_element_type=F32)
        u = lax.dot_general(a, u_ref[...], NT, preferred_element_type=F32)
        og[...] = g.astype(BF16)
        ou[...] = u.astype(BF16)
        oa[...] = (g * jax.nn.sigmoid(g) * u).astype(BF16)

    rows = pl.BlockSpec((tm, K), lambda i, j: (i, 0))
    wspec = pl.BlockSpec((w, K), lambda i, j: (j, 0))
    ospec = pl.BlockSpec((tm, w), lambda i, j: (i, j))
    return pl.pallas_call(
        body, name=name, grid=(M // tm, F // w),
        in_specs=[rows, pl.BlockSpec((1, K), lambda i, j: (0, 0)), wspec, wspec],
        out_specs=[rows] + [ospec] * 3,
        out_shape=[jax.ShapeDtypeStruct((M, K), BF16)] + [jax.ShapeDtypeStruct((M, F), BF16)] * 3,
        scratch_shapes=[pltpu.VMEM((tm, K), BF16)],
        compiler_params=_params(dimension_semantics=("parallel", "arbitrary")),
    )(h, gain, wgt, wut)


def _swiglu_bwd(dact, g, u):
    g = g.astype(F32)
    u = u.astype(F32)
    sig = jax.nn.sigmoid(g)
    silu = g * sig
    d = FFN_RES * dact
    return d * u * (sig * (1.0 + g * (1.0 - sig))), d * silu


def _rms_fwd(name, x, gain, tr=512, after=None):
    R, D = x.shape
    tr = _tile(R, tr)

    def body(x_ref, g_ref, *rest):
        rest[-1][...] = _rms(x_ref[...], g_ref[...]).astype(BF16)

    tokens = [] if after is None else [after]
    return pl.pallas_call(
        body, name=name, grid=(R // tr,),
        in_specs=[pl.BlockSpec((tr, D), lambda i: (i, 0)), pl.BlockSpec((1, D), lambda i: (0, 0))]
        + [pl.BlockSpec(t.shape, lambda i: (0, 0)) for t in tokens],
        out_specs=pl.BlockSpec((tr, D), lambda i: (i, 0)), out_shape=jax.ShapeDtypeStruct((R, D), BF16),
        compiler_params=_params(dimension_semantics=("parallel",)),
    )(x, gain, *tokens)


def _rms_bwd_math(xv, gain, dy):
    rstd = lax.rsqrt(jnp.mean(xv * xv, axis=-1, keepdims=True) + EPS)
    xhat = xv * rstd
    dxh = dy * gain
    dx = rstd * (dxh - xhat * jnp.mean(dxh * xhat, axis=-1, keepdims=True))
    return dx, jnp.sum(dy * xhat, axis=0, keepdims=True)


def _rms_bwd(name, x, gain, dy, skip=None, out_dtype=BF16, tr=512):
    R, D = x.shape
    tr = _tile(R, tr)
    has_skip = skip is not None

    def body(*refs):
        x_ref, g_ref, dy_ref = refs[:3]
        dx_ref, dg_ref = refs[-2:]
        dx, dg = _rms_bwd_math(x_ref[...], g_ref[...], dy_ref[...].astype(F32))
        if has_skip:
            dx = dx + refs[3][...].astype(F32)
        dx_ref[...] = dx.astype(out_dtype)

        @pl.when(pl.program_id(0) == 0)
        def _():
            dg_ref[...] = dg

        @pl.when(pl.program_id(0) > 0)
        def _():
            dg_ref[...] += dg

    row = pl.BlockSpec((tr, D), lambda i: (i, 0))
    vec = pl.BlockSpec((1, D), lambda i: (0, 0))
    return pl.pallas_call(
        body, name=name, grid=(R // tr,),
        in_specs=[row, vec, row] + ([row] if has_skip else []),
        out_specs=[row, vec],
        out_shape=[jax.ShapeDtypeStruct((R, D), out_dtype), jax.ShapeDtypeStruct((1, D), F32)],
        compiler_params=_params(dimension_semantics=("arbitrary",)),
    )(*([x, gain, dy] + ([skip] if has_skip else [])))


def _loss_and_grad(name, h, gain, target, tr=512):
    R, D = h.shape
    tr = _tile(R, tr)

    def body(h_ref, g_ref, t_ref, loss_ref, dhb_ref, dg_ref):
        hv, gain_v = h_ref[...], g_ref[...]
        y = (hv * lax.rsqrt(jnp.mean(hv * hv, axis=-1, keepdims=True) + EPS)) * gain_v
        err = y - t_ref[...]
        part = jnp.full((8, 128), 0.5 * jnp.sum(jnp.mean(err * err, axis=-1, keepdims=True)), F32)
        dh, dg = _rms_bwd_math(hv, gain_v, err * (1.0 / D))
        dhb_ref[...] = dh.astype(BF16)

        @pl.when(pl.program_id(0) == 0)
        def _():
            dg_ref[...] = dg
            loss_ref[...] = part

        @pl.when(pl.program_id(0) > 0)
        def _():
            dg_ref[...] += dg
            loss_ref[...] += part

    row = pl.BlockSpec((tr, D), lambda i: (i, 0))
    vec = pl.BlockSpec((1, D), lambda i: (0, 0))
    return pl.pallas_call(
        body, name=name, grid=(R // tr,), in_specs=[row, vec, row],
        out_specs=[pl.BlockSpec((8, 128), lambda i: (0, 0)), row, vec],
        out_shape=[jax.ShapeDtypeStruct((8, 128), F32), jax.ShapeDtypeStruct((R, D), BF16),
                   jax.ShapeDtypeStruct((1, D), F32)],
        compiler_params=_params(dimension_semantics=("arbitrary",)),
    )(h, gain, target)


def _bias_tile(name, rel):
    width = KBAND + QBLK
    sat = rel[:, 2 * REL_CLIP:]
    n_left = PAD - REL_CLIP + 1
    row0 = jnp.concatenate([jnp.broadcast_to(sat, (ATTN_HEADS, n_left)), rel[:, :2 * REL_CLIP][:, ::-1],
                            jnp.broadcast_to(sat, (ATTN_HEADS, width - n_left - 2 * REL_CLIP))], axis=1)

    def body(e_ref, o_ref):
        rows = pltpu.roll(jnp.broadcast_to(e_ref[...], (QBLK, width)), 0, 1, stride=1, stride_axis=0)
        i = lax.broadcasted_iota(jnp.int32, (QBLK, KBAND), 0) // CHUNK
        j = lax.broadcasted_iota(jnp.int32, (QBLK, KBAND), 1) // CHUNK
        o_ref[...] = jnp.where((j >= i) & (j <= i + LEFT_CHUNKS), rows[:, :KBAND], NEG_INF)

    return pl.pallas_call(
        body, name=name, grid=(ATTN_HEADS,),
        in_specs=[pl.BlockSpec((None, 1, width), lambda h: (h, 0, 0))],
        out_specs=pl.BlockSpec((None, QBLK, KBAND), lambda h: (h, 0, 0)),
        out_shape=jax.ShapeDtypeStruct((ATTN_HEADS, QBLK, KBAND), F32),
        compiler_params=_params(dimension_semantics=("parallel",)),
    )(row0.reshape(ATTN_HEADS, 1, width))


ATTN_SCALE = HEAD_DIM ** -0.5
ROW_PIECES = 1


def _stack_heads(x, first):
    zero = jnp.zeros_like(x)
    return jnp.concatenate([jnp.where(first, x, zero), jnp.where(first, zero, x)], axis=0)


def _band_softmax(q_half_scaled, kb, bias, left_mask):
    s = lax.dot_general(q_half_scaled, kb, NT, preferred_element_type=F32) + bias + left_mask
    e = jnp.exp(s - jnp.max(s, axis=-1, keepdims=True))
    return e * (1.0 / jnp.sum(e, axis=-1, keepdims=True))


def _left_mask(qb):
    kpos = qb * QBLK - PAD + lax.broadcasted_iota(jnp.int32, (1, KBAND), 1)
    return jnp.where(kpos >= 0, 0.0, NEG_INF).astype(F32)


def _fill_padded(dst, src, S):
    dst[pl.ds(0, PAD), :] = jnp.zeros((PAD, dst.shape[1]), dst.dtype)
    dst[pl.ds(PAD, S), :] = src[...].astype(dst.dtype)


def _attn_fwd(name, z, bias):
    B, S, _ = z.shape
    nh2 = ATTN_HEADS // 2

    def body(q_ref, k_ref, v_ref, b_ref, o_ref, kp, vp):
        qb = pl.program_id(2)

        @pl.when(qb == 0)
        def _():
            _fill_padded(kp, k_ref, S)
            _fill_padded(vp, v_ref, S)

        start = pl.multiple_of(qb * QBLK, QBLK)
        kb, vb = kp[pl.ds(start, KBAND), :], vp[pl.ds(start, KBAND), :]
        q = (q_ref[...] * ATTN_SCALE).astype(BF16)
        first = lax.broadcasted_iota(jnp.int32, (QBLK, 2 * HEAD_DIM), 1) < HEAD_DIM
        left = _left_mask(qb)
        zero = jnp.zeros_like(q)
        qh = [jnp.where(first, q, zero), jnp.where(first, zero, q)]
        rp = QBLK // ROW_PIECES
        chains = [(a, r) for r in range(ROW_PIECES) for a in range(2)]
        ss = [lax.dot_general(qh[a][r * rp:(r + 1) * rp], kb, NT, preferred_element_type=F32) for a, r in chains]
        ps = []
        for (a, r), s in zip(chains, ss):
            s = s + b_ref[a, pl.ds(r * rp, rp), :] + left
            e = jnp.exp(s - jnp.max(s, axis=-1, keepdims=True))
            ps.append((e * (1.0 / jnp.sum(e, axis=-1, keepdims=True))).astype(BF16))
        os_ = [jnp.dot(p, vb, preferred_element_type=F32) for p in ps]
        for r in range(ROW_PIECES):
            o_ref[pl.ds(r * rp, rp), :] = jnp.where(first[:rp], os_[2 * r], os_[2 * r + 1]).astype(BF16)

    return pl.pallas_call(
        body, name=name, grid=(B, nh2, S // QBLK),
        in_specs=[pl.BlockSpec((None, QBLK, 128), lambda b, h, i: (b, i, h)),
                  pl.BlockSpec((None, S, 128), lambda b, h, i: (b, 0, nh2 + h)),
                  pl.BlockSpec((None, S, 128), lambda b, h, i: (b, 0, 2 * nh2 + h)),
                  pl.BlockSpec((2, QBLK, KBAND), lambda b, h, i: (h, 0, 0))],
        out_specs=pl.BlockSpec((None, QBLK, 128), lambda b, h, i: (b, i, h)),
        out_shape=jax.ShapeDtypeStruct((B, S, D_ATTN + D_POOL), BF16),
        scratch_shapes=[pltpu.VMEM((PAD + S, 128), BF16), pltpu.VMEM((PAD + S, 128), BF16)],
        compiler_params=_params(dimension_semantics=("parallel", "parallel", "arbitrary")),
    )(z, z, z, bias)


def _attn_bwd(name, z, bias, dcat):
    B, S, _ = z.shape
    nh2 = ATTN_HEADS // 2
    nqb = S // QBLK

    def body(q_ref, k_ref, v_ref, b_ref, do_ref, dq_ref, dk_ref, dv_ref, db_ref, kp, vp, dka, dva):
        b, qb = pl.program_id(1), pl.program_id(2)

        @pl.when(qb == 0)
        def _():
            _fill_padded(kp, k_ref, S)
            _fill_padded(vp, v_ref, S)
            dka[...] = jnp.zeros_like(dka)
            dva[...] = jnp.zeros_like(dva)

        @pl.when((qb == 0) & (b == 0))
        def _():
            db_ref[...] = jnp.zeros_like(db_ref)

        start = pl.multiple_of(qb * QBLK, QBLK)
        band = pl.ds(start, KBAND)
        kb, vb = kp[band, :], vp[band, :]
        q = (q_ref[...] * ATTN_SCALE).astype(BF16)
        do = do_ref[...]
        first = lax.broadcasted_iota(jnp.int32, (QBLK, 2 * HEAD_DIM), 1) < HEAD_DIM
        left = _left_mask(qb)
        q2, do2 = _stack_heads(q, first), _stack_heads(do, first)
        p = _band_softmax(q2, kb, b_ref[...].reshape(2 * QBLK, KBAND), left)
        dp = lax.dot_general(do2, vb, NT, preferred_element_type=F32)
        ds = p * (dp - jnp.sum(p * dp, axis=-1, keepdims=True))
        db_ref[...] += ds.reshape(2, QBLK, KBAND)
        dsb = ds.astype(BF16)
        dq = jnp.dot(dsb, kb, preferred_element_type=F32)
        dq_ref[...] = (jnp.where(first, dq[:QBLK], dq[QBLK:]) * ATTN_SCALE).astype(BF16)
        dka[:, band] += lax.dot_general(q2, dsb, TN, preferred_element_type=F32)
        dva[:, band] += lax.dot_general(do2, p.astype(BF16), TN, preferred_element_type=F32)

        @pl.when(qb == nqb - 1)
        def _():
            dk_ref[...] = dka[:, pl.ds(PAD, S)].T.astype(BF16)
            dv_ref[...] = dva[:, pl.ds(PAD, S)].T.astype(BF16)

    qspec = pl.BlockSpec((None, QBLK, 128), lambda h, b, i: (b, i, h))
    kvout = pl.BlockSpec((None, S, 128), lambda h, b, i: (b, 0, h))
    bspec = pl.BlockSpec((2, QBLK, KBAND), lambda h, b, i: (h, 0, 0))
    act = jax.ShapeDtypeStruct((B, S, D_ATTN), BF16)
    return pl.pallas_call(
        body, name=name, grid=(nh2, B, nqb),
        in_specs=[qspec,
                  pl.BlockSpec((None, S, 128), lambda h, b, i: (b, 0, nh2 + h)),
                  pl.BlockSpec((None, S, 128), lambda h, b, i: (b, 0, 2 * nh2 + h)),
                  bspec, qspec],
        out_specs=[qspec, kvout, kvout, bspec],
        out_shape=[act, act, act, jax.ShapeDtypeStruct((ATTN_HEADS, QBLK, KBAND), F32)],
        scratch_shapes=[pltpu.VMEM((PAD + S, 128), BF16), pltpu.VMEM((PAD + S, 128), BF16),
                        pltpu.VMEM((128, PAD + S), F32), pltpu.VMEM((128, PAD + S), F32)],
        compiler_params=_params(dimension_semantics=("arbitrary", "arbitrary", "arbitrary")),
    )(z, z, z, bias, dcat)


def _bias_grad(name, dbias):
    width = KBAND + QBLK

    def body(d_ref, o_ref):
        acc = jnp.zeros((1, width), F32)
        for i in range(QBLK):
            row = jnp.concatenate([d_ref[pl.ds(i, 1), :], jnp.zeros((1, QBLK), F32)], axis=1)
            shift = QBLK - 1 - i
            acc = acc + (pltpu.roll(row, shift, 1) if shift else row)
        o_ref[...] = acc

    return pl.pallas_call(
        body, name=name, grid=(ATTN_HEADS,),
        in_specs=[pl.BlockSpec((None, QBLK, KBAND), lambda h: (h, 0, 0))],
        out_specs=pl.BlockSpec((None, 1, width), lambda h: (h, 0, 0)),
        out_shape=jax.ShapeDtypeStruct((ATTN_HEADS, 1, width), F32),
        compiler_params=_params(dimension_semantics=("parallel",)),
    )(dbias)


def _rel_grad_from_diagonals(diag):
    top = PAD + QBLK - 1 - REL_CLIP
    sat = jnp.sum(diag[:, :top + 1], axis=1, keepdims=True)
    mid = diag[:, top + 1:top + 2 * REL_CLIP][:, ::-1]
    return jnp.concatenate([jnp.zeros_like(sat), mid, sat], axis=1)


def _shift_rows(x, k, forward):
    S = x.shape[0]
    t = lax.broadcasted_iota(jnp.int32, x.shape, 0)
    if forward:
        return jnp.where(t < S - k, pltpu.roll(x, S - k, 0), 0.0)
    return jnp.where(t >= k, pltpu.roll(x, k, 0), 0.0)


def _window_sum(x, g, forward):
    s = x + _shift_rows(x, 1, forward)
    out = s
    for n, k in enumerate((2, 4, 8)):
        s = s + _shift_rows(s, k, forward)
        out = jnp.where(g > n, s, out)
    return out


def _pool_count(S, g):
    t = lax.broadcasted_iota(jnp.int32, (S, 1), 0)
    w = jnp.left_shift(2, g)
    return jnp.minimum(t + 1, w).astype(F32)


def _pool_fwd(name, z, wp, pscale, mixed):
    B, S, _ = z.shape
    c0 = 3 * D_ATTN // POOL_GROUP
    y0 = D_ATTN // POOL_GROUP

    def body(u_ref, w_ref, s_ref, mixed_ref, d_ref, y_ref):
        g = pl.program_id(1)
        u = u_ref[...]
        d = (_window_sum(u, g, False) / _pool_count(S, g) - u).astype(BF16)
        d_ref[...] = d
        y_ref[...] = (jnp.dot(d, w_ref[...], preferred_element_type=F32) * s_ref[...]).astype(BF16)

    return pl.pallas_call(
        body, name=name, grid=(B, len(POOL_WINDOWS)),
        in_specs=[pl.BlockSpec((None, S, POOL_GROUP), lambda b, g: (b, 0, c0 + g)),
                  pl.BlockSpec((None, POOL_GROUP, POOL_GROUP), lambda b, g: (g, 0, 0)),
                  pl.BlockSpec((1, POOL_GROUP), lambda b, g: (0, g)),
                  pl.BlockSpec(memory_space=pl.ANY)],
        out_specs=[pl.BlockSpec((None, S, POOL_GROUP), lambda b, g: (b, 0, g)),
                   pl.BlockSpec((None, S, POOL_GROUP), lambda b, g: (b, 0, y0 + g))],
        out_shape=[jax.ShapeDtypeStruct((B, S, D_POOL), BF16), jax.ShapeDtypeStruct(mixed.shape, BF16)],
        input_output_aliases={3: 1},
        compiler_params=_params(dimension_semantics=("parallel", "parallel")),
    )(z, wp, pscale, mixed)


def _pool_bwd(name, d, wp, pscale, dcat):
    B, S, _ = d.shape
    c0 = D_ATTN // POOL_GROUP

    def body(d_ref, w_ref, s_ref, dy_ref, du_ref, dw_ref, dsc_ref):
        g, b = pl.program_id(0), pl.program_id(1)
        dv = d_ref[...]
        dy = dy_ref[...].astype(F32)
        w = w_ref[...]
        ypre = jnp.dot(dv, w, preferred_element_type=F32)
        dyp = (dy * s_ref[...]).astype(BF16)
        dd = lax.dot_general(dyp, w, NT, preferred_element_type=F32)
        du_ref[...] = (_window_sum(dd / _pool_count(S, g), g, True) - dd).astype(BF16)
        dw = lax.dot_general(dv, dyp, TN, preferred_element_type=F32)
        dsc = jnp.sum(dy * ypre, axis=0, keepdims=True)

        @pl.when(b == 0)
        def _():
            dw_ref[...] = dw
            dsc_ref[...] = dsc

        @pl.when(b > 0)
        def _():
            dw_ref[...] += dw
            dsc_ref[...] += dsc

    blk = pl.BlockSpec((None, S, POOL_GROUP), lambda g, b: (b, 0, g))
    wspec = pl.BlockSpec((None, POOL_GROUP, POOL_GROUP), lambda g, b: (g, 0, 0))
    sspec = pl.BlockSpec((1, POOL_GROUP), lambda g, b: (0, g))
    return pl.pallas_call(
        body, name=name, grid=(len(POOL_WINDOWS), B),
        in_specs=[blk, wspec, sspec, pl.BlockSpec((None, S, POOL_GROUP), lambda g, b: (b, 0, c0 + g))],
        out_specs=[blk, wspec, sspec],
        out_shape=[jax.ShapeDtypeStruct((B, S, D_POOL), BF16),
                   jax.ShapeDtypeStruct((len(POOL_WINDOWS), POOL_GROUP, POOL_GROUP), F32),
                   jax.ShapeDtypeStruct((1, D_POOL), F32)],
        compiler_params=_params(dimension_semantics=("arbitrary", "arbitrary")),
    )(d, wp, pscale, dcat)


def _cross_softmax(q, k):
    s = lax.dot_general(q, k, NT, preferred_element_type=F32) * (CROSS_DIM ** -0.5)
    e = jnp.exp(s - jnp.max(s, axis=-1, keepdims=True))
    return e * (1.0 / jnp.sum(e, axis=-1, keepdims=True))


def _cross_fwd(name, qc, kv, tq=1024):
    B, S, _ = qc.shape
    M = kv.shape[1]
    tq = _tile(S, tq)

    def body(q_ref, k_ref, v_ref, o_ref):
        p = _cross_softmax(q_ref[...], k_ref[...])
        o_ref[...] = jnp.dot(p.astype(BF16), v_ref[...], preferred_element_type=F32).astype(BF16)

    qspec = pl.BlockSpec((None, tq, CROSS_DIM), lambda b, h, i: (b, i, h))
    return pl.pallas_call(
        body, name=name, grid=(B, CROSS_HEADS, S // tq),
        in_specs=[qspec, pl.BlockSpec((None, M, CROSS_DIM), lambda b, h, i: (b, 0, h)),
                  pl.BlockSpec((None, M, CROSS_DIM), lambda b, h, i: (b, 0, CROSS_HEADS + h))],
        out_specs=qspec, out_shape=jax.ShapeDtypeStruct((B, S, D_CROSS), BF16),
        compiler_params=_params(dimension_semantics=("parallel", "parallel", "parallel")),
    )(qc, kv, kv)


def _cross_bwd(name, qc, kv, do, tq=1024):
    B, S, _ = qc.shape
    M = kv.shape[1]
    tq = _tile(S, tq)
    nq = S // tq
    scale = CROSS_DIM ** -0.5

    def body(q_ref, k_ref, v_ref, do_ref, dq_ref, dk_ref, dv_ref, dka, dva):
        i = pl.program_id(2)
        q, k, v, dov = q_ref[...], k_ref[...], v_ref[...], do_ref[...]
        p = _cross_softmax(q, k)
        dp = lax.dot_general(dov, v, NT, preferred_element_type=F32)
        ds = ((p * (dp - jnp.sum(p * dp, axis=-1, keepdims=True))) * scale).astype(BF16)
        dq_ref[...] = jnp.dot(ds, k, preferred_element_type=F32).astype(BF16)
        dk = lax.dot_general(ds, q, TN, preferred_element_type=F32)
        dv = lax.dot_general(p.astype(BF16), dov, TN, preferred_element_type=F32)

        @pl.when(i == 0)
        def _():
            dka[...] = dk
            dva[...] = dv

        @pl.when(i > 0)
        def _():
            dka[...] += dk
            dva[...] += dv

        @pl.when(i == nq - 1)
        def _():
            dk_ref[...] = dka[...].astype(BF16)
            dv_ref[...] = dva[...].astype(BF16)

    qspec = pl.BlockSpec((None, tq, CROSS_DIM), lambda b, h, i: (b, i, h))
    kspec = pl.BlockSpec((None, M, CROSS_DIM), lambda b, h, i: (b, 0, h))
    return pl.pallas_call(
        body, name=name, grid=(B, CROSS_HEADS, nq),
        in_specs=[qspec, kspec, pl.BlockSpec((None, M, CROSS_DIM), lambda b, h, i: (b, 0, CROSS_HEADS + h)), qspec],
        out_specs=[qspec, kspec, kspec],
        out_shape=[jax.ShapeDtypeStruct((B, S, D_CROSS), BF16), jax.ShapeDtypeStruct((B, M, D_CROSS), BF16),
                   jax.ShapeDtypeStruct((B, M, D_CROSS), BF16)],
        scratch_shapes=[pltpu.VMEM((M, CROSS_DIM), F32), pltpu.VMEM((M, CROSS_DIM), F32)],
        compiler_params=_params(dimension_semantics=("parallel", "parallel", "arbitrary")),
    )(qc, kv, kv, do)


def _local_step(x, mem, target, small, weights, emit, start_token=None):
    B, S, D = x.shape
    T = B * S
    x2, t2 = x.reshape(T, D), target.reshape(T, D)
    mem2 = mem.reshape(-1, D)
    n_mem = mem.shape[1]
    wts = {}

    hn1 = _rms_fwd("norm_ffn1", x2, small["ffn1_norm"], after=start_token)
    memn = _rms_fwd("norm_mem", mem2, small["mem_norm"])
    bias = _bias_tile("bias_tile", small["rel_bias"])
    wts.update(weights(0, [hn1, memn, bias]))
    g1 = _ffn_gate("ffn1_gate", hn1, wts["ffn1_w_gate"])
    wts.update(weights(1, g1))
    u1, a1 = _ffn_up_act("ffn1_up", hn1, wts["ffn1_w_up"], g1)
    wts.update(weights(2, a1))
    h1 = _ffn_out("ffn1_down", a1, wts["ffn1_w_down"], res=x2, scale=FFN_RES)
    wts.update(weights(3, h1))
    z, hn2 = _mm_nn_cols("mix_in", h1, wts["w_in"], F32, norm=small["mix_norm"])
    z = z.reshape(B, S, -1)
    mixed = _attn_fwd("attn_fwd", z, bias)
    d_pool, mixed = _pool_fwd("pool_fwd", z, wts["w_pool"], small["pool_scale"], mixed)
    cat = mixed.reshape(T, -1)
    h2 = _mm_nn("mix_out", cat, wts["w_out"], F32, res=h1)
    wts.update(weights(4, h2))
    qc, hn3 = _mm_nn("cross_q", h2, wts["w_cq"], BF16, norm=small["cross_norm"])
    kv = _mm_nn("cross_kv", memn, wts["w_ckv"], BF16)
    o = _cross_fwd("cross_fwd", qc.reshape(B, S, -1), kv.reshape(B, n_mem, -1)).reshape(T, -1)
    h3 = _mm_nn_cols("cross_out", o, wts["w_co"], F32, res=h2)
    wts.update(weights(5, h3))
    hn4, g2, u2, a2 = _ffn_up("ffn2_up", h3, small["ffn2_norm"], wts["ffn2_w_gate"], wts["ffn2_w_up"])
    h4 = _ffn_out("ffn2_down", a2, wts["ffn2_w_down"], res=h3, scale=FFN_RES)

    gs = {}
    loss_part, dh4, gs["final_norm"] = _loss_and_grad("loss", h4, small["final_norm"], t2)

    def ffn_bwd(tag, dhb, h_in, hn, g, u, a, wg, wu, wd, gain, out_dtype=BF16):
        tok = emit({tag + "_w_down": _ffn_dw(tag + "_dwd", a, dhb, scale=FFN_RES)})
        dg, du = _ffn_dact(tag + "_dact", dhb, wd, g, u, after=tok)
        tok = emit({tag + "_w_gate": _ffn_dw(tag + "_dwg", dg, hn)})
        tok = emit({tag + "_w_up": _ffn_dw(tag + "_dwu", du, hn, after=tok)})
        dhn = _ffn_out(tag + "_dhn_g", dg, wg, after=tok)
        dhn = _ffn_out(tag + "_dhn_u", du, wu, res=dhn, out_dtype=BF16)
        return _rms_bwd(tag + "_dnorm", h_in, gain, dhn, skip=dhb, out_dtype=out_dtype)

    dh3b, gs["ffn2_norm"] = ffn_bwd("ffn2", dh4, h3, hn4, g2, u2, a2, wts["ffn2_w_gate"],
                                    wts["ffn2_w_up"], wts["ffn2_w_down"], small["ffn2_norm"])
    do = _mm_nt_cols("cross_do", dh3b, wts["w_co"], BF16, tn=D_CROSS)
    gw = {"w_co": _mm_tn("cross_dwo", o, dh3b, tm=D_CROSS, col_blocks=N_DEV)}
    dqc, dk, dv = _cross_bwd("cross_bwd", qc.reshape(B, S, -1), kv.reshape(B, n_mem, -1), do.reshape(B, S, -1))
    dqc = dqc.reshape(T, -1)
    dkv = jnp.concatenate([dk, dv], axis=-1).reshape(B * n_mem, -1)
    gw["w_cq"] = _mm_tn("cross_dwq", hn3, dqc, tn=D_CROSS)
    gw["w_ckv"] = _mm_tn("cross_dwkv", memn, dkv)
    tok = emit(gw)
    dhn3 = _mm_nt("cross_dhn", dqc, wts["w_cq"], BF16, after=tok)
    dmemn = _mm_nt("cross_dmem", dkv, wts["w_ckv"], F32, tm=512)
    _, gs["mem_norm"] = _rms_bwd("mem_dnorm", mem2, small["mem_norm"], dmemn)
    dh2b, gs["cross_norm"] = _rms_bwd("cross_dnorm", h2, small["cross_norm"], dhn3, skip=dh3b)
    dcat = _mm_nt("mix_dcat", dh2b, wts["w_out"], BF16)
    gw = {"w_out": _mm_tn("mix_dwout", cat, dh2b)}
    dcat3 = dcat.reshape(B, S, -1)
    dq, dkk, dvv, dbias = _attn_bwd("attn_bwd", z, bias, dcat3)
    du, gw["w_pool"], gs["pool_scale"] = _pool_bwd("pool_bwd", d_pool, wts["w_pool"], small["pool_scale"], dcat3)
    gs["rel_bias"] = _rel_grad_from_diagonals(_bias_grad("bias_grad", dbias)[:, 0, :])
    dz = jnp.concatenate([dq, dkk, dvv, du], axis=-1).reshape(T, -1)
    gw["w_in"] = _mm_tn("mix_dwin", hn2, dz, col_blocks=N_DEV)
    tok = emit(gw)
    dhn2 = _mm_nt_cols("mix_dhn", dz, wts["w_in"], BF16, after=tok)
    dh1b, gs["mix_norm"] = _rms_bwd("mix_dnorm", h1, small["mix_norm"], dhn2, skip=dh2b)
    dx, gs["ffn1_norm"] = ffn_bwd("ffn1", dh1b, x2, hn1, g1, u1, a1, wts["ffn1_w_gate"],
                                  wts["ffn1_w_up"], wts["ffn1_w_down"], small["ffn1_norm"], out_dtype=F32)
    return loss_part, dx.reshape(B, S, D), gs


def _position():
    return lax.axis_index("x"), lax.axis_index("y"), lax.axis_index("c")


def _index(p):
    return 4 * p[0] + 2 * p[1] + p[2]


HBM_SPEC = pl.BlockSpec(memory_space=pltpu.HBM)
SEM_SPEC = pl.BlockSpec(memory_space=pltpu.SEMAPHORE)
ANY_SPEC = pl.BlockSpec(memory_space=pl.ANY)
ORDERED_EFFECT = pltpu.SideEffectType.DATAFLOW_SIDE_EFFECTING


N_COPIES = {"grads": N_DEV - 1, "spread": 4, "relay": 3}
N_SEMS = {"grads": N_DEV, "spread": 5, "relay": 3}


def _copies(pattern, srcs, lands, send, recv, base=0):
    x, y, c = _position()
    me, sibling = _index((x, y, c)), (x, y, 1 - c)
    chips = [(1 - x, y), (x, 1 - y), (1 - x, 1 - y)]
    if pattern == "grads":
        targets = [(x ^ (k >> 2), y ^ ((k >> 1) & 1), c ^ (k & 1)) for k in range(1, N_DEV)]
    else:
        targets = [sibling] + [(*chip, c) for chip in chips]
    per, slots, out = N_COPIES[pattern], N_SEMS[pattern], []
    for a in range(len(lands)):
        for k in range(per):
            if pattern == "relay":
                src = dst = lands[a].at[_index((*chips[k], c))]
                to = sibling
            else:
                to = targets[k]
                src = srcs[a].at[_index(to)] if pattern == "grads" else srcs[a]
                dst = lands[a].at[me]
            slot = base + a * slots + k
            out.append(pltpu.make_async_remote_copy(src_ref=src, dst_ref=dst, send_sem=send.at[slot],
                                                    recv_sem=recv.at[slot], device_id=to, device_id_type=MESH))
    return out


def _own_copies(pattern, srcs, lands, send, base=0):
    if pattern == "relay":
        return []
    x, y, c = _position()
    me = _index((x, y, c))
    slots = N_SEMS[pattern]
    return [pltpu.make_async_copy(srcs[a].at[me] if pattern == "grads" else srcs[a], lands[a].at[me],
                                  send.at[base + a * slots + slots - 1]) for a in range(len(lands))]


def _comm(name, srcs, lands, wait=None, start=None, after=None):
    after = [] if after is None else list(after) if isinstance(after, (list, tuple)) else [after]
    ns, nl = len(srcs), len(lands)
    na = ns + nl
    arrays = list(srcs) + list(lands)
    n_wait = 2 if wait else 0
    n_start = 2 if start else 0

    def body(*refs):
        ins, lnd = refs[:ns], refs[ns:na]
        if wait:
            base = wait[3] if len(wait) > 3 else 0
            for cp in _copies(wait[0], ins, lnd, refs[na], refs[na + 1], base):
                cp.wait_send()
                cp.wait_recv()
            for cp in _own_copies(wait[0], ins, lnd, refs[na], base):
                cp.wait()
        if start:
            outs = refs[na + n_wait + len(after):]
            for cp in _copies(start, ins, lnd, outs[0], outs[1]) + _own_copies(start, ins, lnd, outs[0]):
                cp.start()
            refs[-1][...] = jnp.zeros((8, 128), F32)

    out_shape, out_specs = [], []
    if start:
        sems = pltpu.SemaphoreType.DMA((nl * N_SEMS[start],))
        out_shape += [sems, sems]
        out_specs += [SEM_SPEC, SEM_SPEC]
    out_shape += [pltpu.HBM(a.shape, a.dtype) for a in arrays]
    out_specs += [HBM_SPEC] * na
    if start:
        out_shape.append(jax.ShapeDtypeStruct((8, 128), F32))
        out_specs.append(pl.BlockSpec(memory_space=pltpu.VMEM))
    operands = [pltpu.with_memory_space_constraint(a, pltpu.HBM) for a in arrays]
    operands += list(wait[1:3]) if wait else []
    operands += after
    res = pl.pallas_call(
        body, name=name, out_shape=out_shape, out_specs=out_specs,
        in_specs=[HBM_SPEC] * na + [SEM_SPEC] * n_wait + [ANY_SPEC] * len(after),
        input_output_aliases={i: n_start + i for i in range(na)},
        compiler_params=pltpu.CompilerParams(has_side_effects=ORDERED_EFFECT),
    )(*operands)
    res = list(res)
    thru = res[n_start:n_start + na]
    return thru[:ns], thru[ns:], (tuple(res[:2]) if start else None), (res[-1] if start else None)


def _adamw_math(w, g, m, v):
    m = ADAM_B1 * m + (1.0 - ADAM_B1) * g
    v = ADAM_B2 * v + (1.0 - ADAM_B2) * (g * g)
    m_hat = m / (1.0 - ADAM_B1 ** ADAM_STEP)
    v_hat = v / (1.0 - ADAM_B2 ** ADAM_STEP)
    delta = -ADAM_LR * (m_hat / (jnp.sqrt(v_hat) + ADAM_EPS) + ADAM_WD * w)
    return delta, m, v


def _adamw(name, parts, w, m, v):
    R, C = w.shape
    tr = _tile(R, max(16, ADAM_STEP_ELEMS // C))

    def body(p_ref, w_ref, m_ref, v_ref, g_out, d_out, m_out, v_out):
        g = p_ref[0].astype(F32)
        for d in range(1, N_DEV):
            g = g + p_ref[d].astype(F32)
        g_out[...] = g
        d_out[...], m_out[...], v_out[...] = _adamw_math(w_ref[...], g, m_ref[...], v_ref[...])

    row = pl.BlockSpec((tr, C), lambda i: (i, 0))
    out = jax.ShapeDtypeStruct((R, C), F32)
    return pl.pallas_call(
        body, name=name, grid=(R // tr,),
        in_specs=[pl.BlockSpec((N_DEV, tr, C), lambda i: (0, i, 0)), row, row, row],
        out_specs=[row] * 4, out_shape=[out] * 4,
        compiler_params=_params(dimension_semantics=("parallel",)),
    )(parts, w, m, v)


def _small_allreduce_adamw(name, g, w, m, v):
    R = g.shape[0]

    def body(g_ref, w_ref, m_ref, v_ref, g_out, d_out, m_out, v_out, land, send, recv):
        x, y, c = _position()
        me = _index((x, y, c))
        land[me] = g_ref[...]
        copies = []
        for k in range(1, N_DEV):
            peer = (x ^ (k >> 2), y ^ ((k >> 1) & 1), c ^ (k & 1))
            copies.append(pltpu.make_async_remote_copy(
                src_ref=g_ref, dst_ref=land.at[me], send_sem=send.at[k - 1], recv_sem=recv.at[k - 1],
                device_id=peer, device_id_type=MESH))
        for cp in copies:
            cp.start()
        for cp in copies:
            cp.wait()
        total = land[0]
        for d in range(1, N_DEV):
            total = total + land[d]
        g_out[...] = total
        d_out[...], m_out[...], v_out[...] = _adamw_math(w_ref[...], total, m_ref[...], v_ref[...])

    vm = pl.BlockSpec(memory_space=pltpu.VMEM)
    out = jax.ShapeDtypeStruct((R, 128), F32)
    return pl.pallas_call(
        body, name=name, in_specs=[vm] * 4, out_specs=[vm] * 4, out_shape=[out] * 4,
        scratch_shapes=[pltpu.VMEM((N_DEV, R, 128), F32), pltpu.SemaphoreType.DMA((7,)),
                        pltpu.SemaphoreType.DMA((7,))],
    )(g, w, m, v)


BIG = ("ffn1_w_gate", "ffn1_w_up", "ffn1_w_down", "w_in", "w_pool", "w_out", "w_cq", "w_ckv", "w_co",
       "ffn2_w_gate", "ffn2_w_up", "ffn2_w_down")
SMALL = ("ffn1_norm", "mix_norm", "rel_bias", "pool_scale", "cross_norm", "mem_norm", "ffn2_norm", "final_norm")
ORDER = ("ffn1_norm", "ffn1_w_gate", "ffn1_w_up", "ffn1_w_down", "mix_norm", "w_in", "rel_bias", "w_pool",
         "pool_scale", "w_out", "cross_norm", "mem_norm", "w_cq", "w_ckv", "w_co", "ffn2_norm", "ffn2_w_gate",
         "ffn2_w_up", "ffn2_w_down", "final_norm")
TRANSPOSED = ("ffn1_w_gate", "ffn1_w_up", "ffn2_w_gate", "ffn2_w_up")
ROW_SHARDED = TRANSPOSED + ("ffn1_w_down", "ffn2_w_down", "w_out", "w_cq", "w_ckv")
GATHER_GROUPS = (("ffn1_w_gate",), ("ffn1_w_up",), ("ffn1_w_down",), ("w_in", "w_pool", "w_out"),
                 ("w_cq", "w_ckv", "w_co"), ("ffn2_w_gate", "ffn2_w_up", "ffn2_w_down"))
RELAY_BEFORE_USE = ((0,), (1,), (2,), (3,), (4, 5), ())


def _pack(arrays):
    flat = jnp.concatenate([a.reshape(-1) for a in arrays])
    rows = -(-flat.shape[0] // 1024) * 8
    return jnp.pad(flat, (0, rows * 128 - flat.shape[0])).reshape(rows, 128)


def _unpack(packed, like):
    flat, out, at = packed.reshape(-1), [], 0
    for a in like:
        out.append(flat[at:at + a.size].reshape(a.shape))
        at += a.size
    return out


def _shard2d(a):
    a = a[0]
    return a.reshape(-1, a.shape[-1])


def kernel(x, mem, ffn1_norm, ffn1_w_gate, ffn1_w_up, ffn1_w_down, mix_norm, w_in, rel_bias, w_pool, pool_scale, w_out, cross_norm, mem_norm, w_cq, w_ckv, w_co, ffn2_norm, ffn2_w_gate, ffn2_w_up, ffn2_w_down, final_norm, loss_target, m_ffn1_norm, m_ffn1_w_gate, m_ffn1_w_up, m_ffn1_w_down, m_mix_norm, m_w_in, m_rel_bias, m_w_pool, m_pool_scale, m_w_out, m_cross_norm, m_mem_norm, m_w_cq, m_w_ckv, m_w_co, m_ffn2_norm, m_ffn2_w_gate, m_ffn2_w_up, m_ffn2_w_down, m_final_norm, v_ffn1_norm, v_ffn1_w_gate, v_ffn1_w_up, v_ffn1_w_down, v_mix_norm, v_w_in, v_rel_bias, v_w_pool, v_pool_scale, v_w_out, v_cross_norm, v_mem_norm, v_w_cq, v_w_ckv, v_w_co, v_ffn2_norm, v_ffn2_w_gate, v_ffn2_w_up, v_ffn2_w_down, v_final_norm):
    args = dict(locals())
    def view(n, a):
        return a.transpose(0, 2, 1) if n in TRANSPOSED else a

    w_in_ = {n: view(n, args[n]) for n in ORDER}
    m_in = {n: view(n, args["m_" + n]) for n in ORDER}
    v_in = {n: view(n, args["v_" + n]) for n in ORDER}

    n_g, rows = len(POOL_WINDOWS), POOL_GROUP // N_DEV

    def landing(block_shape, dtype):
        return lax.empty((N_DEV,) + tuple(block_shape), dtype)

    gathers, tok = [], None
    for first in (True, False):
        groups = GATHER_GROUPS[:1] if first else GATHER_GROUPS[1:]
        shards = [_shard2d(w_in_[n]) for group in groups for n in group]
        shards = [(s if tok is None else s + tok[0, 0]).astype(BF16) for s in shards]
        srcs, lands, sems, tok = _comm("gather_start_%d" % (not first), shards,
                                       [landing(s.shape, BF16) for s in shards], start="spread", after=tok)
        at = 0
        for group in groups:
            gathers.append((srcs[at:at + len(group)], lands[at:at + len(group)], sems + (at * N_SEMS["spread"],)))
            at += len(group)

    def weights(gi, after):
        for ri in RELAY_BEFORE_USE[gi]:
            srcs, lands, sems = gathers[ri]
            _, lands, sems, after = _comm("gather_relay_%d" % ri, srcs, lands, wait=("spread",) + sems,
                                          start="relay", after=after)
            gathers[ri] = (None, lands, sems)
        _, lands, sems = gathers[gi]
        _, lands, _, _ = _comm("gather_finish_%d" % gi, [], lands, wait=("relay",) + sems, after=after)
        out = {}
        for n, full in zip(GATHER_GROUPS[gi], lands):
            if n == "w_pool":
                full = full.reshape(N_DEV, n_g, rows, POOL_GROUP).transpose(1, 0, 2, 3).reshape(n_g, POOL_GROUP, POOL_GROUP)
            out[n] = full.reshape(-1, full.shape[-1]) if n in ROW_SHARDED else full
        return out

    scatters = []

    def emit(gw):
        names = list(gw)
        stacks = []
        for n in names:
            g = gw[n]
            if n == "w_pool":
                g = g.reshape(n_g, N_DEV, rows, POOL_GROUP).transpose(1, 0, 2, 3).astype(BF16)
            stacks.append(g.reshape((N_DEV,) + _shard2d(w_in_[n]).shape))
        lands = [landing(s.shape[1:], s.dtype) for s in stacks]
        srcs, lands, sems, token = _comm("grads_start_%d" % len(scatters), stacks, lands, start="grads")
        scatters.append((names, srcs, lands, sems))
        return token

    small = {n: w_in_[n].reshape(1, -1) for n in SMALL if n != "rel_bias"}
    small["rel_bias"] = rel_bias[0]
    loss_part, grad_x, gs = _local_step(x, mem, loss_target, small, weights, emit, start_token=tok)

    grad, delta, new_m, new_v = {}, {}, {}, {}
    after = grad_x
    for si, (names, srcs, lands, sems) in enumerate(scatters):
        _, landed, _, _ = _comm("grads_finish_%d" % si, srcs, lands, wait=("grads",) + sems, after=after)
        for n, parts in zip(names, landed):
            res = _adamw("adamw_" + n, parts, _shard2d(w_in_[n]), _shard2d(m_in[n]), _shard2d(v_in[n]))
            grad[n], delta[n], new_m[n], new_v[n] = [view(n, r.reshape(w_in_[n].shape)) for r in res]
        after = res[0]

    slot = jnp.zeros((1,), F32)
    like = [w_in_[n] for n in SMALL] + [slot]
    gs["rel_bias"] = gs["rel_bias"].reshape(rel_bias.shape)
    res = _small_allreduce_adamw("small_params", _pack([gs[n] for n in SMALL] + [loss_part[0, :1]]), _pack(like),
                                 _pack([m_in[n] for n in SMALL] + [slot]), _pack([v_in[n] for n in SMALL] + [slot]))
    for d, packed in zip((grad, delta, new_m, new_v), res):
        for n, a in zip(SMALL, _unpack(packed, like)):
            d[n] = a
    loss = _unpack(res[0], like)[-1][0]
    return (loss, grad_x, *[grad[n] for n in ORDER], *[delta[n] for n in ORDER],
            *[new_m[n] for n in ORDER], *[new_v[n] for n in ORDER])
```

```python
import jax
import jax.numpy as jnp
from jax import lax
from jax.experimental import pallas as pl
from jax.experimental.pallas import tpu as pltpu

F32 = jnp.float32
BF16 = jnp.bfloat16

N_DEV = 8
EPS = 1e-6
NEG_INF = -1e30
CHUNK = 64
LEFT_CHUNKS = 8
PAD = LEFT_CHUNKS * CHUNK
QBLK = 4 * CHUNK
KBAND = PAD + QBLK
REL_CLIP = 128
ATTN_HEADS = 16
HEAD_DIM = 64
D_ATTN = ATTN_HEADS * HEAD_DIM
POOL_WINDOWS = (2, 4, 8, 16)
POOL_GROUP = 256
D_POOL = len(POOL_WINDOWS) * POOL_GROUP
CROSS_HEADS = 4
CROSS_DIM = 128
D_CROSS = CROSS_HEADS * CROSS_DIM
FFN_RES = 0.5
ADAM_LR, ADAM_B1, ADAM_B2, ADAM_EPS, ADAM_WD, ADAM_STEP = 0.001, 0.9, 0.999, 1e-08, 0.01, 10

NN = (((1,), (0,)), ((), ()))
NT = (((1,), (1,)), ((), ()))
TN = (((0,), (0,)), ((), ()))
MESH = pl.DeviceIdType.MESH
VMEM_LIMIT = 56 * 1024 * 1024
ADAM_STEP_ELEMS = 384 * 1024


def _params(**kw):
    return pltpu.CompilerParams(vmem_limit_bytes=VMEM_LIMIT, **kw)


def _bf(v):
    return v if v.dtype == BF16 else v.astype(BF16)


WHOLE = ((Ellipsis,), (Ellipsis,))


def _rms(xv, gain):
    return (xv * lax.rsqrt(jnp.mean(xv * xv, axis=-1, keepdims=True) + EPS)) * gain


def _gemm(name, a, a_spec, b, b_spec, dims, grid, outs, chunks=(WHOLE,), extras=(), epilogue=None, after=None,
          norm=None):
    nex, nout = len(extras), len(outs)
    first_out = 2 + nex + (after is not None) + (norm is not None)

    def body(*refs):
        a_ref, b_ref = refs[:2]
        if norm is not None:
            hn_out, a_ref = refs[first_out + nout], refs[-1]

            @pl.when(pl.program_id(1) == 0)
            def _():
                hn = _rms(refs[0][...], refs[first_out - 1][...]).astype(BF16)
                a_ref[...] = hn
                hn_out[...] = hn

        total = None
        for ia, ib in chunks:
            d = lax.dot_general(_bf(a_ref[ia]), _bf(b_ref[ib]), dims, preferred_element_type=F32)
            total = d if total is None else total + d
        vals = epilogue(total, *[e[...] for e in refs[2:2 + nex]]) if epilogue is not None else (total,)
        for r, v in zip(refs[first_out:first_out + nout], vals):
            r[...] = v.astype(r.dtype)

    operands = [a, b] + [x for x, _, _ in extras]
    in_specs = [pl.BlockSpec(*a_spec), pl.BlockSpec(*b_spec)] + [pl.BlockSpec(blk, m) for _, blk, m in extras]
    if after is not None:
        operands.append(after)
        in_specs.append(pl.BlockSpec(memory_space=pl.ANY))
    out_specs = [pl.BlockSpec(blk, m) for _, _, blk, m in outs]
    out_shape = [jax.ShapeDtypeStruct(s, d) for s, d, _, _ in outs]
    scratch = []
    if norm is not None:
        operands.append(norm)
        in_specs.append(pl.BlockSpec(norm.shape, lambda i, j: (0, 0)))
        out_specs.append(pl.BlockSpec(*a_spec))
        out_shape.append(jax.ShapeDtypeStruct(a.shape, BF16))
        scratch.append(pltpu.VMEM(a_spec[0], BF16))
    res = pl.pallas_call(
        body, name=name, grid=grid, in_specs=in_specs, out_specs=out_specs, out_shape=out_shape,
        scratch_shapes=scratch,
        compiler_params=_params(dimension_semantics=("parallel", "arbitrary" if norm is not None else "parallel")),
    )(*operands)
    return res[0] if len(res) == 1 else res


def _tile(n, want):
    for t in range(min(n, want), 15, -1):
        if n % t == 0 and t % 16 == 0:
            return t
    return n


def _mm_nn(name, a, b, out_dtype, res=None, tm=1024, tn=1024, norm=None):
    M, K = a.shape
    N = b.shape[1]
    tm, tn = _tile(M, tm), _tile(N, tn)
    extras = [] if res is None else [(res, (tm, tn), lambda i, j: (i, j))]
    epi = None if res is None else (lambda t, r: (r + t,))
    return _gemm(name, a, ((tm, K), lambda i, j: (i, 0)), b, ((K, tn), lambda i, j: (0, j)), NN,
                 (M // tm, N // tn), [((M, N), out_dtype, (tm, tn), lambda i, j: (i, j))], extras=extras, epilogue=epi,
                 norm=norm)


def _mm_nn_cols(name, a, bs, out_dtype, res=None, tm=1024, norm=None):
    M, K = a.shape
    nb, _, w = bs.shape
    tm = _tile(M, tm)
    extras = [] if res is None else [(res, (tm, w), lambda i, j: (i, j))]
    epi = None if res is None else (lambda t, r: (r + t,))
    return _gemm(name, a, ((tm, K), lambda i, j: (i, 0)), bs, ((None, K, w), lambda i, j: (j, 0, 0)), NN,
                 (M // tm, nb), [((M, nb * w), out_dtype, (tm, w), lambda i, j: (i, j))], extras=extras, epilogue=epi,
                 norm=norm)


def _mm_nt(name, a, b, out_dtype, tm=1024, tn=1024, after=None):
    M, K = a.shape
    N = b.shape[0]
    tm, tn = _tile(M, tm), _tile(N, tn)
    return _gemm(name, a, ((tm, K), lambda i, j: (i, 0)), b, ((tn, K), lambda i, j: (j, 0)), NT,
                 (M // tm, N // tn), [((M, N), out_dtype, (tm, tn), lambda i, j: (i, j))], after=after)


def _mm_nt_cols(name, a, bs, out_dtype, tm=1024, tn=512, after=None):
    M = a.shape[0]
    nb, N, w = bs.shape
    tm, tn = _tile(M, tm), _tile(N, tn)
    chunks = [((slice(None), pl.ds(c * w, w)), (c,)) for c in range(nb)]
    return _gemm(name, a, ((tm, nb * w), lambda i, j: (i, 0)), bs, ((nb, tn, w), lambda i, j: (0, j, 0)), NT,
                 (M // tm, N // tn), [((M, N), out_dtype, (tm, tn), lambda i, j: (i, j))], chunks, after=after)


def _mm_tn(name, a, b, tm=1024, tn=1024, col_blocks=None):
    T, Ka = a.shape
    Nb = b.shape[1]
    tm = _tile(Ka, tm)
    if col_blocks is None:
        tn = _tile(Nb, tn)
        out = ((Ka, Nb), BF16, (tm, tn), lambda i, j: (i, j))
    else:
        tn = Nb // col_blocks
        out = ((col_blocks, Ka, tn), BF16, (None, tm, tn), lambda i, j: (j, i, 0))
    return _gemm(name, a, ((T, tm), lambda i, j: (0, i)), b, ((T, tn), lambda i, j: (0, j)), TN,
                 (Ka // tm, Nb // tn), [out])


def _hidden_block(wt):
    return 2 * wt.shape[0] // N_DEV


def _ffn_out(name, a, wt, res=None, scale=1.0, tm=1024, tn=512, after=None, out_dtype=F32):
    M, F = a.shape
    N = wt.shape[1]
    tm, tn = _tile(M, tm), _tile(N, tn)
    extras = [] if res is None else [(res, (tm, tn), lambda i, j: (i, j))]
    epi = None if res is None else (lambda t, r: (r + scale * t,))
    return _gemm(name, a, ((tm, F), lambda i, j: (i, 0)), wt, ((F, tn), lambda i, j: (0, j)), NN,
                 (M // tm, N // tn), [((M, N), out_dtype, (tm, tn), lambda i, j: (i, j))], extras=extras,
                 epilogue=epi, after=after)


def _ffn_dact(name, dhb, wd, g, u, tm=1024, after=None):
    M, K = dhb.shape
    F = wd.shape[0]
    w = _hidden_block(wd)
    tm = _tile(M, tm)
    tr = _tile(tm, 256)
    tokens = [] if after is None else [after]

    def body(dh_ref, wd_ref, g_ref, u_ref, *rest):
        dg_ref, du_ref = rest[-2:]
        pieces = [pl.ds(r * tr, tr) for r in range(tm // tr)]

        def product(rows):
            return lax.dot_general(dh_ref[rows, :], wd_ref[...], NT, preferred_element_type=F32)

        results = []
        dact = product(pieces[0])
        for n, rows in enumerate(pieces):
            ahead = product(pieces[n + 1]) if n + 1 < len(pieces) else None
            results.append(_swiglu_bwd(dact, g_ref[rows, :], u_ref[rows, :]))
            dact = ahead
        for rows, (dg, du) in zip(pieces, results):
            dg_ref[rows, :] = dg.astype(BF16)
            du_ref[rows, :] = du.astype(BF16)

    hid = pl.BlockSpec((tm, w), lambda i, j: (i, j))
    return pl.pallas_call(
        body, name=name, grid=(M // tm, F // w),
        in_specs=[pl.BlockSpec((tm, K), lambda i, j: (i, 0)), pl.BlockSpec((w, K), lambda i, j: (j, 0)), hid, hid]
        + [pl.BlockSpec(memory_space=pl.ANY) for t in tokens],
        out_specs=[hid, hid], out_shape=[jax.ShapeDtypeStruct((M, F), BF16)] * 2,
        compiler_params=_params(dimension_semantics=("parallel", "parallel")),
    )(dhb, wd, g, u, *tokens)


def _ffn_dw(name, a, b, scale=1.0, tn=512, after=None):
    T, F = a.shape
    N = b.shape[1]
    w = 2 * F // N_DEV
    tn = _tile(N, tn)
    epi = None if scale == 1.0 else (lambda t: (t * scale,))
    return _gemm(name, a, ((T, w), lambda i, j: (0, i)), b, ((T, tn), lambda i, j: (0, j)), TN,
                 (F // w, N // tn), [((F, N), BF16, (w, tn), lambda i, j: (i, j))], epilogue=epi, after=after)


def _ffn_gate(name, hn, wgt, tm=1024):
    M, K = hn.shape
    F = wgt.shape[0]
    w = _hidden_block(wgt)
    tm = _tile(M, tm)
    return _gemm(name, hn, ((tm, K), lambda i, j: (i, 0)), wgt, ((w, K), lambda i, j: (j, 0)), NT,
                 (M // tm, F // w), [((M, F), BF16, (tm, w), lambda i, j: (i, j))])


def _ffn_up_act(name, hn, wut, g, tm=1024):
    M, K = hn.shape
    F = wut.shape[0]
    w = _hidden_block(wut)
    tm = _tile(M, tm)
    hid = ((tm, w), lambda i, j: (i, j))

    def epilogue(u, gate):
        gate = gate.astype(F32)
        return u, gate * jax.nn.sigmoid(gate) * u

    return _gemm(name, hn, ((tm, K), lambda i, j: (i, 0)), wut, ((w, K), lambda i, j: (j, 0)), NT,
                 (M // tm, F // w), [((M, F), BF16) + hid] * 2, extras=[(g,) + hid], epilogue=epilogue)


def _ffn_up(name, h, gain, wgt, wut, tm=512):
    M, K = h.shape
    F = wgt.shape[0]
    w = _hidden_block(wgt)
    tm = _tile(M, tm)

    def body(h_ref, gain_ref, g_ref, u_ref, hn_ref, og, ou, oa, a_ref):
        @pl.when(pl.program_id(1) == 0)
        def _():
            hn = _rms(h_ref[...], gain_ref[...]).astype(BF16)
            a_ref[...] = hn
            hn_ref[...] = hn

        a = a_ref[...]
        g = lax.dot_general(a, g_ref[...], NT, preferred_element_type=F32)
        u = lax.dot_general(a, u_ref[...], NT, preferred_element_type=F32)
        og[...] = g.astype(BF16)
        ou[...] = u.astype(BF16)
        oa[...] = (g * jax.nn.sigmoid(g) * u).astype(BF16)

    rows = pl.BlockSpec((tm, K), lambda i, j: (i, 0))
    wspec = pl.BlockSpec((w, K), lambda i, j: (j, 0))
    ospec = pl.BlockSpec((tm, w), lambda i, j: (i, j))
    return pl.pallas_call(
        body, name=name, grid=(M // tm, F // w),
        in_specs=[rows, pl.BlockSpec((1, K), lambda i, j: (0, 0)), wspec, wspec],
        out_specs=[rows] + [ospec] * 3,
        out_shape=[jax.ShapeDtypeStruct((M, K), BF16)] + [jax.ShapeDtypeStruct((M, F), BF16)] * 3,
        scratch_shapes=[pltpu.VMEM((tm, K), BF16)],
        compiler_params=_params(dimension_semantics=("parallel", "arbitrary")),
    )(h, gain, wgt, wut)


def _swiglu_bwd(dact, g, u):
    g = g.astype(F32)
    u = u.astype(F32)
    sig = jax.nn.sigmoid(g)
    silu = g * sig
    d = FFN_RES * dact
    return d * u * (sig * (1.0 + g * (1.0 - sig))), d * silu


def _rms_fwd(name, x, gain, tr=512, after=None):
    R, D = x.shape
    tr = _tile(R, tr)

    def body(x_ref, g_ref, *rest):
        rest[-1][...] = _rms(x_ref[...], g_ref[...]).astype(BF16)

    tokens = [] if after is None else [after]
    return pl.pallas_call(
        body, name=name, grid=(R // tr,),
        in_specs=[pl.BlockSpec((tr, D), lambda i: (i, 0)), pl.BlockSpec((1, D), lambda i: (0, 0))]
        + [pl.BlockSpec(memory_space=pl.ANY) for t in tokens],
        out_specs=pl.BlockSpec((tr, D), lambda i: (i, 0)), out_shape=jax.ShapeDtypeStruct((R, D), BF16),
        compiler_params=_params(dimension_semantics=("parallel",)),
    )(x, gain, *tokens)


def _rms_bwd_math(xv, gain, dy):
    rstd = lax.rsqrt(jnp.mean(xv * xv, axis=-1, keepdims=True) + EPS)
    xhat = xv * rstd
    dxh = dy * gain
    dx = rstd * (dxh - xhat * jnp.mean(dxh * xhat, axis=-1, keepdims=True))
    return dx, jnp.sum(dy * xhat, axis=0, keepdims=True)


def _rms_bwd(name, x, gain, dy, skip=None, out_dtype=BF16, tr=512):
    R, D = x.shape
    tr = _tile(R, tr)
    has_skip = skip is not None

    def body(*refs):
        x_ref, g_ref, dy_ref = refs[:3]
        dx_ref, dg_ref = refs[-2:]
        dx, dg = _rms_bwd_math(x_ref[...], g_ref[...], dy_ref[...].astype(F32))
        if has_skip:
            dx = dx + refs[3][...].astype(F32)
        dx_ref[...] = dx.astype(out_dtype)

        @pl.when(pl.program_id(0) == 0)
        def _():
            dg_ref[...] = dg

        @pl.when(pl.program_id(0) > 0)
        def _():
            dg_ref[...] += dg

    row = pl.BlockSpec((tr, D), lambda i: (i, 0))
    vec = pl.BlockSpec((1, D), lambda i: (0, 0))
    return pl.pallas_call(
        body, name=name, grid=(R // tr,),
        in_specs=[row, vec, row] + ([row] if has_skip else []),
        out_specs=[row, vec],
        out_shape=[jax.ShapeDtypeStruct((R, D), out_dtype), jax.ShapeDtypeStruct((1, D), F32)],
        compiler_params=_params(dimension_semantics=("arbitrary",)),
    )(*([x, gain, dy] + ([skip] if has_skip else [])))


def _loss_and_grad(name, h, gain, target, tr=512):
    R, D = h.shape
    tr = _tile(R, tr)

    def body(h_ref, g_ref, t_ref, loss_ref, dhb_ref, dg_ref):
        hv, gain_v = h_ref[...], g_ref[...]
        y = (hv * lax.rsqrt(jnp.mean(hv * hv, axis=-1, keepdims=True) + EPS)) * gain_v
        err = y - t_ref[...]
        part = jnp.full((8, 128), 0.5 * jnp.sum(jnp.mean(err * err, axis=-1, keepdims=True)), F32)
        dh, dg = _rms_bwd_math(hv, gain_v, err * (1.0 / D))
        dhb_ref[...] = dh.astype(BF16)

        @pl.when(pl.program_id(0) == 0)
        def _():
            dg_ref[...] = dg
            loss_ref[...] = part

        @pl.when(pl.program_id(0) > 0)
        def _():
            dg_ref[...] += dg
            loss_ref[...] += part

    row = pl.BlockSpec((tr, D), lambda i: (i, 0))
    vec = pl.BlockSpec((1, D), lambda i: (0, 0))
    return pl.pallas_call(
        body, name=name, grid=(R // tr,), in_specs=[row, vec, row],
        out_specs=[pl.BlockSpec((8, 128), lambda i: (0, 0)), row, vec],
        out_shape=[jax.ShapeDtypeStruct((8, 128), F32), jax.ShapeDtypeStruct((R, D), BF16),
                   jax.ShapeDtypeStruct((1, D), F32)],
        compiler_params=_params(dimension_semantics=("arbitrary",)),
    )(h, gain, target)


def _bias_tile(name, rel):
    width = KBAND + QBLK
    sat = rel[:, 2 * REL_CLIP:]
    n_left = PAD - REL_CLIP + 1
    row0 = jnp.concatenate([jnp.broadcast_to(sat, (ATTN_HEADS, n_left)), rel[:, :2 * REL_CLIP][:, ::-1],
                            jnp.broadcast_to(sat, (ATTN_HEADS, width - n_left - 2 * REL_CLIP))], axis=1)

    def body(e_ref, o_ref):
        rows = pltpu.roll(jnp.broadcast_to(e_ref[...], (QBLK, width)), 0, 1, stride=1, stride_axis=0)
        i = lax.broadcasted_iota(jnp.int32, (QBLK, KBAND), 0) // CHUNK
        j = lax.broadcasted_iota(jnp.int32, (QBLK, KBAND), 1) // CHUNK
        o_ref[...] = jnp.where((j >= i) & (j <= i + LEFT_CHUNKS), rows[:, :KBAND], NEG_INF)

    return pl.pallas_call(
        body, name=name, grid=(ATTN_HEADS,),
        in_specs=[pl.BlockSpec((None, 1, width), lambda h: (h, 0, 0))],
        out_specs=pl.BlockSpec((None, QBLK, KBAND), lambda h: (h, 0, 0)),
        out_shape=jax.ShapeDtypeStruct((ATTN_HEADS, QBLK, KBAND), F32),
        compiler_params=_params(dimension_semantics=("parallel",)),
    )(row0.reshape(ATTN_HEADS, 1, width))


ATTN_SCALE = HEAD_DIM ** -0.5
ROW_PIECES = 1


def _stack_heads(x, first):
    zero = jnp.zeros_like(x)
    return jnp.concatenate([jnp.where(first, x, zero), jnp.where(first, zero, x)], axis=0)


def _band_softmax(q_half_scaled, kb, bias, left_mask):
    s = lax.dot_general(q_half_scaled, kb, NT, preferred_element_type=F32) + bias + left_mask
    e = jnp.exp(s - jnp.max(s, axis=-1, keepdims=True))
    return e * (1.0 / jnp.sum(e, axis=-1, keepdims=True))


def _left_mask(qb):
    kpos = qb * QBLK - PAD + lax.broadcasted_iota(jnp.int32, (1, KBAND), 1)
    return jnp.where(kpos >= 0, 0.0, NEG_INF).astype(F32)


def _fill_padded(dst, src, S):
    dst[pl.ds(0, PAD), :] = jnp.zeros((PAD, dst.shape[1]), dst.dtype)
    dst[pl.ds(PAD, S), :] = src[...].astype(dst.dtype)


def _attn_fwd(name, z, bias):
    B, S, _ = z.shape
    nh2 = ATTN_HEADS // 2

    def body(q_ref, k_ref, v_ref, b_ref, o_ref, kp, vp):
        qb = pl.program_id(2)

        @pl.when(qb == 0)
        def _():
            _fill_padded(kp, k_ref, S)
            _fill_padded(vp, v_ref, S)

        start = pl.multiple_of(qb * QBLK, QBLK)
        kb, vb = kp[pl.ds(start, KBAND), :], vp[pl.ds(start, KBAND), :]
        q = (q_ref[...] * ATTN_SCALE).astype(BF16)
        first = lax.broadcasted_iota(jnp.int32, (QBLK, 2 * HEAD_DIM), 1) < HEAD_DIM
        left = _left_mask(qb)
        zero = jnp.zeros_like(q)
        qh = [jnp.where(first, q, zero), jnp.where(first, zero, q)]
        rp = QBLK // ROW_PIECES
        chains = [(a, r) for r in range(ROW_PIECES) for a in range(2)]
        ss = [lax.dot_general(qh[a][r * rp:(r + 1) * rp], kb, NT, preferred_element_type=F32) for a, r in chains]
        ps = []
        for (a, r), s in zip(chains, ss):
            s = s + b_ref[a, pl.ds(r * rp, rp), :] + left
            e = jnp.exp(s - jnp.max(s, axis=-1, keepdims=True))
            ps.append((e * (1.0 / jnp.sum(e, axis=-1, keepdims=True))).astype(BF16))
        os_ = [jnp.dot(p, vb, preferred_element_type=F32) for p in ps]
        for r in range(ROW_PIECES):
            o_ref[pl.ds(r * rp, rp), :] = jnp.where(first[:rp], os_[2 * r], os_[2 * r + 1]).astype(BF16)

    return pl.pallas_call(
        body, name=name, grid=(B, nh2, S // QBLK),
        in_specs=[pl.BlockSpec((None, QBLK, 128), lambda b, h, i: (b, i, h)),
                  pl.BlockSpec((None, S, 128), lambda b, h, i: (b, 0, nh2 + h)),
                  pl.BlockSpec((None, S, 128), lambda b, h, i: (b, 0, 2 * nh2 + h)),
                  pl.BlockSpec((2, QBLK, KBAND), lambda b, h, i: (h, 0, 0))],
        out_specs=pl.BlockSpec((None, QBLK, 128), lambda b, h, i: (b, i, h)),
        out_shape=jax.ShapeDtypeStruct((B, S, D_ATTN + D_POOL), BF16),
        scratch_shapes=[pltpu.VMEM((PAD + S, 128), BF16), pltpu.VMEM((PAD + S, 128), BF16)],
        compiler_params=_params(dimension_semantics=("parallel", "parallel", "arbitrary")),
    )(z, z, z, bias)


def _attn_bwd(name, z, bias, dcat):
    B, S, _ = z.shape
    nh2 = ATTN_HEADS // 2
    nqb = S // QBLK

    def body(q_ref, k_ref, v_ref, b_ref, do_ref, dq_ref, dk_ref, dv_ref, db_ref, kp, vp, dka, dva):
        b, qb = pl.program_id(1), pl.program_id(2)

        @pl.when(qb == 0)
        def _():
            _fill_padded(kp, k_ref, S)
            _fill_padded(vp, v_ref, S)
            dka[...] = jnp.zeros_like(dka)
            dva[...] = jnp.zeros_like(dva)

        @pl.when((qb == 0) & (b == 0))
        def _():
            db_ref[...] = jnp.zeros_like(db_ref)

        start = pl.multiple_of(qb * QBLK, QBLK)
        band = pl.ds(start, KBAND)
        kb, vb = kp[band, :], vp[band, :]
        q = (q_ref[...] * ATTN_SCALE).astype(BF16)
        do = do_ref[...]
        first = lax.broadcasted_iota(jnp.int32, (QBLK, 2 * HEAD_DIM), 1) < HEAD_DIM
        left = _left_mask(qb)
        q2, do2 = _stack_heads(q, first), _stack_heads(do, first)
        p = _band_softmax(q2, kb, b_ref[...].reshape(2 * QBLK, KBAND), left)
        dp = lax.dot_general(do2, vb, NT, preferred_element_type=F32)
        ds = p * (dp - jnp.sum(p * dp, axis=-1, keepdims=True))
        db_ref[...] += ds.reshape(2, QBLK, KBAND)
        dsb = ds.astype(BF16)
        dq = jnp.dot(dsb, kb, preferred_element_type=F32)
        dq_ref[...] = (jnp.where(first, dq[:QBLK], dq[QBLK:]) * ATTN_SCALE).astype(BF16)
        dka[:, band] += lax.dot_general(q2, dsb, TN, preferred_element_type=F32)
        dva[:, band] += lax.dot_general(do2, p.astype(BF16), TN, preferred_element_type=F32)

        @pl.when(qb == nqb - 1)
        def _():
            dk_ref[...] = dka[:, pl.ds(PAD, S)].T.astype(BF16)
            dv_ref[...] = dva[:, pl.ds(PAD, S)].T.astype(BF16)

    qspec = pl.BlockSpec((None, QBLK, 128), lambda h, b, i: (b, i, h))
    kvout = pl.BlockSpec((None, S, 128), lambda h, b, i: (b, 0, h))
    bspec = pl.BlockSpec((2, QBLK, KBAND), lambda h, b, i: (h, 0, 0))
    act = jax.ShapeDtypeStruct((B, S, D_ATTN), BF16)
    return pl.pallas_call(
        body, name=name, grid=(nh2, B, nqb),
        in_specs=[qspec,
                  pl.BlockSpec((None, S, 128), lambda h, b, i: (b, 0, nh2 + h)),
                  pl.BlockSpec((None, S, 128), lambda h, b, i: (b, 0, 2 * nh2 + h)),
                  bspec, qspec],
        out_specs=[qspec, kvout, kvout, bspec],
        out_shape=[act, act, act, jax.ShapeDtypeStruct((ATTN_HEADS, QBLK, KBAND), F32)],
        scratch_shapes=[pltpu.VMEM((PAD + S, 128), BF16), pltpu.VMEM((PAD + S, 128), BF16),
                        pltpu.VMEM((128, PAD + S), F32), pltpu.VMEM((128, PAD + S), F32)],
        compiler_params=_params(dimension_semantics=("arbitrary", "arbitrary", "arbitrary")),
    )(z, z, z, bias, dcat)


def _bias_grad(name, dbias, after):
    width = KBAND + QBLK

    def body(d_ref, after_ref, o_ref):
        acc = jnp.zeros((1, width), F32)
        for i in range(QBLK):
            row = jnp.concatenate([d_ref[pl.ds(i, 1), :], jnp.zeros((1, QBLK), F32)], axis=1)
            shift = QBLK - 1 - i
            acc = acc + (pltpu.roll(row, shift, 1) if shift else row)
        o_ref[...] = acc

    return pl.pallas_call(
        body, name=name, grid=(ATTN_HEADS,),
        in_specs=[pl.BlockSpec((None, QBLK, KBAND), lambda h: (h, 0, 0)), pl.BlockSpec(memory_space=pl.ANY)],
        out_specs=pl.BlockSpec((None, 1, width), lambda h: (h, 0, 0)),
        out_shape=jax.ShapeDtypeStruct((ATTN_HEADS, 1, width), F32),
        compiler_params=_params(dimension_semantics=("parallel",)),
    )(dbias, after)


def _rel_grad_from_diagonals(diag):
    top = PAD + QBLK - 1 - REL_CLIP
    sat = jnp.sum(diag[:, :top + 1], axis=1, keepdims=True)
    mid = diag[:, top + 1:top + 2 * REL_CLIP][:, ::-1]
    return jnp.concatenate([jnp.zeros_like(sat), mid, sat], axis=1)


def _shift_rows(x, k, forward):
    S = x.shape[0]
    t = lax.broadcasted_iota(jnp.int32, x.shape, 0)
    if forward:
        return jnp.where(t < S - k, pltpu.roll(x, S - k, 0), 0.0)
    return jnp.where(t >= k, pltpu.roll(x, k, 0), 0.0)


def _window_sum(x, g, forward):
    s = x + _shift_rows(x, 1, forward)
    out = s
    for n, k in enumerate((2, 4, 8)):
        s = s + _shift_rows(s, k, forward)
        out = jnp.where(g > n, s, out)
    return out


def _pool_count(S, g):
    t = lax.broadcasted_iota(jnp.int32, (S, 1), 0)
    w = jnp.left_shift(2, g)
    return jnp.minimum(t + 1, w).astype(F32)


def _pool_fwd(name, z, wp, pscale, mixed):
    B, S, _ = z.shape
    c0 = 3 * D_ATTN // POOL_GROUP
    y0 = D_ATTN // POOL_GROUP

    def body(u_ref, w_ref, s_ref, mixed_ref, d_ref, y_ref):
        g = pl.program_id(1)
        u = u_ref[...]
        d = (_window_sum(u, g, False) / _pool_count(S, g) - u).astype(BF16)
        d_ref[...] = d
        y_ref[...] = (jnp.dot(d, w_ref[...], preferred_element_type=F32) * s_ref[...]).astype(BF16)

    return pl.pallas_call(
        body, name=name, grid=(B, len(POOL_WINDOWS)),
        in_specs=[pl.BlockSpec((None, S, POOL_GROUP), lambda b, g: (b, 0, c0 + g)),
                  pl.BlockSpec((None, POOL_GROUP, POOL_GROUP), lambda b, g: (g, 0, 0)),
                  pl.BlockSpec((1, POOL_GROUP), lambda b, g: (0, g)),
                  pl.BlockSpec(memory_space=pl.ANY)],
        out_specs=[pl.BlockSpec((None, S, POOL_GROUP), lambda b, g: (b, 0, g)),
                   pl.BlockSpec((None, S, POOL_GROUP), lambda b, g: (b, 0, y0 + g))],
        out_shape=[jax.ShapeDtypeStruct((B, S, D_POOL), BF16), jax.ShapeDtypeStruct(mixed.shape, BF16)],
        input_output_aliases={3: 1},
        compiler_params=_params(dimension_semantics=("parallel", "parallel")),
    )(z, wp, pscale, mixed)


def _pool_bwd(name, d, wp, pscale, dcat):
    B, S, _ = d.shape
    c0 = D_ATTN // POOL_GROUP

    def body(d_ref, w_ref, s_ref, dy_ref, du_ref, dw_ref, dsc_ref):
        g, b = pl.program_id(0), pl.program_id(1)
        dv = d_ref[...]
        dy = dy_ref[...].astype(F32)
        w = w_ref[...]
        ypre = jnp.dot(dv, w, preferred_element_type=F32)
        dyp = (dy * s_ref[...]).astype(BF16)
        dd = lax.dot_general(dyp, w, NT, preferred_element_type=F32)
        du_ref[...] = (_window_sum(dd / _pool_count(S, g), g, True) - dd).astype(BF16)
        dw = lax.dot_general(dv, dyp, TN, preferred_element_type=F32)
        dsc = jnp.sum(dy * ypre, axis=0, keepdims=True)

        @pl.when(b == 0)
        def _():
            dw_ref[...] = dw
            dsc_ref[...] = dsc

        @pl.when(b > 0)
        def _():
            dw_ref[...] += dw
            dsc_ref[...] += dsc

    blk = pl.BlockSpec((None, S, POOL_GROUP), lambda g, b: (b, 0, g))
    wspec = pl.BlockSpec((None, POOL_GROUP, POOL_GROUP), lambda g, b: (g, 0, 0))
    sspec = pl.BlockSpec((1, POOL_GROUP), lambda g, b: (0, g))
    return pl.pallas_call(
        body, name=name, grid=(len(POOL_WINDOWS), B),
        in_specs=[blk, wspec, sspec, pl.BlockSpec((None, S, POOL_GROUP), lambda g, b: (b, 0, c0 + g))],
        out_specs=[blk, wspec, sspec],
        out_shape=[jax.ShapeDtypeStruct((B, S, D_POOL), BF16),
                   jax.ShapeDtypeStruct((len(POOL_WINDOWS), POOL_GROUP, POOL_GROUP), F32),
                   jax.ShapeDtypeStruct((1, D_POOL), F32)],
        compiler_params=_params(dimension_semantics=("arbitrary", "arbitrary")),
    )(d, wp, pscale, dcat)


def _cross_softmax(q, k):
    s = lax.dot_general(q, k, NT, preferred_element_type=F32) * (CROSS_DIM ** -0.5)
    e = jnp.exp(s - jnp.max(s, axis=-1, keepdims=True))
    return e * (1.0 / jnp.sum(e, axis=-1, keepdims=True))


def _cross_fwd(name, qc, kv, tq=1024):
    B, S, _ = qc.shape
    M = kv.shape[1]
    tq = _tile(S, tq)

    def body(q_ref, k_ref, v_ref, o_ref):
        p = _cross_softmax(q_ref[...], k_ref[...])
        o_ref[...] = jnp.dot(p.astype(BF16), v_ref[...], preferred_element_type=F32).astype(BF16)

    qspec = pl.BlockSpec((None, tq, CROSS_DIM), lambda b, h, i: (b, i, h))
    return pl.pallas_call(
        body, name=name, grid=(B, CROSS_HEADS, S // tq),
        in_specs=[qspec, pl.BlockSpec((None, M, CROSS_DIM), lambda b, h, i: (b, 0, h)),
                  pl.BlockSpec((None, M, CROSS_DIM), lambda b, h, i: (b, 0, CROSS_HEADS + h))],
        out_specs=qspec, out_shape=jax.ShapeDtypeStruct((B, S, D_CROSS), BF16),
        compiler_params=_params(dimension_semantics=("parallel", "parallel", "parallel")),
    )(qc, kv, kv)


def _cross_bwd(name, qc, kv, do, tq=1024):
    B, S, _ = qc.shape
    M = kv.shape[1]
    tq = _tile(S, tq)
    nq = S // tq
    scale = CROSS_DIM ** -0.5

    def body(q_ref, k_ref, v_ref, do_ref, dq_ref, dk_ref, dv_ref, dka, dva):
        i = pl.program_id(2)
        q, k, v, dov = q_ref[...], k_ref[...], v_ref[...], do_ref[...]
        p = _cross_softmax(q, k)
        dp = lax.dot_general(dov, v, NT, preferred_element_type=F32)
        ds = ((p * (dp - jnp.sum(p * dp, axis=-1, keepdims=True))) * scale).astype(BF16)
        dq_ref[...] = jnp.dot(ds, k, preferred_element_type=F32).astype(BF16)
        dk = lax.dot_general(ds, q, TN, preferred_element_type=F32)
        dv = lax.dot_general(p.astype(BF16), dov, TN, preferred_element_type=F32)

        @pl.when(i == 0)
        def _():
            dka[...] = dk
            dva[...] = dv

        @pl.when(i > 0)
        def _():
            dka[...] += dk
            dva[...] += dv

        @pl.when(i == nq - 1)
        def _():
            dk_ref[...] = dka[...].astype(BF16)
            dv_ref[...] = dva[...].astype(BF16)

    qspec = pl.BlockSpec((None, tq, CROSS_DIM), lambda b, h, i: (b, i, h))
    kspec = pl.BlockSpec((None, M, CROSS_DIM), lambda b, h, i: (b, 0, h))
    return pl.pallas_call(
        body, name=name, grid=(B, CROSS_HEADS, nq),
        in_specs=[qspec, kspec, pl.BlockSpec((None, M, CROSS_DIM), lambda b, h, i: (b, 0, CROSS_HEADS + h)), qspec],
        out_specs=[qspec, kspec, kspec],
        out_shape=[jax.ShapeDtypeStruct((B, S, D_CROSS), BF16), jax.ShapeDtypeStruct((B, M, D_CROSS), BF16),
                   jax.ShapeDtypeStruct((B, M, D_CROSS), BF16)],
        scratch_shapes=[pltpu.VMEM((M, CROSS_DIM), F32), pltpu.VMEM((M, CROSS_DIM), F32)],
        compiler_params=_params(dimension_semantics=("parallel", "parallel", "arbitrary")),
    )(qc, kv, kv, do)


def _local_step(x, mem, target, small, weights, emit, start_token=None):
    B, S, D = x.shape
    T = B * S
    x2, t2 = x.reshape(T, D), target.reshape(T, D)
    mem2 = mem.reshape(-1, D)
    n_mem = mem.shape[1]
    wts = {}

    hn1 = _rms_fwd("norm_ffn1", x2, small["ffn1_norm"], after=start_token)
    memn = _rms_fwd("norm_mem", mem2, small["mem_norm"])
    bias = _bias_tile("bias_tile", small["rel_bias"])
    wts.update(weights(0, [hn1, memn, bias]))
    g1 = _ffn_gate("ffn1_gate", hn1, wts["ffn1_w_gate"])
    wts.update(weights(1, g1))
    u1, a1 = _ffn_up_act("ffn1_up", hn1, wts["ffn1_w_up"], g1)
    wts.update(weights(2, a1))
    h1 = _ffn_out("ffn1_down", a1, wts["ffn1_w_down"], res=x2, scale=FFN_RES)
    wts.update(weights(3, h1))
    z, hn2 = _mm_nn_cols("mix_in", h1, wts["w_in"], F32, norm=small["mix_norm"])
    z = z.reshape(B, S, -1)
    mixed = _attn_fwd("attn_fwd", z, bias)
    d_pool, mixed = _pool_fwd("pool_fwd", z, wts["w_pool"], small["pool_scale"], mixed)
    cat = mixed.reshape(T, -1)
    h2 = _mm_nn("mix_out", cat, wts["w_out"], F32, res=h1)
    wts.update(weights(4, h2))
    qc, hn3 = _mm_nn("cross_q", h2, wts["w_cq"], BF16, norm=small["cross_norm"])
    kv = _mm_nn("cross_kv", memn, wts["w_ckv"], BF16)
    o = _cross_fwd("cross_fwd", qc.reshape(B, S, -1), kv.reshape(B, n_mem, -1)).reshape(T, -1)
    h3 = _mm_nn_cols("cross_out", o, wts["w_co"], F32, res=h2)
    wts.update(weights(5, h3))
    hn4, g2, u2, a2 = _ffn_up("ffn2_up", h3, small["ffn2_norm"], wts["ffn2_w_gate"], wts["ffn2_w_up"])
    h4 = _ffn_out("ffn2_down", a2, wts["ffn2_w_down"], res=h3, scale=FFN_RES)

    gs = {}
    loss_part, dh4, gs["final_norm"] = _loss_and_grad("loss", h4, small["final_norm"], t2)

    def ffn_bwd(tag, dhb, h_in, hn, g, u, a, wg, wu, wd, gain, out_dtype=BF16):
        tok = emit({tag + "_w_down": _ffn_dw(tag + "_dwd", a, dhb, scale=FFN_RES)})
        dg, du = _ffn_dact(tag + "_dact", dhb, wd, g, u, after=tok)
        tok = emit({tag + "_w_gate": _ffn_dw(tag + "_dwg", dg, hn)})
        tok = emit({tag + "_w_up": _ffn_dw(tag + "_dwu", du, hn, after=tok)})
        dhn = _ffn_out(tag + "_dhn_g", dg, wg, after=tok)
        dhn = _ffn_out(tag + "_dhn_u", du, wu, res=dhn, out_dtype=BF16)
        return _rms_bwd(tag + "_dnorm", h_in, gain, dhn, skip=dhb, out_dtype=out_dtype)

    dh3b, gs["ffn2_norm"] = ffn_bwd("ffn2", dh4, h3, hn4, g2, u2, a2, wts["ffn2_w_gate"],
                                    wts["ffn2_w_up"], wts["ffn2_w_down"], small["ffn2_norm"])
    do = _mm_nt_cols("cross_do", dh3b, wts["w_co"], BF16, tn=D_CROSS)
    gw = {"w_co": _mm_tn("cross_dwo", o, dh3b, tm=D_CROSS, col_blocks=N_DEV)}
    dqc, dk, dv = _cross_bwd("cross_bwd", qc.reshape(B, S, -1), kv.reshape(B, n_mem, -1), do.reshape(B, S, -1))
    dqc = dqc.reshape(T, -1)
    dkv = jnp.concatenate([dk, dv], axis=-1).reshape(B * n_mem, -1)
    gw["w_cq"] = _mm_tn("cross_dwq", hn3, dqc, tn=D_CROSS)
    gw["w_ckv"] = _mm_tn("cross_dwkv", memn, dkv)
    tok = emit(gw)
    dhn3 = _mm_nt("cross_dhn", dqc, wts["w_cq"], BF16, after=tok)
    dh2b, gs["cross_norm"] = _rms_bwd("cross_dnorm", h2, small["cross_norm"], dhn3, skip=dh3b)
    dcat = _mm_nt("mix_dcat", dh2b, wts["w_out"], BF16)
    gw = {"w_out": _mm_tn("mix_dwout", cat, dh2b)}
    dcat3 = dcat.reshape(B, S, -1)
    dq, dkk, dvv, dbias = _attn_bwd("attn_bwd", z, bias, dcat3)
    du, gw["w_pool"], gs["pool_scale"] = _pool_bwd("pool_bwd", d_pool, wts["w_pool"], small["pool_scale"], dcat3)
    dz = jnp.concatenate([dq, dkk, dvv, du], axis=-1).reshape(T, -1)
    gw["w_in"] = _mm_tn("mix_dwin", hn2, dz, col_blocks=N_DEV)
    tok = emit(gw)
    dhn2 = _mm_nt_cols("mix_dhn", dz, wts["w_in"], BF16, after=tok)
    dh1b, gs["mix_norm"] = _rms_bwd("mix_dnorm", h1, small["mix_norm"], dhn2, skip=dh2b)
    dx, gs["ffn1_norm"] = ffn_bwd("ffn1", dh1b, x2, hn1, g1, u1, a1, wts["ffn1_w_gate"],
                                  wts["ffn1_w_up"], wts["ffn1_w_down"], small["ffn1_norm"], out_dtype=F32)
    gs["rel_bias"] = _rel_grad_from_diagonals(_bias_grad("bias_grad", dbias, after=dx)[:, 0, :])
    dmemn = _mm_nt("cross_dmem", dkv, wts["w_ckv"], F32, tm=512, after=dx)
    _, gs["mem_norm"] = _rms_bwd("mem_dnorm", mem2, small["mem_norm"], dmemn)
    return loss_part, dx.reshape(B, S, D), gs


def _position():
    return lax.axis_index("x"), lax.axis_index("y"), lax.axis_index("c")


def _index(p):
    return 4 * p[0] + 2 * p[1] + p[2]


HBM_SPEC = pl.BlockSpec(memory_space=pltpu.HBM)
SEM_SPEC = pl.BlockSpec(memory_space=pltpu.SEMAPHORE)
ANY_SPEC = pl.BlockSpec(memory_space=pl.ANY)
ORDERED_EFFECT = pltpu.SideEffectType.DATAFLOW_SIDE_EFFECTING


N_COPIES = {"grads": N_DEV - 1, "spread": 4, "relay": 3}
N_SEMS = {"grads": N_DEV, "spread": 5, "relay": 3}


def _copies(pattern, srcs, lands, send, recv, base=0):
    x, y, c = _position()
    me, sibling = _index((x, y, c)), (x, y, 1 - c)
    chips = [(1 - x, y), (x, 1 - y), (1 - x, 1 - y)]
    if pattern == "grads":
        targets = [(x ^ (k >> 2), y ^ ((k >> 1) & 1), c ^ (k & 1)) for k in range(1, N_DEV)]
    else:
        targets = [sibling] + [(*chip, c) for chip in chips]
    per, slots, out = N_COPIES[pattern], N_SEMS[pattern], []
    for a in range(len(lands)):
        for k in range(per):
            if pattern == "relay":
                src = dst = lands[a].at[_index((*chips[k], c))]
                to = sibling
            else:
                to = targets[k]
                src = srcs[a].at[_index(to)] if pattern == "grads" else srcs[a]
                dst = lands[a].at[me]
            slot = base + a * slots + k
            out.append(pltpu.make_async_remote_copy(src_ref=src, dst_ref=dst, send_sem=send.at[slot],
                                                    recv_sem=recv.at[slot], device_id=to, device_id_type=MESH))
    return out


def _own_copies(pattern, srcs, lands, send, base=0):
    if pattern == "relay":
        return []
    x, y, c = _position()
    me = _index((x, y, c))
    slots = N_SEMS[pattern]
    return [pltpu.make_async_copy(srcs[a].at[me] if pattern == "grads" else srcs[a], lands[a].at[me],
                                  send.at[base + a * slots + slots - 1]) for a in range(len(lands))]


def _comm(name, srcs, lands, wait=None, start=None, after=None):
    after = [] if after is None else list(after) if isinstance(after, (list, tuple)) else [after]
    ns, nl = len(srcs), len(lands)
    na = ns + nl
    arrays = list(srcs) + list(lands)
    n_wait = 2 if wait else 0
    n_start = 2 if start else 0

    def body(*refs):
        ins, lnd = refs[:ns], refs[ns:na]
        if wait:
            base = wait[3] if len(wait) > 3 else 0
            for cp in _copies(wait[0], ins, lnd, refs[na], refs[na + 1], base):
                cp.wait_send()
                cp.wait_recv()
            for cp in _own_copies(wait[0], ins, lnd, refs[na], base):
                cp.wait()
        if start:
            outs = refs[na + n_wait + len(after):]
            for cp in _copies(start, ins, lnd, outs[0], outs[1]) + _own_copies(start, ins, lnd, outs[0]):
                cp.start()
            refs[-1][...] = jnp.zeros((8, 128), F32)

    out_shape, out_specs = [], []
    if start:
        sems = pltpu.SemaphoreType.DMA((nl * N_SEMS[start],))
        out_shape += [sems, sems]
        out_specs += [SEM_SPEC, SEM_SPEC]
    out_shape += [pltpu.HBM(a.shape, a.dtype) for a in arrays]
    out_specs += [HBM_SPEC] * na
    if start:
        out_shape.append(jax.ShapeDtypeStruct((8, 128), F32))
        out_specs.append(pl.BlockSpec(memory_space=pltpu.VMEM))
    operands = [pltpu.with_memory_space_constraint(a, pltpu.HBM) for a in arrays]
    operands += list(wait[1:3]) if wait else []
    operands += after
    res = pl.pallas_call(
        body, name=name, out_shape=out_shape, out_specs=out_specs,
        in_specs=[HBM_SPEC] * na + [SEM_SPEC] * n_wait + [ANY_SPEC] * len(after),
        input_output_aliases={i: n_start + i for i in range(na)},
        compiler_params=pltpu.CompilerParams(has_side_effects=ORDERED_EFFECT),
    )(*operands)
    res = list(res)
    thru = res[n_start:n_start + na]
    return thru[:ns], thru[ns:], (tuple(res[:2]) if start else None), (res[-1] if start else None)


def _adamw_math(w, g, m, v):
    m = ADAM_B1 * m + (1.0 - ADAM_B1) * g
    v = ADAM_B2 * v + (1.0 - ADAM_B2) * (g * g)
    m_hat = m / (1.0 - ADAM_B1 ** ADAM_STEP)
    v_hat = v / (1.0 - ADAM_B2 ** ADAM_STEP)
    delta = -ADAM_LR * (m_hat / (jnp.sqrt(v_hat) + ADAM_EPS) + ADAM_WD * w)
    return delta, m, v


def _adamw(name, parts, w, m, v):
    R, C = w.shape
    tr = _tile(R, max(16, ADAM_STEP_ELEMS // C))

    def body(p_ref, w_ref, m_ref, v_ref, g_out, d_out, m_out, v_out):
        g = p_ref[0].astype(F32)
        for d in range(1, N_DEV):
            g = g + p_ref[d].astype(F32)
        g_out[...] = g
        d_out[...], m_out[...], v_out[...] = _adamw_math(w_ref[...], g, m_ref[...], v_ref[...])

    row = pl.BlockSpec((tr, C), lambda i: (i, 0))
    out = jax.ShapeDtypeStruct((R, C), F32)
    return pl.pallas_call(
        body, name=name, grid=(R // tr,),
        in_specs=[pl.BlockSpec((N_DEV, tr, C), lambda i: (0, i, 0)), row, row, row],
        out_specs=[row] * 4, out_shape=[out] * 4,
        compiler_params=_params(dimension_semantics=("parallel",)),
    )(parts, w, m, v)


def _small_allreduce_adamw(name, g, w, m, v):
    R = g.shape[0]

    def body(g_ref, w_ref, m_ref, v_ref, g_out, d_out, m_out, v_out, land, send, recv):
        x, y, c = _position()
        me = _index((x, y, c))
        land[me] = g_ref[...]
        copies = []
        for k in range(1, N_DEV):
            peer = (x ^ (k >> 2), y ^ ((k >> 1) & 1), c ^ (k & 1))
            copies.append(pltpu.make_async_remote_copy(
                src_ref=g_ref, dst_ref=land.at[me], send_sem=send.at[k - 1], recv_sem=recv.at[k - 1],
                device_id=peer, device_id_type=MESH))
        for cp in copies:
            cp.start()
        for cp in copies:
            cp.wait()
        total = land[0]
        for d in range(1, N_DEV):
            total = total + land[d]
        g_out[...] = total
        d_out[...], m_out[...], v_out[...] = _adamw_math(w_ref[...], total, m_ref[...], v_ref[...])

    vm = pl.BlockSpec(memory_space=pltpu.VMEM)
    out = jax.ShapeDtypeStruct((R, 128), F32)
    return pl.pallas_call(
        body, name=name, in_specs=[vm] * 4, out_specs=[vm] * 4, out_shape=[out] * 4,
        scratch_shapes=[pltpu.VMEM((N_DEV, R, 128), F32), pltpu.SemaphoreType.DMA((7,)),
                        pltpu.SemaphoreType.DMA((7,))],
    )(g, w, m, v)


BIG = ("ffn1_w_gate", "ffn1_w_up", "ffn1_w_down", "w_in", "w_pool", "w_out", "w_cq", "w_ckv", "w_co",
       "ffn2_w_gate", "ffn2_w_up", "ffn2_w_down")
SMALL = ("ffn1_norm", "mix_norm", "rel_bias", "pool_scale", "cross_norm", "mem_norm", "ffn2_norm", "final_norm")
ORDER = ("ffn1_norm", "ffn1_w_gate", "ffn1_w_up", "ffn1_w_down", "mix_norm", "w_in", "rel_bias", "w_pool",
         "pool_scale", "w_out", "cross_norm", "mem_norm", "w_cq", "w_ckv", "w_co", "ffn2_norm", "ffn2_w_gate",
         "ffn2_w_up", "ffn2_w_down", "final_norm")
TRANSPOSED = ("ffn1_w_gate", "ffn1_w_up", "ffn2_w_gate", "ffn2_w_up")
ROW_SHARDED = TRANSPOSED + ("ffn1_w_down", "ffn2_w_down", "w_out", "w_cq", "w_ckv")
GATHER_GROUPS = (("ffn1_w_gate",), ("ffn1_w_up",), ("ffn1_w_down",), ("w_in", "w_pool", "w_out"),
                 ("w_cq", "w_ckv", "w_co"), ("ffn2_w_gate", "ffn2_w_up", "ffn2_w_down"))
RELAY_BEFORE_USE = ((0,), (1,), (2,), (3,), (4, 5), ())


def _pack(arrays):
    flat = jnp.concatenate([a.reshape(-1) for a in arrays])
    rows = -(-flat.shape[0] // 1024) * 8
    return jnp.pad(flat, (0, rows * 128 - flat.shape[0])).reshape(rows, 128)


def _unpack(packed, like):
    flat, out, at = packed.reshape(-1), [], 0
    for a in like:
        out.append(flat[at:at + a.size].reshape(a.shape))
        at += a.size
    return out


def _shard2d(a):
    a = a[0]
    return a.reshape(-1, a.shape[-1])


def kernel(x, mem, ffn1_norm, ffn1_w_gate, ffn1_w_up, ffn1_w_down, mix_norm, w_in, rel_bias, w_pool, pool_scale, w_out, cross_norm, mem_norm, w_cq, w_ckv, w_co, ffn2_norm, ffn2_w_gate, ffn2_w_up, ffn2_w_down, final_norm, loss_target, m_ffn1_norm, m_ffn1_w_gate, m_ffn1_w_up, m_ffn1_w_down, m_mix_norm, m_w_in, m_rel_bias, m_w_pool, m_pool_scale, m_w_out, m_cross_norm, m_mem_norm, m_w_cq, m_w_ckv, m_w_co, m_ffn2_norm, m_ffn2_w_gate, m_ffn2_w_up, m_ffn2_w_down, m_final_norm, v_ffn1_norm, v_ffn1_w_gate, v_ffn1_w_up, v_ffn1_w_down, v_mix_norm, v_w_in, v_rel_bias, v_w_pool, v_pool_scale, v_w_out, v_cross_norm, v_mem_norm, v_w_cq, v_w_ckv, v_w_co, v_ffn2_norm, v_ffn2_w_gate, v_ffn2_w_up, v_ffn2_w_down, v_final_norm):
    args = dict(locals())
    def view(n, a):
        return a.transpose(0, 2, 1) if n in TRANSPOSED else a

    w_in_ = {n: view(n, args[n]) for n in ORDER}
    m_in = {n: view(n, args["m_" + n]) for n in ORDER}
    v_in = {n: view(n, args["v_" + n]) for n in ORDER}

    n_g, rows = len(POOL_WINDOWS), POOL_GROUP // N_DEV

    def landing(block_shape, dtype):
        return lax.empty((N_DEV,) + tuple(block_shape), dtype)

    gathers, tok = [], None
    for first in (True, False):
        groups = GATHER_GROUPS[:1] if first else GATHER_GROUPS[1:]
        shards = [_shard2d(w_in_[n]) for group in groups for n in group]
        shards = [(s if tok is None else s + tok[0, 0]).astype(BF16) for s in shards]
        srcs, lands, sems, tok = _comm("gather_start_%d" % (not first), shards,
                                       [landing(s.shape, BF16) for s in shards], start="spread", after=tok)
        at = 0
        for group in groups:
            gathers.append((srcs[at:at + len(group)], lands[at:at + len(group)], sems + (at * N_SEMS["spread"],)))
            at += len(group)

    def weights(gi, after):
        for ri in RELAY_BEFORE_USE[gi]:
            srcs, lands, sems = gathers[ri]
            _, lands, sems, after = _comm("gather_relay_%d" % ri, srcs, lands, wait=("spread",) + sems,
                                          start="relay", after=after)
            gathers[ri] = (None, lands, sems)
        _, lands, sems = gathers[gi]
        _, lands, _, _ = _comm("gather_finish_%d" % gi, [], lands, wait=("relay",) + sems, after=after)
        out = {}
        for n, full in zip(GATHER_GROUPS[gi], lands):
            if n == "w_pool":
                full = full.reshape(N_DEV, n_g, rows, POOL_GROUP).transpose(1, 0, 2, 3).reshape(n_g, POOL_GROUP, POOL_GROUP)
            out[n] = full.reshape(-1, full.shape[-1]) if n in ROW_SHARDED else full
        return out

    scatters = []

    def emit(gw):
        names = list(gw)
        stacks = []
        for n in names:
            g = gw[n]
            if n == "w_pool":
                g = g.reshape(n_g, N_DEV, rows, POOL_GROUP).transpose(1, 0, 2, 3).astype(BF16)
            stacks.append(g.reshape((N_DEV,) + _shard2d(w_in_[n]).shape))
        lands = [landing(s.shape[1:], s.dtype) for s in stacks]
        srcs, lands, sems, token = _comm("grads_start_%d" % len(scatters), stacks, lands, start="grads")
        scatters.append((names, srcs, lands, sems))
        return token

    small = {n: w_in_[n].reshape(1, -1) for n in SMALL if n != "rel_bias"}
    small["rel_bias"] = rel_bias[0]
    loss_part, grad_x, gs = _local_step(x, mem, loss_target, small, weights, emit, start_token=tok)

    grad, delta, new_m, new_v = {}, {}, {}, {}
    after = grad_x
    for si, (names, srcs, lands, sems) in enumerate(scatters):
        _, landed, _, _ = _comm("grads_finish_%d" % si, srcs, lands, wait=("grads",) + sems, after=after)
        for n, parts in zip(names, landed):
            res = _adamw("adamw_" + n, parts, _shard2d(w_in_[n]), _shard2d(m_in[n]), _shard2d(v_in[n]))
            grad[n], delta[n], new_m[n], new_v[n] = [view(n, r.reshape(w_in_[n].shape)) for r in res]
        after = res[0]

    slot = jnp.zeros((1,), F32)
    like = [w_in_[n] for n in SMALL] + [slot]
    gs["rel_bias"] = gs["rel_bias"].reshape(rel_bias.shape)
    res = _small_allreduce_adamw("small_params", _pack([gs[n] for n in SMALL] + [loss_part[0, :1]]), _pack(like),
                                 _pack([m_in[n] for n in SMALL] + [slot]), _pack([v_in[n] for n in SMALL] + [slot]))
    for d, packed in zip((grad, delta, new_m, new_v), res):
        for n, a in zip(SMALL, _unpack(packed, like)):
            d[n] = a
    loss = _unpack(res[0], like)[-1][0]
    return (loss, grad_x, *[grad[n] for n in ORDER], *[delta[n] for n in ORDER],
            *[new_m[n] for n in ORDER], *[new_v[n] for n in ORDER])
```

```python
import jax
import jax.numpy as jnp
from jax import lax
from jax.experimental import pallas as pl
from jax.experimental.pallas import tpu as pltpu

F32 = jnp.float32
BF16 = jnp.bfloat16

N_DEV = 8
EPS = 1e-6
NEG_INF = -1e30
CHUNK = 64
LEFT_CHUNKS = 8
PAD = LEFT_CHUNKS * CHUNK
QBLK = 4 * CHUNK
KBAND = PAD + QBLK
REL_CLIP = 128
ATTN_HEADS = 16
HEAD_DIM = 64
D_ATTN = ATTN_HEADS * HEAD_DIM
POOL_WINDOWS = (2, 4, 8, 16)
POOL_GROUP = 256
D_POOL = len(POOL_WINDOWS) * POOL_GROUP
CROSS_HEADS = 4
CROSS_DIM = 128
D_CROSS = CROSS_HEADS * CROSS_DIM
FFN_RES = 0.5
ADAM_LR, ADAM_B1, ADAM_B2, ADAM_EPS, ADAM_WD, ADAM_STEP = 0.001, 0.9, 0.999, 1e-08, 0.01, 10

NN = (((1,), (0,)), ((), ()))
NT = (((1,), (1,)), ((), ()))
TN = (((0,), (0,)), ((), ()))
MESH = pl.DeviceIdType.MESH
VMEM_LIMIT = 56 * 1024 * 1024
ADAM_STEP_ELEMS = 384 * 1024


def _params(**kw):
    return pltpu.CompilerParams(vmem_limit_bytes=VMEM_LIMIT, **kw)


def _bf(v):
    return v if v.dtype == BF16 else v.astype(BF16)


WHOLE = ((Ellipsis,), (Ellipsis,))


def _rms(xv, gain):
    return (xv * lax.rsqrt(jnp.mean(xv * xv, axis=-1, keepdims=True) + EPS)) * gain


def _gemm(name, a, a_spec, b, b_spec, dims, grid, outs, chunks=(WHOLE,), extras=(), epilogue=None, after=None,
          norm=None):
    nex, nout = len(extras), len(outs)
    first_out = 2 + nex + (after is not None) + (norm is not None)

    def body(*refs):
        a_ref, b_ref = refs[:2]
        if norm is not None:
            hn_out, a_ref = refs[first_out + nout], refs[-1]

            @pl.when(pl.program_id(1) == 0)
            def _():
                hn = _rms(refs[0][...], refs[first_out - 1][...]).astype(BF16)
                a_ref[...] = hn
                hn_out[...] = hn

        total = None
        for ia, ib in chunks:
            d = lax.dot_general(_bf(a_ref[ia]), _bf(b_ref[ib]), dims, preferred_element_type=F32)
            total = d if total is None else total + d
        vals = epilogue(total, *[e[...] for e in refs[2:2 + nex]]) if epilogue is not None else (total,)
        for r, v in zip(refs[first_out:first_out + nout], vals):
            r[...] = v.astype(r.dtype)

    operands = [a, b] + [x for x, _, _ in extras]
    in_specs = [pl.BlockSpec(*a_spec), pl.BlockSpec(*b_spec)] + [pl.BlockSpec(blk, m) for _, blk, m in extras]
    if after is not None:
        operands.append(after)
        in_specs.append(pl.BlockSpec(memory_space=pl.ANY))
    out_specs = [pl.BlockSpec(blk, m) for _, _, blk, m in outs]
    out_shape = [jax.ShapeDtypeStruct(s, d) for s, d, _, _ in outs]
    scratch = []
    if norm is not None:
        operands.append(norm)
        in_specs.append(pl.BlockSpec(norm.shape, lambda i, j: (0, 0)))
        out_specs.append(pl.BlockSpec(*a_spec))
        out_shape.append(jax.ShapeDtypeStruct(a.shape, BF16))
        scratch.append(pltpu.VMEM(a_spec[0], BF16))
    res = pl.pallas_call(
        body, name=name, grid=grid, in_specs=in_specs, out_specs=out_specs, out_shape=out_shape,
        scratch_shapes=scratch,
        compiler_params=_params(dimension_semantics=("parallel", "arbitrary" if norm is not None else "parallel")),
    )(*operands)
    return res[0] if len(res) == 1 else res


def _tile(n, want):
    for t in range(min(n, want), 15, -1):
        if n % t == 0 and t % 16 == 0:
            return t
    return n


def _mm_nn(name, a, b, out_dtype, res=None, tm=1024, tn=1024, norm=None):
    M, K = a.shape
    N = b.shape[1]
    tm, tn = _tile(M, tm), _tile(N, tn)
    extras = [] if res is None else [(res, (tm, tn), lambda i, j: (i, j))]
    epi = None if res is None else (lambda t, r: (r + t,))
    return _gemm(name, a, ((tm, K), lambda i, j: (i, 0)), b, ((K, tn), lambda i, j: (0, j)), NN,
                 (M // tm, N // tn), [((M, N), out_dtype, (tm, tn), lambda i, j: (i, j))], extras=extras, epilogue=epi,
                 norm=norm)


def _mm_nn_cols(name, a, bs, out_dtype, res=None, tm=1024, norm=None):
    M, K = a.shape
    nb, _, w = bs.shape
    tm = _tile(M, tm)
    extras = [] if res is None else [(res, (tm, w), lambda i, j: (i, j))]
    epi = None if res is None else (lambda t, r: (r + t,))
    return _gemm(name, a, ((tm, K), lambda i, j: (i, 0)), bs, ((None, K, w), lambda i, j: (j, 0, 0)), NN,
                 (M // tm, nb), [((M, nb * w), out_dtype, (tm, w), lambda i, j: (i, j))], extras=extras, epilogue=epi,
                 norm=norm)


def _mm_nt(name, a, b, out_dtype, tm=1024, tn=1024, after=None):
    M, K = a.shape
    N = b.shape[0]
    tm, tn = _tile(M, tm), _tile(N, tn)
    return _gemm(name, a, ((tm, K), lambda i, j: (i, 0)), b, ((tn, K), lambda i, j: (j, 0)), NT,
                 (M // tm, N // tn), [((M, N), out_dtype, (tm, tn), lambda i, j: (i, j))], after=after)


def _mm_nt_cols(name, a, bs, out_dtype, tm=1024, tn=512, after=None):
    M = a.shape[0]
    nb, N, w = bs.shape
    tm, tn = _tile(M, tm), _tile(N, tn)
    chunks = [((slice(None), pl.ds(c * w, w)), (c,)) for c in range(nb)]
    return _gemm(name, a, ((tm, nb * w), lambda i, j: (i, 0)), bs, ((nb, tn, w), lambda i, j: (0, j, 0)), NT,
                 (M // tm, N // tn), [((M, N), out_dtype, (tm, tn), lambda i, j: (i, j))], chunks, after=after)


def _mm_tn(name, a, b, tm=1024, tn=1024, col_blocks=None):
    T, Ka = a.shape
    Nb = b.shape[1]
    tm = _tile(Ka, tm)
    if col_blocks is None:
        tn = _tile(Nb, tn)
        out = ((Ka, Nb), BF16, (tm, tn), lambda i, j: (i, j))
    else:
        tn = Nb // col_blocks
        out = ((col_blocks, Ka, tn), BF16, (None, tm, tn), lambda i, j: (j, i, 0))
    return _gemm(name, a, ((T, tm), lambda i, j: (0, i)), b, ((T, tn), lambda i, j: (0, j)), TN,
                 (Ka // tm, Nb // tn), [out])


def _hidden_block(wt):
    return 2 * wt.shape[0] // N_DEV


def _ffn_out(name, a, wt, res=None, scale=1.0, tm=1024, tn=512, after=None, out_dtype=F32):
    M, F = a.shape
    N = wt.shape[1]
    tm, tn = _tile(M, tm), _tile(N, tn)
    extras = [] if res is None else [(res, (tm, tn), lambda i, j: (i, j))]
    epi = None if res is None else (lambda t, r: (r + scale * t,))
    return _gemm(name, a, ((tm, F), lambda i, j: (i, 0)), wt, ((F, tn), lambda i, j: (0, j)), NN,
                 (M // tm, N // tn), [((M, N), out_dtype, (tm, tn), lambda i, j: (i, j))], extras=extras,
                 epilogue=epi, after=after)


def _ffn_dact(name, dhb, wd, g, u, tm=1024, after=None):
    M, K = dhb.shape
    F = wd.shape[0]
    w = _hidden_block(wd)
    tm = _tile(M, tm)
    tr = _tile(tm, 256)
    tokens = [] if after is None else [after]

    def body(dh_ref, wd_ref, g_ref, u_ref, *rest):
        dg_ref, du_ref = rest[-2:]
        pieces = [pl.ds(r * tr, tr) for r in range(tm // tr)]

        def product(rows):
            return lax.dot_general(dh_ref[rows, :], wd_ref[...], NT, preferred_element_type=F32)

        results = []
        dact = product(pieces[0])
        for n, rows in enumerate(pieces):
            ahead = product(pieces[n + 1]) if n + 1 < len(pieces) else None
            results.append(_swiglu_bwd(dact, g_ref[rows, :], u_ref[rows, :]))
            dact = ahead
        for rows, (dg, du) in zip(pieces, results):
            dg_ref[rows, :] = dg.astype(BF16)
            du_ref[rows, :] = du.astype(BF16)

    hid = pl.BlockSpec((tm, w), lambda i, j: (i, j))
    return pl.pallas_call(
        body, name=name, grid=(M // tm, F // w),
        in_specs=[pl.BlockSpec((tm, K), lambda i, j: (i, 0)), pl.BlockSpec((w, K), lambda i, j: (j, 0)), hid, hid]
        + [pl.BlockSpec(memory_space=pl.ANY) for t in tokens],
        out_specs=[hid, hid], out_shape=[jax.ShapeDtypeStruct((M, F), BF16)] * 2,
        compiler_params=_params(dimension_semantics=("parallel", "parallel")),
    )(dhb, wd, g, u, *tokens)


def _ffn_dw(name, a, b, scale=1.0, tn=512, after=None):
    T, F = a.shape
    N = b.shape[1]
    w = 2 * F // N_DEV
    tn = _tile(N, tn)
    epi = None if scale == 1.0 else (lambda t: (t * scale,))
    return _gemm(name, a, ((T, w), lambda i, j: (0, i)), b, ((T, tn), lambda i, j: (0, j)), TN,
                 (F // w, N // tn), [((F, N), BF16, (w, tn), lambda i, j: (i, j))], epilogue=epi, after=after)


def _ffn_gate(name, hn, wgt, tm=1024):
    M, K = hn.shape
    F = wgt.shape[0]
    w = _hidden_block(wgt)
    tm = _tile(M, tm)
    return _gemm(name, hn, ((tm, K), lambda i, j: (i, 0)), wgt, ((w, K), lambda i, j: (j, 0)), NT,
                 (M // tm, F // w), [((M, F), BF16, (tm, w), lambda i, j: (i, j))])


def _ffn_up_act(name, hn, wut, g, tm=1024):
    M, K = hn.shape
    F = wut.shape[0]
    w = _hidden_block(wut)
    tm = _tile(M, tm)
    hid = ((tm, w), lambda i, j: (i, j))

    def epilogue(u, gate):
        gate = gate.astype(F32)
        return u, gate * jax.nn.sigmoid(gate) * u

    return _gemm(name, hn, ((tm, K), lambda i, j: (i, 0)), wut, ((w, K), lambda i, j: (j, 0)), NT,
                 (M // tm, F // w), [((M, F), BF16) + hid] * 2, extras=[(g,) + hid], epilogue=epilogue)


def _ffn_up(name, h, gain, wgt, wut, tm=512):
    M, K = h.shape
    F = wgt.shape[0]
    w = _hidden_block(wgt)
    tm = _tile(M, tm)

    def body(h_ref, gain_ref, g_ref, u_ref, hn_ref, og, ou, oa, a_ref):
        @pl.when(pl.program_id(1) == 0)
        def _():
            hn = _rms(h_ref[...], gain_ref[...]).astype(BF16)
            a_ref[...] = hn
            hn_ref[...] = hn

        a = a_ref[...]
        g = lax.dot_general(a, g_ref[...], NT, preferred_element_type=F32)
        u = lax.dot_general(a, u_ref[...], NT, preferred_element_type=F32)
        og[...] = g.astype(BF16)
        ou[...] = u.astype(BF16)
        oa[...] = (g * jax.nn.sigmoid(g) * u).astype(BF16)

    rows = pl.BlockSpec((tm, K), lambda i, j: (i, 0))
    wspec = pl.BlockSpec((w, K), lambda i, j: (j, 0))
    ospec = pl.BlockSpec((tm, w), lambda i, j: (i, j))
    return pl.pallas_call(
        body, name=name, grid=(M // tm, F // w),
        in_specs=[rows, pl.BlockSpec((1, K), lambda i, j: (0, 0)), wspec, wspec],
        out_specs=[rows] + [ospec] * 3,
        out_shape=[jax.ShapeDtypeStruct((M, K), BF16)] + [jax.ShapeDtypeStruct((M, F), BF16)] * 3,
        scratch_shapes=[pltpu.VMEM((tm, K), BF16)],
        compiler_params=_params(dimension_semantics=("parallel", "arbitrary")),
    )(h, gain, wgt, wut)


def _swiglu_bwd(dact, g, u):
    g = g.astype(F32)
    u = u.astype(F32)
    sig = jax.nn.sigmoid(g)
    silu = g * sig
    d = FFN_RES * dact
    return d * u * (sig * (1.0 + g * (1.0 - sig))), d * silu


def _rms_fwd(name, x, gain, tr=512, after=None):
    R, D = x.shape
    tr = _tile(R, tr)

    def body(x_ref, g_ref, *rest):
        rest[-1][...] = _rms(x_ref[...], g_ref[...]).astype(BF16)

    tokens = [] if after is None else [after]
    return pl.pallas_call(
        body, name=name, grid=(R // tr,),
        in_specs=[pl.BlockSpec((tr, D), lambda i: (i, 0)), pl.BlockSpec((1, D), lambda i: (0, 0))]
        + [pl.BlockSpec(memory_space=pl.ANY) for t in tokens],
        out_specs=pl.BlockSpec((tr, D), lambda i: (i, 0)), out_shape=jax.ShapeDtypeStruct((R, D), BF16),
        compiler_params=_params(dimension_semantics=("parallel",)),
    )(x, gain, *tokens)


def _rms_bwd_math(xv, gain, dy):
    rstd = lax.rsqrt(jnp.mean(xv * xv, axis=-1, keepdims=True) + EPS)
    xhat = xv * rstd
    dxh = dy * gain
    dx = rstd * (dxh - xhat * jnp.mean(dxh * xhat, axis=-1, keepdims=True))
    return dx, jnp.sum(dy * xhat, axis=0, keepdims=True)


def _rms_bwd(name, x, gain, dy, skip=None, out_dtype=BF16, tr=512):
    R, D = x.shape
    tr = _tile(R, tr)
    has_skip = skip is not None

    def body(*refs):
        x_ref, g_ref, dy_ref = refs[:3]
        dx_ref, dg_ref = refs[-2:]
        dx, dg = _rms_bwd_math(x_ref[...], g_ref[...], dy_ref[...].astype(F32))
        if has_skip:
            dx = dx + refs[3][...].astype(F32)
        dx_ref[...] = dx.astype(out_dtype)

        @pl.when(pl.program_id(0) == 0)
        def _():
            dg_ref[...] = dg

        @pl.when(pl.program_id(0) > 0)
        def _():
            dg_ref[...] += dg

    row = pl.BlockSpec((tr, D), lambda i: (i, 0))
    vec = pl.BlockSpec((1, D), lambda i: (0, 0))
    return pl.pallas_call(
        body, name=name, grid=(R // tr,),
        in_specs=[row, vec, row] + ([row] if has_skip else []),
        out_specs=[row, vec],
        out_shape=[jax.ShapeDtypeStruct((R, D), out_dtype), jax.ShapeDtypeStruct((1, D), F32)],
        compiler_params=_params(dimension_semantics=("arbitrary",)),
    )(*([x, gain, dy] + ([skip] if has_skip else [])))


def _loss_and_grad(name, h, gain, target, tr=512):
    R, D = h.shape
    tr = _tile(R, tr)

    def body(h_ref, g_ref, t_ref, loss_ref, dhb_ref, dg_ref):
        hv, gain_v = h_ref[...], g_ref[...]
        y = (hv * lax.rsqrt(jnp.mean(hv * hv, axis=-1, keepdims=True) + EPS)) * gain_v
        err = y - t_ref[...]
        part = jnp.full((8, 128), 0.5 * jnp.sum(jnp.mean(err * err, axis=-1, keepdims=True)), F32)
        dh, dg = _rms_bwd_math(hv, gain_v, err * (1.0 / D))
        dhb_ref[...] = dh.astype(BF16)

        @pl.when(pl.program_id(0) == 0)
        def _():
            dg_ref[...] = dg
            loss_ref[...] = part

        @pl.when(pl.program_id(0) > 0)
        def _():
            dg_ref[...] += dg
            loss_ref[...] += part

    row = pl.BlockSpec((tr, D), lambda i: (i, 0))
    vec = pl.BlockSpec((1, D), lambda i: (0, 0))
    return pl.pallas_call(
        body, name=name, grid=(R // tr,), in_specs=[row, vec, row],
        out_specs=[pl.BlockSpec((8, 128), lambda i: (0, 0)), row, vec],
        out_shape=[jax.ShapeDtypeStruct((8, 128), F32), jax.ShapeDtypeStruct((R, D), BF16),
                   jax.ShapeDtypeStruct((1, D), F32)],
        compiler_params=_params(dimension_semantics=("arbitrary",)),
    )(h, gain, target)


def _bias_tile(name, rel):
    width = KBAND + QBLK
    sat = rel[:, 2 * REL_CLIP:]
    n_left = PAD - REL_CLIP + 1
    row0 = jnp.concatenate([jnp.broadcast_to(sat, (ATTN_HEADS, n_left)), rel[:, :2 * REL_CLIP][:, ::-1],
                            jnp.broadcast_to(sat, (ATTN_HEADS, width - n_left - 2 * REL_CLIP))], axis=1)

    def body(e_ref, o_ref):
        rows = pltpu.roll(jnp.broadcast_to(e_ref[...], (QBLK, width)), 0, 1, stride=1, stride_axis=0)
        i = lax.broadcasted_iota(jnp.int32, (QBLK, KBAND), 0) // CHUNK
        j = lax.broadcasted_iota(jnp.int32, (QBLK, KBAND), 1) // CHUNK
        o_ref[...] = jnp.where((j >= i) & (j <= i + LEFT_CHUNKS), rows[:, :KBAND], NEG_INF)

    return pl.pallas_call(
        body, name=name, grid=(ATTN_HEADS,),
        in_specs=[pl.BlockSpec((None, 1, width), lambda h: (h, 0, 0))],
        out_specs=pl.BlockSpec((None, QBLK, KBAND), lambda h: (h, 0, 0)),
        out_shape=jax.ShapeDtypeStruct((ATTN_HEADS, QBLK, KBAND), F32),
        compiler_params=_params(dimension_semantics=("parallel",)),
    )(row0.reshape(ATTN_HEADS, 1, width))


ATTN_SCALE = HEAD_DIM ** -0.5
ROW_PIECES = 1


def _stack_heads(x, first):
    zero = jnp.zeros_like(x)
    return jnp.concatenate([jnp.where(first, x, zero), jnp.where(first, zero, x)], axis=0)


def _band_softmax(q_half_scaled, kb, bias, left_mask):
    s = lax.dot_general(q_half_scaled, kb, NT, preferred_element_type=F32) + bias + left_mask
    e = jnp.exp(s - jnp.max(s, axis=-1, keepdims=True))
    return e * (1.0 / jnp.sum(e, axis=-1, keepdims=True))


def _left_mask(qb):
    kpos = qb * QBLK - PAD + lax.broadcasted_iota(jnp.int32, (1, KBAND), 1)
    return jnp.where(kpos >= 0, 0.0, NEG_INF).astype(F32)


def _fill_padded(dst, src, S):
    dst[pl.ds(0, PAD), :] = jnp.zeros((PAD, dst.shape[1]), dst.dtype)
    dst[pl.ds(PAD, S), :] = src[...].astype(dst.dtype)


def _attn_fwd(name, z, bias):
    B, S, _ = z.shape
    nh2 = ATTN_HEADS // 2

    def body(q_ref, k_ref, v_ref, b_ref, o_ref, kp, vp):
        qb = pl.program_id(2)

        @pl.when(qb == 0)
        def _():
            _fill_padded(kp, k_ref, S)
            _fill_padded(vp, v_ref, S)

        start = pl.multiple_of(qb * QBLK, QBLK)
        kb, vb = kp[pl.ds(start, KBAND), :], vp[pl.ds(start, KBAND), :]
        q = (q_ref[...] * ATTN_SCALE).astype(BF16)
        first = lax.broadcasted_iota(jnp.int32, (QBLK, 2 * HEAD_DIM), 1) < HEAD_DIM
        left = _left_mask(qb)
        zero = jnp.zeros_like(q)
        qh = [jnp.where(first, q, zero), jnp.where(first, zero, q)]
        rp = QBLK // ROW_PIECES
        chains = [(a, r) for r in range(ROW_PIECES) for a in range(2)]
        ss = [lax.dot_general(qh[a][r * rp:(r + 1) * rp], kb, NT, preferred_element_type=F32) for a, r in chains]
        ps = []
        for (a, r), s in zip(chains, ss):
            s = s + b_ref[a, pl.ds(r * rp, rp), :] + left
            e = jnp.exp(s - jnp.max(s, axis=-1, keepdims=True))
            ps.append((e * (1.0 / jnp.sum(e, axis=-1, keepdims=True))).astype(BF16))
        os_ = [jnp.dot(p, vb, preferred_element_type=F32) for p in ps]
        for r in range(ROW_PIECES):
            o_ref[pl.ds(r * rp, rp), :] = jnp.where(first[:rp], os_[2 * r], os_[2 * r + 1]).astype(BF16)

    return pl.pallas_call(
        body, name=name, grid=(B, nh2, S // QBLK),
        in_specs=[pl.BlockSpec((None, QBLK, 128), lambda b, h, i: (b, i, h)),
                  pl.BlockSpec((None, S, 128), lambda b, h, i: (b, 0, nh2 + h)),
                  pl.BlockSpec((None, S, 128), lambda b, h, i: (b, 0, 2 * nh2 + h)),
                  pl.BlockSpec((2, QBLK, KBAND), lambda b, h, i: (h, 0, 0))],
        out_specs=pl.BlockSpec((None, QBLK, 128), lambda b, h, i: (b, i, h)),
        out_shape=jax.ShapeDtypeStruct((B, S, D_ATTN + D_POOL), BF16),
        scratch_shapes=[pltpu.VMEM((PAD + S, 128), BF16), pltpu.VMEM((PAD + S, 128), BF16)],
        compiler_params=_params(dimension_semantics=("parallel", "parallel", "arbitrary")),
    )(z, z, z, bias)


def _attn_bwd(name, z, bias, dcat):
    B, S, _ = z.shape
    nh2 = ATTN_HEADS // 2
    nqb = S // QBLK

    def body(q_ref, k_ref, v_ref, b_ref, do_ref, dq_ref, dk_ref, dv_ref, db_ref, kp, vp, dka, dva):
        b, qb = pl.program_id(1), pl.program_id(2)

        @pl.when(qb == 0)
        def _():
            _fill_padded(kp, k_ref, S)
            _fill_padded(vp, v_ref, S)
            dka[...] = jnp.zeros_like(dka)
            dva[...] = jnp.zeros_like(dva)

        @pl.when((qb == 0) & (b == 0))
        def _():
            db_ref[...] = jnp.zeros_like(db_ref)

        start = pl.multiple_of(qb * QBLK, QBLK)
        band = pl.ds(start, KBAND)
        kb, vb = kp[band, :], vp[band, :]
        q = (q_ref[...] * ATTN_SCALE).astype(BF16)
        do = do_ref[...]
        first = lax.broadcasted_iota(jnp.int32, (QBLK, 2 * HEAD_DIM), 1) < HEAD_DIM
        left = _left_mask(qb)
        q2, do2 = _stack_heads(q, first), _stack_heads(do, first)
        p = _band_softmax(q2, kb, b_ref[...].reshape(2 * QBLK, KBAND), left)
        dp = lax.dot_general(do2, vb, NT, preferred_element_type=F32)
        ds = p * (dp - jnp.sum(p * dp, axis=-1, keepdims=True))
        db_ref[...] += ds.reshape(2, QBLK, KBAND)
        dsb = ds.astype(BF16)
        dq = jnp.dot(dsb, kb, preferred_element_type=F32)
        dq_ref[...] = (jnp.where(first, dq[:QBLK], dq[QBLK:]) * ATTN_SCALE).astype(BF16)
        dka[:, band] += lax.dot_general(q2, dsb, TN, preferred_element_type=F32)
        dva[:, band] += lax.dot_general(do2, p.astype(BF16), TN, preferred_element_type=F32)

        @pl.when(qb == nqb - 1)
        def _():
            dk_ref[...] = dka[:, pl.ds(PAD, S)].T.astype(BF16)
            dv_ref[...] = dva[:, pl.ds(PAD, S)].T.astype(BF16)

    qspec = pl.BlockSpec((None, QBLK, 128), lambda h, b, i: (b, i, h))
    kvout = pl.BlockSpec((None, S, 128), lambda h, b, i: (b, 0, h))
    bspec = pl.BlockSpec((2, QBLK, KBAND), lambda h, b, i: (h, 0, 0))
    act = jax.ShapeDtypeStruct((B, S, D_ATTN), BF16)
    return pl.pallas_call(
        body, name=name, grid=(nh2, B, nqb),
        in_specs=[qspec,
                  pl.BlockSpec((None, S, 128), lambda h, b, i: (b, 0, nh2 + h)),
                  pl.BlockSpec((None, S, 128), lambda h, b, i: (b, 0, 2 * nh2 + h)),
                  bspec, qspec],
        out_specs=[qspec, kvout, kvout, bspec],
        out_shape=[act, act, act, jax.ShapeDtypeStruct((ATTN_HEADS, QBLK, KBAND), F32)],
        scratch_shapes=[pltpu.VMEM((PAD + S, 128), BF16), pltpu.VMEM((PAD + S, 128), BF16),
                        pltpu.VMEM((128, PAD + S), F32), pltpu.VMEM((128, PAD + S), F32)],
        compiler_params=_params(dimension_semantics=("arbitrary", "arbitrary", "arbitrary")),
    )(z, z, z, bias, dcat)


def _bias_grad(name, dbias, after):
    width = KBAND + QBLK

    def body(d_ref, after_ref, o_ref):
        acc = jnp.zeros((1, width), F32)
        for i in range(QBLK):
            row = jnp.concatenate([d_ref[pl.ds(i, 1), :], jnp.zeros((1, QBLK), F32)], axis=1)
            shift = QBLK - 1 - i
            acc = acc + (pltpu.roll(row, shift, 1) if shift else row)
        o_ref[...] = acc

    return pl.pallas_call(
        body, name=name, grid=(ATTN_HEADS,),
        in_specs=[pl.BlockSpec((None, QBLK, KBAND), lambda h: (h, 0, 0)), pl.BlockSpec(memory_space=pl.ANY)],
        out_specs=pl.BlockSpec((None, 1, width), lambda h: (h, 0, 0)),
        out_shape=jax.ShapeDtypeStruct((ATTN_HEADS, 1, width), F32),
        compiler_params=_params(dimension_semantics=("parallel",)),
    )(dbias, after)


def _rel_grad_from_diagonals(diag):
    top = PAD + QBLK - 1 - REL_CLIP
    sat = jnp.sum(diag[:, :top + 1], axis=1, keepdims=True)
    mid = diag[:, top + 1:top + 2 * REL_CLIP][:, ::-1]
    return jnp.concatenate([jnp.zeros_like(sat), mid, sat], axis=1)


def _shift_rows(x, k, forward):
    S = x.shape[0]
    t = lax.broadcasted_iota(jnp.int32, x.shape, 0)
    if forward:
        return jnp.where(t < S - k, pltpu.roll(x, S - k, 0), 0.0)
    return jnp.where(t >= k, pltpu.roll(x, k, 0), 0.0)


def _window_sum(x, g, forward):
    s = x + _shift_rows(x, 1, forward)
    out = s
    for n, k in enumerate((2, 4, 8)):
        s = s + _shift_rows(s, k, forward)
        out = jnp.where(g > n, s, out)
    return out


def _pool_count(S, g):
    t = lax.broadcasted_iota(jnp.int32, (S, 1), 0)
    w = jnp.left_shift(2, g)
    return jnp.minimum(t + 1, w).astype(F32)


def _pool_fwd(name, z, wp, pscale, mixed):
    B, S, _ = z.shape
    c0 = 3 * D_ATTN // POOL_GROUP
    y0 = D_ATTN // POOL_GROUP

    def body(u_ref, w_ref, s_ref, mixed_ref, d_ref, y_ref):
        g = pl.program_id(1)
        u = u_ref[...]
        d = (_window_sum(u, g, False) / _pool_count(S, g) - u).astype(BF16)
        d_ref[...] = d
        y_ref[...] = (jnp.dot(d, w_ref[...], preferred_element_type=F32) * s_ref[...]).astype(BF16)

    return pl.pallas_call(
        body, name=name, grid=(B, len(POOL_WINDOWS)),
        in_specs=[pl.BlockSpec((None, S, POOL_GROUP), lambda b, g: (b, 0, c0 + g)),
                  pl.BlockSpec((None, POOL_GROUP, POOL_GROUP), lambda b, g: (g, 0, 0)),
                  pl.BlockSpec((1, POOL_GROUP), lambda b, g: (0, g)),
                  pl.BlockSpec(memory_space=pl.ANY)],
        out_specs=[pl.BlockSpec((None, S, POOL_GROUP), lambda b, g: (b, 0, g)),
                   pl.BlockSpec((None, S, POOL_GROUP), lambda b, g: (b, 0, y0 + g))],
        out_shape=[jax.ShapeDtypeStruct((B, S, D_POOL), BF16), jax.ShapeDtypeStruct(mixed.shape, BF16)],
        input_output_aliases={3: 1},
        compiler_params=_params(dimension_semantics=("parallel", "parallel")),
    )(z, wp, pscale, mixed)


def _pool_bwd(name, d, wp, pscale, dcat):
    B, S, _ = d.shape
    c0 = D_ATTN // POOL_GROUP

    def body(d_ref, w_ref, s_ref, dy_ref, du_ref, dw_ref, dsc_ref):
        g, b = pl.program_id(0), pl.program_id(1)
        dv = d_ref[...]
        dy = dy_ref[...].astype(F32)
        w = w_ref[...]
        ypre = jnp.dot(dv, w, preferred_element_type=F32)
        dyp = (dy * s_ref[...]).astype(BF16)
        dd = lax.dot_general(dyp, w, NT, preferred_element_type=F32)
        du_ref[...] = (_window_sum(dd / _pool_count(S, g), g, True) - dd).astype(BF16)
        dw = lax.dot_general(dv, dyp, TN, preferred_element_type=F32)
        dsc = jnp.sum(dy * ypre, axis=0, keepdims=True)

        @pl.when(b == 0)
        def _():
            dw_ref[...] = dw
            dsc_ref[...] = dsc

        @pl.when(b > 0)
        def _():
            dw_ref[...] += dw
            dsc_ref[...] += dsc

    blk = pl.BlockSpec((None, S, POOL_GROUP), lambda g, b: (b, 0, g))
    wspec = pl.BlockSpec((None, POOL_GROUP, POOL_GROUP), lambda g, b: (g, 0, 0))
    sspec = pl.BlockSpec((1, POOL_GROUP), lambda g, b: (0, g))
    return pl.pallas_call(
        body, name=name, grid=(len(POOL_WINDOWS), B),
        in_specs=[blk, wspec, sspec, pl.BlockSpec((None, S, POOL_GROUP), lambda g, b: (b, 0, c0 + g))],
        out_specs=[blk, wspec, sspec],
        out_shape=[jax.ShapeDtypeStruct((B, S, D_POOL), BF16),
                   jax.ShapeDtypeStruct((len(POOL_WINDOWS), POOL_GROUP, POOL_GROUP), F32),
                   jax.ShapeDtypeStruct((1, D_POOL), F32)],
        compiler_params=_params(dimension_semantics=("arbitrary", "arbitrary")),
    )(d, wp, pscale, dcat)


def _cross_softmax(q, k):
    s = lax.dot_general(q, k, NT, preferred_element_type=F32) * (CROSS_DIM ** -0.5)
    e = jnp.exp(s - jnp.max(s, axis=-1, keepdims=True))
    return e * (1.0 / jnp.sum(e, axis=-1, keepdims=True))


def _cross_fwd(name, qc, kv, tq=1024):
    B, S, _ = qc.shape
    M = kv.shape[1]
    tq = _tile(S, tq)

    def body(q_ref, k_ref, v_ref, o_ref):
        p = _cross_softmax(q_ref[...], k_ref[...])
        o_ref[...] = jnp.dot(p.astype(BF16), v_ref[...], preferred_element_type=F32).astype(BF16)

    qspec = pl.BlockSpec((None, tq, CROSS_DIM), lambda b, h, i: (b, i, h))
    return pl.pallas_call(
        body, name=name, grid=(B, CROSS_HEADS, S // tq),
        in_specs=[qspec, pl.BlockSpec((None, M, CROSS_DIM), lambda b, h, i: (b, 0, h)),
                  pl.BlockSpec((None, M, CROSS_DIM), lambda b, h, i: (b, 0, CROSS_HEADS + h))],
        out_specs=qspec, out_shape=jax.ShapeDtypeStruct((B, S, D_CROSS), BF16),
        compiler_params=_params(dimension_semantics=("parallel", "parallel", "parallel")),
    )(qc, kv, kv)


def _cross_bwd(name, qc, kv, do, tq=1024):
    B, S, _ = qc.shape
    M = kv.shape[1]
    tq = _tile(S, tq)
    nq = S // tq
    scale = CROSS_DIM ** -0.5

    def body(q_ref, k_ref, v_ref, do_ref, dq_ref, dk_ref, dv_ref, dka, dva):
        i = pl.program_id(2)
        q, k, v, dov = q_ref[...], k_ref[...], v_ref[...], do_ref[...]
        p = _cross_softmax(q, k)
        dp = lax.dot_general(dov, v, NT, preferred_element_type=F32)
        ds = ((p * (dp - jnp.sum(p * dp, axis=-1, keepdims=True))) * scale).astype(BF16)
        dq_ref[...] = jnp.dot(ds, k, preferred_element_type=F32).astype(BF16)
        dk = lax.dot_general(ds, q, TN, preferred_element_type=F32)
        dv = lax.dot_general(p.astype(BF16), dov, TN, preferred_element_type=F32)

        @pl.when(i == 0)
        def _():
            dka[...] = dk
            dva[...] = dv

        @pl.when(i > 0)
        def _():
            dka[...] += dk
            dva[...] += dv

        @pl.when(i == nq - 1)
        def _():
            dk_ref[...] = dka[...].astype(BF16)
            dv_ref[...] = dva[...].astype(BF16)

    qspec = pl.BlockSpec((None, tq, CROSS_DIM), lambda b, h, i: (b, i, h))
    kspec = pl.BlockSpec((None, M, CROSS_DIM), lambda b, h, i: (b, 0, h))
    return pl.pallas_call(
        body, name=name, grid=(B, CROSS_HEADS, nq),
        in_specs=[qspec, kspec, pl.BlockSpec((None, M, CROSS_DIM), lambda b, h, i: (b, 0, CROSS_HEADS + h)), qspec],
        out_specs=[qspec, kspec, kspec],
        out_shape=[jax.ShapeDtypeStruct((B, S, D_CROSS), BF16), jax.ShapeDtypeStruct((B, M, D_CROSS), BF16),
                   jax.ShapeDtypeStruct((B, M, D_CROSS), BF16)],
        scratch_shapes=[pltpu.VMEM((M, CROSS_DIM), F32), pltpu.VMEM((M, CROSS_DIM), F32)],
        compiler_params=_params(dimension_semantics=("parallel", "parallel", "arbitrary")),
    )(qc, kv, kv, do)


def _local_step(x, mem, target, small, weights, emit, start_token=None):
    B, S, D = x.shape
    T = B * S
    x2, t2 = x.reshape(T, D), target.reshape(T, D)
    mem2 = mem.reshape(-1, D)
    n_mem = mem.shape[1]
    wts = {}

    hn1 = _rms_fwd("norm_ffn1", x2, small["ffn1_norm"], after=start_token)
    memn = _rms_fwd("norm_mem", mem2, small["mem_norm"])
    bias = _bias_tile("bias_tile", small["rel_bias"])
    wts.update(weights(0, [hn1, memn, bias]))
    g1 = _ffn_gate("ffn1_gate", hn1, wts["ffn1_w_gate"])
    wts.update(weights(1, g1))
    u1, a1 = _ffn_up_act("ffn1_up", hn1, wts["ffn1_w_up"], g1)
    wts.update(weights(2, a1))
    h1 = _ffn_out("ffn1_down", a1, wts["ffn1_w_down"], res=x2, scale=FFN_RES)
    wts.update(weights(3, h1))
    z, hn2 = _mm_nn_cols("mix_in", h1, wts["w_in"], F32, norm=small["mix_norm"])
    z = z.reshape(B, S, -1)
    mixed = _attn_fwd("attn_fwd", z, bias)
    d_pool, mixed = _pool_fwd("pool_fwd", z, wts["w_pool"], small["pool_scale"], mixed)
    cat = mixed.reshape(T, -1)
    h2 = _mm_nn("mix_out", cat, wts["w_out"], F32, res=h1)
    wts.update(weights(4, h2))
    qc, hn3 = _mm_nn("cross_q", h2, wts["w_cq"], BF16, norm=small["cross_norm"])
    kv = _mm_nn("cross_kv", memn, wts["w_ckv"], BF16)
    o = _cross_fwd("cross_fwd", qc.reshape(B, S, -1), kv.reshape(B, n_mem, -1)).reshape(T, -1)
    h3 = _mm_nn_cols("cross_out", o, wts["w_co"], F32, res=h2)
    wts.update(weights(5, h3))
    hn4, g2, u2, a2 = _ffn_up("ffn2_up", h3, small["ffn2_norm"], wts["ffn2_w_gate"], wts["ffn2_w_up"])
    h4 = _ffn_out("ffn2_down", a2, wts["ffn2_w_down"], res=h3, scale=FFN_RES)

    gs = {}
    loss_part, dh4, gs["final_norm"] = _loss_and_grad("loss", h4, small["final_norm"], t2)

    def ffn_bwd(tag, dhb, h_in, hn, g, u, a, wg, wu, wd, gain, out_dtype=BF16):
        tok = emit({tag + "_w_down": _ffn_dw(tag + "_dwd", a, dhb, scale=FFN_RES)})
        dg, du = _ffn_dact(tag + "_dact", dhb, wd, g, u, after=tok)
        tok = emit({tag + "_w_gate": _ffn_dw(tag + "_dwg", dg, hn)})
        tok = emit({tag + "_w_up": _ffn_dw(tag + "_dwu", du, hn, after=tok)})
        dhn = _ffn_out(tag + "_dhn_g", dg, wg, after=tok)
        dhn = _ffn_out(tag + "_dhn_u", du, wu, res=dhn, out_dtype=BF16)
        return _rms_bwd(tag + "_dnorm", h_in, gain, dhn, skip=dhb, out_dtype=out_dtype)

    dh3b, gs["ffn2_norm"] = ffn_bwd("ffn2", dh4, h3, hn4, g2, u2, a2, wts["ffn2_w_gate"],
                                    wts["ffn2_w_up"], wts["ffn2_w_down"], small["ffn2_norm"])
    do = _mm_nt_cols("cross_do", dh3b, wts["w_co"], BF16, tn=D_CROSS)
    gw = {"w_co": _mm_tn("cross_dwo", o, dh3b, tm=D_CROSS, col_blocks=N_DEV)}
    dqc, dk, dv = _cross_bwd("cross_bwd", qc.reshape(B, S, -1), kv.reshape(B, n_mem, -1), do.reshape(B, S, -1))
    dqc = dqc.reshape(T, -1)
    dkv = jnp.concatenate([dk, dv], axis=-1).reshape(B * n_mem, -1)
    gw["w_cq"] = _mm_tn("cross_dwq", hn3, dqc, tn=D_CROSS)
    gw["w_ckv"] = _mm_tn("cross_dwkv", memn, dkv)
    tok = emit(gw)
    dhn3 = _mm_nt("cross_dhn", dqc, wts["w_cq"], BF16, after=tok)
    dh2b, gs["cross_norm"] = _rms_bwd("cross_dnorm", h2, small["cross_norm"], dhn3, skip=dh3b)
    dcat = _mm_nt("mix_dcat", dh2b, wts["w_out"], BF16)
    gw = {"w_out": _mm_tn("mix_dwout", cat, dh2b)}
    dcat3 = dcat.reshape(B, S, -1)
    dq, dkk, dvv, dbias = _attn_bwd("attn_bwd", z, bias, dcat3)
    du, gw["w_pool"], gs["pool_scale"] = _pool_bwd("pool_bwd", d_pool, wts["w_pool"], small["pool_scale"], dcat3)
    dz = jnp.concatenate([dq, dkk, dvv, du], axis=-1).reshape(T, -1)
    gw["w_in"] = _mm_tn("mix_dwin", hn2, dz, col_blocks=N_DEV)
    tok = emit(gw)
    dhn2 = _mm_nt_cols("mix_dhn", dz, wts["w_in"], BF16, after=tok)
    dh1b, gs["mix_norm"] = _rms_bwd("mix_dnorm", h1, small["mix_norm"], dhn2, skip=dh2b)
    dx, gs["ffn1_norm"] = ffn_bwd("ffn1", dh1b, x2, hn1, g1, u1, a1, wts["ffn1_w_gate"],
                                  wts["ffn1_w_up"], wts["ffn1_w_down"], small["ffn1_norm"], out_dtype=F32)
    gs["rel_bias"] = _rel_grad_from_diagonals(_bias_grad("bias_grad", dbias, after=dx)[:, 0, :])
    dmemn = _mm_nt("cross_dmem", dkv, wts["w_ckv"], F32, tm=512, after=dx)
    _, gs["mem_norm"] = _rms_bwd("mem_dnorm", mem2, small["mem_norm"], dmemn)
    return loss_part, dx.reshape(B, S, D), gs


def _position():
    return lax.axis_index("x"), lax.axis_index("y"), lax.axis_index("c")


def _index(p):
    return 4 * p[0] + 2 * p[1] + p[2]


HBM_SPEC = pl.BlockSpec(memory_space=pltpu.HBM)
SEM_SPEC = pl.BlockSpec(memory_space=pltpu.SEMAPHORE)
ANY_SPEC = pl.BlockSpec(memory_space=pl.ANY)
ORDERED_EFFECT = pltpu.SideEffectType.DATAFLOW_SIDE_EFFECTING


N_COPIES = {"grads": N_DEV - 1, "spread": 4, "relay": 3}
N_SEMS = {"grads": N_DEV, "spread": 5, "relay": 3}


def _copies(pattern, srcs, lands, send, recv, base=0):
    x, y, c = _position()
    me, sibling = _index((x, y, c)), (x, y, 1 - c)
    chips = [(1 - x, y), (x, 1 - y), (1 - x, 1 - y)]
    if pattern == "grads":
        targets = [(x ^ (k >> 2), y ^ ((k >> 1) & 1), c ^ (k & 1)) for k in range(1, N_DEV)]
    else:
        targets = [sibling] + [(*chip, c) for chip in chips]
    per, slots, out = N_COPIES[pattern], N_SEMS[pattern], []
    for a in range(len(lands)):
        for k in range(per):
            if pattern == "relay":
                src = dst = lands[a].at[_index((*chips[k], c))]
                to = sibling
            else:
                to = targets[k]
                src = srcs[a].at[_index(to)] if pattern == "grads" else srcs[a]
                dst = lands[a].at[me]
            slot = base + a * slots + k
            out.append(pltpu.make_async_remote_copy(src_ref=src, dst_ref=dst, send_sem=send.at[slot],
                                                    recv_sem=recv.at[slot], device_id=to, device_id_type=MESH))
    return out


def _own_copies(pattern, srcs, lands, send, base=0):
    if pattern == "relay":
        return []
    x, y, c = _position()
    me = _index((x, y, c))
    slots = N_SEMS[pattern]
    return [pltpu.make_async_copy(srcs[a].at[me] if pattern == "grads" else srcs[a], lands[a].at[me],
                                  send.at[base + a * slots + slots - 1]) for a in range(len(lands))]


def _comm(name, srcs, lands, wait=None, start=None, after=None):
    after = [] if after is None else list(after) if isinstance(after, (list, tuple)) else [after]
    ns, nl = len(srcs), len(lands)
    na = ns + nl
    arrays = list(srcs) + list(lands)
    n_wait = 2 if wait else 0
    n_start = 2 if start else 0

    def body(*refs):
        ins, lnd = refs[:ns], refs[ns:na]
        if wait:
            base = wait[3] if len(wait) > 3 else 0
            for cp in _copies(wait[0], ins, lnd, refs[na], refs[na + 1], base):
                cp.wait_send()
                cp.wait_recv()
            for cp in _own_copies(wait[0], ins, lnd, refs[na], base):
                cp.wait()
        if start:
            outs = refs[na + n_wait + len(after):]
            for cp in _copies(start, ins, lnd, outs[0], outs[1]) + _own_copies(start, ins, lnd, outs[0]):
                cp.start()
            refs[-1][...] = jnp.zeros((8, 128), F32)

    out_shape, out_specs = [], []
    if start:
        sems = pltpu.SemaphoreType.DMA((nl * N_SEMS[start],))
        out_shape += [sems, sems]
        out_specs += [SEM_SPEC, SEM_SPEC]
    out_shape += [pltpu.HBM(a.shape, a.dtype) for a in arrays]
    out_specs += [HBM_SPEC] * na
    if start:
        out_shape.append(jax.ShapeDtypeStruct((8, 128), F32))
        out_specs.append(pl.BlockSpec(memory_space=pltpu.VMEM))
    operands = [pltpu.with_memory_space_constraint(a, pltpu.HBM) for a in arrays]
    operands += list(wait[1:3]) if wait else []
    operands += after
    res = pl.pallas_call(
        body, name=name, out_shape=out_shape, out_specs=out_specs,
        in_specs=[HBM_SPEC] * na + [SEM_SPEC] * n_wait + [ANY_SPEC] * len(after),
        input_output_aliases={i: n_start + i for i in range(na)},
        compiler_params=pltpu.CompilerParams(has_side_effects=ORDERED_EFFECT),
    )(*operands)
    res = list(res)
    thru = res[n_start:n_start + na]
    return thru[:ns], thru[ns:], (tuple(res[:2]) if start else None), (res[-1] if start else None)


def _adamw_math(w, g, m, v):
    m = ADAM_B1 * m + (1.0 - ADAM_B1) * g
    v = ADAM_B2 * v + (1.0 - ADAM_B2) * (g * g)
    m_hat = m / (1.0 - ADAM_B1 ** ADAM_STEP)
    v_hat = v / (1.0 - ADAM_B2 ** ADAM_STEP)
    delta = -ADAM_LR * (m_hat / (jnp.sqrt(v_hat) + ADAM_EPS) + ADAM_WD * w)
    return delta, m, v


def _adamw(name, parts, w, m, v):
    R, C = w.shape
    tr = _tile(R, max(16, ADAM_STEP_ELEMS // C))

    def body(p_ref, w_ref, m_ref, v_ref, g_out, d_out, m_out, v_out):
        g = p_ref[0].astype(F32)
        for d in range(1, N_DEV):
            g = g + p_ref[d].astype(F32)
        g_out[...] = g
        d_out[...], m_out[...], v_out[...] = _adamw_math(w_ref[...], g, m_ref[...], v_ref[...])

    row = pl.BlockSpec((tr, C), lambda i: (i, 0))
    out = jax.ShapeDtypeStruct((R, C), F32)
    return pl.pallas_call(
        body, name=name, grid=(R // tr,),
        in_specs=[pl.BlockSpec((N_DEV, tr, C), lambda i: (0, i, 0)), row, row, row],
        out_specs=[row] * 4, out_shape=[out] * 4,
        compiler_params=_params(dimension_semantics=("parallel",)),
    )(parts, w, m, v)


def _small_allreduce_adamw(name, g, w, m, v, after):
    R = g.shape[0]

    def body(g_ref, w_ref, m_ref, v_ref, after_ref, g_out, d_out, m_out, v_out, land, send, recv):
        x, y, c = _position()
        me = _index((x, y, c))
        land[me] = g_ref[...]
        copies = []
        for k in range(1, N_DEV):
            peer = (x ^ (k >> 2), y ^ ((k >> 1) & 1), c ^ (k & 1))
            copies.append(pltpu.make_async_remote_copy(
                src_ref=g_ref, dst_ref=land.at[me], send_sem=send.at[k - 1], recv_sem=recv.at[k - 1],
                device_id=peer, device_id_type=MESH))
        for cp in copies:
            cp.start()
        for cp in copies:
            cp.wait()
        total = land[0]
        for d in range(1, N_DEV):
            total = total + land[d]
        g_out[...] = total
        d_out[...], m_out[...], v_out[...] = _adamw_math(w_ref[...], total, m_ref[...], v_ref[...])

    vm = pl.BlockSpec(memory_space=pltpu.VMEM)
    out = jax.ShapeDtypeStruct((R, 128), F32)
    return pl.pallas_call(
        body, name=name, in_specs=[vm] * 4 + [ANY_SPEC], out_specs=[vm] * 4, out_shape=[out] * 4,
        scratch_shapes=[pltpu.VMEM((N_DEV, R, 128), F32), pltpu.SemaphoreType.DMA((7,)),
                        pltpu.SemaphoreType.DMA((7,))],
    )(g, w, m, v, after)


BIG = ("ffn1_w_gate", "ffn1_w_up", "ffn1_w_down", "w_in", "w_pool", "w_out", "w_cq", "w_ckv", "w_co",
       "ffn2_w_gate", "ffn2_w_up", "ffn2_w_down")
SMALL = ("ffn1_norm", "mix_norm", "rel_bias", "pool_scale", "cross_norm", "mem_norm", "ffn2_norm", "final_norm")
ORDER = ("ffn1_norm", "ffn1_w_gate", "ffn1_w_up", "ffn1_w_down", "mix_norm", "w_in", "rel_bias", "w_pool",
         "pool_scale", "w_out", "cross_norm", "mem_norm", "w_cq", "w_ckv", "w_co", "ffn2_norm", "ffn2_w_gate",
         "ffn2_w_up", "ffn2_w_down", "final_norm")
TRANSPOSED = ("ffn1_w_gate", "ffn1_w_up", "ffn2_w_gate", "ffn2_w_up")
ROW_SHARDED = TRANSPOSED + ("ffn1_w_down", "ffn2_w_down", "w_out", "w_cq", "w_ckv")
GATHER_GROUPS = (("ffn1_w_gate",), ("ffn1_w_up",), ("ffn1_w_down",), ("w_in", "w_pool", "w_out"),
                 ("w_cq", "w_ckv", "w_co"), ("ffn2_w_gate", "ffn2_w_up", "ffn2_w_down"))
RELAY_BEFORE_USE = ((0,), (1,), (2,), (3,), (4, 5), ())


def _pack(arrays):
    flat = jnp.concatenate([a.reshape(-1) for a in arrays])
    rows = -(-flat.shape[0] // 1024) * 8
    return jnp.pad(flat, (0, rows * 128 - flat.shape[0])).reshape(rows, 128)


def _unpack(packed, like):
    flat, out, at = packed.reshape(-1), [], 0
    for a in like:
        out.append(flat[at:at + a.size].reshape(a.shape))
        at += a.size
    return out


def _shard2d(a):
    a = a[0]
    return a.reshape(-1, a.shape[-1])


def kernel(x, mem, ffn1_norm, ffn1_w_gate, ffn1_w_up, ffn1_w_down, mix_norm, w_in, rel_bias, w_pool, pool_scale, w_out, cross_norm, mem_norm, w_cq, w_ckv, w_co, ffn2_norm, ffn2_w_gate, ffn2_w_up, ffn2_w_down, final_norm, loss_target, m_ffn1_norm, m_ffn1_w_gate, m_ffn1_w_up, m_ffn1_w_down, m_mix_norm, m_w_in, m_rel_bias, m_w_pool, m_pool_scale, m_w_out, m_cross_norm, m_mem_norm, m_w_cq, m_w_ckv, m_w_co, m_ffn2_norm, m_ffn2_w_gate, m_ffn2_w_up, m_ffn2_w_down, m_final_norm, v_ffn1_norm, v_ffn1_w_gate, v_ffn1_w_up, v_ffn1_w_down, v_mix_norm, v_w_in, v_rel_bias, v_w_pool, v_pool_scale, v_w_out, v_cross_norm, v_mem_norm, v_w_cq, v_w_ckv, v_w_co, v_ffn2_norm, v_ffn2_w_gate, v_ffn2_w_up, v_ffn2_w_down, v_final_norm):
    args = dict(locals())
    def view(n, a):
        return a.transpose(0, 2, 1) if n in TRANSPOSED else a

    w_in_ = {n: view(n, args[n]) for n in ORDER}
    m_in = {n: view(n, args["m_" + n]) for n in ORDER}
    v_in = {n: view(n, args["v_" + n]) for n in ORDER}

    n_g, rows = len(POOL_WINDOWS), POOL_GROUP // N_DEV

    def landing(block_shape, dtype):
        return lax.empty((N_DEV,) + tuple(block_shape), dtype)

    gathers, tok = [], None
    for first in (True, False):
        groups = GATHER_GROUPS[:1] if first else GATHER_GROUPS[1:]
        shards = [_shard2d(w_in_[n]) for group in groups for n in group]
        shards = [(s if tok is None else s + tok[0, 0]).astype(BF16) for s in shards]
        srcs, lands, sems, tok = _comm("gather_start_%d" % (not first), shards,
                                       [landing(s.shape, BF16) for s in shards], start="spread", after=tok)
        at = 0
        for group in groups:
            gathers.append((srcs[at:at + len(group)], lands[at:at + len(group)], sems + (at * N_SEMS["spread"],)))
            at += len(group)

    def weights(gi, after):
        for ri in RELAY_BEFORE_USE[gi]:
            srcs, lands, sems = gathers[ri]
            _, lands, sems, after = _comm("gather_relay_%d" % ri, srcs, lands, wait=("spread",) + sems,
                                          start="relay", after=after)
            gathers[ri] = (None, lands, sems)
        _, lands, sems = gathers[gi]
        _, lands, _, _ = _comm("gather_finish_%d" % gi, [], lands, wait=("relay",) + sems, after=after)
        out = {}
        for n, full in zip(GATHER_GROUPS[gi], lands):
            if n == "w_pool":
                full = full.reshape(N_DEV, n_g, rows, POOL_GROUP).transpose(1, 0, 2, 3).reshape(n_g, POOL_GROUP, POOL_GROUP)
            out[n] = full.reshape(-1, full.shape[-1]) if n in ROW_SHARDED else full
        return out

    scatters = []

    def emit(gw):
        names = list(gw)
        stacks = []
        for n in names:
            g = gw[n]
            if n == "w_pool":
                g = g.reshape(n_g, N_DEV, rows, POOL_GROUP).transpose(1, 0, 2, 3).astype(BF16)
            stacks.append(g.reshape((N_DEV,) + _shard2d(w_in_[n]).shape))
        lands = [landing(s.shape[1:], s.dtype) for s in stacks]
        srcs, lands, sems, token = _comm("grads_start_%d" % len(scatters), stacks, lands, start="grads")
        scatters.append((names, srcs, lands, sems))
        return token

    small = {n: w_in_[n].reshape(1, -1) for n in SMALL if n != "rel_bias"}
    small["rel_bias"] = rel_bias[0]
    loss_part, grad_x, gs = _local_step(x, mem, loss_target, small, weights, emit, start_token=tok)

    grad, delta, new_m, new_v = {}, {}, {}, {}
    after = grad_x
    for si, (names, srcs, lands, sems) in enumerate(scatters):
        _, landed, _, _ = _comm("grads_finish_%d" % si, srcs, lands, wait=("grads",) + sems, after=after)
        for n, parts in zip(names, landed):
            res = _adamw("adamw_" + n, parts, _shard2d(w_in_[n]), _shard2d(m_in[n]), _shard2d(v_in[n]))
            grad[n], delta[n], new_m[n], new_v[n] = [view(n, r.reshape(w_in_[n].shape)) for r in res]
        after = res[0]

    slot = jnp.zeros((1,), F32)
    like = [w_in_[n] for n in SMALL] + [slot]
    gs["rel_bias"] = gs["rel_bias"].reshape(rel_bias.shape)
    res = _small_allreduce_adamw("small_params", _pack([gs[n] for n in SMALL] + [loss_part[0, :1]]), _pack(like),
                                 _pack([m_in[n] for n in SMALL] + [slot]), _pack([v_in[n] for n in SMALL] + [slot]),
                                 after=after)
    for d, packed in zip((grad, delta, new_m, new_v), res):
        for n, a in zip(SMALL, _unpack(packed, like)):
            d[n] = a
    loss = _unpack(res[0], like)[-1][0]
    return (loss, grad_x, *[grad[n] for n in ORDER], *[delta[n] for n in ORDER],
            *[new_m[n] for n in ORDER], *[new_v[n] for n in ORDER])
```

```python
import jax
import jax.numpy as jnp
from jax import lax
from jax.experimental import pallas as pl
from jax.experimental.pallas import tpu as pltpu

F32 = jnp.float32
BF16 = jnp.bfloat16

N_DEV = 8
EPS = 1e-6
NEG_INF = -1e30
CHUNK = 64
LEFT_CHUNKS = 8
PAD = LEFT_CHUNKS * CHUNK
QBLK = 4 * CHUNK
KBAND = PAD + QBLK
REL_CLIP = 128
ATTN_HEADS = 16
HEAD_DIM = 64
D_ATTN = ATTN_HEADS * HEAD_DIM
POOL_WINDOWS = (2, 4, 8, 16)
POOL_GROUP = 256
D_POOL = len(POOL_WINDOWS) * POOL_GROUP
CROSS_HEADS = 4
CROSS_DIM = 128
D_CROSS = CROSS_HEADS * CROSS_DIM
FFN_RES = 0.5
ADAM_LR, ADAM_B1, ADAM_B2, ADAM_EPS, ADAM_WD, ADAM_STEP = 0.001, 0.9, 0.999, 1e-08, 0.01, 10

NN = (((1,), (0,)), ((), ()))
NT = (((1,), (1,)), ((), ()))
TN = (((0,), (0,)), ((), ()))
MESH = pl.DeviceIdType.MESH
VMEM_LIMIT = 56 * 1024 * 1024
ADAM_STEP_ELEMS = 384 * 1024


def _params(**kw):
    return pltpu.CompilerParams(vmem_limit_bytes=VMEM_LIMIT, **kw)


def _bf(v):
    return v if v.dtype == BF16 else v.astype(BF16)


WHOLE = ((Ellipsis,), (Ellipsis,))


def _rms(xv, gain):
    return (xv * lax.rsqrt(jnp.mean(xv * xv, axis=-1, keepdims=True) + EPS)) * gain


def _gemm(name, a, a_spec, b, b_spec, dims, grid, outs, chunks=(WHOLE,), extras=(), epilogue=None, after=None,
          norm=None):
    nex, nout = len(extras), len(outs)
    first_out = 2 + nex + (after is not None) + (norm is not None)

    def body(*refs):
        a_ref, b_ref = refs[:2]
        if norm is not None:
            hn_out, a_ref = refs[first_out + nout], refs[-1]

            @pl.when(pl.program_id(1) == 0)
            def _():
                hn = _rms(refs[0][...], refs[first_out - 1][...]).astype(BF16)
                a_ref[...] = hn
                hn_out[...] = hn

        total = None
        for ia, ib in chunks:
            d = lax.dot_general(_bf(a_ref[ia]), _bf(b_ref[ib]), dims, preferred_element_type=F32)
            total = d if total is None else total + d
        vals = epilogue(total, *[e[...] for e in refs[2:2 + nex]]) if epilogue is not None else (total,)
        for r, v in zip(refs[first_out:first_out + nout], vals):
            r[...] = v.astype(r.dtype)

    operands = [a, b] + [x for x, _, _ in extras]
    in_specs = [pl.BlockSpec(*a_spec), pl.BlockSpec(*b_spec)] + [pl.BlockSpec(blk, m) for _, blk, m in extras]
    if after is not None:
        operands.append(after)
        in_specs.append(pl.BlockSpec(memory_space=pl.ANY))
    out_specs = [pl.BlockSpec(blk, m) for _, _, blk, m in outs]
    out_shape = [jax.ShapeDtypeStruct(s, d) for s, d, _, _ in outs]
    scratch = []
    if norm is not None:
        operands.append(norm)
        in_specs.append(pl.BlockSpec(norm.shape, lambda i, j: (0, 0)))
        out_specs.append(pl.BlockSpec(*a_spec))
        out_shape.append(jax.ShapeDtypeStruct(a.shape, BF16))
        scratch.append(pltpu.VMEM(a_spec[0], BF16))
    res = pl.pallas_call(
        body, name=name, grid=grid, in_specs=in_specs, out_specs=out_specs, out_shape=out_shape,
        scratch_shapes=scratch,
        compiler_params=_params(dimension_semantics=("parallel", "arbitrary" if norm is not None else "parallel")),
    )(*operands)
    return res[0] if len(res) == 1 else res


def _tile(n, want):
    for t in range(min(n, want), 15, -1):
        if n % t == 0 and t % 16 == 0:
            return t
    return n


def _mm_nn(name, a, b, out_dtype, res=None, tm=1024, tn=1024, norm=None):
    M, K = a.shape
    N = b.shape[1]
    tm, tn = _tile(M, tm), _tile(N, tn)
    extras = [] if res is None else [(res, (tm, tn), lambda i, j: (i, j))]
    epi = None if res is None else (lambda t, r: (r + t,))
    return _gemm(name, a, ((tm, K), lambda i, j: (i, 0)), b, ((K, tn), lambda i, j: (0, j)), NN,
                 (M // tm, N // tn), [((M, N), out_dtype, (tm, tn), lambda i, j: (i, j))], extras=extras, epilogue=epi,
                 norm=norm)


def _mm_nn_cols(name, a, bs, out_dtype, res=None, tm=1024, norm=None):
    M, K = a.shape
    nb, _, w = bs.shape
    tm = _tile(M, tm)
    extras = [] if res is None else [(res, (tm, w), lambda i, j: (i, j))]
    epi = None if res is None else (lambda t, r: (r + t,))
    return _gemm(name, a, ((tm, K), lambda i, j: (i, 0)), bs, ((None, K, w), lambda i, j: (j, 0, 0)), NN,
                 (M // tm, nb), [((M, nb * w), out_dtype, (tm, w), lambda i, j: (i, j))], extras=extras, epilogue=epi,
                 norm=norm)


def _mm_nt(name, a, b, out_dtype, tm=1024, tn=1024, after=None):
    M, K = a.shape
    N = b.shape[0]
    tm, tn = _tile(M, tm), _tile(N, tn)
    return _gemm(name, a, ((tm, K), lambda i, j: (i, 0)), b, ((tn, K), lambda i, j: (j, 0)), NT,
                 (M // tm, N // tn), [((M, N), out_dtype, (tm, tn), lambda i, j: (i, j))], after=after)


def _mm_nt_cols(name, a, bs, out_dtype, tm=1024, tn=512, after=None):
    M = a.shape[0]
    nb, N, w = bs.shape
    tm, tn = _tile(M, tm), _tile(N, tn)
    chunks = [((slice(None), pl.ds(c * w, w)), (c,)) for c in range(nb)]
    return _gemm(name, a, ((tm, nb * w), lambda i, j: (i, 0)), bs, ((nb, tn, w), lambda i, j: (0, j, 0)), NT,
                 (M // tm, N // tn), [((M, N), out_dtype, (tm, tn), lambda i, j: (i, j))], chunks, after=after)


def _mm_tn(name, a, b, tm=1024, tn=1024, col_blocks=None):
    T, Ka = a.shape
    Nb = b.shape[1]
    tm = _tile(Ka, tm)
    if col_blocks is None:
        tn = _tile(Nb, tn)
        out = ((Ka, Nb), BF16, (tm, tn), lambda i, j: (i, j))
    else:
        tn = Nb // col_blocks
        out = ((col_blocks, Ka, tn), BF16, (None, tm, tn), lambda i, j: (j, i, 0))
    return _gemm(name, a, ((T, tm), lambda i, j: (0, i)), b, ((T, tn), lambda i, j: (0, j)), TN,
                 (Ka // tm, Nb // tn), [out])


def _hidden_block(wt):
    return 2 * wt.shape[0] // N_DEV


def _ffn_out(name, a, wt, res=None, scale=1.0, tm=1024, tn=512, after=None, out_dtype=F32):
    M, F = a.shape
    N = wt.shape[1]
    tm, tn = _tile(M, tm), _tile(N, tn)
    extras = [] if res is None else [(res, (tm, tn), lambda i, j: (i, j))]
    epi = None if res is None else (lambda t, r: (r + scale * t,))
    return _gemm(name, a, ((tm, F), lambda i, j: (i, 0)), wt, ((F, tn), lambda i, j: (0, j)), NN,
                 (M // tm, N // tn), [((M, N), out_dtype, (tm, tn), lambda i, j: (i, j))], extras=extras,
                 epilogue=epi, after=after)


def _ffn_dact(name, dhb, wd, g, u, tm=1024, after=None):
    M, K = dhb.shape
    F = wd.shape[0]
    w = _hidden_block(wd)
    tm = _tile(M, tm)
    tr = _tile(tm, 256)
    tokens = [] if after is None else [after]

    def body(dh_ref, wd_ref, g_ref, u_ref, *rest):
        dg_ref, du_ref = rest[-2:]
        pieces = [pl.ds(r * tr, tr) for r in range(tm // tr)]

        def product(rows):
            return lax.dot_general(dh_ref[rows, :], wd_ref[...], NT, preferred_element_type=F32)

        results = []
        dact = product(pieces[0])
        for n, rows in enumerate(pieces):
            ahead = product(pieces[n + 1]) if n + 1 < len(pieces) else None
            results.append(_swiglu_bwd(dact, g_ref[rows, :], u_ref[rows, :]))
            dact = ahead
        for rows, (dg, du) in zip(pieces, results):
            dg_ref[rows, :] = dg.astype(BF16)
            du_ref[rows, :] = du.astype(BF16)

    hid = pl.BlockSpec((tm, w), lambda i, j: (i, j))
    return pl.pallas_call(
        body, name=name, grid=(M // tm, F // w),
        in_specs=[pl.BlockSpec((tm, K), lambda i, j: (i, 0)), pl.BlockSpec((w, K), lambda i, j: (j, 0)), hid, hid]
        + [pl.BlockSpec(memory_space=pl.ANY) for t in tokens],
        out_specs=[hid, hid], out_shape=[jax.ShapeDtypeStruct((M, F), BF16)] * 2,
        compiler_params=_params(dimension_semantics=("parallel", "parallel")),
    )(dhb, wd, g, u, *tokens)


def _ffn_dw(name, a, b, scale=1.0, tn=512, after=None):
    T, F = a.shape
    N = b.shape[1]
    w = 2 * F // N_DEV
    tn = _tile(N, tn)
    epi = None if scale == 1.0 else (lambda t: (t * scale,))
    return _gemm(name, a, ((T, w), lambda i, j: (0, i)), b, ((T, tn), lambda i, j: (0, j)), TN,
                 (F // w, N // tn), [((F, N), BF16, (w, tn), lambda i, j: (i, j))], epilogue=epi, after=after)


def _ffn_gate(name, hn, wgt, tm=1024):
    M, K = hn.shape
    F = wgt.shape[0]
    w = _hidden_block(wgt)
    tm = _tile(M, tm)
    return _gemm(name, hn, ((tm, K), lambda i, j: (i, 0)), wgt, ((w, K), lambda i, j: (j, 0)), NT,
                 (M // tm, F // w), [((M, F), BF16, (tm, w), lambda i, j: (i, j))])


def _ffn_up_act(name, hn, wut, g, tm=1024):
    M, K = hn.shape
    F = wut.shape[0]
    w = _hidden_block(wut)
    tm = _tile(M, tm)
    hid = ((tm, w), lambda i, j: (i, j))

    def epilogue(u, gate):
        gate = gate.astype(F32)
        return u, gate * jax.nn.sigmoid(gate) * u

    return _gemm(name, hn, ((tm, K), lambda i, j: (i, 0)), wut, ((w, K), lambda i, j: (j, 0)), NT,
                 (M // tm, F // w), [((M, F), BF16) + hid] * 2, extras=[(g,) + hid], epilogue=epilogue)


def _ffn_up(name, h, gain, wgt, wut, tm=512):
    M, K = h.shape
    F = wgt.shape[0]
    w = _hidden_block(wgt)
    tm = _tile(M, tm)

    def body(h_ref, gain_ref, g_ref, u_ref, hn_ref, og, ou, oa, a_ref):
        @pl.when(pl.program_id(1) == 0)
        def _():
            hn = _rms(h_ref[...], gain_ref[...]).astype(BF16)
            a_ref[...] = hn
            hn_ref[...] = hn

        a = a_ref[...]
        g = lax.dot_general(a, g_ref[...], NT, preferred_element_type=F32)
        u = lax.dot_general(a, u_ref[...], NT, preferred_element_type=F32)
        og[...] = g.astype(BF16)
        ou[...] = u.astype(BF16)
        oa[...] = (g * jax.nn.sigmoid(g) * u).astype(BF16)

    rows = pl.BlockSpec((tm, K), lambda i, j: (i, 0))
    wspec = pl.BlockSpec((w, K), lambda i, j: (j, 0))
    ospec = pl.BlockSpec((tm, w), lambda i, j: (i, j))
    return pl.pallas_call(
        body, name=name, grid=(M // tm, F // w),
        in_specs=[rows, pl.BlockSpec((1, K), lambda i, j: (0, 0)), wspec, wspec],
        out_specs=[rows] + [ospec] * 3,
        out_shape=[jax.ShapeDtypeStruct((M, K), BF16)] + [jax.ShapeDtypeStruct((M, F), BF16)] * 3,
        scratch_shapes=[pltpu.VMEM((tm, K), BF16)],
        compiler_params=_params(dimension_semantics=("parallel", "arbitrary")),
    )(h, gain, wgt, wut)


def _swiglu_bwd(dact, g, u):
    g = g.astype(F32)
    u = u.astype(F32)
    sig = jax.nn.sigmoid(g)
    silu = g * sig
    d = FFN_RES * dact
    return d * u * (sig * (1.0 + g * (1.0 - sig))), d * silu


def _rms_fwd(name, x, gain, tr=512, after=None):
    R, D = x.shape
    tr = _tile(R, tr)

    def body(x_ref, g_ref, *rest):
        rest[-1][...] = _rms(x_ref[...], g_ref[...]).astype(BF16)

    tokens = [] if after is None else [after]
    return pl.pallas_call(
        body, name=name, grid=(R // tr,),
        in_specs=[pl.BlockSpec((tr, D), lambda i: (i, 0)), pl.BlockSpec((1, D), lambda i: (0, 0))]
        + [pl.BlockSpec(memory_space=pl.ANY) for t in tokens],
        out_specs=pl.BlockSpec((tr, D), lambda i: (i, 0)), out_shape=jax.ShapeDtypeStruct((R, D), BF16),
        compiler_params=_params(dimension_semantics=("parallel",)),
    )(x, gain, *tokens)


def _rms_bwd_math(xv, gain, dy):
    rstd = lax.rsqrt(jnp.mean(xv * xv, axis=-1, keepdims=True) + EPS)
    xhat = xv * rstd
    dxh = dy * gain
    dx = rstd * (dxh - xhat * jnp.mean(dxh * xhat, axis=-1, keepdims=True))
    return dx, jnp.sum(dy * xhat, axis=0, keepdims=True)


def _rms_bwd(name, x, gain, dy, skip=None, out_dtype=BF16, tr=512):
    R, D = x.shape
    tr = _tile(R, tr)
    has_skip = skip is not None

    def body(*refs):
        x_ref, g_ref, dy_ref = refs[:3]
        dx_ref, dg_ref = refs[-2:]
        dx, dg = _rms_bwd_math(x_ref[...], g_ref[...], dy_ref[...].astype(F32))
        if has_skip:
            dx = dx + refs[3][...].astype(F32)
        dx_ref[...] = dx.astype(out_dtype)

        @pl.when(pl.program_id(0) == 0)
        def _():
            dg_ref[...] = dg

        @pl.when(pl.program_id(0) > 0)
        def _():
            dg_ref[...] += dg

    row = pl.BlockSpec((tr, D), lambda i: (i, 0))
    vec = pl.BlockSpec((1, D), lambda i: (0, 0))
    return pl.pallas_call(
        body, name=name, grid=(R // tr,),
        in_specs=[row, vec, row] + ([row] if has_skip else []),
        out_specs=[row, vec],
        out_shape=[jax.ShapeDtypeStruct((R, D), out_dtype), jax.ShapeDtypeStruct((1, D), F32)],
        compiler_params=_params(dimension_semantics=("arbitrary",)),
    )(*([x, gain, dy] + ([skip] if has_skip else [])))


def _loss_and_grad(name, h, gain, target, tr=512):
    R, D = h.shape
    tr = _tile(R, tr)

    def body(h_ref, g_ref, t_ref, loss_ref, dhb_ref, dg_ref):
        hv, gain_v = h_ref[...], g_ref[...]
        y = (hv * lax.rsqrt(jnp.mean(hv * hv, axis=-1, keepdims=True) + EPS)) * gain_v
        err = y - t_ref[...]
        part = jnp.full((8, 128), 0.5 * jnp.sum(jnp.mean(err * err, axis=-1, keepdims=True)), F32)
        dh, dg = _rms_bwd_math(hv, gain_v, err * (1.0 / D))
        dhb_ref[...] = dh.astype(BF16)

        @pl.when(pl.program_id(0) == 0)
        def _():
            dg_ref[...] = dg
            loss_ref[...] = part

        @pl.when(pl.program_id(0) > 0)
        def _():
            dg_ref[...] += dg
            loss_ref[...] += part

    row = pl.BlockSpec((tr, D), lambda i: (i, 0))
    vec = pl.BlockSpec((1, D), lambda i: (0, 0))
    return pl.pallas_call(
        body, name=name, grid=(R // tr,), in_specs=[row, vec, row],
        out_specs=[pl.BlockSpec((8, 128), lambda i: (0, 0)), row, vec],
        out_shape=[jax.ShapeDtypeStruct((8, 128), F32), jax.ShapeDtypeStruct((R, D), BF16),
                   jax.ShapeDtypeStruct((1, D), F32)],
        compiler_params=_params(dimension_semantics=("arbitrary",)),
    )(h, gain, target)


def _bias_tile(name, rel):
    width = KBAND + QBLK
    sat = rel[:, 2 * REL_CLIP:]
    n_left = PAD - REL_CLIP + 1
    row0 = jnp.concatenate([jnp.broadcast_to(sat, (ATTN_HEADS, n_left)), rel[:, :2 * REL_CLIP][:, ::-1],
                            jnp.broadcast_to(sat, (ATTN_HEADS, width - n_left - 2 * REL_CLIP))], axis=1)

    def body(e_ref, o_ref):
        rows = pltpu.roll(jnp.broadcast_to(e_ref[...], (QBLK, width)), 0, 1, stride=1, stride_axis=0)
        i = lax.broadcasted_iota(jnp.int32, (QBLK, KBAND), 0) // CHUNK
        j = lax.broadcasted_iota(jnp.int32, (QBLK, KBAND), 1) // CHUNK
        o_ref[...] = jnp.where((j >= i) & (j <= i + LEFT_CHUNKS), rows[:, :KBAND], NEG_INF)

    return pl.pallas_call(
        body, name=name, grid=(ATTN_HEADS,),
        in_specs=[pl.BlockSpec((None, 1, width), lambda h: (h, 0, 0))],
        out_specs=pl.BlockSpec((None, QBLK, KBAND), lambda h: (h, 0, 0)),
        out_shape=jax.ShapeDtypeStruct((ATTN_HEADS, QBLK, KBAND), F32),
        compiler_params=_params(dimension_semantics=("parallel",)),
    )(row0.reshape(ATTN_HEADS, 1, width))


ATTN_SCALE = HEAD_DIM ** -0.5
ROW_PIECES = 1


def _stack_heads(x, first):
    zero = jnp.zeros_like(x)
    return jnp.concatenate([jnp.where(first, x, zero), jnp.where(first, zero, x)], axis=0)


def _band_softmax(q_half_scaled, kb, bias, left_mask):
    s = lax.dot_general(q_half_scaled, kb, NT, preferred_element_type=F32) + bias + left_mask
    e = jnp.exp(s - jnp.max(s, axis=-1, keepdims=True))
    return e * (1.0 / jnp.sum(e, axis=-1, keepdims=True))


def _left_mask(qb):
    kpos = qb * QBLK - PAD + lax.broadcasted_iota(jnp.int32, (1, KBAND), 1)
    return jnp.where(kpos >= 0, 0.0, NEG_INF).astype(F32)


def _fill_padded(dst, src, S):
    dst[pl.ds(0, PAD), :] = jnp.zeros((PAD, dst.shape[1]), dst.dtype)
    dst[pl.ds(PAD, S), :] = src[...].astype(dst.dtype)


def _attn_fwd(name, z, bias):
    B, S, _ = z.shape
    nh2 = ATTN_HEADS // 2

    def body(q_ref, k_ref, v_ref, b_ref, o_ref, kp, vp):
        qb = pl.program_id(2)

        @pl.when(qb == 0)
        def _():
            _fill_padded(kp, k_ref, S)
            _fill_padded(vp, v_ref, S)

        start = pl.multiple_of(qb * QBLK, QBLK)
        kb, vb = kp[pl.ds(start, KBAND), :], vp[pl.ds(start, KBAND), :]
        q = (q_ref[...] * ATTN_SCALE).astype(BF16)
        first = lax.broadcasted_iota(jnp.int32, (QBLK, 2 * HEAD_DIM), 1) < HEAD_DIM
        left = _left_mask(qb)
        zero = jnp.zeros_like(q)
        qh = [jnp.where(first, q, zero), jnp.where(first, zero, q)]
        rp = QBLK // ROW_PIECES
        chains = [(a, r) for r in range(ROW_PIECES) for a in range(2)]
        ss = [lax.dot_general(qh[a][r * rp:(r + 1) * rp], kb, NT, preferred_element_type=F32) for a, r in chains]
        ps = []
        for (a, r), s in zip(chains, ss):
            s = s + b_ref[a, pl.ds(r * rp, rp), :] + left
            e = jnp.exp(s - jnp.max(s, axis=-1, keepdims=True))
            ps.append((e * (1.0 / jnp.sum(e, axis=-1, keepdims=True))).astype(BF16))
        os_ = [jnp.dot(p, vb, preferred_element_type=F32) for p in ps]
        for r in range(ROW_PIECES):
            o_ref[pl.ds(r * rp, rp), :] = jnp.where(first[:rp], os_[2 * r], os_[2 * r + 1]).astype(BF16)

    return pl.pallas_call(
        body, name=name, grid=(B, nh2, S // QBLK),
        in_specs=[pl.BlockSpec((None, QBLK, 128), lambda b, h, i: (b, i, h)),
                  pl.BlockSpec((None, S, 128), lambda b, h, i: (b, 0, nh2 + h)),
                  pl.BlockSpec((None, S, 128), lambda b, h, i: (b, 0, 2 * nh2 + h)),
                  pl.BlockSpec((2, QBLK, KBAND), lambda b, h, i: (h, 0, 0))],
        out_specs=pl.BlockSpec((None, QBLK, 128), lambda b, h, i: (b, i, h)),
        out_shape=jax.ShapeDtypeStruct((B, S, D_ATTN + D_POOL), BF16),
        scratch_shapes=[pltpu.VMEM((PAD + S, 128), BF16), pltpu.VMEM((PAD + S, 128), BF16)],
        compiler_params=_params(dimension_semantics=("parallel", "parallel", "arbitrary")),
    )(z, z, z, bias)


def _attn_bwd(name, z, bias, dcat):
    B, S, _ = z.shape
    nh2 = ATTN_HEADS // 2
    nqb = S // QBLK

    def body(q_ref, k_ref, v_ref, b_ref, do_ref, dq_ref, dk_ref, dv_ref, db_ref, kp, vp, dka, dva):
        b, qb = pl.program_id(1), pl.program_id(2)

        @pl.when(qb == 0)
        def _():
            _fill_padded(kp, k_ref, S)
            _fill_padded(vp, v_ref, S)
            dka[...] = jnp.zeros_like(dka)
            dva[...] = jnp.zeros_like(dva)

        @pl.when((qb == 0) & (b == 0))
        def _():
            db_ref[...] = jnp.zeros_like(db_ref)

        start = pl.multiple_of(qb * QBLK, QBLK)
        band = pl.ds(start, KBAND)
        kb, vb = kp[band, :], vp[band, :]
        q = (q_ref[...] * ATTN_SCALE).astype(BF16)
        do = do_ref[...]
        first = lax.broadcasted_iota(jnp.int32, (QBLK, 2 * HEAD_DIM), 1) < HEAD_DIM
        left = _left_mask(qb)
        q2, do2 = _stack_heads(q, first), _stack_heads(do, first)
        p = _band_softmax(q2, kb, b_ref[...].reshape(2 * QBLK, KBAND), left)
        dp = lax.dot_general(do2, vb, NT, preferred_element_type=F32)
        ds = p * (dp - jnp.sum(p * dp, axis=-1, keepdims=True))
        db_ref[...] += ds.reshape(2, QBLK, KBAND)
        dsb = ds.astype(BF16)
        dq = jnp.dot(dsb, kb, preferred_element_type=F32)
        dq_ref[...] = (jnp.where(first, dq[:QBLK], dq[QBLK:]) * ATTN_SCALE).astype(BF16)
        dka[:, band] += lax.dot_general(q2, dsb, TN, preferred_element_type=F32)
        dva[:, band] += lax.dot_general(do2, p.astype(BF16), TN, preferred_element_type=F32)

        @pl.when(qb == nqb - 1)
        def _():
            dk_ref[...] = dka[:, pl.ds(PAD, S)].T.astype(BF16)
            dv_ref[...] = dva[:, pl.ds(PAD, S)].T.astype(BF16)

    qspec = pl.BlockSpec((None, QBLK, 128), lambda h, b, i: (b, i, h))
    kvout = pl.BlockSpec((None, S, 128), lambda h, b, i: (b, 0, h))
    bspec = pl.BlockSpec((2, QBLK, KBAND), lambda h, b, i: (h, 0, 0))
    act = jax.ShapeDtypeStruct((B, S, D_ATTN), BF16)
    return pl.pallas_call(
        body, name=name, grid=(nh2, B, nqb),
        in_specs=[qspec,
                  pl.BlockSpec((None, S, 128), lambda h, b, i: (b, 0, nh2 + h)),
                  pl.BlockSpec((None, S, 128), lambda h, b, i: (b, 0, 2 * nh2 + h)),
                  bspec, qspec],
        out_specs=[qspec, kvout, kvout, bspec],
        out_shape=[act, act, act, jax.ShapeDtypeStruct((ATTN_HEADS, QBLK, KBAND), F32)],
        scratch_shapes=[pltpu.VMEM((PAD + S, 128), BF16), pltpu.VMEM((PAD + S, 128), BF16),
                        pltpu.VMEM((128, PAD + S), F32), pltpu.VMEM((128, PAD + S), F32)],
        compiler_params=_params(dimension_semantics=("arbitrary", "arbitrary", "arbitrary")),
    )(z, z, z, bias, dcat)


def _bias_grad(name, dbias, after):
    width = KBAND + QBLK

    def body(d_ref, after_ref, o_ref):
        acc = jnp.zeros((1, width), F32)
        for i in range(QBLK):
            row = jnp.concatenate([d_ref[pl.ds(i, 1), :], jnp.zeros((1, QBLK), F32)], axis=1)
            shift = QBLK - 1 - i
            acc = acc + (pltpu.roll(row, shift, 1) if shift else row)
        o_ref[...] = acc

    return pl.pallas_call(
        body, name=name, grid=(ATTN_HEADS,),
        in_specs=[pl.BlockSpec((None, QBLK, KBAND), lambda h: (h, 0, 0)), pl.BlockSpec(memory_space=pl.ANY)],
        out_specs=pl.BlockSpec((None, 1, width), lambda h: (h, 0, 0)),
        out_shape=jax.ShapeDtypeStruct((ATTN_HEADS, 1, width), F32),
        compiler_params=_params(dimension_semantics=("parallel",)),
    )(dbias, after)


def _rel_grad_from_diagonals(diag):
    top = PAD + QBLK - 1 - REL_CLIP
    sat = jnp.sum(diag[:, :top + 1], axis=1, keepdims=True)
    mid = diag[:, top + 1:top + 2 * REL_CLIP][:, ::-1]
    return jnp.concatenate([jnp.zeros_like(sat), mid, sat], axis=1)


def _shift_rows(x, k, forward):
    S = x.shape[0]
    t = lax.broadcasted_iota(jnp.int32, x.shape, 0)
    if forward:
        return jnp.where(t < S - k, pltpu.roll(x, S - k, 0), 0.0)
    return jnp.where(t >= k, pltpu.roll(x, k, 0), 0.0)


def _window_sum(x, g, forward):
    s = x + _shift_rows(x, 1, forward)
    out = s
    for n, k in enumerate((2, 4, 8)):
        s = s + _shift_rows(s, k, forward)
        out = jnp.where(g > n, s, out)
    return out


def _pool_count(S, g):
    t = lax.broadcasted_iota(jnp.int32, (S, 1), 0)
    w = jnp.left_shift(2, g)
    return jnp.minimum(t + 1, w).astype(F32)


def _pool_fwd(name, z, wp, pscale, mixed):
    B, S, _ = z.shape
    c0 = 3 * D_ATTN // POOL_GROUP
    y0 = D_ATTN // POOL_GROUP

    def body(u_ref, w_ref, s_ref, mixed_ref, d_ref, y_ref):
        g = pl.program_id(1)
        u = u_ref[...]
        d = (_window_sum(u, g, False) / _pool_count(S, g) - u).astype(BF16)
        d_ref[...] = d
        y_ref[...] = (jnp.dot(d, w_ref[...], preferred_element_type=F32) * s_ref[...]).astype(BF16)

    return pl.pallas_call(
        body, name=name, grid=(B, len(POOL_WINDOWS)),
        in_specs=[pl.BlockSpec((None, S, POOL_GROUP), lambda b, g: (b, 0, c0 + g)),
                  pl.BlockSpec((None, POOL_GROUP, POOL_GROUP), lambda b, g: (g, 0, 0)),
                  pl.BlockSpec((1, POOL_GROUP), lambda b, g: (0, g)),
                  pl.BlockSpec(memory_space=pl.ANY)],
        out_specs=[pl.BlockSpec((None, S, POOL_GROUP), lambda b, g: (b, 0, g)),
                   pl.BlockSpec((None, S, POOL_GROUP), lambda b, g: (b, 0, y0 + g))],
        out_shape=[jax.ShapeDtypeStruct((B, S, D_POOL), BF16), jax.ShapeDtypeStruct(mixed.shape, BF16)],
        input_output_aliases={3: 1},
        compiler_params=_params(dimension_semantics=("parallel", "parallel")),
    )(z, wp, pscale, mixed)


def _pool_bwd(name, d, wp, pscale, dcat):
    B, S, _ = d.shape
    c0 = D_ATTN // POOL_GROUP

    def body(d_ref, w_ref, s_ref, dy_ref, du_ref, dw_ref, dsc_ref):
        g, b = pl.program_id(0), pl.program_id(1)
        dv = d_ref[...]
        dy = dy_ref[...].astype(F32)
        w = w_ref[...]
        ypre = jnp.dot(dv, w, preferred_element_type=F32)
        dyp = (dy * s_ref[...]).astype(BF16)
        dd = lax.dot_general(dyp, w, NT, preferred_element_type=F32)
        du_ref[...] = (_window_sum(dd / _pool_count(S, g), g, True) - dd).astype(BF16)
        dw = lax.dot_general(dv, dyp, TN, preferred_element_type=F32)
        dsc = jnp.sum(dy * ypre, axis=0, keepdims=True)

        @pl.when(b == 0)
        def _():
            dw_ref[...] = dw
            dsc_ref[...] = dsc

        @pl.when(b > 0)
        def _():
            dw_ref[...] += dw
            dsc_ref[...] += dsc

    blk = pl.BlockSpec((None, S, POOL_GROUP), lambda g, b: (b, 0, g))
    wspec = pl.BlockSpec((None, POOL_GROUP, POOL_GROUP), lambda g, b: (g, 0, 0))
    sspec = pl.BlockSpec((1, POOL_GROUP), lambda g, b: (0, g))
    return pl.pallas_call(
        body, name=name, grid=(len(POOL_WINDOWS), B),
        in_specs=[blk, wspec, sspec, pl.BlockSpec((None, S, POOL_GROUP), lambda g, b: (b, 0, c0 + g))],
        out_specs=[blk, wspec, sspec],
        out_shape=[jax.ShapeDtypeStruct((B, S, D_POOL), BF16),
                   jax.ShapeDtypeStruct((len(POOL_WINDOWS), POOL_GROUP, POOL_GROUP), F32),
                   jax.ShapeDtypeStruct((1, D_POOL), F32)],
        compiler_params=_params(dimension_semantics=("arbitrary", "arbitrary")),
    )(d, wp, pscale, dcat)


def _cross_softmax(q, k):
    s = lax.dot_general(q, k, NT, preferred_element_type=F32) * (CROSS_DIM ** -0.5)
    e = jnp.exp(s - jnp.max(s, axis=-1, keepdims=True))
    return e * (1.0 / jnp.sum(e, axis=-1, keepdims=True))


def _cross_fwd(name, qc, kv, tq=1024):
    B, S, _ = qc.shape
    M = kv.shape[1]
    tq = _tile(S, tq)

    def body(q_ref, k_ref, v_ref, o_ref):
        p = _cross_softmax(q_ref[...], k_ref[...])
        o_ref[...] = jnp.dot(p.astype(BF16), v_ref[...], preferred_element_type=F32).astype(BF16)

    qspec = pl.BlockSpec((None, tq, CROSS_DIM), lambda b, h, i: (b, i, h))
    return pl.pallas_call(
        body, name=name, grid=(B, CROSS_HEADS, S // tq),
        in_specs=[qspec, pl.BlockSpec((None, M, CROSS_DIM), lambda b, h, i: (b, 0, h)),
                  pl.BlockSpec((None, M, CROSS_DIM), lambda b, h, i: (b, 0, CROSS_HEADS + h))],
        out_specs=qspec, out_shape=jax.ShapeDtypeStruct((B, S, D_CROSS), BF16),
        compiler_params=_params(dimension_semantics=("parallel", "parallel", "parallel")),
    )(qc, kv, kv)


def _cross_bwd(name, qc, kv, do, tq=1024):
    B, S, _ = qc.shape
    M = kv.shape[1]
    tq = _tile(S, tq)
    nq = S // tq
    scale = CROSS_DIM ** -0.5

    def body(q_ref, k_ref, v_ref, do_ref, dq_ref, dk_ref, dv_ref, dka, dva):
        i = pl.program_id(2)
        q, k, v, dov = q_ref[...], k_ref[...], v_ref[...], do_ref[...]
        p = _cross_softmax(q, k)
        dp = lax.dot_general(dov, v, NT, preferred_element_type=F32)
        ds = ((p * (dp - jnp.sum(p * dp, axis=-1, keepdims=True))) * scale).astype(BF16)
        dq_ref[...] = jnp.dot(ds, k, preferred_element_type=F32).astype(BF16)
        dk = lax.dot_general(ds, q, TN, preferred_element_type=F32)
        dv = lax.dot_general(p.astype(BF16), dov, TN, preferred_element_type=F32)

        @pl.when(i == 0)
        def _():
            dka[...] = dk
            dva[...] = dv

        @pl.when(i > 0)
        def _():
            dka[...] += dk
            dva[...] += dv

        @pl.when(i == nq - 1)
        def _():
            dk_ref[...] = dka[...].astype(BF16)
            dv_ref[...] = dva[...].astype(BF16)

    qspec = pl.BlockSpec((None, tq, CROSS_DIM), lambda b, h, i: (b, i, h))
    kspec = pl.BlockSpec((None, M, CROSS_DIM), lambda b, h, i: (b, 0, h))
    return pl.pallas_call(
        body, name=name, grid=(B, CROSS_HEADS, nq),
        in_specs=[qspec, kspec, pl.BlockSpec((None, M, CROSS_DIM), lambda b, h, i: (b, 0, CROSS_HEADS + h)), qspec],
        out_specs=[qspec, kspec, kspec],
        out_shape=[jax.ShapeDtypeStruct((B, S, D_CROSS), BF16), jax.ShapeDtypeStruct((B, M, D_CROSS), BF16),
                   jax.ShapeDtypeStruct((B, M, D_CROSS), BF16)],
        scratch_shapes=[pltpu.VMEM((M, CROSS_DIM), F32), pltpu.VMEM((M, CROSS_DIM), F32)],
        compiler_params=_params(dimension_semantics=("parallel", "parallel", "arbitrary")),
    )(qc, kv, kv, do)


def _local_step(x, mem, target, small, weights, emit, start_token=None):
    B, S, D = x.shape
    T = B * S
    x2, t2 = x.reshape(T, D), target.reshape(T, D)
    mem2 = mem.reshape(-1, D)
    n_mem = mem.shape[1]
    wts = {}

    hn1 = _rms_fwd("norm_ffn1", x2, small["ffn1_norm"], after=start_token)
    memn = _rms_fwd("norm_mem", mem2, small["mem_norm"])
    bias = _bias_tile("bias_tile", small["rel_bias"])
    wts.update(weights(0, [hn1, memn, bias]))
    g1 = _ffn_gate("ffn1_gate", hn1, wts["ffn1_w_gate"])
    wts.update(weights(1, g1))
    u1, a1 = _ffn_up_act("ffn1_up", hn1, wts["ffn1_w_up"], g1)
    wts.update(weights(2, a1))
    h1 = _ffn_out("ffn1_down", a1, wts["ffn1_w_down"], res=x2, scale=FFN_RES)
    wts.update(weights(3, h1))
    z, hn2 = _mm_nn_cols("mix_in", h1, wts["w_in"], F32, norm=small["mix_norm"])
    z = z.reshape(B, S, -1)
    mixed = _attn_fwd("attn_fwd", z, bias)
    d_pool, mixed = _pool_fwd("pool_fwd", z, wts["w_pool"], small["pool_scale"], mixed)
    cat = mixed.reshape(T, -1)
    h2 = _mm_nn("mix_out", cat, wts["w_out"], F32, res=h1)
    wts.update(weights(4, h2))
    qc, hn3 = _mm_nn("cross_q", h2, wts["w_cq"], BF16, norm=small["cross_norm"])
    kv = _mm_nn("cross_kv", memn, wts["w_ckv"], BF16)
    o = _cross_fwd("cross_fwd", qc.reshape(B, S, -1), kv.reshape(B, n_mem, -1)).reshape(T, -1)
    h3 = _mm_nn("cross_out", o, wts["w_co"], F32, res=h2)
    wts.update(weights(5, h3))
    hn4, g2, u2, a2 = _ffn_up("ffn2_up", h3, small["ffn2_norm"], wts["ffn2_w_gate"], wts["ffn2_w_up"])
    h4 = _ffn_out("ffn2_down", a2, wts["ffn2_w_down"], res=h3, scale=FFN_RES)

    gs = {}
    loss_part, dh4, gs["final_norm"] = _loss_and_grad("loss", h4, small["final_norm"], t2)

    def ffn_bwd(tag, dhb, h_in, hn, g, u, a, wg, wu, wd, gain, out_dtype=BF16):
        tok = emit({tag + "_w_down": _ffn_dw(tag + "_dwd", a, dhb, scale=FFN_RES)})
        dg, du = _ffn_dact(tag + "_dact", dhb, wd, g, u, after=tok)
        tok = emit({tag + "_w_gate": _ffn_dw(tag + "_dwg", dg, hn)})
        tok = emit({tag + "_w_up": _ffn_dw(tag + "_dwu", du, hn, after=tok)})
        dhn = _ffn_out(tag + "_dhn_g", dg, wg, after=tok)
        dhn = _ffn_out(tag + "_dhn_u", du, wu, res=dhn, out_dtype=BF16)
        return _rms_bwd(tag + "_dnorm", h_in, gain, dhn, skip=dhb, out_dtype=out_dtype)

    dh3b, gs["ffn2_norm"] = ffn_bwd("ffn2", dh4, h3, hn4, g2, u2, a2, wts["ffn2_w_gate"],
                                    wts["ffn2_w_up"], wts["ffn2_w_down"], small["ffn2_norm"])
    do = _mm_nt("cross_do", dh3b, wts["w_co"], BF16)
    gw = {"w_co": _mm_tn("cross_dwo", o, dh3b, tm=D_CROSS, col_blocks=N_DEV)}
    dqc, dk, dv = _cross_bwd("cross_bwd", qc.reshape(B, S, -1), kv.reshape(B, n_mem, -1), do.reshape(B, S, -1))
    dqc = dqc.reshape(T, -1)
    dkv = jnp.concatenate([dk, dv], axis=-1).reshape(B * n_mem, -1)
    gw["w_cq"] = _mm_tn("cross_dwq", hn3, dqc, tn=D_CROSS)
    gw["w_ckv"] = _mm_tn("cross_dwkv", memn, dkv)
    tok = emit(gw)
    dhn3 = _mm_nt("cross_dhn", dqc, wts["w_cq"], BF16, after=tok)
    dh2b, gs["cross_norm"] = _rms_bwd("cross_dnorm", h2, small["cross_norm"], dhn3, skip=dh3b)
    dcat = _mm_nt("mix_dcat", dh2b, wts["w_out"], BF16)
    gw = {"w_out": _mm_tn("mix_dwout", cat, dh2b)}
    dcat3 = dcat.reshape(B, S, -1)
    dq, dkk, dvv, dbias = _attn_bwd("attn_bwd", z, bias, dcat3)
    du, gw["w_pool"], gs["pool_scale"] = _pool_bwd("pool_bwd", d_pool, wts["w_pool"], small["pool_scale"], dcat3)
    dz = jnp.concatenate([dq, dkk, dvv, du], axis=-1).reshape(T, -1)
    gw["w_in"] = _mm_tn("mix_dwin", hn2, dz, col_blocks=N_DEV)
    tok = emit(gw)
    dhn2 = _mm_nt_cols("mix_dhn", dz, wts["w_in"], BF16, after=tok)
    dh1b, gs["mix_norm"] = _rms_bwd("mix_dnorm", h1, small["mix_norm"], dhn2, skip=dh2b)
    dx, gs["ffn1_norm"] = ffn_bwd("ffn1", dh1b, x2, hn1, g1, u1, a1, wts["ffn1_w_gate"],
                                  wts["ffn1_w_up"], wts["ffn1_w_down"], small["ffn1_norm"], out_dtype=F32)
    gs["rel_bias"] = _rel_grad_from_diagonals(_bias_grad("bias_grad", dbias, after=dx)[:, 0, :])
    dmemn = _mm_nt("cross_dmem", dkv, wts["w_ckv"], F32, tm=512, after=dx)
    _, gs["mem_norm"] = _rms_bwd("mem_dnorm", mem2, small["mem_norm"], dmemn)
    return loss_part, dx.reshape(B, S, D), gs


def _position():
    return lax.axis_index("x"), lax.axis_index("y"), lax.axis_index("c")


def _index(p):
    return 4 * p[0] + 2 * p[1] + p[2]


HBM_SPEC = pl.BlockSpec(memory_space=pltpu.HBM)
SEM_SPEC = pl.BlockSpec(memory_space=pltpu.SEMAPHORE)
ANY_SPEC = pl.BlockSpec(memory_space=pl.ANY)
ORDERED_EFFECT = pltpu.SideEffectType.DATAFLOW_SIDE_EFFECTING


N_COPIES = {"grads": N_DEV - 1, "spread": 4, "relay": 3}
N_SEMS = {"grads": N_DEV, "spread": 5, "relay": 3}


def _copies(pattern, srcs, lands, send, recv, base=0):
    x, y, c = _position()
    me, sibling = _index((x, y, c)), (x, y, 1 - c)
    chips = [(1 - x, y), (x, 1 - y), (1 - x, 1 - y)]
    if pattern == "grads":
        targets = [(x ^ (k >> 2), y ^ ((k >> 1) & 1), c ^ (k & 1)) for k in range(1, N_DEV)]
    else:
        targets = [sibling] + [(*chip, c) for chip in chips]
    per, slots, out = N_COPIES[pattern], N_SEMS[pattern], []
    for a in range(len(lands)):
        for k in range(per):
            if pattern == "relay":
                src = dst = lands[a].at[_index((*chips[k], c))]
                to = sibling
            else:
                to = targets[k]
                src = srcs[a].at[_index(to)] if pattern == "grads" else srcs[a]
                dst = lands[a].at[me]
            slot = base + a * slots + k
            out.append(pltpu.make_async_remote_copy(src_ref=src, dst_ref=dst, send_sem=send.at[slot],
                                                    recv_sem=recv.at[slot], device_id=to, device_id_type=MESH))
    return out


def _own_copies(pattern, srcs, lands, send, base=0):
    if pattern == "relay":
        return []
    x, y, c = _position()
    me = _index((x, y, c))
    slots = N_SEMS[pattern]
    return [pltpu.make_async_copy(srcs[a].at[me] if pattern == "grads" else srcs[a], lands[a].at[me],
                                  send.at[base + a * slots + slots - 1]) for a in range(len(lands))]


def _comm(name, srcs, lands, wait=None, start=None, after=None):
    after = [] if after is None else list(after) if isinstance(after, (list, tuple)) else [after]
    ns, nl = len(srcs), len(lands)
    na = ns + nl
    arrays = list(srcs) + list(lands)
    n_wait = 2 if wait else 0
    n_start = 2 if start else 0

    def body(*refs):
        ins, lnd = refs[:ns], refs[ns:na]
        if wait:
            base = wait[3] if len(wait) > 3 else 0
            for cp in _copies(wait[0], ins, lnd, refs[na], refs[na + 1], base):
                cp.wait_send()
                cp.wait_recv()
            for cp in _own_copies(wait[0], ins, lnd, refs[na], base):
                cp.wait()
        if start:
            outs = refs[na + n_wait + len(after):]
            for cp in _copies(start, ins, lnd, outs[0], outs[1]) + _own_copies(start, ins, lnd, outs[0]):
                cp.start()
            refs[-1][...] = jnp.zeros((8, 128), F32)

    out_shape, out_specs = [], []
    if start:
        sems = pltpu.SemaphoreType.DMA((nl * N_SEMS[start],))
        out_shape += [sems, sems]
        out_specs += [SEM_SPEC, SEM_SPEC]
    out_shape += [pltpu.HBM(a.shape, a.dtype) for a in arrays]
    out_specs += [HBM_SPEC] * na
    if start:
        out_shape.append(jax.ShapeDtypeStruct((8, 128), F32))
        out_specs.append(pl.BlockSpec(memory_space=pltpu.VMEM))
    operands = [pltpu.with_memory_space_constraint(a, pltpu.HBM) for a in arrays]
    operands += list(wait[1:3]) if wait else []
    operands += after
    res = pl.pallas_call(
        body, name=name, out_shape=out_shape, out_specs=out_specs,
        in_specs=[HBM_SPEC] * na + [SEM_SPEC] * n_wait + [ANY_SPEC] * len(after),
        input_output_aliases={i: n_start + i for i in range(na)},
        compiler_params=pltpu.CompilerParams(has_side_effects=ORDERED_EFFECT),
    )(*operands)
    res = list(res)
    thru = res[n_start:n_start + na]
    return thru[:ns], thru[ns:], (tuple(res[:2]) if start else None), (res[-1] if start else None)


def _adamw_math(w, g, m, v):
    m = ADAM_B1 * m + (1.0 - ADAM_B1) * g
    v = ADAM_B2 * v + (1.0 - ADAM_B2) * (g * g)
    m_hat = m / (1.0 - ADAM_B1 ** ADAM_STEP)
    v_hat = v / (1.0 - ADAM_B2 ** ADAM_STEP)
    delta = -ADAM_LR * (m_hat / (jnp.sqrt(v_hat) + ADAM_EPS) + ADAM_WD * w)
    return delta, m, v


def _adamw(name, parts, w, m, v):
    R, C = w.shape
    tr = _tile(R, max(16, ADAM_STEP_ELEMS // C))

    def body(p_ref, w_ref, m_ref, v_ref, g_out, d_out, m_out, v_out):
        g = p_ref[0].astype(F32)
        for d in range(1, N_DEV):
            g = g + p_ref[d].astype(F32)
        g_out[...] = g
        d_out[...], m_out[...], v_out[...] = _adamw_math(w_ref[...], g, m_ref[...], v_ref[...])

    row = pl.BlockSpec((tr, C), lambda i: (i, 0))
    out = jax.ShapeDtypeStruct((R, C), F32)
    return pl.pallas_call(
        body, name=name, grid=(R // tr,),
        in_specs=[pl.BlockSpec((N_DEV, tr, C), lambda i: (0, i, 0)), row, row, row],
        out_specs=[row] * 4, out_shape=[out] * 4,
        compiler_params=_params(dimension_semantics=("parallel",)),
    )(parts, w, m, v)


def _small_allreduce_adamw(name, g, w, m, v, after):
    R = g.shape[0]

    def body(g_ref, w_ref, m_ref, v_ref, after_ref, g_out, d_out, m_out, v_out, land, send, recv):
        x, y, c = _position()
        me = _index((x, y, c))
        land[me] = g_ref[...]
        copies = []
        for k in range(1, N_DEV):
            peer = (x ^ (k >> 2), y ^ ((k >> 1) & 1), c ^ (k & 1))
            copies.append(pltpu.make_async_remote_copy(
                src_ref=g_ref, dst_ref=land.at[me], send_sem=send.at[k - 1], recv_sem=recv.at[k - 1],
                device_id=peer, device_id_type=MESH))
        for cp in copies:
            cp.start()
        for cp in copies:
            cp.wait()
        total = land[0]
        for d in range(1, N_DEV):
            total = total + land[d]
        g_out[...] = total
        d_out[...], m_out[...], v_out[...] = _adamw_math(w_ref[...], total, m_ref[...], v_ref[...])

    vm = pl.BlockSpec(memory_space=pltpu.VMEM)
    out = jax.ShapeDtypeStruct((R, 128), F32)
    return pl.pallas_call(
        body, name=name, in_specs=[vm] * 4 + [ANY_SPEC], out_specs=[vm] * 4, out_shape=[out] * 4,
        scratch_shapes=[pltpu.VMEM((N_DEV, R, 128), F32), pltpu.SemaphoreType.DMA((7,)),
                        pltpu.SemaphoreType.DMA((7,))],
    )(g, w, m, v, after)


BIG = ("ffn1_w_gate", "ffn1_w_up", "ffn1_w_down", "w_in", "w_pool", "w_out", "w_cq", "w_ckv", "w_co",
       "ffn2_w_gate", "ffn2_w_up", "ffn2_w_down")
SMALL = ("ffn1_norm", "mix_norm", "rel_bias", "pool_scale", "cross_norm", "mem_norm", "ffn2_norm", "final_norm")
ORDER = ("ffn1_norm", "ffn1_w_gate", "ffn1_w_up", "ffn1_w_down", "mix_norm", "w_in", "rel_bias", "w_pool",
         "pool_scale", "w_out", "cross_norm", "mem_norm", "w_cq", "w_ckv", "w_co", "ffn2_norm", "ffn2_w_gate",
         "ffn2_w_up", "ffn2_w_down", "final_norm")
TRANSPOSED = ("ffn1_w_gate", "ffn1_w_up", "ffn2_w_gate", "ffn2_w_up")
ROW_SHARDED = TRANSPOSED + ("ffn1_w_down", "ffn2_w_down", "w_out", "w_cq", "w_ckv")
GATHER_GROUPS = (("ffn1_w_gate",), ("ffn1_w_up",), ("ffn1_w_down",), ("w_in", "w_pool", "w_out"),
                 ("w_cq", "w_ckv", "w_co"), ("ffn2_w_gate", "ffn2_w_up", "ffn2_w_down"))
RELAY_BEFORE_USE = ((0,), (1,), (2,), (3,), (4, 5), ())


def _pack(arrays):
    flat = jnp.concatenate([a.reshape(-1) for a in arrays])
    rows = -(-flat.shape[0] // 1024) * 8
    return jnp.pad(flat, (0, rows * 128 - flat.shape[0])).reshape(rows, 128)


def _unpack(packed, like):
    flat, out, at = packed.reshape(-1), [], 0
    for a in like:
        out.append(flat[at:at + a.size].reshape(a.shape))
        at += a.size
    return out


def _shard2d(a):
    a = a[0]
    return a.reshape(-1, a.shape[-1])


def kernel(x, mem, ffn1_norm, ffn1_w_gate, ffn1_w_up, ffn1_w_down, mix_norm, w_in, rel_bias, w_pool, pool_scale, w_out, cross_norm, mem_norm, w_cq, w_ckv, w_co, ffn2_norm, ffn2_w_gate, ffn2_w_up, ffn2_w_down, final_norm, loss_target, m_ffn1_norm, m_ffn1_w_gate, m_ffn1_w_up, m_ffn1_w_down, m_mix_norm, m_w_in, m_rel_bias, m_w_pool, m_pool_scale, m_w_out, m_cross_norm, m_mem_norm, m_w_cq, m_w_ckv, m_w_co, m_ffn2_norm, m_ffn2_w_gate, m_ffn2_w_up, m_ffn2_w_down, m_final_norm, v_ffn1_norm, v_ffn1_w_gate, v_ffn1_w_up, v_ffn1_w_down, v_mix_norm, v_w_in, v_rel_bias, v_w_pool, v_pool_scale, v_w_out, v_cross_norm, v_mem_norm, v_w_cq, v_w_ckv, v_w_co, v_ffn2_norm, v_ffn2_w_gate, v_ffn2_w_up, v_ffn2_w_down, v_final_norm):
    args = dict(locals())
    def view(n, a):
        return a.transpose(0, 2, 1) if n in TRANSPOSED else a

    w_in_ = {n: view(n, args[n]) for n in ORDER}
    m_in = {n: view(n, args["m_" + n]) for n in ORDER}
    v_in = {n: view(n, args["v_" + n]) for n in ORDER}

    n_g, rows = len(POOL_WINDOWS), POOL_GROUP // N_DEV

    def landing(block_shape, dtype):
        return lax.empty((N_DEV,) + tuple(block_shape), dtype)

    gathers, tok = [], None
    for first in (True, False):
        groups = GATHER_GROUPS[:1] if first else GATHER_GROUPS[1:]
        shards = [_shard2d(w_in_[n]) for group in groups for n in group]
        shards = [(s if tok is None else s + tok[0, 0]).astype(BF16) for s in shards]
        srcs, lands, sems, tok = _comm("gather_start_%d" % (not first), shards,
                                       [landing(s.shape, BF16) for s in shards], start="spread", after=tok)
        at = 0
        for group in groups:
            gathers.append((srcs[at:at + len(group)], lands[at:at + len(group)], sems + (at * N_SEMS["spread"],)))
            at += len(group)

    def weights(gi, after):
        for ri in RELAY_BEFORE_USE[gi]:
            srcs, lands, sems = gathers[ri]
            _, lands, sems, after = _comm("gather_relay_%d" % ri, srcs, lands, wait=("spread",) + sems,
                                          start="relay", after=after)
            gathers[ri] = (None, lands, sems)
        _, lands, sems = gathers[gi]
        _, lands, _, _ = _comm("gather_finish_%d" % gi, [], lands, wait=("relay",) + sems, after=after)
        out = {}
        for n, full in zip(GATHER_GROUPS[gi], lands):
            if n == "w_pool":
                full = full.reshape(N_DEV, n_g, rows, POOL_GROUP).transpose(1, 0, 2, 3).reshape(n_g, POOL_GROUP, POOL_GROUP)
            if n == "w_co":
                full = full.transpose(1, 0, 2).reshape(full.shape[1], -1)
            out[n] = full.reshape(-1, full.shape[-1]) if n in ROW_SHARDED else full
        return out

    scatters = []

    def emit(gw):
        names = list(gw)
        stacks = []
        for n in names:
            g = gw[n]
            if n == "w_pool":
                g = g.reshape(n_g, N_DEV, rows, POOL_GROUP).transpose(1, 0, 2, 3).astype(BF16)
            stacks.append(g.reshape((N_DEV,) + _shard2d(w_in_[n]).shape))
        lands = [landing(s.shape[1:], s.dtype) for s in stacks]
        srcs, lands, sems, token = _comm("grads_start_%d" % len(scatters), stacks, lands, start="grads")
        scatters.append((names, srcs, lands, sems))
        return token

    small = {n: w_in_[n].reshape(1, -1) for n in SMALL if n != "rel_bias"}
    small["rel_bias"] = rel_bias[0]
    loss_part, grad_x, gs = _local_step(x, mem, loss_target, small, weights, emit, start_token=tok)

    grad, delta, new_m, new_v = {}, {}, {}, {}
    after = grad_x
    for si, (names, srcs, lands, sems) in enumerate(scatters):
        _, landed, _, _ = _comm("grads_finish_%d" % si, srcs, lands, wait=("grads",) + sems, after=after)
        for n, parts in zip(names, landed):
            res = _adamw("adamw_" + n, parts, _shard2d(w_in_[n]), _shard2d(m_in[n]), _shard2d(v_in[n]))
            grad[n], delta[n], new_m[n], new_v[n] = [view(n, r.reshape(w_in_[n].shape)) for r in res]
        after = res[0]

    slot = jnp.zeros((1,), F32)
    like = [w_in_[n] for n in SMALL] + [slot]
    gs["rel_bias"] = gs["rel_bias"].reshape(rel_bias.shape)
    res = _small_allreduce_adamw("small_params", _pack([gs[n] for n in SMALL] + [loss_part[0, :1]]), _pack(like),
                                 _pack([m_in[n] for n in SMALL] + [slot]), _pack([v_in[n] for n in SMALL] + [slot]),
                                 after=after)
    for d, packed in zip((grad, delta, new_m, new_v), res):
        for n, a in zip(SMALL, _unpack(packed, like)):
            d[n] = a
    loss = _unpack(res[0], like)[-1][0]
    return (loss, grad_x, *[grad[n] for n in ORDER], *[delta[n] for n in ORDER],
            *[new_m[n] for n in ORDER], *[new_v[n] for n in ORDER])
```

```python
import jax
import jax.numpy as jnp
from jax import lax
from jax.experimental import pallas as pl
from jax.experimental.pallas import tpu as pltpu

F32 = jnp.float32
BF16 = jnp.bfloat16

N_DEV = 8
EPS = 1e-6
NEG_INF = -1e30
CHUNK = 64
LEFT_CHUNKS = 8
PAD = LEFT_CHUNKS * CHUNK
QBLK = 4 * CHUNK
KBAND = PAD + QBLK
REL_CLIP = 128
ATTN_HEADS = 16
HEAD_DIM = 64
D_ATTN = ATTN_HEADS * HEAD_DIM
POOL_WINDOWS = (2, 4, 8, 16)
POOL_GROUP = 256
D_POOL = len(POOL_WINDOWS) * POOL_GROUP
CROSS_HEADS = 4
CROSS_DIM = 128
D_CROSS = CROSS_HEADS * CROSS_DIM
FFN_RES = 0.5
ADAM_LR, ADAM_B1, ADAM_B2, ADAM_EPS, ADAM_WD, ADAM_STEP = 0.001, 0.9, 0.999, 1e-08, 0.01, 10

NN = (((1,), (0,)), ((), ()))
NT = (((1,), (1,)), ((), ()))
TN = (((0,), (0,)), ((), ()))
MESH = pl.DeviceIdType.MESH
VMEM_LIMIT = 56 * 1024 * 1024
ADAM_STEP_ELEMS = 384 * 1024


def _params(**kw):
    return pltpu.CompilerParams(vmem_limit_bytes=VMEM_LIMIT, **kw)


def _bf(v):
    return v if v.dtype == BF16 else v.astype(BF16)


WHOLE = ((Ellipsis,), (Ellipsis,))


def _rms(xv, gain):
    return (xv * lax.rsqrt(jnp.mean(xv * xv, axis=-1, keepdims=True) + EPS)) * gain


def _gemm(name, a, a_spec, b, b_spec, dims, grid, outs, chunks=(WHOLE,), extras=(), epilogue=None, after=None,
          norm=None):
    nex, nout = len(extras), len(outs)
    first_out = 2 + nex + (after is not None) + (norm is not None)

    def body(*refs):
        a_ref, b_ref = refs[:2]
        if norm is not None:
            hn_out, a_ref = refs[first_out + nout], refs[-1]

            @pl.when(pl.program_id(1) == 0)
            def _():
                hn = _rms(refs[0][...], refs[first_out - 1][...]).astype(BF16)
                a_ref[...] = hn
                hn_out[...] = hn

        total = None
        for ia, ib in chunks:
            d = lax.dot_general(_bf(a_ref[ia]), _bf(b_ref[ib]), dims, preferred_element_type=F32)
            total = d if total is None else total + d
        vals = epilogue(total, *[e[...] for e in refs[2:2 + nex]]) if epilogue is not None else (total,)
        for r, v in zip(refs[first_out:first_out + nout], vals):
            r[...] = v.astype(r.dtype)

    operands = [a, b] + [x for x, _, _ in extras]
    in_specs = [pl.BlockSpec(*a_spec), pl.BlockSpec(*b_spec)] + [pl.BlockSpec(blk, m) for _, blk, m in extras]
    if after is not None:
        operands.append(after)
        in_specs.append(pl.BlockSpec(memory_space=pl.ANY))
    out_specs = [pl.BlockSpec(blk, m) for _, _, blk, m in outs]
    out_shape = [jax.ShapeDtypeStruct(s, d) for s, d, _, _ in outs]
    scratch = []
    if norm is not None:
        operands.append(norm)
        in_specs.append(pl.BlockSpec(norm.shape, lambda i, j: (0, 0)))
        out_specs.append(pl.BlockSpec(*a_spec))
        out_shape.append(jax.ShapeDtypeStruct(a.shape, BF16))
        scratch.append(pltpu.VMEM(a_spec[0], BF16))
    res = pl.pallas_call(
        body, name=name, grid=grid, in_specs=in_specs, out_specs=out_specs, out_shape=out_shape,
        scratch_shapes=scratch,
        compiler_params=_params(dimension_semantics=("parallel", "arbitrary" if norm is not None else "parallel")),
    )(*operands)
    return res[0] if len(res) == 1 else res


def _tile(n, want):
    for t in range(min(n, want), 15, -1):
        if n % t == 0 and t % 16 == 0:
            return t
    return n


def _mm_nn(name, a, b, out_dtype, res=None, tm=1024, tn=1024, norm=None):
    M, K = a.shape
    N = b.shape[1]
    tm, tn = _tile(M, tm), _tile(N, tn)
    extras = [] if res is None else [(res, (tm, tn), lambda i, j: (i, j))]
    epi = None if res is None else (lambda t, r: (r + t,))
    return _gemm(name, a, ((tm, K), lambda i, j: (i, 0)), b, ((K, tn), lambda i, j: (0, j)), NN,
                 (M // tm, N // tn), [((M, N), out_dtype, (tm, tn), lambda i, j: (i, j))], extras=extras, epilogue=epi,
                 norm=norm)


def _mm_nn_cols(name, a, bs, out_dtype, res=None, tm=1024, norm=None):
    M, K = a.shape
    nb, _, w = bs.shape
    tm = _tile(M, tm)
    extras = [] if res is None else [(res, (tm, w), lambda i, j: (i, j))]
    epi = None if res is None else (lambda t, r: (r + t,))
    return _gemm(name, a, ((tm, K), lambda i, j: (i, 0)), bs, ((None, K, w), lambda i, j: (j, 0, 0)), NN,
                 (M // tm, nb), [((M, nb * w), out_dtype, (tm, w), lambda i, j: (i, j))], extras=extras, epilogue=epi,
                 norm=norm)


def _mm_nt(name, a, b, out_dtype, tm=1024, tn=1024, after=None):
    M, K = a.shape
    N = b.shape[0]
    tm, tn = _tile(M, tm), _tile(N, tn)
    return _gemm(name, a, ((tm, K), lambda i, j: (i, 0)), b, ((tn, K), lambda i, j: (j, 0)), NT,
                 (M // tm, N // tn), [((M, N), out_dtype, (tm, tn), lambda i, j: (i, j))], after=after)


def _mm_nt_cols(name, a, bs, out_dtype, tm=1024, tn=512, after=None):
    M = a.shape[0]
    nb, N, w = bs.shape
    tm, tn = _tile(M, tm), _tile(N, tn)
    chunks = [((slice(None), pl.ds(c * w, w)), (c,)) for c in range(nb)]
    return _gemm(name, a, ((tm, nb * w), lambda i, j: (i, 0)), bs, ((nb, tn, w), lambda i, j: (0, j, 0)), NT,
                 (M // tm, N // tn), [((M, N), out_dtype, (tm, tn), lambda i, j: (i, j))], chunks, after=after)


def _mm_tn(name, a, b, tm=1024, tn=1024, col_blocks=None):
    T, Ka = a.shape
    Nb = b.shape[1]
    tm = _tile(Ka, tm)
    if col_blocks is None:
        tn = _tile(Nb, tn)
        out = ((Ka, Nb), BF16, (tm, tn), lambda i, j: (i, j))
    else:
        tn = Nb // col_blocks
        out = ((col_blocks, Ka, tn), BF16, (None, tm, tn), lambda i, j: (j, i, 0))
    return _gemm(name, a, ((T, tm), lambda i, j: (0, i)), b, ((T, tn), lambda i, j: (0, j)), TN,
                 (Ka // tm, Nb // tn), [out])


def _hidden_block(wt):
    return 2 * wt.shape[0] // N_DEV


def _ffn_out(name, a, wt, res=None, scale=1.0, tm=1024, tn=512, after=None, out_dtype=F32):
    M, F = a.shape
    N = wt.shape[1]
    tm, tn = _tile(M, tm), _tile(N, tn)
    extras = [] if res is None else [(res, (tm, tn), lambda i, j: (i, j))]
    epi = None if res is None else (lambda t, r: (r + scale * t,))
    return _gemm(name, a, ((tm, F), lambda i, j: (i, 0)), wt, ((F, tn), lambda i, j: (0, j)), NN,
                 (M // tm, N // tn), [((M, N), out_dtype, (tm, tn), lambda i, j: (i, j))], extras=extras,
                 epilogue=epi, after=after)


def _ffn_dact(name, dhb, wd, g, u, tm=1024, after=None):
    M, K = dhb.shape
    F = wd.shape[0]
    w = _hidden_block(wd)
    tm = _tile(M, tm)
    tr = _tile(tm, 256)
    tokens = [] if after is None else [after]

    def body(dh_ref, wd_ref, g_ref, u_ref, *rest):
        dg_ref, du_ref = rest[-2:]
        pieces = [pl.ds(r * tr, tr) for r in range(tm // tr)]

        def product(rows):
            return lax.dot_general(dh_ref[rows, :], wd_ref[...], NT, preferred_element_type=F32)

        results = []
        dact = product(pieces[0])
        for n, rows in enumerate(pieces):
            ahead = product(pieces[n + 1]) if n + 1 < len(pieces) else None
            results.append(_swiglu_bwd(dact, g_ref[rows, :], u_ref[rows, :]))
            dact = ahead
        for rows, (dg, du) in zip(pieces, results):
            dg_ref[rows, :] = dg.astype(BF16)
            du_ref[rows, :] = du.astype(BF16)

    hid = pl.BlockSpec((tm, w), lambda i, j: (i, j))
    return pl.pallas_call(
        body, name=name, grid=(M // tm, F // w),
        in_specs=[pl.BlockSpec((tm, K), lambda i, j: (i, 0)), pl.BlockSpec((w, K), lambda i, j: (j, 0)), hid, hid]
        + [pl.BlockSpec(memory_space=pl.ANY) for t in tokens],
        out_specs=[hid, hid], out_shape=[jax.ShapeDtypeStruct((M, F), BF16)] * 2,
        compiler_params=_params(dimension_semantics=("parallel", "parallel")),
    )(dhb, wd, g, u, *tokens)


def _ffn_dw(name, a, b, scale=1.0, tn=512, after=None):
    T, F = a.shape
    N = b.shape[1]
    w = 2 * F // N_DEV
    tn = _tile(N, tn)
    epi = None if scale == 1.0 else (lambda t: (t * scale,))
    return _gemm(name, a, ((T, w), lambda i, j: (0, i)), b, ((T, tn), lambda i, j: (0, j)), TN,
                 (F // w, N // tn), [((F, N), BF16, (w, tn), lambda i, j: (i, j))], epilogue=epi, after=after)


def _ffn_gate(name, hn, wgt, tm=1024):
    M, K = hn.shape
    F = wgt.shape[0]
    w = _hidden_block(wgt)
    tm = _tile(M, tm)
    return _gemm(name, hn, ((tm, K), lambda i, j: (i, 0)), wgt, ((w, K), lambda i, j: (j, 0)), NT,
                 (M // tm, F // w), [((M, F), BF16, (tm, w), lambda i, j: (i, j))])


def _ffn_up_act(name, hn, wut, g, tm=1024):
    M, K = hn.shape
    F = wut.shape[0]
    w = _hidden_block(wut)
    tm = _tile(M, tm)
    hid = ((tm, w), lambda i, j: (i, j))

    def epilogue(u, gate):
        gate = gate.astype(F32)
        return u, gate * jax.nn.sigmoid(gate) * u

    return _gemm(name, hn, ((tm, K), lambda i, j: (i, 0)), wut, ((w, K), lambda i, j: (j, 0)), NT,
                 (M // tm, F // w), [((M, F), BF16) + hid] * 2, extras=[(g,) + hid], epilogue=epilogue)


def _ffn_up(name, h, gain, wgt, wut, tm=512):
    M, K = h.shape
    F = wgt.shape[0]
    w = _hidden_block(wgt)
    tm = _tile(M, tm)

    def body(h_ref, gain_ref, g_ref, u_ref, hn_ref, og, ou, oa, a_ref):
        @pl.when(pl.program_id(1) == 0)
        def _():
            hn = _rms(h_ref[...], gain_ref[...]).astype(BF16)
            a_ref[...] = hn
            hn_ref[...] = hn

        a = a_ref[...]
        g = lax.dot_general(a, g_ref[...], NT, preferred_element_type=F32)
        u = lax.dot_general(a, u_ref[...], NT, preferred_element_type=F32)
        og[...] = g.astype(BF16)
        ou[...] = u.astype(BF16)
        oa[...] = (g * jax.nn.sigmoid(g) * u).astype(BF16)

    rows = pl.BlockSpec((tm, K), lambda i, j: (i, 0))
    wspec = pl.BlockSpec((w, K), lambda i, j: (j, 0))
    ospec = pl.BlockSpec((tm, w), lambda i, j: (i, j))
    return pl.pallas_call(
        body, name=name, grid=(M // tm, F // w),
        in_specs=[rows, pl.BlockSpec((1, K), lambda i, j: (0, 0)), wspec, wspec],
        out_specs=[rows] + [ospec] * 3,
        out_shape=[jax.ShapeDtypeStruct((M, K), BF16)] + [jax.ShapeDtypeStruct((M, F), BF16)] * 3,
        scratch_shapes=[pltpu.VMEM((tm, K), BF16)],
        compiler_params=_params(dimension_semantics=("parallel", "arbitrary")),
    )(h, gain, wgt, wut)


def _swiglu_bwd(dact, g, u):
    g = g.astype(F32)
    u = u.astype(F32)
    sig = jax.nn.sigmoid(g)
    silu = g * sig
    d = FFN_RES * dact
    return d * u * (sig * (1.0 + g * (1.0 - sig))), d * silu


def _rms_fwd(name, x, gain, tr=512, after=None):
    R, D = x.shape
    tr = _tile(R, tr)

    def body(x_ref, g_ref, *rest):
        rest[-1][...] = _rms(x_ref[...], g_ref[...]).astype(BF16)

    tokens = [] if after is None else [after]
    return pl.pallas_call(
        body, name=name, grid=(R // tr,),
        in_specs=[pl.BlockSpec((tr, D), lambda i: (i, 0)), pl.BlockSpec((1, D), lambda i: (0, 0))]
        + [pl.BlockSpec(memory_space=pl.ANY) for t in tokens],
        out_specs=pl.BlockSpec((tr, D), lambda i: (i, 0)), out_shape=jax.ShapeDtypeStruct((R, D), BF16),
        compiler_params=_params(dimension_semantics=("parallel",)),
    )(x, gain, *tokens)


def _rms_bwd_math(xv, gain, dy):
    rstd = lax.rsqrt(jnp.mean(xv * xv, axis=-1, keepdims=True) + EPS)
    xhat = xv * rstd
    dxh = dy * gain
    dx = rstd * (dxh - xhat * jnp.mean(dxh * xhat, axis=-1, keepdims=True))
    return dx, jnp.sum(dy * xhat, axis=0, keepdims=True)


def _rms_bwd(name, x, gain, dy, skip=None, out_dtype=BF16, tr=512):
    R, D = x.shape
    tr = _tile(R, tr)
    has_skip = skip is not None

    def body(*refs):
        x_ref, g_ref, dy_ref = refs[:3]
        dx_ref, dg_ref = refs[-2:]
        dx, dg = _rms_bwd_math(x_ref[...], g_ref[...], dy_ref[...].astype(F32))
        if has_skip:
            dx = dx + refs[3][...].astype(F32)
        dx_ref[...] = dx.astype(out_dtype)

        @pl.when(pl.program_id(0) == 0)
        def _():
            dg_ref[...] = dg

        @pl.when(pl.program_id(0) > 0)
        def _():
            dg_ref[...] += dg

    row = pl.BlockSpec((tr, D), lambda i: (i, 0))
    vec = pl.BlockSpec((1, D), lambda i: (0, 0))
    return pl.pallas_call(
        body, name=name, grid=(R // tr,),
        in_specs=[row, vec, row] + ([row] if has_skip else []),
        out_specs=[row, vec],
        out_shape=[jax.ShapeDtypeStruct((R, D), out_dtype), jax.ShapeDtypeStruct((1, D), F32)],
        compiler_params=_params(dimension_semantics=("arbitrary",)),
    )(*([x, gain, dy] + ([skip] if has_skip else [])))


def _loss_and_grad(name, h, gain, target, tr=512):
    R, D = h.shape
    tr = _tile(R, tr)

    def body(h_ref, g_ref, t_ref, loss_ref, dhb_ref, dg_ref):
        hv, gain_v = h_ref[...], g_ref[...]
        y = (hv * lax.rsqrt(jnp.mean(hv * hv, axis=-1, keepdims=True) + EPS)) * gain_v
        err = y - t_ref[...]
        part = jnp.full((8, 128), 0.5 * jnp.sum(jnp.mean(err * err, axis=-1, keepdims=True)), F32)
        dh, dg = _rms_bwd_math(hv, gain_v, err * (1.0 / D))
        dhb_ref[...] = dh.astype(BF16)

        @pl.when(pl.program_id(0) == 0)
        def _():
            dg_ref[...] = dg
            loss_ref[...] = part

        @pl.when(pl.program_id(0) > 0)
        def _():
            dg_ref[...] += dg
            loss_ref[...] += part

    row = pl.BlockSpec((tr, D), lambda i: (i, 0))
    vec = pl.BlockSpec((1, D), lambda i: (0, 0))
    return pl.pallas_call(
        body, name=name, grid=(R // tr,), in_specs=[row, vec, row],
        out_specs=[pl.BlockSpec((8, 128), lambda i: (0, 0)), row, vec],
        out_shape=[jax.ShapeDtypeStruct((8, 128), F32), jax.ShapeDtypeStruct((R, D), BF16),
                   jax.ShapeDtypeStruct((1, D), F32)],
        compiler_params=_params(dimension_semantics=("arbitrary",)),
    )(h, gain, target)


def _bias_tile(name, rel, after=None):
    width = KBAND + QBLK
    sat = rel[:, 2 * REL_CLIP:]
    n_left = PAD - REL_CLIP + 1
    row0 = jnp.concatenate([jnp.broadcast_to(sat, (ATTN_HEADS, n_left)), rel[:, :2 * REL_CLIP][:, ::-1],
                            jnp.broadcast_to(sat, (ATTN_HEADS, width - n_left - 2 * REL_CLIP))], axis=1)

    tokens = [] if after is None else [after]

    def body(e_ref, *rest):
        o_ref = rest[-1]
        rows = pltpu.roll(jnp.broadcast_to(e_ref[...], (QBLK, width)), 0, 1, stride=1, stride_axis=0)
        i = lax.broadcasted_iota(jnp.int32, (QBLK, KBAND), 0) // CHUNK
        j = lax.broadcasted_iota(jnp.int32, (QBLK, KBAND), 1) // CHUNK
        o_ref[...] = jnp.where((j >= i) & (j <= i + LEFT_CHUNKS), rows[:, :KBAND], NEG_INF)

    return pl.pallas_call(
        body, name=name, grid=(ATTN_HEADS,),
        in_specs=[pl.BlockSpec((None, 1, width), lambda h: (h, 0, 0))] + [ANY_SPEC for t in tokens],
        out_specs=pl.BlockSpec((None, QBLK, KBAND), lambda h: (h, 0, 0)),
        out_shape=jax.ShapeDtypeStruct((ATTN_HEADS, QBLK, KBAND), F32),
        compiler_params=_params(dimension_semantics=("parallel",)),
    )(row0.reshape(ATTN_HEADS, 1, width), *tokens)


ATTN_SCALE = HEAD_DIM ** -0.5
ROW_PIECES = 1


def _stack_heads(x, first):
    zero = jnp.zeros_like(x)
    return jnp.concatenate([jnp.where(first, x, zero), jnp.where(first, zero, x)], axis=0)


def _band_softmax(q_half_scaled, kb, bias, left_mask):
    s = lax.dot_general(q_half_scaled, kb, NT, preferred_element_type=F32) + bias + left_mask
    e = jnp.exp(s - jnp.max(s, axis=-1, keepdims=True))
    return e * (1.0 / jnp.sum(e, axis=-1, keepdims=True))


def _left_mask(qb):
    kpos = qb * QBLK - PAD + lax.broadcasted_iota(jnp.int32, (1, KBAND), 1)
    return jnp.where(kpos >= 0, 0.0, NEG_INF).astype(F32)


def _fill_padded(dst, src, S):
    dst[pl.ds(0, PAD), :] = jnp.zeros((PAD, dst.shape[1]), dst.dtype)
    dst[pl.ds(PAD, S), :] = src[...].astype(dst.dtype)


def _attn_fwd(name, z, bias):
    B, S, _ = z.shape
    nh2 = ATTN_HEADS // 2

    def body(q_ref, k_ref, v_ref, b_ref, o_ref, kp, vp):
        qb = pl.program_id(2)

        @pl.when(qb == 0)
        def _():
            _fill_padded(kp, k_ref, S)
            _fill_padded(vp, v_ref, S)

        start = pl.multiple_of(qb * QBLK, QBLK)
        kb, vb = kp[pl.ds(start, KBAND), :], vp[pl.ds(start, KBAND), :]
        q = (q_ref[...] * ATTN_SCALE).astype(BF16)
        first = lax.broadcasted_iota(jnp.int32, (QBLK, 2 * HEAD_DIM), 1) < HEAD_DIM
        left = _left_mask(qb)
        zero = jnp.zeros_like(q)
        qh = [jnp.where(first, q, zero), jnp.where(first, zero, q)]
        rp = QBLK // ROW_PIECES
        chains = [(a, r) for r in range(ROW_PIECES) for a in range(2)]
        ss = [lax.dot_general(qh[a][r * rp:(r + 1) * rp], kb, NT, preferred_element_type=F32) for a, r in chains]
        ps = []
        for (a, r), s in zip(chains, ss):
            s = s + b_ref[a, pl.ds(r * rp, rp), :] + left
            e = jnp.exp(s - jnp.max(s, axis=-1, keepdims=True))
            ps.append((e * (1.0 / jnp.sum(e, axis=-1, keepdims=True))).astype(BF16))
        os_ = [jnp.dot(p, vb, preferred_element_type=F32) for p in ps]
        for r in range(ROW_PIECES):
            o_ref[pl.ds(r * rp, rp), :] = jnp.where(first[:rp], os_[2 * r], os_[2 * r + 1]).astype(BF16)

    return pl.pallas_call(
        body, name=name, grid=(B, nh2, S // QBLK),
        in_specs=[pl.BlockSpec((None, QBLK, 128), lambda b, h, i: (b, i, h)),
                  pl.BlockSpec((None, S, 128), lambda b, h, i: (b, 0, nh2 + h)),
                  pl.BlockSpec((None, S, 128), lambda b, h, i: (b, 0, 2 * nh2 + h)),
                  pl.BlockSpec((2, QBLK, KBAND), lambda b, h, i: (h, 0, 0))],
        out_specs=pl.BlockSpec((None, QBLK, 128), lambda b, h, i: (b, i, h)),
        out_shape=jax.ShapeDtypeStruct((B, S, D_ATTN + D_POOL), BF16),
        scratch_shapes=[pltpu.VMEM((PAD + S, 128), BF16), pltpu.VMEM((PAD + S, 128), BF16)],
        compiler_params=_params(dimension_semantics=("parallel", "parallel", "arbitrary")),
    )(z, z, z, bias)


def _attn_bwd(name, z, bias, dcat):
    B, S, _ = z.shape
    nh2 = ATTN_HEADS // 2
    nqb = S // QBLK

    def body(q_ref, k_ref, v_ref, b_ref, do_ref, dq_ref, dk_ref, dv_ref, db_ref, kp, vp, dka, dva):
        b, qb = pl.program_id(1), pl.program_id(2)

        @pl.when(qb == 0)
        def _():
            _fill_padded(kp, k_ref, S)
            _fill_padded(vp, v_ref, S)
            dka[...] = jnp.zeros_like(dka)
            dva[...] = jnp.zeros_like(dva)

        @pl.when((qb == 0) & (b == 0))
        def _():
            db_ref[...] = jnp.zeros_like(db_ref)

        start = pl.multiple_of(qb * QBLK, QBLK)
        band = pl.ds(start, KBAND)
        kb, vb = kp[band, :], vp[band, :]
        q = (q_ref[...] * ATTN_SCALE).astype(BF16)
        do = do_ref[...]
        first = lax.broadcasted_iota(jnp.int32, (QBLK, 2 * HEAD_DIM), 1) < HEAD_DIM
        left = _left_mask(qb)
        q2, do2 = _stack_heads(q, first), _stack_heads(do, first)
        p = _band_softmax(q2, kb, b_ref[...].reshape(2 * QBLK, KBAND), left)
        dp = lax.dot_general(do2, vb, NT, preferred_element_type=F32)
        ds = p * (dp - jnp.sum(p * dp, axis=-1, keepdims=True))
        db_ref[...] += ds.reshape(2, QBLK, KBAND)
        dsb = ds.astype(BF16)
        dq = jnp.dot(dsb, kb, preferred_element_type=F32)
        dq_ref[...] = (jnp.where(first, dq[:QBLK], dq[QBLK:]) * ATTN_SCALE).astype(BF16)
        dka[:, band] += lax.dot_general(q2, dsb, TN, preferred_element_type=F32)
        dva[:, band] += lax.dot_general(do2, p.astype(BF16), TN, preferred_element_type=F32)

        @pl.when(qb == nqb - 1)
        def _():
            dk_ref[...] = dka[:, pl.ds(PAD, S)].T.astype(BF16)
            dv_ref[...] = dva[:, pl.ds(PAD, S)].T.astype(BF16)

    qspec = pl.BlockSpec((None, QBLK, 128), lambda h, b, i: (b, i, h))
    kvout = pl.BlockSpec((None, S, 128), lambda h, b, i: (b, 0, h))
    bspec = pl.BlockSpec((2, QBLK, KBAND), lambda h, b, i: (h, 0, 0))
    act = jax.ShapeDtypeStruct((B, S, D_ATTN), BF16)
    return pl.pallas_call(
        body, name=name, grid=(nh2, B, nqb),
        in_specs=[qspec,
                  pl.BlockSpec((None, S, 128), lambda h, b, i: (b, 0, nh2 + h)),
                  pl.BlockSpec((None, S, 128), lambda h, b, i: (b, 0, 2 * nh2 + h)),
                  bspec, qspec],
        out_specs=[qspec, kvout, kvout, bspec],
        out_shape=[act, act, act, jax.ShapeDtypeStruct((ATTN_HEADS, QBLK, KBAND), F32)],
        scratch_shapes=[pltpu.VMEM((PAD + S, 128), BF16), pltpu.VMEM((PAD + S, 128), BF16),
                        pltpu.VMEM((128, PAD + S), F32), pltpu.VMEM((128, PAD + S), F32)],
        compiler_params=_params(dimension_semantics=("arbitrary", "arbitrary", "arbitrary")),
    )(z, z, z, bias, dcat)


def _bias_grad(name, dbias, after):
    width = KBAND + QBLK

    def body(d_ref, after_ref, o_ref):
        acc = jnp.zeros((1, width), F32)
        for i in range(QBLK):
            row = jnp.concatenate([d_ref[pl.ds(i, 1), :], jnp.zeros((1, QBLK), F32)], axis=1)
            shift = QBLK - 1 - i
            acc = acc + (pltpu.roll(row, shift, 1) if shift else row)
        o_ref[...] = acc

    return pl.pallas_call(
        body, name=name, grid=(ATTN_HEADS,),
        in_specs=[pl.BlockSpec((None, QBLK, KBAND), lambda h: (h, 0, 0)), pl.BlockSpec(memory_space=pl.ANY)],
        out_specs=pl.BlockSpec((None, 1, width), lambda h: (h, 0, 0)),
        out_shape=jax.ShapeDtypeStruct((ATTN_HEADS, 1, width), F32),
        compiler_params=_params(dimension_semantics=("parallel",)),
    )(dbias, after)


def _rel_grad_from_diagonals(diag):
    top = PAD + QBLK - 1 - REL_CLIP
    sat = jnp.sum(diag[:, :top + 1], axis=1, keepdims=True)
    mid = diag[:, top + 1:top + 2 * REL_CLIP][:, ::-1]
    return jnp.concatenate([jnp.zeros_like(sat), mid, sat], axis=1)


def _shift_rows(x, k, forward):
    S = x.shape[0]
    t = lax.broadcasted_iota(jnp.int32, x.shape, 0)
    if forward:
        return jnp.where(t < S - k, pltpu.roll(x, S - k, 0), 0.0)
    return jnp.where(t >= k, pltpu.roll(x, k, 0), 0.0)


def _window_sum(x, g, forward):
    s = x + _shift_rows(x, 1, forward)
    out = s
    for n, k in enumerate((2, 4, 8)):
        s = s + _shift_rows(s, k, forward)
        out = jnp.where(g > n, s, out)
    return out


def _pool_count(S, g):
    t = lax.broadcasted_iota(jnp.int32, (S, 1), 0)
    w = jnp.left_shift(2, g)
    return jnp.minimum(t + 1, w).astype(F32)


def _pool_fwd(name, z, wp, pscale, mixed):
    B, S, _ = z.shape
    c0 = 3 * D_ATTN // POOL_GROUP
    y0 = D_ATTN // POOL_GROUP

    def body(u_ref, w_ref, s_ref, mixed_ref, d_ref, y_ref):
        g = pl.program_id(1)
        u = u_ref[...]
        d = (_window_sum(u, g, False) / _pool_count(S, g) - u).astype(BF16)
        d_ref[...] = d
        y_ref[...] = (jnp.dot(d, w_ref[...], preferred_element_type=F32) * s_ref[...]).astype(BF16)

    return pl.pallas_call(
        body, name=name, grid=(B, len(POOL_WINDOWS)),
        in_specs=[pl.BlockSpec((None, S, POOL_GROUP), lambda b, g: (b, 0, c0 + g)),
                  pl.BlockSpec((None, POOL_GROUP, POOL_GROUP), lambda b, g: (g, 0, 0)),
                  pl.BlockSpec((1, POOL_GROUP), lambda b, g: (0, g)),
                  pl.BlockSpec(memory_space=pl.ANY)],
        out_specs=[pl.BlockSpec((None, S, POOL_GROUP), lambda b, g: (b, 0, g)),
                   pl.BlockSpec((None, S, POOL_GROUP), lambda b, g: (b, 0, y0 + g))],
        out_shape=[jax.ShapeDtypeStruct((B, S, D_POOL), BF16), jax.ShapeDtypeStruct(mixed.shape, BF16)],
        input_output_aliases={3: 1},
        compiler_params=_params(dimension_semantics=("parallel", "parallel")),
    )(z, wp, pscale, mixed)


def _pool_bwd(name, d, wp, pscale, dcat):
    B, S, _ = d.shape
    c0 = D_ATTN // POOL_GROUP

    def body(d_ref, w_ref, s_ref, dy_ref, du_ref, dw_ref, dsc_ref):
        g, b = pl.program_id(0), pl.program_id(1)
        dv = d_ref[...]
        dy = dy_ref[...].astype(F32)
        w = w_ref[...]
        ypre = jnp.dot(dv, w, preferred_element_type=F32)
        dyp = (dy * s_ref[...]).astype(BF16)
        dd = lax.dot_general(dyp, w, NT, preferred_element_type=F32)
        du_ref[...] = (_window_sum(dd / _pool_count(S, g), g, True) - dd).astype(BF16)
        dw = lax.dot_general(dv, dyp, TN, preferred_element_type=F32)
        dsc = jnp.sum(dy * ypre, axis=0, keepdims=True)

        @pl.when(b == 0)
        def _():
            dw_ref[...] = dw
            dsc_ref[...] = dsc

        @pl.when(b > 0)
        def _():
            dw_ref[...] += dw
            dsc_ref[...] += dsc

    blk = pl.BlockSpec((None, S, POOL_GROUP), lambda g, b: (b, 0, g))
    wspec = pl.BlockSpec((None, POOL_GROUP, POOL_GROUP), lambda g, b: (g, 0, 0))
    sspec = pl.BlockSpec((1, POOL_GROUP), lambda g, b: (0, g))
    return pl.pallas_call(
        body, name=name, grid=(len(POOL_WINDOWS), B),
        in_specs=[blk, wspec, sspec, pl.BlockSpec((None, S, POOL_GROUP), lambda g, b: (b, 0, c0 + g))],
        out_specs=[blk, wspec, sspec],
        out_shape=[jax.ShapeDtypeStruct((B, S, D_POOL), BF16),
                   jax.ShapeDtypeStruct((len(POOL_WINDOWS), POOL_GROUP, POOL_GROUP), F32),
                   jax.ShapeDtypeStruct((1, D_POOL), F32)],
        compiler_params=_params(dimension_semantics=("arbitrary", "arbitrary")),
    )(d, wp, pscale, dcat)


def _cross_softmax(q, k):
    s = lax.dot_general(q, k, NT, preferred_element_type=F32) * (CROSS_DIM ** -0.5)
    e = jnp.exp(s - jnp.max(s, axis=-1, keepdims=True))
    return e * (1.0 / jnp.sum(e, axis=-1, keepdims=True))


def _cross_fwd(name, qc, kv, tq=1024):
    B, S, _ = qc.shape
    M = kv.shape[1]
    tq = _tile(S, tq)

    def body(q_ref, k_ref, v_ref, o_ref):
        p = _cross_softmax(q_ref[...], k_ref[...])
        o_ref[...] = jnp.dot(p.astype(BF16), v_ref[...], preferred_element_type=F32).astype(BF16)

    qspec = pl.BlockSpec((None, tq, CROSS_DIM), lambda b, h, i: (b, i, h))
    return pl.pallas_call(
        body, name=name, grid=(B, CROSS_HEADS, S // tq),
        in_specs=[qspec, pl.BlockSpec((None, M, CROSS_DIM), lambda b, h, i: (b, 0, h)),
                  pl.BlockSpec((None, M, CROSS_DIM), lambda b, h, i: (b, 0, CROSS_HEADS + h))],
        out_specs=qspec, out_shape=jax.ShapeDtypeStruct((B, S, D_CROSS), BF16),
        compiler_params=_params(dimension_semantics=("parallel", "parallel", "parallel")),
    )(qc, kv, kv)


def _cross_bwd(name, qc, kv, do, tq=1024):
    B, S, _ = qc.shape
    M = kv.shape[1]
    tq = _tile(S, tq)
    nq = S // tq
    scale = CROSS_DIM ** -0.5

    def body(q_ref, k_ref, v_ref, do_ref, dq_ref, dk_ref, dv_ref, dka, dva):
        i = pl.program_id(2)
        q, k, v, dov = q_ref[...], k_ref[...], v_ref[...], do_ref[...]
        p = _cross_softmax(q, k)
        dp = lax.dot_general(dov, v, NT, preferred_element_type=F32)
        ds = ((p * (dp - jnp.sum(p * dp, axis=-1, keepdims=True))) * scale).astype(BF16)
        dq_ref[...] = jnp.dot(ds, k, preferred_element_type=F32).astype(BF16)
        dk = lax.dot_general(ds, q, TN, preferred_element_type=F32)
        dv = lax.dot_general(p.astype(BF16), dov, TN, preferred_element_type=F32)

        @pl.when(i == 0)
        def _():
            dka[...] = dk
            dva[...] = dv

        @pl.when(i > 0)
        def _():
            dka[...] += dk
            dva[...] += dv

        @pl.when(i == nq - 1)
        def _():
            dk_ref[...] = dka[...].astype(BF16)
            dv_ref[...] = dva[...].astype(BF16)

    qspec = pl.BlockSpec((None, tq, CROSS_DIM), lambda b, h, i: (b, i, h))
    kspec = pl.BlockSpec((None, M, CROSS_DIM), lambda b, h, i: (b, 0, h))
    return pl.pallas_call(
        body, name=name, grid=(B, CROSS_HEADS, nq),
        in_specs=[qspec, kspec, pl.BlockSpec((None, M, CROSS_DIM), lambda b, h, i: (b, 0, CROSS_HEADS + h)), qspec],
        out_specs=[qspec, kspec, kspec],
        out_shape=[jax.ShapeDtypeStruct((B, S, D_CROSS), BF16), jax.ShapeDtypeStruct((B, M, D_CROSS), BF16),
                   jax.ShapeDtypeStruct((B, M, D_CROSS), BF16)],
        scratch_shapes=[pltpu.VMEM((M, CROSS_DIM), F32), pltpu.VMEM((M, CROSS_DIM), F32)],
        compiler_params=_params(dimension_semantics=("parallel", "parallel", "arbitrary")),
    )(qc, kv, kv, do)


def _local_step(x, mem, target, small, weights, emit, start_token=None):
    B, S, D = x.shape
    T = B * S
    x2, t2 = x.reshape(T, D), target.reshape(T, D)
    mem2 = mem.reshape(-1, D)
    n_mem = mem.shape[1]
    wts = {}

    hn1 = _rms_fwd("norm_ffn1", x2, small["ffn1_norm"], after=start_token)
    early = {}

    def cover_bias(tok):
        early["bias"] = _bias_tile("bias_tile", small["rel_bias"], after=tok)
        return early["bias"]

    def cover_mem(tok):
        early["memn"] = _rms_fwd("norm_mem", mem2, small["mem_norm"], after=tok)
        return early["memn"]

    def cover_mix(tok):
        early["hn2"] = _rms_fwd("norm_mix", h1, small["mix_norm"], after=tok)
        return early["hn2"]

    wts.update(weights(0, hn1, cover_bias))
    bias = early["bias"]
    g1 = _ffn_gate("ffn1_gate", hn1, wts["ffn1_w_gate"])
    wts.update(weights(1, g1, cover_mem))
    memn = early["memn"]
    u1, a1 = _ffn_up_act("ffn1_up", hn1, wts["ffn1_w_up"], g1)
    wts.update(weights(2, a1))
    h1 = _ffn_out("ffn1_down", a1, wts["ffn1_w_down"], res=x2, scale=FFN_RES)
    wts.update(weights(3, h1, cover_mix))
    hn2 = early["hn2"]
    z = _mm_nn_cols("mix_in", hn2, wts["w_in"], F32)
    z = z.reshape(B, S, -1)
    mixed = _attn_fwd("attn_fwd", z, bias)
    d_pool, mixed = _pool_fwd("pool_fwd", z, wts["w_pool"], small["pool_scale"], mixed)
    cat = mixed.reshape(T, -1)
    h2 = _mm_nn("mix_out", cat, wts["w_out"], F32, res=h1)
    wts.update(weights(4, h2))
    qc, hn3 = _mm_nn("cross_q", h2, wts["w_cq"], BF16, norm=small["cross_norm"])
    kv = _mm_nn("cross_kv", memn, wts["w_ckv"], BF16)
    o = _cross_fwd("cross_fwd", qc.reshape(B, S, -1), kv.reshape(B, n_mem, -1)).reshape(T, -1)
    h3 = _mm_nn("cross_out", o, wts["w_co"], F32, res=h2)
    wts.update(weights(5, h3))
    hn4, g2, u2, a2 = _ffn_up("ffn2_up", h3, small["ffn2_norm"], wts["ffn2_w_gate"], wts["ffn2_w_up"])
    h4 = _ffn_out("ffn2_down", a2, wts["ffn2_w_down"], res=h3, scale=FFN_RES)

    gs = {}
    loss_part, dh4, gs["final_norm"] = _loss_and_grad("loss", h4, small["final_norm"], t2)

    def ffn_bwd(tag, dhb, h_in, hn, g, u, a, wg, wu, wd, gain, out_dtype=BF16):
        tok = emit({tag + "_w_down": _ffn_dw(tag + "_dwd", a, dhb, scale=FFN_RES)})
        dg, du = _ffn_dact(tag + "_dact", dhb, wd, g, u, after=tok)
        tok = emit({tag + "_w_gate": _ffn_dw(tag + "_dwg", dg, hn)})
        tok = emit({tag + "_w_up": _ffn_dw(tag + "_dwu", du, hn, after=tok)})
        dhn = _ffn_out(tag + "_dhn_g", dg, wg, after=tok)
        dhn = _ffn_out(tag + "_dhn_u", du, wu, res=dhn, out_dtype=BF16)
        return _rms_bwd(tag + "_dnorm", h_in, gain, dhn, skip=dhb, out_dtype=out_dtype)

    dh3b, gs["ffn2_norm"] = ffn_bwd("ffn2", dh4, h3, hn4, g2, u2, a2, wts["ffn2_w_gate"],
                                    wts["ffn2_w_up"], wts["ffn2_w_down"], small["ffn2_norm"])
    do = _mm_nt("cross_do", dh3b, wts["w_co"], BF16)
    gw = {"w_co": _mm_tn("cross_dwo", o, dh3b, tm=D_CROSS, col_blocks=N_DEV)}
    dqc, dk, dv = _cross_bwd("cross_bwd", qc.reshape(B, S, -1), kv.reshape(B, n_mem, -1), do.reshape(B, S, -1))
    dqc = dqc.reshape(T, -1)
    dkv = jnp.concatenate([dk, dv], axis=-1).reshape(B * n_mem, -1)
    gw["w_cq"] = _mm_tn("cross_dwq", hn3, dqc, tn=D_CROSS)
    gw["w_ckv"] = _mm_tn("cross_dwkv", memn, dkv)
    tok = emit(gw)
    dhn3 = _mm_nt("cross_dhn", dqc, wts["w_cq"], BF16, after=tok)
    dh2b, gs["cross_norm"] = _rms_bwd("cross_dnorm", h2, small["cross_norm"], dhn3, skip=dh3b)
    dcat = _mm_nt("mix_dcat", dh2b, wts["w_out"], BF16)
    gw = {"w_out": _mm_tn("mix_dwout", cat, dh2b)}
    dcat3 = dcat.reshape(B, S, -1)
    dq, dkk, dvv, dbias = _attn_bwd("attn_bwd", z, bias, dcat3)
    du, gw["w_pool"], gs["pool_scale"] = _pool_bwd("pool_bwd", d_pool, wts["w_pool"], small["pool_scale"], dcat3)
    dz = jnp.concatenate([dq, dkk, dvv, du], axis=-1).reshape(T, -1)
    gw["w_in"] = _mm_tn("mix_dwin", hn2, dz, col_blocks=N_DEV)
    tok = emit(gw)
    dhn2 = _mm_nt_cols("mix_dhn", dz, wts["w_in"], BF16, after=tok)
    dh1b, gs["mix_norm"] = _rms_bwd("mix_dnorm", h1, small["mix_norm"], dhn2, skip=dh2b)
    dx, gs["ffn1_norm"] = ffn_bwd("ffn1", dh1b, x2, hn1, g1, u1, a1, wts["ffn1_w_gate"],
                                  wts["ffn1_w_up"], wts["ffn1_w_down"], small["ffn1_norm"], out_dtype=F32)
    gs["rel_bias"] = _rel_grad_from_diagonals(_bias_grad("bias_grad", dbias, after=dx)[:, 0, :])
    dmemn = _mm_nt("cross_dmem", dkv, wts["w_ckv"], F32, tm=512, after=dx)
    _, gs["mem_norm"] = _rms_bwd("mem_dnorm", mem2, small["mem_norm"], dmemn)
    return loss_part, dx.reshape(B, S, D), gs


def _position():
    return lax.axis_index("x"), lax.axis_index("y"), lax.axis_index("c")


def _index(p):
    return 4 * p[0] + 2 * p[1] + p[2]


HBM_SPEC = pl.BlockSpec(memory_space=pltpu.HBM)
SEM_SPEC = pl.BlockSpec(memory_space=pltpu.SEMAPHORE)
ANY_SPEC = pl.BlockSpec(memory_space=pl.ANY)
ORDERED_EFFECT = pltpu.SideEffectType.DATAFLOW_SIDE_EFFECTING


N_COPIES = {"grads": N_DEV - 1, "spread": 4, "relay": 3}
N_SEMS = {"grads": N_DEV, "spread": 5, "relay": 3}


def _copies(pattern, srcs, lands, send, recv, base=0):
    x, y, c = _position()
    me, sibling = _index((x, y, c)), (x, y, 1 - c)
    chips = [(1 - x, y), (x, 1 - y), (1 - x, 1 - y)]
    if pattern == "grads":
        targets = [(x ^ (k >> 2), y ^ ((k >> 1) & 1), c ^ (k & 1)) for k in range(1, N_DEV)]
    else:
        targets = [sibling] + [(*chip, c) for chip in chips]
    per, slots, out = N_COPIES[pattern], N_SEMS[pattern], []
    for a in range(len(lands)):
        for k in range(per):
            if pattern == "relay":
                src = dst = lands[a].at[_index((*chips[k], c))]
                to = sibling
            else:
                to = targets[k]
                src = srcs[a].at[_index(to)] if pattern == "grads" else srcs[a]
                dst = lands[a].at[me]
            slot = base + a * slots + k
            out.append(pltpu.make_async_remote_copy(src_ref=src, dst_ref=dst, send_sem=send.at[slot],
                                                    recv_sem=recv.at[slot], device_id=to, device_id_type=MESH))
    return out


def _own_copies(pattern, srcs, lands, send, base=0):
    if pattern == "relay":
        return []
    x, y, c = _position()
    me = _index((x, y, c))
    slots = N_SEMS[pattern]
    return [pltpu.make_async_copy(srcs[a].at[me] if pattern == "grads" else srcs[a], lands[a].at[me],
                                  send.at[base + a * slots + slots - 1]) for a in range(len(lands))]


def _comm(name, srcs, lands, wait=None, start=None, after=None):
    after = [] if after is None else list(after) if isinstance(after, (list, tuple)) else [after]
    ns, nl = len(srcs), len(lands)
    na = ns + nl
    arrays = list(srcs) + list(lands)
    n_wait = 2 if wait else 0
    n_start = 2 if start else 0

    def body(*refs):
        ins, lnd = refs[:ns], refs[ns:na]
        if wait:
            base = wait[3] if len(wait) > 3 else 0
            for cp in _copies(wait[0], ins, lnd, refs[na], refs[na + 1], base):
                cp.wait_send()
                cp.wait_recv()
            for cp in _own_copies(wait[0], ins, lnd, refs[na], base):
                cp.wait()
        if start:
            outs = refs[na + n_wait + len(after):]
            for cp in _copies(start, ins, lnd, outs[0], outs[1]) + _own_copies(start, ins, lnd, outs[0]):
                cp.start()
            refs[-1][...] = jnp.zeros((8, 128), F32)

    out_shape, out_specs = [], []
    if start:
        sems = pltpu.SemaphoreType.DMA((nl * N_SEMS[start],))
        out_shape += [sems, sems]
        out_specs += [SEM_SPEC, SEM_SPEC]
    out_shape += [pltpu.HBM(a.shape, a.dtype) for a in arrays]
    out_specs += [HBM_SPEC] * na
    if start:
        out_shape.append(jax.ShapeDtypeStruct((8, 128), F32))
        out_specs.append(pl.BlockSpec(memory_space=pltpu.VMEM))
    operands = [pltpu.with_memory_space_constraint(a, pltpu.HBM) for a in arrays]
    operands += list(wait[1:3]) if wait else []
    operands += after
    res = pl.pallas_call(
        body, name=name, out_shape=out_shape, out_specs=out_specs,
        in_specs=[HBM_SPEC] * na + [SEM_SPEC] * n_wait + [ANY_SPEC] * len(after),
        input_output_aliases={i: n_start + i for i in range(na)},
        compiler_params=pltpu.CompilerParams(has_side_effects=ORDERED_EFFECT),
    )(*operands)
    res = list(res)
    thru = res[n_start:n_start + na]
    return thru[:ns], thru[ns:], (tuple(res[:2]) if start else None), (res[-1] if start else None)


def _adamw_math(w, g, m, v):
    m = ADAM_B1 * m + (1.0 - ADAM_B1) * g
    v = ADAM_B2 * v + (1.0 - ADAM_B2) * (g * g)
    m_hat = m / (1.0 - ADAM_B1 ** ADAM_STEP)
    v_hat = v / (1.0 - ADAM_B2 ** ADAM_STEP)
    delta = -ADAM_LR * (m_hat / (jnp.sqrt(v_hat) + ADAM_EPS) + ADAM_WD * w)
    return delta, m, v


def _adamw(name, parts, w, m, v):
    R, C = w.shape
    tr = _tile(R, max(16, ADAM_STEP_ELEMS // C))

    def body(p_ref, w_ref, m_ref, v_ref, g_out, d_out, m_out, v_out):
        g = p_ref[0].astype(F32)
        for d in range(1, N_DEV):
            g = g + p_ref[d].astype(F32)
        g_out[...] = g
        d_out[...], m_out[...], v_out[...] = _adamw_math(w_ref[...], g, m_ref[...], v_ref[...])

    row = pl.BlockSpec((tr, C), lambda i: (i, 0))
    out = jax.ShapeDtypeStruct((R, C), F32)
    return pl.pallas_call(
        body, name=name, grid=(R // tr,),
        in_specs=[pl.BlockSpec((N_DEV, tr, C), lambda i: (0, i, 0)), row, row, row],
        out_specs=[row] * 4, out_shape=[out] * 4,
        compiler_params=_params(dimension_semantics=("parallel",)),
    )(parts, w, m, v)


def _small_allreduce_adamw(name, g, w, m, v, after):
    R = g.shape[0]

    def body(g_ref, w_ref, m_ref, v_ref, after_ref, g_out, d_out, m_out, v_out, land, send, recv):
        x, y, c = _position()
        me = _index((x, y, c))
        land[me] = g_ref[...]
        copies = []
        for k in range(1, N_DEV):
            peer = (x ^ (k >> 2), y ^ ((k >> 1) & 1), c ^ (k & 1))
            copies.append(pltpu.make_async_remote_copy(
                src_ref=g_ref, dst_ref=land.at[me], send_sem=send.at[k - 1], recv_sem=recv.at[k - 1],
                device_id=peer, device_id_type=MESH))
        for cp in copies:
            cp.start()
        for cp in copies:
            cp.wait()
        total = land[0]
        for d in range(1, N_DEV):
            total = total + land[d]
        g_out[...] = total
        d_out[...], m_out[...], v_out[...] = _adamw_math(w_ref[...], total, m_ref[...], v_ref[...])

    vm = pl.BlockSpec(memory_space=pltpu.VMEM)
    out = jax.ShapeDtypeStruct((R, 128), F32)
    return pl.pallas_call(
        body, name=name, in_specs=[vm] * 4 + [ANY_SPEC], out_specs=[vm] * 4, out_shape=[out] * 4,
        scratch_shapes=[pltpu.VMEM((N_DEV, R, 128), F32), pltpu.SemaphoreType.DMA((7,)),
                        pltpu.SemaphoreType.DMA((7,))],
    )(g, w, m, v, after)


BIG = ("ffn1_w_gate", "ffn1_w_up", "ffn1_w_down", "w_in", "w_pool", "w_out", "w_cq", "w_ckv", "w_co",
       "ffn2_w_gate", "ffn2_w_up", "ffn2_w_down")
SMALL = ("ffn1_norm", "mix_norm", "rel_bias", "pool_scale", "cross_norm", "mem_norm", "ffn2_norm", "final_norm")
ORDER = ("ffn1_norm", "ffn1_w_gate", "ffn1_w_up", "ffn1_w_down", "mix_norm", "w_in", "rel_bias", "w_pool",
         "pool_scale", "w_out", "cross_norm", "mem_norm", "w_cq", "w_ckv", "w_co", "ffn2_norm", "ffn2_w_gate",
         "ffn2_w_up", "ffn2_w_down", "final_norm")
TRANSPOSED = ("ffn1_w_gate", "ffn1_w_up", "ffn2_w_gate", "ffn2_w_up")
ROW_SHARDED = TRANSPOSED + ("ffn1_w_down", "ffn2_w_down", "w_out", "w_cq", "w_ckv")
GATHER_GROUPS = (("ffn1_w_gate",), ("ffn1_w_up",), ("ffn1_w_down",), ("w_in", "w_pool", "w_out"),
                 ("w_cq", "w_ckv", "w_co"), ("ffn2_w_gate", "ffn2_w_up", "ffn2_w_down"))
RELAY_BEFORE_USE = ((0,), (1,), (2,), (3,), (4, 5), ())


def _pack(arrays):
    flat = jnp.concatenate([a.reshape(-1) for a in arrays])
    rows = -(-flat.shape[0] // 1024) * 8
    return jnp.pad(flat, (0, rows * 128 - flat.shape[0])).reshape(rows, 128)


def _unpack(packed, like):
    flat, out, at = packed.reshape(-1), [], 0
    for a in like:
        out.append(flat[at:at + a.size].reshape(a.shape))
        at += a.size
    return out


def _shard2d(a):
    a = a[0]
    return a.reshape(-1, a.shape[-1])


def kernel(x, mem, ffn1_norm, ffn1_w_gate, ffn1_w_up, ffn1_w_down, mix_norm, w_in, rel_bias, w_pool, pool_scale, w_out, cross_norm, mem_norm, w_cq, w_ckv, w_co, ffn2_norm, ffn2_w_gate, ffn2_w_up, ffn2_w_down, final_norm, loss_target, m_ffn1_norm, m_ffn1_w_gate, m_ffn1_w_up, m_ffn1_w_down, m_mix_norm, m_w_in, m_rel_bias, m_w_pool, m_pool_scale, m_w_out, m_cross_norm, m_mem_norm, m_w_cq, m_w_ckv, m_w_co, m_ffn2_norm, m_ffn2_w_gate, m_ffn2_w_up, m_ffn2_w_down, m_final_norm, v_ffn1_norm, v_ffn1_w_gate, v_ffn1_w_up, v_ffn1_w_down, v_mix_norm, v_w_in, v_rel_bias, v_w_pool, v_pool_scale, v_w_out, v_cross_norm, v_mem_norm, v_w_cq, v_w_ckv, v_w_co, v_ffn2_norm, v_ffn2_w_gate, v_ffn2_w_up, v_ffn2_w_down, v_final_norm):
    args = dict(locals())
    def view(n, a):
        return a.transpose(0, 2, 1) if n in TRANSPOSED else a

    w_in_ = {n: view(n, args[n]) for n in ORDER}
    m_in = {n: view(n, args["m_" + n]) for n in ORDER}
    v_in = {n: view(n, args["v_" + n]) for n in ORDER}

    n_g, rows = len(POOL_WINDOWS), POOL_GROUP // N_DEV

    def landing(block_shape, dtype):
        return lax.empty((N_DEV,) + tuple(block_shape), dtype)

    gathers, tok = [], None
    for first in (True, False):
        groups = GATHER_GROUPS[:1] if first else GATHER_GROUPS[1:]
        shards = [_shard2d(w_in_[n]) for group in groups for n in group]
        shards = [(s if tok is None else s + tok[0, 0]).astype(BF16) for s in shards]
        srcs, lands, sems, tok = _comm("gather_start_%d" % (not first), shards,
                                       [landing(s.shape, BF16) for s in shards], start="spread", after=tok)
        at = 0
        for group in groups:
            gathers.append((srcs[at:at + len(group)], lands[at:at + len(group)], sems + (at * N_SEMS["spread"],)))
            at += len(group)

    def weights(gi, after, cover=None):
        for ri in RELAY_BEFORE_USE[gi]:
            srcs, lands, sems = gathers[ri]
            _, lands, sems, after = _comm("gather_relay_%d" % ri, srcs, lands, wait=("spread",) + sems,
                                          start="relay", after=after)
            gathers[ri] = (None, lands, sems)
        if cover is not None:
            after = cover(after)
        _, lands, sems = gathers[gi]
        _, lands, _, _ = _comm("gather_finish_%d" % gi, [], lands, wait=("relay",) + sems, after=after)
        out = {}
        for n, full in zip(GATHER_GROUPS[gi], lands):
            if n == "w_pool":
                full = full.reshape(N_DEV, n_g, rows, POOL_GROUP).transpose(1, 0, 2, 3).reshape(n_g, POOL_GROUP, POOL_GROUP)
            if n == "w_co":
                full = full.transpose(1, 0, 2).reshape(full.shape[1], -1)
            out[n] = full.reshape(-1, full.shape[-1]) if n in ROW_SHARDED else full
        return out

    scatters = []

    def emit(gw):
        names = list(gw)
        stacks = []
        for n in names:
            g = gw[n]
            if n == "w_pool":
                g = g.reshape(n_g, N_DEV, rows, POOL_GROUP).transpose(1, 0, 2, 3).astype(BF16)
            stacks.append(g.reshape((N_DEV,) + _shard2d(w_in_[n]).shape))
        lands = [landing(s.shape[1:], s.dtype) for s in stacks]
        srcs, lands, sems, token = _comm("grads_start_%d" % len(scatters), stacks, lands, start="grads")
        scatters.append((names, srcs, lands, sems))
        return token

    small = {n: w_in_[n].reshape(1, -1) for n in SMALL if n != "rel_bias"}
    small["rel_bias"] = rel_bias[0]
    loss_part, grad_x, gs = _local_step(x, mem, loss_target, small, weights, emit, start_token=tok)

    grad, delta, new_m, new_v = {}, {}, {}, {}
    after = grad_x
    for si, (names, srcs, lands, sems) in enumerate(scatters):
        _, landed, _, _ = _comm("grads_finish_%d" % si, srcs, lands, wait=("grads",) + sems, after=after)
        for n, parts in zip(names, landed):
            res = _adamw("adamw_" + n, parts, _shard2d(w_in_[n]), _shard2d(m_in[n]), _shard2d(v_in[n]))
            grad[n], delta[n], new_m[n], new_v[n] = [view(n, r.reshape(w_in_[n].shape)) for r in res]
        after = res[0]

    slot = jnp.zeros((1,), F32)
    like = [w_in_[n] for n in SMALL] + [slot]
    gs["rel_bias"] = gs["rel_bias"].reshape(rel_bias.shape)
    res = _small_allreduce_adamw("small_params", _pack([gs[n] for n in SMALL] + [loss_part[0, :1]]), _pack(like),
                                 _pack([m_in[n] for n in SMALL] + [slot]), _pack([v_in[n] for n in SMALL] + [slot]),
                                 after=after)
    for d, packed in zip((grad, delta, new_m, new_v), res):
        for n, a in zip(SMALL, _unpack(packed, like)):
            d[n] = a
    loss = _unpack(res[0], like)[-1][0]
    return (loss, grad_x, *[grad[n] for n in ORDER], *[delta[n] for n in ORDER],
            *[new_m[n] for n in ORDER], *[new_v[n] for n in ORDER])
```

```python
import jax
import jax.numpy as jnp
from jax import lax
from jax.experimental import pallas as pl
from jax.experimental.pallas import tpu as pltpu

F32 = jnp.float32
BF16 = jnp.bfloat16

N_DEV = 8
EPS = 1e-6
NEG_INF = -1e30
CHUNK = 64
LEFT_CHUNKS = 8
PAD = LEFT_CHUNKS * CHUNK
QBLK = 4 * CHUNK
KBAND = PAD + QBLK
REL_CLIP = 128
ATTN_HEADS = 16
HEAD_DIM = 64
D_ATTN = ATTN_HEADS * HEAD_DIM
POOL_WINDOWS = (2, 4, 8, 16)
POOL_GROUP = 256
D_POOL = len(POOL_WINDOWS) * POOL_GROUP
CROSS_HEADS = 4
CROSS_DIM = 128
D_CROSS = CROSS_HEADS * CROSS_DIM
FFN_RES = 0.5
ADAM_LR, ADAM_B1, ADAM_B2, ADAM_EPS, ADAM_WD, ADAM_STEP = 0.001, 0.9, 0.999, 1e-08, 0.01, 10

NN = (((1,), (0,)), ((), ()))
NT = (((1,), (1,)), ((), ()))
TN = (((0,), (0,)), ((), ()))
MESH = pl.DeviceIdType.MESH
VMEM_LIMIT = 56 * 1024 * 1024
ADAM_STEP_ELEMS = 384 * 1024


def _params(**kw):
    return pltpu.CompilerParams(vmem_limit_bytes=VMEM_LIMIT, **kw)


def _bf(v):
    return v if v.dtype == BF16 else v.astype(BF16)


WHOLE = ((Ellipsis,), (Ellipsis,))


def _rms(xv, gain):
    return (xv * lax.rsqrt(jnp.mean(xv * xv, axis=-1, keepdims=True) + EPS)) * gain


def _gemm(name, a, a_spec, b, b_spec, dims, grid, outs, chunks=(WHOLE,), extras=(), epilogue=None, after=None,
          norm=None):
    nex, nout = len(extras), len(outs)
    first_out = 2 + nex + (after is not None) + (norm is not None)

    def body(*refs):
        a_ref, b_ref = refs[:2]
        if norm is not None:
            hn_out, a_ref = refs[first_out + nout], refs[-1]

            @pl.when(pl.program_id(1) == 0)
            def _():
                hn = _rms(refs[0][...], refs[first_out - 1][...]).astype(BF16)
                a_ref[...] = hn
                hn_out[...] = hn

        total = None
        for ia, ib in chunks:
            d = lax.dot_general(_bf(a_ref[ia]), _bf(b_ref[ib]), dims, preferred_element_type=F32)
            total = d if total is None else total + d
        vals = epilogue(total, *[e[...] for e in refs[2:2 + nex]]) if epilogue is not None else (total,)
        for r, v in zip(refs[first_out:first_out + nout], vals):
            r[...] = v.astype(r.dtype)

    operands = [a, b] + [x for x, _, _ in extras]
    in_specs = [pl.BlockSpec(*a_spec), pl.BlockSpec(*b_spec)] + [pl.BlockSpec(blk, m) for _, blk, m in extras]
    if after is not None:
        operands.append(after)
        in_specs.append(pl.BlockSpec(memory_space=pl.ANY))
    out_specs = [pl.BlockSpec(blk, m) for _, _, blk, m in outs]
    out_shape = [jax.ShapeDtypeStruct(s, d) for s, d, _, _ in outs]
    scratch = []
    if norm is not None:
        operands.append(norm)
        in_specs.append(pl.BlockSpec(norm.shape, lambda i, j: (0, 0)))
        out_specs.append(pl.BlockSpec(*a_spec))
        out_shape.append(jax.ShapeDtypeStruct(a.shape, BF16))
        scratch.append(pltpu.VMEM(a_spec[0], BF16))
    res = pl.pallas_call(
        body, name=name, grid=grid, in_specs=in_specs, out_specs=out_specs, out_shape=out_shape,
        scratch_shapes=scratch,
        compiler_params=_params(dimension_semantics=("parallel", "arbitrary" if norm is not None else "parallel")),
    )(*operands)
    return res[0] if len(res) == 1 else res


def _tile(n, want):
    for t in range(min(n, want), 15, -1):
        if n % t == 0 and t % 16 == 0:
            return t
    return n


def _mm_nn(name, a, b, out_dtype, res=None, tm=1024, tn=1024, norm=None):
    M, K = a.shape
    N = b.shape[1]
    tm, tn = _tile(M, tm), _tile(N, tn)
    extras = [] if res is None else [(res, (tm, tn), lambda i, j: (i, j))]
    epi = None if res is None else (lambda t, r: (r + t,))
    return _gemm(name, a, ((tm, K), lambda i, j: (i, 0)), b, ((K, tn), lambda i, j: (0, j)), NN,
                 (M // tm, N // tn), [((M, N), out_dtype, (tm, tn), lambda i, j: (i, j))], extras=extras, epilogue=epi,
                 norm=norm)


def _mm_nn_cols(name, a, bs, out_dtype, res=None, tm=1024, norm=None):
    M, K = a.shape
    nb, _, w = bs.shape
    tm = _tile(M, tm)
    extras = [] if res is None else [(res, (tm, w), lambda i, j: (i, j))]
    epi = None if res is None else (lambda t, r: (r + t,))
    return _gemm(name, a, ((tm, K), lambda i, j: (i, 0)), bs, ((None, K, w), lambda i, j: (j, 0, 0)), NN,
                 (M // tm, nb), [((M, nb * w), out_dtype, (tm, w), lambda i, j: (i, j))], extras=extras, epilogue=epi,
                 norm=norm)


def _mm_nt(name, a, b, out_dtype, tm=1024, tn=1024, after=None):
    M, K = a.shape
    N = b.shape[0]
    tm, tn = _tile(M, tm), _tile(N, tn)
    return _gemm(name, a, ((tm, K), lambda i, j: (i, 0)), b, ((tn, K), lambda i, j: (j, 0)), NT,
                 (M // tm, N // tn), [((M, N), out_dtype, (tm, tn), lambda i, j: (i, j))], after=after)


def _mm_nt_cols(name, a, bs, out_dtype, tm=1024, tn=512, after=None):
    M = a.shape[0]
    nb, N, w = bs.shape
    tm, tn = _tile(M, tm), _tile(N, tn)
    chunks = [((slice(None), pl.ds(c * w, w)), (c,)) for c in range(nb)]
    return _gemm(name, a, ((tm, nb * w), lambda i, j: (i, 0)), bs, ((nb, tn, w), lambda i, j: (0, j, 0)), NT,
                 (M // tm, N // tn), [((M, N), out_dtype, (tm, tn), lambda i, j: (i, j))], chunks, after=after)


def _mm_tn(name, a, b, tm=1024, tn=1024, col_blocks=None):
    T, Ka = a.shape
    Nb = b.shape[1]
    tm = _tile(Ka, tm)
    if col_blocks is None:
        tn = _tile(Nb, tn)
        out = ((Ka, Nb), BF16, (tm, tn), lambda i, j: (i, j))
    else:
        tn = Nb // col_blocks
        out = ((col_blocks, Ka, tn), BF16, (None, tm, tn), lambda i, j: (j, i, 0))
    return _gemm(name, a, ((T, tm), lambda i, j: (0, i)), b, ((T, tn), lambda i, j: (0, j)), TN,
                 (Ka // tm, Nb // tn), [out])


def _hidden_block(wt):
    return 2 * wt.shape[0] // N_DEV


def _ffn_out(name, a, wt, res=None, scale=1.0, tm=1024, tn=512, after=None, out_dtype=F32):
    M, F = a.shape
    N = wt.shape[1]
    tm, tn = _tile(M, tm), _tile(N, tn)
    extras = [] if res is None else [(res, (tm, tn), lambda i, j: (i, j))]
    epi = None if res is None else (lambda t, r: (r + scale * t,))
    return _gemm(name, a, ((tm, F), lambda i, j: (i, 0)), wt, ((F, tn), lambda i, j: (0, j)), NN,
                 (M // tm, N // tn), [((M, N), out_dtype, (tm, tn), lambda i, j: (i, j))], extras=extras,
                 epilogue=epi, after=after)


def _ffn_dact(name, dhb, wd, g, u, tm=1024, after=None):
    M, K = dhb.shape
    F = wd.shape[0]
    w = _hidden_block(wd)
    tm = _tile(M, tm)
    tr = _tile(tm, 256)
    tokens = [] if after is None else [after]

    def body(dh_ref, wd_ref, g_ref, u_ref, *rest):
        dg_ref, du_ref = rest[-2:]
        pieces = [pl.ds(r * tr, tr) for r in range(tm // tr)]

        def product(rows):
            return lax.dot_general(dh_ref[rows, :], wd_ref[...], NT, preferred_element_type=F32)

        results = []
        dact = product(pieces[0])
        for n, rows in enumerate(pieces):
            ahead = product(pieces[n + 1]) if n + 1 < len(pieces) else None
            results.append(_swiglu_bwd(dact, g_ref[rows, :], u_ref[rows, :]))
            dact = ahead
        for rows, (dg, du) in zip(pieces, results):
            dg_ref[rows, :] = dg.astype(BF16)
            du_ref[rows, :] = du.astype(BF16)

    hid = pl.BlockSpec((tm, w), lambda i, j: (i, j))
    return pl.pallas_call(
        body, name=name, grid=(M // tm, F // w),
        in_specs=[pl.BlockSpec((tm, K), lambda i, j: (i, 0)), pl.BlockSpec((w, K), lambda i, j: (j, 0)), hid, hid]
        + [pl.BlockSpec(memory_space=pl.ANY) for t in tokens],
        out_specs=[hid, hid], out_shape=[jax.ShapeDtypeStruct((M, F), BF16)] * 2,
        compiler_params=_params(dimension_semantics=("parallel", "parallel")),
    )(dhb, wd, g, u, *tokens)


def _ffn_dw(name, a, b, scale=1.0, tn=512, after=None):
    T, F = a.shape
    N = b.shape[1]
    w = 2 * F // N_DEV
    tn = _tile(N, tn)
    epi = None if scale == 1.0 else (lambda t: (t * scale,))
    return _gemm(name, a, ((T, w), lambda i, j: (0, i)), b, ((T, tn), lambda i, j: (0, j)), TN,
                 (F // w, N // tn), [((F, N), BF16, (w, tn), lambda i, j: (i, j))], epilogue=epi, after=after)


def _ffn_gate(name, hn, wgt, tm=1024):
    M, K = hn.shape
    F = wgt.shape[0]
    w = _hidden_block(wgt)
    tm = _tile(M, tm)
    return _gemm(name, hn, ((tm, K), lambda i, j: (i, 0)), wgt, ((w, K), lambda i, j: (j, 0)), NT,
                 (M // tm, F // w), [((M, F), BF16, (tm, w), lambda i, j: (i, j))])


def _ffn_up_act(name, hn, wut, g, tm=1024):
    M, K = hn.shape
    F = wut.shape[0]
    w = _hidden_block(wut)
    tm = _tile(M, tm)
    hid = ((tm, w), lambda i, j: (i, j))

    def epilogue(u, gate):
        gate = gate.astype(F32)
        return u, gate * jax.nn.sigmoid(gate) * u

    return _gemm(name, hn, ((tm, K), lambda i, j: (i, 0)), wut, ((w, K), lambda i, j: (j, 0)), NT,
                 (M // tm, F // w), [((M, F), BF16) + hid] * 2, extras=[(g,) + hid], epilogue=epilogue)


def _ffn_up(name, h, gain, wgt, wut, tm=512):
    M, K = h.shape
    F = wgt.shape[0]
    w = _hidden_block(wgt)
    tm = _tile(M, tm)

    def body(h_ref, gain_ref, g_ref, u_ref, hn_ref, og, ou, oa, a_ref):
        @pl.when(pl.program_id(1) == 0)
        def _():
            hn = _rms(h_ref[...], gain_ref[...]).astype(BF16)
            a_ref[...] = hn
            hn_ref[...] = hn

        a = a_ref[...]
        g = lax.dot_general(a, g_ref[...], NT, preferred_element_type=F32)
        u = lax.dot_general(a, u_ref[...], NT, preferred_element_type=F32)
        og[...] = g.astype(BF16)
        ou[...] = u.astype(BF16)
        oa[...] = (g * jax.nn.sigmoid(g) * u).astype(BF16)

    rows = pl.BlockSpec((tm, K), lambda i, j: (i, 0))
    wspec = pl.BlockSpec((w, K), lambda i, j: (j, 0))
    ospec = pl.BlockSpec((tm, w), lambda i, j: (i, j))
    return pl.pallas_call(
        body, name=name, grid=(M // tm, F // w),
        in_specs=[rows, pl.BlockSpec((1, K), lambda i, j: (0, 0)), wspec, wspec],
        out_specs=[rows] + [ospec] * 3,
        out_shape=[jax.ShapeDtypeStruct((M, K), BF16)] + [jax.ShapeDtypeStruct((M, F), BF16)] * 3,
        scratch_shapes=[pltpu.VMEM((tm, K), BF16)],
        compiler_params=_params(dimension_semantics=("parallel", "arbitrary")),
    )(h, gain, wgt, wut)


def _swiglu_bwd(dact, g, u):
    g = g.astype(F32)
    u = u.astype(F32)
    sig = jax.nn.sigmoid(g)
    silu = g * sig
    d = FFN_RES * dact
    return d * u * (sig * (1.0 + g * (1.0 - sig))), d * silu


def _rms_fwd(name, x, gain, tr=512, after=None):
    R, D = x.shape
    tr = _tile(R, tr)

    def body(x_ref, g_ref, *rest):
        rest[-1][...] = _rms(x_ref[...], g_ref[...]).astype(BF16)

    tokens = [] if after is None else [after]
    return pl.pallas_call(
        body, name=name, grid=(R // tr,),
        in_specs=[pl.BlockSpec((tr, D), lambda i: (i, 0)), pl.BlockSpec((1, D), lambda i: (0, 0))]
        + [pl.BlockSpec(memory_space=pl.ANY) for t in tokens],
        out_specs=pl.BlockSpec((tr, D), lambda i: (i, 0)), out_shape=jax.ShapeDtypeStruct((R, D), BF16),
        compiler_params=_params(dimension_semantics=("parallel",)),
    )(x, gain, *tokens)


def _rms_bwd_math(xv, gain, dy):
    rstd = lax.rsqrt(jnp.mean(xv * xv, axis=-1, keepdims=True) + EPS)
    xhat = xv * rstd
    dxh = dy * gain
    dx = rstd * (dxh - xhat * jnp.mean(dxh * xhat, axis=-1, keepdims=True))
    return dx, jnp.sum(dy * xhat, axis=0, keepdims=True)


def _rms_bwd(name, x, gain, dy, skip=None, out_dtype=BF16, tr=512):
    R, D = x.shape
    tr = _tile(R, tr)
    has_skip = skip is not None

    def body(*refs):
        x_ref, g_ref, dy_ref = refs[:3]
        dx_ref, dg_ref = refs[-2:]
        dx, dg = _rms_bwd_math(x_ref[...], g_ref[...], dy_ref[...].astype(F32))
        if has_skip:
            dx = dx + refs[3][...].astype(F32)
        dx_ref[...] = dx.astype(out_dtype)

        @pl.when(pl.program_id(0) == 0)
        def _():
            dg_ref[...] = dg

        @pl.when(pl.program_id(0) > 0)
        def _():
            dg_ref[...] += dg

    row = pl.BlockSpec((tr, D), lambda i: (i, 0))
    vec = pl.BlockSpec((1, D), lambda i: (0, 0))
    return pl.pallas_call(
        body, name=name, grid=(R // tr,),
        in_specs=[row, vec, row] + ([row] if has_skip else []),
        out_specs=[row, vec],
        out_shape=[jax.ShapeDtypeStruct((R, D), out_dtype), jax.ShapeDtypeStruct((1, D), F32)],
        compiler_params=_params(dimension_semantics=("arbitrary",)),
    )(*([x, gain, dy] + ([skip] if has_skip else [])))


def _loss_and_grad(name, h, gain, target, tr=512):
    R, D = h.shape
    tr = _tile(R, tr)

    def body(h_ref, g_ref, t_ref, loss_ref, dhb_ref, dg_ref):
        hv, gain_v = h_ref[...], g_ref[...]
        y = (hv * lax.rsqrt(jnp.mean(hv * hv, axis=-1, keepdims=True) + EPS)) * gain_v
        err = y - t_ref[...]
        part = jnp.full((8, 128), 0.5 * jnp.sum(jnp.mean(err * err, axis=-1, keepdims=True)), F32)
        dh, dg = _rms_bwd_math(hv, gain_v, err * (1.0 / D))
        dhb_ref[...] = dh.astype(BF16)

        @pl.when(pl.program_id(0) == 0)
        def _():
            dg_ref[...] = dg
            loss_ref[...] = part

        @pl.when(pl.program_id(0) > 0)
        def _():
            dg_ref[...] += dg
            loss_ref[...] += part

    row = pl.BlockSpec((tr, D), lambda i: (i, 0))
    vec = pl.BlockSpec((1, D), lambda i: (0, 0))
    return pl.pallas_call(
        body, name=name, grid=(R // tr,), in_specs=[row, vec, row],
        out_specs=[pl.BlockSpec((8, 128), lambda i: (0, 0)), row, vec],
        out_shape=[jax.ShapeDtypeStruct((8, 128), F32), jax.ShapeDtypeStruct((R, D), BF16),
                   jax.ShapeDtypeStruct((1, D), F32)],
        compiler_params=_params(dimension_semantics=("arbitrary",)),
    )(h, gain, target)


def _bias_tile(name, rel, after=None):
    width = KBAND + QBLK
    sat = rel[:, 2 * REL_CLIP:]
    n_left = PAD - REL_CLIP + 1
    row0 = jnp.concatenate([jnp.broadcast_to(sat, (ATTN_HEADS, n_left)), rel[:, :2 * REL_CLIP][:, ::-1],
                            jnp.broadcast_to(sat, (ATTN_HEADS, width - n_left - 2 * REL_CLIP))], axis=1)

    tokens = [] if after is None else [after]

    def body(e_ref, *rest):
        o_ref = rest[-1]
        rows = pltpu.roll(jnp.broadcast_to(e_ref[...], (QBLK, width)), 0, 1, stride=1, stride_axis=0)
        i = lax.broadcasted_iota(jnp.int32, (QBLK, KBAND), 0) // CHUNK
        j = lax.broadcasted_iota(jnp.int32, (QBLK, KBAND), 1) // CHUNK
        o_ref[...] = jnp.where((j >= i) & (j <= i + LEFT_CHUNKS), rows[:, :KBAND], NEG_INF)

    return pl.pallas_call(
        body, name=name, grid=(ATTN_HEADS,),
        in_specs=[pl.BlockSpec((None, 1, width), lambda h: (h, 0, 0))] + [ANY_SPEC for t in tokens],
        out_specs=pl.BlockSpec((None, QBLK, KBAND), lambda h: (h, 0, 0)),
        out_shape=jax.ShapeDtypeStruct((ATTN_HEADS, QBLK, KBAND), F32),
        compiler_params=_params(dimension_semantics=("parallel",)),
    )(row0.reshape(ATTN_HEADS, 1, width), *tokens)


ATTN_SCALE = HEAD_DIM ** -0.5
ROW_PIECES = 1


def _stack_heads(x, first):
    zero = jnp.zeros_like(x)
    return jnp.concatenate([jnp.where(first, x, zero), jnp.where(first, zero, x)], axis=0)


def _band_softmax(q_half_scaled, kb, bias, left_mask):
    s = lax.dot_general(q_half_scaled, kb, NT, preferred_element_type=F32) + bias + left_mask
    e = jnp.exp(s - jnp.max(s, axis=-1, keepdims=True))
    return e * (1.0 / jnp.sum(e, axis=-1, keepdims=True))


def _left_mask(qb):
    kpos = qb * QBLK - PAD + lax.broadcasted_iota(jnp.int32, (1, KBAND), 1)
    return jnp.where(kpos >= 0, 0.0, NEG_INF).astype(F32)


def _fill_padded(dst, src, S):
    dst[pl.ds(0, PAD), :] = jnp.zeros((PAD, dst.shape[1]), dst.dtype)
    dst[pl.ds(PAD, S), :] = src[...].astype(dst.dtype)


def _attn_fwd(name, z, bias):
    B, S, _ = z.shape
    nh2 = ATTN_HEADS // 2

    def body(q_ref, k_ref, v_ref, b_ref, o_ref, kp, vp):
        qb = pl.program_id(2)

        @pl.when(qb == 0)
        def _():
            _fill_padded(kp, k_ref, S)
            _fill_padded(vp, v_ref, S)

        start = pl.multiple_of(qb * QBLK, QBLK)
        kb, vb = kp[pl.ds(start, KBAND), :], vp[pl.ds(start, KBAND), :]
        q = (q_ref[...] * ATTN_SCALE).astype(BF16)
        first = lax.broadcasted_iota(jnp.int32, (QBLK, 2 * HEAD_DIM), 1) < HEAD_DIM
        left = _left_mask(qb)
        zero = jnp.zeros_like(q)
        qh = [jnp.where(first, q, zero), jnp.where(first, zero, q)]
        rp = QBLK // ROW_PIECES
        chains = [(a, r) for r in range(ROW_PIECES) for a in range(2)]
        ss = [lax.dot_general(qh[a][r * rp:(r + 1) * rp], kb, NT, preferred_element_type=F32) for a, r in chains]
        ps = []
        for (a, r), s in zip(chains, ss):
            s = s + b_ref[a, pl.ds(r * rp, rp), :] + left
            e = jnp.exp(s - jnp.max(s, axis=-1, keepdims=True))
            ps.append((e * (1.0 / jnp.sum(e, axis=-1, keepdims=True))).astype(BF16))
        os_ = [jnp.dot(p, vb, preferred_element_type=F32) for p in ps]
        for r in range(ROW_PIECES):
            o_ref[pl.ds(r * rp, rp), :] = jnp.where(first[:rp], os_[2 * r], os_[2 * r + 1]).astype(BF16)

    return pl.pallas_call(
        body, name=name, grid=(B, nh2, S // QBLK),
        in_specs=[pl.BlockSpec((None, QBLK, 128), lambda b, h, i: (b, i, h)),
                  pl.BlockSpec((None, S, 128), lambda b, h, i: (b, 0, nh2 + h)),
                  pl.BlockSpec((None, S, 128), lambda b, h, i: (b, 0, 2 * nh2 + h)),
                  pl.BlockSpec((2, QBLK, KBAND), lambda b, h, i: (h, 0, 0))],
        out_specs=pl.BlockSpec((None, QBLK, 128), lambda b, h, i: (b, i, h)),
        out_shape=jax.ShapeDtypeStruct((B, S, D_ATTN + D_POOL), BF16),
        scratch_shapes=[pltpu.VMEM((PAD + S, 128), BF16), pltpu.VMEM((PAD + S, 128), BF16)],
        compiler_params=_params(dimension_semantics=("parallel", "parallel", "arbitrary")),
    )(z, z, z, bias)


def _attn_bwd(name, z, bias, dcat):
    B, S, _ = z.shape
    nh2 = ATTN_HEADS // 2
    nqb = S // QBLK

    def body(q_ref, k_ref, v_ref, b_ref, do_ref, dq_ref, dk_ref, dv_ref, db_ref, kp, vp, dka, dva):
        b, qb = pl.program_id(1), pl.program_id(2)

        @pl.when(qb == 0)
        def _():
            _fill_padded(kp, k_ref, S)
            _fill_padded(vp, v_ref, S)
            dka[...] = jnp.zeros_like(dka)
            dva[...] = jnp.zeros_like(dva)

        @pl.when((qb == 0) & (b == 0))
        def _():
            db_ref[...] = jnp.zeros_like(db_ref)

        start = pl.multiple_of(qb * QBLK, QBLK)
        band = pl.ds(start, KBAND)
        kb, vb = kp[band, :], vp[band, :]
        q = (q_ref[...] * ATTN_SCALE).astype(BF16)
        do = do_ref[...]
        first = lax.broadcasted_iota(jnp.int32, (QBLK, 2 * HEAD_DIM), 1) < HEAD_DIM
        left = _left_mask(qb)
        q2, do2 = _stack_heads(q, first), _stack_heads(do, first)
        p = _band_softmax(q2, kb, b_ref[...].reshape(2 * QBLK, KBAND), left)
        dp = lax.dot_general(do2, vb, NT, preferred_element_type=F32)
        ds = p * (dp - jnp.sum(p * dp, axis=-1, keepdims=True))
        db_ref[...] += ds.reshape(2, QBLK, KBAND)
        dsb = ds.astype(BF16)
        dq = jnp.dot(dsb, kb, preferred_element_type=F32)
        dq_ref[...] = (jnp.where(first, dq[:QBLK], dq[QBLK:]) * ATTN_SCALE).astype(BF16)
        dka[:, band] += lax.dot_general(q2, dsb, TN, preferred_element_type=F32)
        dva[:, band] += lax.dot_general(do2, p.astype(BF16), TN, preferred_element_type=F32)

        @pl.when(qb == nqb - 1)
        def _():
            dk_ref[...] = dka[:, pl.ds(PAD, S)].T.astype(BF16)
            dv_ref[...] = dva[:, pl.ds(PAD, S)].T.astype(BF16)

    qspec = pl.BlockSpec((None, QBLK, 128), lambda h, b, i: (b, i, h))
    kvout = pl.BlockSpec((None, S, 128), lambda h, b, i: (b, 0, h))
    bspec = pl.BlockSpec((2, QBLK, KBAND), lambda h, b, i: (h, 0, 0))
    act = jax.ShapeDtypeStruct((B, S, D_ATTN), BF16)
    return pl.pallas_call(
        body, name=name, grid=(nh2, B, nqb),
        in_specs=[qspec,
                  pl.BlockSpec((None, S, 128), lambda h, b, i: (b, 0, nh2 + h)),
                  pl.BlockSpec((None, S, 128), lambda h, b, i: (b, 0, 2 * nh2 + h)),
                  bspec, qspec],
        out_specs=[qspec, kvout, kvout, bspec],
        out_shape=[jax.ShapeDtypeStruct((B, S, 3 * D_ATTN + D_POOL), BF16), act, act,
                   jax.ShapeDtypeStruct((ATTN_HEADS, QBLK, KBAND), F32)],
        scratch_shapes=[pltpu.VMEM((PAD + S, 128), BF16), pltpu.VMEM((PAD + S, 128), BF16),
                        pltpu.VMEM((128, PAD + S), F32), pltpu.VMEM((128, PAD + S), F32)],
        compiler_params=_params(dimension_semantics=("arbitrary", "arbitrary", "arbitrary")),
    )(z, z, z, bias, dcat)


def _bias_grad(name, dbias, after):
    width = KBAND + QBLK

    def body(d_ref, after_ref, o_ref):
        acc = jnp.zeros((1, width), F32)
        for i in range(QBLK):
            row = jnp.concatenate([d_ref[pl.ds(i, 1), :], jnp.zeros((1, QBLK), F32)], axis=1)
            shift = QBLK - 1 - i
            acc = acc + (pltpu.roll(row, shift, 1) if shift else row)
        o_ref[...] = acc

    return pl.pallas_call(
        body, name=name, grid=(ATTN_HEADS,),
        in_specs=[pl.BlockSpec((None, QBLK, KBAND), lambda h: (h, 0, 0)), pl.BlockSpec(memory_space=pl.ANY)],
        out_specs=pl.BlockSpec((None, 1, width), lambda h: (h, 0, 0)),
        out_shape=jax.ShapeDtypeStruct((ATTN_HEADS, 1, width), F32),
        compiler_params=_params(dimension_semantics=("parallel",)),
    )(dbias, after)


def _rel_grad_from_diagonals(diag):
    top = PAD + QBLK - 1 - REL_CLIP
    sat = jnp.sum(diag[:, :top + 1], axis=1, keepdims=True)
    mid = diag[:, top + 1:top + 2 * REL_CLIP][:, ::-1]
    return jnp.concatenate([jnp.zeros_like(sat), mid, sat], axis=1)


def _shift_rows(x, k, forward):
    S = x.shape[0]
    t = lax.broadcasted_iota(jnp.int32, x.shape, 0)
    if forward:
        return jnp.where(t < S - k, pltpu.roll(x, S - k, 0), 0.0)
    return jnp.where(t >= k, pltpu.roll(x, k, 0), 0.0)


def _window_sum(x, g, forward):
    s = x + _shift_rows(x, 1, forward)
    out = s
    for n, k in enumerate((2, 4, 8)):
        s = s + _shift_rows(s, k, forward)
        out = jnp.where(g > n, s, out)
    return out


def _pool_count(S, g):
    t = lax.broadcasted_iota(jnp.int32, (S, 1), 0)
    w = jnp.left_shift(2, g)
    return jnp.minimum(t + 1, w).astype(F32)


def _pool_fwd(name, z, wp, pscale, mixed):
    B, S, _ = z.shape
    c0 = 3 * D_ATTN // POOL_GROUP
    y0 = D_ATTN // POOL_GROUP

    def body(u_ref, w_ref, s_ref, mixed_ref, d_ref, y_ref):
        g = pl.program_id(1)
        u = u_ref[...]
        d = (_window_sum(u, g, False) / _pool_count(S, g) - u).astype(BF16)
        d_ref[...] = d
        y_ref[...] = (jnp.dot(d, w_ref[...], preferred_element_type=F32) * s_ref[...]).astype(BF16)

    return pl.pallas_call(
        body, name=name, grid=(B, len(POOL_WINDOWS)),
        in_specs=[pl.BlockSpec((None, S, POOL_GROUP), lambda b, g: (b, 0, c0 + g)),
                  pl.BlockSpec((None, POOL_GROUP, POOL_GROUP), lambda b, g: (g, 0, 0)),
                  pl.BlockSpec((1, POOL_GROUP), lambda b, g: (0, g)),
                  pl.BlockSpec(memory_space=pl.ANY)],
        out_specs=[pl.BlockSpec((None, S, POOL_GROUP), lambda b, g: (b, 0, g)),
                   pl.BlockSpec((None, S, POOL_GROUP), lambda b, g: (b, 0, y0 + g))],
        out_shape=[jax.ShapeDtypeStruct((B, S, D_POOL), BF16), jax.ShapeDtypeStruct(mixed.shape, BF16)],
        input_output_aliases={3: 1},
        compiler_params=_params(dimension_semantics=("parallel", "parallel")),
    )(z, wp, pscale, mixed)


def _pool_bwd(name, d, wp, pscale, dcat, dz):
    B, S, _ = d.shape
    c0 = D_ATTN // POOL_GROUP
    u0 = 3 * D_ATTN // POOL_GROUP

    def body(d_ref, w_ref, s_ref, dy_ref, dz_ref, du_ref, dw_ref, dsc_ref):
        g, b = pl.program_id(0), pl.program_id(1)
        dv = d_ref[...]
        dy = dy_ref[...].astype(F32)
        w = w_ref[...]
        ypre = jnp.dot(dv, w, preferred_element_type=F32)
        dyp = (dy * s_ref[...]).astype(BF16)
        dd = lax.dot_general(dyp, w, NT, preferred_element_type=F32)
        du_ref[...] = (_window_sum(dd / _pool_count(S, g), g, True) - dd).astype(BF16)
        dw = lax.dot_general(dv, dyp, TN, preferred_element_type=F32)
        dsc = jnp.sum(dy * ypre, axis=0, keepdims=True)

        @pl.when(b == 0)
        def _():
            dw_ref[...] = dw
            dsc_ref[...] = dsc

        @pl.when(b > 0)
        def _():
            dw_ref[...] += dw
            dsc_ref[...] += dsc

    blk = pl.BlockSpec((None, S, POOL_GROUP), lambda g, b: (b, 0, g))
    wspec = pl.BlockSpec((None, POOL_GROUP, POOL_GROUP), lambda g, b: (g, 0, 0))
    sspec = pl.BlockSpec((1, POOL_GROUP), lambda g, b: (0, g))
    return pl.pallas_call(
        body, name=name, grid=(len(POOL_WINDOWS), B),
        in_specs=[blk, wspec, sspec, pl.BlockSpec((None, S, POOL_GROUP), lambda g, b: (b, 0, c0 + g)), ANY_SPEC],
        out_specs=[pl.BlockSpec((None, S, POOL_GROUP), lambda g, b: (b, 0, u0 + g)), wspec, sspec],
        out_shape=[jax.ShapeDtypeStruct(dz.shape, BF16),
                   jax.ShapeDtypeStruct((len(POOL_WINDOWS), POOL_GROUP, POOL_GROUP), F32),
                   jax.ShapeDtypeStruct((1, D_POOL), F32)],
        input_output_aliases={4: 0},
        compiler_params=_params(dimension_semantics=("arbitrary", "arbitrary")),
    )(d, wp, pscale, dcat, dz)


def _cross_softmax(q, k):
    s = lax.dot_general(q, k, NT, preferred_element_type=F32) * (CROSS_DIM ** -0.5)
    e = jnp.exp(s - jnp.max(s, axis=-1, keepdims=True))
    return e * (1.0 / jnp.sum(e, axis=-1, keepdims=True))


def _cross_fwd(name, qc, kv, tq=1024):
    B, S, _ = qc.shape
    M = kv.shape[1]
    tq = _tile(S, tq)

    def body(q_ref, k_ref, v_ref, o_ref):
        p = _cross_softmax(q_ref[...], k_ref[...])
        o_ref[...] = jnp.dot(p.astype(BF16), v_ref[...], preferred_element_type=F32).astype(BF16)

    qspec = pl.BlockSpec((None, tq, CROSS_DIM), lambda b, h, i: (b, i, h))
    return pl.pallas_call(
        body, name=name, grid=(B, CROSS_HEADS, S // tq),
        in_specs=[qspec, pl.BlockSpec((None, M, CROSS_DIM), lambda b, h, i: (b, 0, h)),
                  pl.BlockSpec((None, M, CROSS_DIM), lambda b, h, i: (b, 0, CROSS_HEADS + h))],
        out_specs=qspec, out_shape=jax.ShapeDtypeStruct((B, S, D_CROSS), BF16),
        compiler_params=_params(dimension_semantics=("parallel", "parallel", "parallel")),
    )(qc, kv, kv)


def _cross_bwd(name, qc, kv, do, tq=1024):
    B, S, _ = qc.shape
    M = kv.shape[1]
    tq = _tile(S, tq)
    nq = S // tq
    scale = CROSS_DIM ** -0.5

    def body(q_ref, k_ref, v_ref, do_ref, dq_ref, dk_ref, dv_ref, dka, dva):
        i = pl.program_id(2)
        q, k, v, dov = q_ref[...], k_ref[...], v_ref[...], do_ref[...]
        p = _cross_softmax(q, k)
        dp = lax.dot_general(dov, v, NT, preferred_element_type=F32)
        ds = ((p * (dp - jnp.sum(p * dp, axis=-1, keepdims=True))) * scale).astype(BF16)
        dq_ref[...] = jnp.dot(ds, k, preferred_element_type=F32).astype(BF16)
        dk = lax.dot_general(ds, q, TN, preferred_element_type=F32)
        dv = lax.dot_general(p.astype(BF16), dov, TN, preferred_element_type=F32)

        @pl.when(i == 0)
        def _():
            dka[...] = dk
            dva[...] = dv

        @pl.when(i > 0)
        def _():
            dka[...] += dk
            dva[...] += dv

        @pl.when(i == nq - 1)
        def _():
            dk_ref[...] = dka[...].astype(BF16)
            dv_ref[...] = dva[...].astype(BF16)

    qspec = pl.BlockSpec((None, tq, CROSS_DIM), lambda b, h, i: (b, i, h))
    kspec = pl.BlockSpec((None, M, CROSS_DIM), lambda b, h, i: (b, 0, h))
    return pl.pallas_call(
        body, name=name, grid=(B, CROSS_HEADS, nq),
        in_specs=[qspec, kspec, pl.BlockSpec((None, M, CROSS_DIM), lambda b, h, i: (b, 0, CROSS_HEADS + h)), qspec],
        out_specs=[qspec, kspec, kspec],
        out_shape=[jax.ShapeDtypeStruct((B, S, D_CROSS), BF16), jax.ShapeDtypeStruct((B, M, D_CROSS), BF16),
                   jax.ShapeDtypeStruct((B, M, D_CROSS), BF16)],
        scratch_shapes=[pltpu.VMEM((M, CROSS_DIM), F32), pltpu.VMEM((M, CROSS_DIM), F32)],
        compiler_params=_params(dimension_semantics=("parallel", "parallel", "arbitrary")),
    )(qc, kv, kv, do)


def _local_step(x, mem, target, small, weights, emit, start_token=None):
    B, S, D = x.shape
    T = B * S
    x2, t2 = x.reshape(T, D), target.reshape(T, D)
    mem2 = mem.reshape(-1, D)
    n_mem = mem.shape[1]
    wts = {}

    hn1 = _rms_fwd("norm_ffn1", x2, small["ffn1_norm"], after=start_token)
    early = {}

    def cover_bias(tok):
        early["bias"] = _bias_tile("bias_tile", small["rel_bias"], after=tok)
        return early["bias"]

    def cover_mem(tok):
        early["memn"] = _rms_fwd("norm_mem", mem2, small["mem_norm"], after=tok)
        return early["memn"]

    def cover_mix(tok):
        early["hn2"] = _rms_fwd("norm_mix", h1, small["mix_norm"], after=tok)
        return early["hn2"]

    wts.update(weights(0, hn1, cover_bias))
    bias = early["bias"]
    g1 = _ffn_gate("ffn1_gate", hn1, wts["ffn1_w_gate"])
    wts.update(weights(1, g1, cover_mem))
    memn = early["memn"]
    u1, a1 = _ffn_up_act("ffn1_up", hn1, wts["ffn1_w_up"], g1)
    wts.update(weights(2, a1))
    h1 = _ffn_out("ffn1_down", a1, wts["ffn1_w_down"], res=x2, scale=FFN_RES)
    wts.update(weights(3, h1, cover_mix))
    hn2 = early["hn2"]
    z = _mm_nn_cols("mix_in", hn2, wts["w_in"], F32)
    z = z.reshape(B, S, -1)
    mixed = _attn_fwd("attn_fwd", z, bias)
    d_pool, mixed = _pool_fwd("pool_fwd", z, wts["w_pool"], small["pool_scale"], mixed)
    cat = mixed.reshape(T, -1)
    h2 = _mm_nn("mix_out", cat, wts["w_out"], F32, res=h1)
    wts.update(weights(4, h2))
    qc, hn3 = _mm_nn("cross_q", h2, wts["w_cq"], BF16, norm=small["cross_norm"])
    kv = _mm_nn("cross_kv", memn, wts["w_ckv"], BF16)
    o = _cross_fwd("cross_fwd", qc.reshape(B, S, -1), kv.reshape(B, n_mem, -1)).reshape(T, -1)
    h3 = _mm_nn("cross_out", o, wts["w_co"], F32, res=h2)
    wts.update(weights(5, h3))
    hn4, g2, u2, a2 = _ffn_up("ffn2_up", h3, small["ffn2_norm"], wts["ffn2_w_gate"], wts["ffn2_w_up"])
    h4 = _ffn_out("ffn2_down", a2, wts["ffn2_w_down"], res=h3, scale=FFN_RES)

    gs = {}
    loss_part, dh4, gs["final_norm"] = _loss_and_grad("loss", h4, small["final_norm"], t2)

    def ffn_bwd(tag, dhb, h_in, hn, g, u, a, wg, wu, wd, gain, out_dtype=BF16):
        tok = emit({tag + "_w_down": _ffn_dw(tag + "_dwd", a, dhb, scale=FFN_RES)})
        dg, du = _ffn_dact(tag + "_dact", dhb, wd, g, u, after=tok)
        tok = emit({tag + "_w_gate": _ffn_dw(tag + "_dwg", dg, hn)})
        tok = emit({tag + "_w_up": _ffn_dw(tag + "_dwu", du, hn, after=tok)})
        dhn = _ffn_out(tag + "_dhn_g", dg, wg, after=tok)
        dhn = _ffn_out(tag + "_dhn_u", du, wu, res=dhn, out_dtype=BF16)
        return _rms_bwd(tag + "_dnorm", h_in, gain, dhn, skip=dhb, out_dtype=out_dtype)

    dh3b, gs["ffn2_norm"] = ffn_bwd("ffn2", dh4, h3, hn4, g2, u2, a2, wts["ffn2_w_gate"],
                                    wts["ffn2_w_up"], wts["ffn2_w_down"], small["ffn2_norm"])
    do = _mm_nt("cross_do", dh3b, wts["w_co"], BF16)
    gw = {"w_co": _mm_tn("cross_dwo", o, dh3b, tm=D_CROSS, col_blocks=N_DEV)}
    dqc, dk, dv = _cross_bwd("cross_bwd", qc.reshape(B, S, -1), kv.reshape(B, n_mem, -1), do.reshape(B, S, -1))
    dqc = dqc.reshape(T, -1)
    dkv = jnp.concatenate([dk, dv], axis=-1).reshape(B * n_mem, -1)
    gw["w_cq"] = _mm_tn("cross_dwq", hn3, dqc, tn=D_CROSS)
    gw["w_ckv"] = _mm_tn("cross_dwkv", memn, dkv)
    tok = emit(gw)
    dhn3 = _mm_nt("cross_dhn", dqc, wts["w_cq"], BF16, after=tok)
    dh2b, gs["cross_norm"] = _rms_bwd("cross_dnorm", h2, small["cross_norm"], dhn3, skip=dh3b)
    dcat = _mm_nt("mix_dcat", dh2b, wts["w_out"], BF16)
    gw = {"w_out": _mm_tn("mix_dwout", cat, dh2b)}
    dcat3 = dcat.reshape(B, S, -1)
    dz, dkk, dvv, dbias = _attn_bwd("attn_bwd", z, bias, dcat3)
    dz, gw["w_pool"], gs["pool_scale"] = _pool_bwd("pool_bwd", d_pool, wts["w_pool"], small["pool_scale"], dcat3, dz)
    dz = lax.dynamic_update_slice(dz, dkk, (0, 0, D_ATTN))
    dz = lax.dynamic_update_slice(dz, dvv, (0, 0, 2 * D_ATTN)).reshape(T, -1)
    gw["w_in"] = _mm_tn("mix_dwin", hn2, dz, col_blocks=N_DEV)
    tok = emit(gw)
    dhn2 = _mm_nt_cols("mix_dhn", dz, wts["w_in"], BF16, after=tok)
    dh1b, gs["mix_norm"] = _rms_bwd("mix_dnorm", h1, small["mix_norm"], dhn2, skip=dh2b)
    dx, gs["ffn1_norm"] = ffn_bwd("ffn1", dh1b, x2, hn1, g1, u1, a1, wts["ffn1_w_gate"],
                                  wts["ffn1_w_up"], wts["ffn1_w_down"], small["ffn1_norm"], out_dtype=F32)
    gs["rel_bias"] = _rel_grad_from_diagonals(_bias_grad("bias_grad", dbias, after=dx)[:, 0, :])
    dmemn = _mm_nt("cross_dmem", dkv, wts["w_ckv"], F32, tm=512, after=dx)
    _, gs["mem_norm"] = _rms_bwd("mem_dnorm", mem2, small["mem_norm"], dmemn)
    return loss_part, dx.reshape(B, S, D), gs


def _position():
    return lax.axis_index("x"), lax.axis_index("y"), lax.axis_index("c")


def _index(p):
    return 4 * p[0] + 2 * p[1] + p[2]


HBM_SPEC = pl.BlockSpec(memory_space=pltpu.HBM)
SEM_SPEC = pl.BlockSpec(memory_space=pltpu.SEMAPHORE)
ANY_SPEC = pl.BlockSpec(memory_space=pl.ANY)
ORDERED_EFFECT = pltpu.SideEffectType.DATAFLOW_SIDE_EFFECTING


N_COPIES = {"grads": N_DEV - 1, "spread": 4, "relay": 3}
N_SEMS = {"grads": N_DEV, "spread": 5, "relay": 3}


def _copies(pattern, srcs, lands, send, recv, base=0):
    x, y, c = _position()
    me, sibling = _index((x, y, c)), (x, y, 1 - c)
    chips = [(1 - x, y), (x, 1 - y), (1 - x, 1 - y)]
    if pattern == "grads":
        targets = [(x ^ (k >> 2), y ^ ((k >> 1) & 1), c ^ (k & 1)) for k in range(1, N_DEV)]
    else:
        targets = [sibling] + [(*chip, c) for chip in chips]
    per, slots, out = N_COPIES[pattern], N_SEMS[pattern], []
    for a in range(len(lands)):
        for k in range(per):
            if pattern == "relay":
                src = dst = lands[a].at[_index((*chips[k], c))]
                to = sibling
            else:
                to = targets[k]
                src = srcs[a].at[_index(to)] if pattern == "grads" else srcs[a]
                dst = lands[a].at[me]
            slot = base + a * slots + k
            out.append(pltpu.make_async_remote_copy(src_ref=src, dst_ref=dst, send_sem=send.at[slot],
                                                    recv_sem=recv.at[slot], device_id=to, device_id_type=MESH))
    return out


def _own_copies(pattern, srcs, lands, send, base=0):
    if pattern == "relay":
        return []
    x, y, c = _position()
    me = _index((x, y, c))
    slots = N_SEMS[pattern]
    return [pltpu.make_async_copy(srcs[a].at[me] if pattern == "grads" else srcs[a], lands[a].at[me],
                                  send.at[base + a * slots + slots - 1]) for a in range(len(lands))]


def _comm(name, srcs, lands, wait=None, start=None, after=None):
    after = [] if after is None else list(after) if isinstance(after, (list, tuple)) else [after]
    ns, nl = len(srcs), len(lands)
    na = ns + nl
    arrays = list(srcs) + list(lands)
    n_wait = 2 if wait else 0
    n_start = 2 if start else 0

    def body(*refs):
        ins, lnd = refs[:ns], refs[ns:na]
        if wait:
            base = wait[3] if len(wait) > 3 else 0
            for cp in _copies(wait[0], ins, lnd, refs[na], refs[na + 1], base):
                cp.wait_send()
                cp.wait_recv()
            for cp in _own_copies(wait[0], ins, lnd, refs[na], base):
                cp.wait()
        if start:
            outs = refs[na + n_wait + len(after):]
            for cp in _copies(start, ins, lnd, outs[0], outs[1]) + _own_copies(start, ins, lnd, outs[0]):
                cp.start()
            refs[-1][...] = jnp.zeros((8, 128), F32)

    out_shape, out_specs = [], []
    if start:
        sems = pltpu.SemaphoreType.DMA((nl * N_SEMS[start],))
        out_shape += [sems, sems]
        out_specs += [SEM_SPEC, SEM_SPEC]
    out_shape += [pltpu.HBM(a.shape, a.dtype) for a in arrays]
    out_specs += [HBM_SPEC] * na
    if start:
        out_shape.append(jax.ShapeDtypeStruct((8, 128), F32))
        out_specs.append(pl.BlockSpec(memory_space=pltpu.VMEM))
    operands = [pltpu.with_memory_space_constraint(a, pltpu.HBM) for a in arrays]
    operands += list(wait[1:3]) if wait else []
    operands += after
    res = pl.pallas_call(
        body, name=name, out_shape=out_shape, out_specs=out_specs,
        in_specs=[HBM_SPEC] * na + [SEM_SPEC] * n_wait + [ANY_SPEC] * len(after),
        input_output_aliases={i: n_start + i for i in range(na)},
        compiler_params=pltpu.CompilerParams(has_side_effects=ORDERED_EFFECT),
    )(*operands)
    res = list(res)
    thru = res[n_start:n_start + na]
    return thru[:ns], thru[ns:], (tuple(res[:2]) if start else None), (res[-1] if start else None)


def _adamw_math(w, g, m, v):
    m = ADAM_B1 * m + (1.0 - ADAM_B1) * g
    v = ADAM_B2 * v + (1.0 - ADAM_B2) * (g * g)
    m_hat = m / (1.0 - ADAM_B1 ** ADAM_STEP)
    v_hat = v / (1.0 - ADAM_B2 ** ADAM_STEP)
    delta = -ADAM_LR * (m_hat / (jnp.sqrt(v_hat) + ADAM_EPS) + ADAM_WD * w)
    return delta, m, v


def _adamw(name, parts, w, m, v):
    R, C = w.shape
    tr = _tile(R, max(16, ADAM_STEP_ELEMS // C))

    def body(p_ref, w_ref, m_ref, v_ref, g_out, d_out, m_out, v_out):
        g = p_ref[0].astype(F32)
        for d in range(1, N_DEV):
            g = g + p_ref[d].astype(F32)
        g_out[...] = g
        d_out[...], m_out[...], v_out[...] = _adamw_math(w_ref[...], g, m_ref[...], v_ref[...])

    row = pl.BlockSpec((tr, C), lambda i: (i, 0))
    out = jax.ShapeDtypeStruct((R, C), F32)
    return pl.pallas_call(
        body, name=name, grid=(R // tr,),
        in_specs=[pl.BlockSpec((N_DEV, tr, C), lambda i: (0, i, 0)), row, row, row],
        out_specs=[row] * 4, out_shape=[out] * 4,
        compiler_params=_params(dimension_semantics=("parallel",)),
    )(parts, w, m, v)


def _small_allreduce_adamw(name, g, w, m, v, after):
    R = g.shape[0]

    def body(g_ref, w_ref, m_ref, v_ref, after_ref, g_out, d_out, m_out, v_out, land, send, recv):
        x, y, c = _position()
        me = _index((x, y, c))
        land[me] = g_ref[...]
        copies = []
        for k in range(1, N_DEV):
            peer = (x ^ (k >> 2), y ^ ((k >> 1) & 1), c ^ (k & 1))
            copies.append(pltpu.make_async_remote_copy(
                src_ref=g_ref, dst_ref=land.at[me], send_sem=send.at[k - 1], recv_sem=recv.at[k - 1],
                device_id=peer, device_id_type=MESH))
        for cp in copies:
            cp.start()
        for cp in copies:
            cp.wait()
        total = land[0]
        for d in range(1, N_DEV):
            total = total + land[d]
        g_out[...] = total
        d_out[...], m_out[...], v_out[...] = _adamw_math(w_ref[...], total, m_ref[...], v_ref[...])

    vm = pl.BlockSpec(memory_space=pltpu.VMEM)
    out = jax.ShapeDtypeStruct((R, 128), F32)
    return pl.pallas_call(
        body, name=name, in_specs=[vm] * 4 + [ANY_SPEC], out_specs=[vm] * 4, out_shape=[out] * 4,
        scratch_shapes=[pltpu.VMEM((N_DEV, R, 128), F32), pltpu.SemaphoreType.DMA((7,)),
                        pltpu.SemaphoreType.DMA((7,))],
    )(g, w, m, v, after)


BIG = ("ffn1_w_gate", "ffn1_w_up", "ffn1_w_down", "w_in", "w_pool", "w_out", "w_cq", "w_ckv", "w_co",
       "ffn2_w_gate", "ffn2_w_up", "ffn2_w_down")
SMALL = ("ffn1_norm", "mix_norm", "rel_bias", "pool_scale", "cross_norm", "mem_norm", "ffn2_norm", "final_norm")
ORDER = ("ffn1_norm", "ffn1_w_gate", "ffn1_w_up", "ffn1_w_down", "mix_norm", "w_in", "rel_bias", "w_pool",
         "pool_scale", "w_out", "cross_norm", "mem_norm", "w_cq", "w_ckv", "w_co", "ffn2_norm", "ffn2_w_gate",
         "ffn2_w_up", "ffn2_w_down", "final_norm")
TRANSPOSED = ("ffn1_w_gate", "ffn1_w_up", "ffn2_w_gate", "ffn2_w_up")
ROW_SHARDED = TRANSPOSED + ("ffn1_w_down", "ffn2_w_down", "w_out", "w_cq", "w_ckv")
GATHER_GROUPS = (("ffn1_w_gate",), ("ffn1_w_up",), ("ffn1_w_down",), ("w_in", "w_pool", "w_out"),
                 ("w_cq", "w_ckv", "w_co"), ("ffn2_w_gate", "ffn2_w_up", "ffn2_w_down"))
RELAY_BEFORE_USE = ((0,), (1,), (2,), (3,), (4, 5), ())


def _pack(arrays):
    flat = jnp.concatenate([a.reshape(-1) for a in arrays])
    rows = -(-flat.shape[0] // 1024) * 8
    return jnp.pad(flat, (0, rows * 128 - flat.shape[0])).reshape(rows, 128)


def _unpack(packed, like):
    flat, out, at = packed.reshape(-1), [], 0
    for a in like:
        out.append(flat[at:at + a.size].reshape(a.shape))
        at += a.size
    return out


def _shard2d(a):
    a = a[0]
    return a.reshape(-1, a.shape[-1])


def kernel(x, mem, ffn1_norm, ffn1_w_gate, ffn1_w_up, ffn1_w_down, mix_norm, w_in, rel_bias, w_pool, pool_scale, w_out, cross_norm, mem_norm, w_cq, w_ckv, w_co, ffn2_norm, ffn2_w_gate, ffn2_w_up, ffn2_w_down, final_norm, loss_target, m_ffn1_norm, m_ffn1_w_gate, m_ffn1_w_up, m_ffn1_w_down, m_mix_norm, m_w_in, m_rel_bias, m_w_pool, m_pool_scale, m_w_out, m_cross_norm, m_mem_norm, m_w_cq, m_w_ckv, m_w_co, m_ffn2_norm, m_ffn2_w_gate, m_ffn2_w_up, m_ffn2_w_down, m_final_norm, v_ffn1_norm, v_ffn1_w_gate, v_ffn1_w_up, v_ffn1_w_down, v_mix_norm, v_w_in, v_rel_bias, v_w_pool, v_pool_scale, v_w_out, v_cross_norm, v_mem_norm, v_w_cq, v_w_ckv, v_w_co, v_ffn2_norm, v_ffn2_w_gate, v_ffn2_w_up, v_ffn2_w_down, v_final_norm):
    args = dict(locals())
    def view(n, a):
        return a.transpose(0, 2, 1) if n in TRANSPOSED else a

    w_in_ = {n: view(n, args[n]) for n in ORDER}
    m_in = {n: view(n, args["m_" + n]) for n in ORDER}
    v_in = {n: view(n, args["v_" + n]) for n in ORDER}

    n_g, rows = len(POOL_WINDOWS), POOL_GROUP // N_DEV

    def landing(block_shape, dtype):
        return lax.empty((N_DEV,) + tuple(block_shape), dtype)

    gathers, tok = [], None
    for first in (True, False):
        groups = GATHER_GROUPS[:1] if first else GATHER_GROUPS[1:]
        shards = [_shard2d(w_in_[n]) for group in groups for n in group]
        shards = [(s if tok is None else s + tok[0, 0]).astype(BF16) for s in shards]
        srcs, lands, sems, tok = _comm("gather_start_%d" % (not first), shards,
                                       [landing(s.shape, BF16) for s in shards], start="spread", after=tok)
        at = 0
        for group in groups:
            gathers.append((srcs[at:at + len(group)], lands[at:at + len(group)], sems + (at * N_SEMS["spread"],)))
            at += len(group)

    def weights(gi, after, cover=None):
        for ri in RELAY_BEFORE_USE[gi]:
            srcs, lands, sems = gathers[ri]
            _, lands, sems, after = _comm("gather_relay_%d" % ri, srcs, lands, wait=("spread",) + sems,
                                          start="relay", after=after)
            gathers[ri] = (None, lands, sems)
        if cover is not None:
            after = cover(after)
        _, lands, sems = gathers[gi]
        _, lands, _, _ = _comm("gather_finish_%d" % gi, [], lands, wait=("relay",) + sems, after=after)
        out = {}
        for n, full in zip(GATHER_GROUPS[gi], lands):
            if n == "w_pool":
                full = full.reshape(N_DEV, n_g, rows, POOL_GROUP).transpose(1, 0, 2, 3).reshape(n_g, POOL_GROUP, POOL_GROUP)
            if n == "w_co":
                full = full.transpose(1, 0, 2).reshape(full.shape[1], -1)
            out[n] = full.reshape(-1, full.shape[-1]) if n in ROW_SHARDED else full
        return out

    scatters = []

    def emit(gw):
        names = list(gw)
        stacks = []
        for n in names:
            g = gw[n]
            if n == "w_pool":
                g = g.reshape(n_g, N_DEV, rows, POOL_GROUP).transpose(1, 0, 2, 3).astype(BF16)
            stacks.append(g.reshape((N_DEV,) + _shard2d(w_in_[n]).shape))
        lands = [landing(s.shape[1:], s.dtype) for s in stacks]
        srcs, lands, sems, token = _comm("grads_start_%d" % len(scatters), stacks, lands, start="grads")
        scatters.append((names, srcs, lands, sems))
        return token

    small = {n: w_in_[n].reshape(1, -1) for n in SMALL if n != "rel_bias"}
    small["rel_bias"] = rel_bias[0]
    loss_part, grad_x, gs = _local_step(x, mem, loss_target, small, weights, emit, start_token=tok)

    grad, delta, new_m, new_v = {}, {}, {}, {}
    after = grad_x
    for si, (names, srcs, lands, sems) in enumerate(scatters):
        _, landed, _, _ = _comm("grads_finish_%d" % si, srcs, lands, wait=("grads",) + sems, after=after)
        for n, parts in zip(names, landed):
            res = _adamw("adamw_" + n, parts, _shard2d(w_in_[n]), _shard2d(m_in[n]), _shard2d(v_in[n]))
            grad[n], delta[n], new_m[n], new_v[n] = [view(n, r.reshape(w_in_[n].shape)) for r in res]
        after = res[0]

    slot = jnp.zeros((1,), F32)
    like = [w_in_[n] for n in SMALL] + [slot]
    gs["rel_bias"] = gs["rel_bias"].reshape(rel_bias.shape)
    res = _small_allreduce_adamw("small_params", _pack([gs[n] for n in SMALL] + [loss_part[0, :1]]), _pack(like),
                                 _pack([m_in[n] for n in SMALL] + [slot]), _pack([v_in[n] for n in SMALL] + [slot]),
                                 after=after)
    for d, packed in zip((grad, delta, new_m, new_v), res):
        for n, a in zip(SMALL, _unpack(packed, like)):
            d[n] = a
    loss = _unpack(res[0], like)[-1][0]
    return (loss, grad_x, *[grad[n] for n in ORDER], *[delta[n] for n in ORDER],
            *[new_m[n] for n in ORDER], *[new_v[n] for n in ORDER])
```
